```python
import jax, jax.numpy as jnp
from jax import lax
import numpy as np

D_MODEL = 1024
BATCH = 8
SEQ = 8192
DEPTH = 1

D_CONV = D_MODEL
CONV_GROUPS = 8
CONV_A_WIDTH = 3
D_RNN = D_MODEL
RNN_HEADS = 4
RNN_BLOCK = D_RNN // RNN_HEADS
CONV_B_WIDTH = 4
LRU_C = 8.0
D_FF = ((8 * D_MODEL + 3 * 256 - 1) // (3 * 256)) * 256
N_MOD = 6
EPS = 1e-6
IN_WIDTHS = (D_CONV, D_CONV, D_CONV, D_RNN, D_RNN, D_MODEL, D_MODEL)
IN_TOTAL = sum(IN_WIDTHS)
IN_SPLITS = tuple(int(v) for v in np.cumsum(IN_WIDTHS)[:-1])

kernel_name = "hybrid_conv_rglru_gated_merge_adaln"


def rmsnorm(x, g):
    xf = x.astype(jnp.float32)
    y = xf * lax.rsqrt(jnp.mean(xf * xf, axis=-1, keepdims=True) + EPS) * g.astype(jnp.float32)
    return y.astype(x.dtype)


def modulate(h, shift, scale):
    return h * (1.0 + scale[:, None, :]) + shift[:, None, :]


def causal_depthwise_conv(u, w):
    k, ch = w.shape
    return lax.conv_general_dilated(
        u, w[:, None, :].astype(u.dtype), window_strides=(1,), padding=[(k - 1, 0)],
        dimension_numbers=("NWC", "WIO", "NWC"), feature_group_count=ch)


def block_diag_linear(u, w, b):
    bs, s, d = u.shape
    uh = u.reshape(bs, s, RNN_HEADS, RNN_BLOCK)
    return jnp.einsum("bshi,hij->bshj", uh, w).reshape(bs, s, d) + b


def rg_lru(u, w_a, b_a, w_x, b_x, lam):
    r = jax.nn.sigmoid(block_diag_linear(u, w_a, b_a)).astype(jnp.float32)
    i = jax.nn.sigmoid(block_diag_linear(u, w_x, b_x))
    log_a = LRU_C * r * jax.nn.log_sigmoid(lam.astype(jnp.float32))
    a = jnp.exp(log_a)
    mult = jnp.sqrt(jnp.maximum(-jnp.expm1(2.0 * log_a), 0.0))
    mult = mult.at[:, 0].set(1.0)
    bx = mult * (i * u).astype(jnp.float32)

    def combine(left, right):
        a1, b1 = left
        a2, b2 = right
        return a1 * a2, a2 * b1 + b2

    _, h = lax.associative_scan(combine, (a, bx), axis=1)
    return h.astype(u.dtype)


def _fwd_setup_inputs(seed: int = 0) -> dict:
    key = jax.random.key(seed)
    ks = jax.random.split(key, 20)
    f32 = jnp.float32
    nrm = lambda k, shape, s: jax.random.normal(k, shape, f32) * s
    a8 = jax.random.uniform(ks[11], (DEPTH, D_RNN), f32, 0.9, 0.999)
    base = a8 ** (1.0 / LRU_C)
    lru_lambda = jnp.log(base) - jnp.log1p(-base)
    return {
        "x": nrm(ks[0], (BATCH, SEQ, D_MODEL), 1.0),
        "c": nrm(ks[1], (BATCH, D_MODEL), 1.0),
        "w_ada": nrm(ks[2], (DEPTH, D_MODEL, N_MOD * D_MODEL), 0.5 * D_MODEL ** -0.5),
        "b_ada": nrm(ks[3], (DEPTH, N_MOD * D_MODEL), 0.01),
        "g_norm_mix": 1.0 + nrm(ks[4], (DEPTH, D_MODEL), 0.05),
        "w_in": nrm(ks[5], (DEPTH, D_MODEL, IN_TOTAL), D_MODEL ** -0.5),
        "conv_a_w": nrm(ks[6], (DEPTH, CONV_A_WIDTH, D_CONV), CONV_A_WIDTH ** -0.5),
        "conv_b_w": nrm(ks[7], (DEPTH, CONV_B_WIDTH, D_RNN), CONV_B_WIDTH ** -0.5),
        "conv_b_bias": nrm(ks[8], (DEPTH, D_RNN), 0.01),
        "w_rg_a": nrm(ks[9], (DEPTH, RNN_HEADS, RNN_BLOCK, RNN_BLOCK), RNN_BLOCK ** -0.5),
        "b_rg_a": nrm(ks[10], (DEPTH, D_RNN), 0.01),
        "w_rg_x": nrm(ks[12], (DEPTH, RNN_HEADS, RNN_BLOCK, RNN_BLOCK), RNN_BLOCK ** -0.5),
        "b_rg_x": nrm(ks[13], (DEPTH, D_RNN), 0.01),
        "lru_lambda": lru_lambda,
        "w_out": nrm(ks[14], (DEPTH, D_MODEL, D_MODEL), D_MODEL ** -0.5),
        "g_norm_ffn": 1.0 + nrm(ks[15], (DEPTH, D_MODEL), 0.05),
        "w_gate_up": nrm(ks[16], (DEPTH, D_MODEL, 2 * D_FF), D_MODEL ** -0.5),
        "w_down": nrm(ks[17], (DEPTH, D_FF, D_MODEL), D_FF ** -0.5),
        "g_norm_final": 1.0 + nrm(ks[18], (D_MODEL,), 0.05),
    }


def _fwd_reference(x, c, w_ada, b_ada, g_norm_mix, w_in, conv_a_w, conv_b_w, conv_b_bias,
              w_rg_a, b_rg_a, w_rg_x, b_rg_x, lru_lambda, w_out, g_norm_ffn,
              w_gate_up, w_down, g_norm_final):
    c_act = jax.nn.silu(c)
    for l in range(DEPTH):
        mod = c_act @ w_ada[l] + b_ada[l]
        sh1, sc1, gt1, sh2, sc2, gt2 = jnp.split(mod, N_MOD, axis=-1)

        h = modulate(rmsnorm(x, g_norm_mix[l]), sh1, sc1)
        proj = h @ w_in[l]
        cb, cc, cx, rx, rg, ga, gb = jnp.split(proj, IN_SPLITS, axis=-1)
        y_a = cb * causal_depthwise_conv(cc * cx, conv_a_w[l])
        u = causal_depthwise_conv(rx, conv_b_w[l]) + conv_b_bias[l]
        y_b = rg_lru(u, w_rg_a[l], b_rg_a[l], w_rg_x[l], b_rg_x[l], lru_lambda[l]) * jax.nn.gelu(rg)
        merged = jax.nn.sigmoid(ga) * y_a + jax.nn.sigmoid(gb) * y_b
        x = x + gt1[:, None, :] * (merged @ w_out[l])

        h = modulate(rmsnorm(x, g_norm_ffn[l]), sh2, sc2)
        g_ff, u_ff = jnp.split(h @ w_gate_up[l], 2, axis=-1)
        x = x + gt2[:, None, :] * ((jax.nn.silu(g_ff) * u_ff) @ w_down[l])
    return rmsnorm(x, g_norm_final)


import jax as _jax
import jax.numpy as _jnp

TWIN_FORMAT = 'train_step'
FWD_PARAMS = ['x', 'c', 'w_ada', 'b_ada', 'g_norm_mix', 'w_in', 'conv_a_w', 'conv_b_w', 'conv_b_bias', 'w_rg_a', 'b_rg_a', 'w_rg_x', 'b_rg_x', 'lru_lambda', 'w_out', 'g_norm_ffn', 'w_gate_up', 'w_down', 'g_norm_final']
TWIN_WEIGHTS = ['w_ada', 'b_ada', 'g_norm_mix', 'w_in', 'conv_a_w', 'conv_b_w', 'conv_b_bias', 'w_rg_a', 'b_rg_a', 'w_rg_x', 'b_rg_x', 'lru_lambda', 'w_out', 'g_norm_ffn', 'w_gate_up', 'w_down', 'g_norm_final']
TWIN_DIFF_INPUT = 'x'
TWIN_INPUTS = ['x', 'c', 'w_ada', 'b_ada', 'g_norm_mix', 'w_in', 'conv_a_w', 'conv_b_w', 'conv_b_bias', 'w_rg_a', 'b_rg_a', 'w_rg_x', 'b_rg_x', 'lru_lambda', 'w_out', 'g_norm_ffn', 'w_gate_up', 'w_down', 'g_norm_final', 'loss_target', 'm_w_ada', 'm_b_ada', 'm_g_norm_mix', 'm_w_in', 'm_conv_a_w', 'm_conv_b_w', 'm_conv_b_bias', 'm_w_rg_a', 'm_b_rg_a', 'm_w_rg_x', 'm_b_rg_x', 'm_lru_lambda', 'm_w_out', 'm_g_norm_ffn', 'm_w_gate_up', 'm_w_down', 'm_g_norm_final', 'v_w_ada', 'v_b_ada', 'v_g_norm_mix', 'v_w_in', 'v_conv_a_w', 'v_conv_b_w', 'v_conv_b_bias', 'v_w_rg_a', 'v_b_rg_a', 'v_w_rg_x', 'v_b_rg_x', 'v_lru_lambda', 'v_w_out', 'v_g_norm_ffn', 'v_w_gate_up', 'v_w_down', 'v_g_norm_final']
TWIN_OUTPUTS = ['loss', 'grad_x', 'grad_w_ada', 'grad_b_ada', 'grad_g_norm_mix', 'grad_w_in', 'grad_conv_a_w', 'grad_conv_b_w', 'grad_conv_b_bias', 'grad_w_rg_a', 'grad_b_rg_a', 'grad_w_rg_x', 'grad_b_rg_x', 'grad_lru_lambda', 'grad_w_out', 'grad_g_norm_ffn', 'grad_w_gate_up', 'grad_w_down', 'grad_g_norm_final', 'delta_w_ada', 'delta_b_ada', 'delta_g_norm_mix', 'delta_w_in', 'delta_conv_a_w', 'delta_conv_b_w', 'delta_conv_b_bias', 'delta_w_rg_a', 'delta_b_rg_a', 'delta_w_rg_x', 'delta_b_rg_x', 'delta_lru_lambda', 'delta_w_out', 'delta_g_norm_ffn', 'delta_w_gate_up', 'delta_w_down', 'delta_g_norm_final', 'new_m_w_ada', 'new_m_b_ada', 'new_m_g_norm_mix', 'new_m_w_in', 'new_m_conv_a_w', 'new_m_conv_b_w', 'new_m_conv_b_bias', 'new_m_w_rg_a', 'new_m_b_rg_a', 'new_m_w_rg_x', 'new_m_b_rg_x', 'new_m_lru_lambda', 'new_m_w_out', 'new_m_g_norm_ffn', 'new_m_w_gate_up', 'new_m_w_down', 'new_m_g_norm_final', 'new_v_w_ada', 'new_v_b_ada', 'new_v_g_norm_mix', 'new_v_w_in', 'new_v_conv_a_w', 'new_v_conv_b_w', 'new_v_conv_b_bias', 'new_v_w_rg_a', 'new_v_b_rg_a', 'new_v_w_rg_x', 'new_v_b_rg_x', 'new_v_lru_lambda', 'new_v_w_out', 'new_v_g_norm_ffn', 'new_v_w_gate_up', 'new_v_w_down', 'new_v_g_norm_final']
TWIN_LEAF_KINDS = {'loss': 'loss', 'grad_x': 'grad_x', 'grad_w_ada': 'grad_w', 'grad_b_ada': 'grad_w', 'grad_g_norm_mix': 'grad_w', 'grad_w_in': 'grad_w', 'grad_conv_a_w': 'grad_w', 'grad_conv_b_w': 'grad_w', 'grad_conv_b_bias': 'grad_w', 'grad_w_rg_a': 'grad_w', 'grad_b_rg_a': 'grad_w', 'grad_w_rg_x': 'grad_w', 'grad_b_rg_x': 'grad_w', 'grad_lru_lambda': 'grad_w', 'grad_w_out': 'grad_w', 'grad_g_norm_ffn': 'grad_w', 'grad_w_gate_up': 'grad_w', 'grad_w_down': 'grad_w', 'grad_g_norm_final': 'grad_w', 'delta_w_ada': 'delta_w', 'delta_b_ada': 'delta_w', 'delta_g_norm_mix': 'delta_w', 'delta_w_in': 'delta_w', 'delta_conv_a_w': 'delta_w', 'delta_conv_b_w': 'delta_w', 'delta_conv_b_bias': 'delta_w', 'delta_w_rg_a': 'delta_w', 'delta_b_rg_a': 'delta_w', 'delta_w_rg_x': 'delta_w', 'delta_b_rg_x': 'delta_w', 'delta_lru_lambda': 'delta_w', 'delta_w_out': 'delta_w', 'delta_g_norm_ffn': 'delta_w', 'delta_w_gate_up': 'delta_w', 'delta_w_down': 'delta_w', 'delta_g_norm_final': 'delta_w', 'new_m_w_ada': 'new_m', 'new_m_b_ada': 'new_m', 'new_m_g_norm_mix': 'new_m', 'new_m_w_in': 'new_m', 'new_m_conv_a_w': 'new_m', 'new_m_conv_b_w': 'new_m', 'new_m_conv_b_bias': 'new_m', 'new_m_w_rg_a': 'new_m', 'new_m_b_rg_a': 'new_m', 'new_m_w_rg_x': 'new_m', 'new_m_b_rg_x': 'new_m', 'new_m_lru_lambda': 'new_m', 'new_m_w_out': 'new_m', 'new_m_g_norm_ffn': 'new_m', 'new_m_w_gate_up': 'new_m', 'new_m_w_down': 'new_m', 'new_m_g_norm_final': 'new_m', 'new_v_w_ada': 'new_v', 'new_v_b_ada': 'new_v', 'new_v_g_norm_mix': 'new_v', 'new_v_w_in': 'new_v', 'new_v_conv_a_w': 'new_v', 'new_v_conv_b_w': 'new_v', 'new_v_conv_b_bias': 'new_v', 'new_v_w_rg_a': 'new_v', 'new_v_b_rg_a': 'new_v', 'new_v_w_rg_x': 'new_v', 'new_v_b_rg_x': 'new_v', 'new_v_lru_lambda': 'new_v', 'new_v_w_out': 'new_v', 'new_v_g_norm_ffn': 'new_v', 'new_v_w_gate_up': 'new_v', 'new_v_w_down': 'new_v', 'new_v_g_norm_final': 'new_v'}


def _forward(args):
    return _fwd_reference(*[args[k] for k in FWD_PARAMS])


def _output_shape():
    def fwd():
        inp = _fwd_setup_inputs(0)
        return _fwd_reference(*[inp[k] for k in FWD_PARAMS])
    out = _jax.eval_shape(fwd)
    return out.shape, out.dtype

N_MICROBATCH = 1
ADAM_LR = 0.001
ADAM_B1 = 0.9
ADAM_B2 = 0.999
ADAM_EPS = 1e-08
ADAM_WD = 0.01
ADAM_STEP = 10
PER_EXAMPLE_BATCH_AXIS = {'x': 0, 'c': 0, 'loss_target': 0}
SHARED_INPUTS = []
_WEIGHT_DTYPES = {'w_ada': _jnp.float32, 'b_ada': _jnp.float32, 'g_norm_mix': _jnp.float32, 'w_in': _jnp.float32, 'conv_a_w': _jnp.float32, 'conv_b_w': _jnp.float32, 'conv_b_bias': _jnp.float32, 'w_rg_a': _jnp.float32, 'b_rg_a': _jnp.float32, 'w_rg_x': _jnp.float32, 'b_rg_x': _jnp.float32, 'lru_lambda': _jnp.float32, 'w_out': _jnp.float32, 'g_norm_ffn': _jnp.float32, 'w_gate_up': _jnp.float32, 'w_down': _jnp.float32, 'g_norm_final': _jnp.float32}
MOMENT_SCALE = {'w_ada': 1.521874e-01, 'b_ada': 2.919573e-01, 'g_norm_mix': 1.030275e-01, 'w_in': 5.213399e-02, 'conv_a_w': 5.269821e-02, 'conv_b_w': 8.170646e-02, 'conv_b_bias': 2.530248e-01, 'w_rg_a': 7.704520e-03, 'b_rg_a': 1.341546e-02, 'w_rg_x': 1.536206e-02, 'b_rg_x': 2.965142e-02, 'lru_lambda': 3.451029e-02, 'w_out': 8.737643e-02, 'g_norm_ffn': 7.335436e-02, 'w_gate_up': 3.222005e-02, 'w_down': 5.244606e-02, 'g_norm_final': 6.417596e+01}


def _to_microbatches(a, axis):
    t = _jnp.moveaxis(a, axis, 0)
    t = t.reshape((N_MICROBATCH, t.shape[0] // N_MICROBATCH) + t.shape[1:])
    return _jnp.moveaxis(t, 1, axis + 1)


def setup_inputs(seed: int = 0) -> dict:
    inp = _fwd_setup_inputs(seed)
    key = _jax.random.fold_in(_jax.random.key(seed), 7919)
    shape, _ = _output_shape()
    out = dict(inp)
    out["loss_target"] = _jax.random.normal(_jax.random.fold_in(key, 0), shape, _jnp.float32)
    for i, name in enumerate(TWIN_WEIGHTS):
        w = inp[name].astype(_jnp.float32)
        if MOMENT_SCALE is None:
            s = _jnp.sqrt(_jnp.mean(_jnp.square(w)) + 1e-30)
        else:
            s = MOMENT_SCALE[name]
        km, kv = _jax.random.split(_jax.random.fold_in(key, i + 1))
        out[name] = w
        out["m_" + name] = s * _jax.random.normal(km, w.shape, _jnp.float32)
        out["v_" + name] = (s * s) * _jax.random.uniform(kv, w.shape, _jnp.float32, 0.5, 1.5)
    if N_MICROBATCH > 1:
        for name, axis in PER_EXAMPLE_BATCH_AXIS.items():
            out[name] = _to_microbatches(out[name], axis)
    return {'x': out['x'], 'c': out['c'], 'w_ada': out['w_ada'], 'b_ada': out['b_ada'], 'g_norm_mix': out['g_norm_mix'], 'w_in': out['w_in'], 'conv_a_w': out['conv_a_w'], 'conv_b_w': out['conv_b_w'], 'conv_b_bias': out['conv_b_bias'], 'w_rg_a': out['w_rg_a'], 'b_rg_a': out['b_rg_a'], 'w_rg_x': out['w_rg_x'], 'b_rg_x': out['b_rg_x'], 'lru_lambda': out['lru_lambda'], 'w_out': out['w_out'], 'g_norm_ffn': out['g_norm_ffn'], 'w_gate_up': out['w_gate_up'], 'w_down': out['w_down'], 'g_norm_final': out['g_norm_final'], 'loss_target': out['loss_target'], 'm_w_ada': out['m_w_ada'], 'm_b_ada': out['m_b_ada'], 'm_g_norm_mix': out['m_g_norm_mix'], 'm_w_in': out['m_w_in'], 'm_conv_a_w': out['m_conv_a_w'], 'm_conv_b_w': out['m_conv_b_w'], 'm_conv_b_bias': out['m_conv_b_bias'], 'm_w_rg_a': out['m_w_rg_a'], 'm_b_rg_a': out['m_b_rg_a'], 'm_w_rg_x': out['m_w_rg_x'], 'm_b_rg_x': out['m_b_rg_x'], 'm_lru_lambda': out['m_lru_lambda'], 'm_w_out': out['m_w_out'], 'm_g_norm_ffn': out['m_g_norm_ffn'], 'm_w_gate_up': out['m_w_gate_up'], 'm_w_down': out['m_w_down'], 'm_g_norm_final': out['m_g_norm_final'], 'v_w_ada': out['v_w_ada'], 'v_b_ada': out['v_b_ada'], 'v_g_norm_mix': out['v_g_norm_mix'], 'v_w_in': out['v_w_in'], 'v_conv_a_w': out['v_conv_a_w'], 'v_conv_b_w': out['v_conv_b_w'], 'v_conv_b_bias': out['v_conv_b_bias'], 'v_w_rg_a': out['v_w_rg_a'], 'v_b_rg_a': out['v_b_rg_a'], 'v_w_rg_x': out['v_w_rg_x'], 'v_b_rg_x': out['v_b_rg_x'], 'v_lru_lambda': out['v_lru_lambda'], 'v_w_out': out['v_w_out'], 'v_g_norm_ffn': out['v_g_norm_ffn'], 'v_w_gate_up': out['v_w_gate_up'], 'v_w_down': out['v_w_down'], 'v_g_norm_final': out['v_g_norm_final']}


def _loss(weights, diff, rest, loss_target):
    with _jax.named_scope("forward"):
        args = {**rest, TWIN_DIFF_INPUT: diff, **{k: w.astype(_WEIGHT_DTYPES[k]) for k, w in weights.items()}}
        y = _forward(args)
    with _jax.named_scope("loss_head"):
        err = _jnp.square(y.astype(_jnp.float32) - loss_target)
        return 0.5 * _jnp.sum(_jnp.mean(err, axis=-1)) if err.ndim else 0.5 * err


def _adamw(w, g, m, v):
    m = ADAM_B1 * m + (1.0 - ADAM_B1) * g
    v = ADAM_B2 * v + (1.0 - ADAM_B2) * _jnp.square(g)
    m_hat = m / (1.0 - ADAM_B1 ** ADAM_STEP)
    v_hat = v / (1.0 - ADAM_B2 ** ADAM_STEP)
    delta = -ADAM_LR * (m_hat / (_jnp.sqrt(v_hat) + ADAM_EPS) + ADAM_WD * w)
    return delta, m, v


def reference(x, c, w_ada, b_ada, g_norm_mix, w_in, conv_a_w, conv_b_w, conv_b_bias, w_rg_a, b_rg_a, w_rg_x, b_rg_x, lru_lambda, w_out, g_norm_ffn, w_gate_up, w_down, g_norm_final, loss_target, m_w_ada, m_b_ada, m_g_norm_mix, m_w_in, m_conv_a_w, m_conv_b_w, m_conv_b_bias, m_w_rg_a, m_b_rg_a, m_w_rg_x, m_b_rg_x, m_lru_lambda, m_w_out, m_g_norm_ffn, m_w_gate_up, m_w_down, m_g_norm_final, v_w_ada, v_b_ada, v_g_norm_mix, v_w_in, v_conv_a_w, v_conv_b_w, v_conv_b_bias, v_w_rg_a, v_b_rg_a, v_w_rg_x, v_b_rg_x, v_lru_lambda, v_w_out, v_g_norm_ffn, v_w_gate_up, v_w_down, v_g_norm_final):
    given = dict(x=x, c=c, w_ada=w_ada, b_ada=b_ada, g_norm_mix=g_norm_mix, w_in=w_in, conv_a_w=conv_a_w, conv_b_w=conv_b_w, conv_b_bias=conv_b_bias, w_rg_a=w_rg_a, b_rg_a=b_rg_a, w_rg_x=w_rg_x, b_rg_x=b_rg_x, lru_lambda=lru_lambda, w_out=w_out, g_norm_ffn=g_norm_ffn, w_gate_up=w_gate_up, w_down=w_down, g_norm_final=g_norm_final, loss_target=loss_target, m_w_ada=m_w_ada, m_b_ada=m_b_ada, m_g_norm_mix=m_g_norm_mix, m_w_in=m_w_in, m_conv_a_w=m_conv_a_w, m_conv_b_w=m_conv_b_w, m_conv_b_bias=m_conv_b_bias, m_w_rg_a=m_w_rg_a, m_b_rg_a=m_b_rg_a, m_w_rg_x=m_w_rg_x, m_b_rg_x=m_b_rg_x, m_lru_lambda=m_lru_lambda, m_w_out=m_w_out, m_g_norm_ffn=m_g_norm_ffn, m_w_gate_up=m_w_gate_up, m_w_down=m_w_down, m_g_norm_final=m_g_norm_final, v_w_ada=v_w_ada, v_b_ada=v_b_ada, v_g_norm_mix=v_g_norm_mix, v_w_in=v_w_in, v_conv_a_w=v_conv_a_w, v_conv_b_w=v_conv_b_w, v_conv_b_bias=v_conv_b_bias, v_w_rg_a=v_w_rg_a, v_b_rg_a=v_b_rg_a, v_w_rg_x=v_w_rg_x, v_b_rg_x=v_b_rg_x, v_lru_lambda=v_lru_lambda, v_w_out=v_w_out, v_g_norm_ffn=v_g_norm_ffn, v_w_gate_up=v_w_gate_up, v_w_down=v_w_down, v_g_norm_final=v_g_norm_final)
    weights = {n: given[n] for n in TWIN_WEIGHTS}
    shared = {n: given[n] for n in SHARED_INPUTS}
    per_example = {n: given[n] for n in ['x', 'c']}
    grad_fn = _jax.value_and_grad(_loss, argnums=(0, 1))

    def one_microbatch(ex, loss_target):
        ex = dict(ex)
        diff = ex.pop(TWIN_DIFF_INPUT)
        return grad_fn(weights, diff, {**shared, **ex}, loss_target)

    if N_MICROBATCH == 1:
        loss, (grad_w, grad_x) = one_microbatch(per_example, given["loss_target"])
    else:
        def body(carry, xs):
            loss_sum, grad_sum = carry
            l_k, (gw_k, gx_k) = one_microbatch(xs[0], xs[1])
            with _jax.named_scope("update"):
                return (loss_sum + l_k, _jax.tree.map(_jnp.add, grad_sum, gw_k)), gx_k

        init = (_jnp.zeros((), _jnp.float32), _jax.tree.map(_jnp.zeros_like, weights))
        (loss, grad_w), grad_x = _jax.lax.scan(body, init, (per_example, given["loss_target"]))
    with _jax.named_scope("update"):
        delta_w, new_m, new_v = {}, {}, {}
        for n in TWIN_WEIGHTS:
            delta_w[n], new_m[n], new_v[n] = _adamw(weights[n], grad_w[n], given["m_" + n], given["v_" + n])
    return (loss, grad_x, *[grad_w[n] for n in TWIN_WEIGHTS], *[delta_w[n] for n in TWIN_WEIGHTS],
            *[new_m[n] for n in TWIN_WEIGHTS], *[new_v[n] for n in TWIN_WEIGHTS])
```

```python
import functools

import jax
import jax.numpy as jnp
from jax import lax
from jax.experimental import pallas as pl
from jax.experimental.pallas import tpu as pltpu

F32 = jnp.float32
BF16 = jnp.bfloat16
MESH = pl.DeviceIdType.MESH

D = 1024
N_CHIPS = 4
N_DEV = 8
D_IN = 7 * D
C_IN = D_IN // N_CHIPS
D_FF = 2816
C_GU = 2 * D_FF // N_CHIPS
HEADS = 4
HB = D // HEADS
EPS = 1e-6
LRU_C = 8.0
ADAM_LR, ADAM_B1, ADAM_B2, ADAM_EPS, ADAM_WD, ADAM_STEP = 0.001, 0.9, 0.999, 1e-08, 0.01, 10
VMEM_LIMIT = 56 << 20

(V_GMIX, V_SC1, V_SH1, V_GT1, V_GFFN, V_SC2, V_SH2, V_GT2, V_GFIN, V_CBB, V_BA, V_BX, V_LAM,
 V_WA0, V_WA1, V_WA2, V_WB0, V_WB1, V_WB2, V_WB3) = range(20)
N_VEC = 24
(G_SH1, G_SC1, G_GT1, G_SH2, G_SC2, G_GT2, G_GMIX, G_CBB, G_BA, G_BX, G_LAM, G_GFFN, G_GFIN,
 G_WA0, G_WA1, G_WA2, G_WB0, G_WB1, G_WB2, G_WB3, G_LOSS) = range(21)
N_SMALL = 24

_VMEM = pl.BlockSpec(memory_space=pltpu.VMEM)
_ANY = pl.BlockSpec(memory_space=pl.ANY)


def _cparams(n_grid=1):
    return pltpu.CompilerParams(dimension_semantics=("arbitrary",) * n_grid, vmem_limit_bytes=VMEM_LIMIT)


def _rms(x):
    rstd = lax.rsqrt(jnp.mean(x * x, axis=-1, keepdims=True) + EPS)
    return x * rstd, rstd


def _rms_bwd(dxhat, xhat, rstd):
    return rstd * (dxhat - xhat * jnp.mean(dxhat * xhat, axis=-1, keepdims=True))


def _rowsum(v):
    return jnp.sum(v, axis=0, keepdims=True)


def _dot(a, b):
    return jnp.dot(a, b, preferred_element_type=F32)


def _dot_nt(a, b):
    return lax.dot_general(a, b, (((1,), (1,)), ((), ())), preferred_element_type=F32)


def _dot_tn(a, b):
    return lax.dot_general(a, b, (((0,), (0,)), ((), ())), preferred_element_type=F32)


def _gelu(x):
    k, c = 0.7978845608028654, 0.044715
    t = jnp.tanh(k * (x + c * x * x * x))
    return 0.5 * x * (1.0 + t), 0.5 * (1.0 + t) + 0.5 * x * (1.0 - t * t) * k * (1.0 + 3.0 * c * x * x)


def _log_sigmoid(lam):
    return jnp.minimum(lam, 0.0) - jnp.log1p(jnp.exp(-jnp.abs(lam)))


def _lru_gates(u, wa_ref, wx_ref, v_ref, row0):
    ub = u.astype(BF16)
    pre_a = jnp.concatenate([_dot(ub[:, h * HB:(h + 1) * HB], wa_ref[h]) for h in range(HEADS)], axis=1)
    pre_x = jnp.concatenate([_dot(ub[:, h * HB:(h + 1) * HB], wx_ref[h]) for h in range(HEADS)], axis=1)
    r = jax.nn.sigmoid(pre_a + v_ref[V_BA:V_BA + 1, :])
    ig = jax.nn.sigmoid(pre_x + v_ref[V_BX:V_BX + 1, :])
    log_a = LRU_C * r * _log_sigmoid(v_ref[V_LAM:V_LAM + 1, :])
    a = jnp.exp(log_a)
    x2 = 2.0 * log_a
    m2 = jnp.where(x2 > -0.03, -x2 * (1.0 + x2 * (0.5 + x2 * (1.0 / 6.0 + x2 * (1.0 / 24.0)))), 1.0 - a * a)
    mult = jnp.where(row0, 1.0, jnp.sqrt(jnp.maximum(m2, 0.0)))
    return r, ig, a, mult


def _fwd_in(x, vecs, w_in_g, ts):
    s = x.shape[0]

    def body(x_ref, v_ref, w_ref, h1_ref, proj_ref):
        xhat, _ = _rms(x_ref[...])
        h = xhat * v_ref[V_GMIX:V_GMIX + 1, :] * (1.0 + v_ref[V_SC1:V_SC1 + 1, :]) + v_ref[V_SH1:V_SH1 + 1, :]
        hb = h.astype(BF16)
        h1_ref[...] = hb
        for k in range(N_CHIPS):
            proj_ref[:, k * C_IN:(k + 1) * C_IN] = _dot(hb, w_ref[k]).astype(BF16)

    return pl.pallas_call(
        body, grid=(s // ts,),
        out_shape=(jax.ShapeDtypeStruct((s, D), BF16), jax.ShapeDtypeStruct((s, D_IN), BF16)),
        in_specs=[pl.BlockSpec((ts, D), lambda i: (i, 0)), _VMEM, _VMEM],
        out_specs=[pl.BlockSpec((ts, D), lambda i: (i, 0)), pl.BlockSpec((ts, D_IN), lambda i: (i, 0))],
        compiler_params=_cparams(), name="fwd_in")(x, vecs, w_in_g)


def _fwd_mix(proj, x, vecs, w_rga, w_rgx, w_out, ts):
    s = x.shape[0]

    def body(proj_ref, x_ref, v_ref, wa_ref, wx_ref, wo_ref, x1_ref, mg_ref, z1_ref, u_ref, h_ref,
             ua_buf, rx_buf, a_buf, b_buf, hcarry):
        i = pl.program_id(0)

        @pl.when(i == 0)
        def _():
            ua_buf[0:8, :] = jnp.zeros((8, D), F32)
            rx_buf[0:8, :] = jnp.zeros((8, D), F32)
            hcarry[...] = jnp.zeros((8, D), F32)

        def seg(j):
            return proj_ref[:, j * D:(j + 1) * D].astype(F32)

        def vrow(j):
            return v_ref[j:j + 1, :]

        cb, cc, cx, rx, rg, ga, gb = (seg(j) for j in range(7))
        ua = cc * cx
        ua_buf[8:ts + 8, :] = ua
        rx_buf[8:ts + 8, :] = rx
        va = vrow(V_WA2) * ua + vrow(V_WA1) * ua_buf[pl.ds(7, ts), :] + vrow(V_WA0) * ua_buf[pl.ds(6, ts), :]
        u = (vrow(V_WB3) * rx + vrow(V_WB2) * rx_buf[pl.ds(7, ts), :] + vrow(V_WB1) * rx_buf[pl.ds(6, ts), :]
             + vrow(V_WB0) * rx_buf[pl.ds(5, ts), :] + vrow(V_CBB))
        ua_buf[0:8, :] = ua_buf[ts:ts + 8, :]
        rx_buf[0:8, :] = rx_buf[ts:ts + 8, :]
        u_ref[...] = u

        rows = lax.broadcasted_iota(jnp.int32, (ts, D), 0)
        row0 = jnp.logical_and(rows == 0, i == 0)
        _, ig, a, mult = _lru_gates(u, wa_ref, wx_ref, v_ref, row0)
        bx = mult * (ig * u)

        sub = rows % 8
        for d in (1, 2, 4):
            m = sub >= d
            bx = jnp.where(m, a * pltpu.roll(bx, d, axis=0) + bx, bx)
            a = jnp.where(m, a * pltpu.roll(a, d, axis=0), a)
        a_buf[...] = a
        b_buf[...] = bx

        def grp(g, carry):
            off = pl.multiple_of(g * 8, 8)
            h_ref[pl.ds(off, 8), :] = a_buf[pl.ds(off, 8), :] * carry + b_buf[pl.ds(off, 8), :]
            return jnp.broadcast_to(h_ref[pl.ds(off + 7, 1), :], (8, D))

        hcarry[...] = lax.fori_loop(0, ts // 8, grp, hcarry[...])
        h = h_ref[...]

        gel, _ = _gelu(rg)
        merged = (jax.nn.sigmoid(ga) * (cb * va) + jax.nn.sigmoid(gb) * (h * gel)).astype(BF16)
        mg_ref[...] = merged
        z1 = _dot(merged, wo_ref[...])
        z1_ref[...] = z1.astype(BF16)
        x1_ref[...] = x_ref[...] + vrow(V_GT1) * z1

    row = lambda i: (i, 0)
    return pl.pallas_call(
        body, grid=(s // ts,),
        out_shape=(jax.ShapeDtypeStruct((s, D), F32), jax.ShapeDtypeStruct((s, D), BF16), jax.ShapeDtypeStruct((s, D), BF16),
                   jax.ShapeDtypeStruct((s, D), F32), jax.ShapeDtypeStruct((s, D), F32)),
        in_specs=[pl.BlockSpec((ts, D_IN), row), pl.BlockSpec((ts, D), row), _VMEM, _VMEM, _VMEM, _VMEM],
        out_specs=[pl.BlockSpec((ts, D), row)] * 5,
        scratch_shapes=[pltpu.VMEM((ts + 8, D), F32), pltpu.VMEM((ts + 8, D), F32), pltpu.VMEM((ts, D), F32),
                        pltpu.VMEM((ts, D), F32), pltpu.VMEM((8, D), F32)],
        compiler_params=_cparams(), name="fwd_mix")(proj, x, vecs, w_rga, w_rgx, w_out)


def _ffn_loss(x1, target, vecs, w_gu_g, w_dn, ts):
    s = x1.shape[0]

    def body(x1_ref, t_ref, v_ref, wgu_ref, wdn_ref, dx1_ref, h2_ref, act_ref, dz2_ref, dgu_ref, sm_ref):
        @pl.when(pl.program_id(0) == 0)
        def _():
            sm_ref[...] = jnp.zeros((N_SMALL, D), F32)

        def vrow(j):
            return v_ref[j:j + 1, :]

        def acc(j, val):
            sm_ref[j:j + 1, :] += _rowsum(val)

        x1 = x1_ref[...]
        xh1, rstd1 = _rms(x1)
        n2 = xh1 * vrow(V_GFFN)
        h2 = (n2 * (1.0 + vrow(V_SC2)) + vrow(V_SH2)).astype(BF16)
        h2_ref[...] = h2
        g = jnp.concatenate([_dot(h2, wgu_ref[0]), _dot(h2, wgu_ref[1])], axis=1)
        up = jnp.concatenate([_dot(h2, wgu_ref[2]), _dot(h2, wgu_ref[3])], axis=1)
        sg = jax.nn.sigmoid(g)
        silu = g * sg
        act = (silu * up).astype(BF16)
        act_ref[...] = act
        z2 = _dot(act, wdn_ref[...])
        x2 = x1 + vrow(V_GT2) * z2
        xh2, rstd2 = _rms(x2)
        err = xh2 * vrow(V_GFIN) - t_ref[...]
        acc(G_LOSS, (0.5 / D) * err * err)
        dy = err * (1.0 / D)
        acc(G_GFIN, dy * xh2)
        dx2 = _rms_bwd(dy * vrow(V_GFIN), xh2, rstd2)
        acc(G_GT2, dx2 * z2)
        dz2 = (vrow(V_GT2) * dx2).astype(BF16)
        dz2_ref[...] = dz2
        dact = _dot_nt(dz2, wdn_ref[...])
        dgate = (dact * up * (sg * (1.0 + g * (1.0 - sg)))).astype(BF16)
        dup = (dact * silu).astype(BF16)
        dgu_ref[:, 0:D_FF] = dgate
        dgu_ref[:, D_FF:2 * D_FF] = dup
        dh2 = (_dot_nt(dgate[:, 0:C_GU], wgu_ref[0]) + _dot_nt(dgate[:, C_GU:2 * C_GU], wgu_ref[1])
               + _dot_nt(dup[:, 0:C_GU], wgu_ref[2]) + _dot_nt(dup[:, C_GU:2 * C_GU], wgu_ref[3]))
        acc(G_SH2, dh2)
        acc(G_SC2, dh2 * n2)
        dn2 = dh2 * (1.0 + vrow(V_SC2))
        acc(G_GFFN, dn2 * xh1)
        dx1_ref[...] = dx2 + _rms_bwd(dn2 * vrow(V_GFFN), xh1, rstd1)

    row = lambda i: (i, 0)
    return pl.pallas_call(
        body, grid=(s // ts,),
        out_shape=(jax.ShapeDtypeStruct((s, D), F32), jax.ShapeDtypeStruct((s, D), BF16), jax.ShapeDtypeStruct((s, D_FF), BF16),
                   jax.ShapeDtypeStruct((s, D), BF16), jax.ShapeDtypeStruct((s, 2 * D_FF), BF16),
                   jax.ShapeDtypeStruct((N_SMALL, D), F32)),
        in_specs=[pl.BlockSpec((ts, D), row), pl.BlockSpec((ts, D), row), _VMEM, _VMEM, _VMEM],
        out_specs=[pl.BlockSpec((ts, D), row), pl.BlockSpec((ts, D), row), pl.BlockSpec((ts, D_FF), row),
                   pl.BlockSpec((ts, D), row), pl.BlockSpec((ts, 2 * D_FF), row), pl.BlockSpec((N_SMALL, D), lambda i: (0, 0))],
        compiler_params=_cparams(), name="ffn_loss")(x1, target, vecs, w_gu_g, w_dn)


def _bwd_mix(dx1, z1, proj, u, h, vecs, w_rga, w_rgx, w_out, ts):
    s = dx1.shape[0]
    nt = s // ts

    def body(dx1_ref, z1_ref, proj_ref, u_ref, h_ref, hh_ref, cch_ref, cxh_ref, v_ref, wa_ref, wx_ref, wo_ref,
             dproj_ref, dz1_ref, sm_ref, dwa_ref, dwx_ref,
             ua_buf, h_buf, a_buf, dva_buf, du_buf, ca_buf, cb_buf, l_buf, lcarry):
        i = pl.program_id(0)
        first_tile = i == nt - 1

        @pl.when(i == 0)
        def _():
            a_buf[ts:ts + 8, :] = jnp.zeros((8, D), F32)
            dva_buf[ts:ts + 8, :] = jnp.zeros((8, D), F32)
            du_buf[ts:ts + 8, :] = jnp.zeros((8, D), F32)
            lcarry[...] = jnp.zeros((8, D), F32)
            sm_ref[...] = jnp.zeros((N_SMALL, D), F32)
            dwa_ref[...] = jnp.zeros((HEADS, HB, HB), F32)
            dwx_ref[...] = jnp.zeros((HEADS, HB, HB), F32)

        def seg(j):
            return proj_ref[:, j * D:(j + 1) * D].astype(F32)

        def vrow(j):
            return v_ref[j:j + 1, :]

        def acc(j, val):
            sm_ref[j:j + 1, :] += _rowsum(val)

        cb, cc, cx, rx, rg, ga, gb = (seg(j) for j in range(7))
        ua = cc * cx
        ua_halo = cch_ref[8:16, :].astype(F32) * cxh_ref[8:16, :].astype(F32)
        ua_buf[0:8, :] = jnp.where(first_tile, 0.0, ua_halo)
        ua_buf[8:ts + 8, :] = ua
        va = vrow(V_WA2) * ua + vrow(V_WA1) * ua_buf[pl.ds(7, ts), :] + vrow(V_WA0) * ua_buf[pl.ds(6, ts), :]
        u = u_ref[...]
        h = h_ref[...]
        rows = lax.broadcasted_iota(jnp.int32, (ts, D), 0)
        row0 = jnp.logical_and(rows == 0, first_tile)
        r, ig, a, mult = _lru_gates(u, wa_ref, wx_ref, v_ref, row0)
        sga = jax.nn.sigmoid(ga)
        sgb = jax.nn.sigmoid(gb)
        gel, dgel = _gelu(rg)

        dx1 = dx1_ref[...]
        acc(G_GT1, dx1 * z1_ref[...].astype(F32))
        dz1 = (vrow(V_GT1) * dx1).astype(BF16)
        dz1_ref[...] = dz1
        dmg = _dot_nt(dz1, wo_ref[...])
        dproj_ref[:, 5 * D:6 * D] = (dmg * (cb * va) * sga * (1.0 - sga)).astype(BF16)
        dproj_ref[:, 6 * D:7 * D] = (dmg * (h * gel) * sgb * (1.0 - sgb)).astype(BF16)
        dya = dmg * sga
        dyb = dmg * sgb

        dproj_ref[:, 0:D] = (dya * va).astype(BF16)
        dva = dya * cb
        dva_buf[0:ts, :] = dva
        dva1 = dva_buf[pl.ds(1, ts), :]
        dva2 = dva_buf[pl.ds(2, ts), :]
        dua = vrow(V_WA2) * dva + vrow(V_WA1) * dva1 + vrow(V_WA0) * dva2
        acc(G_WA2, ua * dva)
        acc(G_WA1, ua * dva1)
        acc(G_WA0, ua * dva2)
        dva_buf[ts:ts + 8, :] = dva_buf[0:8, :]
        dproj_ref[:, D:2 * D] = (dua * cx).astype(BF16)
        dproj_ref[:, 2 * D:3 * D] = (dua * cc).astype(BF16)

        dproj_ref[:, 4 * D:5 * D] = (dyb * h * dgel).astype(BF16)
        a_buf[0:ts, :] = a
        sa = a_buf[pl.ds(1, ts), :]
        sb = dyb * gel
        sub = rows % 8
        for d in (1, 2, 4):
            m = sub < 8 - d
            sb = jnp.where(m, sb + sa * pltpu.roll(sb, ts - d, axis=0), sb)
            sa = jnp.where(m, sa * pltpu.roll(sa, ts - d, axis=0), sa)
        ca_buf[...] = sa
        cb_buf[...] = sb

        def grp(k, carry):
            off = pl.multiple_of((ts // 8 - 1 - k) * 8, 8)
            l_buf[pl.ds(off, 8), :] = ca_buf[pl.ds(off, 8), :] * carry + cb_buf[pl.ds(off, 8), :]
            return jnp.broadcast_to(l_buf[pl.ds(off, 1), :], (8, D))

        lcarry[...] = lax.fori_loop(0, ts // 8, grp, lcarry[...])
        a_buf[ts:ts + 8, :] = a_buf[0:8, :]
        lam = l_buf[...]

        h_buf[0:8, :] = jnp.where(first_tile, 0.0, hh_ref[...])
        h_buf[8:ts + 8, :] = h
        da = lam * h_buf[pl.ds(7, ts), :]
        dmult = jnp.where(row0, 0.0, lam * (ig * u))
        di = lam * mult * u
        du = lam * mult * ig
        dlog_a = da * a - dmult * (a * a) / mult
        lam_p = vrow(V_LAM)
        dr = dlog_a * (LRU_C * _log_sigmoid(lam_p))
        sm_ref[G_LAM:G_LAM + 1, :] += _rowsum(dlog_a * r) * (LRU_C * jax.nn.sigmoid(-lam_p))
        dpa = dr * r * (1.0 - r)
        dpx = di * ig * (1.0 - ig)
        acc(G_BA, dpa)
        acc(G_BX, dpx)
        dpab = dpa.astype(BF16)
        dpxb = dpx.astype(BF16)
        ub = u.astype(BF16)
        back = []
        for hd in range(HEADS):
            cols = slice(hd * HB, (hd + 1) * HB)
            back.append(_dot_nt(dpab[:, cols], wa_ref[hd]) + _dot_nt(dpxb[:, cols], wx_ref[hd]))
            dwa_ref[hd] += _dot_tn(ub[:, cols], dpab[:, cols])
            dwx_ref[hd] += _dot_tn(ub[:, cols], dpxb[:, cols])
        du = du + jnp.concatenate(back, axis=1)

        acc(G_CBB, du)
        du_buf[0:ts, :] = du
        du1 = du_buf[pl.ds(1, ts), :]
        du2 = du_buf[pl.ds(2, ts), :]
        du3 = du_buf[pl.ds(3, ts), :]
        dproj_ref[:, 3 * D:4 * D] = (vrow(V_WB3) * du + vrow(V_WB2) * du1 + vrow(V_WB1) * du2 + vrow(V_WB0) * du3).astype(BF16)
        acc(G_WB3, rx * du)
        acc(G_WB2, rx * du1)
        acc(G_WB1, rx * du2)
        acc(G_WB0, rx * du3)
        du_buf[ts:ts + 8, :] = du_buf[0:8, :]

    rev = lambda i: (nt - 1 - i, 0)
    halo8 = lambda i: (jnp.maximum((nt - 1 - i) * (ts // 8) - 1, 0), 0)
    const2 = lambda i: (0, 0)
    const3 = lambda i: (0, 0, 0)
    return pl.pallas_call(
        body, grid=(nt,),
        out_shape=(jax.ShapeDtypeStruct((s, D_IN), BF16), jax.ShapeDtypeStruct((s, D), BF16),
                   jax.ShapeDtypeStruct((N_SMALL, D), F32), jax.ShapeDtypeStruct((HEADS, HB, HB), F32),
                   jax.ShapeDtypeStruct((HEADS, HB, HB), F32)),
        in_specs=[pl.BlockSpec((ts, D), rev), pl.BlockSpec((ts, D), rev), pl.BlockSpec((ts, D_IN), rev),
                  pl.BlockSpec((ts, D), rev), pl.BlockSpec((ts, D), rev), pl.BlockSpec((8, D), halo8),
                  pl.BlockSpec((16, D), lambda i: (jnp.maximum((nt - 1 - i) * (ts // 16) - 1, 0), 1)),
                  pl.BlockSpec((16, D), lambda i: (jnp.maximum((nt - 1 - i) * (ts // 16) - 1, 0), 2)),
                  _VMEM, _VMEM, _VMEM, _VMEM],
        out_specs=[pl.BlockSpec((ts, D_IN), rev), pl.BlockSpec((ts, D), rev), pl.BlockSpec((N_SMALL, D), const2),
                   pl.BlockSpec((HEADS, HB, HB), const3), pl.BlockSpec((HEADS, HB, HB), const3)],
        scratch_shapes=[pltpu.VMEM((ts + 8, D), F32)] * 5 + [pltpu.VMEM((ts, D), F32)] * 3 + [pltpu.VMEM((8, D), F32)],
        compiler_params=_cparams(), name="bwd_mix")(dx1, z1, proj, u, h, h, proj, proj, vecs, w_rga, w_rgx, w_out)


def _bwd_in(dproj, x, dx1, vecs, w_in_g, ts):
    s = x.shape[0]

    def body(dp_ref, x_ref, dx1_ref, v_ref, w_ref, gx_ref, sm_ref):
        @pl.when(pl.program_id(0) == 0)
        def _():
            sm_ref[...] = jnp.zeros((N_SMALL, D), F32)

        def vrow(j):
            return v_ref[j:j + 1, :]

        dh1 = _dot_nt(dp_ref[:, 0:C_IN], w_ref[0])
        for k in range(1, N_CHIPS):
            dh1 += _dot_nt(dp_ref[:, k * C_IN:(k + 1) * C_IN], w_ref[k])
        xh, rstd = _rms(x_ref[...])
        sm_ref[G_SH1:G_SH1 + 1, :] += _rowsum(dh1)
        sm_ref[G_SC1:G_SC1 + 1, :] += _rowsum(dh1 * (xh * vrow(V_GMIX)))
        dn1 = dh1 * (1.0 + vrow(V_SC1))
        sm_ref[G_GMIX:G_GMIX + 1, :] += _rowsum(dn1 * xh)
        gx_ref[...] = dx1_ref[...] + _rms_bwd(dn1 * vrow(V_GMIX), xh, rstd)

    row = lambda i: (i, 0)
    return pl.pallas_call(
        body, grid=(s // ts,),
        out_shape=(jax.ShapeDtypeStruct((s, D), F32), jax.ShapeDtypeStruct((N_SMALL, D), F32)),
        in_specs=[pl.BlockSpec((ts, D_IN), row), pl.BlockSpec((ts, D), row), pl.BlockSpec((ts, D), row), _VMEM, _VMEM],
        out_specs=[pl.BlockSpec((ts, D), row), pl.BlockSpec((N_SMALL, D), lambda i: (0, 0))],
        compiler_params=_cparams(), name="bwd_in")(dproj, x, dx1, vecs, w_in_g)


def _grad_w(a, b, n_col_blocks, ts, name):
    s, m = a.shape
    tn = b.shape[1] // n_col_blocks
    n_steps = s // ts

    def body(a_ref, b_ref, o_ref, acc_ref):
        k = pl.program_id(1)

        @pl.when(k == 0)
        def _():
            acc_ref[...] = jnp.zeros((m, tn), F32)

        acc_ref[...] += _dot_tn(a_ref[...], b_ref[...])

        @pl.when(k == n_steps - 1)
        def _():
            o_ref[...] = acc_ref[...].astype(BF16)

    return pl.pallas_call(
        body, grid=(n_col_blocks, n_steps),
        out_shape=jax.ShapeDtypeStruct((n_col_blocks, m, tn), BF16),
        in_specs=[pl.BlockSpec((ts, m), lambda n, k: (k, 0)), pl.BlockSpec((ts, tn), lambda n, k: (k, n))],
        out_specs=pl.BlockSpec((None, m, tn), lambda n, k: (n, 0, 0)),
        scratch_shapes=[pltpu.VMEM((m, tn), F32)],
        compiler_params=_cparams(2), name=name)(a, b)


def _ada_fwd(c_all, w_ada, b_ada):
    n = w_ada.shape[1]

    def body(c_ref, w_ref, b_ref, o_ref, ca_ref):
        c = c_ref[...]
        ca = c * jax.nn.sigmoid(c)
        ca_ref[...] = ca
        o_ref[...] = jnp.dot(ca, w_ref[...], preferred_element_type=F32, precision=lax.Precision.HIGHEST) + b_ref[...]

    return pl.pallas_call(
        body, out_shape=(jax.ShapeDtypeStruct((N_DEV, n), F32), jax.ShapeDtypeStruct((N_DEV, D), F32)),
        in_specs=[_VMEM] * 3, out_specs=[_VMEM] * 2, compiler_params=_cparams(0), name="ada_fwd")(c_all, w_ada, b_ada)


def _ada_bwd(c_act, dmod):
    n = dmod.shape[1]

    def body(c_ref, d_ref, o_ref):
        o_ref[...] = lax.dot_general(c_ref[...], d_ref[...], (((0,), (0,)), ((), ())), preferred_element_type=F32,
                                     precision=lax.Precision.HIGHEST)

    return pl.pallas_call(
        body, out_shape=jax.ShapeDtypeStruct((D, n), F32), in_specs=[_VMEM] * 2, out_specs=_VMEM,
        compiler_params=_cparams(0), name="ada_bwd")(c_act, dmod)


def _sum_small(parts):
    def body(p_ref, o_ref, d_ref):
        tot = None
        for dev in range(N_DEV):
            mine = p_ref[dev, 0] + p_ref[dev, 1] + p_ref[dev, 2]
            d_ref[dev] = mine[0:8, :]
            tot = mine if tot is None else tot + mine
        o_ref[...] = tot

    return pl.pallas_call(
        body, out_shape=(jax.ShapeDtypeStruct((N_SMALL, D), F32), jax.ShapeDtypeStruct((N_DEV, 8, D), F32)),
        in_specs=[_VMEM], out_specs=[_VMEM] * 2, compiler_params=_cparams(0), name="sum_small")(parts)


def _adamw(w, g, m, v, name):
    rows, cols = w.shape
    tr = 128 if rows % 128 == 0 else (64 if rows % 64 == 0 else rows)

    def body(w_ref, g_ref, m_ref, v_ref, d_ref, nm_ref, nv_ref):
        g_ = g_ref[...]
        m_ = ADAM_B1 * m_ref[...] + (1.0 - ADAM_B1) * g_
        v_ = ADAM_B2 * v_ref[...] + (1.0 - ADAM_B2) * (g_ * g_)
        nm_ref[...] = m_
        nv_ref[...] = v_
        m_hat = m_ / (1.0 - ADAM_B1 ** ADAM_STEP)
        v_hat = v_ / (1.0 - ADAM_B2 ** ADAM_STEP)
        d_ref[...] = -ADAM_LR * (m_hat / (jnp.sqrt(v_hat) + ADAM_EPS) + ADAM_WD * w_ref[...])

    spec = pl.BlockSpec((tr, cols), lambda i: (i, 0))
    return pl.pallas_call(
        body, grid=(rows // tr,), out_shape=(jax.ShapeDtypeStruct((rows, cols), F32),) * 3,
        in_specs=[spec] * 4, out_specs=[spec] * 3, compiler_params=_cparams(), name=name)(w, g, m, v)


def _add_halves(g, recv, c_idx, name):
    n, _, r2, cols = g.shape

    def body(c_ref, g_ref, r_ref, o_ref):
        o_ref[...] = (g_ref[...].astype(F32) + r_ref[...].astype(F32)).astype(BF16)

    return pl.pallas_call(
        body,
        grid_spec=pltpu.PrefetchScalarGridSpec(
            num_scalar_prefetch=1, grid=(n,),
            in_specs=[pl.BlockSpec((None, None, r2, cols), lambda k, c: (k, c[0], 0, 0)),
                      pl.BlockSpec((None, r2, cols), lambda k, c: (k, 0, 0))],
            out_specs=pl.BlockSpec((None, r2, cols), lambda k, c: (k, 0, 0))),
        out_shape=jax.ShapeDtypeStruct((n, r2, cols), BF16), compiler_params=_cparams(), name=name)(c_idx, g, recv)


def _sum_chips(parts, name):
    n, r2, cols = parts.shape
    tr = r2 // 2 if (r2 // 2) % 16 == 0 else r2

    def body(p_ref, o_ref):
        o_ref[...] = ((p_ref[0].astype(F32) + p_ref[1].astype(F32)) + p_ref[2].astype(F32)) + p_ref[3].astype(F32)

    return pl.pallas_call(
        body, grid=(r2 // tr,), out_shape=jax.ShapeDtypeStruct((r2, cols), F32),
        in_specs=[pl.BlockSpec((n, tr, cols), lambda i: (0, i, 0))], out_specs=pl.BlockSpec((tr, cols), lambda i: (i, 0)),
        compiler_params=_cparams(), name=name)(parts)


def _place():
    x, y, c = lax.axis_index("x"), lax.axis_index("y"), lax.axis_index("c")
    return x, y, c, 2 * x + y


def _flip(v, bit):
    return 1 - v if bit else v


def _allgather8(v, name):
    r, n = v.shape

    def body(v_ref, out_ref, send_sems, recv_sems, local_sem):
        x, y, c, _ = _place()
        me = 4 * x + 2 * y + c
        mine = pltpu.make_async_copy(v_ref, out_ref.at[me], local_sem)
        mine.start()
        sends = []
        for rel in range(1, N_DEV):
            peer = (_flip(x, rel & 4), _flip(y, rel & 2), _flip(c, rel & 1))
            cp = pltpu.make_async_remote_copy(v_ref, out_ref.at[me], send_sems.at[rel - 1], recv_sems.at[rel - 1],
                                              device_id=peer, device_id_type=MESH)
            cp.start()
            sends.append(cp)
        for rel in range(1, N_DEV):
            peer = (_flip(x, rel & 4), _flip(y, rel & 2), _flip(c, rel & 1))
            peer_idx = 4 * peer[0] + 2 * peer[1] + peer[2]
            pltpu.make_async_remote_copy(v_ref, out_ref.at[peer_idx], send_sems.at[rel - 1], recv_sems.at[rel - 1],
                                         device_id=peer, device_id_type=MESH).wait_recv()
        for cp in sends:
            cp.wait_send()
        mine.wait()

    return pl.pallas_call(
        body, out_shape=jax.ShapeDtypeStruct((N_DEV, r, n), F32), in_specs=[_VMEM], out_specs=_VMEM,
        scratch_shapes=[pltpu.SemaphoreType.DMA((N_DEV - 1,)), pltpu.SemaphoreType.DMA((N_DEV - 1,)), pltpu.SemaphoreType.DMA(())],
        name=name)(v)


def _gather_weights(shards):
    nw = len(shards)

    def body(*refs):
        w_refs, out_refs = refs[:nw], refs[nw:2 * nw]
        send_sems, recv_sems, local_sems = refs[2 * nw:]
        x, y, c, p = _place()
        sibling = (x, y, 1 - c)
        started = []
        for w in range(nw):
            loc = pltpu.make_async_copy(w_refs[w], out_refs[w].at[p], local_sems.at[w])
            loc.start()
            started.append(loc)
        sends = []
        for j in range(1, N_CHIPS):
            peer = (_flip(x, j & 2), _flip(y, j & 1), c)
            for w in range(nw):
                cp = pltpu.make_async_remote_copy(w_refs[w].at[c], out_refs[w].at[p, c], send_sems.at[w * 6 + j - 1],
                                                  recv_sems.at[w * 6 + j - 1], device_id=peer, device_id_type=MESH)
                cp.start()
                sends.append(cp)
        for j in range(1, N_CHIPS):
            peer = (_flip(x, j & 2), _flip(y, j & 1), c)
            q = 2 * peer[0] + peer[1]
            for w in range(nw):
                pltpu.make_async_remote_copy(w_refs[w].at[c], out_refs[w].at[q, c], send_sems.at[w * 6 + j - 1],
                                             recv_sems.at[w * 6 + j - 1], device_id=peer, device_id_type=MESH).wait_recv()
                cp = pltpu.make_async_remote_copy(out_refs[w].at[q, c], out_refs[w].at[q, c], send_sems.at[w * 6 + 2 + j],
                                                  recv_sems.at[w * 6 + 2 + j], device_id=sibling, device_id_type=MESH)
                cp.start()
                sends.append(cp)
        for j in range(1, N_CHIPS):
            q = 2 * _flip(x, j & 2) + _flip(y, j & 1)
            for w in range(nw):
                pltpu.make_async_remote_copy(out_refs[w].at[q, 1 - c], out_refs[w].at[q, 1 - c], send_sems.at[w * 6 + 2 + j],
                                             recv_sems.at[w * 6 + 2 + j], device_id=sibling, device_id_type=MESH).wait_recv()
        for cp in sends:
            cp.wait_send()
        for loc in started:
            loc.wait()

    return pl.pallas_call(
        body, out_shape=tuple(jax.ShapeDtypeStruct((N_CHIPS,) + s.shape, s.dtype) for s in shards),
        in_specs=[_ANY] * nw, out_specs=[_ANY] * nw,
        scratch_shapes=[pltpu.SemaphoreType.DMA((6 * nw,)), pltpu.SemaphoreType.DMA((6 * nw,)), pltpu.SemaphoreType.DMA((nw,))],
        name="gather_weights")(*shards)


def _swap_halves(grads):
    nw = len(grads)

    def body(*refs):
        g_refs, out_refs = refs[:nw], refs[nw:2 * nw]
        send_sems, recv_sems = refs[2 * nw:]
        x, y, c, _ = _place()
        sibling = (x, y, 1 - c)
        sends = []
        for w in range(nw):
            for k in range(N_CHIPS):
                cp = pltpu.make_async_remote_copy(g_refs[w].at[k, 1 - c], out_refs[w].at[k], send_sems.at[w * N_CHIPS + k],
                                                  recv_sems.at[w * N_CHIPS + k], device_id=sibling, device_id_type=MESH)
                cp.start()
                sends.append(cp)
        for cp in sends:
            cp.wait_recv()
        for cp in sends:
            cp.wait_send()

    return pl.pallas_call(
        body, out_shape=tuple(jax.ShapeDtypeStruct((N_CHIPS,) + g.shape[2:], g.dtype) for g in grads),
        in_specs=[_ANY] * nw, out_specs=[_ANY] * nw,
        scratch_shapes=[pltpu.SemaphoreType.DMA((N_CHIPS * nw,)), pltpu.SemaphoreType.DMA((N_CHIPS * nw,))],
        name="swap_halves")(*grads)


def _scatter_chips(parts):
    nw = len(parts)

    def body(*refs):
        p_refs, out_refs = refs[:nw], refs[nw:2 * nw]
        send_sems, recv_sems, local_sems = refs[2 * nw:]
        x, y, c, p = _place()
        started = []
        for w in range(nw):
            loc = pltpu.make_async_copy(p_refs[w].at[p], out_refs[w].at[p], local_sems.at[w])
            loc.start()
            started.append(loc)
        sends = []
        for j in range(1, N_CHIPS):
            peer = (_flip(x, j & 2), _flip(y, j & 1), c)
            q = 2 * peer[0] + peer[1]
            for w in range(nw):
                cp = pltpu.make_async_remote_copy(p_refs[w].at[q], out_refs[w].at[p], send_sems.at[w * 3 + j - 1],
                                                  recv_sems.at[w * 3 + j - 1], device_id=peer, device_id_type=MESH)
                cp.start()
                sends.append(cp)
        for j in range(1, N_CHIPS):
            peer = (_flip(x, j & 2), _flip(y, j & 1), c)
            q = 2 * peer[0] + peer[1]
            for w in range(nw):
                pltpu.make_async_remote_copy(p_refs[w].at[q], out_refs[w].at[q], send_sems.at[w * 3 + j - 1],
                                             recv_sems.at[w * 3 + j - 1], device_id=peer, device_id_type=MESH).wait_recv()
        for cp in sends:
            cp.wait_send()
        for loc in started:
            loc.wait()

    return pl.pallas_call(
        body, out_shape=tuple(jax.ShapeDtypeStruct(s.shape, s.dtype) for s in parts),
        in_specs=[_ANY] * nw, out_specs=[_ANY] * nw,
        scratch_shapes=[pltpu.SemaphoreType.DMA((3 * nw,)), pltpu.SemaphoreType.DMA((3 * nw,)), pltpu.SemaphoreType.DMA((nw,))],
        name="scatter_chips")(*parts)


def _share_halves(halves):
    nw = len(halves)

    def body(*refs):
        h_refs, out_refs = refs[:nw], refs[nw:2 * nw]
        send_sems, recv_sems, local_sems = refs[2 * nw:]
        x, y, c, _ = _place()
        sibling = (x, y, 1 - c)
        started, sends = [], []
        for w in range(nw):
            loc = pltpu.make_async_copy(h_refs[w], out_refs[w].at[c], local_sems.at[w])
            loc.start()
            started.append(loc)
            cp = pltpu.make_async_remote_copy(h_refs[w], out_refs[w].at[c], send_sems.at[w], recv_sems.at[w],
                                              device_id=sibling, device_id_type=MESH)
            cp.start()
            sends.append(cp)
        for w in range(nw):
            pltpu.make_async_remote_copy(h_refs[w], out_refs[w].at[1 - c], send_sems.at[w], recv_sems.at[w],
                                         device_id=sibling, device_id_type=MESH).wait_recv()
        for cp in sends:
            cp.wait_send()
        for loc in started:
            loc.wait()

    return pl.pallas_call(
        body, out_shape=tuple(jax.ShapeDtypeStruct((2,) + s.shape, s.dtype) for s in halves),
        in_specs=[_ANY] * nw, out_specs=[_ANY] * nw,
        scratch_shapes=[pltpu.SemaphoreType.DMA((nw,)), pltpu.SemaphoreType.DMA((nw,)), pltpu.SemaphoreType.DMA((nw,))],
        name="share_halves")(*halves)


def _local_step(x, target, vecs, w_in_g, w_rga, w_rgx, w_out, w_gu_g, w_dn, ts_mm=512, ts_mix=256):
    h1, proj = _fwd_in(x, vecs, w_in_g, ts_mm)
    x1, merged, z1, u, h = _fwd_mix(proj, x, vecs, w_rga, w_rgx, w_out, ts_mix)
    dx1, h2, act, dz2, dgu, sm_ffn = _ffn_loss(x1, target, vecs, w_gu_g, w_dn, ts_mix)
    dproj, dz1, sm_mix, dw_rga, dw_rgx = _bwd_mix(dx1, z1, proj, u, h, vecs, w_rga, w_rgx, w_out, ts_mix)
    grad_x, sm_in = _bwd_in(dproj, x, dx1, vecs, w_in_g, ts_mm)
    g_in = _grad_w(h1, dproj, N_CHIPS, ts_mm, "grad_w_in")
    g_out = _grad_w(merged, dz1, 1, ts_mm, "grad_w_out")
    g_gu = _grad_w(h2, dgu, N_CHIPS, ts_mm, "grad_w_gate_up")
    g_dn = _grad_w(act, dz2, 1, ts_mm, "grad_w_down")
    return grad_x, jnp.stack([sm_ffn, sm_mix, sm_in]), (g_in, g_out, g_gu, g_dn, dw_rga, dw_rgx)


def _halved(a):
    n, r, cols = a.shape
    return a.reshape(n, 2, r // 2, cols)


def kernel(x, c, w_ada, b_ada, g_norm_mix, w_in, conv_a_w, conv_b_w, conv_b_bias, w_rg_a, b_rg_a, w_rg_x, b_rg_x, lru_lambda, w_out, g_norm_ffn, w_gate_up, w_down, g_norm_final, loss_target, m_w_ada, m_b_ada, m_g_norm_mix, m_w_in, m_conv_a_w, m_conv_b_w, m_conv_b_bias, m_w_rg_a, m_b_rg_a, m_w_rg_x, m_b_rg_x, m_lru_lambda, m_w_out, m_g_norm_ffn, m_w_gate_up, m_w_down, m_g_norm_final, v_w_ada, v_b_ada, v_g_norm_mix, v_w_in, v_conv_a_w, v_conv_b_w, v_conv_b_bias, v_w_rg_a, v_b_rg_a, v_w_rg_x, v_b_rg_x, v_lru_lambda, v_w_out, v_g_norm_ffn, v_w_gate_up, v_w_down, v_g_norm_final):
    xi, yi, ci = lax.axis_index("x"), lax.axis_index("y"), lax.axis_index("c")
    chip = 2 * xi + yi
    me = 2 * chip + ci
    n_ada = w_ada.shape[2]

    conv_sh = jnp.concatenate([conv_a_w[0], conv_b_w[0]], axis=0)
    pay = jnp.concatenate([c, jnp.pad(conv_sh, ((0, 0), (0, D - conv_sh.shape[1])))], axis=0)
    got = _allgather8(pay, "gather_c_conv")
    c_all = got[:, 0, :]
    conv_full = jnp.concatenate([got[2 * k, 1:8, :D // N_CHIPS] for k in range(N_CHIPS)], axis=1)

    mod_part, c_act = _ada_fwd(c_all, w_ada[0], lax.dynamic_slice_in_dim(b_ada, chip * n_ada, n_ada, axis=1))
    mod_all = _allgather8(mod_part, "gather_mod")
    mod_mine = jnp.concatenate([lax.dynamic_index_in_dim(mod_all[2 * k], me, axis=0, keepdims=False)
                                for k in range(N_CHIPS)], axis=0)
    sh1, sc1, gt1, sh2, sc2, gt2 = jnp.split(mod_mine, 6)

    vecs = jnp.stack([g_norm_mix[0], sc1, sh1, gt1, g_norm_ffn[0], sc2, sh2, gt2, g_norm_final, conv_b_bias[0],
                      b_rg_a[0], b_rg_x[0], lru_lambda[0]] + [conv_full[k] for k in range(7)]
                     + [jnp.zeros((D,), F32)] * (N_VEC - 20))

    def rg_shard(w):
        return w[0].astype(BF16).reshape(2, HEADS * HB // N_CHIPS // 2, HB)

    shards = (w_in[0].astype(BF16).reshape(2, D // 2, C_IN), w_out[0].astype(BF16).reshape(2, D // N_CHIPS // 2, D),
              w_gate_up[0].astype(BF16).reshape(2, D // 2, C_GU), w_down[0].astype(BF16).reshape(2, D_FF // N_CHIPS // 2, D),
              rg_shard(w_rg_a), rg_shard(w_rg_x))
    wg_in, wg_out, wg_gu, wg_dn, wg_rga, wg_rgx = _gather_weights(shards)

    def rg_full(wg):
        return wg.reshape(N_CHIPS, HEADS, HB // N_CHIPS, HB).transpose(1, 0, 2, 3).reshape(HEADS, HB, HB)

    grad_x, sm3, (g_in, g_out, g_gu, g_dn, dw_rga, dw_rgx) = _local_step(
        x[0], loss_target[0], vecs, wg_in.reshape(N_CHIPS, D, C_IN), rg_full(wg_rga), rg_full(wg_rgx),
        wg_out.reshape(D, D), wg_gu.reshape(N_CHIPS, D, C_GU), wg_dn.reshape(D_FF, D))

    small, per_dev = _sum_small(_allgather8(sm3.reshape(3 * N_SMALL, D), "gather_small").reshape(N_DEV, 3, N_SMALL, D))
    dmod_all = per_dev[:, 0:6, :].reshape(N_DEV, 6 * D)
    grad_w_ada = _ada_bwd(c_act, lax.dynamic_slice_in_dim(dmod_all, chip * n_ada, n_ada, axis=1))

    def rg_chunks(dw):
        return dw.reshape(HEADS, N_CHIPS, HB // N_CHIPS, HB).transpose(1, 0, 2, 3).reshape(N_CHIPS, HB, HB).astype(BF16)

    grads = (_halved(g_in), _halved(g_out.reshape(N_CHIPS, D // N_CHIPS, D)), _halved(g_gu),
             _halved(g_dn.reshape(N_CHIPS, D_FF // N_CHIPS, D)), _halved(rg_chunks(dw_rga)), _halved(rg_chunks(dw_rgx)))
    from_sibling = _swap_halves(grads)
    c_arr = jnp.reshape(ci, (1,)).astype(jnp.int32)
    pair_sums = tuple(_add_halves(g, r, c_arr, "add_halves_%d" % k) for k, (g, r) in enumerate(zip(grads, from_sibling)))
    by_chip = _scatter_chips(pair_sums)
    halves = tuple(_sum_chips(b, "sum_chips_%d" % k) for k, b in enumerate(by_chip))
    full = _share_halves(halves)
    gw_in, gw_out, gw_gu, gw_dn, gw_rga, gw_rgx = (f.reshape(2 * f.shape[1], f.shape[2]) for f in full)

    def step(name, w, g, m, v):
        shape = w.shape
        two_d = (-1, shape[-1])
        d, nm, nv = _adamw(w.reshape(two_d), g.reshape(two_d), m.reshape(two_d), v.reshape(two_d), "adamw_" + name)
        return g.reshape(shape), d.reshape(shape), nm.reshape(shape), nv.reshape(shape)

    def shard_cols(row_block):
        return lax.dynamic_slice_in_dim(row_block, chip * (D // N_CHIPS), D // N_CHIPS, axis=1)

    grad_b_ada = small[0:6].reshape(1, 6 * D)
    res = {
        "w_ada": step("w_ada", w_ada, grad_w_ada[None], m_w_ada, v_w_ada),
        "b_ada": step("b_ada", b_ada.reshape(6, D), grad_b_ada.reshape(6, D), m_b_ada.reshape(6, D), v_b_ada.reshape(6, D)),
        "g_norm_mix": step("g_norm_mix", g_norm_mix, small[G_GMIX:G_GMIX + 1], m_g_norm_mix, v_g_norm_mix),
        "w_in": step("w_in", w_in, gw_in[None], m_w_in, v_w_in),
        "conv_a_w": step("conv_a_w", conv_a_w, shard_cols(small[G_WA0:G_WA0 + 3])[None], m_conv_a_w, v_conv_a_w),
        "conv_b_w": step("conv_b_w", conv_b_w, shard_cols(small[G_WB0:G_WB0 + 4])[None], m_conv_b_w, v_conv_b_w),
        "conv_b_bias": step("conv_b_bias", conv_b_bias, small[G_CBB:G_CBB + 1], m_conv_b_bias, v_conv_b_bias),
        "w_rg_a": step("w_rg_a", w_rg_a, gw_rga.reshape(w_rg_a.shape), m_w_rg_a, v_w_rg_a),
        "b_rg_a": step("b_rg_a", b_rg_a, small[G_BA:G_BA + 1], m_b_rg_a, v_b_rg_a),
        "w_rg_x": step("w_rg_x", w_rg_x, gw_rgx.reshape(w_rg_x.shape), m_w_rg_x, v_w_rg_x),
        "b_rg_x": step("b_rg_x", b_rg_x, small[G_BX:G_BX + 1], m_b_rg_x, v_b_rg_x),
        "lru_lambda": step("lru_lambda", lru_lambda, small[G_LAM:G_LAM + 1], m_lru_lambda, v_lru_lambda),
        "w_out": step("w_out", w_out, gw_out[None], m_w_out, v_w_out),
        "g_norm_ffn": step("g_norm_ffn", g_norm_ffn, small[G_GFFN:G_GFFN + 1], m_g_norm_ffn, v_g_norm_ffn),
        "w_gate_up": step("w_gate_up", w_gate_up, gw_gu[None], m_w_gate_up, v_w_gate_up),
        "w_down": step("w_down", w_down, gw_dn[None], m_w_down, v_w_down),
        "g_norm_final": step("g_norm_final", g_norm_final.reshape(1, D), small[G_GFIN:G_GFIN + 1], m_g_norm_final.reshape(1, D),
                             v_g_norm_final.reshape(1, D)),
    }
    res["b_ada"] = tuple(a.reshape(1, 6 * D) for a in res["b_ada"])
    res["g_norm_final"] = tuple(a.reshape(D) for a in res["g_norm_final"])
    names = ["w_ada", "b_ada", "g_norm_mix", "w_in", "conv_a_w", "conv_b_w", "conv_b_bias", "w_rg_a", "b_rg_a", "w_rg_x",
             "b_rg_x", "lru_lambda", "w_out", "g_norm_ffn", "w_gate_up", "w_down", "g_norm_final"]
    loss = jnp.sum(small[G_LOSS])
    return (loss, grad_x[None], *[res[n][0] for n in names], *[res[n][1] for n in names],
            *[res[n][2] for n in names], *[res[n][3] for n in names])
```

```python
import functools

import jax
import jax.numpy as jnp
from jax import lax
from jax.experimental import pallas as pl
from jax.experimental.pallas import tpu as pltpu

F32 = jnp.float32
BF16 = jnp.bfloat16
MESH = pl.DeviceIdType.MESH

D = 1024
N_CHIPS = 4
N_DEV = 8
D_IN = 7 * D
C_IN = D_IN // N_CHIPS
D_FF = 2816
C_GU = 2 * D_FF // N_CHIPS
HEADS = 4
HB = D // HEADS
EPS = 1e-6
LRU_C = 8.0
ADAM_LR, ADAM_B1, ADAM_B2, ADAM_EPS, ADAM_WD, ADAM_STEP = 0.001, 0.9, 0.999, 1e-08, 0.01, 10
VMEM_LIMIT = 56 << 20

(V_SH1, V_SC1, V_GT1, V_SH2, V_SC2, V_GT2, V_GMIX, V_GFFN, V_GFIN, V_CBB, V_BA, V_BX, V_LAM,
 V_WA0, V_WA1, V_WA2, V_WB0, V_WB1, V_WB2, V_WB3) = range(20)
N_VEC = 24
(G_SH1, G_SC1, G_GT1, G_SH2, G_SC2, G_GT2, G_GMIX, G_CBB, G_BA, G_BX, G_LAM, G_GFFN, G_GFIN,
 G_WA0, G_WA1, G_WA2, G_WB0, G_WB1, G_WB2, G_WB3, G_LOSS) = range(21)
N_SMALL = 24

_VMEM = pl.BlockSpec(memory_space=pltpu.VMEM)
_ANY = pl.BlockSpec(memory_space=pl.ANY)


def _cparams(n_grid=1):
    return pltpu.CompilerParams(dimension_semantics=("arbitrary",) * n_grid, vmem_limit_bytes=VMEM_LIMIT)


def _rms(x):
    rstd = lax.rsqrt(jnp.mean(x * x, axis=-1, keepdims=True) + EPS)
    return x * rstd, rstd


def _rms_bwd(dxhat, xhat, rstd):
    return rstd * (dxhat - xhat * jnp.mean(dxhat * xhat, axis=-1, keepdims=True))


def _rowsum(v):
    return jnp.sum(v, axis=0, keepdims=True)


def _dot(a, b):
    return jnp.dot(a, b, preferred_element_type=F32)


def _dot_nt(a, b):
    return lax.dot_general(a, b, (((1,), (1,)), ((), ())), preferred_element_type=F32)


def _dot_tn(a, b):
    return lax.dot_general(a, b, (((0,), (0,)), ((), ())), preferred_element_type=F32)


def _gelu(x):
    k, c = 0.7978845608028654, 0.044715
    t = jnp.tanh(k * (x + c * x * x * x))
    return 0.5 * x * (1.0 + t), 0.5 * (1.0 + t) + 0.5 * x * (1.0 - t * t) * k * (1.0 + 3.0 * c * x * x)


def _log_sigmoid(lam):
    return jnp.minimum(lam, 0.0) - jnp.log1p(jnp.exp(-jnp.abs(lam)))


def _lru_gates(u, wa_ref, wx_ref, v_ref, row0):
    ub = u.astype(BF16)
    pre_a = jnp.concatenate([_dot(ub[:, h * HB:(h + 1) * HB], wa_ref[h]) for h in range(HEADS)], axis=1)
    pre_x = jnp.concatenate([_dot(ub[:, h * HB:(h + 1) * HB], wx_ref[h]) for h in range(HEADS)], axis=1)
    r = jax.nn.sigmoid(pre_a + v_ref[V_BA:V_BA + 1, :])
    ig = jax.nn.sigmoid(pre_x + v_ref[V_BX:V_BX + 1, :])
    log_a = LRU_C * r * _log_sigmoid(v_ref[V_LAM:V_LAM + 1, :])
    a = jnp.exp(log_a)
    x2 = 2.0 * log_a
    m2 = jnp.where(x2 > -0.03, -x2 * (1.0 + x2 * (0.5 + x2 * (1.0 / 6.0 + x2 * (1.0 / 24.0)))), 1.0 - a * a)
    mult = jnp.where(row0, 1.0, jnp.sqrt(jnp.maximum(m2, 0.0)))
    return r, ig, a, mult


def _fwd_in(x, vecs, w_in_g, ts):
    s = x.shape[0]

    def body(x_ref, v_ref, w_ref, h1_ref, proj_ref):
        xhat, _ = _rms(x_ref[...])
        h = xhat * v_ref[V_GMIX:V_GMIX + 1, :] * (1.0 + v_ref[V_SC1:V_SC1 + 1, :]) + v_ref[V_SH1:V_SH1 + 1, :]
        hb = h.astype(BF16)
        h1_ref[...] = hb
        for k in range(N_CHIPS):
            proj_ref[:, k * C_IN:(k + 1) * C_IN] = _dot(hb, w_ref[k]).astype(BF16)

    return pl.pallas_call(
        body, grid=(s // ts,),
        out_shape=(jax.ShapeDtypeStruct((s, D), BF16), jax.ShapeDtypeStruct((s, D_IN), BF16)),
        in_specs=[pl.BlockSpec((ts, D), lambda i: (i, 0)), _VMEM, _VMEM],
        out_specs=[pl.BlockSpec((ts, D), lambda i: (i, 0)), pl.BlockSpec((ts, D_IN), lambda i: (i, 0))],
        compiler_params=_cparams(), name="fwd_in")(x, vecs, w_in_g)


def _fwd_mix(proj, x, vecs, w_rga, w_rgx, w_out, ts):
    s = x.shape[0]

    def body(proj_ref, x_ref, v_ref, wa_ref, wx_ref, wo_ref, x1_ref, mg_ref, z1_ref, u_ref, h_ref,
             ua_buf, rx_buf, a_buf, b_buf, hcarry):
        i = pl.program_id(0)

        @pl.when(i == 0)
        def _():
            ua_buf[0:8, :] = jnp.zeros((8, D), F32)
            rx_buf[0:8, :] = jnp.zeros((8, D), F32)
            hcarry[...] = jnp.zeros((8, D), F32)

        def seg(j):
            return proj_ref[:, j * D:(j + 1) * D].astype(F32)

        def vrow(j):
            return v_ref[j:j + 1, :]

        cb, cc, cx, rx, rg, ga, gb = (seg(j) for j in range(7))
        ua = cc * cx
        ua_buf[8:ts + 8, :] = ua
        rx_buf[8:ts + 8, :] = rx
        va = vrow(V_WA2) * ua + vrow(V_WA1) * ua_buf[pl.ds(7, ts), :] + vrow(V_WA0) * ua_buf[pl.ds(6, ts), :]
        u = (vrow(V_WB3) * rx + vrow(V_WB2) * rx_buf[pl.ds(7, ts), :] + vrow(V_WB1) * rx_buf[pl.ds(6, ts), :]
             + vrow(V_WB0) * rx_buf[pl.ds(5, ts), :] + vrow(V_CBB))
        ua_buf[0:8, :] = ua_buf[ts:ts + 8, :]
        rx_buf[0:8, :] = rx_buf[ts:ts + 8, :]
        u_ref[...] = u

        rows = lax.broadcasted_iota(jnp.int32, (ts, D), 0)
        row0 = jnp.logical_and(rows == 0, i == 0)
        _, ig, a, mult = _lru_gates(u, wa_ref, wx_ref, v_ref, row0)
        bx = mult * (ig * u)

        sub = rows % 8
        for d in (1, 2, 4):
            m = sub >= d
            bx = jnp.where(m, a * pltpu.roll(bx, d, axis=0) + bx, bx)
            a = jnp.where(m, a * pltpu.roll(a, d, axis=0), a)
        a_buf[...] = a
        b_buf[...] = bx

        def grp(g, carry):
            off = pl.multiple_of(g * 8, 8)
            h_ref[pl.ds(off, 8), :] = a_buf[pl.ds(off, 8), :] * carry + b_buf[pl.ds(off, 8), :]
            return jnp.broadcast_to(h_ref[pl.ds(off + 7, 1), :], (8, D))

        hcarry[...] = lax.fori_loop(0, ts // 8, grp, hcarry[...])
        h = h_ref[...]

        gel, _ = _gelu(rg)
        merged = (jax.nn.sigmoid(ga) * (cb * va) + jax.nn.sigmoid(gb) * (h * gel)).astype(BF16)
        mg_ref[...] = merged
        z1 = _dot(merged, wo_ref[...])
        z1_ref[...] = z1.astype(BF16)
        x1_ref[...] = x_ref[...] + vrow(V_GT1) * z1

    row = lambda i: (i, 0)
    return pl.pallas_call(
        body, grid=(s // ts,),
        out_shape=(jax.ShapeDtypeStruct((s, D), F32), jax.ShapeDtypeStruct((s, D), BF16), jax.ShapeDtypeStruct((s, D), BF16),
                   jax.ShapeDtypeStruct((s, D), F32), jax.ShapeDtypeStruct((s, D), F32)),
        in_specs=[pl.BlockSpec((ts, D_IN), row), pl.BlockSpec((ts, D), row), _VMEM, _VMEM, _VMEM, _VMEM],
        out_specs=[pl.BlockSpec((ts, D), row)] * 5,
        scratch_shapes=[pltpu.VMEM((ts + 8, D), F32), pltpu.VMEM((ts + 8, D), F32), pltpu.VMEM((ts, D), F32),
                        pltpu.VMEM((ts, D), F32), pltpu.VMEM((8, D), F32)],
        compiler_params=_cparams(), name="fwd_mix")(proj, x, vecs, w_rga, w_rgx, w_out)


def _ffn_loss(x1, target, vecs, w_gu_g, w_dn, ts):
    s = x1.shape[0]

    def body(x1_ref, t_ref, v_ref, wgu_ref, wdn_ref, dx1_ref, h2_ref, act_ref, dz2_ref, dgu_ref, sm_ref):
        @pl.when(pl.program_id(0) == 0)
        def _():
            sm_ref[...] = jnp.zeros((N_SMALL, D), F32)

        def vrow(j):
            return v_ref[j:j + 1, :]

        def acc(j, val):
            sm_ref[j:j + 1, :] += _rowsum(val)

        x1 = x1_ref[...]
        xh1, rstd1 = _rms(x1)
        n2 = xh1 * vrow(V_GFFN)
        h2 = (n2 * (1.0 + vrow(V_SC2)) + vrow(V_SH2)).astype(BF16)
        h2_ref[...] = h2
        g = jnp.concatenate([_dot(h2, wgu_ref[0]), _dot(h2, wgu_ref[1])], axis=1)
        up = jnp.concatenate([_dot(h2, wgu_ref[2]), _dot(h2, wgu_ref[3])], axis=1)
        sg = jax.nn.sigmoid(g)
        silu = g * sg
        act = (silu * up).astype(BF16)
        act_ref[...] = act
        z2 = _dot(act, wdn_ref[...])
        x2 = x1 + vrow(V_GT2) * z2
        xh2, rstd2 = _rms(x2)
        err = xh2 * vrow(V_GFIN) - t_ref[...]
        acc(G_LOSS, (0.5 / D) * err * err)
        dy = err * (1.0 / D)
        acc(G_GFIN, dy * xh2)
        dx2 = _rms_bwd(dy * vrow(V_GFIN), xh2, rstd2)
        acc(G_GT2, dx2 * z2)
        dz2 = (vrow(V_GT2) * dx2).astype(BF16)
        dz2_ref[...] = dz2
        dact = _dot_nt(dz2, wdn_ref[...])
        dgate = (dact * up * (sg * (1.0 + g * (1.0 - sg)))).astype(BF16)
        dup = (dact * silu).astype(BF16)
        dgu_ref[:, 0:D_FF] = dgate
        dgu_ref[:, D_FF:2 * D_FF] = dup
        dh2 = (_dot_nt(dgate[:, 0:C_GU], wgu_ref[0]) + _dot_nt(dgate[:, C_GU:2 * C_GU], wgu_ref[1])
               + _dot_nt(dup[:, 0:C_GU], wgu_ref[2]) + _dot_nt(dup[:, C_GU:2 * C_GU], wgu_ref[3]))
        acc(G_SH2, dh2)
        acc(G_SC2, dh2 * n2)
        dn2 = dh2 * (1.0 + vrow(V_SC2))
        acc(G_GFFN, dn2 * xh1)
        dx1_ref[...] = dx2 + _rms_bwd(dn2 * vrow(V_GFFN), xh1, rstd1)

    row = lambda i: (i, 0)
    return pl.pallas_call(
        body, grid=(s // ts,),
        out_shape=(jax.ShapeDtypeStruct((s, D), F32), jax.ShapeDtypeStruct((s, D), BF16), jax.ShapeDtypeStruct((s, D_FF), BF16),
                   jax.ShapeDtypeStruct((s, D), BF16), jax.ShapeDtypeStruct((s, 2 * D_FF), BF16),
                   jax.ShapeDtypeStruct((N_SMALL, D), F32)),
        in_specs=[pl.BlockSpec((ts, D), row), pl.BlockSpec((ts, D), row), _VMEM, _VMEM, _VMEM],
        out_specs=[pl.BlockSpec((ts, D), row), pl.BlockSpec((ts, D), row), pl.BlockSpec((ts, D_FF), row),
                   pl.BlockSpec((ts, D), row), pl.BlockSpec((ts, 2 * D_FF), row), pl.BlockSpec((N_SMALL, D), lambda i: (0, 0))],
        compiler_params=_cparams(), name="ffn_loss")(x1, target, vecs, w_gu_g, w_dn)


def _bwd_mix(dx1, z1, proj, u, h, vecs, w_rga, w_rgx, w_out, ts):
    s = dx1.shape[0]
    nt = s // ts

    def body(dx1_ref, z1_ref, proj_ref, u_ref, h_ref, hh_ref, cch_ref, cxh_ref, v_ref, wa_ref, wx_ref, wo_ref,
             dproj_ref, dz1_ref, sm_ref, dwa_ref, dwx_ref,
             ua_buf, h_buf, a_buf, dva_buf, du_buf, ca_buf, cb_buf, l_buf, lcarry):
        i = pl.program_id(0)
        first_tile = i == nt - 1

        @pl.when(i == 0)
        def _():
            a_buf[ts:ts + 8, :] = jnp.zeros((8, D), F32)
            dva_buf[ts:ts + 8, :] = jnp.zeros((8, D), F32)
            du_buf[ts:ts + 8, :] = jnp.zeros((8, D), F32)
            lcarry[...] = jnp.zeros((8, D), F32)
            sm_ref[...] = jnp.zeros((N_SMALL, D), F32)
            dwa_ref[...] = jnp.zeros((HEADS, HB, HB), F32)
            dwx_ref[...] = jnp.zeros((HEADS, HB, HB), F32)

        def seg(j):
            return proj_ref[:, j * D:(j + 1) * D].astype(F32)

        def vrow(j):
            return v_ref[j:j + 1, :]

        def acc(j, val):
            sm_ref[j:j + 1, :] += _rowsum(val)

        cb, cc, cx, rx, rg, ga, gb = (seg(j) for j in range(7))
        ua = cc * cx
        ua_halo = cch_ref[8:16, :].astype(F32) * cxh_ref[8:16, :].astype(F32)
        ua_buf[0:8, :] = jnp.where(first_tile, 0.0, ua_halo)
        ua_buf[8:ts + 8, :] = ua
        va = vrow(V_WA2) * ua + vrow(V_WA1) * ua_buf[pl.ds(7, ts), :] + vrow(V_WA0) * ua_buf[pl.ds(6, ts), :]
        u = u_ref[...]
        h = h_ref[...]
        rows = lax.broadcasted_iota(jnp.int32, (ts, D), 0)
        row0 = jnp.logical_and(rows == 0, first_tile)
        r, ig, a, mult = _lru_gates(u, wa_ref, wx_ref, v_ref, row0)
        sga = jax.nn.sigmoid(ga)
        sgb = jax.nn.sigmoid(gb)
        gel, dgel = _gelu(rg)

        dx1 = dx1_ref[...]
        acc(G_GT1, dx1 * z1_ref[...].astype(F32))
        dz1 = (vrow(V_GT1) * dx1).astype(BF16)
        dz1_ref[...] = dz1
        dmg = _dot_nt(dz1, wo_ref[...])
        dproj_ref[:, 5 * D:6 * D] = (dmg * (cb * va) * sga * (1.0 - sga)).astype(BF16)
        dproj_ref[:, 6 * D:7 * D] = (dmg * (h * gel) * sgb * (1.0 - sgb)).astype(BF16)
        dya = dmg * sga
        dyb = dmg * sgb

        dproj_ref[:, 0:D] = (dya * va).astype(BF16)
        dva = dya * cb
        dva_buf[0:ts, :] = dva
        dva1 = dva_buf[pl.ds(1, ts), :]
        dva2 = dva_buf[pl.ds(2, ts), :]
        dua = vrow(V_WA2) * dva + vrow(V_WA1) * dva1 + vrow(V_WA0) * dva2
        acc(G_WA2, ua * dva)
        acc(G_WA1, ua * dva1)
        acc(G_WA0, ua * dva2)
        dva_buf[ts:ts + 8, :] = dva_buf[0:8, :]
        dproj_ref[:, D:2 * D] = (dua * cx).astype(BF16)
        dproj_ref[:, 2 * D:3 * D] = (dua * cc).astype(BF16)

        dproj_ref[:, 4 * D:5 * D] = (dyb * h * dgel).astype(BF16)
        a_buf[0:ts, :] = a
        sa = a_buf[pl.ds(1, ts), :]
        sb = dyb * gel
        sub = rows % 8
        for d in (1, 2, 4):
            m = sub < 8 - d
            sb = jnp.where(m, sb + sa * pltpu.roll(sb, ts - d, axis=0), sb)
            sa = jnp.where(m, sa * pltpu.roll(sa, ts - d, axis=0), sa)
        ca_buf[...] = sa
        cb_buf[...] = sb

        def grp(k, carry):
            off = pl.multiple_of((ts // 8 - 1 - k) * 8, 8)
            l_buf[pl.ds(off, 8), :] = ca_buf[pl.ds(off, 8), :] * carry + cb_buf[pl.ds(off, 8), :]
            return jnp.broadcast_to(l_buf[pl.ds(off, 1), :], (8, D))

        lcarry[...] = lax.fori_loop(0, ts // 8, grp, lcarry[...])
        a_buf[ts:ts + 8, :] = a_buf[0:8, :]
        lam = l_buf[...]

        h_buf[0:8, :] = jnp.where(first_tile, 0.0, hh_ref[...])
        h_buf[8:ts + 8, :] = h
        da = lam * h_buf[pl.ds(7, ts), :]
        dmult = jnp.where(row0, 0.0, lam * (ig * u))
        di = lam * mult * u
        du = lam * mult * ig
        dlog_a = da * a - dmult * (a * a) / mult
        lam_p = vrow(V_LAM)
        dr = dlog_a * (LRU_C * _log_sigmoid(lam_p))
        sm_ref[G_LAM:G_LAM + 1, :] += _rowsum(dlog_a * r) * (LRU_C * jax.nn.sigmoid(-lam_p))
        dpa = dr * r * (1.0 - r)
        dpx = di * ig * (1.0 - ig)
        acc(G_BA, dpa)
        acc(G_BX, dpx)
        dpab = dpa.astype(BF16)
        dpxb = dpx.astype(BF16)
        ub = u.astype(BF16)
        back = []
        for hd in range(HEADS):
            cols = slice(hd * HB, (hd + 1) * HB)
            back.append(_dot_nt(dpab[:, cols], wa_ref[hd]) + _dot_nt(dpxb[:, cols], wx_ref[hd]))
            dwa_ref[hd] += _dot_tn(ub[:, cols], dpab[:, cols])
            dwx_ref[hd] += _dot_tn(ub[:, cols], dpxb[:, cols])
        du = du + jnp.concatenate(back, axis=1)

        acc(G_CBB, du)
        du_buf[0:ts, :] = du
        du1 = du_buf[pl.ds(1, ts), :]
        du2 = du_buf[pl.ds(2, ts), :]
        du3 = du_buf[pl.ds(3, ts), :]
        dproj_ref[:, 3 * D:4 * D] = (vrow(V_WB3) * du + vrow(V_WB2) * du1 + vrow(V_WB1) * du2 + vrow(V_WB0) * du3).astype(BF16)
        acc(G_WB3, rx * du)
        acc(G_WB2, rx * du1)
        acc(G_WB1, rx * du2)
        acc(G_WB0, rx * du3)
        du_buf[ts:ts + 8, :] = du_buf[0:8, :]

    rev = lambda i: (nt - 1 - i, 0)
    halo8 = lambda i: (jnp.maximum((nt - 1 - i) * (ts // 8) - 1, 0), 0)
    const2 = lambda i: (0, 0)
    const3 = lambda i: (0, 0, 0)
    return pl.pallas_call(
        body, grid=(nt,),
        out_shape=(jax.ShapeDtypeStruct((s, D_IN), BF16), jax.ShapeDtypeStruct((s, D), BF16),
                   jax.ShapeDtypeStruct((N_SMALL, D), F32), jax.ShapeDtypeStruct((HEADS, HB, HB), F32),
                   jax.ShapeDtypeStruct((HEADS, HB, HB), F32)),
        in_specs=[pl.BlockSpec((ts, D), rev), pl.BlockSpec((ts, D), rev), pl.BlockSpec((ts, D_IN), rev),
                  pl.BlockSpec((ts, D), rev), pl.BlockSpec((ts, D), rev), pl.BlockSpec((8, D), halo8),
                  pl.BlockSpec((16, D), lambda i: (jnp.maximum((nt - 1 - i) * (ts // 16) - 1, 0), 1)),
                  pl.BlockSpec((16, D), lambda i: (jnp.maximum((nt - 1 - i) * (ts // 16) - 1, 0), 2)),
                  _VMEM, _VMEM, _VMEM, _VMEM],
        out_specs=[pl.BlockSpec((ts, D_IN), rev), pl.BlockSpec((ts, D), rev), pl.BlockSpec((N_SMALL, D), const2),
                   pl.BlockSpec((HEADS, HB, HB), const3), pl.BlockSpec((HEADS, HB, HB), const3)],
        scratch_shapes=[pltpu.VMEM((ts + 8, D), F32)] * 5 + [pltpu.VMEM((ts, D), F32)] * 3 + [pltpu.VMEM((8, D), F32)],
        compiler_params=_cparams(), name="bwd_mix")(dx1, z1, proj, u, h, h, proj, proj, vecs, w_rga, w_rgx, w_out)


def _bwd_in(dproj, x, dx1, vecs, w_in_g, ts):
    s = x.shape[0]

    def body(dp_ref, x_ref, dx1_ref, v_ref, w_ref, gx_ref, sm_ref):
        @pl.when(pl.program_id(0) == 0)
        def _():
            sm_ref[...] = jnp.zeros((N_SMALL, D), F32)

        def vrow(j):
            return v_ref[j:j + 1, :]

        dh1 = _dot_nt(dp_ref[:, 0:C_IN], w_ref[0])
        for k in range(1, N_CHIPS):
            dh1 += _dot_nt(dp_ref[:, k * C_IN:(k + 1) * C_IN], w_ref[k])
        xh, rstd = _rms(x_ref[...])
        sm_ref[G_SH1:G_SH1 + 1, :] += _rowsum(dh1)
        sm_ref[G_SC1:G_SC1 + 1, :] += _rowsum(dh1 * (xh * vrow(V_GMIX)))
        dn1 = dh1 * (1.0 + vrow(V_SC1))
        sm_ref[G_GMIX:G_GMIX + 1, :] += _rowsum(dn1 * xh)
        gx_ref[...] = dx1_ref[...] + _rms_bwd(dn1 * vrow(V_GMIX), xh, rstd)

    row = lambda i: (i, 0)
    return pl.pallas_call(
        body, grid=(s // ts,),
        out_shape=(jax.ShapeDtypeStruct((s, D), F32), jax.ShapeDtypeStruct((N_SMALL, D), F32)),
        in_specs=[pl.BlockSpec((ts, D_IN), row), pl.BlockSpec((ts, D), row), pl.BlockSpec((ts, D), row), _VMEM, _VMEM],
        out_specs=[pl.BlockSpec((ts, D), row), pl.BlockSpec((N_SMALL, D), lambda i: (0, 0))],
        compiler_params=_cparams(), name="bwd_in")(dproj, x, dx1, vecs, w_in_g)


def _grad_w(a, b, n_col_blocks, ts, name):
    s, m = a.shape
    tn = b.shape[1] // n_col_blocks
    n_steps = s // ts

    def body(a_ref, b_ref, o_ref, acc_ref):
        k = pl.program_id(1)

        @pl.when(k == 0)
        def _():
            acc_ref[...] = jnp.zeros((m, tn), F32)

        acc_ref[...] += _dot_tn(a_ref[...], b_ref[...])

        @pl.when(k == n_steps - 1)
        def _():
            o_ref[...] = acc_ref[...].astype(BF16)

    return pl.pallas_call(
        body, grid=(n_col_blocks, n_steps),
        out_shape=jax.ShapeDtypeStruct((n_col_blocks, m, tn), BF16),
        in_specs=[pl.BlockSpec((ts, m), lambda n, k: (k, 0)), pl.BlockSpec((ts, tn), lambda n, k: (k, n))],
        out_specs=pl.BlockSpec((None, m, tn), lambda n, k: (n, 0, 0)),
        scratch_shapes=[pltpu.VMEM((m, tn), F32)],
        compiler_params=_cparams(2), name=name)(a, b)


def _ada_fwd(c_all, w_ada, b_ada):
    n = w_ada.shape[1]

    def body(c_ref, w_ref, b_ref, o_ref, ca_ref):
        c = c_ref[...]
        ca = c * jax.nn.sigmoid(c)
        ca_ref[...] = ca
        o_ref[...] = jnp.dot(ca, w_ref[...], preferred_element_type=F32, precision=lax.Precision.HIGHEST) + b_ref[...]

    return pl.pallas_call(
        body, out_shape=(jax.ShapeDtypeStruct((N_DEV, n), F32), jax.ShapeDtypeStruct((N_DEV, D), F32)),
        in_specs=[_VMEM] * 3, out_specs=[_VMEM] * 2, compiler_params=_cparams(0), name="ada_fwd")(c_all, w_ada, b_ada)


def _ada_bwd(c_act, dmod):
    n = dmod.shape[1]

    def body(c_ref, d_ref, o_ref):
        o_ref[...] = lax.dot_general(c_ref[...], d_ref[...], (((0,), (0,)), ((), ())), preferred_element_type=F32,
                                     precision=lax.Precision.HIGHEST)

    return pl.pallas_call(
        body, out_shape=jax.ShapeDtypeStruct((D, n), F32), in_specs=[_VMEM] * 2, out_specs=_VMEM,
        compiler_params=_cparams(0), name="ada_bwd")(c_act, dmod)


def _sum_small(parts):
    def body(p_ref, o_ref, d_ref):
        tot = None
        for dev in range(N_DEV):
            mine = p_ref[dev, 0] + p_ref[dev, 1] + p_ref[dev, 2]
            d_ref[dev] = mine[0:8, :]
            tot = mine if tot is None else tot + mine
        o_ref[...] = tot

    return pl.pallas_call(
        body, out_shape=(jax.ShapeDtypeStruct((N_SMALL, D), F32), jax.ShapeDtypeStruct((N_DEV, 8, D), F32)),
        in_specs=[_VMEM], out_specs=[_VMEM] * 2, compiler_params=_cparams(0), name="sum_small")(parts)


def _adamw(w, g, m, v, name):
    rows, cols = w.shape
    tr = 128 if rows % 128 == 0 else (64 if rows % 64 == 0 else rows)

    def body(w_ref, g_ref, m_ref, v_ref, d_ref, nm_ref, nv_ref):
        g_ = g_ref[...]
        m_ = ADAM_B1 * m_ref[...] + (1.0 - ADAM_B1) * g_
        v_ = ADAM_B2 * v_ref[...] + (1.0 - ADAM_B2) * (g_ * g_)
        nm_ref[...] = m_
        nv_ref[...] = v_
        m_hat = m_ / (1.0 - ADAM_B1 ** ADAM_STEP)
        v_hat = v_ / (1.0 - ADAM_B2 ** ADAM_STEP)
        d_ref[...] = -ADAM_LR * (m_hat / (jnp.sqrt(v_hat) + ADAM_EPS) + ADAM_WD * w_ref[...])

    spec = pl.BlockSpec((tr, cols), lambda i: (i, 0))
    return pl.pallas_call(
        body, grid=(rows // tr,), out_shape=(jax.ShapeDtypeStruct((rows, cols), F32),) * 3,
        in_specs=[spec] * 4, out_specs=[spec] * 3, compiler_params=_cparams(), name=name)(w, g, m, v)


def _add_halves(g, recv, c_idx, name):
    n, _, r2, cols = g.shape

    def body(c_ref, g_ref, r_ref, o_ref):
        o_ref[...] = (g_ref[...].astype(F32) + r_ref[...].astype(F32)).astype(BF16)

    return pl.pallas_call(
        body,
        grid_spec=pltpu.PrefetchScalarGridSpec(
            num_scalar_prefetch=1, grid=(n,),
            in_specs=[pl.BlockSpec((None, None, r2, cols), lambda k, c: (k, c[0], 0, 0)),
                      pl.BlockSpec((None, r2, cols), lambda k, c: (k, 0, 0))],
            out_specs=pl.BlockSpec((None, r2, cols), lambda k, c: (k, 0, 0))),
        out_shape=jax.ShapeDtypeStruct((n, r2, cols), BF16), compiler_params=_cparams(), name=name)(c_idx, g, recv)


def _sum_chips(parts, name):
    n, r2, cols = parts.shape
    tr = r2 // 2 if (r2 // 2) % 16 == 0 else r2

    def body(p_ref, o_ref):
        o_ref[...] = ((p_ref[0].astype(F32) + p_ref[1].astype(F32)) + p_ref[2].astype(F32)) + p_ref[3].astype(F32)

    return pl.pallas_call(
        body, grid=(r2 // tr,), out_shape=jax.ShapeDtypeStruct((r2, cols), F32),
        in_specs=[pl.BlockSpec((n, tr, cols), lambda i: (0, i, 0))], out_specs=pl.BlockSpec((tr, cols), lambda i: (i, 0)),
        compiler_params=_cparams(), name=name)(parts)


def _place():
    x, y, c = lax.axis_index("x"), lax.axis_index("y"), lax.axis_index("c")
    return x, y, c, 2 * x + y


def _flip(v, bit):
    return 1 - v if bit else v


def _allgather8(v, name):
    r, n = v.shape

    def body(v_ref, out_ref, send_sems, recv_sems, local_sem):
        x, y, c, _ = _place()
        me = 4 * x + 2 * y + c
        mine = pltpu.make_async_copy(v_ref, out_ref.at[me], local_sem)
        mine.start()
        sends = []
        for rel in range(1, N_DEV):
            peer = (_flip(x, rel & 4), _flip(y, rel & 2), _flip(c, rel & 1))
            cp = pltpu.make_async_remote_copy(v_ref, out_ref.at[me], send_sems.at[rel - 1], recv_sems.at[rel - 1],
                                              device_id=peer, device_id_type=MESH)
            cp.start()
            sends.append(cp)
        for rel in range(1, N_DEV):
            peer = (_flip(x, rel & 4), _flip(y, rel & 2), _flip(c, rel & 1))
            peer_idx = 4 * peer[0] + 2 * peer[1] + peer[2]
            pltpu.make_async_remote_copy(v_ref, out_ref.at[peer_idx], send_sems.at[rel - 1], recv_sems.at[rel - 1],
                                         device_id=peer, device_id_type=MESH).wait_recv()
        for cp in sends:
            cp.wait_send()
        mine.wait()

    return pl.pallas_call(
        body, out_shape=jax.ShapeDtypeStruct((N_DEV, r, n), F32), in_specs=[_VMEM], out_specs=_VMEM,
        scratch_shapes=[pltpu.SemaphoreType.DMA((N_DEV - 1,)), pltpu.SemaphoreType.DMA((N_DEV - 1,)), pltpu.SemaphoreType.DMA(())],
        name=name)(v)


def _gather_weights(shards):
    nw = len(shards)

    def body(*refs):
        w_refs, out_refs = refs[:nw], refs[nw:2 * nw]
        send_sems, recv_sems = refs[2 * nw:]
        x, y, c, p = _place()
        sibling = (x, y, 1 - c)
        sends = []
        for j in range(1, N_CHIPS):
            peer = (_flip(x, j & 2), _flip(y, j & 1), c)
            for w in range(nw):
                cp = pltpu.make_async_remote_copy(w_refs[w].at[c], out_refs[w].at[p, c], send_sems.at[w * 6 + j - 1],
                                                  recv_sems.at[w * 6 + j - 1], device_id=peer, device_id_type=MESH)
                cp.start()
                sends.append(cp)
        for j in range(1, N_CHIPS):
            peer = (_flip(x, j & 2), _flip(y, j & 1), c)
            q = 2 * peer[0] + peer[1]
            for w in range(nw):
                pltpu.make_async_remote_copy(w_refs[w].at[c], out_refs[w].at[q, c], send_sems.at[w * 6 + j - 1],
                                             recv_sems.at[w * 6 + j - 1], device_id=peer, device_id_type=MESH).wait_recv()
                cp = pltpu.make_async_remote_copy(out_refs[w].at[q, c], out_refs[w].at[q, c], send_sems.at[w * 6 + 2 + j],
                                                  recv_sems.at[w * 6 + 2 + j], device_id=sibling, device_id_type=MESH)
                cp.start()
                sends.append(cp)
        for j in range(1, N_CHIPS):
            q = 2 * _flip(x, j & 2) + _flip(y, j & 1)
            for w in range(nw):
                pltpu.make_async_remote_copy(out_refs[w].at[q, 1 - c], out_refs[w].at[q, 1 - c], send_sems.at[w * 6 + 2 + j],
                                             recv_sems.at[w * 6 + 2 + j], device_id=sibling, device_id_type=MESH).wait_recv()
        for cp in sends:
            cp.wait_send()

    return pl.pallas_call(
        body, out_shape=tuple(jax.ShapeDtypeStruct((N_CHIPS,) + s.shape, s.dtype) for s in shards),
        in_specs=[_ANY] * nw, out_specs=[_ANY] * nw,
        scratch_shapes=[pltpu.SemaphoreType.DMA((6 * nw,)), pltpu.SemaphoreType.DMA((6 * nw,))],
        name="gather_weights")(*shards)


def _swap_halves(grads):
    nw = len(grads)

    def body(*refs):
        g_refs, out_refs = refs[:nw], refs[nw:2 * nw]
        send_sems, recv_sems = refs[2 * nw:]
        x, y, c, _ = _place()
        sibling = (x, y, 1 - c)
        sends = []
        for w in range(nw):
            for k in range(N_CHIPS):
                cp = pltpu.make_async_remote_copy(g_refs[w].at[k, 1 - c], out_refs[w].at[k], send_sems.at[w * N_CHIPS + k],
                                                  recv_sems.at[w * N_CHIPS + k], device_id=sibling, device_id_type=MESH)
                cp.start()
                sends.append(cp)
        for cp in sends:
            cp.wait_recv()
        for cp in sends:
            cp.wait_send()

    return pl.pallas_call(
        body, out_shape=tuple(jax.ShapeDtypeStruct((N_CHIPS,) + g.shape[2:], g.dtype) for g in grads),
        in_specs=[_ANY] * nw, out_specs=[_ANY] * nw,
        scratch_shapes=[pltpu.SemaphoreType.DMA((N_CHIPS * nw,)), pltpu.SemaphoreType.DMA((N_CHIPS * nw,))],
        name="swap_halves")(*grads)


def _scatter_chips(parts):
    nw = len(parts)

    def body(*refs):
        p_refs, out_refs = refs[:nw], refs[nw:2 * nw]
        send_sems, recv_sems = refs[2 * nw:]
        x, y, c, p = _place()
        sends = []
        for j in range(1, N_CHIPS):
            peer = (_flip(x, j & 2), _flip(y, j & 1), c)
            q = 2 * peer[0] + peer[1]
            for w in range(nw):
                cp = pltpu.make_async_remote_copy(p_refs[w].at[q], out_refs[w].at[p], send_sems.at[w * 3 + j - 1],
                                                  recv_sems.at[w * 3 + j - 1], device_id=peer, device_id_type=MESH)
                cp.start()
                sends.append(cp)
        for j in range(1, N_CHIPS):
            peer = (_flip(x, j & 2), _flip(y, j & 1), c)
            q = 2 * peer[0] + peer[1]
            for w in range(nw):
                pltpu.make_async_remote_copy(p_refs[w].at[q], out_refs[w].at[q], send_sems.at[w * 3 + j - 1],
                                             recv_sems.at[w * 3 + j - 1], device_id=peer, device_id_type=MESH).wait_recv()
        for cp in sends:
            cp.wait_send()

    return pl.pallas_call(
        body, out_shape=tuple(jax.ShapeDtypeStruct(s.shape, s.dtype) for s in parts),
        in_specs=[_ANY] * nw, out_specs=[_ANY] * nw,
        scratch_shapes=[pltpu.SemaphoreType.DMA((3 * nw,)), pltpu.SemaphoreType.DMA((3 * nw,))],
        name="scatter_chips")(*parts)


def _share_halves(halves):
    nw = len(halves)

    def body(*refs):
        h_refs, out_refs = refs[:nw], refs[nw:2 * nw]
        send_sems, recv_sems = refs[2 * nw:]
        x, y, c, _ = _place()
        sends = []
        for w in range(nw):
            cp = pltpu.make_async_remote_copy(h_refs[w], out_refs[w], send_sems.at[w], recv_sems.at[w],
                                              device_id=(x, y, 1 - c), device_id_type=MESH)
            cp.start()
            sends.append(cp)
        for cp in sends:
            cp.wait_recv()
        for cp in sends:
            cp.wait_send()

    return pl.pallas_call(
        body, out_shape=tuple(jax.ShapeDtypeStruct(s.shape, s.dtype) for s in halves),
        in_specs=[_ANY] * nw, out_specs=[_ANY] * nw,
        scratch_shapes=[pltpu.SemaphoreType.DMA((nw,)), pltpu.SemaphoreType.DMA((nw,))],
        name="share_halves")(*halves)


def _pack_rows(parts, n_rows, name):
    def body(*refs):
        out_ref = refs[-1]
        out_ref[...] = jnp.zeros((n_rows, D), F32)
        at = 0
        for ref in refs[:-1]:
            k = ref.shape[0]
            out_ref[at:at + k, :] = ref[...]
            at += k

    return pl.pallas_call(
        body, out_shape=jax.ShapeDtypeStruct((n_rows, D), F32), in_specs=[_VMEM] * len(parts), out_specs=_VMEM,
        name=name)(*parts)


def _local_step(x, target, vecs, w_in_g, w_rga, w_rgx, w_out, w_gu_g, w_dn, ts_mm=512, ts_mix=256):
    h1, proj = _fwd_in(x, vecs, w_in_g, ts_mm)
    x1, merged, z1, u, h = _fwd_mix(proj, x, vecs, w_rga, w_rgx, w_out, ts_mix)
    dx1, h2, act, dz2, dgu, sm_ffn = _ffn_loss(x1, target, vecs, w_gu_g, w_dn, ts_mix)
    dproj, dz1, sm_mix, dw_rga, dw_rgx = _bwd_mix(dx1, z1, proj, u, h, vecs, w_rga, w_rgx, w_out, ts_mix)
    grad_x, sm_in = _bwd_in(dproj, x, dx1, vecs, w_in_g, ts_mm)
    g_in = _grad_w(h1, dproj, N_CHIPS, ts_mm, "grad_w_in")
    g_out = _grad_w(merged, dz1, 1, ts_mm, "grad_w_out")
    g_gu = _grad_w(h2, dgu, N_CHIPS, ts_mm, "grad_w_gate_up")
    g_dn = _grad_w(act, dz2, 1, ts_mm, "grad_w_down")
    return grad_x, jnp.stack([sm_ffn, sm_mix, sm_in]), (g_in, g_out, g_gu, g_dn, dw_rga, dw_rgx)


def _halved(a):
    n, r, cols = a.shape
    return a.reshape(n, 2, r // 2, cols)


def kernel(x, c, w_ada, b_ada, g_norm_mix, w_in, conv_a_w, conv_b_w, conv_b_bias, w_rg_a, b_rg_a, w_rg_x, b_rg_x, lru_lambda, w_out, g_norm_ffn, w_gate_up, w_down, g_norm_final, loss_target, m_w_ada, m_b_ada, m_g_norm_mix, m_w_in, m_conv_a_w, m_conv_b_w, m_conv_b_bias, m_w_rg_a, m_b_rg_a, m_w_rg_x, m_b_rg_x, m_lru_lambda, m_w_out, m_g_norm_ffn, m_w_gate_up, m_w_down, m_g_norm_final, v_w_ada, v_b_ada, v_g_norm_mix, v_w_in, v_conv_a_w, v_conv_b_w, v_conv_b_bias, v_w_rg_a, v_b_rg_a, v_w_rg_x, v_b_rg_x, v_lru_lambda, v_w_out, v_g_norm_ffn, v_w_gate_up, v_w_down, v_g_norm_final):
    xi, yi, ci = lax.axis_index("x"), lax.axis_index("y"), lax.axis_index("c")
    chip = 2 * xi + yi
    me = 2 * chip + ci
    n_ada = w_ada.shape[2]

    def widen(w):
        return jnp.pad(w, ((0, 0), (0, D - w.shape[1])))

    got = _allgather8(_pack_rows([c, widen(conv_a_w[0]), widen(conv_b_w[0])], 8, "pack_c_conv"), "gather_c_conv")
    c_all = got[:, 0, :]
    conv_full = got[::2, 1:8, :D // N_CHIPS].transpose(1, 0, 2).reshape(7, D)

    mod_part, c_act = _ada_fwd(c_all, w_ada[0], lax.dynamic_slice_in_dim(b_ada, chip * n_ada, n_ada, axis=1))
    mod_all = _allgather8(mod_part, "gather_mod")
    mod_mine = lax.dynamic_index_in_dim(mod_all, me, axis=1, keepdims=False)[::2].reshape(6, D)
    vecs = _pack_rows([mod_mine, g_norm_mix, g_norm_ffn, g_norm_final.reshape(1, D), conv_b_bias, b_rg_a, b_rg_x, lru_lambda,
                       conv_full], N_VEC, "pack_vecs")

    def rg_shard(w):
        return w[0].astype(BF16).reshape(2, HEADS * HB // N_CHIPS // 2, HB)

    shards = (w_in[0].astype(BF16).reshape(2, D // 2, C_IN), w_out[0].astype(BF16).reshape(2, D // N_CHIPS // 2, D),
              w_gate_up[0].astype(BF16).reshape(2, D // 2, C_GU), w_down[0].astype(BF16).reshape(2, D_FF // N_CHIPS // 2, D),
              rg_shard(w_rg_a), rg_shard(w_rg_x))
    wg_in, wg_out, wg_gu, wg_dn, wg_rga, wg_rgx = (
        lax.dynamic_update_index_in_dim(g, s, chip, 0) for g, s in zip(_gather_weights(shards), shards))

    def rg_full(wg):
        return wg.reshape(N_CHIPS, HEADS, HB // N_CHIPS, HB).transpose(1, 0, 2, 3).reshape(HEADS, HB, HB)

    grad_x, sm3, (g_in, g_out, g_gu, g_dn, dw_rga, dw_rgx) = _local_step(
        x[0], loss_target[0], vecs, wg_in.reshape(N_CHIPS, D, C_IN), rg_full(wg_rga), rg_full(wg_rgx),
        wg_out.reshape(D, D), wg_gu.reshape(N_CHIPS, D, C_GU), wg_dn.reshape(D_FF, D))

    small, per_dev = _sum_small(_allgather8(sm3.reshape(3 * N_SMALL, D), "gather_small").reshape(N_DEV, 3, N_SMALL, D))
    dmod_all = per_dev[:, 0:6, :].reshape(N_DEV, 6 * D)
    grad_w_ada = _ada_bwd(c_act, lax.dynamic_slice_in_dim(dmod_all, chip * n_ada, n_ada, axis=1))

    def rg_chunks(dw):
        return dw.reshape(HEADS, N_CHIPS, HB // N_CHIPS, HB).transpose(1, 0, 2, 3).reshape(N_CHIPS, HB, HB).astype(BF16)

    grads = (_halved(g_in), _halved(g_out.reshape(N_CHIPS, D // N_CHIPS, D)), _halved(g_gu),
             _halved(g_dn.reshape(N_CHIPS, D_FF // N_CHIPS, D)), _halved(rg_chunks(dw_rga)), _halved(rg_chunks(dw_rgx)))
    from_sibling = _swap_halves(grads)
    c_arr = jnp.reshape(ci, (1,)).astype(jnp.int32)
    pair_sums = tuple(_add_halves(g, r, c_arr, "add_halves_%d" % k) for k, (g, r) in enumerate(zip(grads, from_sibling)))
    by_chip = tuple(lax.dynamic_update_index_in_dim(b, lax.dynamic_index_in_dim(p, chip, 0, keepdims=False), chip, 0)
                    for b, p in zip(_scatter_chips(pair_sums), pair_sums))
    halves = tuple(_sum_chips(b, "sum_chips_%d" % k) for k, b in enumerate(by_chip))
    gw_in, gw_out, gw_gu, gw_dn, gw_rga, gw_rgx = (
        jnp.where(ci == 0, jnp.concatenate([mine, other], axis=0), jnp.concatenate([other, mine], axis=0))
        for mine, other in zip(halves, _share_halves(halves)))

    def step(name, w, g, m, v):
        shape = w.shape
        two_d = (-1, shape[-1])
        d, nm, nv = _adamw(w.reshape(two_d), g.reshape(two_d), m.reshape(two_d), v.reshape(two_d), "adamw_" + name)
        return g.reshape(shape), d.reshape(shape), nm.reshape(shape), nv.reshape(shape)

    def shard_cols(row_block):
        return lax.dynamic_slice_in_dim(row_block, chip * (D // N_CHIPS), D // N_CHIPS, axis=1)

    grad_b_ada = small[0:6].reshape(1, 6 * D)
    res = {
        "w_ada": step("w_ada", w_ada, grad_w_ada[None], m_w_ada, v_w_ada),
        "b_ada": step("b_ada", b_ada.reshape(6, D), grad_b_ada.reshape(6, D), m_b_ada.reshape(6, D), v_b_ada.reshape(6, D)),
        "g_norm_mix": step("g_norm_mix", g_norm_mix, small[G_GMIX:G_GMIX + 1], m_g_norm_mix, v_g_norm_mix),
        "w_in": step("w_in", w_in, gw_in[None], m_w_in, v_w_in),
        "conv_a_w": step("conv_a_w", conv_a_w, shard_cols(small[G_WA0:G_WA0 + 3])[None], m_conv_a_w, v_conv_a_w),
        "conv_b_w": step("conv_b_w", conv_b_w, shard_cols(small[G_WB0:G_WB0 + 4])[None], m_conv_b_w, v_conv_b_w),
        "conv_b_bias": step("conv_b_bias", conv_b_bias, small[G_CBB:G_CBB + 1], m_conv_b_bias, v_conv_b_bias),
        "w_rg_a": step("w_rg_a", w_rg_a, gw_rga.reshape(w_rg_a.shape), m_w_rg_a, v_w_rg_a),
        "b_rg_a": step("b_rg_a", b_rg_a, small[G_BA:G_BA + 1], m_b_rg_a, v_b_rg_a),
        "w_rg_x": step("w_rg_x", w_rg_x, gw_rgx.reshape(w_rg_x.shape), m_w_rg_x, v_w_rg_x),
        "b_rg_x": step("b_rg_x", b_rg_x, small[G_BX:G_BX + 1], m_b_rg_x, v_b_rg_x),
        "lru_lambda": step("lru_lambda", lru_lambda, small[G_LAM:G_LAM + 1], m_lru_lambda, v_lru_lambda),
        "w_out": step("w_out", w_out, gw_out[None], m_w_out, v_w_out),
        "g_norm_ffn": step("g_norm_ffn", g_norm_ffn, small[G_GFFN:G_GFFN + 1], m_g_norm_ffn, v_g_norm_ffn),
        "w_gate_up": step("w_gate_up", w_gate_up, gw_gu[None], m_w_gate_up, v_w_gate_up),
        "w_down": step("w_down", w_down, gw_dn[None], m_w_down, v_w_down),
        "g_norm_final": step("g_norm_final", g_norm_final.reshape(1, D), small[G_GFIN:G_GFIN + 1], m_g_norm_final.reshape(1, D),
                             v_g_norm_final.reshape(1, D)),
    }
    res["b_ada"] = tuple(a.reshape(1, 6 * D) for a in res["b_ada"])
    res["g_norm_final"] = tuple(a.reshape(D) for a in res["g_norm_final"])
    names = ["w_ada", "b_ada", "g_norm_mix", "w_in", "conv_a_w", "conv_b_w", "conv_b_bias", "w_rg_a", "b_rg_a", "w_rg_x",
             "b_rg_x", "lru_lambda", "w_out", "g_norm_ffn", "w_gate_up", "w_down", "g_norm_final"]
    loss = jnp.sum(small[G_LOSS])
    return (loss, grad_x[None], *[res[n][0] for n in names], *[res[n][1] for n in names],
            *[res[n][2] for n in names], *[res[n][3] for n in names])
```

```python
import functools

import jax
import jax.numpy as jnp
from jax import lax
from jax.experimental import pallas as pl
from jax.experimental.pallas import tpu as pltpu

F32 = jnp.float32
BF16 = jnp.bfloat16
MESH = pl.DeviceIdType.MESH

D = 1024
N_CHIPS = 4
N_DEV = 8
D_IN = 7 * D
C_IN = D_IN // N_CHIPS
D_FF = 2816
C_GU = 2 * D_FF // N_CHIPS
HEADS = 4
HB = D // HEADS
EPS = 1e-6
LRU_C = 8.0
ADAM_LR, ADAM_B1, ADAM_B2, ADAM_EPS, ADAM_WD, ADAM_STEP = 0.001, 0.9, 0.999, 1e-08, 0.01, 10
VMEM_LIMIT = 56 << 20

(V_SH1, V_SC1, V_GT1, V_SH2, V_SC2, V_GT2, V_GMIX, V_GFFN, V_GFIN, V_CBB, V_BA, V_BX, V_LAM,
 V_WA0, V_WA1, V_WA2, V_WB0, V_WB1, V_WB2, V_WB3) = range(20)
N_VEC = 24
(G_SH1, G_SC1, G_GT1, G_SH2, G_SC2, G_GT2, G_GMIX, G_CBB, G_BA, G_BX, G_LAM, G_GFFN, G_GFIN,
 G_WA0, G_WA1, G_WA2, G_WB0, G_WB1, G_WB2, G_WB3, G_LOSS) = range(21)
N_SMALL = 24

_VMEM = pl.BlockSpec(memory_space=pltpu.VMEM)
_ANY = pl.BlockSpec(memory_space=pl.ANY)


def _cparams(n_grid=1):
    return pltpu.CompilerParams(dimension_semantics=("arbitrary",) * n_grid, vmem_limit_bytes=VMEM_LIMIT)


def _after(deps, body):
    n = len(deps)
    return lambda *refs: body(*refs[n:])


def _rms(x):
    rstd = lax.rsqrt(jnp.mean(x * x, axis=-1, keepdims=True) + EPS)
    return x * rstd, rstd


def _rms_bwd(dxhat, xhat, rstd):
    return rstd * (dxhat - xhat * jnp.mean(dxhat * xhat, axis=-1, keepdims=True))


def _rowsum(v):
    return jnp.sum(v, axis=0, keepdims=True)


def _dot(a, b):
    return jnp.dot(a, b, preferred_element_type=F32)


def _dot_nt(a, b):
    return lax.dot_general(a, b, (((1,), (1,)), ((), ())), preferred_element_type=F32)


def _dot_tn(a, b):
    return lax.dot_general(a, b, (((0,), (0,)), ((), ())), preferred_element_type=F32)


def _gelu(x):
    k, c = 0.7978845608028654, 0.044715
    t = jnp.tanh(k * (x + c * x * x * x))
    return 0.5 * x * (1.0 + t), 0.5 * (1.0 + t) + 0.5 * x * (1.0 - t * t) * k * (1.0 + 3.0 * c * x * x)


def _log_sigmoid(lam):
    return jnp.minimum(lam, 0.0) - jnp.log1p(jnp.exp(-jnp.abs(lam)))


def _lru_gates(u, wa_ref, wx_ref, v_ref, row0):
    ub = u.astype(BF16)
    pre_a = jnp.concatenate([_dot(ub[:, h * HB:(h + 1) * HB], wa_ref[h]) for h in range(HEADS)], axis=1)
    pre_x = jnp.concatenate([_dot(ub[:, h * HB:(h + 1) * HB], wx_ref[h]) for h in range(HEADS)], axis=1)
    r = jax.nn.sigmoid(pre_a + v_ref[V_BA:V_BA + 1, :])
    ig = jax.nn.sigmoid(pre_x + v_ref[V_BX:V_BX + 1, :])
    log_a = LRU_C * r * _log_sigmoid(v_ref[V_LAM:V_LAM + 1, :])
    a = jnp.exp(log_a)
    x2 = 2.0 * log_a
    m2 = jnp.where(x2 > -0.03, -x2 * (1.0 + x2 * (0.5 + x2 * (1.0 / 6.0 + x2 * (1.0 / 24.0)))), 1.0 - a * a)
    mult = jnp.where(row0, 1.0, jnp.sqrt(jnp.maximum(m2, 0.0)))
    return r, ig, a, mult


def _fwd_in(x, vecs, w_in_g, ts, deps=()):
    s = x.shape[0]

    def body(x_ref, v_ref, w_ref, h1_ref, proj_ref):
        xhat, _ = _rms(x_ref[...])
        h = xhat * v_ref[V_GMIX:V_GMIX + 1, :] * (1.0 + v_ref[V_SC1:V_SC1 + 1, :]) + v_ref[V_SH1:V_SH1 + 1, :]
        hb = h.astype(BF16)
        h1_ref[...] = hb
        for k in range(N_CHIPS):
            proj_ref[:, k * C_IN:(k + 1) * C_IN] = _dot(hb, w_ref[k]).astype(BF16)

    return pl.pallas_call(
        _after(deps, body), grid=(s // ts,),
        out_shape=(jax.ShapeDtypeStruct((s, D), BF16), jax.ShapeDtypeStruct((s, D_IN), BF16)),
        in_specs=[_ANY] * len(deps) + [pl.BlockSpec((ts, D), lambda i: (i, 0)), _VMEM, _VMEM],
        out_specs=[pl.BlockSpec((ts, D), lambda i: (i, 0)), pl.BlockSpec((ts, D_IN), lambda i: (i, 0))],
        compiler_params=_cparams(), name="fwd_in")(*deps, x, vecs, w_in_g)


def _fwd_mix(proj, x, vecs, w_rga, w_rgx, w_out, ts, deps=()):
    s = x.shape[0]

    def body(proj_ref, x_ref, v_ref, wa_ref, wx_ref, wo_ref, x1_ref, mg_ref, z1_ref, u_ref, h_ref,
             ua_buf, rx_buf, a_buf, b_buf, hcarry):
        i = pl.program_id(0)

        @pl.when(i == 0)
        def _():
            ua_buf[0:8, :] = jnp.zeros((8, D), F32)
            rx_buf[0:8, :] = jnp.zeros((8, D), F32)
            hcarry[...] = jnp.zeros((8, D), F32)

        def seg(j):
            return proj_ref[:, j * D:(j + 1) * D].astype(F32)

        def vrow(j):
            return v_ref[j:j + 1, :]

        cb, cc, cx, rx, rg, ga, gb = (seg(j) for j in range(7))
        ua = cc * cx
        ua_buf[8:ts + 8, :] = ua
        rx_buf[8:ts + 8, :] = rx
        va = vrow(V_WA2) * ua + vrow(V_WA1) * ua_buf[pl.ds(7, ts), :] + vrow(V_WA0) * ua_buf[pl.ds(6, ts), :]
        u = (vrow(V_WB3) * rx + vrow(V_WB2) * rx_buf[pl.ds(7, ts), :] + vrow(V_WB1) * rx_buf[pl.ds(6, ts), :]
             + vrow(V_WB0) * rx_buf[pl.ds(5, ts), :] + vrow(V_CBB))
        ua_buf[0:8, :] = ua_buf[ts:ts + 8, :]
        rx_buf[0:8, :] = rx_buf[ts:ts + 8, :]
        u_ref[...] = u

        rows = lax.broadcasted_iota(jnp.int32, (ts, D), 0)
        row0 = jnp.logical_and(rows == 0, i == 0)
        _, ig, a, mult = _lru_gates(u, wa_ref, wx_ref, v_ref, row0)
        bx = mult * (ig * u)

        sub = rows % 8
        for d in (1, 2, 4):
            m = sub >= d
            bx = jnp.where(m, a * pltpu.roll(bx, d, axis=0) + bx, bx)
            a = jnp.where(m, a * pltpu.roll(a, d, axis=0), a)
        a_buf[...] = a
        b_buf[...] = bx

        def grp(g, carry):
            off = pl.multiple_of(g * 8, 8)
            h_ref[pl.ds(off, 8), :] = a_buf[pl.ds(off, 8), :] * carry + b_buf[pl.ds(off, 8), :]
            return jnp.broadcast_to(h_ref[pl.ds(off + 7, 1), :], (8, D))

        hcarry[...] = lax.fori_loop(0, ts // 8, grp, hcarry[...])
        h = h_ref[...]

        gel, _ = _gelu(rg)
        merged = (jax.nn.sigmoid(ga) * (cb * va) + jax.nn.sigmoid(gb) * (h * gel)).astype(BF16)
        mg_ref[...] = merged
        z1 = _dot(merged, wo_ref[...])
        z1_ref[...] = z1.astype(BF16)
        x1_ref[...] = x_ref[...] + vrow(V_GT1) * z1

    row = lambda i: (i, 0)
    return pl.pallas_call(
        _after(deps, body), grid=(s // ts,),
        out_shape=(jax.ShapeDtypeStruct((s, D), F32), jax.ShapeDtypeStruct((s, D), BF16), jax.ShapeDtypeStruct((s, D), BF16),
                   jax.ShapeDtypeStruct((s, D), F32), jax.ShapeDtypeStruct((s, D), F32)),
        in_specs=[_ANY] * len(deps) + [pl.BlockSpec((ts, D_IN), row), pl.BlockSpec((ts, D), row), _VMEM, _VMEM, _VMEM, _VMEM],
        out_specs=[pl.BlockSpec((ts, D), row)] * 5,
        scratch_shapes=[pltpu.VMEM((ts + 8, D), F32), pltpu.VMEM((ts + 8, D), F32), pltpu.VMEM((ts, D), F32),
                        pltpu.VMEM((ts, D), F32), pltpu.VMEM((8, D), F32)],
        compiler_params=_cparams(), name="fwd_mix")(*deps, proj, x, vecs, w_rga, w_rgx, w_out)


def _ffn_loss(x1, target, vecs, w_gu_g, w_dn, ts):
    s = x1.shape[0]

    def body(x1_ref, t_ref, v_ref, wgu_ref, wdn_ref, dx1_ref, h2_ref, act_ref, dz2_ref, dgu_ref, sm_ref):
        @pl.when(pl.program_id(0) == 0)
        def _():
            sm_ref[...] = jnp.zeros((N_SMALL, D), F32)

        def vrow(j):
            return v_ref[j:j + 1, :]

        def acc(j, val):
            sm_ref[j:j + 1, :] += _rowsum(val)

        x1 = x1_ref[...]
        xh1, rstd1 = _rms(x1)
        n2 = xh1 * vrow(V_GFFN)
        h2 = (n2 * (1.0 + vrow(V_SC2)) + vrow(V_SH2)).astype(BF16)
        h2_ref[...] = h2
        g = jnp.concatenate([_dot(h2, wgu_ref[0]), _dot(h2, wgu_ref[1])], axis=1)
        up = jnp.concatenate([_dot(h2, wgu_ref[2]), _dot(h2, wgu_ref[3])], axis=1)
        sg = jax.nn.sigmoid(g)
        silu = g * sg
        act = (silu * up).astype(BF16)
        act_ref[...] = act
        z2 = _dot(act, wdn_ref[...])
        x2 = x1 + vrow(V_GT2) * z2
        xh2, rstd2 = _rms(x2)
        err = xh2 * vrow(V_GFIN) - t_ref[...]
        acc(G_LOSS, (0.5 / D) * err * err)
        dy = err * (1.0 / D)
        acc(G_GFIN, dy * xh2)
        dx2 = _rms_bwd(dy * vrow(V_GFIN), xh2, rstd2)
        acc(G_GT2, dx2 * z2)
        dz2 = (vrow(V_GT2) * dx2).astype(BF16)
        dz2_ref[...] = dz2
        dact = _dot_nt(dz2, wdn_ref[...])
        dgate = (dact * up * (sg * (1.0 + g * (1.0 - sg)))).astype(BF16)
        dup = (dact * silu).astype(BF16)
        dgu_ref[:, 0:D_FF] = dgate
        dgu_ref[:, D_FF:2 * D_FF] = dup
        dh2 = (_dot_nt(dgate[:, 0:C_GU], wgu_ref[0]) + _dot_nt(dgate[:, C_GU:2 * C_GU], wgu_ref[1])
               + _dot_nt(dup[:, 0:C_GU], wgu_ref[2]) + _dot_nt(dup[:, C_GU:2 * C_GU], wgu_ref[3]))
        acc(G_SH2, dh2)
        acc(G_SC2, dh2 * n2)
        dn2 = dh2 * (1.0 + vrow(V_SC2))
        acc(G_GFFN, dn2 * xh1)
        dx1_ref[...] = dx2 + _rms_bwd(dn2 * vrow(V_GFFN), xh1, rstd1)

    row = lambda i: (i, 0)
    return pl.pallas_call(
        body, grid=(s // ts,),
        out_shape=(jax.ShapeDtypeStruct((s, D), F32), jax.ShapeDtypeStruct((s, D), BF16), jax.ShapeDtypeStruct((s, D_FF), BF16),
                   jax.ShapeDtypeStruct((s, D), BF16), jax.ShapeDtypeStruct((s, 2 * D_FF), BF16),
                   jax.ShapeDtypeStruct((N_SMALL, D), F32)),
        in_specs=[pl.BlockSpec((ts, D), row), pl.BlockSpec((ts, D), row), _VMEM, _VMEM, _VMEM],
        out_specs=[pl.BlockSpec((ts, D), row), pl.BlockSpec((ts, D), row), pl.BlockSpec((ts, D_FF), row),
                   pl.BlockSpec((ts, D), row), pl.BlockSpec((ts, 2 * D_FF), row), pl.BlockSpec((N_SMALL, D), lambda i: (0, 0))],
        compiler_params=_cparams(), name="ffn_loss")(x1, target, vecs, w_gu_g, w_dn)


def _bwd_mix(dx1, z1, proj, u, h, vecs, w_rga, w_rgx, w_out, ts, deps=()):
    s = dx1.shape[0]
    nt = s // ts

    def body(dx1_ref, z1_ref, proj_ref, u_ref, h_ref, hh_ref, cch_ref, cxh_ref, v_ref, wa_ref, wx_ref, wo_ref,
             dproj_ref, dz1_ref, sm_ref, dwa_ref, dwx_ref,
             ua_buf, h_buf, a_buf, dva_buf, du_buf, ca_buf, cb_buf, l_buf, lcarry):
        i = pl.program_id(0)
        first_tile = i == nt - 1

        @pl.when(i == 0)
        def _():
            a_buf[ts:ts + 8, :] = jnp.zeros((8, D), F32)
            dva_buf[ts:ts + 8, :] = jnp.zeros((8, D), F32)
            du_buf[ts:ts + 8, :] = jnp.zeros((8, D), F32)
            lcarry[...] = jnp.zeros((8, D), F32)
            sm_ref[...] = jnp.zeros((N_SMALL, D), F32)
            dwa_ref[...] = jnp.zeros((HEADS, HB, HB), F32)
            dwx_ref[...] = jnp.zeros((HEADS, HB, HB), F32)

        def seg(j):
            return proj_ref[:, j * D:(j + 1) * D].astype(F32)

        def vrow(j):
            return v_ref[j:j + 1, :]

        def acc(j, val):
            sm_ref[j:j + 1, :] += _rowsum(val)

        cb, cc, cx, rx, rg, ga, gb = (seg(j) for j in range(7))
        ua = cc * cx
        ua_halo = cch_ref[8:16, :].astype(F32) * cxh_ref[8:16, :].astype(F32)
        ua_buf[0:8, :] = jnp.where(first_tile, 0.0, ua_halo)
        ua_buf[8:ts + 8, :] = ua
        va = vrow(V_WA2) * ua + vrow(V_WA1) * ua_buf[pl.ds(7, ts), :] + vrow(V_WA0) * ua_buf[pl.ds(6, ts), :]
        u = u_ref[...]
        h = h_ref[...]
        rows = lax.broadcasted_iota(jnp.int32, (ts, D), 0)
        row0 = jnp.logical_and(rows == 0, first_tile)
        r, ig, a, mult = _lru_gates(u, wa_ref, wx_ref, v_ref, row0)
        sga = jax.nn.sigmoid(ga)
        sgb = jax.nn.sigmoid(gb)
        gel, dgel = _gelu(rg)

        dx1 = dx1_ref[...]
        acc(G_GT1, dx1 * z1_ref[...].astype(F32))
        dz1 = (vrow(V_GT1) * dx1).astype(BF16)
        dz1_ref[...] = dz1
        dmg = _dot_nt(dz1, wo_ref[...])
        dproj_ref[:, 5 * D:6 * D] = (dmg * (cb * va) * sga * (1.0 - sga)).astype(BF16)
        dproj_ref[:, 6 * D:7 * D] = (dmg * (h * gel) * sgb * (1.0 - sgb)).astype(BF16)
        dya = dmg * sga
        dyb = dmg * sgb

        dproj_ref[:, 0:D] = (dya * va).astype(BF16)
        dva = dya * cb
        dva_buf[0:ts, :] = dva
        dva1 = dva_buf[pl.ds(1, ts), :]
        dva2 = dva_buf[pl.ds(2, ts), :]
        dua = vrow(V_WA2) * dva + vrow(V_WA1) * dva1 + vrow(V_WA0) * dva2
        acc(G_WA2, ua * dva)
        acc(G_WA1, ua * dva1)
        acc(G_WA0, ua * dva2)
        dva_buf[ts:ts + 8, :] = dva_buf[0:8, :]
        dproj_ref[:, D:2 * D] = (dua * cx).astype(BF16)
        dproj_ref[:, 2 * D:3 * D] = (dua * cc).astype(BF16)

        dproj_ref[:, 4 * D:5 * D] = (dyb * h * dgel).astype(BF16)
        a_buf[0:ts, :] = a
        sa = a_buf[pl.ds(1, ts), :]
        sb = dyb * gel
        sub = rows % 8
        for d in (1, 2, 4):
            m = sub < 8 - d
            sb = jnp.where(m, sb + sa * pltpu.roll(sb, ts - d, axis=0), sb)
            sa = jnp.where(m, sa * pltpu.roll(sa, ts - d, axis=0), sa)
        ca_buf[...] = sa
        cb_buf[...] = sb

        def grp(k, carry):
            off = pl.multiple_of((ts // 8 - 1 - k) * 8, 8)
            l_buf[pl.ds(off, 8), :] = ca_buf[pl.ds(off, 8), :] * carry + cb_buf[pl.ds(off, 8), :]
            return jnp.broadcast_to(l_buf[pl.ds(off, 1), :], (8, D))

        lcarry[...] = lax.fori_loop(0, ts // 8, grp, lcarry[...])
        a_buf[ts:ts + 8, :] = a_buf[0:8, :]
        lam = l_buf[...]

        h_buf[0:8, :] = jnp.where(first_tile, 0.0, hh_ref[...])
        h_buf[8:ts + 8, :] = h
        da = lam * h_buf[pl.ds(7, ts), :]
        dmult = jnp.where(row0, 0.0, lam * (ig * u))
        di = lam * mult * u
        du = lam * mult * ig
        dlog_a = da * a - dmult * (a * a) / mult
        lam_p = vrow(V_LAM)
        dr = dlog_a * (LRU_C * _log_sigmoid(lam_p))
        sm_ref[G_LAM:G_LAM + 1, :] += _rowsum(dlog_a * r) * (LRU_C * jax.nn.sigmoid(-lam_p))
        dpa = dr * r * (1.0 - r)
        dpx = di * ig * (1.0 - ig)
        acc(G_BA, dpa)
        acc(G_BX, dpx)
        dpab = dpa.astype(BF16)
        dpxb = dpx.astype(BF16)
        ub = u.astype(BF16)
        back = []
        for hd in range(HEADS):
            cols = slice(hd * HB, (hd + 1) * HB)
            back.append(_dot_nt(dpab[:, cols], wa_ref[hd]) + _dot_nt(dpxb[:, cols], wx_ref[hd]))
            dwa_ref[hd] += _dot_tn(ub[:, cols], dpab[:, cols])
            dwx_ref[hd] += _dot_tn(ub[:, cols], dpxb[:, cols])
        du = du + jnp.concatenate(back, axis=1)

        acc(G_CBB, du)
        du_buf[0:ts, :] = du
        du1 = du_buf[pl.ds(1, ts), :]
        du2 = du_buf[pl.ds(2, ts), :]
        du3 = du_buf[pl.ds(3, ts), :]
        dproj_ref[:, 3 * D:4 * D] = (vrow(V_WB3) * du + vrow(V_WB2) * du1 + vrow(V_WB1) * du2 + vrow(V_WB0) * du3).astype(BF16)
        acc(G_WB3, rx * du)
        acc(G_WB2, rx * du1)
        acc(G_WB1, rx * du2)
        acc(G_WB0, rx * du3)
        du_buf[ts:ts + 8, :] = du_buf[0:8, :]

    rev = lambda i: (nt - 1 - i, 0)
    halo8 = lambda i: (jnp.maximum((nt - 1 - i) * (ts // 8) - 1, 0), 0)
    const2 = lambda i: (0, 0)
    const3 = lambda i: (0, 0, 0)
    return pl.pallas_call(
        _after(deps, body), grid=(nt,),
        out_shape=(jax.ShapeDtypeStruct((s, D_IN), BF16), jax.ShapeDtypeStruct((s, D), BF16),
                   jax.ShapeDtypeStruct((N_SMALL, D), F32), jax.ShapeDtypeStruct((HEADS, HB, HB), F32),
                   jax.ShapeDtypeStruct((HEADS, HB, HB), F32)),
        in_specs=[_ANY] * len(deps) + [pl.BlockSpec((ts, D), rev), pl.BlockSpec((ts, D), rev), pl.BlockSpec((ts, D_IN), rev),
                  pl.BlockSpec((ts, D), rev), pl.BlockSpec((ts, D), rev), pl.BlockSpec((8, D), halo8),
                  pl.BlockSpec((16, D), lambda i: (jnp.maximum((nt - 1 - i) * (ts // 16) - 1, 0), 1)),
                  pl.BlockSpec((16, D), lambda i: (jnp.maximum((nt - 1 - i) * (ts // 16) - 1, 0), 2)),
                  _VMEM, _VMEM, _VMEM, _VMEM],
        out_specs=[pl.BlockSpec((ts, D_IN), rev), pl.BlockSpec((ts, D), rev), pl.BlockSpec((N_SMALL, D), const2),
                   pl.BlockSpec((HEADS, HB, HB), const3), pl.BlockSpec((HEADS, HB, HB), const3)],
        scratch_shapes=[pltpu.VMEM((ts + 8, D), F32)] * 5 + [pltpu.VMEM((ts, D), F32)] * 3 + [pltpu.VMEM((8, D), F32)],
        compiler_params=_cparams(), name="bwd_mix")(*deps, dx1, z1, proj, u, h, h, proj, proj, vecs, w_rga, w_rgx, w_out)


def _bwd_in(dproj, x, dx1, vecs, w_in_g, ts, deps=()):
    s = x.shape[0]

    def body(dp_ref, x_ref, dx1_ref, v_ref, w_ref, gx_ref, sm_ref):
        @pl.when(pl.program_id(0) == 0)
        def _():
            sm_ref[...] = jnp.zeros((N_SMALL, D), F32)

        def vrow(j):
            return v_ref[j:j + 1, :]

        dh1 = _dot_nt(dp_ref[:, 0:C_IN], w_ref[0])
        for k in range(1, N_CHIPS):
            dh1 += _dot_nt(dp_ref[:, k * C_IN:(k + 1) * C_IN], w_ref[k])
        xh, rstd = _rms(x_ref[...])
        sm_ref[G_SH1:G_SH1 + 1, :] += _rowsum(dh1)
        sm_ref[G_SC1:G_SC1 + 1, :] += _rowsum(dh1 * (xh * vrow(V_GMIX)))
        dn1 = dh1 * (1.0 + vrow(V_SC1))
        sm_ref[G_GMIX:G_GMIX + 1, :] += _rowsum(dn1 * xh)
        gx_ref[...] = dx1_ref[...] + _rms_bwd(dn1 * vrow(V_GMIX), xh, rstd)

    row = lambda i: (i, 0)
    return pl.pallas_call(
        _after(deps, body), grid=(s // ts,),
        out_shape=(jax.ShapeDtypeStruct((s, D), F32), jax.ShapeDtypeStruct((N_SMALL, D), F32)),
        in_specs=[_ANY] * len(deps) + [pl.BlockSpec((ts, D_IN), row), pl.BlockSpec((ts, D), row), pl.BlockSpec((ts, D), row),
                                       _VMEM, _VMEM],
        out_specs=[pl.BlockSpec((ts, D), row), pl.BlockSpec((N_SMALL, D), lambda i: (0, 0))],
        compiler_params=_cparams(), name="bwd_in")(*deps, dproj, x, dx1, vecs, w_in_g)


def _grad_w(a, b, n_col_blocks, ts, name, deps=()):
    s, m = a.shape
    tn = b.shape[1] // n_col_blocks
    n_steps = s // ts

    def body(a_ref, b_ref, o_ref, acc_ref):
        k = pl.program_id(1)

        @pl.when(k == 0)
        def _():
            acc_ref[...] = jnp.zeros((m, tn), F32)

        acc_ref[...] += _dot_tn(a_ref[...], b_ref[...])

        @pl.when(k == n_steps - 1)
        def _():
            o_ref[...] = acc_ref[...].astype(BF16)

    return pl.pallas_call(
        _after(deps, body), grid=(n_col_blocks, n_steps),
        out_shape=jax.ShapeDtypeStruct((n_col_blocks, m, tn), BF16),
        in_specs=[_ANY] * len(deps) + [pl.BlockSpec((ts, m), lambda n, k: (k, 0)), pl.BlockSpec((ts, tn), lambda n, k: (k, n))],
        out_specs=pl.BlockSpec((None, m, tn), lambda n, k: (n, 0, 0)),
        scratch_shapes=[pltpu.VMEM((m, tn), F32)],
        compiler_params=_cparams(2), name=name)(*deps, a, b)


def _ada_fwd(c_all, w_ada, b_ada):
    n = w_ada.shape[1]

    def body(c_ref, w_ref, b_ref, o_ref, ca_ref):
        c = c_ref[...]
        ca = c * jax.nn.sigmoid(c)
        ca_ref[...] = ca
        o_ref[...] = jnp.dot(ca, w_ref[...], preferred_element_type=F32, precision=lax.Precision.HIGHEST) + b_ref[...]

    return pl.pallas_call(
        body, out_shape=(jax.ShapeDtypeStruct((N_DEV, n), F32), jax.ShapeDtypeStruct((N_DEV, D), F32)),
        in_specs=[_VMEM] * 3, out_specs=[_VMEM] * 2, compiler_params=_cparams(0), name="ada_fwd")(c_all, w_ada, b_ada)


def _ada_bwd(c_act, dmod):
    n = dmod.shape[1]

    def body(c_ref, d_ref, o_ref):
        o_ref[...] = lax.dot_general(c_ref[...], d_ref[...], (((0,), (0,)), ((), ())), preferred_element_type=F32,
                                     precision=lax.Precision.HIGHEST)

    return pl.pallas_call(
        body, out_shape=jax.ShapeDtypeStruct((D, n), F32), in_specs=[_VMEM] * 2, out_specs=_VMEM,
        compiler_params=_cparams(0), name="ada_bwd")(c_act, dmod)


def _sum_small(parts):
    def body(p_ref, o_ref, d_ref):
        tot = None
        for dev in range(N_DEV):
            mine = p_ref[dev, 0] + p_ref[dev, 1] + p_ref[dev, 2]
            d_ref[dev] = mine[0:8, :]
            tot = mine if tot is None else tot + mine
        o_ref[...] = tot

    return pl.pallas_call(
        body, out_shape=(jax.ShapeDtypeStruct((N_SMALL, D), F32), jax.ShapeDtypeStruct((N_DEV, 8, D), F32)),
        in_specs=[_VMEM], out_specs=[_VMEM] * 2, compiler_params=_cparams(0), name="sum_small")(parts)


def _adamw(w, g, m, v, name, deps=()):
    rows, cols = w.shape
    tr = 128 if rows % 128 == 0 else (64 if rows % 64 == 0 else rows)

    def body(w_ref, g_ref, m_ref, v_ref, d_ref, nm_ref, nv_ref):
        g_ = g_ref[...]
        m_ = ADAM_B1 * m_ref[...] + (1.0 - ADAM_B1) * g_
        v_ = ADAM_B2 * v_ref[...] + (1.0 - ADAM_B2) * (g_ * g_)
        nm_ref[...] = m_
        nv_ref[...] = v_
        m_hat = m_ / (1.0 - ADAM_B1 ** ADAM_STEP)
        v_hat = v_ / (1.0 - ADAM_B2 ** ADAM_STEP)
        d_ref[...] = -ADAM_LR * (m_hat / (jnp.sqrt(v_hat) + ADAM_EPS) + ADAM_WD * w_ref[...])

    spec = pl.BlockSpec((tr, cols), lambda i: (i, 0))
    return pl.pallas_call(
        _after(deps, body), grid=(rows // tr,), out_shape=(jax.ShapeDtypeStruct((rows, cols), F32),) * 3,
        in_specs=[_ANY] * len(deps) + [spec] * 4, out_specs=[spec] * 3, compiler_params=_cparams(), name=name)(*deps, w, g, m, v)


def _add_halves(g, recv, c_idx, name):
    n, _, r2, cols = g.shape

    def body(c_ref, g_ref, r_ref, o_ref):
        o_ref[...] = (g_ref[...].astype(F32) + r_ref[...].astype(F32)).astype(BF16)

    return pl.pallas_call(
        body,
        grid_spec=pltpu.PrefetchScalarGridSpec(
            num_scalar_prefetch=1, grid=(n,),
            in_specs=[pl.BlockSpec((None, None, r2, cols), lambda k, c: (k, c[0], 0, 0)),
                      pl.BlockSpec((None, r2, cols), lambda k, c: (k, 0, 0))],
            out_specs=pl.BlockSpec((None, r2, cols), lambda k, c: (k, 0, 0))),
        out_shape=jax.ShapeDtypeStruct((n, r2, cols), BF16), compiler_params=_cparams(), name=name)(c_idx, g, recv)


def _sum_chips(parts, name):
    n, r2, cols = parts.shape
    tr = r2 // 2 if (r2 // 2) % 16 == 0 else r2

    def body(p_ref, o_ref):
        o_ref[...] = ((p_ref[0].astype(F32) + p_ref[1].astype(F32)) + p_ref[2].astype(F32)) + p_ref[3].astype(F32)

    return pl.pallas_call(
        body, grid=(r2 // tr,), out_shape=jax.ShapeDtypeStruct((r2, cols), F32),
        in_specs=[pl.BlockSpec((n, tr, cols), lambda i: (0, i, 0))], out_specs=pl.BlockSpec((tr, cols), lambda i: (i, 0)),
        compiler_params=_cparams(), name=name)(parts)


def _place():
    x, y, c = lax.axis_index("x"), lax.axis_index("y"), lax.axis_index("c")
    return x, y, c, 2 * x + y


def _flip(v, bit):
    return 1 - v if bit else v


def _allgather8(v, name):
    r, n = v.shape

    def body(v_ref, out_ref, send_sems, recv_sems, local_sem):
        x, y, c, _ = _place()
        me = 4 * x + 2 * y + c
        mine = pltpu.make_async_copy(v_ref, out_ref.at[me], local_sem)
        mine.start()
        sends = []
        for rel in range(1, N_DEV):
            peer = (_flip(x, rel & 4), _flip(y, rel & 2), _flip(c, rel & 1))
            cp = pltpu.make_async_remote_copy(v_ref, out_ref.at[me], send_sems.at[rel - 1], recv_sems.at[rel - 1],
                                              device_id=peer, device_id_type=MESH)
            cp.start()
            sends.append(cp)
        for rel in range(1, N_DEV):
            peer = (_flip(x, rel & 4), _flip(y, rel & 2), _flip(c, rel & 1))
            peer_idx = 4 * peer[0] + 2 * peer[1] + peer[2]
            pltpu.make_async_remote_copy(v_ref, out_ref.at[peer_idx], send_sems.at[rel - 1], recv_sems.at[rel - 1],
                                         device_id=peer, device_id_type=MESH).wait_recv()
        for cp in sends:
            cp.wait_send()
        mine.wait()

    return pl.pallas_call(
        body, out_shape=jax.ShapeDtypeStruct((N_DEV, r, n), F32), in_specs=[_VMEM], out_specs=_VMEM,
        scratch_shapes=[pltpu.SemaphoreType.DMA((N_DEV - 1,)), pltpu.SemaphoreType.DMA((N_DEV - 1,)), pltpu.SemaphoreType.DMA(())],
        name=name)(v)


def _gather_weights(shards):
    nw = len(shards)

    def body(*refs):
        w_refs, out_refs = refs[:nw], refs[nw:2 * nw]
        send_sems, recv_sems = refs[2 * nw:]
        x, y, c, p = _place()
        sibling = (x, y, 1 - c)
        sends = []
        for j in range(1, N_CHIPS):
            peer = (_flip(x, j & 2), _flip(y, j & 1), c)
            for w in range(nw):
                cp = pltpu.make_async_remote_copy(w_refs[w].at[c], out_refs[w].at[p, c], send_sems.at[w * 6 + j - 1],
                                                  recv_sems.at[w * 6 + j - 1], device_id=peer, device_id_type=MESH)
                cp.start()
                sends.append(cp)
        for j in range(1, N_CHIPS):
            peer = (_flip(x, j & 2), _flip(y, j & 1), c)
            q = 2 * peer[0] + peer[1]
            for w in range(nw):
                pltpu.make_async_remote_copy(w_refs[w].at[c], out_refs[w].at[q, c], send_sems.at[w * 6 + j - 1],
                                             recv_sems.at[w * 6 + j - 1], device_id=peer, device_id_type=MESH).wait_recv()
                cp = pltpu.make_async_remote_copy(out_refs[w].at[q, c], out_refs[w].at[q, c], send_sems.at[w * 6 + 2 + j],
                                                  recv_sems.at[w * 6 + 2 + j], device_id=sibling, device_id_type=MESH)
                cp.start()
                sends.append(cp)
        for j in range(1, N_CHIPS):
            q = 2 * _flip(x, j & 2) + _flip(y, j & 1)
            for w in range(nw):
                pltpu.make_async_remote_copy(out_refs[w].at[q, 1 - c], out_refs[w].at[q, 1 - c], send_sems.at[w * 6 + 2 + j],
                                             recv_sems.at[w * 6 + 2 + j], device_id=sibling, device_id_type=MESH).wait_recv()
        for cp in sends:
            cp.wait_send()

    return pl.pallas_call(
        body, out_shape=tuple(jax.ShapeDtypeStruct((N_CHIPS,) + s.shape, s.dtype) for s in shards),
        in_specs=[_ANY] * nw, out_specs=[_ANY] * nw,
        scratch_shapes=[pltpu.SemaphoreType.DMA((6 * nw,)), pltpu.SemaphoreType.DMA((6 * nw,))],
        name="gather_weights")(*shards)


def _swap_halves(grads):
    nw = len(grads)

    def body(*refs):
        g_refs, out_refs = refs[:nw], refs[nw:2 * nw]
        send_sems, recv_sems = refs[2 * nw:]
        x, y, c, _ = _place()
        sibling = (x, y, 1 - c)
        sends = []
        for w in range(nw):
            for k in range(N_CHIPS):
                cp = pltpu.make_async_remote_copy(g_refs[w].at[k, 1 - c], out_refs[w].at[k], send_sems.at[w * N_CHIPS + k],
                                                  recv_sems.at[w * N_CHIPS + k], device_id=sibling, device_id_type=MESH)
                cp.start()
                sends.append(cp)
        for cp in sends:
            cp.wait_recv()
        for cp in sends:
            cp.wait_send()

    return pl.pallas_call(
        body, out_shape=tuple(jax.ShapeDtypeStruct((N_CHIPS,) + g.shape[2:], g.dtype) for g in grads),
        in_specs=[_ANY] * nw, out_specs=[_ANY] * nw,
        scratch_shapes=[pltpu.SemaphoreType.DMA((N_CHIPS * nw,)), pltpu.SemaphoreType.DMA((N_CHIPS * nw,))],
        name="swap_halves")(*grads)


def _scatter_chips(parts):
    nw = len(parts)

    def body(*refs):
        p_refs, out_refs = refs[:nw], refs[nw:2 * nw]
        send_sems, recv_sems = refs[2 * nw:]
        x, y, c, p = _place()
        sends = []
        for j in range(1, N_CHIPS):
            peer = (_flip(x, j & 2), _flip(y, j & 1), c)
            q = 2 * peer[0] + peer[1]
            for w in range(nw):
                cp = pltpu.make_async_remote_copy(p_refs[w].at[q], out_refs[w].at[p], send_sems.at[w * 3 + j - 1],
                                                  recv_sems.at[w * 3 + j - 1], device_id=peer, device_id_type=MESH)
                cp.start()
                sends.append(cp)
        for j in range(1, N_CHIPS):
            peer = (_flip(x, j & 2), _flip(y, j & 1), c)
            q = 2 * peer[0] + peer[1]
            for w in range(nw):
                pltpu.make_async_remote_copy(p_refs[w].at[q], out_refs[w].at[q], send_sems.at[w * 3 + j - 1],
                                             recv_sems.at[w * 3 + j - 1], device_id=peer, device_id_type=MESH).wait_recv()
        for cp in sends:
            cp.wait_send()

    return pl.pallas_call(
        body, out_shape=tuple(jax.ShapeDtypeStruct(s.shape, s.dtype) for s in parts),
        in_specs=[_ANY] * nw, out_specs=[_ANY] * nw,
        scratch_shapes=[pltpu.SemaphoreType.DMA((3 * nw,)), pltpu.SemaphoreType.DMA((3 * nw,))],
        name="scatter_chips")(*parts)


def _share_halves(halves):
    nw = len(halves)

    def body(*refs):
        h_refs, out_refs = refs[:nw], refs[nw:2 * nw]
        send_sems, recv_sems = refs[2 * nw:]
        x, y, c, _ = _place()
        sends = []
        for w in range(nw):
            cp = pltpu.make_async_remote_copy(h_refs[w], out_refs[w], send_sems.at[w], recv_sems.at[w],
                                              device_id=(x, y, 1 - c), device_id_type=MESH)
            cp.start()
            sends.append(cp)
        for cp in sends:
            cp.wait_recv()
        for cp in sends:
            cp.wait_send()

    return pl.pallas_call(
        body, out_shape=tuple(jax.ShapeDtypeStruct(s.shape, s.dtype) for s in halves),
        in_specs=[_ANY] * nw, out_specs=[_ANY] * nw,
        scratch_shapes=[pltpu.SemaphoreType.DMA((nw,)), pltpu.SemaphoreType.DMA((nw,))],
        name="share_halves")(*halves)


_HBM = pl.BlockSpec(memory_space=pltpu.HBM)
_SEM = pl.BlockSpec(memory_space=pltpu.SEMAPHORE)
_EFFECT = pltpu.SideEffectType.DATAFLOW_SIDE_EFFECTING


def _xchg_start(name, plan, n_copies, srcs, lands, after=()):
    bufs = list(srcs) + list(lands)
    ns, nb = len(srcs), len(srcs) + len(lands)

    def body(*refs):
        send_sems, recv_sems, token = refs[nb + len(after)], refs[nb + len(after) + 1], refs[-1]
        for i, (src, dst, peer, _) in enumerate(plan(_place(), refs[:ns], refs[ns:nb])):
            pltpu.make_async_remote_copy(src, dst, send_sems.at[i], recv_sems.at[i], device_id=peer, device_id_type=MESH).start()
        token[...] = jnp.zeros_like(token)

    out = pl.pallas_call(
        body, name=name,
        out_shape=(pltpu.SemaphoreType.DMA((n_copies,)), pltpu.SemaphoreType.DMA((n_copies,)),
                   *[pltpu.HBM(a.shape, a.dtype) for a in bufs], jax.ShapeDtypeStruct((8, 128), F32)),
        in_specs=[_HBM] * nb + [_ANY] * len(after), out_specs=(_SEM, _SEM, *[_HBM] * nb, _VMEM),
        input_output_aliases={i: 2 + i for i in range(nb)},
        compiler_params=pltpu.CompilerParams(has_side_effects=_EFFECT),
    )(*[pltpu.with_memory_space_constraint(a, pltpu.HBM) for a in bufs], *after)
    return (out[0], out[1]), out[2:2 + ns], out[2 + ns:2 + nb], out[-1]


def _xchg_wait(name, plan, sems, srcs, lands, after):
    bufs = list(srcs) + list(lands)
    ns, nb = len(srcs), len(srcs) + len(lands)

    def body(*refs):
        send_sems, recv_sems = refs[nb], refs[nb + 1]
        for i, (src, _, peer, mine) in enumerate(plan(_place(), refs[:ns], refs[ns:nb])):
            cp = pltpu.make_async_remote_copy(src, mine, send_sems.at[i], recv_sems.at[i], device_id=peer, device_id_type=MESH)
            cp.wait_send()
            cp.wait_recv()

    out = pl.pallas_call(
        body, name=name, out_shape=tuple(pltpu.HBM(a.shape, a.dtype) for a in bufs),
        in_specs=[_HBM] * nb + [_SEM, _SEM] + [_ANY] * len(after), out_specs=tuple([_HBM] * nb),
        input_output_aliases={i: i for i in range(nb)},
        compiler_params=pltpu.CompilerParams(has_side_effects=_EFFECT),
    )(*bufs, *sems, *after)
    return out[:ns], out[ns:]


def _other_chips(place):
    x, y, c, _ = place
    return [((_flip(x, j & 2), _flip(y, j & 1), c), 2 * _flip(x, j & 2) + _flip(y, j & 1)) for j in range(1, N_CHIPS)]


def _plan_gather_ici(place, src_refs, land_refs):
    _, _, c, p = place
    return [(s.at[c], l.at[p, c], peer, l.at[q, c]) for s, l in zip(src_refs, land_refs) for peer, q in _other_chips(place)]


def _plan_gather_d2d(place, src_refs, land_refs):
    x, y, c, _ = place
    return [(l.at[q, c], l.at[q, c], (x, y, 1 - c), l.at[q, 1 - c]) for l in land_refs for _, q in _other_chips(place)]


def _plan_swap(place, src_refs, land_refs):
    x, y, c, _ = place
    return [(s.at[k, 1 - c], l.at[k], (x, y, 1 - c), l.at[k]) for s, l in zip(src_refs, land_refs) for k in range(N_CHIPS)]


def _plan_scatter(place, src_refs, land_refs):
    _, _, _, p = place
    return [(s.at[q], l.at[p], peer, l.at[q]) for s, l in zip(src_refs, land_refs) for peer, q in _other_chips(place)]


def _plan_share(place, src_refs, land_refs):
    x, y, c, _ = place
    return [(s, l, (x, y, 1 - c), l) for s, l in zip(src_refs, land_refs)]


def _pack_rows(parts, n_rows, name, deps=()):
    def body(*refs):
        refs = refs[len(deps):]
        out_ref = refs[-1]
        out_ref[...] = jnp.zeros((n_rows, D), F32)
        at = 0
        for ref in refs[:-1]:
            k = ref.shape[0]
            out_ref[at:at + k, :] = ref[...]
            at += k

    return pl.pallas_call(
        body, out_shape=jax.ShapeDtypeStruct((n_rows, D), F32), in_specs=[_ANY] * len(deps) + [_VMEM] * len(parts),
        out_specs=_VMEM, name=name)(*deps, *parts)


TS_MM = 512
TS_MIX = 256


def _halved(a):
    n, r, cols = a.shape
    return a.reshape(n, 2, r // 2, cols)


def _gather_begin(name, shards, chip, after=()):
    lands = [lax.dynamic_update_index_in_dim(lax.empty((N_CHIPS,) + s.shape, s.dtype), s, chip, 0) for s in shards]
    sems, srcs, lands, token = _xchg_start(name + "_ici", _plan_gather_ici, 3 * len(shards), shards, lands, after)
    return name, sems, srcs, lands, token


def _gather_relay(handle, after):
    name, sems, srcs, lands, _ = handle
    _, lands = _xchg_wait(name + "_ici_wait", _plan_gather_ici, sems, srcs, lands, after)
    sems, _, lands, token = _xchg_start(name + "_d2d", _plan_gather_d2d, 3 * len(lands), [], lands)
    return name, sems, lands, token


def _gather_end(handle, after):
    name, sems, lands, _ = handle
    return _xchg_wait(name + "_d2d_wait", _plan_gather_d2d, sems, [], lands, after)[1]


def _rs_swap(name, grads):
    lands = [lax.empty((N_CHIPS,) + g.shape[2:], g.dtype) for g in grads]
    sems, grads, lands, token = _xchg_start(name + "_swap", _plan_swap, N_CHIPS * len(grads), grads, lands)
    return name, sems, grads, lands, token


def _rs_scatter(handle, after, chip, ci):
    name, sems, grads, lands, _ = handle
    grads, from_sibling = _xchg_wait(name + "_swap_wait", _plan_swap, sems, grads, lands, after)
    c_arr = jnp.reshape(ci, (1,)).astype(jnp.int32)
    pair_sums = [_add_halves(g, r, c_arr, "%s_add_halves_%d" % (name, k)) for k, (g, r) in enumerate(zip(grads, from_sibling))]
    lands = [lax.dynamic_update_index_in_dim(lax.empty(p.shape, p.dtype), lax.dynamic_index_in_dim(p, chip, 0, keepdims=False),
                                             chip, 0) for p in pair_sums]
    sems, pair_sums, lands, token = _xchg_start(name + "_scatter", _plan_scatter, 3 * len(pair_sums), pair_sums, lands)
    return name, sems, pair_sums, lands, token


def _rs_share(handle, after):
    name, sems, pair_sums, lands, _ = handle
    _, by_chip = _xchg_wait(name + "_scatter_wait", _plan_scatter, sems, pair_sums, lands, after)
    halves = [_sum_chips(b, "%s_sum_chips_%d" % (name, k)) for k, b in enumerate(by_chip)]
    lands = [lax.empty(h.shape, h.dtype) for h in halves]
    sems, halves, lands, token = _xchg_start(name + "_share", _plan_share, len(halves), halves, lands)
    return name, sems, halves, lands, token


def _rs_end(handle, after, ci):
    name, sems, halves, lands, _ = handle
    halves, others = _xchg_wait(name + "_share_wait", _plan_share, sems, halves, lands, after)
    return [jnp.where(ci == 0, jnp.concatenate([mine, other], axis=0), jnp.concatenate([other, mine], axis=0))
            for mine, other in zip(halves, others)]


def kernel(x, c, w_ada, b_ada, g_norm_mix, w_in, conv_a_w, conv_b_w, conv_b_bias, w_rg_a, b_rg_a, w_rg_x, b_rg_x, lru_lambda, w_out, g_norm_ffn, w_gate_up, w_down, g_norm_final, loss_target, m_w_ada, m_b_ada, m_g_norm_mix, m_w_in, m_conv_a_w, m_conv_b_w, m_conv_b_bias, m_w_rg_a, m_b_rg_a, m_w_rg_x, m_b_rg_x, m_lru_lambda, m_w_out, m_g_norm_ffn, m_w_gate_up, m_w_down, m_g_norm_final, v_w_ada, v_b_ada, v_g_norm_mix, v_w_in, v_conv_a_w, v_conv_b_w, v_conv_b_bias, v_w_rg_a, v_b_rg_a, v_w_rg_x, v_b_rg_x, v_lru_lambda, v_w_out, v_g_norm_ffn, v_w_gate_up, v_w_down, v_g_norm_final):
    xi, yi, ci = lax.axis_index("x"), lax.axis_index("y"), lax.axis_index("c")
    chip = 2 * xi + yi
    me = 2 * chip + ci
    n_ada = w_ada.shape[2]

    def rg_shard(w):
        return w[0].astype(BF16).reshape(2, HEADS * HB // N_CHIPS // 2, HB)

    gather_a = _gather_begin("gather_a", [w_in[0].astype(BF16).reshape(2, D // 2, C_IN), rg_shard(w_rg_a), rg_shard(w_rg_x),
                                          w_out[0].astype(BF16).reshape(2, D // N_CHIPS // 2, D)], chip)

    def widen(w):
        return jnp.pad(w, ((0, 0), (0, D - w.shape[1])))

    got = _allgather8(_pack_rows([c, widen(conv_a_w[0]), widen(conv_b_w[0])], 8, "pack_c_conv", deps=[gather_a[-1]]),
                      "gather_c_conv")
    c_all = got[:, 0, :]
    conv_full = got[::2, 1:8, :D // N_CHIPS].transpose(1, 0, 2).reshape(7, D)

    mod_part, c_act = _ada_fwd(c_all, w_ada[0], lax.dynamic_slice_in_dim(b_ada, chip * n_ada, n_ada, axis=1))
    mod_all = _allgather8(mod_part, "gather_mod")
    mod_mine = lax.dynamic_index_in_dim(mod_all, me, axis=1, keepdims=False)[::2].reshape(6, D)
    vecs = _pack_rows([mod_mine, g_norm_mix, g_norm_ffn, g_norm_final.reshape(1, D), conv_b_bias, b_rg_a, b_rg_x, lru_lambda,
                       conv_full], N_VEC, "pack_vecs")

    gather_b = _gather_begin("gather_b", [w_gate_up[0].astype(BF16).reshape(2, D // 2, C_GU),
                                          w_down[0].astype(BF16).reshape(2, D_FF // N_CHIPS // 2, D)], chip, after=[vecs])
    gather_a = _gather_relay(gather_a, [gather_b[-1]])
    wg_in, wg_rga, wg_rgx, wg_out = _gather_end(gather_a, [])
    wg_in = wg_in.reshape(N_CHIPS, D, C_IN)
    wg_out = wg_out.reshape(D, D)

    def rg_full(wg):
        return wg.reshape(N_CHIPS, HEADS, HB // N_CHIPS, HB).transpose(1, 0, 2, 3).reshape(HEADS, HB, HB)

    wg_rga, wg_rgx = rg_full(wg_rga), rg_full(wg_rgx)

    xs, target = x[0], loss_target[0]
    h1, proj = _fwd_in(xs, vecs, wg_in, TS_MM)
    gather_b = _gather_relay(gather_b, [proj])
    x1, merged, z1, u, h = _fwd_mix(proj, xs, vecs, wg_rga, wg_rgx, wg_out, TS_MIX, deps=[gather_b[-1]])
    wg_gu, wg_dn = _gather_end(gather_b, [x1])
    wg_gu, wg_dn = wg_gu.reshape(N_CHIPS, D, C_GU), wg_dn.reshape(D_FF, D)
    dx1, h2, act, dz2, dgu, sm_ffn = _ffn_loss(x1, target, vecs, wg_gu, wg_dn, TS_MIX)

    def rg_chunks(dw):
        return _halved(dw.reshape(HEADS, N_CHIPS, HB // N_CHIPS, HB).transpose(1, 0, 2, 3).reshape(N_CHIPS, HB, HB).astype(BF16))

    g_dn = _grad_w(act, dz2, 1, TS_MM, "grad_w_down")
    g_gu = _grad_w(h2, dgu, N_CHIPS, TS_MM, "grad_w_gate_up")
    rs_b = _rs_swap("rs_b", [_halved(g_gu), _halved(g_dn.reshape(N_CHIPS, D_FF // N_CHIPS, D))])
    dproj, dz1, sm_mix, dw_rga, dw_rgx = _bwd_mix(dx1, z1, proj, u, h, vecs, wg_rga, wg_rgx, wg_out, TS_MIX, deps=[rs_b[-1]])
    rs_b = _rs_scatter(rs_b, [dproj], chip, ci)
    g_in = _grad_w(h1, dproj, N_CHIPS, TS_MM, "grad_w_in", deps=[rs_b[-1]])
    rs_b = _rs_share(rs_b, [g_in])
    g_out = _grad_w(merged, dz1, 1, TS_MM, "grad_w_out", deps=[rs_b[-1]])
    rs_a = _rs_swap("rs_a", [_halved(g_in), rg_chunks(dw_rga), rg_chunks(dw_rgx), _halved(g_out.reshape(N_CHIPS, D // N_CHIPS, D))])
    grad_x, sm_in = _bwd_in(dproj, xs, dx1, vecs, wg_in, TS_MM, deps=[rs_a[-1]])
    gw_gu, gw_dn = _rs_end(rs_b, [grad_x], ci)

    sm3 = jnp.stack([sm_ffn, sm_mix, sm_in])
    small, per_dev = _sum_small(_allgather8(sm3.reshape(3 * N_SMALL, D), "gather_small").reshape(N_DEV, 3, N_SMALL, D))
    dmod_all = per_dev[:, 0:6, :].reshape(N_DEV, 6 * D)
    grad_w_ada = _ada_bwd(c_act, lax.dynamic_slice_in_dim(dmod_all, chip * n_ada, n_ada, axis=1))
    rs_a = _rs_scatter(rs_a, [grad_w_ada], chip, ci)

    def step(name, w, g, m, v, deps=()):
        shape = w.shape
        two_d = (-1, shape[-1])
        d, nm, nv = _adamw(w.reshape(two_d), g.reshape(two_d), m.reshape(two_d), v.reshape(two_d), "adamw_" + name, deps)
        return g.reshape(shape), d.reshape(shape), nm.reshape(shape), nv.reshape(shape)

    def shard_cols(row_block):
        return lax.dynamic_slice_in_dim(row_block, chip * (D // N_CHIPS), D // N_CHIPS, axis=1)

    behind = [rs_a[-1]]
    grad_b_ada = small[0:6].reshape(1, 6 * D)
    res = {
        "w_gate_up": step("w_gate_up", w_gate_up, gw_gu[None], m_w_gate_up, v_w_gate_up, behind),
        "w_down": step("w_down", w_down, gw_dn[None], m_w_down, v_w_down, behind),
        "w_ada": step("w_ada", w_ada, grad_w_ada[None], m_w_ada, v_w_ada, behind),
        "b_ada": step("b_ada", b_ada.reshape(6, D), grad_b_ada.reshape(6, D), m_b_ada.reshape(6, D), v_b_ada.reshape(6, D), behind),
        "g_norm_mix": step("g_norm_mix", g_norm_mix, small[G_GMIX:G_GMIX + 1], m_g_norm_mix, v_g_norm_mix, behind),
        "conv_a_w": step("conv_a_w", conv_a_w, shard_cols(small[G_WA0:G_WA0 + 3])[None], m_conv_a_w, v_conv_a_w, behind),
        "conv_b_w": step("conv_b_w", conv_b_w, shard_cols(small[G_WB0:G_WB0 + 4])[None], m_conv_b_w, v_conv_b_w, behind),
        "conv_b_bias": step("conv_b_bias", conv_b_bias, small[G_CBB:G_CBB + 1], m_conv_b_bias, v_conv_b_bias, behind),
        "b_rg_a": step("b_rg_a", b_rg_a, small[G_BA:G_BA + 1], m_b_rg_a, v_b_rg_a, behind),
        "b_rg_x": step("b_rg_x", b_rg_x, small[G_BX:G_BX + 1], m_b_rg_x, v_b_rg_x, behind),
        "lru_lambda": step("lru_lambda", lru_lambda, small[G_LAM:G_LAM + 1], m_lru_lambda, v_lru_lambda, behind),
        "g_norm_ffn": step("g_norm_ffn", g_norm_ffn, small[G_GFFN:G_GFFN + 1], m_g_norm_ffn, v_g_norm_ffn, behind),
        "g_norm_final": step("g_norm_final", g_norm_final.reshape(1, D), small[G_GFIN:G_GFIN + 1], m_g_norm_final.reshape(1, D),
                             v_g_norm_final.reshape(1, D), behind),
    }
    rs_a = _rs_share(rs_a, [res[n][1] for n in res])
    gw_in, gw_rga, gw_rgx, gw_out = _rs_end(rs_a, [], ci)
    res["w_in"] = step("w_in", w_in, gw_in[None], m_w_in, v_w_in)
    res["w_rg_a"] = step("w_rg_a", w_rg_a, gw_rga.reshape(w_rg_a.shape), m_w_rg_a, v_w_rg_a)
    res["w_rg_x"] = step("w_rg_x", w_rg_x, gw_rgx.reshape(w_rg_x.shape), m_w_rg_x, v_w_rg_x)
    res["w_out"] = step("w_out", w_out, gw_out[None], m_w_out, v_w_out)
    res["b_ada"] = tuple(a.reshape(1, 6 * D) for a in res["b_ada"])
    res["g_norm_final"] = tuple(a.reshape(D) for a in res["g_norm_final"])
    names = ["w_ada", "b_ada", "g_norm_mix", "w_in", "conv_a_w", "conv_b_w", "conv_b_bias", "w_rg_a", "b_rg_a", "w_rg_x",
             "b_rg_x", "lru_lambda", "w_out", "g_norm_ffn", "w_gate_up", "w_down", "g_norm_final"]
    loss = jnp.sum(small[G_LOSS])
    return (loss, grad_x[None], *[res[n][0] for n in names], *[res[n][1] for n in names],
            *[res[n][2] for n in names], *[res[n][3] for n in names])
```

```python
import functools

import jax
import jax.numpy as jnp
from jax import lax
from jax.experimental import pallas as pl
from jax.experimental.pallas import tpu as pltpu

F32 = jnp.float32
BF16 = jnp.bfloat16
MESH = pl.DeviceIdType.MESH

D = 1024
N_CHIPS = 4
N_DEV = 8
D_IN = 7 * D
C_IN = D_IN // N_CHIPS
D_FF = 2816
C_GU = 2 * D_FF // N_CHIPS
HEADS = 4
HB = D // HEADS
EPS = 1e-6
LRU_C = 8.0
ADAM_LR, ADAM_B1, ADAM_B2, ADAM_EPS, ADAM_WD, ADAM_STEP = 0.001, 0.9, 0.999, 1e-08, 0.01, 10
VMEM_LIMIT = 56 << 20

(V_SH1, V_SC1, V_GT1, V_SH2, V_SC2, V_GT2, V_GMIX, V_GFFN, V_GFIN, V_CBB, V_BA, V_BX, V_LAM,
 V_WA0, V_WA1, V_WA2, V_WB0, V_WB1, V_WB2, V_WB3) = range(20)
N_VEC = 24
(G_SH1, G_SC1, G_GT1, G_SH2, G_SC2, G_GT2, G_GMIX, G_CBB, G_BA, G_BX, G_LAM, G_GFFN, G_GFIN,
 G_WA0, G_WA1, G_WA2, G_WB0, G_WB1, G_WB2, G_WB3, G_LOSS) = range(21)
N_SMALL = 24

_VMEM = pl.BlockSpec(memory_space=pltpu.VMEM)
_ANY = pl.BlockSpec(memory_space=pl.ANY)


def _cparams(n_grid=1):
    return pltpu.CompilerParams(dimension_semantics=("arbitrary",) * n_grid, vmem_limit_bytes=VMEM_LIMIT)


def _after(deps, body):
    n = len(deps)
    return lambda *refs: body(*refs[n:])


def _rms(x):
    rstd = lax.rsqrt(jnp.mean(x * x, axis=-1, keepdims=True) + EPS)
    return x * rstd, rstd


def _rms_bwd(dxhat, xhat, rstd):
    return rstd * (dxhat - xhat * jnp.mean(dxhat * xhat, axis=-1, keepdims=True))


def _rowsum(v):
    return jnp.sum(v, axis=0, keepdims=True)


def _dot(a, b):
    return jnp.dot(a, b, preferred_element_type=F32)


def _dot_nt(a, b):
    return lax.dot_general(a, b, (((1,), (1,)), ((), ())), preferred_element_type=F32)


def _dot_tn(a, b):
    return lax.dot_general(a, b, (((0,), (0,)), ((), ())), preferred_element_type=F32)


def _gelu(x):
    k, c = 0.7978845608028654, 0.044715
    t = jnp.tanh(k * (x + c * x * x * x))
    return 0.5 * x * (1.0 + t), 0.5 * (1.0 + t) + 0.5 * x * (1.0 - t * t) * k * (1.0 + 3.0 * c * x * x)


def _log_sigmoid(lam):
    return jnp.minimum(lam, 0.0) - jnp.log1p(jnp.exp(-jnp.abs(lam)))


def _lru_gates(u, wa_ref, wx_ref, v_ref, row0):
    ub = u.astype(BF16)
    pre_a = jnp.concatenate([_dot(ub[:, h * HB:(h + 1) * HB], wa_ref[h]) for h in range(HEADS)], axis=1)
    pre_x = jnp.concatenate([_dot(ub[:, h * HB:(h + 1) * HB], wx_ref[h]) for h in range(HEADS)], axis=1)
    r = jax.nn.sigmoid(pre_a + v_ref[V_BA:V_BA + 1, :])
    ig = jax.nn.sigmoid(pre_x + v_ref[V_BX:V_BX + 1, :])
    log_a = LRU_C * r * _log_sigmoid(v_ref[V_LAM:V_LAM + 1, :])
    a = jnp.exp(log_a)
    x2 = 2.0 * log_a
    m2 = jnp.where(x2 > -0.03, -x2 * (1.0 + x2 * (0.5 + x2 * (1.0 / 6.0 + x2 * (1.0 / 24.0)))), 1.0 - a * a)
    mult = jnp.where(row0, 1.0, jnp.sqrt(jnp.maximum(m2, 0.0)))
    return r, ig, a, mult


def _fwd_in(x, vecs, w_in_g, ts, deps=()):
    s = x.shape[0]

    def body(x_ref, v_ref, w_ref, h1_ref, proj_ref):
        xhat, _ = _rms(x_ref[...])
        h = xhat * v_ref[V_GMIX:V_GMIX + 1, :] * (1.0 + v_ref[V_SC1:V_SC1 + 1, :]) + v_ref[V_SH1:V_SH1 + 1, :]
        hb = h.astype(BF16)
        h1_ref[...] = hb
        for k in range(N_CHIPS):
            proj_ref[:, k * C_IN:(k + 1) * C_IN] = _dot(hb, w_ref[k]).astype(BF16)

    return pl.pallas_call(
        _after(deps, body), grid=(s // ts,),
        out_shape=(jax.ShapeDtypeStruct((s, D), BF16), jax.ShapeDtypeStruct((s, D_IN), BF16)),
        in_specs=[_ANY] * len(deps) + [pl.BlockSpec((ts, D), lambda i: (i, 0)), _VMEM, _VMEM],
        out_specs=[pl.BlockSpec((ts, D), lambda i: (i, 0)), pl.BlockSpec((ts, D_IN), lambda i: (i, 0))],
        compiler_params=_cparams(), name="fwd_in")(*deps, x, vecs, w_in_g)


def _fwd_mix(proj, x, vecs, w_rga, w_rgx, w_out, ts, deps=()):
    s = x.shape[0]

    def body(proj_ref, x_ref, v_ref, wa_ref, wx_ref, wo_ref, x1_ref, mg_ref, z1_ref, u_ref, h_ref,
             ua_buf, rx_buf, a_buf, b_buf, hcarry):
        i = pl.program_id(0)

        @pl.when(i == 0)
        def _():
            ua_buf[0:8, :] = jnp.zeros((8, D), F32)
            rx_buf[0:8, :] = jnp.zeros((8, D), F32)
            hcarry[...] = jnp.zeros((8, D), F32)

        def seg(j):
            return proj_ref[:, j * D:(j + 1) * D].astype(F32)

        def vrow(j):
            return v_ref[j:j + 1, :]

        cb, cc, cx, rx, rg, ga, gb = (seg(j) for j in range(7))
        ua = cc * cx
        ua_buf[8:ts + 8, :] = ua
        rx_buf[8:ts + 8, :] = rx
        va = vrow(V_WA2) * ua + vrow(V_WA1) * ua_buf[pl.ds(7, ts), :] + vrow(V_WA0) * ua_buf[pl.ds(6, ts), :]
        u = (vrow(V_WB3) * rx + vrow(V_WB2) * rx_buf[pl.ds(7, ts), :] + vrow(V_WB1) * rx_buf[pl.ds(6, ts), :]
             + vrow(V_WB0) * rx_buf[pl.ds(5, ts), :] + vrow(V_CBB))
        ua_buf[0:8, :] = ua_buf[ts:ts + 8, :]
        rx_buf[0:8, :] = rx_buf[ts:ts + 8, :]
        u_ref[...] = u

        rows = lax.broadcasted_iota(jnp.int32, (ts, D), 0)
        row0 = jnp.logical_and(rows == 0, i == 0)
        _, ig, a, mult = _lru_gates(u, wa_ref, wx_ref, v_ref, row0)
        bx = mult * (ig * u)

        sub = rows % 8
        for d in (1, 2, 4):
            m = sub >= d
            bx = jnp.where(m, a * pltpu.roll(bx, d, axis=0) + bx, bx)
            a = jnp.where(m, a * pltpu.roll(a, d, axis=0), a)
        a_buf[...] = a
        b_buf[...] = bx

        def grp(g, carry):
            off = pl.multiple_of(g * 8, 8)
            h_ref[pl.ds(off, 8), :] = a_buf[pl.ds(off, 8), :] * carry + b_buf[pl.ds(off, 8), :]
            return jnp.broadcast_to(h_ref[pl.ds(off + 7, 1), :], (8, D))

        hcarry[...] = lax.fori_loop(0, ts // 8, grp, hcarry[...])
        h = h_ref[...]

        gel, _ = _gelu(rg)
        merged = (jax.nn.sigmoid(ga) * (cb * va) + jax.nn.sigmoid(gb) * (h * gel)).astype(BF16)
        mg_ref[...] = merged
        z1 = _dot(merged, wo_ref[...])
        z1_ref[...] = z1.astype(BF16)
        x1_ref[...] = x_ref[...] + vrow(V_GT1) * z1

    row = lambda i: (i, 0)
    return pl.pallas_call(
        _after(deps, body), grid=(s // ts,),
        out_shape=(jax.ShapeDtypeStruct((s, D), F32), jax.ShapeDtypeStruct((s, D), BF16), jax.ShapeDtypeStruct((s, D), BF16),
                   jax.ShapeDtypeStruct((s, D), F32), jax.ShapeDtypeStruct((s, D), F32)),
        in_specs=[_ANY] * len(deps) + [pl.BlockSpec((ts, D_IN), row), pl.BlockSpec((ts, D), row), _VMEM, _VMEM, _VMEM, _VMEM],
        out_specs=[pl.BlockSpec((ts, D), row)] * 5,
        scratch_shapes=[pltpu.VMEM((ts + 8, D), F32), pltpu.VMEM((ts + 8, D), F32), pltpu.VMEM((ts, D), F32),
                        pltpu.VMEM((ts, D), F32), pltpu.VMEM((8, D), F32)],
        compiler_params=_cparams(), name="fwd_mix")(*deps, proj, x, vecs, w_rga, w_rgx, w_out)


def _ffn_loss(x1, target, vecs, w_gu_g, w_dn, ts):
    s = x1.shape[0]

    def body(x1_ref, t_ref, v_ref, wgu_ref, wdn_ref, dx1_ref, h2_ref, act_ref, dz2_ref, dgu_ref, sm_ref):
        @pl.when(pl.program_id(0) == 0)
        def _():
            sm_ref[...] = jnp.zeros((N_SMALL, D), F32)

        def vrow(j):
            return v_ref[j:j + 1, :]

        def acc(j, val):
            sm_ref[j:j + 1, :] += _rowsum(val)

        x1 = x1_ref[...]
        xh1, rstd1 = _rms(x1)
        n2 = xh1 * vrow(V_GFFN)
        h2 = (n2 * (1.0 + vrow(V_SC2)) + vrow(V_SH2)).astype(BF16)
        h2_ref[...] = h2
        g = jnp.concatenate([_dot(h2, wgu_ref[0]), _dot(h2, wgu_ref[1])], axis=1)
        up = jnp.concatenate([_dot(h2, wgu_ref[2]), _dot(h2, wgu_ref[3])], axis=1)
        sg = jax.nn.sigmoid(g)
        silu = g * sg
        act = (silu * up).astype(BF16)
        act_ref[...] = act
        z2 = _dot(act, wdn_ref[...])
        x2 = x1 + vrow(V_GT2) * z2
        xh2, rstd2 = _rms(x2)
        err = xh2 * vrow(V_GFIN) - t_ref[...]
        acc(G_LOSS, (0.5 / D) * err * err)
        dy = err * (1.0 / D)
        acc(G_GFIN, dy * xh2)
        dx2 = _rms_bwd(dy * vrow(V_GFIN), xh2, rstd2)
        acc(G_GT2, dx2 * z2)
        dz2 = (vrow(V_GT2) * dx2).astype(BF16)
        dz2_ref[...] = dz2
        dact = _dot_nt(dz2, wdn_ref[...])
        dgate = (dact * up * (sg * (1.0 + g * (1.0 - sg)))).astype(BF16)
        dup = (dact * silu).astype(BF16)
        dgu_ref[:, 0:D_FF] = dgate
        dgu_ref[:, D_FF:2 * D_FF] = dup
        dh2 = (_dot_nt(dgate[:, 0:C_GU], wgu_ref[0]) + _dot_nt(dgate[:, C_GU:2 * C_GU], wgu_ref[1])
               + _dot_nt(dup[:, 0:C_GU], wgu_ref[2]) + _dot_nt(dup[:, C_GU:2 * C_GU], wgu_ref[3]))
        acc(G_SH2, dh2)
        acc(G_SC2, dh2 * n2)
        dn2 = dh2 * (1.0 + vrow(V_SC2))
        acc(G_GFFN, dn2 * xh1)
        dx1_ref[...] = dx2 + _rms_bwd(dn2 * vrow(V_GFFN), xh1, rstd1)

    row = lambda i: (i, 0)
    return pl.pallas_call(
        body, grid=(s // ts,),
        out_shape=(jax.ShapeDtypeStruct((s, D), F32), jax.ShapeDtypeStruct((s, D), BF16), jax.ShapeDtypeStruct((s, D_FF), BF16),
                   jax.ShapeDtypeStruct((s, D), BF16), jax.ShapeDtypeStruct((s, 2 * D_FF), BF16),
                   jax.ShapeDtypeStruct((N_SMALL, D), F32)),
        in_specs=[pl.BlockSpec((ts, D), row), pl.BlockSpec((ts, D), row), _VMEM, _VMEM, _VMEM],
        out_specs=[pl.BlockSpec((ts, D), row), pl.BlockSpec((ts, D), row), pl.BlockSpec((ts, D_FF), row),
                   pl.BlockSpec((ts, D), row), pl.BlockSpec((ts, 2 * D_FF), row), pl.BlockSpec((N_SMALL, D), lambda i: (0, 0))],
        compiler_params=_cparams(), name="ffn_loss")(x1, target, vecs, w_gu_g, w_dn)


def _bwd_mix(dx1, z1, proj, u, h, vecs, w_rga, w_rgx, w_out, small, ts, deps=()):
    s = dx1.shape[0]
    nt = s // ts

    def body(dx1_ref, z1_ref, proj_ref, u_ref, h_ref, hh_ref, cch_ref, cxh_ref, v_ref, wa_ref, wx_ref, wo_ref, sm0_ref,
             dproj_ref, dz1_ref, sm_ref, dwa_ref, dwx_ref,
             ua_buf, h_buf, a_buf, dva_buf, du_buf, ca_buf, cb_buf, l_buf, lcarry):
        i = pl.program_id(0)
        first_tile = i == nt - 1

        @pl.when(i == 0)
        def _():
            a_buf[ts:ts + 8, :] = jnp.zeros((8, D), F32)
            dva_buf[ts:ts + 8, :] = jnp.zeros((8, D), F32)
            du_buf[ts:ts + 8, :] = jnp.zeros((8, D), F32)
            lcarry[...] = jnp.zeros((8, D), F32)
            sm_ref[...] = sm0_ref[...]
            dwa_ref[...] = jnp.zeros((HEADS, HB, HB), F32)
            dwx_ref[...] = jnp.zeros((HEADS, HB, HB), F32)

        def seg(j):
            return proj_ref[:, j * D:(j + 1) * D].astype(F32)

        def vrow(j):
            return v_ref[j:j + 1, :]

        def acc(j, val):
            sm_ref[j:j + 1, :] += _rowsum(val)

        cb, cc, cx, rx, rg, ga, gb = (seg(j) for j in range(7))
        ua = cc * cx
        ua_halo = cch_ref[8:16, :].astype(F32) * cxh_ref[8:16, :].astype(F32)
        ua_buf[0:8, :] = jnp.where(first_tile, 0.0, ua_halo)
        ua_buf[8:ts + 8, :] = ua
        va = vrow(V_WA2) * ua + vrow(V_WA1) * ua_buf[pl.ds(7, ts), :] + vrow(V_WA0) * ua_buf[pl.ds(6, ts), :]
        u = u_ref[...]
        h = h_ref[...]
        rows = lax.broadcasted_iota(jnp.int32, (ts, D), 0)
        row0 = jnp.logical_and(rows == 0, first_tile)
        r, ig, a, mult = _lru_gates(u, wa_ref, wx_ref, v_ref, row0)
        sga = jax.nn.sigmoid(ga)
        sgb = jax.nn.sigmoid(gb)
        gel, dgel = _gelu(rg)

        dx1 = dx1_ref[...]
        acc(G_GT1, dx1 * z1_ref[...].astype(F32))
        dz1 = (vrow(V_GT1) * dx1).astype(BF16)
        dz1_ref[...] = dz1
        dmg = _dot_nt(dz1, wo_ref[...])
        dproj_ref[:, 5 * D:6 * D] = (dmg * (cb * va) * sga * (1.0 - sga)).astype(BF16)
        dproj_ref[:, 6 * D:7 * D] = (dmg * (h * gel) * sgb * (1.0 - sgb)).astype(BF16)
        dya = dmg * sga
        dyb = dmg * sgb

        dproj_ref[:, 0:D] = (dya * va).astype(BF16)
        dva = dya * cb
        dva_buf[0:ts, :] = dva
        dva1 = dva_buf[pl.ds(1, ts), :]
        dva2 = dva_buf[pl.ds(2, ts), :]
        dua = vrow(V_WA2) * dva + vrow(V_WA1) * dva1 + vrow(V_WA0) * dva2
        acc(G_WA2, ua * dva)
        acc(G_WA1, ua * dva1)
        acc(G_WA0, ua * dva2)
        dva_buf[ts:ts + 8, :] = dva_buf[0:8, :]
        dproj_ref[:, D:2 * D] = (dua * cx).astype(BF16)
        dproj_ref[:, 2 * D:3 * D] = (dua * cc).astype(BF16)

        dproj_ref[:, 4 * D:5 * D] = (dyb * h * dgel).astype(BF16)
        a_buf[0:ts, :] = a
        sa = a_buf[pl.ds(1, ts), :]
        sb = dyb * gel
        sub = rows % 8
        for d in (1, 2, 4):
            m = sub < 8 - d
            sb = jnp.where(m, sb + sa * pltpu.roll(sb, ts - d, axis=0), sb)
            sa = jnp.where(m, sa * pltpu.roll(sa, ts - d, axis=0), sa)
        ca_buf[...] = sa
        cb_buf[...] = sb

        def grp(k, carry):
            off = pl.multiple_of((ts // 8 - 1 - k) * 8, 8)
            l_buf[pl.ds(off, 8), :] = ca_buf[pl.ds(off, 8), :] * carry + cb_buf[pl.ds(off, 8), :]
            return jnp.broadcast_to(l_buf[pl.ds(off, 1), :], (8, D))

        lcarry[...] = lax.fori_loop(0, ts // 8, grp, lcarry[...])
        a_buf[ts:ts + 8, :] = a_buf[0:8, :]
        lam = l_buf[...]

        h_buf[0:8, :] = jnp.where(first_tile, 0.0, hh_ref[...])
        h_buf[8:ts + 8, :] = h
        da = lam * h_buf[pl.ds(7, ts), :]
        dmult = jnp.where(row0, 0.0, lam * (ig * u))
        di = lam * mult * u
        du = lam * mult * ig
        dlog_a = da * a - dmult * (a * a) / mult
        lam_p = vrow(V_LAM)
        dr = dlog_a * (LRU_C * _log_sigmoid(lam_p))
        sm_ref[G_LAM:G_LAM + 1, :] += _rowsum(dlog_a * r) * (LRU_C * jax.nn.sigmoid(-lam_p))
        dpa = dr * r * (1.0 - r)
        dpx = di * ig * (1.0 - ig)
        acc(G_BA, dpa)
        acc(G_BX, dpx)
        dpab = dpa.astype(BF16)
        dpxb = dpx.astype(BF16)
        ub = u.astype(BF16)
        back = []
        for hd in range(HEADS):
            cols = slice(hd * HB, (hd + 1) * HB)
            back.append(_dot_nt(dpab[:, cols], wa_ref[hd]) + _dot_nt(dpxb[:, cols], wx_ref[hd]))
            dwa_ref[hd] += _dot_tn(ub[:, cols], dpab[:, cols])
            dwx_ref[hd] += _dot_tn(ub[:, cols], dpxb[:, cols])
        du = du + jnp.concatenate(back, axis=1)

        acc(G_CBB, du)
        du_buf[0:ts, :] = du
        du1 = du_buf[pl.ds(1, ts), :]
        du2 = du_buf[pl.ds(2, ts), :]
        du3 = du_buf[pl.ds(3, ts), :]
        dproj_ref[:, 3 * D:4 * D] = (vrow(V_WB3) * du + vrow(V_WB2) * du1 + vrow(V_WB1) * du2 + vrow(V_WB0) * du3).astype(BF16)
        acc(G_WB3, rx * du)
        acc(G_WB2, rx * du1)
        acc(G_WB1, rx * du2)
        acc(G_WB0, rx * du3)
        du_buf[ts:ts + 8, :] = du_buf[0:8, :]

    rev = lambda i: (nt - 1 - i, 0)
    halo8 = lambda i: (jnp.maximum((nt - 1 - i) * (ts // 8) - 1, 0), 0)
    const2 = lambda i: (0, 0)
    const3 = lambda i: (0, 0, 0)
    return pl.pallas_call(
        _after(deps, body), grid=(nt,),
        out_shape=(jax.ShapeDtypeStruct((s, D_IN), BF16), jax.ShapeDtypeStruct((s, D), BF16),
                   jax.ShapeDtypeStruct((N_SMALL, D), F32), jax.ShapeDtypeStruct((HEADS, HB, HB), F32),
                   jax.ShapeDtypeStruct((HEADS, HB, HB), F32)),
        in_specs=[_ANY] * len(deps) + [pl.BlockSpec((ts, D), rev), pl.BlockSpec((ts, D), rev), pl.BlockSpec((ts, D_IN), rev),
                  pl.BlockSpec((ts, D), rev), pl.BlockSpec((ts, D), rev), pl.BlockSpec((8, D), halo8),
                  pl.BlockSpec((16, D), lambda i: (jnp.maximum((nt - 1 - i) * (ts // 16) - 1, 0), 1)),
                  pl.BlockSpec((16, D), lambda i: (jnp.maximum((nt - 1 - i) * (ts // 16) - 1, 0), 2)),
                  _VMEM, _VMEM, _VMEM, _VMEM, _VMEM],
        out_specs=[pl.BlockSpec((ts, D_IN), rev), pl.BlockSpec((ts, D), rev), pl.BlockSpec((N_SMALL, D), const2),
                   pl.BlockSpec((HEADS, HB, HB), const3), pl.BlockSpec((HEADS, HB, HB), const3)],
        scratch_shapes=[pltpu.VMEM((ts + 8, D), F32)] * 5 + [pltpu.VMEM((ts, D), F32)] * 3 + [pltpu.VMEM((8, D), F32)],
        compiler_params=_cparams(), name="bwd_mix")(*deps, dx1, z1, proj, u, h, h, proj, proj, vecs, w_rga, w_rgx, w_out, small)


def _bwd_in(dproj, x, dx1, vecs, w_in_g, small, ts, deps=()):
    s = x.shape[0]

    def body(dp_ref, x_ref, dx1_ref, v_ref, w_ref, sm0_ref, gx_ref, sm_ref):
        @pl.when(pl.program_id(0) == 0)
        def _():
            sm_ref[...] = sm0_ref[...]

        def vrow(j):
            return v_ref[j:j + 1, :]

        dh1 = _dot_nt(dp_ref[:, 0:C_IN], w_ref[0])
        for k in range(1, N_CHIPS):
            dh1 += _dot_nt(dp_ref[:, k * C_IN:(k + 1) * C_IN], w_ref[k])
        xh, rstd = _rms(x_ref[...])
        sm_ref[G_SH1:G_SH1 + 1, :] += _rowsum(dh1)
        sm_ref[G_SC1:G_SC1 + 1, :] += _rowsum(dh1 * (xh * vrow(V_GMIX)))
        dn1 = dh1 * (1.0 + vrow(V_SC1))
        sm_ref[G_GMIX:G_GMIX + 1, :] += _rowsum(dn1 * xh)
        gx_ref[...] = dx1_ref[...] + _rms_bwd(dn1 * vrow(V_GMIX), xh, rstd)

    row = lambda i: (i, 0)
    return pl.pallas_call(
        _after(deps, body), grid=(s // ts,),
        out_shape=(jax.ShapeDtypeStruct((s, D), F32), jax.ShapeDtypeStruct((N_SMALL, D), F32)),
        in_specs=[_ANY] * len(deps) + [pl.BlockSpec((ts, D_IN), row), pl.BlockSpec((ts, D), row), pl.BlockSpec((ts, D), row),
                                       _VMEM, _VMEM, _VMEM],
        out_specs=[pl.BlockSpec((ts, D), row), pl.BlockSpec((N_SMALL, D), lambda i: (0, 0))],
        compiler_params=_cparams(), name="bwd_in")(*deps, dproj, x, dx1, vecs, w_in_g, small)


def _grad_w(a, b, n_col_blocks, ts, name, deps=()):
    s, m = a.shape
    tn = b.shape[1] // n_col_blocks
    n_steps = s // ts

    def body(a_ref, b_ref, o_ref, acc_ref):
        k = pl.program_id(1)

        @pl.when(k == 0)
        def _():
            acc_ref[...] = jnp.zeros((m, tn), F32)

        acc_ref[...] += _dot_tn(a_ref[...], b_ref[...])

        @pl.when(k == n_steps - 1)
        def _():
            o_ref[...] = acc_ref[...].astype(BF16)

    return pl.pallas_call(
        _after(deps, body), grid=(n_col_blocks, n_steps),
        out_shape=jax.ShapeDtypeStruct((n_col_blocks, m, tn), BF16),
        in_specs=[_ANY] * len(deps) + [pl.BlockSpec((ts, m), lambda n, k: (k, 0)), pl.BlockSpec((ts, tn), lambda n, k: (k, n))],
        out_specs=pl.BlockSpec((None, m, tn), lambda n, k: (n, 0, 0)),
        scratch_shapes=[pltpu.VMEM((m, tn), F32)],
        compiler_params=_cparams(2), name=name)(*deps, a, b)


def _ada_fwd(c_all, w_ada, b_ada):
    n = w_ada.shape[1]

    def body(c_ref, w_ref, b_ref, o_ref, ca_ref):
        c = c_ref[...]
        ca = c * jax.nn.sigmoid(c)
        ca_ref[...] = ca
        o_ref[...] = jnp.dot(ca, w_ref[...], preferred_element_type=F32, precision=lax.Precision.HIGHEST) + b_ref[...]

    return pl.pallas_call(
        body, out_shape=(jax.ShapeDtypeStruct((N_DEV, n), F32), jax.ShapeDtypeStruct((N_DEV, D), F32)),
        in_specs=[_VMEM] * 3, out_specs=[_VMEM] * 2, compiler_params=_cparams(0), name="ada_fwd")(c_all, w_ada, b_ada)


def _ada_bwd(c_act, dmod):
    n = dmod.shape[1]

    def body(c_ref, d_ref, o_ref):
        o_ref[...] = lax.dot_general(c_ref[...], d_ref[...], (((0,), (0,)), ((), ())), preferred_element_type=F32,
                                     precision=lax.Precision.HIGHEST)

    return pl.pallas_call(
        body, out_shape=jax.ShapeDtypeStruct((D, n), F32), in_specs=[_VMEM] * 2, out_specs=_VMEM,
        compiler_params=_cparams(0), name="ada_bwd")(c_act, dmod)


def _sum_small(parts):
    def body(p_ref, o_ref, d_ref):
        tot = p_ref[0]
        for dev in range(1, N_DEV):
            tot = tot + p_ref[dev]
        o_ref[...] = tot
        d_ref[...] = p_ref[:, 0:8, :]

    return pl.pallas_call(
        body, out_shape=(jax.ShapeDtypeStruct((N_SMALL, D), F32), jax.ShapeDtypeStruct((N_DEV, 8, D), F32)),
        in_specs=[_VMEM], out_specs=[_VMEM] * 2, compiler_params=_cparams(0), name="sum_small")(parts)


def _adamw(w, g, m, v, name, deps=()):
    rows, cols = w.shape
    tr = 128 if rows % 128 == 0 else (64 if rows % 64 == 0 else rows)

    def body(w_ref, g_ref, m_ref, v_ref, d_ref, nm_ref, nv_ref):
        g_ = g_ref[...]
        m_ = ADAM_B1 * m_ref[...] + (1.0 - ADAM_B1) * g_
        v_ = ADAM_B2 * v_ref[...] + (1.0 - ADAM_B2) * (g_ * g_)
        nm_ref[...] = m_
        nv_ref[...] = v_
        m_hat = m_ / (1.0 - ADAM_B1 ** ADAM_STEP)
        v_hat = v_ / (1.0 - ADAM_B2 ** ADAM_STEP)
        d_ref[...] = -ADAM_LR * (m_hat / (jnp.sqrt(v_hat) + ADAM_EPS) + ADAM_WD * w_ref[...])

    spec = pl.BlockSpec((tr, cols), lambda i: (i, 0))
    return pl.pallas_call(
        _after(deps, body), grid=(rows // tr,), out_shape=(jax.ShapeDtypeStruct((rows, cols), F32),) * 3,
        in_specs=[_ANY] * len(deps) + [spec] * 4, out_specs=[spec] * 3, compiler_params=_cparams(), name=name)(*deps, w, g, m, v)


def _adamw_halves(w, mine, other, m, v, c_idx, name, deps=()):
    r2, cols = mine.shape
    tr = next(t for t in (128, 64, 32, 16, 8) if r2 % t == 0)
    nh = r2 // tr

    def body(c_ref, w_ref, mine_ref, other_ref, m_ref, v_ref, g_ref, d_ref, nm_ref, nv_ref):
        g_ = jnp.where(pl.program_id(0) // nh == c_ref[0], mine_ref[...], other_ref[...])
        g_ref[...] = g_
        m_ = ADAM_B1 * m_ref[...] + (1.0 - ADAM_B1) * g_
        v_ = ADAM_B2 * v_ref[...] + (1.0 - ADAM_B2) * (g_ * g_)
        nm_ref[...] = m_
        nv_ref[...] = v_
        m_hat = m_ / (1.0 - ADAM_B1 ** ADAM_STEP)
        v_hat = v_ / (1.0 - ADAM_B2 ** ADAM_STEP)
        d_ref[...] = -ADAM_LR * (m_hat / (jnp.sqrt(v_hat) + ADAM_EPS) + ADAM_WD * w_ref[...])

    full = pl.BlockSpec((tr, cols), lambda i, c: (i, 0))
    mine_spec = pl.BlockSpec((tr, cols), lambda i, c: (jnp.clip(i - c[0] * nh, 0, nh - 1), 0))
    other_spec = pl.BlockSpec((tr, cols), lambda i, c: (jnp.clip(i - (1 - c[0]) * nh, 0, nh - 1), 0))
    return pl.pallas_call(
        lambda c_ref, *refs: body(c_ref, *refs[len(deps):]),
        grid_spec=pltpu.PrefetchScalarGridSpec(
            num_scalar_prefetch=1, grid=(2 * nh,),
            in_specs=[_ANY] * len(deps) + [full, mine_spec, other_spec, full, full], out_specs=[full] * 4),
        out_shape=(jax.ShapeDtypeStruct((2 * r2, cols), F32),) * 4, compiler_params=_cparams(), name=name,
    )(c_idx, *deps, w, mine, other, m, v)


def _add_halves(g, recv, c_idx, name):
    n, _, r2, cols = g.shape

    def body(c_ref, g_ref, r_ref, o_ref):
        o_ref[...] = (g_ref[...].astype(F32) + r_ref[...].astype(F32)).astype(BF16)

    return pl.pallas_call(
        body,
        grid_spec=pltpu.PrefetchScalarGridSpec(
            num_scalar_prefetch=1, grid=(n,),
            in_specs=[pl.BlockSpec((None, None, r2, cols), lambda k, c: (k, c[0], 0, 0)),
                      pl.BlockSpec((None, r2, cols), lambda k, c: (k, 0, 0))],
            out_specs=pl.BlockSpec((None, r2, cols), lambda k, c: (k, 0, 0))),
        out_shape=jax.ShapeDtypeStruct((n, r2, cols), BF16), compiler_params=_cparams(), name=name)(c_idx, g, recv)


def _sum_chips(parts, name):
    n, r2, cols = parts.shape
    tr = r2 // 2 if (r2 // 2) % 16 == 0 else r2

    def body(p_ref, o_ref):
        o_ref[...] = ((p_ref[0].astype(F32) + p_ref[1].astype(F32)) + p_ref[2].astype(F32)) + p_ref[3].astype(F32)

    return pl.pallas_call(
        body, grid=(r2 // tr,), out_shape=jax.ShapeDtypeStruct((r2, cols), F32),
        in_specs=[pl.BlockSpec((n, tr, cols), lambda i: (0, i, 0))], out_specs=pl.BlockSpec((tr, cols), lambda i: (i, 0)),
        compiler_params=_cparams(), name=name)(parts)


def _place():
    x, y, c = lax.axis_index("x"), lax.axis_index("y"), lax.axis_index("c")
    return x, y, c, 2 * x + y


def _flip(v, bit):
    return 1 - v if bit else v


def _allgather8(v, name, deps=()):
    r, n = v.shape

    def body(*refs):
        v_ref, out_ref, send_sems, recv_sems, local_sem = refs[len(deps):]
        x, y, c, _ = _place()
        me = 4 * x + 2 * y + c
        mine = pltpu.make_async_copy(v_ref, out_ref.at[me], local_sem)
        mine.start()
        sends = []
        for rel in range(1, N_DEV):
            peer = (_flip(x, rel & 4), _flip(y, rel & 2), _flip(c, rel & 1))
            cp = pltpu.make_async_remote_copy(v_ref, out_ref.at[me], send_sems.at[rel - 1], recv_sems.at[rel - 1],
                                              device_id=peer, device_id_type=MESH)
            cp.start()
            sends.append(cp)
        for rel in range(1, N_DEV):
            peer = (_flip(x, rel & 4), _flip(y, rel & 2), _flip(c, rel & 1))
            peer_idx = 4 * peer[0] + 2 * peer[1] + peer[2]
            pltpu.make_async_remote_copy(v_ref, out_ref.at[peer_idx], send_sems.at[rel - 1], recv_sems.at[rel - 1],
                                         device_id=peer, device_id_type=MESH).wait_recv()
        for cp in sends:
            cp.wait_send()
        mine.wait()

    return pl.pallas_call(
        body, out_shape=jax.ShapeDtypeStruct((N_DEV, r, n), F32), in_specs=[_ANY] * len(deps) + [_VMEM], out_specs=_VMEM,
        scratch_shapes=[pltpu.SemaphoreType.DMA((N_DEV - 1,)), pltpu.SemaphoreType.DMA((N_DEV - 1,)), pltpu.SemaphoreType.DMA(())],
        name=name)(*deps, v)


def _gather_weights(shards):
    nw = len(shards)

    def body(*refs):
        w_refs, out_refs = refs[:nw], refs[nw:2 * nw]
        send_sems, recv_sems = refs[2 * nw:]
        x, y, c, p = _place()
        sibling = (x, y, 1 - c)
        sends = []
        for j in range(1, N_CHIPS):
            peer = (_flip(x, j & 2), _flip(y, j & 1), c)
            for w in range(nw):
                cp = pltpu.make_async_remote_copy(w_refs[w].at[c], out_refs[w].at[p, c], send_sems.at[w * 6 + j - 1],
                                                  recv_sems.at[w * 6 + j - 1], device_id=peer, device_id_type=MESH)
                cp.start()
                sends.append(cp)
        for j in range(1, N_CHIPS):
            peer = (_flip(x, j & 2), _flip(y, j & 1), c)
            q = 2 * peer[0] + peer[1]
            for w in range(nw):
                pltpu.make_async_remote_copy(w_refs[w].at[c], out_refs[w].at[q, c], send_sems.at[w * 6 + j - 1],
                                             recv_sems.at[w * 6 + j - 1], device_id=peer, device_id_type=MESH).wait_recv()
                cp = pltpu.make_async_remote_copy(out_refs[w].at[q, c], out_refs[w].at[q, c], send_sems.at[w * 6 + 2 + j],
                                                  recv_sems.at[w * 6 + 2 + j], device_id=sibling, device_id_type=MESH)
                cp.start()
                sends.append(cp)
        for j in range(1, N_CHIPS):
            q = 2 * _flip(x, j & 2) + _flip(y, j & 1)
            for w in range(nw):
                pltpu.make_async_remote_copy(out_refs[w].at[q, 1 - c], out_refs[w].at[q, 1 - c], send_sems.at[w * 6 + 2 + j],
                                             recv_sems.at[w * 6 + 2 + j], device_id=sibling, device_id_type=MESH).wait_recv()
        for cp in sends:
            cp.wait_send()

    return pl.pallas_call(
        body, out_shape=tuple(jax.ShapeDtypeStruct((N_CHIPS,) + s.shape, s.dtype) for s in shards),
        in_specs=[_ANY] * nw, out_specs=[_ANY] * nw,
        scratch_shapes=[pltpu.SemaphoreType.DMA((6 * nw,)), pltpu.SemaphoreType.DMA((6 * nw,))],
        name="gather_weights")(*shards)


def _swap_halves(grads):
    nw = len(grads)

    def body(*refs):
        g_refs, out_refs = refs[:nw], refs[nw:2 * nw]
        send_sems, recv_sems = refs[2 * nw:]
        x, y, c, _ = _place()
        sibling = (x, y, 1 - c)
        sends = []
        for w in range(nw):
            for k in range(N_CHIPS):
                cp = pltpu.make_async_remote_copy(g_refs[w].at[k, 1 - c], out_refs[w].at[k], send_sems.at[w * N_CHIPS + k],
                                                  recv_sems.at[w * N_CHIPS + k], device_id=sibling, device_id_type=MESH)
                cp.start()
                sends.append(cp)
        for cp in sends:
            cp.wait_recv()
        for cp in sends:
            cp.wait_send()

    return pl.pallas_call(
        body, out_shape=tuple(jax.ShapeDtypeStruct((N_CHIPS,) + g.shape[2:], g.dtype) for g in grads),
        in_specs=[_ANY] * nw, out_specs=[_ANY] * nw,
        scratch_shapes=[pltpu.SemaphoreType.DMA((N_CHIPS * nw,)), pltpu.SemaphoreType.DMA((N_CHIPS * nw,))],
        name="swap_halves")(*grads)


def _scatter_chips(parts):
    nw = len(parts)

    def body(*refs):
        p_refs, out_refs = refs[:nw], refs[nw:2 * nw]
        send_sems, recv_sems = refs[2 * nw:]
        x, y, c, p = _place()
        sends = []
        for j in range(1, N_CHIPS):
            peer = (_flip(x, j & 2), _flip(y, j & 1), c)
            q = 2 * peer[0] + peer[1]
            for w in range(nw):
                cp = pltpu.make_async_remote_copy(p_refs[w].at[q], out_refs[w].at[p], send_sems.at[w * 3 + j - 1],
                                                  recv_sems.at[w * 3 + j - 1], device_id=peer, device_id_type=MESH)
                cp.start()
                sends.append(cp)
        for j in range(1, N_CHIPS):
            peer = (_flip(x, j & 2), _flip(y, j & 1), c)
            q = 2 * peer[0] + peer[1]
            for w in range(nw):
                pltpu.make_async_remote_copy(p_refs[w].at[q], out_refs[w].at[q], send_sems.at[w * 3 + j - 1],
                                             recv_sems.at[w * 3 + j - 1], device_id=peer, device_id_type=MESH).wait_recv()
        for cp in sends:
            cp.wait_send()

    return pl.pallas_call(
        body, out_shape=tuple(jax.ShapeDtypeStruct(s.shape, s.dtype) for s in parts),
        in_specs=[_ANY] * nw, out_specs=[_ANY] * nw,
        scratch_shapes=[pltpu.SemaphoreType.DMA((3 * nw,)), pltpu.SemaphoreType.DMA((3 * nw,))],
        name="scatter_chips")(*parts)


def _share_halves(halves):
    nw = len(halves)

    def body(*refs):
        h_refs, out_refs = refs[:nw], refs[nw:2 * nw]
        send_sems, recv_sems = refs[2 * nw:]
        x, y, c, _ = _place()
        sends = []
        for w in range(nw):
            cp = pltpu.make_async_remote_copy(h_refs[w], out_refs[w], send_sems.at[w], recv_sems.at[w],
                                              device_id=(x, y, 1 - c), device_id_type=MESH)
            cp.start()
            sends.append(cp)
        for cp in sends:
            cp.wait_recv()
        for cp in sends:
            cp.wait_send()

    return pl.pallas_call(
        body, out_shape=tuple(jax.ShapeDtypeStruct(s.shape, s.dtype) for s in halves),
        in_specs=[_ANY] * nw, out_specs=[_ANY] * nw,
        scratch_shapes=[pltpu.SemaphoreType.DMA((nw,)), pltpu.SemaphoreType.DMA((nw,))],
        name="share_halves")(*halves)


_HBM = pl.BlockSpec(memory_space=pltpu.HBM)
_SEM = pl.BlockSpec(memory_space=pltpu.SEMAPHORE)
_EFFECT = pltpu.SideEffectType.DATAFLOW_SIDE_EFFECTING


def _xchg_start(name, plan, n_copies, srcs, lands, after=()):
    bufs = list(srcs) + list(lands)
    ns, nb = len(srcs), len(srcs) + len(lands)

    def body(*refs):
        send_sems, recv_sems, token = refs[nb + len(after)], refs[nb + len(after) + 1], refs[-1]
        for i, (src, dst, peer, _) in enumerate(plan(_place(), refs[:ns], refs[ns:nb])):
            pltpu.make_async_remote_copy(src, dst, send_sems.at[i], recv_sems.at[i], device_id=peer, device_id_type=MESH).start()
        token[...] = jnp.zeros_like(token)

    out = pl.pallas_call(
        body, name=name,
        out_shape=(pltpu.SemaphoreType.DMA((n_copies,)), pltpu.SemaphoreType.DMA((n_copies,)),
                   *[pltpu.HBM(a.shape, a.dtype) for a in bufs], jax.ShapeDtypeStruct((8, 128), F32)),
        in_specs=[_HBM] * nb + [_ANY] * len(after), out_specs=(_SEM, _SEM, *[_HBM] * nb, _VMEM),
        input_output_aliases={i: 2 + i for i in range(nb)},
        compiler_params=pltpu.CompilerParams(has_side_effects=_EFFECT),
    )(*[pltpu.with_memory_space_constraint(a, pltpu.HBM) for a in bufs], *after)
    return (out[0], out[1]), out[2:2 + ns], out[2 + ns:2 + nb], out[-1]


def _xchg_wait(name, plan, sems, srcs, lands, after):
    bufs = list(srcs) + list(lands)
    ns, nb = len(srcs), len(srcs) + len(lands)

    def body(*refs):
        send_sems, recv_sems = refs[nb], refs[nb + 1]
        for i, (src, _, peer, mine) in enumerate(plan(_place(), refs[:ns], refs[ns:nb])):
            cp = pltpu.make_async_remote_copy(src, mine, send_sems.at[i], recv_sems.at[i], device_id=peer, device_id_type=MESH)
            cp.wait_send()
            cp.wait_recv()

    out = pl.pallas_call(
        body, name=name, out_shape=tuple(pltpu.HBM(a.shape, a.dtype) for a in bufs),
        in_specs=[_HBM] * nb + [_SEM, _SEM] + [_ANY] * len(after), out_specs=tuple([_HBM] * nb),
        input_output_aliases={i: i for i in range(nb)},
        compiler_params=pltpu.CompilerParams(has_side_effects=_EFFECT),
    )(*bufs, *sems, *after)
    return out[:ns], out[ns:]


def _other_chips(place):
    x, y, c, _ = place
    return [((_flip(x, j & 2), _flip(y, j & 1), c), 2 * _flip(x, j & 2) + _flip(y, j & 1)) for j in range(1, N_CHIPS)]


def _plan_gather_ici(place, src_refs, land_refs):
    _, _, c, p = place
    return [(s.at[c], l.at[p, c], peer, l.at[q, c]) for s, l in zip(src_refs, land_refs) for peer, q in _other_chips(place)]


def _plan_gather_d2d(place, src_refs, land_refs):
    x, y, c, _ = place
    return [(l.at[q, c], l.at[q, c], (x, y, 1 - c), l.at[q, 1 - c]) for l in land_refs for _, q in _other_chips(place)]


def _plan_swap(place, src_refs, land_refs):
    x, y, c, _ = place
    return [(s.at[k, 1 - c], l.at[k], (x, y, 1 - c), l.at[k]) for s, l in zip(src_refs, land_refs) for k in range(N_CHIPS)]


def _plan_scatter(place, src_refs, land_refs):
    _, _, _, p = place
    return [(s.at[q], l.at[p], peer, l.at[q]) for s, l in zip(src_refs, land_refs) for peer, q in _other_chips(place)]


def _plan_share(place, src_refs, land_refs):
    x, y, c, _ = place
    return [(s, l, (x, y, 1 - c), l) for s, l in zip(src_refs, land_refs)]


def _pack_rows(parts, n_rows, name, deps=()):
    def body(*refs):
        refs = refs[len(deps):]
        out_ref = refs[-1]
        out_ref[...] = jnp.zeros((n_rows, D), F32)
        at = 0
        for ref in refs[:-1]:
            k = ref.shape[0]
            out_ref[at:at + k, :] = ref[...]
            at += k

    return pl.pallas_call(
        body, out_shape=jax.ShapeDtypeStruct((n_rows, D), F32), in_specs=[_ANY] * len(deps) + [_VMEM] * len(parts),
        out_specs=_VMEM, name=name)(*deps, *parts)


TS_MM = 512
TS_MIX = 256


def _halved(a):
    n, r, cols = a.shape
    return a.reshape(n, 2, r // 2, cols)


def _gather_begin(name, shards, chip, after=()):
    lands = [lax.dynamic_update_index_in_dim(lax.empty((N_CHIPS,) + s.shape, s.dtype), s, chip, 0) for s in shards]
    sems, srcs, lands, token = _xchg_start(name + "_ici", _plan_gather_ici, 3 * len(shards), shards, lands, after)
    return name, sems, srcs, lands, token


def _gather_relay(handle, after):
    name, sems, srcs, lands, _ = handle
    _, lands = _xchg_wait(name + "_ici_wait", _plan_gather_ici, sems, srcs, lands, after)
    sems, _, lands, token = _xchg_start(name + "_d2d", _plan_gather_d2d, 3 * len(lands), [], lands)
    return name, sems, lands, token


def _gather_end(handle, after):
    name, sems, lands, _ = handle
    return _xchg_wait(name + "_d2d_wait", _plan_gather_d2d, sems, [], lands, after)[1]


def _rs_swap(name, grads):
    lands = [lax.empty((N_CHIPS,) + g.shape[2:], g.dtype) for g in grads]
    sems, grads, lands, token = _xchg_start(name + "_swap", _plan_swap, N_CHIPS * len(grads), grads, lands)
    return name, sems, grads, lands, token


def _rs_scatter(handle, after, chip, ci):
    name, sems, grads, lands, _ = handle
    grads, from_sibling = _xchg_wait(name + "_swap_wait", _plan_swap, sems, grads, lands, after)
    c_arr = jnp.reshape(ci, (1,)).astype(jnp.int32)
    pair_sums = [_add_halves(g, r, c_arr, "%s_add_halves_%d" % (name, k)) for k, (g, r) in enumerate(zip(grads, from_sibling))]
    lands = [lax.dynamic_update_index_in_dim(lax.empty(p.shape, p.dtype), lax.dynamic_index_in_dim(p, chip, 0, keepdims=False),
                                             chip, 0) for p in pair_sums]
    sems, pair_sums, lands, token = _xchg_start(name + "_scatter", _plan_scatter, 3 * len(pair_sums), pair_sums, lands)
    return name, sems, pair_sums, lands, token


def _rs_share(handle, after):
    name, sems, pair_sums, lands, _ = handle
    _, by_chip = _xchg_wait(name + "_scatter_wait", _plan_scatter, sems, pair_sums, lands, after)
    halves = [_sum_chips(b, "%s_sum_chips_%d" % (name, k)) for k, b in enumerate(by_chip)]
    lands = [lax.empty(h.shape, h.dtype) for h in halves]
    sems, halves, lands, token = _xchg_start(name + "_share", _plan_share, len(halves), halves, lands)
    return name, sems, halves, lands, token


def _rs_end(handle, after):
    name, sems, halves, lands, _ = handle
    halves, others = _xchg_wait(name + "_share_wait", _plan_share, sems, halves, lands, after)
    return list(zip(halves, others))


def kernel(x, c, w_ada, b_ada, g_norm_mix, w_in, conv_a_w, conv_b_w, conv_b_bias, w_rg_a, b_rg_a, w_rg_x, b_rg_x, lru_lambda, w_out, g_norm_ffn, w_gate_up, w_down, g_norm_final, loss_target, m_w_ada, m_b_ada, m_g_norm_mix, m_w_in, m_conv_a_w, m_conv_b_w, m_conv_b_bias, m_w_rg_a, m_b_rg_a, m_w_rg_x, m_b_rg_x, m_lru_lambda, m_w_out, m_g_norm_ffn, m_w_gate_up, m_w_down, m_g_norm_final, v_w_ada, v_b_ada, v_g_norm_mix, v_w_in, v_conv_a_w, v_conv_b_w, v_conv_b_bias, v_w_rg_a, v_b_rg_a, v_w_rg_x, v_b_rg_x, v_lru_lambda, v_w_out, v_g_norm_ffn, v_w_gate_up, v_w_down, v_g_norm_final):
    xi, yi, ci = lax.axis_index("x"), lax.axis_index("y"), lax.axis_index("c")
    chip = 2 * xi + yi
    me = 2 * chip + ci
    n_ada = w_ada.shape[2]

    def rg_shard(w):
        return w[0].astype(BF16).reshape(2, HEADS * HB // N_CHIPS // 2, HB)

    gather_a = _gather_begin("gather_a", [w_in[0].astype(BF16).reshape(2, D // 2, C_IN), rg_shard(w_rg_a), rg_shard(w_rg_x),
                                          w_out[0].astype(BF16).reshape(2, D // N_CHIPS // 2, D)], chip)

    def widen(w):
        return jnp.pad(w, ((0, 0), (0, D - w.shape[1])))

    got = _allgather8(_pack_rows([c, widen(conv_a_w[0]), widen(conv_b_w[0])], 8, "pack_c_conv", deps=[gather_a[-1]]),
                      "gather_c_conv")
    c_all = got[:, 0, :]
    conv_full = got[::2, 1:8, :D // N_CHIPS].transpose(1, 0, 2).reshape(7, D)

    mod_part, c_act = _ada_fwd(c_all, w_ada[0], lax.dynamic_slice_in_dim(b_ada, chip * n_ada, n_ada, axis=1))
    mod_all = _allgather8(mod_part, "gather_mod")
    mod_mine = lax.dynamic_index_in_dim(mod_all, me, axis=1, keepdims=False)[::2].reshape(6, D)
    vecs = _pack_rows([mod_mine, g_norm_mix, g_norm_ffn, g_norm_final.reshape(1, D), conv_b_bias, b_rg_a, b_rg_x, lru_lambda,
                       conv_full], N_VEC, "pack_vecs")

    gather_b = _gather_begin("gather_b", [w_gate_up[0].astype(BF16).reshape(2, D // 2, C_GU),
                                          w_down[0].astype(BF16).reshape(2, D_FF // N_CHIPS // 2, D)], chip, after=[vecs])
    gather_a = _gather_relay(gather_a, [gather_b[-1]])
    wg_in, wg_rga, wg_rgx, wg_out = _gather_end(gather_a, [])
    wg_in = wg_in.reshape(N_CHIPS, D, C_IN)
    wg_out = wg_out.reshape(D, D)

    def rg_full(wg):
        return wg.reshape(N_CHIPS, HEADS, HB // N_CHIPS, HB).transpose(1, 0, 2, 3).reshape(HEADS, HB, HB)

    wg_rga, wg_rgx = rg_full(wg_rga), rg_full(wg_rgx)

    xs, target = x[0], loss_target[0]
    h1, proj = _fwd_in(xs, vecs, wg_in, TS_MM)
    gather_b = _gather_relay(gather_b, [proj])
    x1, merged, z1, u, h = _fwd_mix(proj, xs, vecs, wg_rga, wg_rgx, wg_out, TS_MIX, deps=[gather_b[-1]])
    wg_gu, wg_dn = _gather_end(gather_b, [x1])
    wg_gu, wg_dn = wg_gu.reshape(N_CHIPS, D, C_GU), wg_dn.reshape(D_FF, D)
    dx1, h2, act, dz2, dgu, sm_ffn = _ffn_loss(x1, target, vecs, wg_gu, wg_dn, TS_MIX)

    def rg_chunks(dw):
        return _halved(dw.reshape(HEADS, N_CHIPS, HB // N_CHIPS, HB).transpose(1, 0, 2, 3).reshape(N_CHIPS, HB, HB).astype(BF16))

    g_dn = _grad_w(act, dz2, 1, TS_MM, "grad_w_down")
    g_gu = _grad_w(h2, dgu, N_CHIPS, TS_MM, "grad_w_gate_up")
    rs_b = _rs_swap("rs_b", [_halved(g_gu), _halved(g_dn.reshape(N_CHIPS, D_FF // N_CHIPS, D))])
    dproj, dz1, sm_mix, dw_rga, dw_rgx = _bwd_mix(dx1, z1, proj, u, h, vecs, wg_rga, wg_rgx, wg_out, sm_ffn, TS_MIX,
                                                  deps=[rs_b[-1]])
    rs_b = _rs_scatter(rs_b, [dproj], chip, ci)
    g_in = _grad_w(h1, dproj, N_CHIPS, TS_MM, "grad_w_in", deps=[rs_b[-1]])
    rs_b = _rs_share(rs_b, [g_in])
    g_out = _grad_w(merged, dz1, 1, TS_MM, "grad_w_out", deps=[rs_b[-1]])
    rs_a = _rs_swap("rs_a", [_halved(g_in), rg_chunks(dw_rga), rg_chunks(dw_rgx), _halved(g_out.reshape(N_CHIPS, D // N_CHIPS, D))])

    c_arr = jnp.reshape(ci, (1,)).astype(jnp.int32)

    def step(name, w, g, m, v, deps=()):
        shape = w.shape
        two_d = (-1, shape[-1])
        d, nm, nv = _adamw(w.reshape(two_d), g.reshape(two_d), m.reshape(two_d), v.reshape(two_d), "adamw_" + name, deps)
        return g.reshape(shape), d.reshape(shape), nm.reshape(shape), nv.reshape(shape)

    def step_halves(name, w, halves, m, v, deps=()):
        shape = w.shape
        two_d = (-1, shape[-1])
        out = _adamw_halves(w.reshape(two_d), halves[0], halves[1], m.reshape(two_d), v.reshape(two_d), c_arr, "adamw_" + name, deps)
        return tuple(a.reshape(shape) for a in out)

    def shard_cols(row_block):
        return lax.dynamic_slice_in_dim(row_block, chip * (D // N_CHIPS), D // N_CHIPS, axis=1)

    gw_gu, gw_dn = _rs_end(rs_b, [g_out])
    res = {
        "w_gate_up": step_halves("w_gate_up", w_gate_up, gw_gu, m_w_gate_up, v_w_gate_up, [rs_a[-1]]),
        "w_down": step_halves("w_down", w_down, gw_dn, m_w_down, v_w_down, [rs_a[-1]]),
    }
    rs_a = _rs_scatter(rs_a, [res["w_gate_up"][1], res["w_down"][1]], chip, ci)
    grad_x, sm_in = _bwd_in(dproj, xs, dx1, vecs, wg_in, sm_mix, TS_MM, deps=[rs_a[-1]])
    rs_a = _rs_share(rs_a, [grad_x])

    small, per_dev = _sum_small(_allgather8(sm_in, "gather_small", deps=[rs_a[-1]]))
    dmod_all = per_dev[:, 0:6, :].reshape(N_DEV, 6 * D)
    grad_w_ada = _ada_bwd(c_act, lax.dynamic_slice_in_dim(dmod_all, chip * n_ada, n_ada, axis=1))
    grad_b_ada = small[0:6].reshape(1, 6 * D)
    behind = ()
    res.update({
        "w_ada": step("w_ada", w_ada, grad_w_ada[None], m_w_ada, v_w_ada, behind),
        "b_ada": step("b_ada", b_ada.reshape(6, D), grad_b_ada.reshape(6, D), m_b_ada.reshape(6, D), v_b_ada.reshape(6, D), behind),
        "g_norm_mix": step("g_norm_mix", g_norm_mix, small[G_GMIX:G_GMIX + 1], m_g_norm_mix, v_g_norm_mix, behind),
        "conv_a_w": step("conv_a_w", conv_a_w, shard_cols(small[G_WA0:G_WA0 + 3])[None], m_conv_a_w, v_conv_a_w, behind),
        "conv_b_w": step("conv_b_w", conv_b_w, shard_cols(small[G_WB0:G_WB0 + 4])[None], m_conv_b_w, v_conv_b_w, behind),
        "conv_b_bias": step("conv_b_bias", conv_b_bias, small[G_CBB:G_CBB + 1], m_conv_b_bias, v_conv_b_bias, behind),
        "b_rg_a": step("b_rg_a", b_rg_a, small[G_BA:G_BA + 1], m_b_rg_a, v_b_rg_a, behind),
        "b_rg_x": step("b_rg_x", b_rg_x, small[G_BX:G_BX + 1], m_b_rg_x, v_b_rg_x, behind),
        "lru_lambda": step("lru_lambda", lru_lambda, small[G_LAM:G_LAM + 1], m_lru_lambda, v_lru_lambda, behind),
        "g_norm_ffn": step("g_norm_ffn", g_norm_ffn, small[G_GFFN:G_GFFN + 1], m_g_norm_ffn, v_g_norm_ffn, behind),
        "g_norm_final": step("g_norm_final", g_norm_final.reshape(1, D), small[G_GFIN:G_GFIN + 1], m_g_norm_final.reshape(1, D),
                             v_g_norm_final.reshape(1, D), behind),
    })
    gw_in, gw_rga, gw_rgx, gw_out = _rs_end(rs_a, [res[n][1] for n in res])
    res["w_in"] = step_halves("w_in", w_in, gw_in, m_w_in, v_w_in)
    res["w_rg_a"] = step_halves("w_rg_a", w_rg_a, gw_rga, m_w_rg_a, v_w_rg_a)
    res["w_rg_x"] = step_halves("w_rg_x", w_rg_x, gw_rgx, m_w_rg_x, v_w_rg_x)
    res["w_out"] = step_halves("w_out", w_out, gw_out, m_w_out, v_w_out)
    res["b_ada"] = tuple(a.reshape(1, 6 * D) for a in res["b_ada"])
    res["g_norm_final"] = tuple(a.reshape(D) for a in res["g_norm_final"])
    names = ["w_ada", "b_ada", "g_norm_mix", "w_in", "conv_a_w", "conv_b_w", "conv_b_bias", "w_rg_a", "b_rg_a", "w_rg_x",
             "b_rg_x", "lru_lambda", "w_out", "g_norm_ffn", "w_gate_up", "w_down", "g_norm_final"]
    loss = jnp.sum(small[G_LOSS])
    return (loss, grad_x[None], *[res[n][0] for n in names], *[res[n][1] for n in names],
            *[res[n][2] for n in names], *[res[n][3] for n in names])
```

```python
import functools

import jax
import jax.numpy as jnp
from jax import lax
from jax.experimental import pallas as pl
from jax.experimental.pallas import tpu as pltpu

F32 = jnp.float32
BF16 = jnp.bfloat16
MESH = pl.DeviceIdType.MESH

D = 1024
N_CHIPS = 4
N_DEV = 8
D_IN = 7 * D
C_IN = D_IN // N_CHIPS
D_FF = 2816
C_GU = 2 * D_FF // N_CHIPS
HEADS = 4
HB = D // HEADS
EPS = 1e-6
LRU_C = 8.0
ADAM_LR, ADAM_B1, ADAM_B2, ADAM_EPS, ADAM_WD, ADAM_STEP = 0.001, 0.9, 0.999, 1e-08, 0.01, 10
VMEM_LIMIT = 56 << 20

(V_SH1, V_SC1, V_GT1, V_SH2, V_SC2, V_GT2, V_GMIX, V_GFFN, V_GFIN, V_CBB, V_BA, V_BX, V_LAM,
 V_WA0, V_WA1, V_WA2, V_WB0, V_WB1, V_WB2, V_WB3) = range(20)
N_VEC = 24
(G_SH1, G_SC1, G_GT1, G_SH2, G_SC2, G_GT2, G_GMIX, G_CBB, G_BA, G_BX, G_LAM, G_GFFN, G_GFIN,
 G_WA0, G_WA1, G_WA2, G_WB0, G_WB1, G_WB2, G_WB3, G_LOSS) = range(21)
N_SMALL = 24

_VMEM = pl.BlockSpec(memory_space=pltpu.VMEM)
_ANY = pl.BlockSpec(memory_space=pl.ANY)


def _cparams(n_grid=1):
    return pltpu.CompilerParams(dimension_semantics=("arbitrary",) * n_grid, vmem_limit_bytes=VMEM_LIMIT)


def _after(deps, body):
    n = len(deps)
    return lambda *refs: body(*refs[n:])


def _rms(x):
    rstd = lax.rsqrt(jnp.mean(x * x, axis=-1, keepdims=True) + EPS)
    return x * rstd, rstd


def _rms_bwd(dxhat, xhat, rstd):
    return rstd * (dxhat - xhat * jnp.mean(dxhat * xhat, axis=-1, keepdims=True))


def _rowsum(v):
    return jnp.sum(v, axis=0, keepdims=True)


def _dot(a, b):
    return jnp.dot(a, b, preferred_element_type=F32)


def _dot_nt(a, b):
    return lax.dot_general(a, b, (((1,), (1,)), ((), ())), preferred_element_type=F32)


def _dot_tn(a, b):
    return lax.dot_general(a, b, (((0,), (0,)), ((), ())), preferred_element_type=F32)


def _gelu(x):
    k, c = 0.7978845608028654, 0.044715
    t = jnp.tanh(k * (x + c * x * x * x))
    return 0.5 * x * (1.0 + t), 0.5 * (1.0 + t) + 0.5 * x * (1.0 - t * t) * k * (1.0 + 3.0 * c * x * x)


def _log_sigmoid(lam):
    return jnp.minimum(lam, 0.0) - jnp.log1p(jnp.exp(-jnp.abs(lam)))


def _lru_gates(u, wa_ref, wx_ref, v_ref, row0):
    ub = u.astype(BF16)
    pre_a = jnp.concatenate([_dot(ub[:, h * HB:(h + 1) * HB], wa_ref[h]) for h in range(HEADS)], axis=1)
    pre_x = jnp.concatenate([_dot(ub[:, h * HB:(h + 1) * HB], wx_ref[h]) for h in range(HEADS)], axis=1)
    r = jax.nn.sigmoid(pre_a + v_ref[V_BA:V_BA + 1, :])
    ig = jax.nn.sigmoid(pre_x + v_ref[V_BX:V_BX + 1, :])
    log_a = LRU_C * r * _log_sigmoid(v_ref[V_LAM:V_LAM + 1, :])
    a = jnp.exp(log_a)
    x2 = 2.0 * log_a
    m2 = jnp.where(x2 > -0.03, -x2 * (1.0 + x2 * (0.5 + x2 * (1.0 / 6.0 + x2 * (1.0 / 24.0)))), 1.0 - a * a)
    mult = jnp.where(row0, 1.0, jnp.sqrt(jnp.maximum(m2, 0.0)))
    return r, ig, a, mult


TIME_BLOCKS = 8


def _late_blocks(v, buf, g, halo=None):
    n = buf.shape[0]
    out = []
    for idx in range(n):
        k = TIME_BLOCKS - n + idx
        buf[idx, 8:g + 8, :] = v[k * g:(k + 1) * g]
        if halo is not None:
            buf[idx, 7:8, :] = halo[idx]
        out.append(buf[idx, pl.ds(7, g), :])
        if halo is None:
            buf[idx, 7:8, :] = buf[idx, g + 7:g + 8, :]
    return out


def _earlier(v, s, late, g):
    return jnp.concatenate(late[len(late) - s:] + [v[0:(TIME_BLOCKS - s) * g]], axis=0)


def _early_blocks(v, buf, g):
    out = []
    for k in range(buf.shape[0]):
        buf[k, 0:g, :] = v[k * g:(k + 1) * g]
        out.append(buf[k, pl.ds(1, g), :])
        buf[k, g:g + 1, :] = buf[k, 0:1, :]
    return out


def _later(v, s, early, g):
    return jnp.concatenate([v[s * g:]] + early[0:s], axis=0)


def _fwd_in(x, vecs, w_in_g, ts, deps=()):
    s = x.shape[0]

    def body(x_ref, v_ref, w_ref, h1_ref, proj_ref):
        xhat, _ = _rms(x_ref[...])
        h = xhat * v_ref[V_GMIX:V_GMIX + 1, :] * (1.0 + v_ref[V_SC1:V_SC1 + 1, :]) + v_ref[V_SH1:V_SH1 + 1, :]
        hb = h.astype(BF16)
        h1_ref[...] = hb
        for k in range(N_CHIPS):
            proj_ref[:, k * C_IN:(k + 1) * C_IN] = _dot(hb, w_ref[k]).astype(BF16)

    return pl.pallas_call(
        _after(deps, body), grid=(s // ts,),
        out_shape=(jax.ShapeDtypeStruct((s, D), BF16), jax.ShapeDtypeStruct((s, D_IN), BF16)),
        in_specs=[_ANY] * len(deps) + [pl.BlockSpec((ts, D), lambda i: (i, 0)), _VMEM, _VMEM],
        out_specs=[pl.BlockSpec((ts, D), lambda i: (i, 0)), pl.BlockSpec((ts, D_IN), lambda i: (i, 0))],
        compiler_params=_cparams(), name="fwd_in")(*deps, x, vecs, w_in_g)


def _fwd_mix(proj, x, vecs, w_rga, w_rgx, w_out, ts, deps=()):
    s = x.shape[0]
    g = ts // TIME_BLOCKS

    def body(proj_ref, x_ref, v_ref, wa_ref, wx_ref, wo_ref, x1_ref, mg_ref, z1_ref, u_ref, h_ref,
             ua_buf, rx_buf, p_buf, q_buf, c_buf, hcarry):
        i = pl.program_id(0)

        @pl.when(i == 0)
        def _():
            ua_buf[...] = jnp.zeros(ua_buf.shape, F32)
            rx_buf[...] = jnp.zeros(rx_buf.shape, F32)
            hcarry[...] = jnp.zeros((8, D), F32)

        def seg(j):
            return proj_ref[:, j * D:(j + 1) * D].astype(F32)

        def vrow(j):
            return v_ref[j:j + 1, :]

        cb, cc, cx, rx, rg, ga, gb = (seg(j) for j in range(7))
        ua = cc * cx
        ua_late = _late_blocks(ua, ua_buf, g)
        rx_late = _late_blocks(rx, rx_buf, g)
        va = vrow(V_WA2) * ua + vrow(V_WA1) * _earlier(ua, 1, ua_late, g) + vrow(V_WA0) * _earlier(ua, 2, ua_late, g)
        u = (vrow(V_WB3) * rx + vrow(V_WB2) * _earlier(rx, 1, rx_late, g) + vrow(V_WB1) * _earlier(rx, 2, rx_late, g)
             + vrow(V_WB0) * _earlier(rx, 3, rx_late, g) + vrow(V_CBB))
        u_ref[...] = u

        rows = lax.broadcasted_iota(jnp.int32, (ts, D), 0)
        row0 = jnp.logical_and(rows == 0, i == 0)
        _, ig, a, mult = _lru_gates(u, wa_ref, wx_ref, v_ref, row0)
        bx = mult * (ig * u)

        prods, sums = [a[0:g]], [bx[0:g]]
        for k in range(1, TIME_BLOCKS):
            ak = a[k * g:(k + 1) * g]
            sums.append(ak * sums[-1] + bx[k * g:(k + 1) * g])
            prods.append(ak * prods[-1])
        p_buf[...] = prods[-1]
        q_buf[...] = sums[-1]
        state = hcarry[0:1, :]
        for j in range(g):
            c_buf[j:j + 1, :] = state
            state = p_buf[j:j + 1, :] * state + q_buf[j:j + 1, :]
        hcarry[0:1, :] = state
        entering = c_buf[...]
        h = jnp.concatenate([sums[k] + prods[k] * entering for k in range(TIME_BLOCKS)], axis=0)
        h_ref[...] = h

        gel, _ = _gelu(rg)
        merged = (jax.nn.sigmoid(ga) * (cb * va) + jax.nn.sigmoid(gb) * (h * gel)).astype(BF16)
        mg_ref[...] = merged
        z1 = _dot(merged, wo_ref[...])
        z1_ref[...] = z1.astype(BF16)
        x1_ref[...] = x_ref[...] + vrow(V_GT1) * z1

    row = lambda i: (i, 0)
    return pl.pallas_call(
        _after(deps, body), grid=(s // ts,),
        out_shape=(jax.ShapeDtypeStruct((s, D), F32), jax.ShapeDtypeStruct((s, D), BF16), jax.ShapeDtypeStruct((s, D), BF16),
                   jax.ShapeDtypeStruct((s, D), F32), jax.ShapeDtypeStruct((s, D), F32)),
        in_specs=[_ANY] * len(deps) + [pl.BlockSpec((ts, D_IN), row), pl.BlockSpec((ts, D), row), _VMEM, _VMEM, _VMEM, _VMEM],
        out_specs=[pl.BlockSpec((ts, D), row)] * 5,
        scratch_shapes=[pltpu.VMEM((2, g + 8, D), F32), pltpu.VMEM((3, g + 8, D), F32), pltpu.VMEM((g, D), F32),
                        pltpu.VMEM((g, D), F32), pltpu.VMEM((g, D), F32), pltpu.VMEM((8, D), F32)],
        compiler_params=_cparams(), name="fwd_mix")(*deps, proj, x, vecs, w_rga, w_rgx, w_out)


def _ffn_loss(x1, target, vecs, w_gu_g, w_dn, ts):
    s = x1.shape[0]

    def body(x1_ref, t_ref, v_ref, wgu_ref, wdn_ref, dx1_ref, h2_ref, act_ref, dz2_ref, dgu_ref, sm_ref):
        @pl.when(pl.program_id(0) == 0)
        def _():
            sm_ref[...] = jnp.zeros((N_SMALL, D), F32)

        def vrow(j):
            return v_ref[j:j + 1, :]

        def acc(j, val):
            sm_ref[j:j + 1, :] += _rowsum(val)

        x1 = x1_ref[...]
        xh1, rstd1 = _rms(x1)
        n2 = xh1 * vrow(V_GFFN)
        h2 = (n2 * (1.0 + vrow(V_SC2)) + vrow(V_SH2)).astype(BF16)
        h2_ref[...] = h2
        g = jnp.concatenate([_dot(h2, wgu_ref[0]), _dot(h2, wgu_ref[1])], axis=1)
        up = jnp.concatenate([_dot(h2, wgu_ref[2]), _dot(h2, wgu_ref[3])], axis=1)
        sg = jax.nn.sigmoid(g)
        silu = g * sg
        act = (silu * up).astype(BF16)
        act_ref[...] = act
        z2 = _dot(act, wdn_ref[...])
        x2 = x1 + vrow(V_GT2) * z2
        xh2, rstd2 = _rms(x2)
        err = xh2 * vrow(V_GFIN) - t_ref[...]
        acc(G_LOSS, (0.5 / D) * err * err)
        dy = err * (1.0 / D)
        acc(G_GFIN, dy * xh2)
        dx2 = _rms_bwd(dy * vrow(V_GFIN), xh2, rstd2)
        acc(G_GT2, dx2 * z2)
        dz2 = (vrow(V_GT2) * dx2).astype(BF16)
        dz2_ref[...] = dz2
        dact = _dot_nt(dz2, wdn_ref[...])
        dgate = (dact * up * (sg * (1.0 + g * (1.0 - sg)))).astype(BF16)
        dup = (dact * silu).astype(BF16)
        dgu_ref[:, 0:D_FF] = dgate
        dgu_ref[:, D_FF:2 * D_FF] = dup
        dh2 = (_dot_nt(dgate[:, 0:C_GU], wgu_ref[0]) + _dot_nt(dgate[:, C_GU:2 * C_GU], wgu_ref[1])
               + _dot_nt(dup[:, 0:C_GU], wgu_ref[2]) + _dot_nt(dup[:, C_GU:2 * C_GU], wgu_ref[3]))
        acc(G_SH2, dh2)
        acc(G_SC2, dh2 * n2)
        dn2 = dh2 * (1.0 + vrow(V_SC2))
        acc(G_GFFN, dn2 * xh1)
        dx1_ref[...] = dx2 + _rms_bwd(dn2 * vrow(V_GFFN), xh1, rstd1)

    row = lambda i: (i, 0)
    return pl.pallas_call(
        body, grid=(s // ts,),
        out_shape=(jax.ShapeDtypeStruct((s, D), F32), jax.ShapeDtypeStruct((s, D), BF16), jax.ShapeDtypeStruct((s, D_FF), BF16),
                   jax.ShapeDtypeStruct((s, D), BF16), jax.ShapeDtypeStruct((s, 2 * D_FF), BF16),
                   jax.ShapeDtypeStruct((N_SMALL, D), F32)),
        in_specs=[pl.BlockSpec((ts, D), row), pl.BlockSpec((ts, D), row), _VMEM, _VMEM, _VMEM],
        out_specs=[pl.BlockSpec((ts, D), row), pl.BlockSpec((ts, D), row), pl.BlockSpec((ts, D_FF), row),
                   pl.BlockSpec((ts, D), row), pl.BlockSpec((ts, 2 * D_FF), row), pl.BlockSpec((N_SMALL, D), lambda i: (0, 0))],
        compiler_params=_cparams(), name="ffn_loss")(x1, target, vecs, w_gu_g, w_dn)


def _bwd_mix(dx1, z1, proj, u, h, vecs, w_rga, w_rgx, w_out, small, ts, deps=()):
    s = dx1.shape[0]
    nt = s // ts
    g = ts // TIME_BLOCKS
    assert g % 16 == 0

    def body(dx1_ref, z1_ref, proj_ref, u_ref, h_ref, hh_ref, cc6_ref, cx6_ref, cc7_ref, cx7_ref, v_ref, wa_ref, wx_ref, wo_ref,
             sm0_ref, dproj_ref, dz1_ref, sm_ref, dwa_ref, dwx_ref,
             ua_buf, h_buf, a_buf, dva_buf, du_buf, p_buf, q_buf, c_buf, lcarry):
        i = pl.program_id(0)
        first_tile = i == nt - 1

        @pl.when(i == 0)
        def _():
            a_buf[...] = jnp.zeros(a_buf.shape, F32)
            dva_buf[...] = jnp.zeros(dva_buf.shape, F32)
            du_buf[...] = jnp.zeros(du_buf.shape, F32)
            lcarry[...] = jnp.zeros((8, D), F32)
            sm_ref[...] = sm0_ref[...]
            dwa_ref[...] = jnp.zeros((HEADS, HB, HB), F32)
            dwx_ref[...] = jnp.zeros((HEADS, HB, HB), F32)

        def seg(j):
            return proj_ref[:, j * D:(j + 1) * D].astype(F32)

        def vrow(j):
            return v_ref[j:j + 1, :]

        def acc(j, val):
            sm_ref[j:j + 1, :] += _rowsum(val)

        cb, cc, cx, rx, rg, ga, gb = (seg(j) for j in range(7))
        ua = cc * cx

        def last_row(v):
            pick = lax.broadcasted_iota(jnp.int32, v.shape, 0) == v.shape[0] - 1
            return jnp.where(first_tile, 0.0, jnp.sum(jnp.where(pick, v, 0.0), axis=0, keepdims=True))

        ua_halo = [last_row(cc6_ref[...].astype(F32) * cx6_ref[...].astype(F32)),
                   last_row(cc7_ref[...].astype(F32) * cx7_ref[...].astype(F32))]
        ua_late = _late_blocks(ua, ua_buf, g, ua_halo)
        va = vrow(V_WA2) * ua + vrow(V_WA1) * _earlier(ua, 1, ua_late, g) + vrow(V_WA0) * _earlier(ua, 2, ua_late, g)
        u = u_ref[...]
        h = h_ref[...]
        rows = lax.broadcasted_iota(jnp.int32, (ts, D), 0)
        row0 = jnp.logical_and(rows == 0, first_tile)
        r, ig, a, mult = _lru_gates(u, wa_ref, wx_ref, v_ref, row0)
        sga = jax.nn.sigmoid(ga)
        sgb = jax.nn.sigmoid(gb)
        gel, dgel = _gelu(rg)

        dx1 = dx1_ref[...]
        acc(G_GT1, dx1 * z1_ref[...].astype(F32))
        dz1 = (vrow(V_GT1) * dx1).astype(BF16)
        dz1_ref[...] = dz1
        dmg = _dot_nt(dz1, wo_ref[...])
        dproj_ref[:, 5 * D:6 * D] = (dmg * (cb * va) * sga * (1.0 - sga)).astype(BF16)
        dproj_ref[:, 6 * D:7 * D] = (dmg * (h * gel) * sgb * (1.0 - sgb)).astype(BF16)
        dya = dmg * sga
        dyb = dmg * sgb

        dproj_ref[:, 0:D] = (dya * va).astype(BF16)
        dva = dya * cb
        dva_early = _early_blocks(dva, dva_buf, g)
        dva1 = _later(dva, 1, dva_early, g)
        dva2 = _later(dva, 2, dva_early, g)
        dua = vrow(V_WA2) * dva + vrow(V_WA1) * dva1 + vrow(V_WA0) * dva2
        acc(G_WA2, ua * dva)
        acc(G_WA1, ua * dva1)
        acc(G_WA0, ua * dva2)
        dproj_ref[:, D:2 * D] = (dua * cx).astype(BF16)
        dproj_ref[:, 2 * D:3 * D] = (dua * cc).astype(BF16)

        dproj_ref[:, 4 * D:5 * D] = (dyb * h * dgel).astype(BF16)
        a_next = _later(a, 1, _early_blocks(a, a_buf, g), g)
        dh = dyb * gel
        last = TIME_BLOCKS - 1
        prods, sums = {last: a_next[last * g:]}, {last: dh[last * g:]}
        for k in range(last - 1, -1, -1):
            ak = a_next[k * g:(k + 1) * g]
            sums[k] = dh[k * g:(k + 1) * g] + ak * sums[k + 1]
            prods[k] = ak * prods[k + 1]
        p_buf[...] = prods[0]
        q_buf[...] = sums[0]
        state = lcarry[0:1, :]
        for j in range(g - 1, -1, -1):
            c_buf[j:j + 1, :] = state
            state = q_buf[j:j + 1, :] + p_buf[j:j + 1, :] * state
        lcarry[0:1, :] = state
        entering = c_buf[...]
        lam = jnp.concatenate([sums[k] + prods[k] * entering for k in range(TIME_BLOCKS)], axis=0)

        h_halo = [jnp.where(first_tile, 0.0, hh_ref[7:8, :])]
        da = lam * _earlier(h, 1, _late_blocks(h, h_buf, g, h_halo), g)
        dmult = jnp.where(row0, 0.0, lam * (ig * u))
        di = lam * mult * u
        du = lam * mult * ig
        dlog_a = da * a - dmult * (a * a) / mult
        lam_p = vrow(V_LAM)
        dr = dlog_a * (LRU_C * _log_sigmoid(lam_p))
        sm_ref[G_LAM:G_LAM + 1, :] += _rowsum(dlog_a * r) * (LRU_C * jax.nn.sigmoid(-lam_p))
        dpa = dr * r * (1.0 - r)
        dpx = di * ig * (1.0 - ig)
        acc(G_BA, dpa)
        acc(G_BX, dpx)
        dpab = dpa.astype(BF16)
        dpxb = dpx.astype(BF16)
        ub = u.astype(BF16)
        back = []
        for hd in range(HEADS):
            cols = slice(hd * HB, (hd + 1) * HB)
            back.append(_dot_nt(dpab[:, cols], wa_ref[hd]) + _dot_nt(dpxb[:, cols], wx_ref[hd]))
            dwa_ref[hd] += _dot_tn(ub[:, cols], dpab[:, cols])
            dwx_ref[hd] += _dot_tn(ub[:, cols], dpxb[:, cols])
        du = du + jnp.concatenate(back, axis=1)

        acc(G_CBB, du)
        du_early = _early_blocks(du, du_buf, g)
        du1 = _later(du, 1, du_early, g)
        du2 = _later(du, 2, du_early, g)
        du3 = _later(du, 3, du_early, g)
        dproj_ref[:, 3 * D:4 * D] = (vrow(V_WB3) * du + vrow(V_WB2) * du1 + vrow(V_WB1) * du2 + vrow(V_WB0) * du3).astype(BF16)
        acc(G_WB3, rx * du)
        acc(G_WB2, rx * du1)
        acc(G_WB1, rx * du2)
        acc(G_WB0, rx * du3)

    rev = lambda i: (nt - 1 - i, 0)
    halo8 = lambda i: (jnp.maximum((nt - 1 - i) * (ts // 8) - 1, 0), 0)
    const2 = lambda i: (0, 0)
    const3 = lambda i: (0, 0, 0)

    def halo16(back, col):
        return pl.BlockSpec((16, D), lambda i: (jnp.maximum((nt - 1 - i) * (ts // 16) - back, 0), col))
    return pl.pallas_call(
        _after(deps, body), grid=(nt,),
        out_shape=(jax.ShapeDtypeStruct((s, D_IN), BF16), jax.ShapeDtypeStruct((s, D), BF16),
                   jax.ShapeDtypeStruct((N_SMALL, D), F32), jax.ShapeDtypeStruct((HEADS, HB, HB), F32),
                   jax.ShapeDtypeStruct((HEADS, HB, HB), F32)),
        in_specs=[_ANY] * len(deps) + [pl.BlockSpec((ts, D), rev), pl.BlockSpec((ts, D), rev), pl.BlockSpec((ts, D_IN), rev),
                  pl.BlockSpec((ts, D), rev), pl.BlockSpec((ts, D), rev), pl.BlockSpec((8, D), halo8),
                  halo16(1 + g // 16, 1), halo16(1 + g // 16, 2), halo16(1, 1), halo16(1, 2),
                  _VMEM, _VMEM, _VMEM, _VMEM, _VMEM],
        out_specs=[pl.BlockSpec((ts, D_IN), rev), pl.BlockSpec((ts, D), rev), pl.BlockSpec((N_SMALL, D), const2),
                   pl.BlockSpec((HEADS, HB, HB), const3), pl.BlockSpec((HEADS, HB, HB), const3)],
        scratch_shapes=[pltpu.VMEM((2, g + 8, D), F32), pltpu.VMEM((1, g + 8, D), F32), pltpu.VMEM((1, g + 8, D), F32),
                        pltpu.VMEM((2, g + 8, D), F32), pltpu.VMEM((3, g + 8, D), F32), pltpu.VMEM((g, D), F32),
                        pltpu.VMEM((g, D), F32), pltpu.VMEM((g, D), F32), pltpu.VMEM((8, D), F32)],
        compiler_params=_cparams(), name="bwd_mix")(*deps, dx1, z1, proj, u, h, h, proj, proj, proj, proj, vecs, w_rga, w_rgx,
                                                    w_out, small)


def _bwd_in(dproj, x, dx1, vecs, w_in_g, small, ts, deps=()):
    s = x.shape[0]

    def body(dp_ref, x_ref, dx1_ref, v_ref, w_ref, sm0_ref, gx_ref, sm_ref):
        @pl.when(pl.program_id(0) == 0)
        def _():
            sm_ref[...] = sm0_ref[...]

        def vrow(j):
            return v_ref[j:j + 1, :]

        dh1 = _dot_nt(dp_ref[:, 0:C_IN], w_ref[0])
        for k in range(1, N_CHIPS):
            dh1 += _dot_nt(dp_ref[:, k * C_IN:(k + 1) * C_IN], w_ref[k])
        xh, rstd = _rms(x_ref[...])
        sm_ref[G_SH1:G_SH1 + 1, :] += _rowsum(dh1)
        sm_ref[G_SC1:G_SC1 + 1, :] += _rowsum(dh1 * (xh * vrow(V_GMIX)))
        dn1 = dh1 * (1.0 + vrow(V_SC1))
        sm_ref[G_GMIX:G_GMIX + 1, :] += _rowsum(dn1 * xh)
        gx_ref[...] = dx1_ref[...] + _rms_bwd(dn1 * vrow(V_GMIX), xh, rstd)

    row = lambda i: (i, 0)
    return pl.pallas_call(
        _after(deps, body), grid=(s // ts,),
        out_shape=(jax.ShapeDtypeStruct((s, D), F32), jax.ShapeDtypeStruct((N_SMALL, D), F32)),
        in_specs=[_ANY] * len(deps) + [pl.BlockSpec((ts, D_IN), row), pl.BlockSpec((ts, D), row), pl.BlockSpec((ts, D), row),
                                       _VMEM, _VMEM, _VMEM],
        out_specs=[pl.BlockSpec((ts, D), row), pl.BlockSpec((N_SMALL, D), lambda i: (0, 0))],
        compiler_params=_cparams(), name="bwd_in")(*deps, dproj, x, dx1, vecs, w_in_g, small)


def _grad_w(a, b, n_col_blocks, ts, name, deps=()):
    s, m = a.shape
    tn = b.shape[1] // n_col_blocks
    n_steps = s // ts

    def body(a_ref, b_ref, o_ref, acc_ref):
        k = pl.program_id(1)

        @pl.when(k == 0)
        def _():
            acc_ref[...] = jnp.zeros((m, tn), F32)

        acc_ref[...] += _dot_tn(a_ref[...], b_ref[...])

        @pl.when(k == n_steps - 1)
        def _():
            o_ref[...] = acc_ref[...].astype(BF16)

    return pl.pallas_call(
        _after(deps, body), grid=(n_col_blocks, n_steps),
        out_shape=jax.ShapeDtypeStruct((n_col_blocks, m, tn), BF16),
        in_specs=[_ANY] * len(deps) + [pl.BlockSpec((ts, m), lambda n, k: (k, 0)), pl.BlockSpec((ts, tn), lambda n, k: (k, n))],
        out_specs=pl.BlockSpec((None, m, tn), lambda n, k: (n, 0, 0)),
        scratch_shapes=[pltpu.VMEM((m, tn), F32)],
        compiler_params=_cparams(2), name=name)(*deps, a, b)


def _ada_fwd(c_all, w_ada, b_ada):
    n = w_ada.shape[1]

    def body(c_ref, w_ref, b_ref, o_ref, ca_ref):
        c = c_ref[...]
        ca = c * jax.nn.sigmoid(c)
        ca_ref[...] = ca
        o_ref[...] = jnp.dot(ca, w_ref[...], preferred_element_type=F32, precision=lax.Precision.HIGHEST) + b_ref[...]

    return pl.pallas_call(
        body, out_shape=(jax.ShapeDtypeStruct((N_DEV, n), F32), jax.ShapeDtypeStruct((N_DEV, D), F32)),
        in_specs=[_VMEM] * 3, out_specs=[_VMEM] * 2, compiler_params=_cparams(0), name="ada_fwd")(c_all, w_ada, b_ada)


def _ada_bwd(c_act, dmod):
    n = dmod.shape[1]

    def body(c_ref, d_ref, o_ref):
        o_ref[...] = lax.dot_general(c_ref[...], d_ref[...], (((0,), (0,)), ((), ())), preferred_element_type=F32,
                                     precision=lax.Precision.HIGHEST)

    return pl.pallas_call(
        body, out_shape=jax.ShapeDtypeStruct((D, n), F32), in_specs=[_VMEM] * 2, out_specs=_VMEM,
        compiler_params=_cparams(0), name="ada_bwd")(c_act, dmod)


def _sum_small(parts):
    def body(p_ref, o_ref, d_ref):
        tot = p_ref[0]
        for dev in range(1, N_DEV):
            tot = tot + p_ref[dev]
        o_ref[...] = tot
        d_ref[...] = p_ref[:, 0:8, :]

    return pl.pallas_call(
        body, out_shape=(jax.ShapeDtypeStruct((N_SMALL, D), F32), jax.ShapeDtypeStruct((N_DEV, 8, D), F32)),
        in_specs=[_VMEM], out_specs=[_VMEM] * 2, compiler_params=_cparams(0), name="sum_small")(parts)


def _adamw(w, g, m, v, name, deps=()):
    rows, cols = w.shape
    tr = 128 if rows % 128 == 0 else (64 if rows % 64 == 0 else rows)

    def body(w_ref, g_ref, m_ref, v_ref, d_ref, nm_ref, nv_ref):
        g_ = g_ref[...]
        m_ = ADAM_B1 * m_ref[...] + (1.0 - ADAM_B1) * g_
        v_ = ADAM_B2 * v_ref[...] + (1.0 - ADAM_B2) * (g_ * g_)
        nm_ref[...] = m_
        nv_ref[...] = v_
        m_hat = m_ / (1.0 - ADAM_B1 ** ADAM_STEP)
        v_hat = v_ / (1.0 - ADAM_B2 ** ADAM_STEP)
        d_ref[...] = -ADAM_LR * (m_hat / (jnp.sqrt(v_hat) + ADAM_EPS) + ADAM_WD * w_ref[...])

    spec = pl.BlockSpec((tr, cols), lambda i: (i, 0))
    return pl.pallas_call(
        _after(deps, body), grid=(rows // tr,), out_shape=(jax.ShapeDtypeStruct((rows, cols), F32),) * 3,
        in_specs=[_ANY] * len(deps) + [spec] * 4, out_specs=[spec] * 3, compiler_params=_cparams(), name=name)(*deps, w, g, m, v)


def _adamw_halves(w, mine, other, m, v, c_idx, name, deps=()):
    r2, cols = mine.shape
    tr = next(t for t in (128, 64, 32, 16, 8) if r2 % t == 0)
    nh = r2 // tr

    def body(c_ref, w_ref, mine_ref, other_ref, m_ref, v_ref, g_ref, d_ref, nm_ref, nv_ref):
        g_ = jnp.where(pl.program_id(0) // nh == c_ref[0], mine_ref[...], other_ref[...])
        g_ref[...] = g_
        m_ = ADAM_B1 * m_ref[...] + (1.0 - ADAM_B1) * g_
        v_ = ADAM_B2 * v_ref[...] + (1.0 - ADAM_B2) * (g_ * g_)
        nm_ref[...] = m_
        nv_ref[...] = v_
        m_hat = m_ / (1.0 - ADAM_B1 ** ADAM_STEP)
        v_hat = v_ / (1.0 - ADAM_B2 ** ADAM_STEP)
        d_ref[...] = -ADAM_LR * (m_hat / (jnp.sqrt(v_hat) + ADAM_EPS) + ADAM_WD * w_ref[...])

    full = pl.BlockSpec((tr, cols), lambda i, c: (i, 0))
    mine_spec = pl.BlockSpec((tr, cols), lambda i, c: (jnp.clip(i - c[0] * nh, 0, nh - 1), 0))
    other_spec = pl.BlockSpec((tr, cols), lambda i, c: (jnp.clip(i - (1 - c[0]) * nh, 0, nh - 1), 0))
    return pl.pallas_call(
        lambda c_ref, *refs: body(c_ref, *refs[len(deps):]),
        grid_spec=pltpu.PrefetchScalarGridSpec(
            num_scalar_prefetch=1, grid=(2 * nh,),
            in_specs=[_ANY] * len(deps) + [full, mine_spec, other_spec, full, full], out_specs=[full] * 4),
        out_shape=(jax.ShapeDtypeStruct((2 * r2, cols), F32),) * 4, compiler_params=_cparams(), name=name,
    )(c_idx, *deps, w, mine, other, m, v)


def _add_halves(g, recv, c_idx, name):
    n, _, r2, cols = g.shape

    def body(c_ref, g_ref, r_ref, o_ref):
        o_ref[...] = (g_ref[...].astype(F32) + r_ref[...].astype(F32)).astype(BF16)

    return pl.pallas_call(
        body,
        grid_spec=pltpu.PrefetchScalarGridSpec(
            num_scalar_prefetch=1, grid=(n,),
            in_specs=[pl.BlockSpec((None, None, r2, cols), lambda k, c: (k, c[0], 0, 0)),
                      pl.BlockSpec((None, r2, cols), lambda k, c: (k, 0, 0))],
            out_specs=pl.BlockSpec((None, r2, cols), lambda k, c: (k, 0, 0))),
        out_shape=jax.ShapeDtypeStruct((n, r2, cols), BF16), compiler_params=_cparams(), name=name)(c_idx, g, recv)


def _sum_chips(parts, name):
    n, r2, cols = parts.shape
    tr = r2 // 2 if (r2 // 2) % 16 == 0 else r2

    def body(p_ref, o_ref):
        o_ref[...] = ((p_ref[0].astype(F32) + p_ref[1].astype(F32)) + p_ref[2].astype(F32)) + p_ref[3].astype(F32)

    return pl.pallas_call(
        body, grid=(r2 // tr,), out_shape=jax.ShapeDtypeStruct((r2, cols), F32),
        in_specs=[pl.BlockSpec((n, tr, cols), lambda i: (0, i, 0))], out_specs=pl.BlockSpec((tr, cols), lambda i: (i, 0)),
        compiler_params=_cparams(), name=name)(parts)


def _place():
    x, y, c = lax.axis_index("x"), lax.axis_index("y"), lax.axis_index("c")
    return x, y, c, 2 * x + y


def _flip(v, bit):
    return 1 - v if bit else v


def _allgather8(v, name, deps=()):
    r, n = v.shape

    def body(*refs):
        v_ref, out_ref, send_sems, recv_sems, local_sem = refs[len(deps):]
        x, y, c, _ = _place()
        me = 4 * x + 2 * y + c
        mine = pltpu.make_async_copy(v_ref, out_ref.at[me], local_sem)
        mine.start()
        sends = []
        for rel in range(1, N_DEV):
            peer = (_flip(x, rel & 4), _flip(y, rel & 2), _flip(c, rel & 1))
            cp = pltpu.make_async_remote_copy(v_ref, out_ref.at[me], send_sems.at[rel - 1], recv_sems.at[rel - 1],
                                              device_id=peer, device_id_type=MESH)
            cp.start()
            sends.append(cp)
        for rel in range(1, N_DEV):
            peer = (_flip(x, rel & 4), _flip(y, rel & 2), _flip(c, rel & 1))
            peer_idx = 4 * peer[0] + 2 * peer[1] + peer[2]
            pltpu.make_async_remote_copy(v_ref, out_ref.at[peer_idx], send_sems.at[rel - 1], recv_sems.at[rel - 1],
                                         device_id=peer, device_id_type=MESH).wait_recv()
        for cp in sends:
            cp.wait_send()
        mine.wait()

    return pl.pallas_call(
        body, out_shape=jax.ShapeDtypeStruct((N_DEV, r, n), F32), in_specs=[_ANY] * len(deps) + [_VMEM], out_specs=_VMEM,
        scratch_shapes=[pltpu.SemaphoreType.DMA((N_DEV - 1,)), pltpu.SemaphoreType.DMA((N_DEV - 1,)), pltpu.SemaphoreType.DMA(())],
        name=name)(*deps, v)


def _gather_weights(shards):
    nw = len(shards)

    def body(*refs):
        w_refs, out_refs = refs[:nw], refs[nw:2 * nw]
        send_sems, recv_sems = refs[2 * nw:]
        x, y, c, p = _place()
        sibling = (x, y, 1 - c)
        sends = []
        for j in range(1, N_CHIPS):
            peer = (_flip(x, j & 2), _flip(y, j & 1), c)
            for w in range(nw):
                cp = pltpu.make_async_remote_copy(w_refs[w].at[c], out_refs[w].at[p, c], send_sems.at[w * 6 + j - 1],
                                                  recv_sems.at[w * 6 + j - 1], device_id=peer, device_id_type=MESH)
                cp.start()
                sends.append(cp)
        for j in range(1, N_CHIPS):
            peer = (_flip(x, j & 2), _flip(y, j & 1), c)
            q = 2 * peer[0] + peer[1]
            for w in range(nw):
                pltpu.make_async_remote_copy(w_refs[w].at[c], out_refs[w].at[q, c], send_sems.at[w * 6 + j - 1],
                                             recv_sems.at[w * 6 + j - 1], device_id=peer, device_id_type=MESH).wait_recv()
                cp = pltpu.make_async_remote_copy(out_refs[w].at[q, c], out_refs[w].at[q, c], send_sems.at[w * 6 + 2 + j],
                                                  recv_sems.at[w * 6 + 2 + j], device_id=sibling, device_id_type=MESH)
                cp.start()
                sends.append(cp)
        for j in range(1, N_CHIPS):
            q = 2 * _flip(x, j & 2) + _flip(y, j & 1)
            for w in range(nw):
                pltpu.make_async_remote_copy(out_refs[w].at[q, 1 - c], out_refs[w].at[q, 1 - c], send_sems.at[w * 6 + 2 + j],
                                             recv_sems.at[w * 6 + 2 + j], device_id=sibling, device_id_type=MESH).wait_recv()
        for cp in sends:
            cp.wait_send()

    return pl.pallas_call(
        body, out_shape=tuple(jax.ShapeDtypeStruct((N_CHIPS,) + s.shape, s.dtype) for s in shards),
        in_specs=[_ANY] * nw, out_specs=[_ANY] * nw,
        scratch_shapes=[pltpu.SemaphoreType.DMA((6 * nw,)), pltpu.SemaphoreType.DMA((6 * nw,))],
        name="gather_weights")(*shards)


def _swap_halves(grads):
    nw = len(grads)

    def body(*refs):
        g_refs, out_refs = refs[:nw], refs[nw:2 * nw]
        send_sems, recv_sems = refs[2 * nw:]
        x, y, c, _ = _place()
        sibling = (x, y, 1 - c)
        sends = []
        for w in range(nw):
            for k in range(N_CHIPS):
                cp = pltpu.make_async_remote_copy(g_refs[w].at[k, 1 - c], out_refs[w].at[k], send_sems.at[w * N_CHIPS + k],
                                                  recv_sems.at[w * N_CHIPS + k], device_id=sibling, device_id_type=MESH)
                cp.start()
                sends.append(cp)
        for cp in sends:
            cp.wait_recv()
        for cp in sends:
            cp.wait_send()

    return pl.pallas_call(
        body, out_shape=tuple(jax.ShapeDtypeStruct((N_CHIPS,) + g.shape[2:], g.dtype) for g in grads),
        in_specs=[_ANY] * nw, out_specs=[_ANY] * nw,
        scratch_shapes=[pltpu.SemaphoreType.DMA((N_CHIPS * nw,)), pltpu.SemaphoreType.DMA((N_CHIPS * nw,))],
        name="swap_halves")(*grads)


def _scatter_chips(parts):
    nw = len(parts)

    def body(*refs):
        p_refs, out_refs = refs[:nw], refs[nw:2 * nw]
        send_sems, recv_sems = refs[2 * nw:]
        x, y, c, p = _place()
        sends = []
        for j in range(1, N_CHIPS):
            peer = (_flip(x, j & 2), _flip(y, j & 1), c)
            q = 2 * peer[0] + peer[1]
            for w in range(nw):
                cp = pltpu.make_async_remote_copy(p_refs[w].at[q], out_refs[w].at[p], send_sems.at[w * 3 + j - 1],
                                                  recv_sems.at[w * 3 + j - 1], device_id=peer, device_id_type=MESH)
                cp.start()
                sends.append(cp)
        for j in range(1, N_CHIPS):
            peer = (_flip(x, j & 2), _flip(y, j & 1), c)
            q = 2 * peer[0] + peer[1]
            for w in range(nw):
                pltpu.make_async_remote_copy(p_refs[w].at[q], out_refs[w].at[q], send_sems.at[w * 3 + j - 1],
                                             recv_sems.at[w * 3 + j - 1], device_id=peer, device_id_type=MESH).wait_recv()
        for cp in sends:
            cp.wait_send()

    return pl.pallas_call(
        body, out_shape=tuple(jax.ShapeDtypeStruct(s.shape, s.dtype) for s in parts),
        in_specs=[_ANY] * nw, out_specs=[_ANY] * nw,
        scratch_shapes=[pltpu.SemaphoreType.DMA((3 * nw,)), pltpu.SemaphoreType.DMA((3 * nw,))],
        name="scatter_chips")(*parts)


def _share_halves(halves):
    nw = len(halves)

    def body(*refs):
        h_refs, out_refs = refs[:nw], refs[nw:2 * nw]
        send_sems, recv_sems = refs[2 * nw:]
        x, y, c, _ = _place()
        sends = []
        for w in range(nw):
            cp = pltpu.make_async_remote_copy(h_refs[w], out_refs[w], send_sems.at[w], recv_sems.at[w],
                                              device_id=(x, y, 1 - c), device_id_type=MESH)
            cp.start()
            sends.append(cp)
        for cp in sends:
            cp.wait_recv()
        for cp in sends:
            cp.wait_send()

    return pl.pallas_call(
        body, out_shape=tuple(jax.ShapeDtypeStruct(s.shape, s.dtype) for s in halves),
        in_specs=[_ANY] * nw, out_specs=[_ANY] * nw,
        scratch_shapes=[pltpu.SemaphoreType.DMA((nw,)), pltpu.SemaphoreType.DMA((nw,))],
        name="share_halves")(*halves)


_HBM = pl.BlockSpec(memory_space=pltpu.HBM)
_SEM = pl.BlockSpec(memory_space=pltpu.SEMAPHORE)
_EFFECT = pltpu.SideEffectType.DATAFLOW_SIDE_EFFECTING


def _xchg_start(name, plan, n_copies, srcs, lands, after=()):
    bufs = list(srcs) + list(lands)
    ns, nb = len(srcs), len(srcs) + len(lands)

    def body(*refs):
        send_sems, recv_sems, token = refs[nb + len(after)], refs[nb + len(after) + 1], refs[-1]
        for i, (src, dst, peer, _) in enumerate(plan(_place(), refs[:ns], refs[ns:nb])):
            pltpu.make_async_remote_copy(src, dst, send_sems.at[i], recv_sems.at[i], device_id=peer, device_id_type=MESH).start()
        token[...] = jnp.zeros_like(token)

    out = pl.pallas_call(
        body, name=name,
        out_shape=(pltpu.SemaphoreType.DMA((n_copies,)), pltpu.SemaphoreType.DMA((n_copies,)),
                   *[pltpu.HBM(a.shape, a.dtype) for a in bufs], jax.ShapeDtypeStruct((8, 128), F32)),
        in_specs=[_HBM] * nb + [_ANY] * len(after), out_specs=(_SEM, _SEM, *[_HBM] * nb, _VMEM),
        input_output_aliases={i: 2 + i for i in range(nb)},
        compiler_params=pltpu.CompilerParams(has_side_effects=_EFFECT),
    )(*[pltpu.with_memory_space_constraint(a, pltpu.HBM) for a in bufs], *after)
    return (out[0], out[1]), out[2:2 + ns], out[2 + ns:2 + nb], out[-1]


def _xchg_wait(name, plan, sems, srcs, lands, after):
    bufs = list(srcs) + list(lands)
    ns, nb = len(srcs), len(srcs) + len(lands)

    def body(*refs):
        send_sems, recv_sems = refs[nb], refs[nb + 1]
        for i, (src, _, peer, mine) in enumerate(plan(_place(), refs[:ns], refs[ns:nb])):
            cp = pltpu.make_async_remote_copy(src, mine, send_sems.at[i], recv_sems.at[i], device_id=peer, device_id_type=MESH)
            cp.wait_send()
            cp.wait_recv()

    out = pl.pallas_call(
        body, name=name, out_shape=tuple(pltpu.HBM(a.shape, a.dtype) for a in bufs),
        in_specs=[_HBM] * nb + [_SEM, _SEM] + [_ANY] * len(after), out_specs=tuple([_HBM] * nb),
        input_output_aliases={i: i for i in range(nb)},
        compiler_params=pltpu.CompilerParams(has_side_effects=_EFFECT),
    )(*bufs, *sems, *after)
    return out[:ns], out[ns:]


def _other_chips(place):
    x, y, c, _ = place
    return [((_flip(x, j & 2), _flip(y, j & 1), c), 2 * _flip(x, j & 2) + _flip(y, j & 1)) for j in range(1, N_CHIPS)]


def _plan_gather_ici(place, src_refs, land_refs):
    _, _, c, p = place
    return [(s.at[c], l.at[p, c], peer, l.at[q, c]) for s, l in zip(src_refs, land_refs) for peer, q in _other_chips(place)]


def _plan_gather_d2d(place, src_refs, land_refs):
    x, y, c, _ = place
    return [(l.at[q, c], l.at[q, c], (x, y, 1 - c), l.at[q, 1 - c]) for l in land_refs for _, q in _other_chips(place)]


def _plan_swap(place, src_refs, land_refs):
    x, y, c, _ = place
    return [(s.at[k, 1 - c], l.at[k], (x, y, 1 - c), l.at[k]) for s, l in zip(src_refs, land_refs) for k in range(N_CHIPS)]


def _plan_scatter(place, src_refs, land_refs):
    _, _, _, p = place
    return [(s.at[q], l.at[p], peer, l.at[q]) for s, l in zip(src_refs, land_refs) for peer, q in _other_chips(place)]


def _plan_share(place, src_refs, land_refs):
    x, y, c, _ = place
    return [(s, l, (x, y, 1 - c), l) for s, l in zip(src_refs, land_refs)]


def _pack_rows(parts, n_rows, name, deps=()):
    def body(*refs):
        refs = refs[len(deps):]
        out_ref = refs[-1]
        out_ref[...] = jnp.zeros((n_rows, D), F32)
        at = 0
        for ref in refs[:-1]:
            k = ref.shape[0]
            out_ref[at:at + k, :] = ref[...]
            at += k

    return pl.pallas_call(
        body, out_shape=jax.ShapeDtypeStruct((n_rows, D), F32), in_specs=[_ANY] * len(deps) + [_VMEM] * len(parts),
        out_specs=_VMEM, name=name)(*deps, *parts)


TS_MM = 512
TS_MIX = 256


def _halved(a):
    n, r, cols = a.shape
    return a.reshape(n, 2, r // 2, cols)


def _gather_begin(name, shards, chip, after=()):
    lands = [lax.dynamic_update_index_in_dim(lax.empty((N_CHIPS,) + s.shape, s.dtype), s, chip, 0) for s in shards]
    sems, srcs, lands, token = _xchg_start(name + "_ici", _plan_gather_ici, 3 * len(shards), shards, lands, after)
    return name, sems, srcs, lands, token


def _gather_relay(handle, after):
    name, sems, srcs, lands, _ = handle
    _, lands = _xchg_wait(name + "_ici_wait", _plan_gather_ici, sems, srcs, lands, after)
    sems, _, lands, token = _xchg_start(name + "_d2d", _plan_gather_d2d, 3 * len(lands), [], lands)
    return name, sems, lands, token


def _gather_end(handle, after):
    name, sems, lands, _ = handle
    return _xchg_wait(name + "_d2d_wait", _plan_gather_d2d, sems, [], lands, after)[1]


def _rs_swap(name, grads):
    lands = [lax.empty((N_CHIPS,) + g.shape[2:], g.dtype) for g in grads]
    sems, grads, lands, token = _xchg_start(name + "_swap", _plan_swap, N_CHIPS * len(grads), grads, lands)
    return name, sems, grads, lands, token


def _rs_scatter(handle, after, chip, ci):
    name, sems, grads, lands, _ = handle
    grads, from_sibling = _xchg_wait(name + "_swap_wait", _plan_swap, sems, grads, lands, after)
    c_arr = jnp.reshape(ci, (1,)).astype(jnp.int32)
    pair_sums = [_add_halves(g, r, c_arr, "%s_add_halves_%d" % (name, k)) for k, (g, r) in enumerate(zip(grads, from_sibling))]
    lands = [lax.dynamic_update_index_in_dim(lax.empty(p.shape, p.dtype), lax.dynamic_index_in_dim(p, chip, 0, keepdims=False),
                                             chip, 0) for p in pair_sums]
    sems, pair_sums, lands, token = _xchg_start(name + "_scatter", _plan_scatter, 3 * len(pair_sums), pair_sums, lands)
    return name, sems, pair_sums, lands, token


def _rs_share(handle, after):
    name, sems, pair_sums, lands, _ = handle
    _, by_chip = _xchg_wait(name + "_scatter_wait", _plan_scatter, sems, pair_sums, lands, after)
    halves = [_sum_chips(b, "%s_sum_chips_%d" % (name, k)) for k, b in enumerate(by_chip)]
    lands = [lax.empty(h.shape, h.dtype) for h in halves]
    sems, halves, lands, token = _xchg_start(name + "_share", _plan_share, len(halves), halves, lands)
    return name, sems, halves, lands, token


def _rs_end(handle, after):
    name, sems, halves, lands, _ = handle
    halves, others = _xchg_wait(name + "_share_wait", _plan_share, sems, halves, lands, after)
    return list(zip(halves, others))


def kernel(x, c, w_ada, b_ada, g_norm_mix, w_in, conv_a_w, conv_b_w, conv_b_bias, w_rg_a, b_rg_a, w_rg_x, b_rg_x, lru_lambda, w_out, g_norm_ffn, w_gate_up, w_down, g_norm_final, loss_target, m_w_ada, m_b_ada, m_g_norm_mix, m_w_in, m_conv_a_w, m_conv_b_w, m_conv_b_bias, m_w_rg_a, m_b_rg_a, m_w_rg_x, m_b_rg_x, m_lru_lambda, m_w_out, m_g_norm_ffn, m_w_gate_up, m_w_down, m_g_norm_final, v_w_ada, v_b_ada, v_g_norm_mix, v_w_in, v_conv_a_w, v_conv_b_w, v_conv_b_bias, v_w_rg_a, v_b_rg_a, v_w_rg_x, v_b_rg_x, v_lru_lambda, v_w_out, v_g_norm_ffn, v_w_gate_up, v_w_down, v_g_norm_final):
    xi, yi, ci = lax.axis_index("x"), lax.axis_index("y"), lax.axis_index("c")
    chip = 2 * xi + yi
    me = 2 * chip + ci
    n_ada = w_ada.shape[2]

    def rg_shard(w):
        return w[0].astype(BF16).reshape(2, HEADS * HB // N_CHIPS // 2, HB)

    gather_a = _gather_begin("gather_a", [w_in[0].astype(BF16).reshape(2, D // 2, C_IN), rg_shard(w_rg_a), rg_shard(w_rg_x),
                                          w_out[0].astype(BF16).reshape(2, D // N_CHIPS // 2, D)], chip)

    def widen(w):
        return jnp.pad(w, ((0, 0), (0, D - w.shape[1])))

    got = _allgather8(_pack_rows([c, widen(conv_a_w[0]), widen(conv_b_w[0])], 8, "pack_c_conv", deps=[gather_a[-1]]),
                      "gather_c_conv")
    c_all = got[:, 0, :]
    conv_full = got[::2, 1:8, :D // N_CHIPS].transpose(1, 0, 2).reshape(7, D)

    mod_part, c_act = _ada_fwd(c_all, w_ada[0], lax.dynamic_slice_in_dim(b_ada, chip * n_ada, n_ada, axis=1))
    mod_all = _allgather8(mod_part, "gather_mod")
    mod_mine = lax.dynamic_index_in_dim(mod_all, me, axis=1, keepdims=False)[::2].reshape(6, D)
    vecs = _pack_rows([mod_mine, g_norm_mix, g_norm_ffn, g_norm_final.reshape(1, D), conv_b_bias, b_rg_a, b_rg_x, lru_lambda,
                       conv_full], N_VEC, "pack_vecs")

    gather_b = _gather_begin("gather_b", [w_gate_up[0].astype(BF16).reshape(2, D // 2, C_GU),
                                          w_down[0].astype(BF16).reshape(2, D_FF // N_CHIPS // 2, D)], chip, after=[vecs])
    gather_a = _gather_relay(gather_a, [gather_b[-1]])
    wg_in, wg_rga, wg_rgx, wg_out = _gather_end(gather_a, [])
    wg_in = wg_in.reshape(N_CHIPS, D, C_IN)
    wg_out = wg_out.reshape(D, D)

    def rg_full(wg):
        return wg.reshape(N_CHIPS, HEADS, HB // N_CHIPS, HB).transpose(1, 0, 2, 3).reshape(HEADS, HB, HB)

    wg_rga, wg_rgx = rg_full(wg_rga), rg_full(wg_rgx)

    def to_blocks(v):
        return v.reshape(-1, TS_MIX // TIME_BLOCKS, TIME_BLOCKS, D).transpose(0, 2, 1, 3).reshape(v.shape)

    def from_blocks(v):
        return v.reshape(-1, TIME_BLOCKS, TS_MIX // TIME_BLOCKS, D).transpose(0, 2, 1, 3).reshape(v.shape)

    xs, target = to_blocks(x[0]), to_blocks(loss_target[0])
    h1, proj = _fwd_in(xs, vecs, wg_in, TS_MM)
    gather_b = _gather_relay(gather_b, [proj])
    x1, merged, z1, u, h = _fwd_mix(proj, xs, vecs, wg_rga, wg_rgx, wg_out, TS_MIX, deps=[gather_b[-1]])
    wg_gu, wg_dn = _gather_end(gather_b, [x1])
    wg_gu, wg_dn = wg_gu.reshape(N_CHIPS, D, C_GU), wg_dn.reshape(D_FF, D)
    dx1, h2, act, dz2, dgu, sm_ffn = _ffn_loss(x1, target, vecs, wg_gu, wg_dn, TS_MIX)

    def rg_chunks(dw):
        return _halved(dw.reshape(HEADS, N_CHIPS, HB // N_CHIPS, HB).transpose(1, 0, 2, 3).reshape(N_CHIPS, HB, HB).astype(BF16))

    g_dn = _grad_w(act, dz2, 1, TS_MM, "grad_w_down")
    g_gu = _grad_w(h2, dgu, N_CHIPS, TS_MM, "grad_w_gate_up")
    rs_b = _rs_swap("rs_b", [_halved(g_gu), _halved(g_dn.reshape(N_CHIPS, D_FF // N_CHIPS, D))])
    dproj, dz1, sm_mix, dw_rga, dw_rgx = _bwd_mix(dx1, z1, proj, u, h, vecs, wg_rga, wg_rgx, wg_out, sm_ffn, TS_MIX,
                                                  deps=[rs_b[-1]])
    rs_b = _rs_scatter(rs_b, [dproj], chip, ci)
    g_in = _grad_w(h1, dproj, N_CHIPS, TS_MM, "grad_w_in", deps=[rs_b[-1]])
    rs_b = _rs_share(rs_b, [g_in])
    g_out = _grad_w(merged, dz1, 1, TS_MM, "grad_w_out", deps=[rs_b[-1]])
    rs_a = _rs_swap("rs_a", [_halved(g_in), rg_chunks(dw_rga), rg_chunks(dw_rgx), _halved(g_out.reshape(N_CHIPS, D // N_CHIPS, D))])

    c_arr = jnp.reshape(ci, (1,)).astype(jnp.int32)

    def step(name, w, g, m, v, deps=()):
        shape = w.shape
        two_d = (-1, shape[-1])
        d, nm, nv = _adamw(w.reshape(two_d), g.reshape(two_d), m.reshape(two_d), v.reshape(two_d), "adamw_" + name, deps)
        return g.reshape(shape), d.reshape(shape), nm.reshape(shape), nv.reshape(shape)

    def step_halves(name, w, halves, m, v, deps=()):
        shape = w.shape
        two_d = (-1, shape[-1])
        out = _adamw_halves(w.reshape(two_d), halves[0], halves[1], m.reshape(two_d), v.reshape(two_d), c_arr, "adamw_" + name, deps)
        return tuple(a.reshape(shape) for a in out)

    def shard_cols(row_block):
        return lax.dynamic_slice_in_dim(row_block, chip * (D // N_CHIPS), D // N_CHIPS, axis=1)

    gw_gu, gw_dn = _rs_end(rs_b, [g_out])
    res = {
        "w_gate_up": step_halves("w_gate_up", w_gate_up, gw_gu, m_w_gate_up, v_w_gate_up, [rs_a[-1]]),
        "w_down": step_halves("w_down", w_down, gw_dn, m_w_down, v_w_down, [rs_a[-1]]),
    }
    rs_a = _rs_scatter(rs_a, [res["w_gate_up"][1], res["w_down"][1]], chip, ci)
    grad_x, sm_in = _bwd_in(dproj, xs, dx1, vecs, wg_in, sm_mix, TS_MM, deps=[rs_a[-1]])
    rs_a = _rs_share(rs_a, [grad_x])

    small, per_dev = _sum_small(_allgather8(sm_in, "gather_small", deps=[rs_a[-1]]))
    dmod_all = per_dev[:, 0:6, :].reshape(N_DEV, 6 * D)
    grad_w_ada = _ada_bwd(c_act, lax.dynamic_slice_in_dim(dmod_all, chip * n_ada, n_ada, axis=1))
    grad_b_ada = small[0:6].reshape(1, 6 * D)
    behind = ()
    res.update({
        "w_ada": step("w_ada", w_ada, grad_w_ada[None], m_w_ada, v_w_ada, behind),
        "b_ada": step("b_ada", b_ada.reshape(6, D), grad_b_ada.reshape(6, D), m_b_ada.reshape(6, D), v_b_ada.reshape(6, D), behind),
        "g_norm_mix": step("g_norm_mix", g_norm_mix, small[G_GMIX:G_GMIX + 1], m_g_norm_mix, v_g_norm_mix, behind),
        "conv_a_w": step("conv_a_w", conv_a_w, shard_cols(small[G_WA0:G_WA0 + 3])[None], m_conv_a_w, v_conv_a_w, behind),
        "conv_b_w": step("conv_b_w", conv_b_w, shard_cols(small[G_WB0:G_WB0 + 4])[None], m_conv_b_w, v_conv_b_w, behind),
        "conv_b_bias": step("conv_b_bias", conv_b_bias, small[G_CBB:G_CBB + 1], m_conv_b_bias, v_conv_b_bias, behind),
        "b_rg_a": step("b_rg_a", b_rg_a, small[G_BA:G_BA + 1], m_b_rg_a, v_b_rg_a, behind),
        "b_rg_x": step("b_rg_x", b_rg_x, small[G_BX:G_BX + 1], m_b_rg_x, v_b_rg_x, behind),
        "lru_lambda": step("lru_lambda", lru_lambda, small[G_LAM:G_LAM + 1], m_lru_lambda, v_lru_lambda, behind),
        "g_norm_ffn": step("g_norm_ffn", g_norm_ffn, small[G_GFFN:G_GFFN + 1], m_g_norm_ffn, v_g_norm_ffn, behind),
        "g_norm_final": step("g_norm_final", g_norm_final.reshape(1, D), small[G_GFIN:G_GFIN + 1], m_g_norm_final.reshape(1, D),
                             v_g_norm_final.reshape(1, D), behind),
    })
    gw_in, gw_rga, gw_rgx, gw_out = _rs_end(rs_a, [res[n][1] for n in res])
    res["w_in"] = step_halves("w_in", w_in, gw_in, m_w_in, v_w_in)
    res["w_rg_a"] = step_halves("w_rg_a", w_rg_a, gw_rga, m_w_rg_a, v_w_rg_a)
    res["w_rg_x"] = step_halves("w_rg_x", w_rg_x, gw_rgx, m_w_rg_x, v_w_rg_x)
    res["w_out"] = step_halves("w_out", w_out, gw_out, m_w_out, v_w_out)
    res["b_ada"] = tuple(a.reshape(1, 6 * D) for a in res["b_ada"])
    res["g_norm_final"] = tuple(a.reshape(D) for a in res["g_norm_final"])
    names = ["w_ada", "b_ada", "g_norm_mix", "w_in", "conv_a_w", "conv_b_w", "conv_b_bias", "w_rg_a", "b_rg_a", "w_rg_x",
             "b_rg_x", "lru_lambda", "w_out", "g_norm_ffn", "w_gate_up", "w_down", "g_norm_final"]
    loss = jnp.sum(small[G_LOSS])
    return (loss, from_blocks(grad_x)[None], *[res[n][0] for n in names], *[res[n][1] for n in names],
            *[res[n][2] for n in names], *[res[n][3] for n in names])
```

```python
import functools

import jax
import jax.numpy as jnp
from jax import lax
from jax.experimental import pallas as pl
from jax.experimental.pallas import tpu as pltpu

F32 = jnp.float32
BF16 = jnp.bfloat16
MESH = pl.DeviceIdType.MESH

D = 1024
N_CHIPS = 4
N_DEV = 8
D_IN = 7 * D
C_IN = D_IN // N_CHIPS
D_FF = 2816
C_GU = 2 * D_FF // N_CHIPS
HEADS = 4
HB = D // HEADS
EPS = 1e-6
LRU_C = 8.0
ADAM_LR, ADAM_B1, ADAM_B2, ADAM_EPS, ADAM_WD, ADAM_STEP = 0.001, 0.9, 0.999, 1e-08, 0.01, 10
VMEM_LIMIT = 56 << 20

(V_SH1, V_SC1, V_GT1, V_SH2, V_SC2, V_GT2, V_GMIX, V_GFFN, V_GFIN, V_CBB, V_BA, V_BX, V_LAM,
 V_WA0, V_WA1, V_WA2, V_WB0, V_WB1, V_WB2, V_WB3) = range(20)
N_VEC = 24
(G_SH1, G_SC1, G_GT1, G_SH2, G_SC2, G_GT2, G_GMIX, G_CBB, G_BA, G_BX, G_LAM, G_GFFN, G_GFIN,
 G_WA0, G_WA1, G_WA2, G_WB0, G_WB1, G_WB2, G_WB3, G_LOSS) = range(21)
N_SMALL = 24

_VMEM = pl.BlockSpec(memory_space=pltpu.VMEM)
_ANY = pl.BlockSpec(memory_space=pl.ANY)


def _cparams(n_grid=1):
    return pltpu.CompilerParams(dimension_semantics=("arbitrary",) * n_grid, vmem_limit_bytes=VMEM_LIMIT)


def _after(deps, body):
    n = len(deps)
    return lambda *refs: body(*refs[n:])


def _rms(x):
    rstd = lax.rsqrt(jnp.mean(x * x, axis=-1, keepdims=True) + EPS)
    return x * rstd, rstd


def _rms_bwd(dxhat, xhat, rstd):
    return rstd * (dxhat - xhat * jnp.mean(dxhat * xhat, axis=-1, keepdims=True))


def _rowsum(v):
    return jnp.sum(v, axis=0, keepdims=True)


def _dot(a, b):
    return jnp.dot(a, b, preferred_element_type=F32)


def _dot_nt(a, b):
    return lax.dot_general(a, b, (((1,), (1,)), ((), ())), preferred_element_type=F32)


def _dot_tn(a, b):
    return lax.dot_general(a, b, (((0,), (0,)), ((), ())), preferred_element_type=F32)


def _gelu(x):
    k, c = 0.7978845608028654, 0.044715
    t = jnp.tanh(k * (x + c * x * x * x))
    return 0.5 * x * (1.0 + t), 0.5 * (1.0 + t) + 0.5 * x * (1.0 - t * t) * k * (1.0 + 3.0 * c * x * x)


def _log_sigmoid(lam):
    return jnp.minimum(lam, 0.0) - jnp.log1p(jnp.exp(-jnp.abs(lam)))


def _lru_gates(u, wa_ref, wx_ref, v_ref, row0):
    ub = u.astype(BF16)
    pre_a = jnp.concatenate([_dot(ub[:, h * HB:(h + 1) * HB], wa_ref[h]) for h in range(HEADS)], axis=1)
    pre_x = jnp.concatenate([_dot(ub[:, h * HB:(h + 1) * HB], wx_ref[h]) for h in range(HEADS)], axis=1)
    r = jax.nn.sigmoid(pre_a + v_ref[V_BA:V_BA + 1, :])
    ig = jax.nn.sigmoid(pre_x + v_ref[V_BX:V_BX + 1, :])
    log_a = LRU_C * r * _log_sigmoid(v_ref[V_LAM:V_LAM + 1, :])
    a = jnp.exp(log_a)
    x2 = 2.0 * log_a
    m2 = jnp.where(x2 > -0.03, -x2 * (1.0 + x2 * (0.5 + x2 * (1.0 / 6.0 + x2 * (1.0 / 24.0)))), 1.0 - a * a)
    mult = jnp.where(row0, 1.0, jnp.sqrt(jnp.maximum(m2, 0.0)))
    return r, ig, a, mult


TIME_BLOCKS = 8


def _late_blocks(v, buf, g, halo=None):
    n = buf.shape[0]
    out = []
    for idx in range(n):
        k = TIME_BLOCKS - n + idx
        buf[idx, 8:g + 8, :] = v[k * g:(k + 1) * g]
        if halo is not None:
            buf[idx, 7:8, :] = halo[idx]
        out.append(buf[idx, pl.ds(7, g), :])
        if halo is None:
            buf[idx, 7:8, :] = buf[idx, g + 7:g + 8, :]
    return out


def _earlier(v, s, late, g):
    return jnp.concatenate(late[len(late) - s:] + [v[0:(TIME_BLOCKS - s) * g]], axis=0)


def _early_blocks(v, buf, g):
    out = []
    for k in range(buf.shape[0]):
        buf[k, 0:g, :] = v[k * g:(k + 1) * g]
        out.append(buf[k, pl.ds(1, g), :])
        buf[k, g:g + 1, :] = buf[k, 0:1, :]
    return out


def _later(v, s, early, g):
    return jnp.concatenate([v[s * g:]] + early[0:s], axis=0)


def _fwd_in(x, vecs, w_in_g, ts, deps=()):
    s = x.shape[0]

    def body(x_ref, v_ref, w_ref, h1_ref, proj_ref):
        xhat, _ = _rms(x_ref[...])
        h = xhat * v_ref[V_GMIX:V_GMIX + 1, :] * (1.0 + v_ref[V_SC1:V_SC1 + 1, :]) + v_ref[V_SH1:V_SH1 + 1, :]
        hb = h.astype(BF16)
        h1_ref[...] = hb
        for k in range(N_CHIPS):
            proj_ref[:, k * C_IN:(k + 1) * C_IN] = _dot(hb, w_ref[k]).astype(BF16)

    return pl.pallas_call(
        _after(deps, body), grid=(s // ts,),
        out_shape=(jax.ShapeDtypeStruct((s, D), BF16), jax.ShapeDtypeStruct((s, D_IN), BF16)),
        in_specs=[_ANY] * len(deps) + [pl.BlockSpec((ts, D), lambda i: (i, 0)), _VMEM, _VMEM],
        out_specs=[pl.BlockSpec((ts, D), lambda i: (i, 0)), pl.BlockSpec((ts, D_IN), lambda i: (i, 0))],
        compiler_params=_cparams(), name="fwd_in")(*deps, x, vecs, w_in_g)


def _fwd_mix(proj, x, vecs, w_rga, w_rgx, w_out, ts, deps=()):
    s = x.shape[0]
    g = ts // TIME_BLOCKS

    def body(proj_ref, x_ref, v_ref, wa_ref, wx_ref, wo_ref, x1_ref, mg_ref, z1_ref, u_ref, h_ref,
             ua_buf, rx_buf, p_buf, q_buf, c_buf, hcarry):
        i = pl.program_id(0)

        @pl.when(i == 0)
        def _():
            ua_buf[...] = jnp.zeros(ua_buf.shape, F32)
            rx_buf[...] = jnp.zeros(rx_buf.shape, F32)
            hcarry[...] = jnp.zeros((8, D), F32)

        def seg(j):
            return proj_ref[:, j * D:(j + 1) * D].astype(F32)

        def vrow(j):
            return v_ref[j:j + 1, :]

        cb, cc, cx, rx, rg, ga, gb = (seg(j) for j in range(7))
        ua = cc * cx
        ua_late = _late_blocks(ua, ua_buf, g)
        rx_late = _late_blocks(rx, rx_buf, g)
        va = vrow(V_WA2) * ua + vrow(V_WA1) * _earlier(ua, 1, ua_late, g) + vrow(V_WA0) * _earlier(ua, 2, ua_late, g)
        u = (vrow(V_WB3) * rx + vrow(V_WB2) * _earlier(rx, 1, rx_late, g) + vrow(V_WB1) * _earlier(rx, 2, rx_late, g)
             + vrow(V_WB0) * _earlier(rx, 3, rx_late, g) + vrow(V_CBB))
        u_ref[...] = u

        rows = lax.broadcasted_iota(jnp.int32, (ts, D), 0)
        row0 = jnp.logical_and(rows == 0, i == 0)
        _, ig, a, mult = _lru_gates(u, wa_ref, wx_ref, v_ref, row0)
        bx = mult * (ig * u)

        prods, sums = [a[0:g]], [bx[0:g]]
        for k in range(1, TIME_BLOCKS):
            ak = a[k * g:(k + 1) * g]
            sums.append(ak * sums[-1] + bx[k * g:(k + 1) * g])
            prods.append(ak * prods[-1])
        p_buf[...] = prods[-1]
        q_buf[...] = sums[-1]
        state = hcarry[0:1, :]
        for j in range(g):
            c_buf[j:j + 1, :] = state
            state = p_buf[j:j + 1, :] * state + q_buf[j:j + 1, :]
        hcarry[0:1, :] = state
        entering = c_buf[...]
        h = jnp.concatenate([sums[k] + prods[k] * entering for k in range(TIME_BLOCKS)], axis=0)
        h_ref[...] = h

        gel, _ = _gelu(rg)
        merged = (jax.nn.sigmoid(ga) * (cb * va) + jax.nn.sigmoid(gb) * (h * gel)).astype(BF16)
        mg_ref[...] = merged
        z1 = _dot(merged, wo_ref[...])
        z1_ref[...] = z1.astype(BF16)
        x1_ref[...] = x_ref[...] + vrow(V_GT1) * z1

    row = lambda i: (i, 0)
    return pl.pallas_call(
        _after(deps, body), grid=(s // ts,),
        out_shape=(jax.ShapeDtypeStruct((s, D), F32), jax.ShapeDtypeStruct((s, D), BF16), jax.ShapeDtypeStruct((s, D), BF16),
                   jax.ShapeDtypeStruct((s, D), F32), jax.ShapeDtypeStruct((s, D), F32)),
        in_specs=[_ANY] * len(deps) + [pl.BlockSpec((ts, D_IN), row), pl.BlockSpec((ts, D), row), _VMEM, _VMEM, _VMEM, _VMEM],
        out_specs=[pl.BlockSpec((ts, D), row)] * 5,
        scratch_shapes=[pltpu.VMEM((2, g + 8, D), F32), pltpu.VMEM((3, g + 8, D), F32), pltpu.VMEM((g, D), F32),
                        pltpu.VMEM((g, D), F32), pltpu.VMEM((g, D), F32), pltpu.VMEM((8, D), F32)],
        compiler_params=_cparams(), name="fwd_mix")(*deps, proj, x, vecs, w_rga, w_rgx, w_out)


def _ffn_loss(x1, target, vecs, w_gu_g, w_dn, ts):
    s = x1.shape[0]

    def body(x1_ref, t_ref, v_ref, wgu_ref, wdn_ref, dx1_ref, h2_ref, act_ref, dz2_ref, dgu_ref, sm_ref):
        @pl.when(pl.program_id(0) == 0)
        def _():
            sm_ref[...] = jnp.zeros((N_SMALL, D), F32)

        def vrow(j):
            return v_ref[j:j + 1, :]

        def acc(j, val):
            sm_ref[j:j + 1, :] += _rowsum(val)

        x1 = x1_ref[...]
        xh1, rstd1 = _rms(x1)
        n2 = xh1 * vrow(V_GFFN)
        h2 = (n2 * (1.0 + vrow(V_SC2)) + vrow(V_SH2)).astype(BF16)
        h2_ref[...] = h2
        g = jnp.concatenate([_dot(h2, wgu_ref[0]), _dot(h2, wgu_ref[1])], axis=1)
        up = jnp.concatenate([_dot(h2, wgu_ref[2]), _dot(h2, wgu_ref[3])], axis=1)
        sg = jax.nn.sigmoid(g)
        silu = g * sg
        act = (silu * up).astype(BF16)
        act_ref[...] = act
        z2 = _dot(act, wdn_ref[...])
        x2 = x1 + vrow(V_GT2) * z2
        xh2, rstd2 = _rms(x2)
        err = xh2 * vrow(V_GFIN) - t_ref[...]
        acc(G_LOSS, (0.5 / D) * err * err)
        dy = err * (1.0 / D)
        acc(G_GFIN, dy * xh2)
        dx2 = _rms_bwd(dy * vrow(V_GFIN), xh2, rstd2)
        acc(G_GT2, dx2 * z2)
        dz2 = (vrow(V_GT2) * dx2).astype(BF16)
        dz2_ref[...] = dz2
        dact = _dot_nt(dz2, wdn_ref[...])
        dgate = (dact * up * (sg * (1.0 + g * (1.0 - sg)))).astype(BF16)
        dup = (dact * silu).astype(BF16)
        dgu_ref[:, 0:D_FF] = dgate
        dgu_ref[:, D_FF:2 * D_FF] = dup
        dh2 = (_dot_nt(dgate[:, 0:C_GU], wgu_ref[0]) + _dot_nt(dgate[:, C_GU:2 * C_GU], wgu_ref[1])
               + _dot_nt(dup[:, 0:C_GU], wgu_ref[2]) + _dot_nt(dup[:, C_GU:2 * C_GU], wgu_ref[3]))
        acc(G_SH2, dh2)
        acc(G_SC2, dh2 * n2)
        dn2 = dh2 * (1.0 + vrow(V_SC2))
        acc(G_GFFN, dn2 * xh1)
        dx1_ref[...] = dx2 + _rms_bwd(dn2 * vrow(V_GFFN), xh1, rstd1)

    row = lambda i: (i, 0)
    return pl.pallas_call(
        body, grid=(s // ts,),
        out_shape=(jax.ShapeDtypeStruct((s, D), F32), jax.ShapeDtypeStruct((s, D), BF16), jax.ShapeDtypeStruct((s, D_FF), BF16),
                   jax.ShapeDtypeStruct((s, D), BF16), jax.ShapeDtypeStruct((s, 2 * D_FF), BF16),
                   jax.ShapeDtypeStruct((N_SMALL, D), F32)),
        in_specs=[pl.BlockSpec((ts, D), row), pl.BlockSpec((ts, D), row), _VMEM, _VMEM, _VMEM],
        out_specs=[pl.BlockSpec((ts, D), row), pl.BlockSpec((ts, D), row), pl.BlockSpec((ts, D_FF), row),
                   pl.BlockSpec((ts, D), row), pl.BlockSpec((ts, 2 * D_FF), row), pl.BlockSpec((N_SMALL, D), lambda i: (0, 0))],
        compiler_params=_cparams(), name="ffn_loss")(x1, target, vecs, w_gu_g, w_dn)


def _bwd_mix(dx1, z1, merged, proj, u, h, vecs, w_rga, w_rgx, w_out, small, ts, deps=()):
    s = dx1.shape[0]
    nt = s // ts
    g = ts // TIME_BLOCKS
    assert g % 16 == 0

    def body(dx1_ref, z1_ref, mg_ref, proj_ref, u_ref, h_ref, hh_ref, cc6_ref, cx6_ref, cc7_ref, cx7_ref, v_ref, wa_ref, wx_ref,
             wo_ref, sm0_ref, dproj_ref, sm_ref, dwa_ref, dwx_ref, dwo_ref,
             ua_buf, h_buf, a_buf, dva_buf, du_buf, p_buf, q_buf, c_buf, lcarry):
        i = pl.program_id(0)
        first_tile = i == nt - 1

        @pl.when(i == 0)
        def _():
            a_buf[...] = jnp.zeros(a_buf.shape, F32)
            dva_buf[...] = jnp.zeros(dva_buf.shape, F32)
            du_buf[...] = jnp.zeros(du_buf.shape, F32)
            lcarry[...] = jnp.zeros((8, D), F32)
            sm_ref[...] = sm0_ref[...]
            dwa_ref[...] = jnp.zeros((HEADS, HB, HB), F32)
            dwx_ref[...] = jnp.zeros((HEADS, HB, HB), F32)
            dwo_ref[...] = jnp.zeros((D, D), F32)

        def seg(j):
            return proj_ref[:, j * D:(j + 1) * D].astype(F32)

        def vrow(j):
            return v_ref[j:j + 1, :]

        def acc(j, val):
            sm_ref[j:j + 1, :] += _rowsum(val)

        cb, cc, cx, rx, rg, ga, gb = (seg(j) for j in range(7))
        ua = cc * cx

        def last_row(v):
            pick = lax.broadcasted_iota(jnp.int32, v.shape, 0) == v.shape[0] - 1
            return jnp.where(first_tile, 0.0, jnp.sum(jnp.where(pick, v, 0.0), axis=0, keepdims=True))

        ua_halo = [last_row(cc6_ref[...].astype(F32) * cx6_ref[...].astype(F32)),
                   last_row(cc7_ref[...].astype(F32) * cx7_ref[...].astype(F32))]
        ua_late = _late_blocks(ua, ua_buf, g, ua_halo)
        va = vrow(V_WA2) * ua + vrow(V_WA1) * _earlier(ua, 1, ua_late, g) + vrow(V_WA0) * _earlier(ua, 2, ua_late, g)
        u = u_ref[...]
        h = h_ref[...]
        rows = lax.broadcasted_iota(jnp.int32, (ts, D), 0)
        row0 = jnp.logical_and(rows == 0, first_tile)
        r, ig, a, mult = _lru_gates(u, wa_ref, wx_ref, v_ref, row0)
        sga = jax.nn.sigmoid(ga)
        sgb = jax.nn.sigmoid(gb)
        gel, dgel = _gelu(rg)

        dx1 = dx1_ref[...]
        acc(G_GT1, dx1 * z1_ref[...].astype(F32))
        dz1 = (vrow(V_GT1) * dx1).astype(BF16)
        dwo_ref[...] += _dot_tn(mg_ref[...], dz1)
        dmg = _dot_nt(dz1, wo_ref[...])
        dproj_ref[:, 5 * D:6 * D] = (dmg * (cb * va) * sga * (1.0 - sga)).astype(BF16)
        dproj_ref[:, 6 * D:7 * D] = (dmg * (h * gel) * sgb * (1.0 - sgb)).astype(BF16)
        dya = dmg * sga
        dyb = dmg * sgb

        dproj_ref[:, 0:D] = (dya * va).astype(BF16)
        dva = dya * cb
        dva_early = _early_blocks(dva, dva_buf, g)
        dva1 = _later(dva, 1, dva_early, g)
        dva2 = _later(dva, 2, dva_early, g)
        dua = vrow(V_WA2) * dva + vrow(V_WA1) * dva1 + vrow(V_WA0) * dva2
        acc(G_WA2, ua * dva)
        acc(G_WA1, ua * dva1)
        acc(G_WA0, ua * dva2)
        dproj_ref[:, D:2 * D] = (dua * cx).astype(BF16)
        dproj_ref[:, 2 * D:3 * D] = (dua * cc).astype(BF16)

        dproj_ref[:, 4 * D:5 * D] = (dyb * h * dgel).astype(BF16)
        a_next = _later(a, 1, _early_blocks(a, a_buf, g), g)
        dh = dyb * gel
        last = TIME_BLOCKS - 1
        prods, sums = {last: a_next[last * g:]}, {last: dh[last * g:]}
        for k in range(last - 1, -1, -1):
            ak = a_next[k * g:(k + 1) * g]
            sums[k] = dh[k * g:(k + 1) * g] + ak * sums[k + 1]
            prods[k] = ak * prods[k + 1]
        p_buf[...] = prods[0]
        q_buf[...] = sums[0]
        state = lcarry[0:1, :]
        for j in range(g - 1, -1, -1):
            c_buf[j:j + 1, :] = state
            state = q_buf[j:j + 1, :] + p_buf[j:j + 1, :] * state
        lcarry[0:1, :] = state
        entering = c_buf[...]
        lam = jnp.concatenate([sums[k] + prods[k] * entering for k in range(TIME_BLOCKS)], axis=0)

        h_halo = [jnp.where(first_tile, 0.0, hh_ref[7:8, :])]
        da = lam * _earlier(h, 1, _late_blocks(h, h_buf, g, h_halo), g)
        dmult = jnp.where(row0, 0.0, lam * (ig * u))
        di = lam * mult * u
        du = lam * mult * ig
        dlog_a = da * a - dmult * (a * a) / mult
        lam_p = vrow(V_LAM)
        dr = dlog_a * (LRU_C * _log_sigmoid(lam_p))
        sm_ref[G_LAM:G_LAM + 1, :] += _rowsum(dlog_a * r) * (LRU_C * jax.nn.sigmoid(-lam_p))
        dpa = dr * r * (1.0 - r)
        dpx = di * ig * (1.0 - ig)
        acc(G_BA, dpa)
        acc(G_BX, dpx)
        dpab = dpa.astype(BF16)
        dpxb = dpx.astype(BF16)
        ub = u.astype(BF16)
        back = []
        for hd in range(HEADS):
            cols = slice(hd * HB, (hd + 1) * HB)
            back.append(_dot_nt(dpab[:, cols], wa_ref[hd]) + _dot_nt(dpxb[:, cols], wx_ref[hd]))
            dwa_ref[hd] += _dot_tn(ub[:, cols], dpab[:, cols])
            dwx_ref[hd] += _dot_tn(ub[:, cols], dpxb[:, cols])
        du = du + jnp.concatenate(back, axis=1)

        acc(G_CBB, du)
        du_early = _early_blocks(du, du_buf, g)
        du1 = _later(du, 1, du_early, g)
        du2 = _later(du, 2, du_early, g)
        du3 = _later(du, 3, du_early, g)
        dproj_ref[:, 3 * D:4 * D] = (vrow(V_WB3) * du + vrow(V_WB2) * du1 + vrow(V_WB1) * du2 + vrow(V_WB0) * du3).astype(BF16)
        acc(G_WB3, rx * du)
        acc(G_WB2, rx * du1)
        acc(G_WB1, rx * du2)
        acc(G_WB0, rx * du3)

    rev = lambda i: (nt - 1 - i, 0)
    halo8 = lambda i: (jnp.maximum((nt - 1 - i) * (ts // 8) - 1, 0), 0)
    const2 = lambda i: (0, 0)
    const3 = lambda i: (0, 0, 0)

    def halo16(back, col):
        return pl.BlockSpec((16, D), lambda i: (jnp.maximum((nt - 1 - i) * (ts // 16) - back, 0), col))
    return pl.pallas_call(
        _after(deps, body), grid=(nt,),
        out_shape=(jax.ShapeDtypeStruct((s, D_IN), BF16), jax.ShapeDtypeStruct((N_SMALL, D), F32),
                   jax.ShapeDtypeStruct((HEADS, HB, HB), F32), jax.ShapeDtypeStruct((HEADS, HB, HB), F32),
                   jax.ShapeDtypeStruct((D, D), F32)),
        in_specs=[_ANY] * len(deps) + [pl.BlockSpec((ts, D), rev), pl.BlockSpec((ts, D), rev), pl.BlockSpec((ts, D), rev),
                  pl.BlockSpec((ts, D_IN), rev), pl.BlockSpec((ts, D), rev), pl.BlockSpec((ts, D), rev),
                  pl.BlockSpec((8, D), halo8),
                  halo16(1 + g // 16, 1), halo16(1 + g // 16, 2), halo16(1, 1), halo16(1, 2),
                  _VMEM, _VMEM, _VMEM, _VMEM, _VMEM],
        out_specs=[pl.BlockSpec((ts, D_IN), rev), pl.BlockSpec((N_SMALL, D), const2),
                   pl.BlockSpec((HEADS, HB, HB), const3), pl.BlockSpec((HEADS, HB, HB), const3), pl.BlockSpec((D, D), const2)],
        scratch_shapes=[pltpu.VMEM((2, g + 8, D), F32), pltpu.VMEM((1, g + 8, D), F32), pltpu.VMEM((1, g + 8, D), F32),
                        pltpu.VMEM((2, g + 8, D), F32), pltpu.VMEM((3, g + 8, D), F32), pltpu.VMEM((g, D), F32),
                        pltpu.VMEM((g, D), F32), pltpu.VMEM((g, D), F32), pltpu.VMEM((8, D), F32)],
        compiler_params=_cparams(), name="bwd_mix")(*deps, dx1, z1, merged, proj, u, h, h, proj, proj, proj, proj, vecs, w_rga,
                                                    w_rgx, w_out, small)


def _bwd_in(dproj, x, dx1, vecs, w_in_g, small, ts, deps=()):
    s = x.shape[0]

    def body(dp_ref, x_ref, dx1_ref, v_ref, w_ref, sm0_ref, gx_ref, sm_ref):
        @pl.when(pl.program_id(0) == 0)
        def _():
            sm_ref[...] = sm0_ref[...]

        def vrow(j):
            return v_ref[j:j + 1, :]

        dh1 = _dot_nt(dp_ref[:, 0:C_IN], w_ref[0])
        for k in range(1, N_CHIPS):
            dh1 += _dot_nt(dp_ref[:, k * C_IN:(k + 1) * C_IN], w_ref[k])
        xh, rstd = _rms(x_ref[...])
        sm_ref[G_SH1:G_SH1 + 1, :] += _rowsum(dh1)
        sm_ref[G_SC1:G_SC1 + 1, :] += _rowsum(dh1 * (xh * vrow(V_GMIX)))
        dn1 = dh1 * (1.0 + vrow(V_SC1))
        sm_ref[G_GMIX:G_GMIX + 1, :] += _rowsum(dn1 * xh)
        gx_ref[...] = dx1_ref[...] + _rms_bwd(dn1 * vrow(V_GMIX), xh, rstd)

    row = lambda i: (i, 0)
    return pl.pallas_call(
        _after(deps, body), grid=(s // ts,),
        out_shape=(jax.ShapeDtypeStruct((s, D), F32), jax.ShapeDtypeStruct((N_SMALL, D), F32)),
        in_specs=[_ANY] * len(deps) + [pl.BlockSpec((ts, D_IN), row), pl.BlockSpec((ts, D), row), pl.BlockSpec((ts, D), row),
                                       _VMEM, _VMEM, _VMEM],
        out_specs=[pl.BlockSpec((ts, D), row), pl.BlockSpec((N_SMALL, D), lambda i: (0, 0))],
        compiler_params=_cparams(), name="bwd_in")(*deps, dproj, x, dx1, vecs, w_in_g, small)


def _grad_w(a, b, n_col_blocks, ts, name, deps=()):
    s, m = a.shape
    tn = b.shape[1] // n_col_blocks
    n_steps = s // ts

    def body(a_ref, b_ref, o_ref, acc_ref):
        k = pl.program_id(1)

        @pl.when(k == 0)
        def _():
            acc_ref[...] = jnp.zeros((m, tn), F32)

        acc_ref[...] += _dot_tn(a_ref[...], b_ref[...])

        @pl.when(k == n_steps - 1)
        def _():
            o_ref[...] = acc_ref[...].astype(BF16)

    return pl.pallas_call(
        _after(deps, body), grid=(n_col_blocks, n_steps),
        out_shape=jax.ShapeDtypeStruct((n_col_blocks, m, tn), BF16),
        in_specs=[_ANY] * len(deps) + [pl.BlockSpec((ts, m), lambda n, k: (k, 0)), pl.BlockSpec((ts, tn), lambda n, k: (k, n))],
        out_specs=pl.BlockSpec((None, m, tn), lambda n, k: (n, 0, 0)),
        scratch_shapes=[pltpu.VMEM((m, tn), F32)],
        compiler_params=_cparams(2), name=name)(*deps, a, b)


def _ada_fwd(c_all, w_ada, b_ada):
    n = w_ada.shape[1]

    def body(c_ref, w_ref, b_ref, o_ref, ca_ref):
        c = c_ref[...]
        ca = c * jax.nn.sigmoid(c)
        ca_ref[...] = ca
        o_ref[...] = jnp.dot(ca, w_ref[...], preferred_element_type=F32, precision=lax.Precision.HIGHEST) + b_ref[...]

    return pl.pallas_call(
        body, out_shape=(jax.ShapeDtypeStruct((N_DEV, n), F32), jax.ShapeDtypeStruct((N_DEV, D), F32)),
        in_specs=[_VMEM] * 3, out_specs=[_VMEM] * 2, compiler_params=_cparams(0), name="ada_fwd")(c_all, w_ada, b_ada)


def _ada_bwd(c_act, dmod):
    n = dmod.shape[1]

    def body(c_ref, d_ref, o_ref):
        o_ref[...] = lax.dot_general(c_ref[...], d_ref[...], (((0,), (0,)), ((), ())), preferred_element_type=F32,
                                     precision=lax.Precision.HIGHEST)

    return pl.pallas_call(
        body, out_shape=jax.ShapeDtypeStruct((D, n), F32), in_specs=[_VMEM] * 2, out_specs=_VMEM,
        compiler_params=_cparams(0), name="ada_bwd")(c_act, dmod)


def _sum_small(parts):
    def body(p_ref, o_ref, d_ref):
        tot = p_ref[0]
        for dev in range(1, N_DEV):
            tot = tot + p_ref[dev]
        o_ref[...] = tot
        d_ref[...] = p_ref[:, 0:8, :]

    return pl.pallas_call(
        body, out_shape=(jax.ShapeDtypeStruct((N_SMALL, D), F32), jax.ShapeDtypeStruct((N_DEV, 8, D), F32)),
        in_specs=[_VMEM], out_specs=[_VMEM] * 2, compiler_params=_cparams(0), name="sum_small")(parts)


def _adamw(w, g, m, v, name, deps=()):
    rows, cols = w.shape
    tr = 128 if rows % 128 == 0 else (64 if rows % 64 == 0 else rows)

    def body(w_ref, g_ref, m_ref, v_ref, d_ref, nm_ref, nv_ref):
        g_ = g_ref[...]
        m_ = ADAM_B1 * m_ref[...] + (1.0 - ADAM_B1) * g_
        v_ = ADAM_B2 * v_ref[...] + (1.0 - ADAM_B2) * (g_ * g_)
        nm_ref[...] = m_
        nv_ref[...] = v_
        m_hat = m_ / (1.0 - ADAM_B1 ** ADAM_STEP)
        v_hat = v_ / (1.0 - ADAM_B2 ** ADAM_STEP)
        d_ref[...] = -ADAM_LR * (m_hat / (jnp.sqrt(v_hat) + ADAM_EPS) + ADAM_WD * w_ref[...])

    spec = pl.BlockSpec((tr, cols), lambda i: (i, 0))
    return pl.pallas_call(
        _after(deps, body), grid=(rows // tr,), out_shape=(jax.ShapeDtypeStruct((rows, cols), F32),) * 3,
        in_specs=[_ANY] * len(deps) + [spec] * 4, out_specs=[spec] * 3, compiler_params=_cparams(), name=name)(*deps, w, g, m, v)


def _adamw_small(items, name):
    n = len(items)

    def body(*refs):
        ins, outs = refs[:4 * n], refs[4 * n:]
        for k in range(n):
            w_ref, g_ref, m_ref, v_ref = ins[4 * k:4 * k + 4]
            d_ref, nm_ref, nv_ref = outs[3 * k:3 * k + 3]
            g_ = g_ref[...]
            m_ = ADAM_B1 * m_ref[...] + (1.0 - ADAM_B1) * g_
            v_ = ADAM_B2 * v_ref[...] + (1.0 - ADAM_B2) * (g_ * g_)
            nm_ref[...] = m_
            nv_ref[...] = v_
            m_hat = m_ / (1.0 - ADAM_B1 ** ADAM_STEP)
            v_hat = v_ / (1.0 - ADAM_B2 ** ADAM_STEP)
            d_ref[...] = -ADAM_LR * (m_hat / (jnp.sqrt(v_hat) + ADAM_EPS) + ADAM_WD * w_ref[...])

    out = pl.pallas_call(
        body, out_shape=tuple(jax.ShapeDtypeStruct(it[0].shape, F32) for it in items for _ in range(3)),
        in_specs=[_VMEM] * (4 * n), out_specs=[_VMEM] * (3 * n), name=name)(*[a for it in items for a in it])
    return [tuple(out[3 * k:3 * k + 3]) for k in range(n)]


def _adamw_halves(w, mine, other, m, v, c_idx, name, deps=()):
    r2, cols = mine.shape
    tr = next(t for t in (128, 64, 32, 16, 8) if r2 % t == 0)
    nh = r2 // tr

    def body(c_ref, w_ref, mine_ref, other_ref, m_ref, v_ref, g_ref, d_ref, nm_ref, nv_ref):
        g_ = jnp.where(pl.program_id(0) // nh == c_ref[0], mine_ref[...], other_ref[...])
        g_ref[...] = g_
        m_ = ADAM_B1 * m_ref[...] + (1.0 - ADAM_B1) * g_
        v_ = ADAM_B2 * v_ref[...] + (1.0 - ADAM_B2) * (g_ * g_)
        nm_ref[...] = m_
        nv_ref[...] = v_
        m_hat = m_ / (1.0 - ADAM_B1 ** ADAM_STEP)
        v_hat = v_ / (1.0 - ADAM_B2 ** ADAM_STEP)
        d_ref[...] = -ADAM_LR * (m_hat / (jnp.sqrt(v_hat) + ADAM_EPS) + ADAM_WD * w_ref[...])

    full = pl.BlockSpec((tr, cols), lambda i, c: (i, 0))
    mine_spec = pl.BlockSpec((tr, cols), lambda i, c: (jnp.clip(i - c[0] * nh, 0, nh - 1), 0))
    other_spec = pl.BlockSpec((tr, cols), lambda i, c: (jnp.clip(i - (1 - c[0]) * nh, 0, nh - 1), 0))
    return pl.pallas_call(
        lambda c_ref, *refs: body(c_ref, *refs[len(deps):]),
        grid_spec=pltpu.PrefetchScalarGridSpec(
            num_scalar_prefetch=1, grid=(2 * nh,),
            in_specs=[_ANY] * len(deps) + [full, mine_spec, other_spec, full, full], out_specs=[full] * 4),
        out_shape=(jax.ShapeDtypeStruct((2 * r2, cols), F32),) * 4, compiler_params=_cparams(), name=name,
    )(c_idx, *deps, w, mine, other, m, v)


def _add_halves(g, recv, c_idx, name):
    n, _, r2, cols = g.shape

    def body(c_ref, g_ref, r_ref, o_ref):
        o_ref[...] = (g_ref[...].astype(F32) + r_ref[...].astype(F32)).astype(BF16)

    return pl.pallas_call(
        body,
        grid_spec=pltpu.PrefetchScalarGridSpec(
            num_scalar_prefetch=1, grid=(n,),
            in_specs=[pl.BlockSpec((None, None, r2, cols), lambda k, c: (k, c[0], 0, 0)),
                      pl.BlockSpec((None, r2, cols), lambda k, c: (k, 0, 0))],
            out_specs=pl.BlockSpec((None, r2, cols), lambda k, c: (k, 0, 0))),
        out_shape=jax.ShapeDtypeStruct((n, r2, cols), BF16), compiler_params=_cparams(), name=name)(c_idx, g, recv)


def _sum_chips(parts, name):
    n, r2, cols = parts.shape
    tr = r2 // 2 if (r2 // 2) % 16 == 0 else r2

    def body(p_ref, o_ref):
        o_ref[...] = ((p_ref[0].astype(F32) + p_ref[1].astype(F32)) + p_ref[2].astype(F32)) + p_ref[3].astype(F32)

    return pl.pallas_call(
        body, grid=(r2 // tr,), out_shape=jax.ShapeDtypeStruct((r2, cols), F32),
        in_specs=[pl.BlockSpec((n, tr, cols), lambda i: (0, i, 0))], out_specs=pl.BlockSpec((tr, cols), lambda i: (i, 0)),
        compiler_params=_cparams(), name=name)(parts)


def _place():
    x, y, c = lax.axis_index("x"), lax.axis_index("y"), lax.axis_index("c")
    return x, y, c, 2 * x + y


def _flip(v, bit):
    return 1 - v if bit else v


def _allgather8(v, name, deps=()):
    r, n = v.shape

    def body(*refs):
        v_ref, out_ref, send_sems, recv_sems, local_sem = refs[len(deps):]
        x, y, c, _ = _place()
        me = 4 * x + 2 * y + c
        mine = pltpu.make_async_copy(v_ref, out_ref.at[me], local_sem)
        mine.start()
        sends = []
        for rel in range(1, N_DEV):
            peer = (_flip(x, rel & 4), _flip(y, rel & 2), _flip(c, rel & 1))
            cp = pltpu.make_async_remote_copy(v_ref, out_ref.at[me], send_sems.at[rel - 1], recv_sems.at[rel - 1],
                                              device_id=peer, device_id_type=MESH)
            cp.start()
            sends.append(cp)
        for rel in range(1, N_DEV):
            peer = (_flip(x, rel & 4), _flip(y, rel & 2), _flip(c, rel & 1))
            peer_idx = 4 * peer[0] + 2 * peer[1] + peer[2]
            pltpu.make_async_remote_copy(v_ref, out_ref.at[peer_idx], send_sems.at[rel - 1], recv_sems.at[rel - 1],
                                         device_id=peer, device_id_type=MESH).wait_recv()
        for cp in sends:
            cp.wait_send()
        mine.wait()

    return pl.pallas_call(
        body, out_shape=jax.ShapeDtypeStruct((N_DEV, r, n), F32), in_specs=[_ANY] * len(deps) + [_VMEM], out_specs=_VMEM,
        scratch_shapes=[pltpu.SemaphoreType.DMA((N_DEV - 1,)), pltpu.SemaphoreType.DMA((N_DEV - 1,)), pltpu.SemaphoreType.DMA(())],
        name=name)(*deps, v)


def _gather_weights(shards):
    nw = len(shards)

    def body(*refs):
        w_refs, out_refs = refs[:nw], refs[nw:2 * nw]
        send_sems, recv_sems = refs[2 * nw:]
        x, y, c, p = _place()
        sibling = (x, y, 1 - c)
        sends = []
        for j in range(1, N_CHIPS):
            peer = (_flip(x, j & 2), _flip(y, j & 1), c)
            for w in range(nw):
                cp = pltpu.make_async_remote_copy(w_refs[w].at[c], out_refs[w].at[p, c], send_sems.at[w * 6 + j - 1],
                                                  recv_sems.at[w * 6 + j - 1], device_id=peer, device_id_type=MESH)
                cp.start()
                sends.append(cp)
        for j in range(1, N_CHIPS):
            peer = (_flip(x, j & 2), _flip(y, j & 1), c)
            q = 2 * peer[0] + peer[1]
            for w in range(nw):
                pltpu.make_async_remote_copy(w_refs[w].at[c], out_refs[w].at[q, c], send_sems.at[w * 6 + j - 1],
                                             recv_sems.at[w * 6 + j - 1], device_id=peer, device_id_type=MESH).wait_recv()
                cp = pltpu.make_async_remote_copy(out_refs[w].at[q, c], out_refs[w].at[q, c], send_sems.at[w * 6 + 2 + j],
                                                  recv_sems.at[w * 6 + 2 + j], device_id=sibling, device_id_type=MESH)
                cp.start()
                sends.append(cp)
        for j in range(1, N_CHIPS):
            q = 2 * _flip(x, j & 2) + _flip(y, j & 1)
            for w in range(nw):
                pltpu.make_async_remote_copy(out_refs[w].at[q, 1 - c], out_refs[w].at[q, 1 - c], send_sems.at[w * 6 + 2 + j],
                                             recv_sems.at[w * 6 + 2 + j], device_id=sibling, device_id_type=MESH).wait_recv()
        for cp in sends:
            cp.wait_send()

    return pl.pallas_call(
        body, out_shape=tuple(jax.ShapeDtypeStruct((N_CHIPS,) + s.shape, s.dtype) for s in shards),
        in_specs=[_ANY] * nw, out_specs=[_ANY] * nw,
        scratch_shapes=[pltpu.SemaphoreType.DMA((6 * nw,)), pltpu.SemaphoreType.DMA((6 * nw,))],
        name="gather_weights")(*shards)


def _swap_halves(grads):
    nw = len(grads)

    def body(*refs):
        g_refs, out_refs = refs[:nw], refs[nw:2 * nw]
        send_sems, recv_sems = refs[2 * nw:]
        x, y, c, _ = _place()
        sibling = (x, y, 1 - c)
        sends = []
        for w in range(nw):
            for k in range(N_CHIPS):
                cp = pltpu.make_async_remote_copy(g_refs[w].at[k, 1 - c], out_refs[w].at[k], send_sems.at[w * N_CHIPS + k],
                                                  recv_sems.at[w * N_CHIPS + k], device_id=sibling, device_id_type=MESH)
                cp.start()
                sends.append(cp)
        for cp in sends:
            cp.wait_recv()
        for cp in sends:
            cp.wait_send()

    return pl.pallas_call(
        body, out_shape=tuple(jax.ShapeDtypeStruct((N_CHIPS,) + g.shape[2:], g.dtype) for g in grads),
        in_specs=[_ANY] * nw, out_specs=[_ANY] * nw,
        scratch_shapes=[pltpu.SemaphoreType.DMA((N_CHIPS * nw,)), pltpu.SemaphoreType.DMA((N_CHIPS * nw,))],
        name="swap_halves")(*grads)


def _scatter_chips(parts):
    nw = len(parts)

    def body(*refs):
        p_refs, out_refs = refs[:nw], refs[nw:2 * nw]
        send_sems, recv_sems = refs[2 * nw:]
        x, y, c, p = _place()
        sends = []
        for j in range(1, N_CHIPS):
            peer = (_flip(x, j & 2), _flip(y, j & 1), c)
            q = 2 * peer[0] + peer[1]
            for w in range(nw):
                cp = pltpu.make_async_remote_copy(p_refs[w].at[q], out_refs[w].at[p], send_sems.at[w * 3 + j - 1],
                                                  recv_sems.at[w * 3 + j - 1], device_id=peer, device_id_type=MESH)
                cp.start()
                sends.append(cp)
        for j in range(1, N_CHIPS):
            peer = (_flip(x, j & 2), _flip(y, j & 1), c)
            q = 2 * peer[0] + peer[1]
            for w in range(nw):
                pltpu.make_async_remote_copy(p_refs[w].at[q], out_refs[w].at[q], send_sems.at[w * 3 + j - 1],
                                             recv_sems.at[w * 3 + j - 1], device_id=peer, device_id_type=MESH).wait_recv()
        for cp in sends:
            cp.wait_send()

    return pl.pallas_call(
        body, out_shape=tuple(jax.ShapeDtypeStruct(s.shape, s.dtype) for s in parts),
        in_specs=[_ANY] * nw, out_specs=[_ANY] * nw,
        scratch_shapes=[pltpu.SemaphoreType.DMA((3 * nw,)), pltpu.SemaphoreType.DMA((3 * nw,))],
        name="scatter_chips")(*parts)


def _share_halves(halves):
    nw = len(halves)

    def body(*refs):
        h_refs, out_refs = refs[:nw], refs[nw:2 * nw]
        send_sems, recv_sems = refs[2 * nw:]
        x, y, c, _ = _place()
        sends = []
        for w in range(nw):
            cp = pltpu.make_async_remote_copy(h_refs[w], out_refs[w], send_sems.at[w], recv_sems.at[w],
                                              device_id=(x, y, 1 - c), device_id_type=MESH)
            cp.start()
            sends.append(cp)
        for cp in sends:
            cp.wait_recv()
        for cp in sends:
            cp.wait_send()

    return pl.pallas_call(
        body, out_shape=tuple(jax.ShapeDtypeStruct(s.shape, s.dtype) for s in halves),
        in_specs=[_ANY] * nw, out_specs=[_ANY] * nw,
        scratch_shapes=[pltpu.SemaphoreType.DMA((nw,)), pltpu.SemaphoreType.DMA((nw,))],
        name="share_halves")(*halves)


_HBM = pl.BlockSpec(memory_space=pltpu.HBM)
_SEM = pl.BlockSpec(memory_space=pltpu.SEMAPHORE)
_EFFECT = pltpu.SideEffectType.DATAFLOW_SIDE_EFFECTING


def _xchg_start(name, plan, n_copies, srcs, lands, after=()):
    bufs = list(srcs) + list(lands)
    ns, nb = len(srcs), len(srcs) + len(lands)

    def body(*refs):
        send_sems, recv_sems, token = refs[nb + len(after)], refs[nb + len(after) + 1], refs[-1]
        for i, (src, dst, peer, _) in enumerate(plan(_place(), refs[:ns], refs[ns:nb])):
            pltpu.make_async_remote_copy(src, dst, send_sems.at[i], recv_sems.at[i], device_id=peer, device_id_type=MESH).start()
        token[...] = jnp.zeros_like(token)

    out = pl.pallas_call(
        body, name=name,
        out_shape=(pltpu.SemaphoreType.DMA((n_copies,)), pltpu.SemaphoreType.DMA((n_copies,)),
                   *[pltpu.HBM(a.shape, a.dtype) for a in bufs], jax.ShapeDtypeStruct((8, 128), F32)),
        in_specs=[_HBM] * nb + [_ANY] * len(after), out_specs=(_SEM, _SEM, *[_HBM] * nb, _VMEM),
        input_output_aliases={i: 2 + i for i in range(nb)},
        compiler_params=pltpu.CompilerParams(has_side_effects=_EFFECT),
    )(*[pltpu.with_memory_space_constraint(a, pltpu.HBM) for a in bufs], *after)
    return (out[0], out[1]), out[2:2 + ns], out[2 + ns:2 + nb], out[-1]


def _xchg_wait(name, plan, sems, srcs, lands, after):
    bufs = list(srcs) + list(lands)
    ns, nb = len(srcs), len(srcs) + len(lands)

    def body(*refs):
        send_sems, recv_sems = refs[nb], refs[nb + 1]
        for i, (src, _, peer, mine) in enumerate(plan(_place(), refs[:ns], refs[ns:nb])):
            cp = pltpu.make_async_remote_copy(src, mine, send_sems.at[i], recv_sems.at[i], device_id=peer, device_id_type=MESH)
            cp.wait_send()
            cp.wait_recv()

    out = pl.pallas_call(
        body, name=name, out_shape=tuple(pltpu.HBM(a.shape, a.dtype) for a in bufs),
        in_specs=[_HBM] * nb + [_SEM, _SEM] + [_ANY] * len(after), out_specs=tuple([_HBM] * nb),
        input_output_aliases={i: i for i in range(nb)},
        compiler_params=pltpu.CompilerParams(has_side_effects=_EFFECT),
    )(*bufs, *sems, *after)
    return out[:ns], out[ns:]


def _other_chips(place):
    x, y, c, _ = place
    return [((_flip(x, j & 2), _flip(y, j & 1), c), 2 * _flip(x, j & 2) + _flip(y, j & 1)) for j in range(1, N_CHIPS)]


def _plan_gather_ici(place, src_refs, land_refs):
    _, _, c, p = place
    return [(s.at[c], l.at[p, c], peer, l.at[q, c]) for s, l in zip(src_refs, land_refs) for peer, q in _other_chips(place)]


def _plan_gather_d2d(place, src_refs, land_refs):
    x, y, c, _ = place
    return [(l.at[q, c], l.at[q, c], (x, y, 1 - c), l.at[q, 1 - c]) for l in land_refs for _, q in _other_chips(place)]


def _plan_swap(place, src_refs, land_refs):
    x, y, c, _ = place
    return [(s.at[k, 1 - c], l.at[k], (x, y, 1 - c), l.at[k]) for s, l in zip(src_refs, land_refs) for k in range(N_CHIPS)]


def _plan_scatter(place, src_refs, land_refs):
    _, _, _, p = place
    return [(s.at[q], l.at[p], peer, l.at[q]) for s, l in zip(src_refs, land_refs) for peer, q in _other_chips(place)]


def _plan_share(place, src_refs, land_refs):
    x, y, c, _ = place
    return [(s, l, (x, y, 1 - c), l) for s, l in zip(src_refs, land_refs)]


def _pack_rows(parts, n_rows, name, deps=()):
    def body(*refs):
        refs = refs[len(deps):]
        out_ref = refs[-1]
        out_ref[...] = jnp.zeros((n_rows, D), F32)
        at = 0
        for ref in refs[:-1]:
            k = ref.shape[0]
            out_ref[at:at + k, :] = ref[...]
            at += k

    return pl.pallas_call(
        body, out_shape=jax.ShapeDtypeStruct((n_rows, D), F32), in_specs=[_ANY] * len(deps) + [_VMEM] * len(parts),
        out_specs=_VMEM, name=name)(*deps, *parts)


TS_MM = 512
TS_GW = 1024
TS_MIX = 256


def _halved(a):
    n, r, cols = a.shape
    return a.reshape(n, 2, r // 2, cols)


def _gather_begin(name, shards, chip, after=()):
    lands = [lax.dynamic_update_index_in_dim(lax.empty((N_CHIPS,) + s.shape, s.dtype), s, chip, 0) for s in shards]
    sems, srcs, lands, token = _xchg_start(name + "_ici", _plan_gather_ici, 3 * len(shards), shards, lands, after)
    return name, sems, srcs, lands, token


def _gather_relay(handle, after):
    name, sems, srcs, lands, _ = handle
    _, lands = _xchg_wait(name + "_ici_wait", _plan_gather_ici, sems, srcs, lands, after)
    sems, _, lands, token = _xchg_start(name + "_d2d", _plan_gather_d2d, 3 * len(lands), [], lands)
    return name, sems, lands, token


def _gather_end(handle, after):
    name, sems, lands, _ = handle
    return _xchg_wait(name + "_d2d_wait", _plan_gather_d2d, sems, [], lands, after)[1]


def _rs_swap(name, grads, after=()):
    lands = [lax.empty((N_CHIPS,) + g.shape[2:], g.dtype) for g in grads]
    sems, grads, lands, token = _xchg_start(name + "_swap", _plan_swap, N_CHIPS * len(grads), grads, lands, after)
    return name, sems, grads, lands, token


def _rs_scatter(handle, after, chip, ci):
    name, sems, grads, lands, _ = handle
    grads, from_sibling = _xchg_wait(name + "_swap_wait", _plan_swap, sems, grads, lands, after)
    c_arr = jnp.reshape(ci, (1,)).astype(jnp.int32)
    pair_sums = [_add_halves(g, r, c_arr, "%s_add_halves_%d" % (name, k)) for k, (g, r) in enumerate(zip(grads, from_sibling))]
    lands = [lax.dynamic_update_index_in_dim(lax.empty(p.shape, p.dtype), lax.dynamic_index_in_dim(p, chip, 0, keepdims=False),
                                             chip, 0) for p in pair_sums]
    sems, pair_sums, lands, token = _xchg_start(name + "_scatter", _plan_scatter, 3 * len(pair_sums), pair_sums, lands)
    return name, sems, pair_sums, lands, token


def _rs_share(handle, after):
    name, sems, pair_sums, lands, _ = handle
    _, by_chip = _xchg_wait(name + "_scatter_wait", _plan_scatter, sems, pair_sums, lands, after)
    halves = [_sum_chips(b, "%s_sum_chips_%d" % (name, k)) for k, b in enumerate(by_chip)]
    lands = [lax.empty(h.shape, h.dtype) for h in halves]
    sems, halves, lands, token = _xchg_start(name + "_share", _plan_share, len(halves), halves, lands)
    return name, sems, halves, lands, token


def _rs_end(handle, after):
    name, sems, halves, lands, _ = handle
    halves, others = _xchg_wait(name + "_share_wait", _plan_share, sems, halves, lands, after)
    return list(zip(halves, others))


def kernel(x, c, w_ada, b_ada, g_norm_mix, w_in, conv_a_w, conv_b_w, conv_b_bias, w_rg_a, b_rg_a, w_rg_x, b_rg_x, lru_lambda, w_out, g_norm_ffn, w_gate_up, w_down, g_norm_final, loss_target, m_w_ada, m_b_ada, m_g_norm_mix, m_w_in, m_conv_a_w, m_conv_b_w, m_conv_b_bias, m_w_rg_a, m_b_rg_a, m_w_rg_x, m_b_rg_x, m_lru_lambda, m_w_out, m_g_norm_ffn, m_w_gate_up, m_w_down, m_g_norm_final, v_w_ada, v_b_ada, v_g_norm_mix, v_w_in, v_conv_a_w, v_conv_b_w, v_conv_b_bias, v_w_rg_a, v_b_rg_a, v_w_rg_x, v_b_rg_x, v_lru_lambda, v_w_out, v_g_norm_ffn, v_w_gate_up, v_w_down, v_g_norm_final):
    xi, yi, ci = lax.axis_index("x"), lax.axis_index("y"), lax.axis_index("c")
    chip = 2 * xi + yi
    me = 2 * chip + ci
    n_ada = w_ada.shape[2]

    def rg_shard(w):
        return w[0].astype(BF16).reshape(2, HEADS * HB // N_CHIPS // 2, HB)

    gather_a = _gather_begin("gather_a", [w_in[0].astype(BF16).reshape(2, D // 2, C_IN), rg_shard(w_rg_a), rg_shard(w_rg_x),
                                          w_out[0].astype(BF16).reshape(2, D // N_CHIPS // 2, D)], chip)

    def widen(w):
        return jnp.pad(w, ((0, 0), (0, D - w.shape[1])))

    got = _allgather8(_pack_rows([c, widen(conv_a_w[0]), widen(conv_b_w[0])], 8, "pack_c_conv", deps=[gather_a[-1]]),
                      "gather_c_conv")
    c_all = got[:, 0, :]
    conv_full = got[::2, 1:8, :D // N_CHIPS].transpose(1, 0, 2).reshape(7, D)

    mod_part, c_act = _ada_fwd(c_all, w_ada[0], lax.dynamic_slice_in_dim(b_ada, chip * n_ada, n_ada, axis=1))
    mod_all = _allgather8(mod_part, "gather_mod")
    mod_mine = lax.dynamic_index_in_dim(mod_all, me, axis=1, keepdims=False)[::2].reshape(6, D)
    vecs = _pack_rows([mod_mine, g_norm_mix, g_norm_ffn, g_norm_final.reshape(1, D), conv_b_bias, b_rg_a, b_rg_x, lru_lambda,
                       conv_full], N_VEC, "pack_vecs")

    gather_b = _gather_begin("gather_b", [w_gate_up[0].astype(BF16).reshape(2, D // 2, C_GU),
                                          w_down[0].astype(BF16).reshape(2, D_FF // N_CHIPS // 2, D)], chip, after=[vecs])
    gather_a = _gather_relay(gather_a, [gather_b[-1]])
    wg_in, wg_rga, wg_rgx, wg_out = _gather_end(gather_a, [])
    wg_in = wg_in.reshape(N_CHIPS, D, C_IN)
    wg_out = wg_out.reshape(D, D)

    def rg_full(wg):
        return wg.reshape(N_CHIPS, HEADS, HB // N_CHIPS, HB).transpose(1, 0, 2, 3).reshape(HEADS, HB, HB)

    wg_rga, wg_rgx = rg_full(wg_rga), rg_full(wg_rgx)

    def to_blocks(v):
        return v.reshape(-1, TS_MIX // TIME_BLOCKS, TIME_BLOCKS, D).transpose(0, 2, 1, 3).reshape(v.shape)

    def from_blocks(v):
        return v.reshape(-1, TIME_BLOCKS, TS_MIX // TIME_BLOCKS, D).transpose(0, 2, 1, 3).reshape(v.shape)

    xs, target = to_blocks(x[0]), to_blocks(loss_target[0])
    h1, proj = _fwd_in(xs, vecs, wg_in, TS_MM)
    gather_b = _gather_relay(gather_b, [proj])
    x1, merged, z1, u, h = _fwd_mix(proj, xs, vecs, wg_rga, wg_rgx, wg_out, TS_MIX, deps=[gather_b[-1]])
    wg_gu, wg_dn = _gather_end(gather_b, [x1])
    wg_gu, wg_dn = wg_gu.reshape(N_CHIPS, D, C_GU), wg_dn.reshape(D_FF, D)
    dx1, h2, act, dz2, dgu, sm_ffn = _ffn_loss(x1, target, vecs, wg_gu, wg_dn, TS_MIX)

    def rg_chunks(dw):
        return _halved(dw.reshape(HEADS, N_CHIPS, HB // N_CHIPS, HB).transpose(1, 0, 2, 3).reshape(N_CHIPS, HB, HB).astype(BF16))

    ts_gw = min(TS_GW, xs.shape[0])
    g_dn = _grad_w(act, dz2, 1, ts_gw, "grad_w_down")
    g_gu = _grad_w(h2, dgu, N_CHIPS, ts_gw, "grad_w_gate_up")
    rs_b = _rs_swap("rs_b", [_halved(g_gu), _halved(g_dn.reshape(N_CHIPS, D_FF // N_CHIPS, D))])
    dproj, sm_mix, dw_rga, dw_rgx, dw_out = _bwd_mix(dx1, z1, merged, proj, u, h, vecs, wg_rga, wg_rgx, wg_out, sm_ffn, TS_MIX,
                                                     deps=[rs_b[-1]])
    rs_b = _rs_scatter(rs_b, [dproj], chip, ci)
    g_in = _grad_w(h1, dproj, N_CHIPS, ts_gw, "grad_w_in", deps=[rs_b[-1]])
    rs_b = _rs_share(rs_b, [g_in])
    rs_a = _rs_swap("rs_a", [_halved(g_in), rg_chunks(dw_rga), rg_chunks(dw_rgx),
                             _halved(dw_out.astype(BF16).reshape(N_CHIPS, D // N_CHIPS, D))], after=[rs_b[-1]])

    c_arr = jnp.reshape(ci, (1,)).astype(jnp.int32)

    def step(name, w, g, m, v, deps=()):
        shape = w.shape
        two_d = (-1, shape[-1])
        d, nm, nv = _adamw(w.reshape(two_d), g.reshape(two_d), m.reshape(two_d), v.reshape(two_d), "adamw_" + name, deps)
        return g.reshape(shape), d.reshape(shape), nm.reshape(shape), nv.reshape(shape)

    def step_halves(name, w, halves, m, v, deps=()):
        shape = w.shape
        two_d = (-1, shape[-1])
        out = _adamw_halves(w.reshape(two_d), halves[0], halves[1], m.reshape(two_d), v.reshape(two_d), c_arr, "adamw_" + name, deps)
        return tuple(a.reshape(shape) for a in out)

    def shard_cols(row_block):
        return lax.dynamic_slice_in_dim(row_block, chip * (D // N_CHIPS), D // N_CHIPS, axis=1)

    gw_gu, gw_dn = _rs_end(rs_b, [rs_a[-1]])
    res = {
        "w_gate_up": step_halves("w_gate_up", w_gate_up, gw_gu, m_w_gate_up, v_w_gate_up, [rs_a[-1]]),
        "w_down": step_halves("w_down", w_down, gw_dn, m_w_down, v_w_down, [rs_a[-1]]),
    }
    rs_a = _rs_scatter(rs_a, [res["w_gate_up"][1], res["w_down"][1]], chip, ci)
    grad_x, sm_in = _bwd_in(dproj, xs, dx1, vecs, wg_in, sm_mix, TS_MM, deps=[rs_a[-1]])
    rs_a = _rs_share(rs_a, [grad_x])

    small, per_dev = _sum_small(_allgather8(sm_in, "gather_small", deps=[rs_a[-1]]))
    dmod_all = per_dev[:, 0:6, :].reshape(N_DEV, 6 * D)
    grad_w_ada = _ada_bwd(c_act, lax.dynamic_slice_in_dim(dmod_all, chip * n_ada, n_ada, axis=1))
    grad_b_ada = small[0:6].reshape(1, 6 * D)
    res["w_ada"] = step("w_ada", w_ada, grad_w_ada[None], m_w_ada, v_w_ada)
    small_sets = {
        "b_ada": (b_ada.reshape(6, D), grad_b_ada.reshape(6, D), m_b_ada.reshape(6, D), v_b_ada.reshape(6, D)),
        "g_norm_mix": (g_norm_mix, small[G_GMIX:G_GMIX + 1], m_g_norm_mix, v_g_norm_mix),
        "conv_a_w": (conv_a_w[0], shard_cols(small[G_WA0:G_WA0 + 3]), m_conv_a_w[0], v_conv_a_w[0]),
        "conv_b_w": (conv_b_w[0], shard_cols(small[G_WB0:G_WB0 + 4]), m_conv_b_w[0], v_conv_b_w[0]),
        "conv_b_bias": (conv_b_bias, small[G_CBB:G_CBB + 1], m_conv_b_bias, v_conv_b_bias),
        "b_rg_a": (b_rg_a, small[G_BA:G_BA + 1], m_b_rg_a, v_b_rg_a),
        "b_rg_x": (b_rg_x, small[G_BX:G_BX + 1], m_b_rg_x, v_b_rg_x),
        "lru_lambda": (lru_lambda, small[G_LAM:G_LAM + 1], m_lru_lambda, v_lru_lambda),
        "g_norm_ffn": (g_norm_ffn, small[G_GFFN:G_GFFN + 1], m_g_norm_ffn, v_g_norm_ffn),
        "g_norm_final": (g_norm_final.reshape(1, D), small[G_GFIN:G_GFIN + 1], m_g_norm_final.reshape(1, D),
                         v_g_norm_final.reshape(1, D)),
    }
    stepped = _adamw_small(list(small_sets.values()), "adamw_small")
    for (n, (w_, g_, _, _)), (d_, nm_, nv_) in zip(small_sets.items(), stepped):
        shape = (1,) + w_.shape if n.startswith("conv_") and n != "conv_b_bias" else w_.shape
        res[n] = tuple(a.reshape(shape) for a in (g_, d_, nm_, nv_))
    gw_in, gw_rga, gw_rgx, gw_out = _rs_end(rs_a, [res[n][1] for n in res])
    res["w_in"] = step_halves("w_in", w_in, gw_in, m_w_in, v_w_in)
    res["w_rg_a"] = step_halves("w_rg_a", w_rg_a, gw_rga, m_w_rg_a, v_w_rg_a)
    res["w_rg_x"] = step_halves("w_rg_x", w_rg_x, gw_rgx, m_w_rg_x, v_w_rg_x)
    res["w_out"] = step_halves("w_out", w_out, gw_out, m_w_out, v_w_out)
    res["b_ada"] = tuple(a.reshape(1, 6 * D) for a in res["b_ada"])
    res["g_norm_final"] = tuple(a.reshape(D) for a in res["g_norm_final"])
    names = ["w_ada", "b_ada", "g_norm_mix", "w_in", "conv_a_w", "conv_b_w", "conv_b_bias", "w_rg_a", "b_rg_a", "w_rg_x",
             "b_rg_x", "lru_lambda", "w_out", "g_norm_ffn", "w_gate_up", "w_down", "g_norm_final"]
    loss = jnp.sum(small[G_LOSS])
    return (loss, from_blocks(grad_x)[None], *[res[n][0] for n in names], *[res[n][1] for n in names],
            *[res[n][2] for n in names], *[res[n][3] for n in names])
```

```python
import functools

import jax
import jax.numpy as jnp
from jax import lax
from jax.experimental import pallas as pl
from jax.experimental.pallas import tpu as pltpu

F32 = jnp.float32
BF16 = jnp.bfloat16
MESH = pl.DeviceIdType.MESH

D = 1024
N_CHIPS = 4
N_DEV = 8
D_IN = 7 * D
C_IN = D_IN // N_CHIPS
D_FF = 2816
C_GU = 2 * D_FF // N_CHIPS
HEADS = 4
HB = D // HEADS
EPS = 1e-6
LRU_C = 8.0
ADAM_LR, ADAM_B1, ADAM_B2, ADAM_EPS, ADAM_WD, ADAM_STEP = 0.001, 0.9, 0.999, 1e-08, 0.01, 10
VMEM_LIMIT = 56 << 20

(V_SH1, V_SC1, V_GT1, V_SH2, V_SC2, V_GT2, V_GMIX, V_GFFN, V_GFIN, V_CBB, V_BA, V_BX, V_LAM,
 V_WA0, V_WA1, V_WA2, V_WB0, V_WB1, V_WB2, V_WB3) = range(20)
N_VEC = 24
(G_SH1, G_SC1, G_GT1, G_SH2, G_SC2, G_GT2, G_GMIX, G_CBB, G_BA, G_BX, G_LAM, G_GFFN, G_GFIN,
 G_WA0, G_WA1, G_WA2, G_WB0, G_WB1, G_WB2, G_WB3, G_LOSS) = range(21)
N_SMALL = 24

_VMEM = pl.BlockSpec(memory_space=pltpu.VMEM)
_ANY = pl.BlockSpec(memory_space=pl.ANY)


def _cparams(n_grid=1):
    return pltpu.CompilerParams(dimension_semantics=("arbitrary",) * n_grid, vmem_limit_bytes=VMEM_LIMIT)


def _after(deps, body):
    n = len(deps)
    return lambda *refs: body(*refs[n:])


def _rms(x):
    rstd = lax.rsqrt(jnp.mean(x * x, axis=-1, keepdims=True) + EPS)
    return x * rstd, rstd


def _rms_bwd(dxhat, xhat, rstd):
    return rstd * (dxhat - xhat * jnp.mean(dxhat * xhat, axis=-1, keepdims=True))


def _rowsum(v):
    return jnp.sum(v, axis=0, keepdims=True)


def _dot(a, b):
    return jnp.dot(a, b, preferred_element_type=F32)


def _dot_nt(a, b):
    return lax.dot_general(a, b, (((1,), (1,)), ((), ())), preferred_element_type=F32)


def _dot_tn(a, b):
    return lax.dot_general(a, b, (((0,), (0,)), ((), ())), preferred_element_type=F32)


def _gelu(x):
    k, c = 0.7978845608028654, 0.044715
    t = jnp.tanh(k * (x + c * x * x * x))
    return 0.5 * x * (1.0 + t), 0.5 * (1.0 + t) + 0.5 * x * (1.0 - t * t) * k * (1.0 + 3.0 * c * x * x)


def _log_sigmoid(lam):
    return jnp.minimum(lam, 0.0) - jnp.log1p(jnp.exp(-jnp.abs(lam)))


def _lru_gates(u, wa_ref, wx_ref, v_ref, row0):
    ub = u.astype(BF16)
    pre_a = jnp.concatenate([_dot(ub[:, h * HB:(h + 1) * HB], wa_ref[h]) for h in range(HEADS)], axis=1)
    pre_x = jnp.concatenate([_dot(ub[:, h * HB:(h + 1) * HB], wx_ref[h]) for h in range(HEADS)], axis=1)
    r = jax.nn.sigmoid(pre_a + v_ref[V_BA:V_BA + 1, :])
    ig = jax.nn.sigmoid(pre_x + v_ref[V_BX:V_BX + 1, :])
    log_a = LRU_C * r * _log_sigmoid(v_ref[V_LAM:V_LAM + 1, :])
    a = jnp.exp(log_a)
    x2 = 2.0 * log_a
    m2 = jnp.where(x2 > -0.03, -x2 * (1.0 + x2 * (0.5 + x2 * (1.0 / 6.0 + x2 * (1.0 / 24.0)))), 1.0 - a * a)
    mult = jnp.where(row0, 1.0, jnp.sqrt(jnp.maximum(m2, 0.0)))
    return r, ig, a, mult


TIME_BLOCKS = 8


def _late_blocks(v, buf, g, halo=None):
    n = buf.shape[0]
    out = []
    for idx in range(n):
        k = TIME_BLOCKS - n + idx
        buf[idx, 8:g + 8, :] = v[k * g:(k + 1) * g]
        if halo is not None:
            buf[idx, 7:8, :] = halo[idx]
        out.append(buf[idx, pl.ds(7, g), :])
        if halo is None:
            buf[idx, 7:8, :] = buf[idx, g + 7:g + 8, :]
    return out


def _earlier(v, s, late, g):
    return jnp.concatenate(late[len(late) - s:] + [v[0:(TIME_BLOCKS - s) * g]], axis=0)


def _early_blocks(v, buf, g):
    out = []
    for k in range(buf.shape[0]):
        buf[k, 0:g, :] = v[k * g:(k + 1) * g]
        out.append(buf[k, pl.ds(1, g), :])
        buf[k, g:g + 1, :] = buf[k, 0:1, :]
    return out


def _later(v, s, early, g):
    return jnp.concatenate([v[s * g:]] + early[0:s], axis=0)


def _fwd_in_first(x, vecs, w_in_g, q_idx, ts, deps=()):
    s = x.shape[0]

    def body(q_ref, x_ref, v_ref, w_ref, h1_ref, proj_ref):
        xhat, _ = _rms(x_ref[...])
        h = xhat * v_ref[V_GMIX:V_GMIX + 1, :] * (1.0 + v_ref[V_SC1:V_SC1 + 1, :]) + v_ref[V_SH1:V_SH1 + 1, :]
        hb = h.astype(BF16)
        h1_ref[...] = hb
        proj_ref[...] = _dot(hb, w_ref[...]).astype(BF16)

    return pl.pallas_call(
        lambda q_ref, *refs: body(q_ref, *refs[len(deps):]),
        grid_spec=pltpu.PrefetchScalarGridSpec(
            num_scalar_prefetch=1, grid=(s // ts,),
            in_specs=[_ANY] * len(deps) + [pl.BlockSpec((ts, D), lambda i, q: (i, 0)), _VMEM,
                                           pl.BlockSpec((None, D, C_IN), lambda i, q: (q[0], 0, 0))],
            out_specs=[pl.BlockSpec((ts, D), lambda i, q: (i, 0)), pl.BlockSpec((ts, C_IN), lambda i, q: (i, q[0]))]),
        out_shape=(jax.ShapeDtypeStruct((s, D), BF16), jax.ShapeDtypeStruct((s, D_IN), BF16)),
        compiler_params=_cparams(), name="fwd_in_own")(q_idx, *deps, x, vecs, w_in_g)


def _fwd_in_more(h1, w_in_g, proj, q_idx, ts, name, deps=()):
    s = h1.shape[0]

    def body(q_ref, h1_ref, w_ref, proj_in_ref, proj_ref):
        proj_ref[...] = _dot(h1_ref[...], w_ref[...]).astype(BF16)

    return pl.pallas_call(
        lambda q_ref, *refs: body(q_ref, *refs[len(deps):]),
        grid_spec=pltpu.PrefetchScalarGridSpec(
            num_scalar_prefetch=1, grid=(s // ts,),
            in_specs=[_ANY] * len(deps) + [pl.BlockSpec((ts, D), lambda i, q: (i, 0)),
                                           pl.BlockSpec((None, D, C_IN), lambda i, q: (q[0], 0, 0)), _ANY],
            out_specs=pl.BlockSpec((ts, C_IN), lambda i, q: (i, q[0]))),
        out_shape=jax.ShapeDtypeStruct((s, D_IN), BF16), input_output_aliases={len(deps) + 3: 0},
        compiler_params=_cparams(), name=name)(q_idx, *deps, h1, w_in_g, proj)


def _fwd_mix(proj, x, vecs, w_rga, w_rgx, w_out, ts, deps=()):
    s = x.shape[0]
    g = ts // TIME_BLOCKS

    def body(proj_ref, x_ref, v_ref, wa_ref, wx_ref, wo_ref, x1_ref, mg_ref, z1_ref, u_ref, h_ref,
             ua_buf, rx_buf, p_buf, q_buf, c_buf, hcarry):
        i = pl.program_id(0)

        @pl.when(i == 0)
        def _():
            ua_buf[...] = jnp.zeros(ua_buf.shape, F32)
            rx_buf[...] = jnp.zeros(rx_buf.shape, F32)
            hcarry[...] = jnp.zeros((8, D), F32)

        def seg(j):
            return proj_ref[:, j * D:(j + 1) * D].astype(F32)

        def vrow(j):
            return v_ref[j:j + 1, :]

        cb, cc, cx, rx, rg, ga, gb = (seg(j) for j in range(7))
        ua = cc * cx
        ua_late = _late_blocks(ua, ua_buf, g)
        rx_late = _late_blocks(rx, rx_buf, g)
        va = vrow(V_WA2) * ua + vrow(V_WA1) * _earlier(ua, 1, ua_late, g) + vrow(V_WA0) * _earlier(ua, 2, ua_late, g)
        u = (vrow(V_WB3) * rx + vrow(V_WB2) * _earlier(rx, 1, rx_late, g) + vrow(V_WB1) * _earlier(rx, 2, rx_late, g)
             + vrow(V_WB0) * _earlier(rx, 3, rx_late, g) + vrow(V_CBB))
        u_ref[...] = u

        rows = lax.broadcasted_iota(jnp.int32, (ts, D), 0)
        row0 = jnp.logical_and(rows == 0, i == 0)
        _, ig, a, mult = _lru_gates(u, wa_ref, wx_ref, v_ref, row0)
        bx = mult * (ig * u)

        prods, sums = [a[0:g]], [bx[0:g]]
        for k in range(1, TIME_BLOCKS):
            ak = a[k * g:(k + 1) * g]
            sums.append(ak * sums[-1] + bx[k * g:(k + 1) * g])
            prods.append(ak * prods[-1])
        p_buf[...] = prods[-1]
        q_buf[...] = sums[-1]
        state = hcarry[0:1, :]
        for j in range(g):
            c_buf[j:j + 1, :] = state
            state = p_buf[j:j + 1, :] * state + q_buf[j:j + 1, :]
        hcarry[0:1, :] = state
        entering = c_buf[...]
        h = jnp.concatenate([sums[k] + prods[k] * entering for k in range(TIME_BLOCKS)], axis=0)
        h_ref[...] = h

        gel, _ = _gelu(rg)
        merged = (jax.nn.sigmoid(ga) * (cb * va) + jax.nn.sigmoid(gb) * (h * gel)).astype(BF16)
        mg_ref[...] = merged
        z1 = _dot(merged, wo_ref[...])
        z1_ref[...] = z1.astype(BF16)
        x1_ref[...] = x_ref[...] + vrow(V_GT1) * z1

    row = lambda i: (i, 0)
    return pl.pallas_call(
        _after(deps, body), grid=(s // ts,),
        out_shape=(jax.ShapeDtypeStruct((s, D), F32), jax.ShapeDtypeStruct((s, D), BF16), jax.ShapeDtypeStruct((s, D), BF16),
                   jax.ShapeDtypeStruct((s, D), F32), jax.ShapeDtypeStruct((s, D), F32)),
        in_specs=[_ANY] * len(deps) + [pl.BlockSpec((ts, D_IN), row), pl.BlockSpec((ts, D), row), _VMEM, _VMEM, _VMEM, _VMEM],
        out_specs=[pl.BlockSpec((ts, D), row)] * 5,
        scratch_shapes=[pltpu.VMEM((2, g + 8, D), F32), pltpu.VMEM((3, g + 8, D), F32), pltpu.VMEM((g, D), F32),
                        pltpu.VMEM((g, D), F32), pltpu.VMEM((g, D), F32), pltpu.VMEM((8, D), F32)],
        compiler_params=_cparams(), name="fwd_mix")(*deps, proj, x, vecs, w_rga, w_rgx, w_out)


def _ffn_loss(x1, target, vecs, w_gu_g, w_dn, ts):
    s = x1.shape[0]

    def body(x1_ref, t_ref, v_ref, wgu_ref, wdn_ref, dx1_ref, h2_ref, act_ref, dz2_ref, dgu_ref, sm_ref):
        @pl.when(pl.program_id(0) == 0)
        def _():
            sm_ref[...] = jnp.zeros((N_SMALL, D), F32)

        def vrow(j):
            return v_ref[j:j + 1, :]

        def acc(j, val):
            sm_ref[j:j + 1, :] += _rowsum(val)

        x1 = x1_ref[...]
        xh1, rstd1 = _rms(x1)
        n2 = xh1 * vrow(V_GFFN)
        h2 = (n2 * (1.0 + vrow(V_SC2)) + vrow(V_SH2)).astype(BF16)
        h2_ref[...] = h2
        g = jnp.concatenate([_dot(h2, wgu_ref[0]), _dot(h2, wgu_ref[1])], axis=1)
        up = jnp.concatenate([_dot(h2, wgu_ref[2]), _dot(h2, wgu_ref[3])], axis=1)
        sg = jax.nn.sigmoid(g)
        silu = g * sg
        act = (silu * up).astype(BF16)
        act_ref[...] = act
        z2 = _dot(act, wdn_ref[...])
        x2 = x1 + vrow(V_GT2) * z2
        xh2, rstd2 = _rms(x2)
        err = xh2 * vrow(V_GFIN) - t_ref[...]
        acc(G_LOSS, (0.5 / D) * err * err)
        dy = err * (1.0 / D)
        acc(G_GFIN, dy * xh2)
        dx2 = _rms_bwd(dy * vrow(V_GFIN), xh2, rstd2)
        acc(G_GT2, dx2 * z2)
        dz2 = (vrow(V_GT2) * dx2).astype(BF16)
        dz2_ref[...] = dz2
        dact = _dot_nt(dz2, wdn_ref[...])
        dgate = (dact * up * (sg * (1.0 + g * (1.0 - sg)))).astype(BF16)
        dup = (dact * silu).astype(BF16)
        dgu_ref[:, 0:D_FF] = dgate
        dgu_ref[:, D_FF:2 * D_FF] = dup
        dh2 = (_dot_nt(dgate[:, 0:C_GU], wgu_ref[0]) + _dot_nt(dgate[:, C_GU:2 * C_GU], wgu_ref[1])
               + _dot_nt(dup[:, 0:C_GU], wgu_ref[2]) + _dot_nt(dup[:, C_GU:2 * C_GU], wgu_ref[3]))
        acc(G_SH2, dh2)
        acc(G_SC2, dh2 * n2)
        dn2 = dh2 * (1.0 + vrow(V_SC2))
        acc(G_GFFN, dn2 * xh1)
        dx1_ref[...] = dx2 + _rms_bwd(dn2 * vrow(V_GFFN), xh1, rstd1)

    row = lambda i: (i, 0)
    return pl.pallas_call(
        body, grid=(s // ts,),
        out_shape=(jax.ShapeDtypeStruct((s, D), F32), jax.ShapeDtypeStruct((s, D), BF16), jax.ShapeDtypeStruct((s, D_FF), BF16),
                   jax.ShapeDtypeStruct((s, D), BF16), jax.ShapeDtypeStruct((s, 2 * D_FF), BF16),
                   jax.ShapeDtypeStruct((N_SMALL, D), F32)),
        in_specs=[pl.BlockSpec((ts, D), row), pl.BlockSpec((ts, D), row), _VMEM, _VMEM, _VMEM],
        out_specs=[pl.BlockSpec((ts, D), row), pl.BlockSpec((ts, D), row), pl.BlockSpec((ts, D_FF), row),
                   pl.BlockSpec((ts, D), row), pl.BlockSpec((ts, 2 * D_FF), row), pl.BlockSpec((N_SMALL, D), lambda i: (0, 0))],
        compiler_params=_cparams(), name="ffn_loss")(x1, target, vecs, w_gu_g, w_dn)


def _bwd_mix(dx1, z1, merged, proj, u, h, vecs, w_rga, w_rgx, w_out, small, ts, deps=()):
    s = dx1.shape[0]
    nt = s // ts
    g = ts // TIME_BLOCKS
    assert g % 16 == 0

    def body(dx1_ref, z1_ref, mg_ref, proj_ref, u_ref, h_ref, hh_ref, cc6_ref, cx6_ref, cc7_ref, cx7_ref, v_ref, wa_ref, wx_ref,
             wo_ref, sm0_ref, dproj_ref, sm_ref, dwa_ref, dwx_ref, dwo_ref,
             ua_buf, h_buf, a_buf, dva_buf, du_buf, p_buf, q_buf, c_buf, lcarry):
        i = pl.program_id(0)
        first_tile = i == nt - 1

        @pl.when(i == 0)
        def _():
            a_buf[...] = jnp.zeros(a_buf.shape, F32)
            dva_buf[...] = jnp.zeros(dva_buf.shape, F32)
            du_buf[...] = jnp.zeros(du_buf.shape, F32)
            lcarry[...] = jnp.zeros((8, D), F32)
            sm_ref[...] = sm0_ref[...]
            dwa_ref[...] = jnp.zeros((HEADS, HB, HB), F32)
            dwx_ref[...] = jnp.zeros((HEADS, HB, HB), F32)
            dwo_ref[...] = jnp.zeros((D, D), F32)

        def seg(j):
            return proj_ref[:, j * D:(j + 1) * D].astype(F32)

        def vrow(j):
            return v_ref[j:j + 1, :]

        def acc(j, val):
            sm_ref[j:j + 1, :] += _rowsum(val)

        cb, cc, cx, rx, rg, ga, gb = (seg(j) for j in range(7))
        ua = cc * cx

        def last_row(v):
            pick = lax.broadcasted_iota(jnp.int32, v.shape, 0) == v.shape[0] - 1
            return jnp.where(first_tile, 0.0, jnp.sum(jnp.where(pick, v, 0.0), axis=0, keepdims=True))

        ua_halo = [last_row(cc6_ref[...].astype(F32) * cx6_ref[...].astype(F32)),
                   last_row(cc7_ref[...].astype(F32) * cx7_ref[...].astype(F32))]
        ua_late = _late_blocks(ua, ua_buf, g, ua_halo)
        va = vrow(V_WA2) * ua + vrow(V_WA1) * _earlier(ua, 1, ua_late, g) + vrow(V_WA0) * _earlier(ua, 2, ua_late, g)
        u = u_ref[...]
        h = h_ref[...]
        rows = lax.broadcasted_iota(jnp.int32, (ts, D), 0)
        row0 = jnp.logical_and(rows == 0, first_tile)
        r, ig, a, mult = _lru_gates(u, wa_ref, wx_ref, v_ref, row0)
        sga = jax.nn.sigmoid(ga)
        sgb = jax.nn.sigmoid(gb)
        gel, dgel = _gelu(rg)

        dx1 = dx1_ref[...]
        acc(G_GT1, dx1 * z1_ref[...].astype(F32))
        dz1 = (vrow(V_GT1) * dx1).astype(BF16)
        dwo_ref[...] += _dot_tn(mg_ref[...], dz1)
        dmg = _dot_nt(dz1, wo_ref[...])
        dproj_ref[:, 5 * D:6 * D] = (dmg * (cb * va) * sga * (1.0 - sga)).astype(BF16)
        dproj_ref[:, 6 * D:7 * D] = (dmg * (h * gel) * sgb * (1.0 - sgb)).astype(BF16)
        dya = dmg * sga
        dyb = dmg * sgb

        dproj_ref[:, 0:D] = (dya * va).astype(BF16)
        dva = dya * cb
        dva_early = _early_blocks(dva, dva_buf, g)
        dva1 = _later(dva, 1, dva_early, g)
        dva2 = _later(dva, 2, dva_early, g)
        dua = vrow(V_WA2) * dva + vrow(V_WA1) * dva1 + vrow(V_WA0) * dva2
        acc(G_WA2, ua * dva)
        acc(G_WA1, ua * dva1)
        acc(G_WA0, ua * dva2)
        dproj_ref[:, D:2 * D] = (dua * cx).astype(BF16)
        dproj_ref[:, 2 * D:3 * D] = (dua * cc).astype(BF16)

        dproj_ref[:, 4 * D:5 * D] = (dyb * h * dgel).astype(BF16)
        a_next = _later(a, 1, _early_blocks(a, a_buf, g), g)
        dh = dyb * gel
        last = TIME_BLOCKS - 1
        prods, sums = {last: a_next[last * g:]}, {last: dh[last * g:]}
        for k in range(last - 1, -1, -1):
            ak = a_next[k * g:(k + 1) * g]
            sums[k] = dh[k * g:(k + 1) * g] + ak * sums[k + 1]
            prods[k] = ak * prods[k + 1]
        p_buf[...] = prods[0]
        q_buf[...] = sums[0]
        state = lcarry[0:1, :]
        for j in range(g - 1, -1, -1):
            c_buf[j:j + 1, :] = state
            state = q_buf[j:j + 1, :] + p_buf[j:j + 1, :] * state
        lcarry[0:1, :] = state
        entering = c_buf[...]
        lam = jnp.concatenate([sums[k] + prods[k] * entering for k in range(TIME_BLOCKS)], axis=0)

        h_halo = [jnp.where(first_tile, 0.0, hh_ref[7:8, :])]
        da = lam * _earlier(h, 1, _late_blocks(h, h_buf, g, h_halo), g)
        dmult = jnp.where(row0, 0.0, lam * (ig * u))
        di = lam * mult * u
        du = lam * mult * ig
        dlog_a = da * a - dmult * (a * a) / mult
        lam_p = vrow(V_LAM)
        dr = dlog_a * (LRU_C * _log_sigmoid(lam_p))
        sm_ref[G_LAM:G_LAM + 1, :] += _rowsum(dlog_a * r) * (LRU_C * jax.nn.sigmoid(-lam_p))
        dpa = dr * r * (1.0 - r)
        dpx = di * ig * (1.0 - ig)
        acc(G_BA, dpa)
        acc(G_BX, dpx)
        dpab = dpa.astype(BF16)
        dpxb = dpx.astype(BF16)
        ub = u.astype(BF16)
        back = []
        for hd in range(HEADS):
            cols = slice(hd * HB, (hd + 1) * HB)
            back.append(_dot_nt(dpab[:, cols], wa_ref[hd]) + _dot_nt(dpxb[:, cols], wx_ref[hd]))
            dwa_ref[hd] += _dot_tn(ub[:, cols], dpab[:, cols])
            dwx_ref[hd] += _dot_tn(ub[:, cols], dpxb[:, cols])
        du = du + jnp.concatenate(back, axis=1)

        acc(G_CBB, du)
        du_early = _early_blocks(du, du_buf, g)
        du1 = _later(du, 1, du_early, g)
        du2 = _later(du, 2, du_early, g)
        du3 = _later(du, 3, du_early, g)
        dproj_ref[:, 3 * D:4 * D] = (vrow(V_WB3) * du + vrow(V_WB2) * du1 + vrow(V_WB1) * du2 + vrow(V_WB0) * du3).astype(BF16)
        acc(G_WB3, rx * du)
        acc(G_WB2, rx * du1)
        acc(G_WB1, rx * du2)
        acc(G_WB0, rx * du3)

    rev = lambda i: (nt - 1 - i, 0)
    halo8 = lambda i: (jnp.maximum((nt - 1 - i) * (ts // 8) - 1, 0), 0)
    const2 = lambda i: (0, 0)
    const3 = lambda i: (0, 0, 0)

    def halo16(back, col):
        return pl.BlockSpec((16, D), lambda i: (jnp.maximum((nt - 1 - i) * (ts // 16) - back, 0), col))
    return pl.pallas_call(
        _after(deps, body), grid=(nt,),
        out_shape=(jax.ShapeDtypeStruct((s, D_IN), BF16), jax.ShapeDtypeStruct((N_SMALL, D), F32),
                   jax.ShapeDtypeStruct((HEADS, HB, HB), F32), jax.ShapeDtypeStruct((HEADS, HB, HB), F32),
                   jax.ShapeDtypeStruct((D, D), F32)),
        in_specs=[_ANY] * len(deps) + [pl.BlockSpec((ts, D), rev), pl.BlockSpec((ts, D), rev), pl.BlockSpec((ts, D), rev),
                  pl.BlockSpec((ts, D_IN), rev), pl.BlockSpec((ts, D), rev), pl.BlockSpec((ts, D), rev),
                  pl.BlockSpec((8, D), halo8),
                  halo16(1 + g // 16, 1), halo16(1 + g // 16, 2), halo16(1, 1), halo16(1, 2),
                  _VMEM, _VMEM, _VMEM, _VMEM, _VMEM],
        out_specs=[pl.BlockSpec((ts, D_IN), rev), pl.BlockSpec((N_SMALL, D), const2),
                   pl.BlockSpec((HEADS, HB, HB), const3), pl.BlockSpec((HEADS, HB, HB), const3), pl.BlockSpec((D, D), const2)],
        scratch_shapes=[pltpu.VMEM((2, g + 8, D), F32), pltpu.VMEM((1, g + 8, D), F32), pltpu.VMEM((1, g + 8, D), F32),
                        pltpu.VMEM((2, g + 8, D), F32), pltpu.VMEM((3, g + 8, D), F32), pltpu.VMEM((g, D), F32),
                        pltpu.VMEM((g, D), F32), pltpu.VMEM((g, D), F32), pltpu.VMEM((8, D), F32)],
        compiler_params=_cparams(), name="bwd_mix")(*deps, dx1, z1, merged, proj, u, h, h, proj, proj, proj, proj, vecs, w_rga,
                                                    w_rgx, w_out, small)


def _bwd_in(dproj, x, dx1, vecs, w_in_g, small, ts, deps=()):
    s = x.shape[0]

    def body(dp_ref, x_ref, dx1_ref, v_ref, w_ref, sm0_ref, gx_ref, sm_ref):
        @pl.when(pl.program_id(0) == 0)
        def _():
            sm_ref[...] = sm0_ref[...]

        def vrow(j):
            return v_ref[j:j + 1, :]

        dh1 = _dot_nt(dp_ref[:, 0:C_IN], w_ref[0])
        for k in range(1, N_CHIPS):
            dh1 += _dot_nt(dp_ref[:, k * C_IN:(k + 1) * C_IN], w_ref[k])
        xh, rstd = _rms(x_ref[...])
        sm_ref[G_SH1:G_SH1 + 1, :] += _rowsum(dh1)
        sm_ref[G_SC1:G_SC1 + 1, :] += _rowsum(dh1 * (xh * vrow(V_GMIX)))
        dn1 = dh1 * (1.0 + vrow(V_SC1))
        sm_ref[G_GMIX:G_GMIX + 1, :] += _rowsum(dn1 * xh)
        gx_ref[...] = dx1_ref[...] + _rms_bwd(dn1 * vrow(V_GMIX), xh, rstd)

    row = lambda i: (i, 0)
    return pl.pallas_call(
        _after(deps, body), grid=(s // ts,),
        out_shape=(jax.ShapeDtypeStruct((s, D), F32), jax.ShapeDtypeStruct((N_SMALL, D), F32)),
        in_specs=[_ANY] * len(deps) + [pl.BlockSpec((ts, D_IN), row), pl.BlockSpec((ts, D), row), pl.BlockSpec((ts, D), row),
                                       _VMEM, _VMEM, _VMEM],
        out_specs=[pl.BlockSpec((ts, D), row), pl.BlockSpec((N_SMALL, D), lambda i: (0, 0))],
        compiler_params=_cparams(), name="bwd_in")(*deps, dproj, x, dx1, vecs, w_in_g, small)


def _grad_w(a, b, n_col_blocks, ts, name, deps=()):
    s, m = a.shape
    tn = b.shape[1] // n_col_blocks
    n_steps = s // ts

    def body(a_ref, b_ref, o_ref, acc_ref):
        k = pl.program_id(1)

        @pl.when(k == 0)
        def _():
            acc_ref[...] = jnp.zeros((m, tn), F32)

        acc_ref[...] += _dot_tn(a_ref[...], b_ref[...])

        @pl.when(k == n_steps - 1)
        def _():
            o_ref[...] = acc_ref[...].astype(BF16)

    return pl.pallas_call(
        _after(deps, body), grid=(n_col_blocks, n_steps),
        out_shape=jax.ShapeDtypeStruct((n_col_blocks, m, tn), BF16),
        in_specs=[_ANY] * len(deps) + [pl.BlockSpec((ts, m), lambda n, k: (k, 0)), pl.BlockSpec((ts, tn), lambda n, k: (k, n))],
        out_specs=pl.BlockSpec((None, m, tn), lambda n, k: (n, 0, 0)),
        scratch_shapes=[pltpu.VMEM((m, tn), F32)],
        compiler_params=_cparams(2), name=name)(*deps, a, b)


def _ada_fwd(c_all, w_ada, b_ada):
    n = w_ada.shape[1]

    def body(c_ref, w_ref, b_ref, o_ref, ca_ref):
        c = c_ref[...]
        ca = c * jax.nn.sigmoid(c)
        ca_ref[...] = ca
        o_ref[...] = jnp.dot(ca, w_ref[...], preferred_element_type=F32, precision=lax.Precision.HIGHEST) + b_ref[...]

    return pl.pallas_call(
        body, out_shape=(jax.ShapeDtypeStruct((N_DEV, n), F32), jax.ShapeDtypeStruct((N_DEV, D), F32)),
        in_specs=[_VMEM] * 3, out_specs=[_VMEM] * 2, compiler_params=_cparams(0), name="ada_fwd")(c_all, w_ada, b_ada)


def _ada_bwd(c_act, dmod):
    n = dmod.shape[1]

    def body(c_ref, d_ref, o_ref):
        o_ref[...] = lax.dot_general(c_ref[...], d_ref[...], (((0,), (0,)), ((), ())), preferred_element_type=F32,
                                     precision=lax.Precision.HIGHEST)

    return pl.pallas_call(
        body, out_shape=jax.ShapeDtypeStruct((D, n), F32), in_specs=[_VMEM] * 2, out_specs=_VMEM,
        compiler_params=_cparams(0), name="ada_bwd")(c_act, dmod)


def _sum_small(parts):
    def body(p_ref, o_ref, d_ref):
        tot = p_ref[0]
        for dev in range(1, N_DEV):
            tot = tot + p_ref[dev]
        o_ref[...] = tot
        d_ref[...] = p_ref[:, 0:8, :]

    return pl.pallas_call(
        body, out_shape=(jax.ShapeDtypeStruct((N_SMALL, D), F32), jax.ShapeDtypeStruct((N_DEV, 8, D), F32)),
        in_specs=[_VMEM], out_specs=[_VMEM] * 2, compiler_params=_cparams(0), name="sum_small")(parts)


def _adamw(w, g, m, v, name, deps=()):
    rows, cols = w.shape
    tr = 128 if rows % 128 == 0 else (64 if rows % 64 == 0 else rows)

    def body(w_ref, g_ref, m_ref, v_ref, d_ref, nm_ref, nv_ref):
        g_ = g_ref[...]
        m_ = ADAM_B1 * m_ref[...] + (1.0 - ADAM_B1) * g_
        v_ = ADAM_B2 * v_ref[...] + (1.0 - ADAM_B2) * (g_ * g_)
        nm_ref[...] = m_
        nv_ref[...] = v_
        m_hat = m_ / (1.0 - ADAM_B1 ** ADAM_STEP)
        v_hat = v_ / (1.0 - ADAM_B2 ** ADAM_STEP)
        d_ref[...] = -ADAM_LR * (m_hat / (jnp.sqrt(v_hat) + ADAM_EPS) + ADAM_WD * w_ref[...])

    spec = pl.BlockSpec((tr, cols), lambda i: (i, 0))
    return pl.pallas_call(
        _after(deps, body), grid=(rows // tr,), out_shape=(jax.ShapeDtypeStruct((rows, cols), F32),) * 3,
        in_specs=[_ANY] * len(deps) + [spec] * 4, out_specs=[spec] * 3, compiler_params=_cparams(), name=name)(*deps, w, g, m, v)


def _adamw_small(items, name):
    n = len(items)

    def body(*refs):
        ins, outs = refs[:4 * n], refs[4 * n:]
        for k in range(n):
            w_ref, g_ref, m_ref, v_ref = ins[4 * k:4 * k + 4]
            d_ref, nm_ref, nv_ref = outs[3 * k:3 * k + 3]
            g_ = g_ref[...]
            m_ = ADAM_B1 * m_ref[...] + (1.0 - ADAM_B1) * g_
            v_ = ADAM_B2 * v_ref[...] + (1.0 - ADAM_B2) * (g_ * g_)
            nm_ref[...] = m_
            nv_ref[...] = v_
            m_hat = m_ / (1.0 - ADAM_B1 ** ADAM_STEP)
            v_hat = v_ / (1.0 - ADAM_B2 ** ADAM_STEP)
            d_ref[...] = -ADAM_LR * (m_hat / (jnp.sqrt(v_hat) + ADAM_EPS) + ADAM_WD * w_ref[...])

    out = pl.pallas_call(
        body, out_shape=tuple(jax.ShapeDtypeStruct(it[0].shape, F32) for it in items for _ in range(3)),
        in_specs=[_VMEM] * (4 * n), out_specs=[_VMEM] * (3 * n), name=name)(*[a for it in items for a in it])
    return [tuple(out[3 * k:3 * k + 3]) for k in range(n)]


def _adamw_halves(w, mine, other, m, v, c_idx, name, deps=()):
    r2, cols = mine.shape
    tr = next(t for t in (128, 64, 32, 16, 8) if r2 % t == 0)
    nh = r2 // tr

    def body(c_ref, w_ref, mine_ref, other_ref, m_ref, v_ref, g_ref, d_ref, nm_ref, nv_ref):
        g_ = jnp.where(pl.program_id(0) // nh == c_ref[0], mine_ref[...], other_ref[...])
        g_ref[...] = g_
        m_ = ADAM_B1 * m_ref[...] + (1.0 - ADAM_B1) * g_
        v_ = ADAM_B2 * v_ref[...] + (1.0 - ADAM_B2) * (g_ * g_)
        nm_ref[...] = m_
        nv_ref[...] = v_
        m_hat = m_ / (1.0 - ADAM_B1 ** ADAM_STEP)
        v_hat = v_ / (1.0 - ADAM_B2 ** ADAM_STEP)
        d_ref[...] = -ADAM_LR * (m_hat / (jnp.sqrt(v_hat) + ADAM_EPS) + ADAM_WD * w_ref[...])

    full = pl.BlockSpec((tr, cols), lambda i, c: (i, 0))
    mine_spec = pl.BlockSpec((tr, cols), lambda i, c: (jnp.clip(i - c[0] * nh, 0, nh - 1), 0))
    other_spec = pl.BlockSpec((tr, cols), lambda i, c: (jnp.clip(i - (1 - c[0]) * nh, 0, nh - 1), 0))
    return pl.pallas_call(
        lambda c_ref, *refs: body(c_ref, *refs[len(deps):]),
        grid_spec=pltpu.PrefetchScalarGridSpec(
            num_scalar_prefetch=1, grid=(2 * nh,),
            in_specs=[_ANY] * len(deps) + [full, mine_spec, other_spec, full, full], out_specs=[full] * 4),
        out_shape=(jax.ShapeDtypeStruct((2 * r2, cols), F32),) * 4, compiler_params=_cparams(), name=name,
    )(c_idx, *deps, w, mine, other, m, v)


def _add_halves(g, recv, c_idx, name):
    n, _, r2, cols = g.shape

    def body(c_ref, g_ref, r_ref, o_ref):
        o_ref[...] = (g_ref[...].astype(F32) + r_ref[...].astype(F32)).astype(BF16)

    return pl.pallas_call(
        body,
        grid_spec=pltpu.PrefetchScalarGridSpec(
            num_scalar_prefetch=1, grid=(n,),
            in_specs=[pl.BlockSpec((None, None, r2, cols), lambda k, c: (k, c[0], 0, 0)),
                      pl.BlockSpec((None, r2, cols), lambda k, c: (k, 0, 0))],
            out_specs=pl.BlockSpec((None, r2, cols), lambda k, c: (k, 0, 0))),
        out_shape=jax.ShapeDtypeStruct((n, r2, cols), BF16), compiler_params=_cparams(), name=name)(c_idx, g, recv)


def _sum_chips(parts, name):
    n, r2, cols = parts.shape
    tr = r2 // 2 if (r2 // 2) % 16 == 0 else r2

    def body(p_ref, o_ref):
        o_ref[...] = ((p_ref[0].astype(F32) + p_ref[1].astype(F32)) + p_ref[2].astype(F32)) + p_ref[3].astype(F32)

    return pl.pallas_call(
        body, grid=(r2 // tr,), out_shape=jax.ShapeDtypeStruct((r2, cols), F32),
        in_specs=[pl.BlockSpec((n, tr, cols), lambda i: (0, i, 0))], out_specs=pl.BlockSpec((tr, cols), lambda i: (i, 0)),
        compiler_params=_cparams(), name=name)(parts)


def _place():
    x, y, c = lax.axis_index("x"), lax.axis_index("y"), lax.axis_index("c")
    return x, y, c, 2 * x + y


def _flip(v, bit):
    return 1 - v if bit else v


def _allgather8(v, name, deps=()):
    r, n = v.shape

    def body(*refs):
        v_ref, out_ref, send_sems, recv_sems, local_sem = refs[len(deps):]
        x, y, c, _ = _place()
        me = 4 * x + 2 * y + c
        mine = pltpu.make_async_copy(v_ref, out_ref.at[me], local_sem)
        mine.start()
        sends = []
        for rel in range(1, N_DEV):
            peer = (_flip(x, rel & 4), _flip(y, rel & 2), _flip(c, rel & 1))
            cp = pltpu.make_async_remote_copy(v_ref, out_ref.at[me], send_sems.at[rel - 1], recv_sems.at[rel - 1],
                                              device_id=peer, device_id_type=MESH)
            cp.start()
            sends.append(cp)
        for rel in range(1, N_DEV):
            peer = (_flip(x, rel & 4), _flip(y, rel & 2), _flip(c, rel & 1))
            peer_idx = 4 * peer[0] + 2 * peer[1] + peer[2]
            pltpu.make_async_remote_copy(v_ref, out_ref.at[peer_idx], send_sems.at[rel - 1], recv_sems.at[rel - 1],
                                         device_id=peer, device_id_type=MESH).wait_recv()
        for cp in sends:
            cp.wait_send()
        mine.wait()

    return pl.pallas_call(
        body, out_shape=jax.ShapeDtypeStruct((N_DEV, r, n), F32), in_specs=[_ANY] * len(deps) + [_VMEM], out_specs=_VMEM,
        scratch_shapes=[pltpu.SemaphoreType.DMA((N_DEV - 1,)), pltpu.SemaphoreType.DMA((N_DEV - 1,)), pltpu.SemaphoreType.DMA(())],
        name=name)(*deps, v)


def _gather_weights(shards):
    nw = len(shards)

    def body(*refs):
        w_refs, out_refs = refs[:nw], refs[nw:2 * nw]
        send_sems, recv_sems = refs[2 * nw:]
        x, y, c, p = _place()
        sibling = (x, y, 1 - c)
        sends = []
        for j in range(1, N_CHIPS):
            peer = (_flip(x, j & 2), _flip(y, j & 1), c)
            for w in range(nw):
                cp = pltpu.make_async_remote_copy(w_refs[w].at[c], out_refs[w].at[p, c], send_sems.at[w * 6 + j - 1],
                                                  recv_sems.at[w * 6 + j - 1], device_id=peer, device_id_type=MESH)
                cp.start()
                sends.append(cp)
        for j in range(1, N_CHIPS):
            peer = (_flip(x, j & 2), _flip(y, j & 1), c)
            q = 2 * peer[0] + peer[1]
            for w in range(nw):
                pltpu.make_async_remote_copy(w_refs[w].at[c], out_refs[w].at[q, c], send_sems.at[w * 6 + j - 1],
                                             recv_sems.at[w * 6 + j - 1], device_id=peer, device_id_type=MESH).wait_recv()
                cp = pltpu.make_async_remote_copy(out_refs[w].at[q, c], out_refs[w].at[q, c], send_sems.at[w * 6 + 2 + j],
                                                  recv_sems.at[w * 6 + 2 + j], device_id=sibling, device_id_type=MESH)
                cp.start()
                sends.append(cp)
        for j in range(1, N_CHIPS):
            q = 2 * _flip(x, j & 2) + _flip(y, j & 1)
            for w in range(nw):
                pltpu.make_async_remote_copy(out_refs[w].at[q, 1 - c], out_refs[w].at[q, 1 - c], send_sems.at[w * 6 + 2 + j],
                                             recv_sems.at[w * 6 + 2 + j], device_id=sibling, device_id_type=MESH).wait_recv()
        for cp in sends:
            cp.wait_send()

    return pl.pallas_call(
        body, out_shape=tuple(jax.ShapeDtypeStruct((N_CHIPS,) + s.shape, s.dtype) for s in shards),
        in_specs=[_ANY] * nw, out_specs=[_ANY] * nw,
        scratch_shapes=[pltpu.SemaphoreType.DMA((6 * nw,)), pltpu.SemaphoreType.DMA((6 * nw,))],
        name="gather_weights")(*shards)


def _swap_halves(grads):
    nw = len(grads)

    def body(*refs):
        g_refs, out_refs = refs[:nw], refs[nw:2 * nw]
        send_sems, recv_sems = refs[2 * nw:]
        x, y, c, _ = _place()
        sibling = (x, y, 1 - c)
        sends = []
        for w in range(nw):
            for k in range(N_CHIPS):
                cp = pltpu.make_async_remote_copy(g_refs[w].at[k, 1 - c], out_refs[w].at[k], send_sems.at[w * N_CHIPS + k],
                                                  recv_sems.at[w * N_CHIPS + k], device_id=sibling, device_id_type=MESH)
                cp.start()
                sends.append(cp)
        for cp in sends:
            cp.wait_recv()
        for cp in sends:
            cp.wait_send()

    return pl.pallas_call(
        body, out_shape=tuple(jax.ShapeDtypeStruct((N_CHIPS,) + g.shape[2:], g.dtype) for g in grads),
        in_specs=[_ANY] * nw, out_specs=[_ANY] * nw,
        scratch_shapes=[pltpu.SemaphoreType.DMA((N_CHIPS * nw,)), pltpu.SemaphoreType.DMA((N_CHIPS * nw,))],
        name="swap_halves")(*grads)


def _scatter_chips(parts):
    nw = len(parts)

    def body(*refs):
        p_refs, out_refs = refs[:nw], refs[nw:2 * nw]
        send_sems, recv_sems = refs[2 * nw:]
        x, y, c, p = _place()
        sends = []
        for j in range(1, N_CHIPS):
            peer = (_flip(x, j & 2), _flip(y, j & 1), c)
            q = 2 * peer[0] + peer[1]
            for w in range(nw):
                cp = pltpu.make_async_remote_copy(p_refs[w].at[q], out_refs[w].at[p], send_sems.at[w * 3 + j - 1],
                                                  recv_sems.at[w * 3 + j - 1], device_id=peer, device_id_type=MESH)
                cp.start()
                sends.append(cp)
        for j in range(1, N_CHIPS):
            peer = (_flip(x, j & 2), _flip(y, j & 1), c)
            q = 2 * peer[0] + peer[1]
            for w in range(nw):
                pltpu.make_async_remote_copy(p_refs[w].at[q], out_refs[w].at[q], send_sems.at[w * 3 + j - 1],
                                             recv_sems.at[w * 3 + j - 1], device_id=peer, device_id_type=MESH).wait_recv()
        for cp in sends:
            cp.wait_send()

    return pl.pallas_call(
        body, out_shape=tuple(jax.ShapeDtypeStruct(s.shape, s.dtype) for s in parts),
        in_specs=[_ANY] * nw, out_specs=[_ANY] * nw,
        scratch_shapes=[pltpu.SemaphoreType.DMA((3 * nw,)), pltpu.SemaphoreType.DMA((3 * nw,))],
        name="scatter_chips")(*parts)


def _share_halves(halves):
    nw = len(halves)

    def body(*refs):
        h_refs, out_refs = refs[:nw], refs[nw:2 * nw]
        send_sems, recv_sems = refs[2 * nw:]
        x, y, c, _ = _place()
        sends = []
        for w in range(nw):
            cp = pltpu.make_async_remote_copy(h_refs[w], out_refs[w], send_sems.at[w], recv_sems.at[w],
                                              device_id=(x, y, 1 - c), device_id_type=MESH)
            cp.start()
            sends.append(cp)
        for cp in sends:
            cp.wait_recv()
        for cp in sends:
            cp.wait_send()

    return pl.pallas_call(
        body, out_shape=tuple(jax.ShapeDtypeStruct(s.shape, s.dtype) for s in halves),
        in_specs=[_ANY] * nw, out_specs=[_ANY] * nw,
        scratch_shapes=[pltpu.SemaphoreType.DMA((nw,)), pltpu.SemaphoreType.DMA((nw,))],
        name="share_halves")(*halves)


_HBM = pl.BlockSpec(memory_space=pltpu.HBM)
_SEM = pl.BlockSpec(memory_space=pltpu.SEMAPHORE)
_EFFECT = pltpu.SideEffectType.DATAFLOW_SIDE_EFFECTING


def _xchg_start(name, plan, n_copies, srcs, lands, after=()):
    bufs = list(srcs) + list(lands)
    ns, nb = len(srcs), len(srcs) + len(lands)

    def body(*refs):
        send_sems, recv_sems, token = refs[nb + len(after)], refs[nb + len(after) + 1], refs[-1]
        for i, (src, dst, peer, _) in enumerate(plan(_place(), refs[:ns], refs[ns:nb])):
            pltpu.make_async_remote_copy(src, dst, send_sems.at[i], recv_sems.at[i], device_id=peer, device_id_type=MESH).start()
        token[...] = jnp.zeros_like(token)

    out = pl.pallas_call(
        body, name=name,
        out_shape=(pltpu.SemaphoreType.DMA((n_copies,)), pltpu.SemaphoreType.DMA((n_copies,)),
                   *[pltpu.HBM(a.shape, a.dtype) for a in bufs], jax.ShapeDtypeStruct((8, 128), F32)),
        in_specs=[_HBM] * nb + [_ANY] * len(after), out_specs=(_SEM, _SEM, *[_HBM] * nb, _VMEM),
        input_output_aliases={i: 2 + i for i in range(nb)},
        compiler_params=pltpu.CompilerParams(has_side_effects=_EFFECT),
    )(*[pltpu.with_memory_space_constraint(a, pltpu.HBM) for a in bufs], *after)
    return (out[0], out[1]), out[2:2 + ns], out[2 + ns:2 + nb], out[-1]


def _xchg_wait(name, plan, sems, srcs, lands, after, sem_ids=None):
    bufs = list(srcs) + list(lands)
    ns, nb = len(srcs), len(srcs) + len(lands)

    def body(*refs):
        send_sems, recv_sems = refs[nb], refs[nb + 1]
        copies = plan(_place(), refs[:ns], refs[ns:nb])
        ids = range(len(copies)) if sem_ids is None else sem_ids
        for i, (src, _, peer, mine) in zip(ids, copies, strict=True):
            if i is not None:
                cp = pltpu.make_async_remote_copy(src, mine, send_sems.at[i], recv_sems.at[i], device_id=peer,
                                                  device_id_type=MESH)
                cp.wait_send()
                cp.wait_recv()

    out = pl.pallas_call(
        body, name=name, out_shape=tuple(pltpu.HBM(a.shape, a.dtype) for a in bufs),
        in_specs=[_HBM] * nb + [_SEM, _SEM] + [_ANY] * len(after), out_specs=tuple([_HBM] * nb),
        input_output_aliases={i: i for i in range(nb)},
        compiler_params=pltpu.CompilerParams(has_side_effects=_EFFECT),
    )(*bufs, *sems, *after)
    return out[:ns], out[ns:]


def _other_chips(place, which=(1, 2, 3)):
    x, y, c, _ = place
    return [((_flip(x, j & 2), _flip(y, j & 1), c), 2 * _flip(x, j & 2) + _flip(y, j & 1)) for j in which]


def _plan_gather_ici(place, src_refs, land_refs):
    _, _, c, p = place
    return [(s.at[c], l.at[p, c], peer, l.at[q, c]) for s, l in zip(src_refs, land_refs) for peer, q in _other_chips(place)]


def _plan_relay(which):
    def plan(place, src_refs, land_refs):
        x, y, c, _ = place
        return [(l.at[q, c], l.at[q, c], (x, y, 1 - c), l.at[q, 1 - c]) for l in land_refs for _, q in _other_chips(place, which)]
    return plan


def _plan_swap(place, src_refs, land_refs):
    x, y, c, _ = place
    return [(s.at[k, 1 - c], l.at[k], (x, y, 1 - c), l.at[k]) for s, l in zip(src_refs, land_refs) for k in range(N_CHIPS)]


def _plan_scatter(place, src_refs, land_refs):
    _, _, _, p = place
    return [(s.at[q], l.at[p], peer, l.at[q]) for s, l in zip(src_refs, land_refs) for peer, q in _other_chips(place)]


def _plan_share(place, src_refs, land_refs):
    x, y, c, _ = place
    return [(s, l, (x, y, 1 - c), l) for s, l in zip(src_refs, land_refs)]


def _pack_rows(parts, n_rows, name, deps=()):
    def body(*refs):
        refs = refs[len(deps):]
        out_ref = refs[-1]
        out_ref[...] = jnp.zeros((n_rows, D), F32)
        at = 0
        for ref in refs[:-1]:
            k = ref.shape[0]
            out_ref[at:at + k, :] = ref[...]
            at += k

    return pl.pallas_call(
        body, out_shape=jax.ShapeDtypeStruct((n_rows, D), F32), in_specs=[_ANY] * len(deps) + [_VMEM] * len(parts),
        out_specs=_VMEM, name=name)(*deps, *parts)


TS_MM = 512
TS_GW = 1024
TS_MIX = 256


def _halved(a):
    n, r, cols = a.shape
    return a.reshape(n, 2, r // 2, cols)


def _rs_swap(name, grads, after=()):
    lands = [lax.empty((N_CHIPS,) + g.shape[2:], g.dtype) for g in grads]
    sems, grads, lands, token = _xchg_start(name + "_swap", _plan_swap, N_CHIPS * len(grads), grads, lands, after)
    return name, sems, grads, lands, token


def _rs_scatter(handle, after, chip, ci):
    name, sems, grads, lands, _ = handle
    grads, from_sibling = _xchg_wait(name + "_swap_wait", _plan_swap, sems, grads, lands, after)
    c_arr = jnp.reshape(ci, (1,)).astype(jnp.int32)
    pair_sums = [_add_halves(g, r, c_arr, "%s_add_halves_%d" % (name, k)) for k, (g, r) in enumerate(zip(grads, from_sibling))]
    lands = [lax.dynamic_update_index_in_dim(lax.empty(p.shape, p.dtype), lax.dynamic_index_in_dim(p, chip, 0, keepdims=False),
                                             chip, 0) for p in pair_sums]
    sems, pair_sums, lands, token = _xchg_start(name + "_scatter", _plan_scatter, 3 * len(pair_sums), pair_sums, lands)
    return name, sems, pair_sums, lands, token


def _rs_share(handle, after):
    name, sems, pair_sums, lands, _ = handle
    _, by_chip = _xchg_wait(name + "_scatter_wait", _plan_scatter, sems, pair_sums, lands, after)
    halves = [_sum_chips(b, "%s_sum_chips_%d" % (name, k)) for k, b in enumerate(by_chip)]
    lands = [lax.empty(h.shape, h.dtype) for h in halves]
    sems, halves, lands, token = _xchg_start(name + "_share", _plan_share, len(halves), halves, lands)
    return name, sems, halves, lands, token


def _rs_end(handle, after):
    name, sems, halves, lands, _ = handle
    halves, others = _xchg_wait(name + "_share_wait", _plan_share, sems, halves, lands, after)
    return list(zip(halves, others))


def kernel(x, c, w_ada, b_ada, g_norm_mix, w_in, conv_a_w, conv_b_w, conv_b_bias, w_rg_a, b_rg_a, w_rg_x, b_rg_x, lru_lambda, w_out, g_norm_ffn, w_gate_up, w_down, g_norm_final, loss_target, m_w_ada, m_b_ada, m_g_norm_mix, m_w_in, m_conv_a_w, m_conv_b_w, m_conv_b_bias, m_w_rg_a, m_b_rg_a, m_w_rg_x, m_b_rg_x, m_lru_lambda, m_w_out, m_g_norm_ffn, m_w_gate_up, m_w_down, m_g_norm_final, v_w_ada, v_b_ada, v_g_norm_mix, v_w_in, v_conv_a_w, v_conv_b_w, v_conv_b_bias, v_w_rg_a, v_b_rg_a, v_w_rg_x, v_b_rg_x, v_lru_lambda, v_w_out, v_g_norm_ffn, v_w_gate_up, v_w_down, v_g_norm_final):
    xi, yi, ci = lax.axis_index("x"), lax.axis_index("y"), lax.axis_index("c")
    chip = 2 * xi + yi
    me = 2 * chip + ci
    n_ada = w_ada.shape[2]

    def widen(w):
        return jnp.pad(w, ((0, 0), (0, D - w.shape[1])))

    got = _allgather8(_pack_rows([c, widen(conv_a_w[0]), widen(conv_b_w[0])], 8, "pack_c_conv"), "gather_c_conv")
    c_all = got[:, 0, :]
    conv_full = got[::2, 1:8, :D // N_CHIPS].transpose(1, 0, 2).reshape(7, D)

    mod_part, c_act = _ada_fwd(c_all, w_ada[0], lax.dynamic_slice_in_dim(b_ada, chip * n_ada, n_ada, axis=1))
    mod_all = _allgather8(mod_part, "gather_mod")
    mod_mine = lax.dynamic_index_in_dim(mod_all, me, axis=1, keepdims=False)[::2].reshape(6, D)
    vecs = _pack_rows([mod_mine, g_norm_mix, g_norm_ffn, g_norm_final.reshape(1, D), conv_b_bias, b_rg_a, b_rg_x, lru_lambda,
                       conv_full], N_VEC, "pack_vecs")

    def rg_shard(w):
        return w[0].astype(BF16).reshape(2, HEADS * HB // N_CHIPS // 2, HB)

    shards = [w_in[0].astype(BF16).reshape(2, D // 2, C_IN), rg_shard(w_rg_a), rg_shard(w_rg_x),
              w_out[0].astype(BF16).reshape(2, D // N_CHIPS // 2, D), w_gate_up[0].astype(BF16).reshape(2, D // 2, C_GU),
              w_down[0].astype(BF16).reshape(2, D_FF // N_CHIPS // 2, D)]
    lands = [lax.dynamic_update_index_in_dim(lax.empty((N_CHIPS,) + s.shape, s.dtype), s, chip, 0) for s in shards]
    ici_sems, shards, lands, ici_token = _xchg_start("gather_ici", _plan_gather_ici, 3 * len(shards), shards, lands, after=[vecs])
    shards, lands = list(shards), list(lands)

    def arrive(name, first, last, after, sem_ids):
        srcs, zone = _xchg_wait(name + "_ici_wait", _plan_gather_ici, ici_sems, shards[first:last], lands[first:last], after, sem_ids)
        shards[first:last], lands[first:last] = srcs, zone

    def relay(name, first, last, which):
        plan = _plan_relay(which)
        sems, _, zone, token = _xchg_start(name + "_d2d", plan, len(which) * (last - first), [], lands[first:last])
        lands[first:last] = zone
        return name, plan, sems, first, last, token

    def relayed(handle, after):
        name, plan, sems, first, last, _ = handle
        lands[first:last] = _xchg_wait(name + "_d2d_wait", plan, sems, [], lands[first:last], after)[1]

    def to_blocks(v):
        return v.reshape(-1, TS_MIX // TIME_BLOCKS, TIME_BLOCKS, D).transpose(0, 2, 1, 3).reshape(v.shape)

    def from_blocks(v):
        return v.reshape(-1, TIME_BLOCKS, TS_MIX // TIME_BLOCKS, D).transpose(0, 2, 1, 3).reshape(v.shape)

    def chip_index(j):
        return jnp.reshape(chip ^ j, (1,)).astype(jnp.int32)

    def wg_in():
        return lands[0].reshape(N_CHIPS, D, C_IN)

    xs, target = to_blocks(x[0]), to_blocks(loss_target[0])
    h1, proj = _fwd_in_first(xs, vecs, wg_in(), chip_index(0), TS_MM, deps=[ici_token])
    arrive("gather_in_near", 0, 1, [proj], [0, 1, None])
    near = relay("gather_in_near", 0, 1, (1, 2))
    relayed(near, [near[-1]])
    proj = _fwd_in_more(h1, wg_in(), proj, chip_index(1), TS_MM, "fwd_in_y")
    arrive("gather_in_far", 0, 1, [proj], [None, None, 2])
    far = relay("gather_in_far", 0, 1, (3,))
    proj = _fwd_in_more(h1, wg_in(), proj, chip_index(2), TS_MM, "fwd_in_x", deps=[far[-1]])
    relayed(far, [proj])
    arrive("gather_mix", 1, 4, [proj], list(range(3, 12)))
    mix = relay("gather_mix", 1, 4, (1, 2, 3))
    proj = _fwd_in_more(h1, wg_in(), proj, chip_index(3), TS_MM, "fwd_in_xy", deps=[mix[-1]])
    relayed(mix, [proj])
    wg_rga, wg_rgx, wg_out = lands[1:4]
    wg_out = wg_out.reshape(D, D)

    def rg_full(wg):
        return wg.reshape(N_CHIPS, HEADS, HB // N_CHIPS, HB).transpose(1, 0, 2, 3).reshape(HEADS, HB, HB)

    wg_rga, wg_rgx = rg_full(wg_rga), rg_full(wg_rgx)

    arrive("gather_ffn", 4, 6, [proj], list(range(12, 18)))
    ffn = relay("gather_ffn", 4, 6, (1, 2, 3))
    x1, merged, z1, u, h = _fwd_mix(proj, xs, vecs, wg_rga, wg_rgx, wg_out, TS_MIX, deps=[ffn[-1]])
    relayed(ffn, [x1])
    wg_gu, wg_dn = lands[4:6]
    wg_gu, wg_dn = wg_gu.reshape(N_CHIPS, D, C_GU), wg_dn.reshape(D_FF, D)
    dx1, h2, act, dz2, dgu, sm_ffn = _ffn_loss(x1, target, vecs, wg_gu, wg_dn, TS_MIX)

    def rg_chunks(dw):
        return _halved(dw.reshape(HEADS, N_CHIPS, HB // N_CHIPS, HB).transpose(1, 0, 2, 3).reshape(N_CHIPS, HB, HB).astype(BF16))

    ts_gw = min(TS_GW, xs.shape[0])
    g_dn = _grad_w(act, dz2, 1, ts_gw, "grad_w_down")
    g_gu = _grad_w(h2, dgu, N_CHIPS, ts_gw, "grad_w_gate_up")
    rs_b = _rs_swap("rs_b", [_halved(g_gu), _halved(g_dn.reshape(N_CHIPS, D_FF // N_CHIPS, D))])
    dproj, sm_mix, dw_rga, dw_rgx, dw_out = _bwd_mix(dx1, z1, merged, proj, u, h, vecs, wg_rga, wg_rgx, wg_out, sm_ffn, TS_MIX,
                                                     deps=[rs_b[-1]])
    rs_b = _rs_scatter(rs_b, [dproj], chip, ci)
    g_in = _grad_w(h1, dproj, N_CHIPS, ts_gw, "grad_w_in", deps=[rs_b[-1]])
    rs_b = _rs_share(rs_b, [g_in])
    rs_a = _rs_swap("rs_a", [_halved(g_in), rg_chunks(dw_rga), rg_chunks(dw_rgx),
                             _halved(dw_out.astype(BF16).reshape(N_CHIPS, D // N_CHIPS, D))], after=[rs_b[-1]])

    c_arr = jnp.reshape(ci, (1,)).astype(jnp.int32)

    def step(name, w, g, m, v, deps=()):
        shape = w.shape
        two_d = (-1, shape[-1])
        d, nm, nv = _adamw(w.reshape(two_d), g.reshape(two_d), m.reshape(two_d), v.reshape(two_d), "adamw_" + name, deps)
        return g.reshape(shape), d.reshape(shape), nm.reshape(shape), nv.reshape(shape)

    def step_halves(name, w, halves, m, v, deps=()):
        shape = w.shape
        two_d = (-1, shape[-1])
        out = _adamw_halves(w.reshape(two_d), halves[0], halves[1], m.reshape(two_d), v.reshape(two_d), c_arr, "adamw_" + name, deps)
        return tuple(a.reshape(shape) for a in out)

    def shard_cols(row_block):
        return lax.dynamic_slice_in_dim(row_block, chip * (D // N_CHIPS), D // N_CHIPS, axis=1)

    gw_gu, gw_dn = _rs_end(rs_b, [rs_a[-1]])
    res = {
        "w_gate_up": step_halves("w_gate_up", w_gate_up, gw_gu, m_w_gate_up, v_w_gate_up, [rs_a[-1]]),
        "w_down": step_halves("w_down", w_down, gw_dn, m_w_down, v_w_down, [rs_a[-1]]),
    }
    rs_a = _rs_scatter(rs_a, [res["w_gate_up"][1], res["w_down"][1]], chip, ci)
    grad_x, sm_in = _bwd_in(dproj, xs, dx1, vecs, wg_in(), sm_mix, TS_MM, deps=[rs_a[-1]])
    rs_a = _rs_share(rs_a, [grad_x])

    small, per_dev = _sum_small(_allgather8(sm_in, "gather_small", deps=[rs_a[-1]]))
    dmod_all = per_dev[:, 0:6, :].reshape(N_DEV, 6 * D)
    grad_w_ada = _ada_bwd(c_act, lax.dynamic_slice_in_dim(dmod_all, chip * n_ada, n_ada, axis=1))
    grad_b_ada = small[0:6].reshape(1, 6 * D)
    res["w_ada"] = step("w_ada", w_ada, grad_w_ada[None], m_w_ada, v_w_ada)
    small_sets = {
        "b_ada": (b_ada.reshape(6, D), grad_b_ada.reshape(6, D), m_b_ada.reshape(6, D), v_b_ada.reshape(6, D)),
        "g_norm_mix": (g_norm_mix, small[G_GMIX:G_GMIX + 1], m_g_norm_mix, v_g_norm_mix),
        "conv_a_w": (conv_a_w[0], shard_cols(small[G_WA0:G_WA0 + 3]), m_conv_a_w[0], v_conv_a_w[0]),
        "conv_b_w": (conv_b_w[0], shard_cols(small[G_WB0:G_WB0 + 4]), m_conv_b_w[0], v_conv_b_w[0]),
        "conv_b_bias": (conv_b_bias, small[G_CBB:G_CBB + 1], m_conv_b_bias, v_conv_b_bias),
        "b_rg_a": (b_rg_a, small[G_BA:G_BA + 1], m_b_rg_a, v_b_rg_a),
        "b_rg_x": (b_rg_x, small[G_BX:G_BX + 1], m_b_rg_x, v_b_rg_x),
        "lru_lambda": (lru_lambda, small[G_LAM:G_LAM + 1], m_lru_lambda, v_lru_lambda),
        "g_norm_ffn": (g_norm_ffn, small[G_GFFN:G_GFFN + 1], m_g_norm_ffn, v_g_norm_ffn),
        "g_norm_final": (g_norm_final.reshape(1, D), small[G_GFIN:G_GFIN + 1], m_g_norm_final.reshape(1, D),
                         v_g_norm_final.reshape(1, D)),
    }
    stepped = _adamw_small(list(small_sets.values()), "adamw_small")
    for (n, (w_, g_, _, _)), (d_, nm_, nv_) in zip(small_sets.items(), stepped):
        shape = (1,) + w_.shape if n.startswith("conv_") and n != "conv_b_bias" else w_.shape
        res[n] = tuple(a.reshape(shape) for a in (g_, d_, nm_, nv_))
    gw_in, gw_rga, gw_rgx, gw_out = _rs_end(rs_a, [res[n][1] for n in res])
    res["w_in"] = step_halves("w_in", w_in, gw_in, m_w_in, v_w_in)
    res["w_rg_a"] = step_halves("w_rg_a", w_rg_a, gw_rga, m_w_rg_a, v_w_rg_a)
    res["w_rg_x"] = step_halves("w_rg_x", w_rg_x, gw_rgx, m_w_rg_x, v_w_rg_x)
    res["w_out"] = step_halves("w_out", w_out, gw_out, m_w_out, v_w_out)
    res["b_ada"] = tuple(a.reshape(1, 6 * D) for a in res["b_ada"])
    res["g_norm_final"] = tuple(a.reshape(D) for a in res["g_norm_final"])
    names = ["w_ada", "b_ada", "g_norm_mix", "w_in", "conv_a_w", "conv_b_w", "conv_b_bias", "w_rg_a", "b_rg_a", "w_rg_x",
             "b_rg_x", "lru_lambda", "w_out", "g_norm_ffn", "w_gate_up", "w_down", "g_norm_final"]
    loss = jnp.sum(small[G_LOSS])
    return (loss, from_blocks(grad_x)[None], *[res[n][0] for n in names], *[res[n][1] for n in names],
            *[res[n][2] for n in names], *[res[n][3] for n in names])
```

```python
import functools

import jax
import jax.numpy as jnp
from jax import lax
from jax.experimental import pallas as pl
from jax.experimental.pallas import tpu as pltpu

F32 = jnp.float32
BF16 = jnp.bfloat16
MESH = pl.DeviceIdType.MESH

D = 1024
N_CHIPS = 4
N_DEV = 8
D_IN = 7 * D
C_IN = D_IN // N_CHIPS
D_FF = 2816
C_GU = 2 * D_FF // N_CHIPS
HEADS = 4
HB = D // HEADS
EPS = 1e-6
LRU_C = 8.0
ADAM_LR, ADAM_B1, ADAM_B2, ADAM_EPS, ADAM_WD, ADAM_STEP = 0.001, 0.9, 0.999, 1e-08, 0.01, 10
VMEM_LIMIT = 56 << 20

(V_SH1, V_SC1, V_GT1, V_SH2, V_SC2, V_GT2, V_GMIX, V_GFFN, V_GFIN, V_CBB, V_BA, V_BX, V_LAM,
 V_WA0, V_WA1, V_WA2, V_WB0, V_WB1, V_WB2, V_WB3) = range(20)
N_VEC = 24
(G_SH1, G_SC1, G_GT1, G_SH2, G_SC2, G_GT2, G_GMIX, G_CBB, G_BA, G_BX, G_LAM, G_GFFN, G_GFIN,
 G_WA0, G_WA1, G_WA2, G_WB0, G_WB1, G_WB2, G_WB3, G_LOSS) = range(21)
N_SMALL = 24

_VMEM = pl.BlockSpec(memory_space=pltpu.VMEM)
_ANY = pl.BlockSpec(memory_space=pl.ANY)


def _cparams(n_grid=1):
    return pltpu.CompilerParams(dimension_semantics=("arbitrary",) * n_grid, vmem_limit_bytes=VMEM_LIMIT)


def _after(deps, body):
    n = len(deps)
    return lambda *refs: body(*refs[n:])


def _rms(x):
    rstd = lax.rsqrt(jnp.mean(x * x, axis=-1, keepdims=True) + EPS)
    return x * rstd, rstd


def _rms_bwd(dxhat, xhat, rstd):
    return rstd * (dxhat - xhat * jnp.mean(dxhat * xhat, axis=-1, keepdims=True))


def _rowsum(v):
    return jnp.sum(v, axis=0, keepdims=True)


def _dot(a, b):
    return jnp.dot(a, b, preferred_element_type=F32)


def _dot_nt(a, b):
    return lax.dot_general(a, b, (((1,), (1,)), ((), ())), preferred_element_type=F32)


def _dot_tn(a, b):
    return lax.dot_general(a, b, (((0,), (0,)), ((), ())), preferred_element_type=F32)


def _gelu(x):
    k, c = 0.7978845608028654, 0.044715
    t = jnp.tanh(k * (x + c * x * x * x))
    return 0.5 * x * (1.0 + t), 0.5 * (1.0 + t) + 0.5 * x * (1.0 - t * t) * k * (1.0 + 3.0 * c * x * x)


def _log_sigmoid(lam):
    return jnp.minimum(lam, 0.0) - jnp.log1p(jnp.exp(-jnp.abs(lam)))


def _lru_gates(u, wa_ref, wx_ref, v_ref, row0):
    ub = u.astype(BF16)
    pre_a = jnp.concatenate([_dot(ub[:, h * HB:(h + 1) * HB], wa_ref[h]) for h in range(HEADS)], axis=1)
    pre_x = jnp.concatenate([_dot(ub[:, h * HB:(h + 1) * HB], wx_ref[h]) for h in range(HEADS)], axis=1)
    r = jax.nn.sigmoid(pre_a + v_ref[V_BA:V_BA + 1, :])
    ig = jax.nn.sigmoid(pre_x + v_ref[V_BX:V_BX + 1, :])
    log_a = LRU_C * r * _log_sigmoid(v_ref[V_LAM:V_LAM + 1, :])
    a = jnp.exp(log_a)
    x2 = 2.0 * log_a
    m2 = jnp.where(x2 > -0.03, -x2 * (1.0 + x2 * (0.5 + x2 * (1.0 / 6.0 + x2 * (1.0 / 24.0)))), 1.0 - a * a)
    mult = jnp.where(row0, 1.0, jnp.sqrt(jnp.maximum(m2, 0.0)))
    return r, ig, a, mult


TIME_BLOCKS = 8


def _late_blocks(v, buf, g, halo=None):
    n = buf.shape[0]
    out = []
    for idx in range(n):
        k = TIME_BLOCKS - n + idx
        buf[idx, 8:g + 8, :] = v[k * g:(k + 1) * g]
        if halo is not None:
            buf[idx, 7:8, :] = halo[idx]
        out.append(buf[idx, pl.ds(7, g), :])
        if halo is None:
            buf[idx, 7:8, :] = buf[idx, g + 7:g + 8, :]
    return out


def _earlier(v, s, late, g):
    return jnp.concatenate(late[len(late) - s:] + [v[0:(TIME_BLOCKS - s) * g]], axis=0)


def _early_blocks(v, buf, g):
    out = []
    for k in range(buf.shape[0]):
        buf[k, 0:g, :] = v[k * g:(k + 1) * g]
        out.append(buf[k, pl.ds(1, g), :])
        buf[k, g:g + 1, :] = buf[k, 0:1, :]
    return out


def _later(v, s, early, g):
    return jnp.concatenate([v[s * g:]] + early[0:s], axis=0)


def _fwd_in_first(x, vecs, w_in_g, q_idx, ts, deps=()):
    s = x.shape[0]

    def body(q_ref, x_ref, v_ref, w_ref, h1_ref, proj_ref):
        xhat, _ = _rms(x_ref[...])
        h = xhat * v_ref[V_GMIX:V_GMIX + 1, :] * (1.0 + v_ref[V_SC1:V_SC1 + 1, :]) + v_ref[V_SH1:V_SH1 + 1, :]
        hb = h.astype(BF16)
        h1_ref[...] = hb
        proj_ref[...] = _dot(hb, w_ref[...]).astype(BF16)

    return pl.pallas_call(
        lambda q_ref, *refs: body(q_ref, *refs[len(deps):]),
        grid_spec=pltpu.PrefetchScalarGridSpec(
            num_scalar_prefetch=1, grid=(s // ts,),
            in_specs=[_ANY] * len(deps) + [pl.BlockSpec((ts, D), lambda i, q: (i, 0)), _VMEM,
                                           pl.BlockSpec((None, D, C_IN), lambda i, q: (q[0], 0, 0))],
            out_specs=[pl.BlockSpec((ts, D), lambda i, q: (i, 0)), pl.BlockSpec((ts, C_IN), lambda i, q: (i, q[0]))]),
        out_shape=(jax.ShapeDtypeStruct((s, D), BF16), jax.ShapeDtypeStruct((s, D_IN), BF16)),
        compiler_params=_cparams(), name="fwd_in_own")(q_idx, *deps, x, vecs, w_in_g)


def _fwd_in_more(h1, w_in_g, proj, q_idx, ts, name, deps=()):
    s = h1.shape[0]

    def body(q_ref, h1_ref, w_ref, proj_in_ref, proj_ref):
        proj_ref[...] = _dot(h1_ref[...], w_ref[...]).astype(BF16)

    return pl.pallas_call(
        lambda q_ref, *refs: body(q_ref, *refs[len(deps):]),
        grid_spec=pltpu.PrefetchScalarGridSpec(
            num_scalar_prefetch=1, grid=(s // ts,),
            in_specs=[_ANY] * len(deps) + [pl.BlockSpec((ts, D), lambda i, q: (i, 0)),
                                           pl.BlockSpec((None, D, C_IN), lambda i, q: (q[0], 0, 0)), _ANY],
            out_specs=pl.BlockSpec((ts, C_IN), lambda i, q: (i, q[0]))),
        out_shape=jax.ShapeDtypeStruct((s, D_IN), BF16), input_output_aliases={len(deps) + 3: 0},
        compiler_params=_cparams(), name=name)(q_idx, *deps, h1, w_in_g, proj)


def _fwd_mix(proj, x, vecs, w_rga, w_rgx, w_out, ts, deps=()):
    s = x.shape[0]
    g = ts // TIME_BLOCKS

    def body(proj_ref, x_ref, v_ref, wa_ref, wx_ref, wo_ref, x1_ref, mg_ref, z1_ref, u_ref, h_ref,
             ua_buf, rx_buf, p_buf, q_buf, c_buf, hcarry):
        i = pl.program_id(0)

        @pl.when(i == 0)
        def _():
            ua_buf[...] = jnp.zeros(ua_buf.shape, F32)
            rx_buf[...] = jnp.zeros(rx_buf.shape, F32)
            hcarry[...] = jnp.zeros((8, D), F32)

        def seg(j):
            return proj_ref[:, j * D:(j + 1) * D].astype(F32)

        def vrow(j):
            return v_ref[j:j + 1, :]

        cb, cc, cx, rx, rg, ga, gb = (seg(j) for j in range(7))
        ua = cc * cx
        ua_late = _late_blocks(ua, ua_buf, g)
        rx_late = _late_blocks(rx, rx_buf, g)
        va = vrow(V_WA2) * ua + vrow(V_WA1) * _earlier(ua, 1, ua_late, g) + vrow(V_WA0) * _earlier(ua, 2, ua_late, g)
        u = (vrow(V_WB3) * rx + vrow(V_WB2) * _earlier(rx, 1, rx_late, g) + vrow(V_WB1) * _earlier(rx, 2, rx_late, g)
             + vrow(V_WB0) * _earlier(rx, 3, rx_late, g) + vrow(V_CBB))
        u_ref[...] = u

        rows = lax.broadcasted_iota(jnp.int32, (ts, D), 0)
        row0 = jnp.logical_and(rows == 0, i == 0)
        _, ig, a, mult = _lru_gates(u, wa_ref, wx_ref, v_ref, row0)
        bx = mult * (ig * u)

        prods, sums = [a[0:g]], [bx[0:g]]
        for k in range(1, TIME_BLOCKS):
            ak = a[k * g:(k + 1) * g]
            sums.append(ak * sums[-1] + bx[k * g:(k + 1) * g])
            prods.append(ak * prods[-1])
        p_buf[...] = prods[-1]
        q_buf[...] = sums[-1]
        state = hcarry[0:1, :]
        for j in range(g):
            c_buf[j:j + 1, :] = state
            state = p_buf[j:j + 1, :] * state + q_buf[j:j + 1, :]
        hcarry[0:1, :] = state
        entering = c_buf[...]
        h = jnp.concatenate([sums[k] + prods[k] * entering for k in range(TIME_BLOCKS)], axis=0)
        h_ref[...] = h

        gel, _ = _gelu(rg)
        merged = (jax.nn.sigmoid(ga) * (cb * va) + jax.nn.sigmoid(gb) * (h * gel)).astype(BF16)
        mg_ref[...] = merged
        z1 = _dot(merged, wo_ref[...])
        z1_ref[...] = z1.astype(BF16)
        x1_ref[...] = x_ref[...] + vrow(V_GT1) * z1

    row = lambda i: (i, 0)
    return pl.pallas_call(
        _after(deps, body), grid=(s // ts,),
        out_shape=(jax.ShapeDtypeStruct((s, D), F32), jax.ShapeDtypeStruct((s, D), BF16), jax.ShapeDtypeStruct((s, D), BF16),
                   jax.ShapeDtypeStruct((s, D), F32), jax.ShapeDtypeStruct((s, D), F32)),
        in_specs=[_ANY] * len(deps) + [pl.BlockSpec((ts, D_IN), row), pl.BlockSpec((ts, D), row), _VMEM, _VMEM, _VMEM, _VMEM],
        out_specs=[pl.BlockSpec((ts, D), row)] * 5,
        scratch_shapes=[pltpu.VMEM((2, g + 8, D), F32), pltpu.VMEM((3, g + 8, D), F32), pltpu.VMEM((g, D), F32),
                        pltpu.VMEM((g, D), F32), pltpu.VMEM((g, D), F32), pltpu.VMEM((8, D), F32)],
        compiler_params=_cparams(), name="fwd_mix")(*deps, proj, x, vecs, w_rga, w_rgx, w_out)


def _ffn_loss(x1, target, vecs, w_gu_g, w_dn, ts):
    s = x1.shape[0]

    def body(x1_ref, t_ref, v_ref, wgu_ref, wdn_ref, dx1_ref, h2_ref, act_ref, dz2_ref, dgu_ref, sm_ref):
        @pl.when(pl.program_id(0) == 0)
        def _():
            sm_ref[...] = jnp.zeros((N_SMALL, D), F32)

        def vrow(j):
            return v_ref[j:j + 1, :]

        def acc(j, val):
            sm_ref[j:j + 1, :] += _rowsum(val)

        x1 = x1_ref[...]
        xh1, rstd1 = _rms(x1)
        n2 = xh1 * vrow(V_GFFN)
        h2 = (n2 * (1.0 + vrow(V_SC2)) + vrow(V_SH2)).astype(BF16)
        h2_ref[...] = h2
        g = jnp.concatenate([_dot(h2, wgu_ref[0]), _dot(h2, wgu_ref[1])], axis=1)
        up = jnp.concatenate([_dot(h2, wgu_ref[2]), _dot(h2, wgu_ref[3])], axis=1)
        sg = jax.nn.sigmoid(g)
        silu = g * sg
        act = (silu * up).astype(BF16)
        act_ref[...] = act
        z2 = _dot(act, wdn_ref[...])
        x2 = x1 + vrow(V_GT2) * z2
        xh2, rstd2 = _rms(x2)
        err = xh2 * vrow(V_GFIN) - t_ref[...]
        acc(G_LOSS, (0.5 / D) * err * err)
        dy = err * (1.0 / D)
        acc(G_GFIN, dy * xh2)
        dx2 = _rms_bwd(dy * vrow(V_GFIN), xh2, rstd2)
        acc(G_GT2, dx2 * z2)
        dz2 = (vrow(V_GT2) * dx2).astype(BF16)
        dz2_ref[...] = dz2
        dact = _dot_nt(dz2, wdn_ref[...])
        dgate = (dact * up * (sg * (1.0 + g * (1.0 - sg)))).astype(BF16)
        dup = (dact * silu).astype(BF16)
        dgu_ref[:, 0:D_FF] = dgate
        dgu_ref[:, D_FF:2 * D_FF] = dup
        dh2 = (_dot_nt(dgate[:, 0:C_GU], wgu_ref[0]) + _dot_nt(dgate[:, C_GU:2 * C_GU], wgu_ref[1])
               + _dot_nt(dup[:, 0:C_GU], wgu_ref[2]) + _dot_nt(dup[:, C_GU:2 * C_GU], wgu_ref[3]))
        acc(G_SH2, dh2)
        acc(G_SC2, dh2 * n2)
        dn2 = dh2 * (1.0 + vrow(V_SC2))
        acc(G_GFFN, dn2 * xh1)
        dx1_ref[...] = dx2 + _rms_bwd(dn2 * vrow(V_GFFN), xh1, rstd1)

    row = lambda i: (i, 0)
    return pl.pallas_call(
        body, grid=(s // ts,),
        out_shape=(jax.ShapeDtypeStruct((s, D), F32), jax.ShapeDtypeStruct((s, D), BF16), jax.ShapeDtypeStruct((s, D_FF), BF16),
                   jax.ShapeDtypeStruct((s, D), BF16), jax.ShapeDtypeStruct((s, 2 * D_FF), BF16),
                   jax.ShapeDtypeStruct((N_SMALL, D), F32)),
        in_specs=[pl.BlockSpec((ts, D), row), pl.BlockSpec((ts, D), row), _VMEM, _VMEM, _VMEM],
        out_specs=[pl.BlockSpec((ts, D), row), pl.BlockSpec((ts, D), row), pl.BlockSpec((ts, D_FF), row),
                   pl.BlockSpec((ts, D), row), pl.BlockSpec((ts, 2 * D_FF), row), pl.BlockSpec((N_SMALL, D), lambda i: (0, 0))],
        compiler_params=_cparams(), name="ffn_loss")(x1, target, vecs, w_gu_g, w_dn)


def _bwd_mix(dx1, z1, merged, proj, u, h, vecs, w_rga, w_rgx, w_out, small, ts, deps=()):
    s = dx1.shape[0]
    nt = s // ts
    g = ts // TIME_BLOCKS
    assert g % 16 == 0

    def body(dx1_ref, z1_ref, mg_ref, proj_ref, u_ref, h_ref, hh_ref, cc6_ref, cx6_ref, cc7_ref, cx7_ref, v_ref, wa_ref, wx_ref,
             wo_ref, sm0_ref, dproj_ref, sm_ref, dwa_ref, dwx_ref, dwo_ref,
             ua_buf, h_buf, a_buf, dva_buf, du_buf, p_buf, q_buf, c_buf, lcarry):
        i = pl.program_id(0)
        first_tile = i == nt - 1

        @pl.when(i == 0)
        def _():
            a_buf[...] = jnp.zeros(a_buf.shape, F32)
            dva_buf[...] = jnp.zeros(dva_buf.shape, F32)
            du_buf[...] = jnp.zeros(du_buf.shape, F32)
            lcarry[...] = jnp.zeros((8, D), F32)
            sm_ref[...] = sm0_ref[...]
            dwa_ref[...] = jnp.zeros((HEADS, HB, HB), F32)
            dwx_ref[...] = jnp.zeros((HEADS, HB, HB), F32)
            dwo_ref[...] = jnp.zeros((D, D), F32)

        def seg(j):
            return proj_ref[:, j * D:(j + 1) * D].astype(F32)

        def vrow(j):
            return v_ref[j:j + 1, :]

        def acc(j, val):
            sm_ref[j:j + 1, :] += _rowsum(val)

        cb, cc, cx, rx, rg, ga, gb = (seg(j) for j in range(7))
        ua = cc * cx

        def last_row(v):
            pick = lax.broadcasted_iota(jnp.int32, v.shape, 0) == v.shape[0] - 1
            return jnp.where(first_tile, 0.0, jnp.sum(jnp.where(pick, v, 0.0), axis=0, keepdims=True))

        ua_halo = [last_row(cc6_ref[...].astype(F32) * cx6_ref[...].astype(F32)),
                   last_row(cc7_ref[...].astype(F32) * cx7_ref[...].astype(F32))]
        ua_late = _late_blocks(ua, ua_buf, g, ua_halo)
        va = vrow(V_WA2) * ua + vrow(V_WA1) * _earlier(ua, 1, ua_late, g) + vrow(V_WA0) * _earlier(ua, 2, ua_late, g)
        u = u_ref[...]
        h = h_ref[...]
        rows = lax.broadcasted_iota(jnp.int32, (ts, D), 0)
        row0 = jnp.logical_and(rows == 0, first_tile)
        r, ig, a, mult = _lru_gates(u, wa_ref, wx_ref, v_ref, row0)
        sga = jax.nn.sigmoid(ga)
        sgb = jax.nn.sigmoid(gb)
        gel, dgel = _gelu(rg)

        dx1 = dx1_ref[...]
        acc(G_GT1, dx1 * z1_ref[...].astype(F32))
        dz1 = (vrow(V_GT1) * dx1).astype(BF16)
        dwo_ref[...] += _dot_tn(mg_ref[...], dz1)
        dmg = _dot_nt(dz1, wo_ref[...])
        dproj_ref[:, 5 * D:6 * D] = (dmg * (cb * va) * sga * (1.0 - sga)).astype(BF16)
        dproj_ref[:, 6 * D:7 * D] = (dmg * (h * gel) * sgb * (1.0 - sgb)).astype(BF16)
        dya = dmg * sga
        dyb = dmg * sgb

        dproj_ref[:, 0:D] = (dya * va).astype(BF16)
        dva = dya * cb
        dva_early = _early_blocks(dva, dva_buf, g)
        dva1 = _later(dva, 1, dva_early, g)
        dva2 = _later(dva, 2, dva_early, g)
        dua = vrow(V_WA2) * dva + vrow(V_WA1) * dva1 + vrow(V_WA0) * dva2
        acc(G_WA2, ua * dva)
        acc(G_WA1, ua * dva1)
        acc(G_WA0, ua * dva2)
        dproj_ref[:, D:2 * D] = (dua * cx).astype(BF16)
        dproj_ref[:, 2 * D:3 * D] = (dua * cc).astype(BF16)

        dproj_ref[:, 4 * D:5 * D] = (dyb * h * dgel).astype(BF16)
        a_next = _later(a, 1, _early_blocks(a, a_buf, g), g)
        dh = dyb * gel
        last = TIME_BLOCKS - 1
        prods, sums = {last: a_next[last * g:]}, {last: dh[last * g:]}
        for k in range(last - 1, -1, -1):
            ak = a_next[k * g:(k + 1) * g]
            sums[k] = dh[k * g:(k + 1) * g] + ak * sums[k + 1]
            prods[k] = ak * prods[k + 1]
        p_buf[...] = prods[0]
        q_buf[...] = sums[0]
        state = lcarry[0:1, :]
        for j in range(g - 1, -1, -1):
            c_buf[j:j + 1, :] = state
            state = q_buf[j:j + 1, :] + p_buf[j:j + 1, :] * state
        lcarry[0:1, :] = state
        entering = c_buf[...]
        lam = jnp.concatenate([sums[k] + prods[k] * entering for k in range(TIME_BLOCKS)], axis=0)

        h_halo = [jnp.where(first_tile, 0.0, hh_ref[7:8, :])]
        da = lam * _earlier(h, 1, _late_blocks(h, h_buf, g, h_halo), g)
        dmult = jnp.where(row0, 0.0, lam * (ig * u))
        di = lam * mult * u
        du = lam * mult * ig
        dlog_a = da * a - dmult * (a * a) / mult
        lam_p = vrow(V_LAM)
        dr = dlog_a * (LRU_C * _log_sigmoid(lam_p))
        sm_ref[G_LAM:G_LAM + 1, :] += _rowsum(dlog_a * r) * (LRU_C * jax.nn.sigmoid(-lam_p))
        dpa = dr * r * (1.0 - r)
        dpx = di * ig * (1.0 - ig)
        acc(G_BA, dpa)
        acc(G_BX, dpx)
        dpab = dpa.astype(BF16)
        dpxb = dpx.astype(BF16)
        ub = u.astype(BF16)
        back = []
        for hd in range(HEADS):
            cols = slice(hd * HB, (hd + 1) * HB)
            back.append(_dot_nt(dpab[:, cols], wa_ref[hd]) + _dot_nt(dpxb[:, cols], wx_ref[hd]))
            dwa_ref[hd] += _dot_tn(ub[:, cols], dpab[:, cols])
            dwx_ref[hd] += _dot_tn(ub[:, cols], dpxb[:, cols])
        du = du + jnp.concatenate(back, axis=1)

        acc(G_CBB, du)
        du_early = _early_blocks(du, du_buf, g)
        du1 = _later(du, 1, du_early, g)
        du2 = _later(du, 2, du_early, g)
        du3 = _later(du, 3, du_early, g)
        dproj_ref[:, 3 * D:4 * D] = (vrow(V_WB3) * du + vrow(V_WB2) * du1 + vrow(V_WB1) * du2 + vrow(V_WB0) * du3).astype(BF16)
        acc(G_WB3, rx * du)
        acc(G_WB2, rx * du1)
        acc(G_WB1, rx * du2)
        acc(G_WB0, rx * du3)

    rev = lambda i: (nt - 1 - i, 0)
    halo8 = lambda i: (jnp.maximum((nt - 1 - i) * (ts // 8) - 1, 0), 0)
    const2 = lambda i: (0, 0)
    const3 = lambda i: (0, 0, 0)

    def halo16(back, col):
        return pl.BlockSpec((16, D), lambda i: (jnp.maximum((nt - 1 - i) * (ts // 16) - back, 0), col))
    return pl.pallas_call(
        _after(deps, body), grid=(nt,),
        out_shape=(jax.ShapeDtypeStruct((s, D_IN), BF16), jax.ShapeDtypeStruct((N_SMALL, D), F32),
                   jax.ShapeDtypeStruct((HEADS, HB, HB), F32), jax.ShapeDtypeStruct((HEADS, HB, HB), F32),
                   jax.ShapeDtypeStruct((D, D), F32)),
        in_specs=[_ANY] * len(deps) + [pl.BlockSpec((ts, D), rev), pl.BlockSpec((ts, D), rev), pl.BlockSpec((ts, D), rev),
                  pl.BlockSpec((ts, D_IN), rev), pl.BlockSpec((ts, D), rev), pl.BlockSpec((ts, D), rev),
                  pl.BlockSpec((8, D), halo8),
                  halo16(1 + g // 16, 1), halo16(1 + g // 16, 2), halo16(1, 1), halo16(1, 2),
                  _VMEM, _VMEM, _VMEM, _VMEM, _VMEM],
        out_specs=[pl.BlockSpec((ts, D_IN), rev), pl.BlockSpec((N_SMALL, D), const2),
                   pl.BlockSpec((HEADS, HB, HB), const3), pl.BlockSpec((HEADS, HB, HB), const3), pl.BlockSpec((D, D), const2)],
        scratch_shapes=[pltpu.VMEM((2, g + 8, D), F32), pltpu.VMEM((1, g + 8, D), F32), pltpu.VMEM((1, g + 8, D), F32),
                        pltpu.VMEM((2, g + 8, D), F32), pltpu.VMEM((3, g + 8, D), F32), pltpu.VMEM((g, D), F32),
                        pltpu.VMEM((g, D), F32), pltpu.VMEM((g, D), F32), pltpu.VMEM((8, D), F32)],
        compiler_params=_cparams(), name="bwd_mix")(*deps, dx1, z1, merged, proj, u, h, h, proj, proj, proj, proj, vecs, w_rga,
                                                    w_rgx, w_out, small)


def _bwd_in(dproj, x, dx1, vecs, w_in_g, small, ts, deps=()):
    s = x.shape[0]

    def body(dp_ref, x_ref, dx1_ref, v_ref, w_ref, sm0_ref, gx_ref, sm_ref):
        @pl.when(pl.program_id(0) == 0)
        def _():
            sm_ref[...] = sm0_ref[...]

        def vrow(j):
            return v_ref[j:j + 1, :]

        dh1 = _dot_nt(dp_ref[:, 0:C_IN], w_ref[0])
        for k in range(1, N_CHIPS):
            dh1 += _dot_nt(dp_ref[:, k * C_IN:(k + 1) * C_IN], w_ref[k])
        xh, rstd = _rms(x_ref[...])
        sm_ref[G_SH1:G_SH1 + 1, :] += _rowsum(dh1)
        sm_ref[G_SC1:G_SC1 + 1, :] += _rowsum(dh1 * (xh * vrow(V_GMIX)))
        dn1 = dh1 * (1.0 + vrow(V_SC1))
        sm_ref[G_GMIX:G_GMIX + 1, :] += _rowsum(dn1 * xh)
        gx_ref[...] = dx1_ref[...] + _rms_bwd(dn1 * vrow(V_GMIX), xh, rstd)

    row = lambda i: (i, 0)
    return pl.pallas_call(
        _after(deps, body), grid=(s // ts,),
        out_shape=(jax.ShapeDtypeStruct((s, D), F32), jax.ShapeDtypeStruct((N_SMALL, D), F32)),
        in_specs=[_ANY] * len(deps) + [pl.BlockSpec((ts, D_IN), row), pl.BlockSpec((ts, D), row), pl.BlockSpec((ts, D), row),
                                       _VMEM, _VMEM, _VMEM],
        out_specs=[pl.BlockSpec((ts, D), row), pl.BlockSpec((N_SMALL, D), lambda i: (0, 0))],
        compiler_params=_cparams(), name="bwd_in")(*deps, dproj, x, dx1, vecs, w_in_g, small)


def _grad_w(a, b, n_col_blocks, ts, name, deps=()):
    s, m = a.shape
    tn = b.shape[1] // n_col_blocks
    n_steps = s // ts

    def body(a_ref, b_ref, o_ref, acc_ref):
        k = pl.program_id(1)

        @pl.when(k == 0)
        def _():
            acc_ref[...] = jnp.zeros((m, tn), F32)

        acc_ref[...] += _dot_tn(a_ref[...], b_ref[...])

        @pl.when(k == n_steps - 1)
        def _():
            o_ref[...] = acc_ref[...].astype(BF16)

    return pl.pallas_call(
        _after(deps, body), grid=(n_col_blocks, n_steps),
        out_shape=jax.ShapeDtypeStruct((n_col_blocks, m, tn), BF16),
        in_specs=[_ANY] * len(deps) + [pl.BlockSpec((ts, m), lambda n, k: (k, 0)), pl.BlockSpec((ts, tn), lambda n, k: (k, n))],
        out_specs=pl.BlockSpec((None, m, tn), lambda n, k: (n, 0, 0)),
        scratch_shapes=[pltpu.VMEM((m, tn), F32)],
        compiler_params=_cparams(2), name=name)(*deps, a, b)


def _ada_fwd(c_all, w_ada, b_ada):
    n = w_ada.shape[1]

    def body(c_ref, w_ref, b_ref, o_ref, ca_ref):
        c = c_ref[...]
        ca = c * jax.nn.sigmoid(c)
        ca_ref[...] = ca
        o_ref[...] = jnp.dot(ca, w_ref[...], preferred_element_type=F32, precision=lax.Precision.HIGHEST) + b_ref[...]

    return pl.pallas_call(
        body, out_shape=(jax.ShapeDtypeStruct((N_DEV, n), F32), jax.ShapeDtypeStruct((N_DEV, D), F32)),
        in_specs=[_VMEM] * 3, out_specs=[_VMEM] * 2, compiler_params=_cparams(0), name="ada_fwd")(c_all, w_ada, b_ada)


def _ada_bwd(c_act, dmod):
    n = dmod.shape[1]

    def body(c_ref, d_ref, o_ref):
        o_ref[...] = lax.dot_general(c_ref[...], d_ref[...], (((0,), (0,)), ((), ())), preferred_element_type=F32,
                                     precision=lax.Precision.HIGHEST)

    return pl.pallas_call(
        body, out_shape=jax.ShapeDtypeStruct((D, n), F32), in_specs=[_VMEM] * 2, out_specs=_VMEM,
        compiler_params=_cparams(0), name="ada_bwd")(c_act, dmod)


def _sum_small(parts):
    def body(p_ref, o_ref, d_ref):
        tot = p_ref[0]
        for dev in range(1, N_DEV):
            tot = tot + p_ref[dev]
        o_ref[...] = tot
        d_ref[...] = p_ref[:, 0:8, :]

    return pl.pallas_call(
        body, out_shape=(jax.ShapeDtypeStruct((N_SMALL, D), F32), jax.ShapeDtypeStruct((N_DEV, 8, D), F32)),
        in_specs=[_VMEM], out_specs=[_VMEM] * 2, compiler_params=_cparams(0), name="sum_small")(parts)


def _adamw(w, g, m, v, name, deps=()):
    rows, cols = w.shape
    tr = 128 if rows % 128 == 0 else (64 if rows % 64 == 0 else rows)

    def body(w_ref, g_ref, m_ref, v_ref, d_ref, nm_ref, nv_ref):
        g_ = g_ref[...]
        m_ = ADAM_B1 * m_ref[...] + (1.0 - ADAM_B1) * g_
        v_ = ADAM_B2 * v_ref[...] + (1.0 - ADAM_B2) * (g_ * g_)
        nm_ref[...] = m_
        nv_ref[...] = v_
        m_hat = m_ / (1.0 - ADAM_B1 ** ADAM_STEP)
        v_hat = v_ / (1.0 - ADAM_B2 ** ADAM_STEP)
        d_ref[...] = -ADAM_LR * (m_hat / (jnp.sqrt(v_hat) + ADAM_EPS) + ADAM_WD * w_ref[...])

    spec = pl.BlockSpec((tr, cols), lambda i: (i, 0))
    return pl.pallas_call(
        _after(deps, body), grid=(rows // tr,), out_shape=(jax.ShapeDtypeStruct((rows, cols), F32),) * 3,
        in_specs=[_ANY] * len(deps) + [spec] * 4, out_specs=[spec] * 3, compiler_params=_cparams(), name=name)(*deps, w, g, m, v)


def _adamw_small(items, name):
    n = len(items)

    def body(*refs):
        ins, outs = refs[:4 * n], refs[4 * n:]
        for k in range(n):
            w_ref, g_ref, m_ref, v_ref = ins[4 * k:4 * k + 4]
            d_ref, nm_ref, nv_ref = outs[3 * k:3 * k + 3]
            g_ = g_ref[...]
            m_ = ADAM_B1 * m_ref[...] + (1.0 - ADAM_B1) * g_
            v_ = ADAM_B2 * v_ref[...] + (1.0 - ADAM_B2) * (g_ * g_)
            nm_ref[...] = m_
            nv_ref[...] = v_
            m_hat = m_ / (1.0 - ADAM_B1 ** ADAM_STEP)
            v_hat = v_ / (1.0 - ADAM_B2 ** ADAM_STEP)
            d_ref[...] = -ADAM_LR * (m_hat / (jnp.sqrt(v_hat) + ADAM_EPS) + ADAM_WD * w_ref[...])

    out = pl.pallas_call(
        body, out_shape=tuple(jax.ShapeDtypeStruct(it[0].shape, F32) for it in items for _ in range(3)),
        in_specs=[_VMEM] * (4 * n), out_specs=[_VMEM] * (3 * n), name=name)(*[a for it in items for a in it])
    return [tuple(out[3 * k:3 * k + 3]) for k in range(n)]


def _adamw_halves(w, mine, other, m, v, c_idx, name, deps=()):
    r2, cols = mine.shape
    tr = next(t for t in (128, 64, 32, 16, 8) if r2 % t == 0)
    nh = r2 // tr

    def body(c_ref, w_ref, mine_ref, other_ref, m_ref, v_ref, g_ref, d_ref, nm_ref, nv_ref):
        g_ = jnp.where(pl.program_id(0) // nh == c_ref[0], mine_ref[...], other_ref[...])
        g_ref[...] = g_
        m_ = ADAM_B1 * m_ref[...] + (1.0 - ADAM_B1) * g_
        v_ = ADAM_B2 * v_ref[...] + (1.0 - ADAM_B2) * (g_ * g_)
        nm_ref[...] = m_
        nv_ref[...] = v_
        m_hat = m_ / (1.0 - ADAM_B1 ** ADAM_STEP)
        v_hat = v_ / (1.0 - ADAM_B2 ** ADAM_STEP)
        d_ref[...] = -ADAM_LR * (m_hat / (jnp.sqrt(v_hat) + ADAM_EPS) + ADAM_WD * w_ref[...])

    full = pl.BlockSpec((tr, cols), lambda i, c: (i, 0))
    mine_spec = pl.BlockSpec((tr, cols), lambda i, c: (jnp.clip(i - c[0] * nh, 0, nh - 1), 0))
    other_spec = pl.BlockSpec((tr, cols), lambda i, c: (jnp.clip(i - (1 - c[0]) * nh, 0, nh - 1), 0))
    return pl.pallas_call(
        lambda c_ref, *refs: body(c_ref, *refs[len(deps):]),
        grid_spec=pltpu.PrefetchScalarGridSpec(
            num_scalar_prefetch=1, grid=(2 * nh,),
            in_specs=[_ANY] * len(deps) + [full, mine_spec, other_spec, full, full], out_specs=[full] * 4),
        out_shape=(jax.ShapeDtypeStruct((2 * r2, cols), F32),) * 4, compiler_params=_cparams(), name=name,
    )(c_idx, *deps, w, mine, other, m, v)


def _add_halves(g, recv, c_idx, name):
    n, _, r2, cols = g.shape

    def body(c_ref, g_ref, r_ref, o_ref):
        o_ref[...] = (g_ref[...].astype(F32) + r_ref[...].astype(F32)).astype(BF16)

    return pl.pallas_call(
        body,
        grid_spec=pltpu.PrefetchScalarGridSpec(
            num_scalar_prefetch=1, grid=(n,),
            in_specs=[pl.BlockSpec((None, None, r2, cols), lambda k, c: (k, c[0], 0, 0)),
                      pl.BlockSpec((None, r2, cols), lambda k, c: (k, 0, 0))],
            out_specs=pl.BlockSpec((None, r2, cols), lambda k, c: (k, 0, 0))),
        out_shape=jax.ShapeDtypeStruct((n, r2, cols), BF16), compiler_params=_cparams(), name=name)(c_idx, g, recv)


def _sum_chips(parts, name):
    n, r2, cols = parts.shape
    tr = r2 // 2 if (r2 // 2) % 16 == 0 else r2

    def body(p_ref, o_ref):
        o_ref[...] = ((p_ref[0].astype(F32) + p_ref[1].astype(F32)) + p_ref[2].astype(F32)) + p_ref[3].astype(F32)

    return pl.pallas_call(
        body, grid=(r2 // tr,), out_shape=jax.ShapeDtypeStruct((r2, cols), F32),
        in_specs=[pl.BlockSpec((n, tr, cols), lambda i: (0, i, 0))], out_specs=pl.BlockSpec((tr, cols), lambda i: (i, 0)),
        compiler_params=_cparams(), name=name)(parts)


def _place():
    x, y, c = lax.axis_index("x"), lax.axis_index("y"), lax.axis_index("c")
    return x, y, c, 2 * x + y


def _flip(v, bit):
    return 1 - v if bit else v


def _allgather8(v, name, deps=()):
    r, n = v.shape

    def body(*refs):
        v_ref, out_ref, send_sems, recv_sems, local_sem = refs[len(deps):]
        x, y, c, _ = _place()
        me = 4 * x + 2 * y + c
        mine = pltpu.make_async_copy(v_ref, out_ref.at[me], local_sem)
        mine.start()
        sends = []
        for rel in range(1, N_DEV):
            peer = (_flip(x, rel & 4), _flip(y, rel & 2), _flip(c, rel & 1))
            cp = pltpu.make_async_remote_copy(v_ref, out_ref.at[me], send_sems.at[rel - 1], recv_sems.at[rel - 1],
                                              device_id=peer, device_id_type=MESH)
            cp.start()
            sends.append(cp)
        for rel in range(1, N_DEV):
            peer = (_flip(x, rel & 4), _flip(y, rel & 2), _flip(c, rel & 1))
            peer_idx = 4 * peer[0] + 2 * peer[1] + peer[2]
            pltpu.make_async_remote_copy(v_ref, out_ref.at[peer_idx], send_sems.at[rel - 1], recv_sems.at[rel - 1],
                                         device_id=peer, device_id_type=MESH).wait_recv()
        for cp in sends:
            cp.wait_send()
        mine.wait()

    return pl.pallas_call(
        body, out_shape=jax.ShapeDtypeStruct((N_DEV, r, n), F32), in_specs=[_ANY] * len(deps) + [_VMEM], out_specs=_VMEM,
        scratch_shapes=[pltpu.SemaphoreType.DMA((N_DEV - 1,)), pltpu.SemaphoreType.DMA((N_DEV - 1,)), pltpu.SemaphoreType.DMA(())],
        name=name)(*deps, v)


def _gather_weights(shards):
    nw = len(shards)

    def body(*refs):
        w_refs, out_refs = refs[:nw], refs[nw:2 * nw]
        send_sems, recv_sems = refs[2 * nw:]
        x, y, c, p = _place()
        sibling = (x, y, 1 - c)
        sends = []
        for j in range(1, N_CHIPS):
            peer = (_flip(x, j & 2), _flip(y, j & 1), c)
            for w in range(nw):
                cp = pltpu.make_async_remote_copy(w_refs[w].at[c], out_refs[w].at[p, c], send_sems.at[w * 6 + j - 1],
                                                  recv_sems.at[w * 6 + j - 1], device_id=peer, device_id_type=MESH)
                cp.start()
                sends.append(cp)
        for j in range(1, N_CHIPS):
            peer = (_flip(x, j & 2), _flip(y, j & 1), c)
            q = 2 * peer[0] + peer[1]
            for w in range(nw):
                pltpu.make_async_remote_copy(w_refs[w].at[c], out_refs[w].at[q, c], send_sems.at[w * 6 + j - 1],
                                             recv_sems.at[w * 6 + j - 1], device_id=peer, device_id_type=MESH).wait_recv()
                cp = pltpu.make_async_remote_copy(out_refs[w].at[q, c], out_refs[w].at[q, c], send_sems.at[w * 6 + 2 + j],
                                                  recv_sems.at[w * 6 + 2 + j], device_id=sibling, device_id_type=MESH)
                cp.start()
                sends.append(cp)
        for j in range(1, N_CHIPS):
            q = 2 * _flip(x, j & 2) + _flip(y, j & 1)
            for w in range(nw):
                pltpu.make_async_remote_copy(out_refs[w].at[q, 1 - c], out_refs[w].at[q, 1 - c], send_sems.at[w * 6 + 2 + j],
                                             recv_sems.at[w * 6 + 2 + j], device_id=sibling, device_id_type=MESH).wait_recv()
        for cp in sends:
            cp.wait_send()

    return pl.pallas_call(
        body, out_shape=tuple(jax.ShapeDtypeStruct((N_CHIPS,) + s.shape, s.dtype) for s in shards),
        in_specs=[_ANY] * nw, out_specs=[_ANY] * nw,
        scratch_shapes=[pltpu.SemaphoreType.DMA((6 * nw,)), pltpu.SemaphoreType.DMA((6 * nw,))],
        name="gather_weights")(*shards)


def _swap_halves(grads):
    nw = len(grads)

    def body(*refs):
        g_refs, out_refs = refs[:nw], refs[nw:2 * nw]
        send_sems, recv_sems = refs[2 * nw:]
        x, y, c, _ = _place()
        sibling = (x, y, 1 - c)
        sends = []
        for w in range(nw):
            for k in range(N_CHIPS):
                cp = pltpu.make_async_remote_copy(g_refs[w].at[k, 1 - c], out_refs[w].at[k], send_sems.at[w * N_CHIPS + k],
                                                  recv_sems.at[w * N_CHIPS + k], device_id=sibling, device_id_type=MESH)
                cp.start()
                sends.append(cp)
        for cp in sends:
            cp.wait_recv()
        for cp in sends:
            cp.wait_send()

    return pl.pallas_call(
        body, out_shape=tuple(jax.ShapeDtypeStruct((N_CHIPS,) + g.shape[2:], g.dtype) for g in grads),
        in_specs=[_ANY] * nw, out_specs=[_ANY] * nw,
        scratch_shapes=[pltpu.SemaphoreType.DMA((N_CHIPS * nw,)), pltpu.SemaphoreType.DMA((N_CHIPS * nw,))],
        name="swap_halves")(*grads)


def _scatter_chips(parts):
    nw = len(parts)

    def body(*refs):
        p_refs, out_refs = refs[:nw], refs[nw:2 * nw]
        send_sems, recv_sems = refs[2 * nw:]
        x, y, c, p = _place()
        sends = []
        for j in range(1, N_CHIPS):
            peer = (_flip(x, j & 2), _flip(y, j & 1), c)
            q = 2 * peer[0] + peer[1]
            for w in range(nw):
                cp = pltpu.make_async_remote_copy(p_refs[w].at[q], out_refs[w].at[p], send_sems.at[w * 3 + j - 1],
                                                  recv_sems.at[w * 3 + j - 1], device_id=peer, device_id_type=MESH)
                cp.start()
                sends.append(cp)
        for j in range(1, N_CHIPS):
            peer = (_flip(x, j & 2), _flip(y, j & 1), c)
            q = 2 * peer[0] + peer[1]
            for w in range(nw):
                pltpu.make_async_remote_copy(p_refs[w].at[q], out_refs[w].at[q], send_sems.at[w * 3 + j - 1],
                                             recv_sems.at[w * 3 + j - 1], device_id=peer, device_id_type=MESH).wait_recv()
        for cp in sends:
            cp.wait_send()

    return pl.pallas_call(
        body, out_shape=tuple(jax.ShapeDtypeStruct(s.shape, s.dtype) for s in parts),
        in_specs=[_ANY] * nw, out_specs=[_ANY] * nw,
        scratch_shapes=[pltpu.SemaphoreType.DMA((3 * nw,)), pltpu.SemaphoreType.DMA((3 * nw,))],
        name="scatter_chips")(*parts)


def _share_halves(halves):
    nw = len(halves)

    def body(*refs):
        h_refs, out_refs = refs[:nw], refs[nw:2 * nw]
        send_sems, recv_sems = refs[2 * nw:]
        x, y, c, _ = _place()
        sends = []
        for w in range(nw):
            cp = pltpu.make_async_remote_copy(h_refs[w], out_refs[w], send_sems.at[w], recv_sems.at[w],
                                              device_id=(x, y, 1 - c), device_id_type=MESH)
            cp.start()
            sends.append(cp)
        for cp in sends:
            cp.wait_recv()
        for cp in sends:
            cp.wait_send()

    return pl.pallas_call(
        body, out_shape=tuple(jax.ShapeDtypeStruct(s.shape, s.dtype) for s in halves),
        in_specs=[_ANY] * nw, out_specs=[_ANY] * nw,
        scratch_shapes=[pltpu.SemaphoreType.DMA((nw,)), pltpu.SemaphoreType.DMA((nw,))],
        name="share_halves")(*halves)


_HBM = pl.BlockSpec(memory_space=pltpu.HBM)
_SEM = pl.BlockSpec(memory_space=pltpu.SEMAPHORE)
_EFFECT = pltpu.SideEffectType.DATAFLOW_SIDE_EFFECTING


def _xchg_start(name, plan, n_copies, srcs, lands, after=()):
    bufs = list(srcs) + list(lands)
    ns, nb = len(srcs), len(srcs) + len(lands)

    def body(*refs):
        send_sems, recv_sems, token = refs[nb + len(after)], refs[nb + len(after) + 1], refs[-1]
        for i, (src, dst, peer, _) in enumerate(plan(_place(), refs[:ns], refs[ns:nb])):
            pltpu.make_async_remote_copy(src, dst, send_sems.at[i], recv_sems.at[i], device_id=peer, device_id_type=MESH).start()
        token[...] = jnp.zeros_like(token)

    out = pl.pallas_call(
        body, name=name,
        out_shape=(pltpu.SemaphoreType.DMA((n_copies,)), pltpu.SemaphoreType.DMA((n_copies,)),
                   *[pltpu.HBM(a.shape, a.dtype) for a in bufs], jax.ShapeDtypeStruct((8, 128), F32)),
        in_specs=[_HBM] * nb + [_ANY] * len(after), out_specs=(_SEM, _SEM, *[_HBM] * nb, _VMEM),
        input_output_aliases={i: 2 + i for i in range(nb)},
        compiler_params=pltpu.CompilerParams(has_side_effects=_EFFECT),
    )(*[pltpu.with_memory_space_constraint(a, pltpu.HBM) for a in bufs], *after)
    return (out[0], out[1]), out[2:2 + ns], out[2 + ns:2 + nb], out[-1]


def _xchg_wait(name, plan, sems, srcs, lands, after, sem_ids=None):
    bufs = list(srcs) + list(lands)
    ns, nb = len(srcs), len(srcs) + len(lands)

    def body(*refs):
        send_sems, recv_sems = refs[nb], refs[nb + 1]
        copies = plan(_place(), refs[:ns], refs[ns:nb])
        ids = range(len(copies)) if sem_ids is None else sem_ids
        for i, (src, _, peer, mine) in zip(ids, copies, strict=True):
            if i is not None:
                cp = pltpu.make_async_remote_copy(src, mine, send_sems.at[i], recv_sems.at[i], device_id=peer,
                                                  device_id_type=MESH)
                cp.wait_send()
                cp.wait_recv()

    out = pl.pallas_call(
        body, name=name, out_shape=tuple(pltpu.HBM(a.shape, a.dtype) for a in bufs),
        in_specs=[_HBM] * nb + [_SEM, _SEM] + [_ANY] * len(after), out_specs=tuple([_HBM] * nb),
        input_output_aliases={i: i for i in range(nb)},
        compiler_params=pltpu.CompilerParams(has_side_effects=_EFFECT),
    )(*bufs, *sems, *after)
    return out[:ns], out[ns:]


def _other_chips(place, which=(1, 2, 3)):
    x, y, c, _ = place
    return [((_flip(x, j & 2), _flip(y, j & 1), c), 2 * _flip(x, j & 2) + _flip(y, j & 1)) for j in which]


def _plan_gather_ici(place, src_refs, land_refs):
    _, _, c, p = place
    return [(s.at[c], l.at[p, c], peer, l.at[q, c]) for s, l in zip(src_refs, land_refs) for peer, q in _other_chips(place)]


def _plan_relay(which):
    def plan(place, src_refs, land_refs):
        x, y, c, _ = place
        return [(l.at[q, c], l.at[q, c], (x, y, 1 - c), l.at[q, 1 - c]) for l in land_refs for _, q in _other_chips(place, which)]
    return plan


def _plan_swap(place, src_refs, land_refs):
    x, y, c, _ = place
    return [(s.at[k, 1 - c], l.at[k], (x, y, 1 - c), l.at[k]) for s, l in zip(src_refs, land_refs) for k in range(N_CHIPS)]


def _plan_scatter(place, src_refs, land_refs):
    _, _, _, p = place
    return [(s.at[q], l.at[p], peer, l.at[q]) for s, l in zip(src_refs, land_refs) for peer, q in _other_chips(place)]


def _plan_share(place, src_refs, land_refs):
    x, y, c, _ = place
    return [(s, l, (x, y, 1 - c), l) for s, l in zip(src_refs, land_refs)]


def _pack_rows(parts, n_rows, name, deps=()):
    def body(*refs):
        refs = refs[len(deps):]
        out_ref = refs[-1]
        out_ref[...] = jnp.zeros((n_rows, D), F32)
        at = 0
        for ref in refs[:-1]:
            k = ref.shape[0]
            out_ref[at:at + k, :] = ref[...]
            at += k

    return pl.pallas_call(
        body, out_shape=jax.ShapeDtypeStruct((n_rows, D), F32), in_specs=[_ANY] * len(deps) + [_VMEM] * len(parts),
        out_specs=_VMEM, name=name)(*deps, *parts)


TS_MM = 512
TS_GW = 1024
TS_MIX = 256


def _halved(a):
    n, r, cols = a.shape
    return a.reshape(n, 2, r // 2, cols)


def _rs_swap(name, grads, after=()):
    lands = [lax.empty((N_CHIPS,) + g.shape[2:], g.dtype) for g in grads]
    sems, grads, lands, token = _xchg_start(name + "_swap", _plan_swap, N_CHIPS * len(grads), grads, lands, after)
    return name, sems, grads, lands, token


def _rs_scatter(handle, after, chip, ci):
    name, sems, grads, lands, _ = handle
    grads, from_sibling = _xchg_wait(name + "_swap_wait", _plan_swap, sems, grads, lands, after)
    c_arr = jnp.reshape(ci, (1,)).astype(jnp.int32)
    pair_sums = [_add_halves(g, r, c_arr, "%s_add_halves_%d" % (name, k)) for k, (g, r) in enumerate(zip(grads, from_sibling))]
    lands = [lax.dynamic_update_index_in_dim(lax.empty(p.shape, p.dtype), lax.dynamic_index_in_dim(p, chip, 0, keepdims=False),
                                             chip, 0) for p in pair_sums]
    sems, pair_sums, lands, token = _xchg_start(name + "_scatter", _plan_scatter, 3 * len(pair_sums), pair_sums, lands)
    return name, sems, pair_sums, lands, token


def _rs_share(handle, after):
    name, sems, pair_sums, lands, _ = handle
    _, by_chip = _xchg_wait(name + "_scatter_wait", _plan_scatter, sems, pair_sums, lands, after)
    halves = [_sum_chips(b, "%s_sum_chips_%d" % (name, k)) for k, b in enumerate(by_chip)]
    lands = [lax.empty(h.shape, h.dtype) for h in halves]
    sems, halves, lands, token = _xchg_start(name + "_share", _plan_share, len(halves), halves, lands)
    return name, sems, halves, lands, token


def _rs_end(handle, after):
    name, sems, halves, lands, _ = handle
    halves, others = _xchg_wait(name + "_share_wait", _plan_share, sems, halves, lands, after)
    return list(zip(halves, others))


def kernel(x, c, w_ada, b_ada, g_norm_mix, w_in, conv_a_w, conv_b_w, conv_b_bias, w_rg_a, b_rg_a, w_rg_x, b_rg_x, lru_lambda, w_out, g_norm_ffn, w_gate_up, w_down, g_norm_final, loss_target, m_w_ada, m_b_ada, m_g_norm_mix, m_w_in, m_conv_a_w, m_conv_b_w, m_conv_b_bias, m_w_rg_a, m_b_rg_a, m_w_rg_x, m_b_rg_x, m_lru_lambda, m_w_out, m_g_norm_ffn, m_w_gate_up, m_w_down, m_g_norm_final, v_w_ada, v_b_ada, v_g_norm_mix, v_w_in, v_conv_a_w, v_conv_b_w, v_conv_b_bias, v_w_rg_a, v_b_rg_a, v_w_rg_x, v_b_rg_x, v_lru_lambda, v_w_out, v_g_norm_ffn, v_w_gate_up, v_w_down, v_g_norm_final):
    xi, yi, ci = lax.axis_index("x"), lax.axis_index("y"), lax.axis_index("c")
    chip = 2 * xi + yi
    me = 2 * chip + ci
    n_ada = w_ada.shape[2]

    def widen(w):
        return jnp.pad(w, ((0, 0), (0, D - w.shape[1])))

    got = _allgather8(_pack_rows([c, widen(conv_a_w[0]), widen(conv_b_w[0])], 8, "pack_c_conv"), "gather_c_conv")
    c_all = got[:, 0, :]
    conv_full = got[::2, 1:8, :D // N_CHIPS].transpose(1, 0, 2).reshape(7, D)

    mod_part, c_act = _ada_fwd(c_all, w_ada[0], lax.dynamic_slice_in_dim(b_ada, chip * n_ada, n_ada, axis=1))
    mod_all = _allgather8(mod_part, "gather_mod")
    mod_mine = lax.dynamic_index_in_dim(mod_all, me, axis=1, keepdims=False)[::2].reshape(6, D)
    vecs = _pack_rows([mod_mine, g_norm_mix, g_norm_ffn, g_norm_final.reshape(1, D), conv_b_bias, b_rg_a, b_rg_x, lru_lambda,
                       conv_full], N_VEC, "pack_vecs")

    def rg_shard(w):
        return w[0].astype(BF16).reshape(2, HEADS * HB // N_CHIPS // 2, HB)

    shards = [w_in[0].astype(BF16).reshape(2, D // 2, C_IN), rg_shard(w_rg_a), rg_shard(w_rg_x),
              w_out[0].astype(BF16).reshape(2, D // N_CHIPS // 2, D), w_gate_up[0].astype(BF16).reshape(2, D // 2, C_GU),
              w_down[0].astype(BF16).reshape(2, D_FF // N_CHIPS // 2, D)]
    lands = [lax.dynamic_update_index_in_dim(lax.empty((N_CHIPS,) + s.shape, s.dtype), s, chip, 0) for s in shards]

    def send(name, first, last, after):
        sems, srcs, zone, token = _xchg_start(name + "_ici", _plan_gather_ici, 3 * (last - first), shards[first:last],
                                              lands[first:last], after)
        shards[first:last], lands[first:last] = srcs, zone
        return sems, first, token

    def arrive(name, sent, first, last, after, which=(1, 2, 3)):
        sems, base, _ = sent
        ids = [3 * (k - base) + j - 1 if j in which else None for k in range(first, last) for j in (1, 2, 3)]
        srcs, zone = _xchg_wait(name + "_ici_wait", _plan_gather_ici, sems, shards[first:last], lands[first:last], after, ids)
        shards[first:last], lands[first:last] = srcs, zone

    def relay(name, first, last, which):
        plan = _plan_relay(which)
        sems, _, zone, token = _xchg_start(name + "_d2d", plan, len(which) * (last - first), [], lands[first:last])
        lands[first:last] = zone
        return name, plan, sems, first, last, token

    def relayed(handle, after):
        name, plan, sems, first, last, _ = handle
        lands[first:last] = _xchg_wait(name + "_d2d_wait", plan, sems, [], lands[first:last], after)[1]

    def to_blocks(v):
        return v.reshape(-1, TS_MIX // TIME_BLOCKS, TIME_BLOCKS, D).transpose(0, 2, 1, 3).reshape(v.shape)

    def from_blocks(v):
        return v.reshape(-1, TIME_BLOCKS, TS_MIX // TIME_BLOCKS, D).transpose(0, 2, 1, 3).reshape(v.shape)

    def chip_index(j):
        return jnp.reshape(chip ^ j, (1,)).astype(jnp.int32)

    def wg_in():
        return lands[0].reshape(N_CHIPS, D, C_IN)

    xs, target = to_blocks(x[0]), to_blocks(loss_target[0])
    sent_in = send("gather_in", 0, 1, [vecs])
    h1, proj = _fwd_in_first(xs, vecs, wg_in(), chip_index(0), TS_MM, deps=[sent_in[-1]])
    arrive("gather_in_near", sent_in, 0, 1, [proj], (1, 2))
    near = relay("gather_in_near", 0, 1, (1, 2))
    sent_rest = send("gather_rest", 1, 6, [near[-1]])
    relayed(near, [sent_rest[-1]])
    proj = _fwd_in_more(h1, wg_in(), proj, chip_index(1), TS_MM, "fwd_in_y")
    proj = _fwd_in_more(h1, wg_in(), proj, chip_index(2), TS_MM, "fwd_in_x")
    arrive("gather_in_far", sent_in, 0, 1, [proj], (3,))
    far = relay("gather_in_far", 0, 1, (3,))
    arrive("gather_mix", sent_rest, 1, 4, [far[-1]])
    relayed(far, [far[-1]])
    mix = relay("gather_mix", 1, 4, (1, 2, 3))
    proj = _fwd_in_more(h1, wg_in(), proj, chip_index(3), TS_MM, "fwd_in_xy", deps=[mix[-1]])
    relayed(mix, [proj])
    wg_rga, wg_rgx, wg_out = lands[1:4]
    wg_out = wg_out.reshape(D, D)

    def rg_full(wg):
        return wg.reshape(N_CHIPS, HEADS, HB // N_CHIPS, HB).transpose(1, 0, 2, 3).reshape(HEADS, HB, HB)

    wg_rga, wg_rgx = rg_full(wg_rga), rg_full(wg_rgx)

    arrive("gather_ffn", sent_rest, 4, 6, [proj])
    ffn = relay("gather_ffn", 4, 6, (1, 2, 3))
    x1, merged, z1, u, h = _fwd_mix(proj, xs, vecs, wg_rga, wg_rgx, wg_out, TS_MIX, deps=[ffn[-1]])
    relayed(ffn, [x1])
    wg_gu, wg_dn = lands[4:6]
    wg_gu, wg_dn = wg_gu.reshape(N_CHIPS, D, C_GU), wg_dn.reshape(D_FF, D)
    dx1, h2, act, dz2, dgu, sm_ffn = _ffn_loss(x1, target, vecs, wg_gu, wg_dn, TS_MIX)

    def rg_chunks(dw):
        return _halved(dw.reshape(HEADS, N_CHIPS, HB // N_CHIPS, HB).transpose(1, 0, 2, 3).reshape(N_CHIPS, HB, HB).astype(BF16))

    ts_gw = min(TS_GW, xs.shape[0])
    g_dn = _grad_w(act, dz2, 1, ts_gw, "grad_w_down")
    g_gu = _grad_w(h2, dgu, N_CHIPS, ts_gw, "grad_w_gate_up")
    rs_b = _rs_swap("rs_b", [_halved(g_gu), _halved(g_dn.reshape(N_CHIPS, D_FF // N_CHIPS, D))])
    dproj, sm_mix, dw_rga, dw_rgx, dw_out = _bwd_mix(dx1, z1, merged, proj, u, h, vecs, wg_rga, wg_rgx, wg_out, sm_ffn, TS_MIX,
                                                     deps=[rs_b[-1]])
    rs_b = _rs_scatter(rs_b, [dproj], chip, ci)
    g_in = _grad_w(h1, dproj, N_CHIPS, ts_gw, "grad_w_in", deps=[rs_b[-1]])
    rs_b = _rs_share(rs_b, [g_in])
    rs_a = _rs_swap("rs_a", [_halved(g_in), rg_chunks(dw_rga), rg_chunks(dw_rgx),
                             _halved(dw_out.astype(BF16).reshape(N_CHIPS, D // N_CHIPS, D))], after=[rs_b[-1]])

    c_arr = jnp.reshape(ci, (1,)).astype(jnp.int32)

    def step(name, w, g, m, v, deps=()):
        shape = w.shape
        two_d = (-1, shape[-1])
        d, nm, nv = _adamw(w.reshape(two_d), g.reshape(two_d), m.reshape(two_d), v.reshape(two_d), "adamw_" + name, deps)
        return g.reshape(shape), d.reshape(shape), nm.reshape(shape), nv.reshape(shape)

    def step_halves(name, w, halves, m, v, deps=()):
        shape = w.shape
        two_d = (-1, shape[-1])
        out = _adamw_halves(w.reshape(two_d), halves[0], halves[1], m.reshape(two_d), v.reshape(two_d), c_arr, "adamw_" + name, deps)
        return tuple(a.reshape(shape) for a in out)

    def shard_cols(row_block):
        return lax.dynamic_slice_in_dim(row_block, chip * (D // N_CHIPS), D // N_CHIPS, axis=1)

    gw_gu, gw_dn = _rs_end(rs_b, [rs_a[-1]])
    res = {
        "w_gate_up": step_halves("w_gate_up", w_gate_up, gw_gu, m_w_gate_up, v_w_gate_up, [rs_a[-1]]),
        "w_down": step_halves("w_down", w_down, gw_dn, m_w_down, v_w_down, [rs_a[-1]]),
    }
    rs_a = _rs_scatter(rs_a, [res["w_gate_up"][1], res["w_down"][1]], chip, ci)
    grad_x, sm_in = _bwd_in(dproj, xs, dx1, vecs, wg_in(), sm_mix, TS_MM, deps=[rs_a[-1]])
    rs_a = _rs_share(rs_a, [grad_x])

    small, per_dev = _sum_small(_allgather8(sm_in, "gather_small", deps=[rs_a[-1]]))
    dmod_all = per_dev[:, 0:6, :].reshape(N_DEV, 6 * D)
    grad_w_ada = _ada_bwd(c_act, lax.dynamic_slice_in_dim(dmod_all, chip * n_ada, n_ada, axis=1))
    grad_b_ada = small[0:6].reshape(1, 6 * D)
    res["w_ada"] = step("w_ada", w_ada, grad_w_ada[None], m_w_ada, v_w_ada)
    small_sets = {
        "b_ada": (b_ada.reshape(6, D), grad_b_ada.reshape(6, D), m_b_ada.reshape(6, D), v_b_ada.reshape(6, D)),
        "g_norm_mix": (g_norm_mix, small[G_GMIX:G_GMIX + 1], m_g_norm_mix, v_g_norm_mix),
        "conv_a_w": (conv_a_w[0], shard_cols(small[G_WA0:G_WA0 + 3]), m_conv_a_w[0], v_conv_a_w[0]),
        "conv_b_w": (conv_b_w[0], shard_cols(small[G_WB0:G_WB0 + 4]), m_conv_b_w[0], v_conv_b_w[0]),
        "conv_b_bias": (conv_b_bias, small[G_CBB:G_CBB + 1], m_conv_b_bias, v_conv_b_bias),
        "b_rg_a": (b_rg_a, small[G_BA:G_BA + 1], m_b_rg_a, v_b_rg_a),
        "b_rg_x": (b_rg_x, small[G_BX:G_BX + 1], m_b_rg_x, v_b_rg_x),
        "lru_lambda": (lru_lambda, small[G_LAM:G_LAM + 1], m_lru_lambda, v_lru_lambda),
        "g_norm_ffn": (g_norm_ffn, small[G_GFFN:G_GFFN + 1], m_g_norm_ffn, v_g_norm_ffn),
        "g_norm_final": (g_norm_final.reshape(1, D), small[G_GFIN:G_GFIN + 1], m_g_norm_final.reshape(1, D),
                         v_g_norm_final.reshape(1, D)),
    }
    stepped = _adamw_small(list(small_sets.values()), "adamw_small")
    for (n, (w_, g_, _, _)), (d_, nm_, nv_) in zip(small_sets.items(), stepped):
        shape = (1,) + w_.shape if n.startswith("conv_") and n != "conv_b_bias" else w_.shape
        res[n] = tuple(a.reshape(shape) for a in (g_, d_, nm_, nv_))
    gw_in, gw_rga, gw_rgx, gw_out = _rs_end(rs_a, [res[n][1] for n in res])
    res["w_in"] = step_halves("w_in", w_in, gw_in, m_w_in, v_w_in)
    res["w_rg_a"] = step_halves("w_rg_a", w_rg_a, gw_rga, m_w_rg_a, v_w_rg_a)
    res["w_rg_x"] = step_halves("w_rg_x", w_rg_x, gw_rgx, m_w_rg_x, v_w_rg_x)
    res["w_out"] = step_halves("w_out", w_out, gw_out, m_w_out, v_w_out)
    res["b_ada"] = tuple(a.reshape(1, 6 * D) for a in res["b_ada"])
    res["g_norm_final"] = tuple(a.reshape(D) for a in res["g_norm_final"])
    names = ["w_ada", "b_ada", "g_norm_mix", "w_in", "conv_a_w", "conv_b_w", "conv_b_bias", "w_rg_a", "b_rg_a", "w_rg_x",
             "b_rg_x", "lru_lambda", "w_out", "g_norm_ffn", "w_gate_up", "w_down", "g_norm_final"]
    loss = jnp.sum(small[G_LOSS])
    return (loss, from_blocks(grad_x)[None], *[res[n][0] for n in names], *[res[n][1] for n in names],
            *[res[n][2] for n in names], *[res[n][3] for n in names])
```

```python
import functools

import jax
import jax.numpy as jnp
from jax import lax
from jax.experimental import pallas as pl
from jax.experimental.pallas import tpu as pltpu

F32 = jnp.float32
BF16 = jnp.bfloat16
MESH = pl.DeviceIdType.MESH

D = 1024
N_CHIPS = 4
N_DEV = 8
D_IN = 7 * D
C_IN = D_IN // N_CHIPS
D_FF = 2816
C_GU = 2 * D_FF // N_CHIPS
HEADS = 4
HB = D // HEADS
EPS = 1e-6
LRU_C = 8.0
ADAM_LR, ADAM_B1, ADAM_B2, ADAM_EPS, ADAM_WD, ADAM_STEP = 0.001, 0.9, 0.999, 1e-08, 0.01, 10
VMEM_LIMIT = 56 << 20

(V_SH1, V_SC1, V_GT1, V_SH2, V_SC2, V_GT2, V_GMIX, V_GFFN, V_GFIN, V_CBB, V_BA, V_BX, V_LAM,
 V_WA0, V_WA1, V_WA2, V_WB0, V_WB1, V_WB2, V_WB3) = range(20)
N_VEC = 24
(G_SH1, G_SC1, G_GT1, G_SH2, G_SC2, G_GT2, G_GMIX, G_CBB, G_BA, G_BX, G_LAM, G_GFFN, G_GFIN,
 G_WA0, G_WA1, G_WA2, G_WB0, G_WB1, G_WB2, G_WB3, G_LOSS) = range(21)
N_SMALL = 24

_VMEM = pl.BlockSpec(memory_space=pltpu.VMEM)
_ANY = pl.BlockSpec(memory_space=pl.ANY)


def _cparams(n_grid=1):
    return pltpu.CompilerParams(dimension_semantics=("arbitrary",) * n_grid, vmem_limit_bytes=VMEM_LIMIT)


def _after(deps, body):
    n = len(deps)
    return lambda *refs: body(*refs[n:])


def _rms(x):
    rstd = lax.rsqrt(jnp.mean(x * x, axis=-1, keepdims=True) + EPS)
    return x * rstd, rstd


def _rms_bwd(dxhat, xhat, rstd):
    return rstd * (dxhat - xhat * jnp.mean(dxhat * xhat, axis=-1, keepdims=True))


def _rowsum(v):
    return jnp.sum(v, axis=0, keepdims=True)


def _dot(a, b):
    return jnp.dot(a, b, preferred_element_type=F32)


def _dot_nt(a, b):
    return lax.dot_general(a, b, (((1,), (1,)), ((), ())), preferred_element_type=F32)


def _dot_tn(a, b):
    return lax.dot_general(a, b, (((0,), (0,)), ((), ())), preferred_element_type=F32)


def _gelu(x):
    k, c = 0.7978845608028654, 0.044715
    t = jnp.tanh(k * (x + c * x * x * x))
    return 0.5 * x * (1.0 + t), 0.5 * (1.0 + t) + 0.5 * x * (1.0 - t * t) * k * (1.0 + 3.0 * c * x * x)


def _log_sigmoid(lam):
    return jnp.minimum(lam, 0.0) - jnp.log1p(jnp.exp(-jnp.abs(lam)))


def _lru_gates(u, wa_ref, wx_ref, v_ref, row0):
    ub = u.astype(BF16)
    pre_a = jnp.concatenate([_dot(ub[:, h * HB:(h + 1) * HB], wa_ref[h]) for h in range(HEADS)], axis=1)
    pre_x = jnp.concatenate([_dot(ub[:, h * HB:(h + 1) * HB], wx_ref[h]) for h in range(HEADS)], axis=1)
    r = jax.nn.sigmoid(pre_a + v_ref[V_BA:V_BA + 1, :])
    ig = jax.nn.sigmoid(pre_x + v_ref[V_BX:V_BX + 1, :])
    log_a = LRU_C * r * _log_sigmoid(v_ref[V_LAM:V_LAM + 1, :])
    a = jnp.exp(log_a)
    x2 = 2.0 * log_a
    m2 = jnp.where(x2 > -0.03, -x2 * (1.0 + x2 * (0.5 + x2 * (1.0 / 6.0 + x2 * (1.0 / 24.0)))), 1.0 - a * a)
    mult = jnp.where(row0, 1.0, jnp.sqrt(jnp.maximum(m2, 0.0)))
    return r, ig, a, mult


TIME_BLOCKS = 8


def _late_blocks(v, buf, g, halo=None):
    n = buf.shape[0]
    out = []
    for idx in range(n):
        k = TIME_BLOCKS - n + idx
        buf[idx, 8:g + 8, :] = v[k * g:(k + 1) * g]
        if halo is not None:
            buf[idx, 7:8, :] = halo[idx]
        out.append(buf[idx, pl.ds(7, g), :])
        if halo is None:
            buf[idx, 7:8, :] = buf[idx, g + 7:g + 8, :]
    return out


def _earlier(v, s, late, g):
    return jnp.concatenate(late[len(late) - s:] + [v[0:(TIME_BLOCKS - s) * g]], axis=0)


def _early_blocks(v, buf, g):
    out = []
    for k in range(buf.shape[0]):
        buf[k, 0:g, :] = v[k * g:(k + 1) * g]
        out.append(buf[k, pl.ds(1, g), :])
        buf[k, g:g + 1, :] = buf[k, 0:1, :]
    return out


def _later(v, s, early, g):
    return jnp.concatenate([v[s * g:]] + early[0:s], axis=0)


def _fwd_in_first(x, vecs, w_in_g, q_idx, ts, deps=()):
    s = x.shape[0]

    def body(q_ref, x_ref, v_ref, w_ref, h1_ref, proj_ref):
        xhat, _ = _rms(x_ref[...])
        h = xhat * v_ref[V_GMIX:V_GMIX + 1, :] * (1.0 + v_ref[V_SC1:V_SC1 + 1, :]) + v_ref[V_SH1:V_SH1 + 1, :]
        hb = h.astype(BF16)
        h1_ref[...] = hb
        proj_ref[...] = _dot(hb, w_ref[...]).astype(BF16)

    return pl.pallas_call(
        lambda q_ref, *refs: body(q_ref, *refs[len(deps):]),
        grid_spec=pltpu.PrefetchScalarGridSpec(
            num_scalar_prefetch=1, grid=(s // ts,),
            in_specs=[_ANY] * len(deps) + [pl.BlockSpec((ts, D), lambda i, q: (i, 0)), _VMEM,
                                           pl.BlockSpec((None, D, C_IN), lambda i, q: (q[0], 0, 0))],
            out_specs=[pl.BlockSpec((ts, D), lambda i, q: (i, 0)), pl.BlockSpec((ts, C_IN), lambda i, q: (i, q[0]))]),
        out_shape=(jax.ShapeDtypeStruct((s, D), BF16), jax.ShapeDtypeStruct((s, D_IN), BF16)),
        compiler_params=_cparams(), name="fwd_in_own")(q_idx, *deps, x, vecs, w_in_g)


def _fwd_in_more(h1, w_in_g, proj, q_idx, ts, name, deps=()):
    s = h1.shape[0]

    def body(q_ref, h1_ref, w_ref, proj_in_ref, proj_ref):
        proj_ref[...] = _dot(h1_ref[...], w_ref[...]).astype(BF16)

    return pl.pallas_call(
        lambda q_ref, *refs: body(q_ref, *refs[len(deps):]),
        grid_spec=pltpu.PrefetchScalarGridSpec(
            num_scalar_prefetch=1, grid=(s // ts,),
            in_specs=[_ANY] * len(deps) + [pl.BlockSpec((ts, D), lambda i, q: (i, 0)),
                                           pl.BlockSpec((None, D, C_IN), lambda i, q: (q[0], 0, 0)), _ANY],
            out_specs=pl.BlockSpec((ts, C_IN), lambda i, q: (i, q[0]))),
        out_shape=jax.ShapeDtypeStruct((s, D_IN), BF16), input_output_aliases={len(deps) + 3: 0},
        compiler_params=_cparams(), name=name)(q_idx, *deps, h1, w_in_g, proj)


def _fwd_mix(proj, x, vecs, w_rga, w_rgx, w_out, ts, deps=()):
    s = x.shape[0]
    g = ts // TIME_BLOCKS

    def body(proj_ref, x_ref, v_ref, wa_ref, wx_ref, wo_ref, x1_ref, mg_ref, z1_ref, u_ref, h_ref,
             ua_buf, rx_buf, p_buf, q_buf, c_buf, hcarry):
        i = pl.program_id(0)

        @pl.when(i == 0)
        def _():
            ua_buf[...] = jnp.zeros(ua_buf.shape, F32)
            rx_buf[...] = jnp.zeros(rx_buf.shape, F32)
            hcarry[...] = jnp.zeros((8, D), F32)

        def seg(j):
            return proj_ref[:, j * D:(j + 1) * D].astype(F32)

        def vrow(j):
            return v_ref[j:j + 1, :]

        cb, cc, cx, rx, rg, ga, gb = (seg(j) for j in range(7))
        ua = cc * cx
        ua_late = _late_blocks(ua, ua_buf, g)
        rx_late = _late_blocks(rx, rx_buf, g)
        va = vrow(V_WA2) * ua + vrow(V_WA1) * _earlier(ua, 1, ua_late, g) + vrow(V_WA0) * _earlier(ua, 2, ua_late, g)
        u = (vrow(V_WB3) * rx + vrow(V_WB2) * _earlier(rx, 1, rx_late, g) + vrow(V_WB1) * _earlier(rx, 2, rx_late, g)
             + vrow(V_WB0) * _earlier(rx, 3, rx_late, g) + vrow(V_CBB))
        u_ref[...] = u

        rows = lax.broadcasted_iota(jnp.int32, (ts, D), 0)
        row0 = jnp.logical_and(rows == 0, i == 0)
        _, ig, a, mult = _lru_gates(u, wa_ref, wx_ref, v_ref, row0)
        bx = mult * (ig * u)

        prods, sums = [a[0:g]], [bx[0:g]]
        for k in range(1, TIME_BLOCKS):
            ak = a[k * g:(k + 1) * g]
            sums.append(ak * sums[-1] + bx[k * g:(k + 1) * g])
            prods.append(ak * prods[-1])
        p_buf[...] = prods[-1]
        q_buf[...] = sums[-1]
        state = hcarry[0:1, :]
        for j in range(g):
            c_buf[j:j + 1, :] = state
            state = p_buf[j:j + 1, :] * state + q_buf[j:j + 1, :]
        hcarry[0:1, :] = state
        entering = c_buf[...]
        h = jnp.concatenate([sums[k] + prods[k] * entering for k in range(TIME_BLOCKS)], axis=0)
        h_ref[...] = h

        gel, _ = _gelu(rg)
        merged = (jax.nn.sigmoid(ga) * (cb * va) + jax.nn.sigmoid(gb) * (h * gel)).astype(BF16)
        mg_ref[...] = merged
        z1 = _dot(merged, wo_ref[...])
        z1_ref[...] = z1.astype(BF16)
        x1_ref[...] = x_ref[...] + vrow(V_GT1) * z1

    row = lambda i: (i, 0)
    return pl.pallas_call(
        _after(deps, body), grid=(s // ts,),
        out_shape=(jax.ShapeDtypeStruct((s, D), F32), jax.ShapeDtypeStruct((s, D), BF16), jax.ShapeDtypeStruct((s, D), BF16),
                   jax.ShapeDtypeStruct((s, D), F32), jax.ShapeDtypeStruct((s, D), F32)),
        in_specs=[_ANY] * len(deps) + [pl.BlockSpec((ts, D_IN), row), pl.BlockSpec((ts, D), row), _VMEM, _VMEM, _VMEM, _VMEM],
        out_specs=[pl.BlockSpec((ts, D), row)] * 5,
        scratch_shapes=[pltpu.VMEM((2, g + 8, D), F32), pltpu.VMEM((3, g + 8, D), F32), pltpu.VMEM((g, D), F32),
                        pltpu.VMEM((g, D), F32), pltpu.VMEM((g, D), F32), pltpu.VMEM((8, D), F32)],
        compiler_params=_cparams(), name="fwd_mix")(*deps, proj, x, vecs, w_rga, w_rgx, w_out)


def _ffn_loss(x1, target, vecs, w_gu_g, w_dn, ts):
    s = x1.shape[0]

    def body(x1_ref, t_ref, v_ref, wgu_ref, wdn_ref, dx1_ref, h2_ref, act_ref, dz2_ref, dgu_ref, sm_ref):
        @pl.when(pl.program_id(0) == 0)
        def _():
            sm_ref[...] = jnp.zeros((N_SMALL, D), F32)

        def vrow(j):
            return v_ref[j:j + 1, :]

        def acc(j, val):
            sm_ref[j:j + 1, :] += _rowsum(val)

        x1 = x1_ref[...]
        xh1, rstd1 = _rms(x1)
        n2 = xh1 * vrow(V_GFFN)
        h2 = (n2 * (1.0 + vrow(V_SC2)) + vrow(V_SH2)).astype(BF16)
        h2_ref[...] = h2
        g = jnp.concatenate([_dot(h2, wgu_ref[0]), _dot(h2, wgu_ref[1])], axis=1)
        up = jnp.concatenate([_dot(h2, wgu_ref[2]), _dot(h2, wgu_ref[3])], axis=1)
        sg = jax.nn.sigmoid(g)
        silu = g * sg
        act = (silu * up).astype(BF16)
        act_ref[...] = act
        z2 = _dot(act, wdn_ref[...])
        x2 = x1 + vrow(V_GT2) * z2
        xh2, rstd2 = _rms(x2)
        err = xh2 * vrow(V_GFIN) - t_ref[...]
        acc(G_LOSS, (0.5 / D) * err * err)
        dy = err * (1.0 / D)
        acc(G_GFIN, dy * xh2)
        dx2 = _rms_bwd(dy * vrow(V_GFIN), xh2, rstd2)
        acc(G_GT2, dx2 * z2)
        dz2 = (vrow(V_GT2) * dx2).astype(BF16)
        dz2_ref[...] = dz2
        dact = _dot_nt(dz2, wdn_ref[...])
        dgate = (dact * up * (sg * (1.0 + g * (1.0 - sg)))).astype(BF16)
        dup = (dact * silu).astype(BF16)
        dgu_ref[:, 0:D_FF] = dgate
        dgu_ref[:, D_FF:2 * D_FF] = dup
        dh2 = (_dot_nt(dgate[:, 0:C_GU], wgu_ref[0]) + _dot_nt(dgate[:, C_GU:2 * C_GU], wgu_ref[1])
               + _dot_nt(dup[:, 0:C_GU], wgu_ref[2]) + _dot_nt(dup[:, C_GU:2 * C_GU], wgu_ref[3]))
        acc(G_SH2, dh2)
        acc(G_SC2, dh2 * n2)
        dn2 = dh2 * (1.0 + vrow(V_SC2))
        acc(G_GFFN, dn2 * xh1)
        dx1_ref[...] = dx2 + _rms_bwd(dn2 * vrow(V_GFFN), xh1, rstd1)

    row = lambda i: (i, 0)
    return pl.pallas_call(
        body, grid=(s // ts,),
        out_shape=(jax.ShapeDtypeStruct((s, D), F32), jax.ShapeDtypeStruct((s, D), BF16), jax.ShapeDtypeStruct((s, D_FF), BF16),
                   jax.ShapeDtypeStruct((s, D), BF16), jax.ShapeDtypeStruct((s, 2 * D_FF), BF16),
                   jax.ShapeDtypeStruct((N_SMALL, D), F32)),
        in_specs=[pl.BlockSpec((ts, D), row), pl.BlockSpec((ts, D), row), _VMEM, _VMEM, _VMEM],
        out_specs=[pl.BlockSpec((ts, D), row), pl.BlockSpec((ts, D), row), pl.BlockSpec((ts, D_FF), row),
                   pl.BlockSpec((ts, D), row), pl.BlockSpec((ts, 2 * D_FF), row), pl.BlockSpec((N_SMALL, D), lambda i: (0, 0))],
        compiler_params=_cparams(), name="ffn_loss")(x1, target, vecs, w_gu_g, w_dn)


def _bwd_mix(dx1, z1, merged, proj, u, h, vecs, w_rga, w_rgx, w_out, small, ts, deps=()):
    s = dx1.shape[0]
    nt = s // ts
    g = ts // TIME_BLOCKS
    assert g % 16 == 0

    def body(dx1_ref, z1_ref, mg_ref, proj_ref, u_ref, h_ref, hh_ref, cc6_ref, cx6_ref, cc7_ref, cx7_ref, v_ref, wa_ref, wx_ref,
             wo_ref, sm0_ref, dproj_ref, sm_ref, dwa_ref, dwx_ref, dwo_ref,
             ua_buf, h_buf, a_buf, dva_buf, du_buf, p_buf, q_buf, c_buf, lcarry):
        i = pl.program_id(0)
        first_tile = i == nt - 1

        @pl.when(i == 0)
        def _():
            a_buf[...] = jnp.zeros(a_buf.shape, F32)
            dva_buf[...] = jnp.zeros(dva_buf.shape, F32)
            du_buf[...] = jnp.zeros(du_buf.shape, F32)
            lcarry[...] = jnp.zeros((8, D), F32)
            sm_ref[...] = sm0_ref[...]
            dwa_ref[...] = jnp.zeros((HEADS, HB, HB), F32)
            dwx_ref[...] = jnp.zeros((HEADS, HB, HB), F32)
            dwo_ref[...] = jnp.zeros((D, D), F32)

        def seg(j):
            return proj_ref[:, j * D:(j + 1) * D].astype(F32)

        def vrow(j):
            return v_ref[j:j + 1, :]

        def acc(j, val):
            sm_ref[j:j + 1, :] += _rowsum(val)

        cb, cc, cx, rx, rg, ga, gb = (seg(j) for j in range(7))
        ua = cc * cx

        def last_row(v):
            pick = lax.broadcasted_iota(jnp.int32, v.shape, 0) == v.shape[0] - 1
            return jnp.where(first_tile, 0.0, jnp.sum(jnp.where(pick, v, 0.0), axis=0, keepdims=True))

        ua_halo = [last_row(cc6_ref[...].astype(F32) * cx6_ref[...].astype(F32)),
                   last_row(cc7_ref[...].astype(F32) * cx7_ref[...].astype(F32))]
        ua_late = _late_blocks(ua, ua_buf, g, ua_halo)
        va = vrow(V_WA2) * ua + vrow(V_WA1) * _earlier(ua, 1, ua_late, g) + vrow(V_WA0) * _earlier(ua, 2, ua_late, g)
        u = u_ref[...]
        h = h_ref[...]
        rows = lax.broadcasted_iota(jnp.int32, (ts, D), 0)
        row0 = jnp.logical_and(rows == 0, first_tile)
        r, ig, a, mult = _lru_gates(u, wa_ref, wx_ref, v_ref, row0)
        sga = jax.nn.sigmoid(ga)
        sgb = jax.nn.sigmoid(gb)
        gel, dgel = _gelu(rg)

        dx1 = dx1_ref[...]
        acc(G_GT1, dx1 * z1_ref[...].astype(F32))
        dz1 = (vrow(V_GT1) * dx1).astype(BF16)
        dwo_ref[...] += _dot_tn(mg_ref[...], dz1)
        dmg = _dot_nt(dz1, wo_ref[...])
        dproj_ref[:, 5 * D:6 * D] = (dmg * (cb * va) * sga * (1.0 - sga)).astype(BF16)
        dproj_ref[:, 6 * D:7 * D] = (dmg * (h * gel) * sgb * (1.0 - sgb)).astype(BF16)
        dya = dmg * sga
        dyb = dmg * sgb

        dproj_ref[:, 0:D] = (dya * va).astype(BF16)
        dva = dya * cb
        dva_early = _early_blocks(dva, dva_buf, g)
        dva1 = _later(dva, 1, dva_early, g)
        dva2 = _later(dva, 2, dva_early, g)
        dua = vrow(V_WA2) * dva + vrow(V_WA1) * dva1 + vrow(V_WA0) * dva2
        acc(G_WA2, ua * dva)
        acc(G_WA1, ua * dva1)
        acc(G_WA0, ua * dva2)
        dproj_ref[:, D:2 * D] = (dua * cx).astype(BF16)
        dproj_ref[:, 2 * D:3 * D] = (dua * cc).astype(BF16)

        dproj_ref[:, 4 * D:5 * D] = (dyb * h * dgel).astype(BF16)
        a_next = _later(a, 1, _early_blocks(a, a_buf, g), g)
        dh = dyb * gel
        last = TIME_BLOCKS - 1
        prods, sums = {last: a_next[last * g:]}, {last: dh[last * g:]}
        for k in range(last - 1, -1, -1):
            ak = a_next[k * g:(k + 1) * g]
            sums[k] = dh[k * g:(k + 1) * g] + ak * sums[k + 1]
            prods[k] = ak * prods[k + 1]
        p_buf[...] = prods[0]
        q_buf[...] = sums[0]
        state = lcarry[0:1, :]
        for j in range(g - 1, -1, -1):
            c_buf[j:j + 1, :] = state
            state = q_buf[j:j + 1, :] + p_buf[j:j + 1, :] * state
        lcarry[0:1, :] = state
        entering = c_buf[...]
        lam = jnp.concatenate([sums[k] + prods[k] * entering for k in range(TIME_BLOCKS)], axis=0)

        h_halo = [jnp.where(first_tile, 0.0, hh_ref[7:8, :])]
        da = lam * _earlier(h, 1, _late_blocks(h, h_buf, g, h_halo), g)
        dmult = jnp.where(row0, 0.0, lam * (ig * u))
        di = lam * mult * u
        du = lam * mult * ig
        dlog_a = da * a - dmult * (a * a) / mult
        lam_p = vrow(V_LAM)
        dr = dlog_a * (LRU_C * _log_sigmoid(lam_p))
        sm_ref[G_LAM:G_LAM + 1, :] += _rowsum(dlog_a * r) * (LRU_C * jax.nn.sigmoid(-lam_p))
        dpa = dr * r * (1.0 - r)
        dpx = di * ig * (1.0 - ig)
        acc(G_BA, dpa)
        acc(G_BX, dpx)
        dpab = dpa.astype(BF16)
        dpxb = dpx.astype(BF16)
        ub = u.astype(BF16)
        back = []
        for hd in range(HEADS):
            cols = slice(hd * HB, (hd + 1) * HB)
            back.append(_dot_nt(dpab[:, cols], wa_ref[hd]) + _dot_nt(dpxb[:, cols], wx_ref[hd]))
            dwa_ref[hd] += _dot_tn(ub[:, cols], dpab[:, cols])
            dwx_ref[hd] += _dot_tn(ub[:, cols], dpxb[:, cols])
        du = du + jnp.concatenate(back, axis=1)

        acc(G_CBB, du)
        du_early = _early_blocks(du, du_buf, g)
        du1 = _later(du, 1, du_early, g)
        du2 = _later(du, 2, du_early, g)
        du3 = _later(du, 3, du_early, g)
        dproj_ref[:, 3 * D:4 * D] = (vrow(V_WB3) * du + vrow(V_WB2) * du1 + vrow(V_WB1) * du2 + vrow(V_WB0) * du3).astype(BF16)
        acc(G_WB3, rx * du)
        acc(G_WB2, rx * du1)
        acc(G_WB1, rx * du2)
        acc(G_WB0, rx * du3)

    rev = lambda i: (nt - 1 - i, 0)
    halo8 = lambda i: (jnp.maximum((nt - 1 - i) * (ts // 8) - 1, 0), 0)
    const2 = lambda i: (0, 0)
    const3 = lambda i: (0, 0, 0)

    def halo16(back, col):
        return pl.BlockSpec((16, D), lambda i: (jnp.maximum((nt - 1 - i) * (ts // 16) - back, 0), col))
    return pl.pallas_call(
        _after(deps, body), grid=(nt,),
        out_shape=(jax.ShapeDtypeStruct((s, D_IN), BF16), jax.ShapeDtypeStruct((N_SMALL, D), F32),
                   jax.ShapeDtypeStruct((HEADS, HB, HB), F32), jax.ShapeDtypeStruct((HEADS, HB, HB), F32),
                   jax.ShapeDtypeStruct((D, D), F32)),
        in_specs=[_ANY] * len(deps) + [pl.BlockSpec((ts, D), rev), pl.BlockSpec((ts, D), rev), pl.BlockSpec((ts, D), rev),
                  pl.BlockSpec((ts, D_IN), rev), pl.BlockSpec((ts, D), rev), pl.BlockSpec((ts, D), rev),
                  pl.BlockSpec((8, D), halo8),
                  halo16(1 + g // 16, 1), halo16(1 + g // 16, 2), halo16(1, 1), halo16(1, 2),
                  _VMEM, _VMEM, _VMEM, _VMEM, _VMEM],
        out_specs=[pl.BlockSpec((ts, D_IN), rev), pl.BlockSpec((N_SMALL, D), const2),
                   pl.BlockSpec((HEADS, HB, HB), const3), pl.BlockSpec((HEADS, HB, HB), const3), pl.BlockSpec((D, D), const2)],
        scratch_shapes=[pltpu.VMEM((2, g + 8, D), F32), pltpu.VMEM((1, g + 8, D), F32), pltpu.VMEM((1, g + 8, D), F32),
                        pltpu.VMEM((2, g + 8, D), F32), pltpu.VMEM((3, g + 8, D), F32), pltpu.VMEM((g, D), F32),
                        pltpu.VMEM((g, D), F32), pltpu.VMEM((g, D), F32), pltpu.VMEM((8, D), F32)],
        compiler_params=_cparams(), name="bwd_mix")(*deps, dx1, z1, merged, proj, u, h, h, proj, proj, proj, proj, vecs, w_rga,
                                                    w_rgx, w_out, small)


def _bwd_in(dproj, x, dx1, vecs, w_in_g, small, ts, deps=()):
    s = x.shape[0]

    def body(dp_ref, x_ref, dx1_ref, v_ref, w_ref, sm0_ref, gx_ref, sm_ref):
        @pl.when(pl.program_id(0) == 0)
        def _():
            sm_ref[...] = sm0_ref[...]

        def vrow(j):
            return v_ref[j:j + 1, :]

        dh1 = _dot_nt(dp_ref[:, 0:C_IN], w_ref[0])
        for k in range(1, N_CHIPS):
            dh1 += _dot_nt(dp_ref[:, k * C_IN:(k + 1) * C_IN], w_ref[k])
        xh, rstd = _rms(x_ref[...])
        sm_ref[G_SH1:G_SH1 + 1, :] += _rowsum(dh1)
        sm_ref[G_SC1:G_SC1 + 1, :] += _rowsum(dh1 * (xh * vrow(V_GMIX)))
        dn1 = dh1 * (1.0 + vrow(V_SC1))
        sm_ref[G_GMIX:G_GMIX + 1, :] += _rowsum(dn1 * xh)
        gx_ref[...] = dx1_ref[...] + _rms_bwd(dn1 * vrow(V_GMIX), xh, rstd)

    row = lambda i: (i, 0)
    return pl.pallas_call(
        _after(deps, body), grid=(s // ts,),
        out_shape=(jax.ShapeDtypeStruct((s, D), F32), jax.ShapeDtypeStruct((N_SMALL, D), F32)),
        in_specs=[_ANY] * len(deps) + [pl.BlockSpec((ts, D_IN), row), pl.BlockSpec((ts, D), row), pl.BlockSpec((ts, D), row),
                                       _VMEM, _VMEM, _VMEM],
        out_specs=[pl.BlockSpec((ts, D), row), pl.BlockSpec((N_SMALL, D), lambda i: (0, 0))],
        compiler_params=_cparams(), name="bwd_in")(*deps, dproj, x, dx1, vecs, w_in_g, small)


def _grad_w(a, b, n_col_blocks, ts, name, deps=()):
    s, m = a.shape
    tn = b.shape[1] // n_col_blocks
    n_steps = s // ts

    def body(a_ref, b_ref, o_ref, acc_ref):
        k = pl.program_id(1)

        @pl.when(k == 0)
        def _():
            acc_ref[...] = jnp.zeros((m, tn), F32)

        acc_ref[...] += _dot_tn(a_ref[...], b_ref[...])

        @pl.when(k == n_steps - 1)
        def _():
            o_ref[...] = acc_ref[...].astype(BF16)

    return pl.pallas_call(
        _after(deps, body), grid=(n_col_blocks, n_steps),
        out_shape=jax.ShapeDtypeStruct((n_col_blocks, m, tn), BF16),
        in_specs=[_ANY] * len(deps) + [pl.BlockSpec((ts, m), lambda n, k: (k, 0)), pl.BlockSpec((ts, tn), lambda n, k: (k, n))],
        out_specs=pl.BlockSpec((None, m, tn), lambda n, k: (n, 0, 0)),
        scratch_shapes=[pltpu.VMEM((m, tn), F32)],
        compiler_params=_cparams(2), name=name)(*deps, a, b)


def _ada_fwd(c_all, w_ada, b_ada):
    n = w_ada.shape[1]

    def body(c_ref, w_ref, b_ref, o_ref, ca_ref):
        c = c_ref[...]
        ca = c * jax.nn.sigmoid(c)
        ca_ref[...] = ca
        o_ref[...] = jnp.dot(ca, w_ref[...], preferred_element_type=F32, precision=lax.Precision.HIGHEST) + b_ref[...]

    return pl.pallas_call(
        body, out_shape=(jax.ShapeDtypeStruct((N_DEV, n), F32), jax.ShapeDtypeStruct((N_DEV, D), F32)),
        in_specs=[_VMEM] * 3, out_specs=[_VMEM] * 2, compiler_params=_cparams(0), name="ada_fwd")(c_all, w_ada, b_ada)


def _ada_bwd(c_act, dmod):
    n = dmod.shape[1]

    def body(c_ref, d_ref, o_ref):
        o_ref[...] = lax.dot_general(c_ref[...], d_ref[...], (((0,), (0,)), ((), ())), preferred_element_type=F32,
                                     precision=lax.Precision.HIGHEST)

    return pl.pallas_call(
        body, out_shape=jax.ShapeDtypeStruct((D, n), F32), in_specs=[_VMEM] * 2, out_specs=_VMEM,
        compiler_params=_cparams(0), name="ada_bwd")(c_act, dmod)


def _sum_small(parts):
    def body(p_ref, o_ref, d_ref):
        tot = p_ref[0]
        for dev in range(1, N_DEV):
            tot = tot + p_ref[dev]
        o_ref[...] = tot
        d_ref[...] = p_ref[:, 0:8, :]

    return pl.pallas_call(
        body, out_shape=(jax.ShapeDtypeStruct((N_SMALL, D), F32), jax.ShapeDtypeStruct((N_DEV, 8, D), F32)),
        in_specs=[_VMEM], out_specs=[_VMEM] * 2, compiler_params=_cparams(0), name="sum_small")(parts)


def _adamw(w, g, m, v, name, deps=()):
    rows, cols = w.shape
    tr = 128 if rows % 128 == 0 else (64 if rows % 64 == 0 else rows)

    def body(w_ref, g_ref, m_ref, v_ref, d_ref, nm_ref, nv_ref):
        g_ = g_ref[...]
        m_ = ADAM_B1 * m_ref[...] + (1.0 - ADAM_B1) * g_
        v_ = ADAM_B2 * v_ref[...] + (1.0 - ADAM_B2) * (g_ * g_)
        nm_ref[...] = m_
        nv_ref[...] = v_
        m_hat = m_ / (1.0 - ADAM_B1 ** ADAM_STEP)
        v_hat = v_ / (1.0 - ADAM_B2 ** ADAM_STEP)
        d_ref[...] = -ADAM_LR * (m_hat / (jnp.sqrt(v_hat) + ADAM_EPS) + ADAM_WD * w_ref[...])

    spec = pl.BlockSpec((tr, cols), lambda i: (i, 0))
    return pl.pallas_call(
        _after(deps, body), grid=(rows // tr,), out_shape=(jax.ShapeDtypeStruct((rows, cols), F32),) * 3,
        in_specs=[_ANY] * len(deps) + [spec] * 4, out_specs=[spec] * 3, compiler_params=_cparams(), name=name)(*deps, w, g, m, v)


def _adamw_small(items, name):
    n = len(items)

    def body(*refs):
        ins, outs = refs[:4 * n], refs[4 * n:]
        for k in range(n):
            w_ref, g_ref, m_ref, v_ref = ins[4 * k:4 * k + 4]
            d_ref, nm_ref, nv_ref = outs[3 * k:3 * k + 3]
            g_ = g_ref[...]
            m_ = ADAM_B1 * m_ref[...] + (1.0 - ADAM_B1) * g_
            v_ = ADAM_B2 * v_ref[...] + (1.0 - ADAM_B2) * (g_ * g_)
            nm_ref[...] = m_
            nv_ref[...] = v_
            m_hat = m_ / (1.0 - ADAM_B1 ** ADAM_STEP)
            v_hat = v_ / (1.0 - ADAM_B2 ** ADAM_STEP)
            d_ref[...] = -ADAM_LR * (m_hat / (jnp.sqrt(v_hat) + ADAM_EPS) + ADAM_WD * w_ref[...])

    out = pl.pallas_call(
        body, out_shape=tuple(jax.ShapeDtypeStruct(it[0].shape, F32) for it in items for _ in range(3)),
        in_specs=[_VMEM] * (4 * n), out_specs=[_VMEM] * (3 * n), name=name)(*[a for it in items for a in it])
    return [tuple(out[3 * k:3 * k + 3]) for k in range(n)]


def _adamw_halves(w, mine, other, m, v, c_idx, name, deps=()):
    r2, cols = mine.shape
    tr = next(t for t in (128, 64, 32, 16, 8) if r2 % t == 0)
    nh = r2 // tr

    def body(c_ref, w_ref, mine_ref, other_ref, m_ref, v_ref, g_ref, d_ref, nm_ref, nv_ref):
        g_ = jnp.where(pl.program_id(0) // nh == c_ref[0], mine_ref[...], other_ref[...])
        g_ref[...] = g_
        m_ = ADAM_B1 * m_ref[...] + (1.0 - ADAM_B1) * g_
        v_ = ADAM_B2 * v_ref[...] + (1.0 - ADAM_B2) * (g_ * g_)
        nm_ref[...] = m_
        nv_ref[...] = v_
        m_hat = m_ / (1.0 - ADAM_B1 ** ADAM_STEP)
        v_hat = v_ / (1.0 - ADAM_B2 ** ADAM_STEP)
        d_ref[...] = -ADAM_LR * (m_hat / (jnp.sqrt(v_hat) + ADAM_EPS) + ADAM_WD * w_ref[...])

    full = pl.BlockSpec((tr, cols), lambda i, c: (i, 0))
    mine_spec = pl.BlockSpec((tr, cols), lambda i, c: (jnp.clip(i - c[0] * nh, 0, nh - 1), 0))
    other_spec = pl.BlockSpec((tr, cols), lambda i, c: (jnp.clip(i - (1 - c[0]) * nh, 0, nh - 1), 0))
    return pl.pallas_call(
        lambda c_ref, *refs: body(c_ref, *refs[len(deps):]),
        grid_spec=pltpu.PrefetchScalarGridSpec(
            num_scalar_prefetch=1, grid=(2 * nh,),
            in_specs=[_ANY] * len(deps) + [full, mine_spec, other_spec, full, full], out_specs=[full] * 4),
        out_shape=(jax.ShapeDtypeStruct((2 * r2, cols), F32),) * 4, compiler_params=_cparams(), name=name,
    )(c_idx, *deps, w, mine, other, m, v)


def _add_halves(g, recv, c_idx, name):
    n, _, r2, cols = g.shape

    def body(c_ref, g_ref, r_ref, o_ref):
        o_ref[...] = (g_ref[...].astype(F32) + r_ref[...].astype(F32)).astype(BF16)

    return pl.pallas_call(
        body,
        grid_spec=pltpu.PrefetchScalarGridSpec(
            num_scalar_prefetch=1, grid=(n,),
            in_specs=[pl.BlockSpec((None, None, r2, cols), lambda k, c: (k, c[0], 0, 0)),
                      pl.BlockSpec((None, r2, cols), lambda k, c: (k, 0, 0))],
            out_specs=pl.BlockSpec((None, r2, cols), lambda k, c: (k, 0, 0))),
        out_shape=jax.ShapeDtypeStruct((n, r2, cols), BF16), compiler_params=_cparams(), name=name)(c_idx, g, recv)


def _sum_chips(parts, name):
    n, r2, cols = parts.shape
    tr = next(t for t in (64, 32, 16) if r2 % t == 0)

    def body(p_ref, o_ref):
        o_ref[...] = ((p_ref[0].astype(F32) + p_ref[1].astype(F32)) + p_ref[2].astype(F32)) + p_ref[3].astype(F32)

    return pl.pallas_call(
        body, grid=(r2 // tr,), out_shape=jax.ShapeDtypeStruct((r2, cols), F32),
        in_specs=[pl.BlockSpec((n, tr, cols), lambda i: (0, i, 0))], out_specs=pl.BlockSpec((tr, cols), lambda i: (i, 0)),
        compiler_params=_cparams(), name=name)(parts)


def _place():
    x, y, c = lax.axis_index("x"), lax.axis_index("y"), lax.axis_index("c")
    return x, y, c, 2 * x + y


def _flip(v, bit):
    return 1 - v if bit else v


def _allgather8(v, name, deps=()):
    r, n = v.shape

    def body(*refs):
        v_ref, out_ref, send_sems, recv_sems, local_sem = refs[len(deps):]
        x, y, c, _ = _place()
        me = 4 * x + 2 * y + c
        mine = pltpu.make_async_copy(v_ref, out_ref.at[me], local_sem)
        mine.start()
        sends = []
        for rel in range(1, N_DEV):
            peer = (_flip(x, rel & 4), _flip(y, rel & 2), _flip(c, rel & 1))
            cp = pltpu.make_async_remote_copy(v_ref, out_ref.at[me], send_sems.at[rel - 1], recv_sems.at[rel - 1],
                                              device_id=peer, device_id_type=MESH)
            cp.start()
            sends.append(cp)
        for rel in range(1, N_DEV):
            peer = (_flip(x, rel & 4), _flip(y, rel & 2), _flip(c, rel & 1))
            peer_idx = 4 * peer[0] + 2 * peer[1] + peer[2]
            pltpu.make_async_remote_copy(v_ref, out_ref.at[peer_idx], send_sems.at[rel - 1], recv_sems.at[rel - 1],
                                         device_id=peer, device_id_type=MESH).wait_recv()
        for cp in sends:
            cp.wait_send()
        mine.wait()

    return pl.pallas_call(
        body, out_shape=jax.ShapeDtypeStruct((N_DEV, r, n), F32), in_specs=[_ANY] * len(deps) + [_VMEM], out_specs=_VMEM,
        scratch_shapes=[pltpu.SemaphoreType.DMA((N_DEV - 1,)), pltpu.SemaphoreType.DMA((N_DEV - 1,)), pltpu.SemaphoreType.DMA(())],
        name=name)(*deps, v)


def _gather_weights(shards):
    nw = len(shards)

    def body(*refs):
        w_refs, out_refs = refs[:nw], refs[nw:2 * nw]
        send_sems, recv_sems = refs[2 * nw:]
        x, y, c, p = _place()
        sibling = (x, y, 1 - c)
        sends = []
        for j in range(1, N_CHIPS):
            peer = (_flip(x, j & 2), _flip(y, j & 1), c)
            for w in range(nw):
                cp = pltpu.make_async_remote_copy(w_refs[w].at[c], out_refs[w].at[p, c], send_sems.at[w * 6 + j - 1],
                                                  recv_sems.at[w * 6 + j - 1], device_id=peer, device_id_type=MESH)
                cp.start()
                sends.append(cp)
        for j in range(1, N_CHIPS):
            peer = (_flip(x, j & 2), _flip(y, j & 1), c)
            q = 2 * peer[0] + peer[1]
            for w in range(nw):
                pltpu.make_async_remote_copy(w_refs[w].at[c], out_refs[w].at[q, c], send_sems.at[w * 6 + j - 1],
                                             recv_sems.at[w * 6 + j - 1], device_id=peer, device_id_type=MESH).wait_recv()
                cp = pltpu.make_async_remote_copy(out_refs[w].at[q, c], out_refs[w].at[q, c], send_sems.at[w * 6 + 2 + j],
                                                  recv_sems.at[w * 6 + 2 + j], device_id=sibling, device_id_type=MESH)
                cp.start()
                sends.append(cp)
        for j in range(1, N_CHIPS):
            q = 2 * _flip(x, j & 2) + _flip(y, j & 1)
            for w in range(nw):
                pltpu.make_async_remote_copy(out_refs[w].at[q, 1 - c], out_refs[w].at[q, 1 - c], send_sems.at[w * 6 + 2 + j],
                                             recv_sems.at[w * 6 + 2 + j], device_id=sibling, device_id_type=MESH).wait_recv()
        for cp in sends:
            cp.wait_send()

    return pl.pallas_call(
        body, out_shape=tuple(jax.ShapeDtypeStruct((N_CHIPS,) + s.shape, s.dtype) for s in shards),
        in_specs=[_ANY] * nw, out_specs=[_ANY] * nw,
        scratch_shapes=[pltpu.SemaphoreType.DMA((6 * nw,)), pltpu.SemaphoreType.DMA((6 * nw,))],
        name="gather_weights")(*shards)


def _swap_halves(grads):
    nw = len(grads)

    def body(*refs):
        g_refs, out_refs = refs[:nw], refs[nw:2 * nw]
        send_sems, recv_sems = refs[2 * nw:]
        x, y, c, _ = _place()
        sibling = (x, y, 1 - c)
        sends = []
        for w in range(nw):
            for k in range(N_CHIPS):
                cp = pltpu.make_async_remote_copy(g_refs[w].at[k, 1 - c], out_refs[w].at[k], send_sems.at[w * N_CHIPS + k],
                                                  recv_sems.at[w * N_CHIPS + k], device_id=sibling, device_id_type=MESH)
                cp.start()
                sends.append(cp)
        for cp in sends:
            cp.wait_recv()
        for cp in sends:
            cp.wait_send()

    return pl.pallas_call(
        body, out_shape=tuple(jax.ShapeDtypeStruct((N_CHIPS,) + g.shape[2:], g.dtype) for g in grads),
        in_specs=[_ANY] * nw, out_specs=[_ANY] * nw,
        scratch_shapes=[pltpu.SemaphoreType.DMA((N_CHIPS * nw,)), pltpu.SemaphoreType.DMA((N_CHIPS * nw,))],
        name="swap_halves")(*grads)


def _scatter_chips(parts):
    nw = len(parts)

    def body(*refs):
        p_refs, out_refs = refs[:nw], refs[nw:2 * nw]
        send_sems, recv_sems = refs[2 * nw:]
        x, y, c, p = _place()
        sends = []
        for j in range(1, N_CHIPS):
            peer = (_flip(x, j & 2), _flip(y, j & 1), c)
            q = 2 * peer[0] + peer[1]
            for w in range(nw):
                cp = pltpu.make_async_remote_copy(p_refs[w].at[q], out_refs[w].at[p], send_sems.at[w * 3 + j - 1],
                                                  recv_sems.at[w * 3 + j - 1], device_id=peer, device_id_type=MESH)
                cp.start()
                sends.append(cp)
        for j in range(1, N_CHIPS):
            peer = (_flip(x, j & 2), _flip(y, j & 1), c)
            q = 2 * peer[0] + peer[1]
            for w in range(nw):
                pltpu.make_async_remote_copy(p_refs[w].at[q], out_refs[w].at[q], send_sems.at[w * 3 + j - 1],
                                             recv_sems.at[w * 3 + j - 1], device_id=peer, device_id_type=MESH).wait_recv()
        for cp in sends:
            cp.wait_send()

    return pl.pallas_call(
        body, out_shape=tuple(jax.ShapeDtypeStruct(s.shape, s.dtype) for s in parts),
        in_specs=[_ANY] * nw, out_specs=[_ANY] * nw,
        scratch_shapes=[pltpu.SemaphoreType.DMA((3 * nw,)), pltpu.SemaphoreType.DMA((3 * nw,))],
        name="scatter_chips")(*parts)


def _share_halves(halves):
    nw = len(halves)

    def body(*refs):
        h_refs, out_refs = refs[:nw], refs[nw:2 * nw]
        send_sems, recv_sems = refs[2 * nw:]
        x, y, c, _ = _place()
        sends = []
        for w in range(nw):
            cp = pltpu.make_async_remote_copy(h_refs[w], out_refs[w], send_sems.at[w], recv_sems.at[w],
                                              device_id=(x, y, 1 - c), device_id_type=MESH)
            cp.start()
            sends.append(cp)
        for cp in sends:
            cp.wait_recv()
        for cp in sends:
            cp.wait_send()

    return pl.pallas_call(
        body, out_shape=tuple(jax.ShapeDtypeStruct(s.shape, s.dtype) for s in halves),
        in_specs=[_ANY] * nw, out_specs=[_ANY] * nw,
        scratch_shapes=[pltpu.SemaphoreType.DMA((nw,)), pltpu.SemaphoreType.DMA((nw,))],
        name="share_halves")(*halves)


_HBM = pl.BlockSpec(memory_space=pltpu.HBM)
_SEM = pl.BlockSpec(memory_space=pltpu.SEMAPHORE)
_EFFECT = pltpu.SideEffectType.DATAFLOW_SIDE_EFFECTING


def _xchg_start(name, plan, n_copies, srcs, lands, after=()):
    bufs = list(srcs) + list(lands)
    ns, nb = len(srcs), len(srcs) + len(lands)

    def body(*refs):
        send_sems, recv_sems, token = refs[nb + len(after)], refs[nb + len(after) + 1], refs[-1]
        for i, (src, dst, peer, _) in enumerate(plan(_place(), refs[:ns], refs[ns:nb])):
            pltpu.make_async_remote_copy(src, dst, send_sems.at[i], recv_sems.at[i], device_id=peer, device_id_type=MESH).start()
        token[...] = jnp.zeros_like(token)

    out = pl.pallas_call(
        body, name=name,
        out_shape=(pltpu.SemaphoreType.DMA((n_copies,)), pltpu.SemaphoreType.DMA((n_copies,)),
                   *[pltpu.HBM(a.shape, a.dtype) for a in bufs], jax.ShapeDtypeStruct((8, 128), F32)),
        in_specs=[_HBM] * nb + [_ANY] * len(after), out_specs=(_SEM, _SEM, *[_HBM] * nb, _VMEM),
        input_output_aliases={i: 2 + i for i in range(nb)},
        compiler_params=pltpu.CompilerParams(has_side_effects=_EFFECT),
    )(*[pltpu.with_memory_space_constraint(a, pltpu.HBM) for a in bufs], *after)
    return (out[0], out[1]), out[2:2 + ns], out[2 + ns:2 + nb], out[-1]


def _xchg_wait(name, plan, sems, srcs, lands, after, sem_ids=None):
    bufs = list(srcs) + list(lands)
    ns, nb = len(srcs), len(srcs) + len(lands)

    def body(*refs):
        send_sems, recv_sems = refs[nb], refs[nb + 1]
        copies = plan(_place(), refs[:ns], refs[ns:nb])
        ids = range(len(copies)) if sem_ids is None else sem_ids
        for i, (src, _, peer, mine) in zip(ids, copies, strict=True):
            if i is not None:
                cp = pltpu.make_async_remote_copy(src, mine, send_sems.at[i], recv_sems.at[i], device_id=peer,
                                                  device_id_type=MESH)
                cp.wait_send()
                cp.wait_recv()

    out = pl.pallas_call(
        body, name=name, out_shape=tuple(pltpu.HBM(a.shape, a.dtype) for a in bufs),
        in_specs=[_HBM] * nb + [_SEM, _SEM] + [_ANY] * len(after), out_specs=tuple([_HBM] * nb),
        input_output_aliases={i: i for i in range(nb)},
        compiler_params=pltpu.CompilerParams(has_side_effects=_EFFECT),
    )(*bufs, *sems, *after)
    return out[:ns], out[ns:]


def _other_chips(place, which=(1, 2, 3)):
    x, y, c, _ = place
    return [((_flip(x, j & 2), _flip(y, j & 1), c), 2 * _flip(x, j & 2) + _flip(y, j & 1)) for j in which]


def _plan_gather_ici(place, src_refs, land_refs):
    _, _, c, p = place
    return [(s.at[c], l.at[p, c], peer, l.at[q, c]) for s, l in zip(src_refs, land_refs) for peer, q in _other_chips(place)]


def _plan_relay(which):
    def plan(place, src_refs, land_refs):
        x, y, c, _ = place
        return [(l.at[q, c], l.at[q, c], (x, y, 1 - c), l.at[q, 1 - c]) for l in land_refs for _, q in _other_chips(place, which)]
    return plan


def _plan_swap(place, src_refs, land_refs):
    x, y, c, _ = place
    return [(s.at[k, 1 - c], l.at[k], (x, y, 1 - c), l.at[k]) for s, l in zip(src_refs, land_refs) for k in range(N_CHIPS)]


def _plan_scatter(place, src_refs, land_refs):
    _, _, _, p = place
    return [(s.at[q], l.at[p], peer, l.at[q]) for s, l in zip(src_refs, land_refs) for peer, q in _other_chips(place)]


def _plan_share(place, src_refs, land_refs):
    x, y, c, _ = place
    return [(s, l, (x, y, 1 - c), l) for s, l in zip(src_refs, land_refs)]


def _pack_rows(parts, n_rows, name, deps=()):
    def body(*refs):
        refs = refs[len(deps):]
        out_ref = refs[-1]
        out_ref[...] = jnp.zeros((n_rows, D), F32)
        at = 0
        for ref in refs[:-1]:
            k = ref.shape[0]
            out_ref[at:at + k, :] = ref[...]
            at += k

    return pl.pallas_call(
        body, out_shape=jax.ShapeDtypeStruct((n_rows, D), F32), in_specs=[_ANY] * len(deps) + [_VMEM] * len(parts),
        out_specs=_VMEM, name=name)(*deps, *parts)


TS_MM = 512
TS_IN = 1024
TS_GW = 1024
TS_MIX = 256


def _halved(a):
    n, r, cols = a.shape
    return a.reshape(n, 2, r // 2, cols)


def _rs_swap(name, grads, after=()):
    lands = [lax.empty((N_CHIPS,) + g.shape[2:], g.dtype) for g in grads]
    sems, grads, lands, token = _xchg_start(name + "_swap", _plan_swap, N_CHIPS * len(grads), grads, lands, after)
    return name, sems, grads, lands, token


def _rs_scatter(handle, after, chip, ci):
    name, sems, grads, lands, _ = handle
    grads, from_sibling = _xchg_wait(name + "_swap_wait", _plan_swap, sems, grads, lands, after)
    c_arr = jnp.reshape(ci, (1,)).astype(jnp.int32)
    pair_sums = [_add_halves(g, r, c_arr, "%s_add_halves_%d" % (name, k)) for k, (g, r) in enumerate(zip(grads, from_sibling))]
    lands = [lax.dynamic_update_index_in_dim(lax.empty(p.shape, p.dtype), lax.dynamic_index_in_dim(p, chip, 0, keepdims=False),
                                             chip, 0) for p in pair_sums]
    sems, pair_sums, lands, token = _xchg_start(name + "_scatter", _plan_scatter, 3 * len(pair_sums), pair_sums, lands)
    return name, sems, pair_sums, lands, token


def _rs_share(handle, after):
    name, sems, pair_sums, lands, _ = handle
    _, by_chip = _xchg_wait(name + "_scatter_wait", _plan_scatter, sems, pair_sums, lands, after)
    halves = [_sum_chips(b, "%s_sum_chips_%d" % (name, k)) for k, b in enumerate(by_chip)]
    lands = [lax.empty(h.shape, h.dtype) for h in halves]
    sems, halves, lands, token = _xchg_start(name + "_share", _plan_share, len(halves), halves, lands)
    return name, sems, halves, lands, token


def _rs_end(handle, after):
    name, sems, halves, lands, _ = handle
    halves, others = _xchg_wait(name + "_share_wait", _plan_share, sems, halves, lands, after)
    return list(zip(halves, others))


def kernel(x, c, w_ada, b_ada, g_norm_mix, w_in, conv_a_w, conv_b_w, conv_b_bias, w_rg_a, b_rg_a, w_rg_x, b_rg_x, lru_lambda, w_out, g_norm_ffn, w_gate_up, w_down, g_norm_final, loss_target, m_w_ada, m_b_ada, m_g_norm_mix, m_w_in, m_conv_a_w, m_conv_b_w, m_conv_b_bias, m_w_rg_a, m_b_rg_a, m_w_rg_x, m_b_rg_x, m_lru_lambda, m_w_out, m_g_norm_ffn, m_w_gate_up, m_w_down, m_g_norm_final, v_w_ada, v_b_ada, v_g_norm_mix, v_w_in, v_conv_a_w, v_conv_b_w, v_conv_b_bias, v_w_rg_a, v_b_rg_a, v_w_rg_x, v_b_rg_x, v_lru_lambda, v_w_out, v_g_norm_ffn, v_w_gate_up, v_w_down, v_g_norm_final):
    xi, yi, ci = lax.axis_index("x"), lax.axis_index("y"), lax.axis_index("c")
    chip = 2 * xi + yi
    me = 2 * chip + ci
    n_ada = w_ada.shape[2]

    def widen(w):
        return jnp.pad(w, ((0, 0), (0, D - w.shape[1])))

    got = _allgather8(_pack_rows([c, widen(conv_a_w[0]), widen(conv_b_w[0])], 8, "pack_c_conv"), "gather_c_conv")
    c_all = got[:, 0, :]
    conv_full = got[::2, 1:8, :D // N_CHIPS].transpose(1, 0, 2).reshape(7, D)

    mod_part, c_act = _ada_fwd(c_all, w_ada[0], lax.dynamic_slice_in_dim(b_ada, chip * n_ada, n_ada, axis=1))
    mod_all = _allgather8(mod_part, "gather_mod")
    mod_mine = lax.dynamic_index_in_dim(mod_all, me, axis=1, keepdims=False)[::2].reshape(6, D)
    vecs = _pack_rows([mod_mine, g_norm_mix, g_norm_ffn, g_norm_final.reshape(1, D), conv_b_bias, b_rg_a, b_rg_x, lru_lambda,
                       conv_full], N_VEC, "pack_vecs")

    def rg_shard(w):
        return w[0].astype(BF16).reshape(2, HEADS * HB // N_CHIPS // 2, HB)

    shards = [w_in[0].astype(BF16).reshape(2, D // 2, C_IN), rg_shard(w_rg_a), rg_shard(w_rg_x),
              w_out[0].astype(BF16).reshape(2, D // N_CHIPS // 2, D), w_gate_up[0].astype(BF16).reshape(2, D // 2, C_GU),
              w_down[0].astype(BF16).reshape(2, D_FF // N_CHIPS // 2, D)]
    lands = [lax.dynamic_update_index_in_dim(lax.empty((N_CHIPS,) + s.shape, s.dtype), s, chip, 0) for s in shards]

    def send(name, first, last, after):
        sems, srcs, zone, token = _xchg_start(name + "_ici", _plan_gather_ici, 3 * (last - first), shards[first:last],
                                              lands[first:last], after)
        shards[first:last], lands[first:last] = srcs, zone
        return sems, first, token

    def arrive(name, sent, first, last, after, which=(1, 2, 3)):
        sems, base, _ = sent
        ids = [3 * (k - base) + j - 1 if j in which else None for k in range(first, last) for j in (1, 2, 3)]
        srcs, zone = _xchg_wait(name + "_ici_wait", _plan_gather_ici, sems, shards[first:last], lands[first:last], after, ids)
        shards[first:last], lands[first:last] = srcs, zone

    def relay(name, first, last, which):
        plan = _plan_relay(which)
        sems, _, zone, token = _xchg_start(name + "_d2d", plan, len(which) * (last - first), [], lands[first:last])
        lands[first:last] = zone
        return name, plan, sems, first, last, token

    def relayed(handle, after):
        name, plan, sems, first, last, _ = handle
        lands[first:last] = _xchg_wait(name + "_d2d_wait", plan, sems, [], lands[first:last], after)[1]

    def to_blocks(v):
        return v.reshape(-1, TS_MIX // TIME_BLOCKS, TIME_BLOCKS, D).transpose(0, 2, 1, 3).reshape(v.shape)

    def from_blocks(v):
        return v.reshape(-1, TIME_BLOCKS, TS_MIX // TIME_BLOCKS, D).transpose(0, 2, 1, 3).reshape(v.shape)

    def chip_index(j):
        return jnp.reshape(chip ^ j, (1,)).astype(jnp.int32)

    def wg_in():
        return lands[0].reshape(N_CHIPS, D, C_IN)

    xs, target = to_blocks(x[0]), to_blocks(loss_target[0])
    sent_in = send("gather_in", 0, 1, [vecs])
    ts_in = min(TS_IN, xs.shape[0])
    h1, proj = _fwd_in_first(xs, vecs, wg_in(), chip_index(0), ts_in, deps=[sent_in[-1]])
    arrive("gather_in_near", sent_in, 0, 1, [proj], (1, 2))
    near = relay("gather_in_near", 0, 1, (1, 2))
    sent_rest = send("gather_rest", 1, 6, [near[-1]])
    relayed(near, [sent_rest[-1]])
    proj = _fwd_in_more(h1, wg_in(), proj, chip_index(1), ts_in, "fwd_in_y")
    proj = _fwd_in_more(h1, wg_in(), proj, chip_index(2), ts_in, "fwd_in_x")
    arrive("gather_in_far", sent_in, 0, 1, [proj], (3,))
    far = relay("gather_in_far", 0, 1, (3,))
    arrive("gather_mix", sent_rest, 1, 4, [far[-1]])
    relayed(far, [far[-1]])
    mix = relay("gather_mix", 1, 4, (1, 2, 3))
    proj = _fwd_in_more(h1, wg_in(), proj, chip_index(3), ts_in, "fwd_in_xy", deps=[mix[-1]])
    relayed(mix, [proj])
    wg_rga, wg_rgx, wg_out = lands[1:4]
    wg_out = wg_out.reshape(D, D)

    def rg_full(wg):
        return wg.reshape(N_CHIPS, HEADS, HB // N_CHIPS, HB).transpose(1, 0, 2, 3).reshape(HEADS, HB, HB)

    wg_rga, wg_rgx = rg_full(wg_rga), rg_full(wg_rgx)

    arrive("gather_ffn", sent_rest, 4, 6, [proj])
    ffn = relay("gather_ffn", 4, 6, (1, 2, 3))
    x1, merged, z1, u, h = _fwd_mix(proj, xs, vecs, wg_rga, wg_rgx, wg_out, TS_MIX, deps=[ffn[-1]])
    relayed(ffn, [x1])
    wg_gu, wg_dn = lands[4:6]
    wg_gu, wg_dn = wg_gu.reshape(N_CHIPS, D, C_GU), wg_dn.reshape(D_FF, D)
    dx1, h2, act, dz2, dgu, sm_ffn = _ffn_loss(x1, target, vecs, wg_gu, wg_dn, TS_MIX)

    def rg_chunks(dw):
        return _halved(dw.reshape(HEADS, N_CHIPS, HB // N_CHIPS, HB).transpose(1, 0, 2, 3).reshape(N_CHIPS, HB, HB).astype(BF16))

    ts_gw = min(TS_GW, xs.shape[0])
    g_dn = _grad_w(act, dz2, 1, ts_gw, "grad_w_down")
    g_gu = _grad_w(h2, dgu, N_CHIPS, ts_gw, "grad_w_gate_up")
    rs_b = _rs_swap("rs_b", [_halved(g_gu), _halved(g_dn.reshape(N_CHIPS, D_FF // N_CHIPS, D))])
    dproj, sm_mix, dw_rga, dw_rgx, dw_out = _bwd_mix(dx1, z1, merged, proj, u, h, vecs, wg_rga, wg_rgx, wg_out, sm_ffn, TS_MIX,
                                                     deps=[rs_b[-1]])
    rs_b = _rs_scatter(rs_b, [dproj], chip, ci)
    g_in = _grad_w(h1, dproj, N_CHIPS, ts_gw, "grad_w_in", deps=[rs_b[-1]])
    rs_b = _rs_share(rs_b, [g_in])
    rs_a = _rs_swap("rs_a", [_halved(g_in), rg_chunks(dw_rga), rg_chunks(dw_rgx),
                             _halved(dw_out.astype(BF16).reshape(N_CHIPS, D // N_CHIPS, D))], after=[rs_b[-1]])

    c_arr = jnp.reshape(ci, (1,)).astype(jnp.int32)

    def step(name, w, g, m, v, deps=()):
        shape = w.shape
        two_d = (-1, shape[-1])
        d, nm, nv = _adamw(w.reshape(two_d), g.reshape(two_d), m.reshape(two_d), v.reshape(two_d), "adamw_" + name, deps)
        return g.reshape(shape), d.reshape(shape), nm.reshape(shape), nv.reshape(shape)

    def step_halves(name, w, halves, m, v, deps=()):
        shape = w.shape
        two_d = (-1, shape[-1])
        out = _adamw_halves(w.reshape(two_d), halves[0], halves[1], m.reshape(two_d), v.reshape(two_d), c_arr, "adamw_" + name, deps)
        return tuple(a.reshape(shape) for a in out)

    def shard_cols(row_block):
        return lax.dynamic_slice_in_dim(row_block, chip * (D // N_CHIPS), D // N_CHIPS, axis=1)

    gw_gu, gw_dn = _rs_end(rs_b, [rs_a[-1]])
    res = {
        "w_gate_up": step_halves("w_gate_up", w_gate_up, gw_gu, m_w_gate_up, v_w_gate_up, [rs_a[-1]]),
        "w_down": step_halves("w_down", w_down, gw_dn, m_w_down, v_w_down, [rs_a[-1]]),
    }
    rs_a = _rs_scatter(rs_a, [res["w_gate_up"][1], res["w_down"][1]], chip, ci)
    grad_x, sm_in = _bwd_in(dproj, xs, dx1, vecs, wg_in(), sm_mix, TS_MM, deps=[rs_a[-1]])
    rs_a = _rs_share(rs_a, [grad_x])

    small, per_dev = _sum_small(_allgather8(sm_in, "gather_small", deps=[rs_a[-1]]))
    dmod_all = per_dev[:, 0:6, :].reshape(N_DEV, 6 * D)
    grad_w_ada = _ada_bwd(c_act, lax.dynamic_slice_in_dim(dmod_all, chip * n_ada, n_ada, axis=1))
    grad_b_ada = small[0:6].reshape(1, 6 * D)
    res["w_ada"] = step("w_ada", w_ada, grad_w_ada[None], m_w_ada, v_w_ada)
    small_sets = {
        "b_ada": (b_ada.reshape(6, D), grad_b_ada.reshape(6, D), m_b_ada.reshape(6, D), v_b_ada.reshape(6, D)),
        "g_norm_mix": (g_norm_mix, small[G_GMIX:G_GMIX + 1], m_g_norm_mix, v_g_norm_mix),
        "conv_a_w": (conv_a_w[0], shard_cols(small[G_WA0:G_WA0 + 3]), m_conv_a_w[0], v_conv_a_w[0]),
        "conv_b_w": (conv_b_w[0], shard_cols(small[G_WB0:G_WB0 + 4]), m_conv_b_w[0], v_conv_b_w[0]),
        "conv_b_bias": (conv_b_bias, small[G_CBB:G_CBB + 1], m_conv_b_bias, v_conv_b_bias),
        "b_rg_a": (b_rg_a, small[G_BA:G_BA + 1], m_b_rg_a, v_b_rg_a),
        "b_rg_x": (b_rg_x, small[G_BX:G_BX + 1], m_b_rg_x, v_b_rg_x),
        "lru_lambda": (lru_lambda, small[G_LAM:G_LAM + 1], m_lru_lambda, v_lru_lambda),
        "g_norm_ffn": (g_norm_ffn, small[G_GFFN:G_GFFN + 1], m_g_norm_ffn, v_g_norm_ffn),
        "g_norm_final": (g_norm_final.reshape(1, D), small[G_GFIN:G_GFIN + 1], m_g_norm_final.reshape(1, D),
                         v_g_norm_final.reshape(1, D)),
    }
    stepped = _adamw_small(list(small_sets.values()), "adamw_small")
    for (n, (w_, g_, _, _)), (d_, nm_, nv_) in zip(small_sets.items(), stepped):
        shape = (1,) + w_.shape if n.startswith("conv_") and n != "conv_b_bias" else w_.shape
        res[n] = tuple(a.reshape(shape) for a in (g_, d_, nm_, nv_))
    gw_in, gw_rga, gw_rgx, gw_out = _rs_end(rs_a, [res[n][1] for n in res])
    res["w_in"] = step_halves("w_in", w_in, gw_in, m_w_in, v_w_in)
    res["w_rg_a"] = step_halves("w_rg_a", w_rg_a, gw_rga, m_w_rg_a, v_w_rg_a)
    res["w_rg_x"] = step_halves("w_rg_x", w_rg_x, gw_rgx, m_w_rg_x, v_w_rg_x)
    res["w_out"] = step_halves("w_out", w_out, gw_out, m_w_out, v_w_out)
    res["b_ada"] = tuple(a.reshape(1, 6 * D) for a in res["b_ada"])
    res["g_norm_final"] = tuple(a.reshape(D) for a in res["g_norm_final"])
    names = ["w_ada", "b_ada", "g_norm_mix", "w_in", "conv_a_w", "conv_b_w", "conv_b_bias", "w_rg_a", "b_rg_a", "w_rg_x",
             "b_rg_x", "lru_lambda", "w_out", "g_norm_ffn", "w_gate_up", "w_down", "g_norm_final"]
    loss = jnp.sum(small[G_LOSS])
    return (loss, from_blocks(grad_x)[None], *[res[n][0] for n in names], *[res[n][1] for n in names],
            *[res[n][2] for n in names], *[res[n][3] for n in names])
```

```python
import functools

import jax
import jax.numpy as jnp
from jax import lax
from jax.experimental import pallas as pl
from jax.experimental.pallas import tpu as pltpu

F32 = jnp.float32
BF16 = jnp.bfloat16
MESH = pl.DeviceIdType.MESH

D = 1024
N_CHIPS = 4
N_DEV = 8
D_IN = 7 * D
C_IN = D_IN // N_CHIPS
D_FF = 2816
C_GU = 2 * D_FF // N_CHIPS
HEADS = 4
HB = D // HEADS
EPS = 1e-6
LRU_C = 8.0
ADAM_LR, ADAM_B1, ADAM_B2, ADAM_EPS, ADAM_WD, ADAM_STEP = 0.001, 0.9, 0.999, 1e-08, 0.01, 10
VMEM_LIMIT = 56 << 20

(V_SH1, V_SC1, V_GT1, V_SH2, V_SC2, V_GT2, V_GMIX, V_GFFN, V_GFIN, V_CBB, V_BA, V_BX, V_LAM,
 V_WA0, V_WA1, V_WA2, V_WB0, V_WB1, V_WB2, V_WB3) = range(20)
N_VEC = 24
(G_SH1, G_SC1, G_GT1, G_SH2, G_SC2, G_GT2, G_GMIX, G_CBB, G_BA, G_BX, G_LAM, G_GFFN, G_GFIN,
 G_WA0, G_WA1, G_WA2, G_WB0, G_WB1, G_WB2, G_WB3, G_LOSS) = range(21)
N_SMALL = 24

_VMEM = pl.BlockSpec(memory_space=pltpu.VMEM)
_ANY = pl.BlockSpec(memory_space=pl.ANY)


def _cparams(n_grid=1):
    return pltpu.CompilerParams(dimension_semantics=("arbitrary",) * n_grid, vmem_limit_bytes=VMEM_LIMIT)


def _after(deps, body):
    n = len(deps)
    return lambda *refs: body(*refs[n:])


def _rms(x):
    rstd = lax.rsqrt(jnp.mean(x * x, axis=-1, keepdims=True) + EPS)
    return x * rstd, rstd


def _rms_bwd(dxhat, xhat, rstd):
    return rstd * (dxhat - xhat * jnp.mean(dxhat * xhat, axis=-1, keepdims=True))


def _rowsum(v):
    return jnp.sum(v, axis=0, keepdims=True)


def _dot(a, b):
    return jnp.dot(a, b, preferred_element_type=F32)


def _dot_nt(a, b):
    return lax.dot_general(a, b, (((1,), (1,)), ((), ())), preferred_element_type=F32)


def _dot_tn(a, b):
    return lax.dot_general(a, b, (((0,), (0,)), ((), ())), preferred_element_type=F32)


def _gelu(x):
    k, c = 0.7978845608028654, 0.044715
    t = jnp.tanh(k * (x + c * x * x * x))
    return 0.5 * x * (1.0 + t), 0.5 * (1.0 + t) + 0.5 * x * (1.0 - t * t) * k * (1.0 + 3.0 * c * x * x)


def _log_sigmoid(lam):
    return jnp.minimum(lam, 0.0) - jnp.log1p(jnp.exp(-jnp.abs(lam)))


def _lru_gates(u, wa_ref, wx_ref, v_ref, row0):
    ub = u.astype(BF16)
    pre_a = jnp.concatenate([_dot(ub[:, h * HB:(h + 1) * HB], wa_ref[h]) for h in range(HEADS)], axis=1)
    pre_x = jnp.concatenate([_dot(ub[:, h * HB:(h + 1) * HB], wx_ref[h]) for h in range(HEADS)], axis=1)
    r = jax.nn.sigmoid(pre_a + v_ref[V_BA:V_BA + 1, :])
    ig = jax.nn.sigmoid(pre_x + v_ref[V_BX:V_BX + 1, :])
    log_a = LRU_C * r * _log_sigmoid(v_ref[V_LAM:V_LAM + 1, :])
    a = jnp.exp(log_a)
    x2 = 2.0 * log_a
    m2 = jnp.where(x2 > -0.03, -x2 * (1.0 + x2 * (0.5 + x2 * (1.0 / 6.0 + x2 * (1.0 / 24.0)))), 1.0 - a * a)
    mult = jnp.where(row0, 1.0, jnp.sqrt(jnp.maximum(m2, 0.0)))
    return r, ig, a, mult


TIME_BLOCKS = 8


def _late_blocks(v, buf, g, halo=None):
    n = buf.shape[0]
    out = []
    for idx in range(n):
        k = TIME_BLOCKS - n + idx
        buf[idx, 8:g + 8, :] = v[k * g:(k + 1) * g]
        if halo is not None:
            buf[idx, 7:8, :] = halo[idx]
        out.append(buf[idx, pl.ds(7, g), :])
        if halo is None:
            buf[idx, 7:8, :] = buf[idx, g + 7:g + 8, :]
    return out


def _earlier(v, s, late, g):
    return jnp.concatenate(late[len(late) - s:] + [v[0:(TIME_BLOCKS - s) * g]], axis=0)


def _early_blocks(v, buf, g):
    out = []
    for k in range(buf.shape[0]):
        buf[k, 0:g, :] = v[k * g:(k + 1) * g]
        out.append(buf[k, pl.ds(1, g), :])
        buf[k, g:g + 1, :] = buf[k, 0:1, :]
    return out


def _later(v, s, early, g):
    return jnp.concatenate([v[s * g:]] + early[0:s], axis=0)


def _fwd_in_first(x, vecs, w_in_g, q_idx, ts, deps=()):
    s = x.shape[0]

    def body(q_ref, x_ref, v_ref, w_ref, h1_ref, proj_ref):
        xhat, _ = _rms(x_ref[...])
        h = xhat * v_ref[V_GMIX:V_GMIX + 1, :] * (1.0 + v_ref[V_SC1:V_SC1 + 1, :]) + v_ref[V_SH1:V_SH1 + 1, :]
        hb = h.astype(BF16)
        h1_ref[...] = hb
        proj_ref[...] = _dot(hb, w_ref[...]).astype(BF16)

    return pl.pallas_call(
        lambda q_ref, *refs: body(q_ref, *refs[len(deps):]),
        grid_spec=pltpu.PrefetchScalarGridSpec(
            num_scalar_prefetch=1, grid=(s // ts,),
            in_specs=[_ANY] * len(deps) + [pl.BlockSpec((ts, D), lambda i, q: (i, 0)), _VMEM,
                                           pl.BlockSpec((None, D, C_IN), lambda i, q: (q[0], 0, 0))],
            out_specs=[pl.BlockSpec((ts, D), lambda i, q: (i, 0)), pl.BlockSpec((ts, C_IN), lambda i, q: (i, q[0]))]),
        out_shape=(jax.ShapeDtypeStruct((s, D), BF16), jax.ShapeDtypeStruct((s, D_IN), BF16)),
        compiler_params=_cparams(), name="fwd_in_own")(q_idx, *deps, x, vecs, w_in_g)


def _fwd_in_more(h1, w_in_g, proj, q_idx, ts, name, deps=()):
    s = h1.shape[0]

    def body(q_ref, h1_ref, w_ref, proj_in_ref, proj_ref):
        proj_ref[...] = _dot(h1_ref[...], w_ref[...]).astype(BF16)

    return pl.pallas_call(
        lambda q_ref, *refs: body(q_ref, *refs[len(deps):]),
        grid_spec=pltpu.PrefetchScalarGridSpec(
            num_scalar_prefetch=1, grid=(s // ts,),
            in_specs=[_ANY] * len(deps) + [pl.BlockSpec((ts, D), lambda i, q: (i, 0)),
                                           pl.BlockSpec((None, D, C_IN), lambda i, q: (q[0], 0, 0)), _ANY],
            out_specs=pl.BlockSpec((ts, C_IN), lambda i, q: (i, q[0]))),
        out_shape=jax.ShapeDtypeStruct((s, D_IN), BF16), input_output_aliases={len(deps) + 3: 0},
        compiler_params=_cparams(), name=name)(q_idx, *deps, h1, w_in_g, proj)


def _fwd_mix(proj, x, vecs, w_rga, w_rgx, w_out, ts, deps=()):
    s = x.shape[0]
    g = ts // TIME_BLOCKS

    def body(proj_ref, x_ref, v_ref, wa_ref, wx_ref, wo_ref, x1_ref, mg_ref, z1_ref, u_ref, h_ref, kept_ref, decay_ref,
             ua_buf, rx_buf, p_buf, q_buf, c_buf, hcarry):
        i = pl.program_id(0)

        @pl.when(i == 0)
        def _():
            ua_buf[...] = jnp.zeros(ua_buf.shape, F32)
            rx_buf[...] = jnp.zeros(rx_buf.shape, F32)
            hcarry[...] = jnp.zeros((8, D), F32)

        def seg(j):
            return proj_ref[:, j * D:(j + 1) * D].astype(F32)

        def vrow(j):
            return v_ref[j:j + 1, :]

        cb, cc, cx, rx, rg, ga, gb = (seg(j) for j in range(7))
        ua = cc * cx
        ua_late = _late_blocks(ua, ua_buf, g)
        rx_late = _late_blocks(rx, rx_buf, g)
        va = vrow(V_WA2) * ua + vrow(V_WA1) * _earlier(ua, 1, ua_late, g) + vrow(V_WA0) * _earlier(ua, 2, ua_late, g)
        u = (vrow(V_WB3) * rx + vrow(V_WB2) * _earlier(rx, 1, rx_late, g) + vrow(V_WB1) * _earlier(rx, 2, rx_late, g)
             + vrow(V_WB0) * _earlier(rx, 3, rx_late, g) + vrow(V_CBB))
        u_ref[...] = u

        rows = lax.broadcasted_iota(jnp.int32, (ts, D), 0)
        row0 = jnp.logical_and(rows == 0, i == 0)
        r, ig, a, mult = _lru_gates(u, wa_ref, wx_ref, v_ref, row0)
        decay_ref[:, 0:D] = a
        decay_ref[:, D:2 * D] = mult
        bx = mult * (ig * u)

        prods, sums = [a[0:g]], [bx[0:g]]
        for k in range(1, TIME_BLOCKS):
            ak = a[k * g:(k + 1) * g]
            sums.append(ak * sums[-1] + bx[k * g:(k + 1) * g])
            prods.append(ak * prods[-1])
        p_buf[...] = prods[-1]
        q_buf[...] = sums[-1]
        state = hcarry[0:1, :]
        for j in range(g):
            c_buf[j:j + 1, :] = state
            state = p_buf[j:j + 1, :] * state + q_buf[j:j + 1, :]
        hcarry[0:1, :] = state
        entering = c_buf[...]
        h = jnp.concatenate([sums[k] + prods[k] * entering for k in range(TIME_BLOCKS)], axis=0)
        h_ref[...] = h

        gel, dgel = _gelu(rg)
        sga = jax.nn.sigmoid(ga)
        sgb = jax.nn.sigmoid(gb)
        for j, keep in enumerate((va, r, ig, sga, sgb, gel, dgel)):
            kept_ref[:, j * D:(j + 1) * D] = keep.astype(BF16)
        merged = (sga * (cb * va) + sgb * (h * gel)).astype(BF16)
        mg_ref[...] = merged
        z1 = _dot(merged, wo_ref[...])
        z1_ref[...] = z1.astype(BF16)
        x1_ref[...] = x_ref[...] + vrow(V_GT1) * z1

    row = lambda i: (i, 0)
    return pl.pallas_call(
        _after(deps, body), grid=(s // ts,),
        out_shape=(jax.ShapeDtypeStruct((s, D), F32), jax.ShapeDtypeStruct((s, D), BF16), jax.ShapeDtypeStruct((s, D), BF16),
                   jax.ShapeDtypeStruct((s, D), F32), jax.ShapeDtypeStruct((s, D), F32), jax.ShapeDtypeStruct((s, 7 * D), BF16),
                   jax.ShapeDtypeStruct((s, 2 * D), F32)),
        in_specs=[_ANY] * len(deps) + [pl.BlockSpec((ts, D_IN), row), pl.BlockSpec((ts, D), row), _VMEM, _VMEM, _VMEM, _VMEM],
        out_specs=[pl.BlockSpec((ts, D), row)] * 5 + [pl.BlockSpec((ts, 7 * D), row), pl.BlockSpec((ts, 2 * D), row)],
        scratch_shapes=[pltpu.VMEM((2, g + 8, D), F32), pltpu.VMEM((3, g + 8, D), F32), pltpu.VMEM((g, D), F32),
                        pltpu.VMEM((g, D), F32), pltpu.VMEM((g, D), F32), pltpu.VMEM((8, D), F32)],
        compiler_params=_cparams(), name="fwd_mix")(*deps, proj, x, vecs, w_rga, w_rgx, w_out)


def _ffn_loss(x1, target, vecs, w_gu_g, w_dn, ts):
    s = x1.shape[0]

    def body(x1_ref, t_ref, v_ref, wgu_ref, wdn_ref, dx1_ref, h2_ref, act_ref, dz2_ref, dgu_ref, sm_ref):
        @pl.when(pl.program_id(0) == 0)
        def _():
            sm_ref[...] = jnp.zeros((N_SMALL, D), F32)

        def vrow(j):
            return v_ref[j:j + 1, :]

        n_sub = 1
        rows = [slice(k * (ts // n_sub), (k + 1) * (ts // n_sub)) for k in range(n_sub)]
        subs = [dict(r=r, sums={}) for r in rows]

        def stage_norm(t):
            t["x1"] = x1_ref[t["r"], :]
            t["xh1"], t["rstd1"] = _rms(t["x1"])
            t["n2"] = t["xh1"] * vrow(V_GFFN)
            t["h2"] = (t["n2"] * (1.0 + vrow(V_SC2)) + vrow(V_SH2)).astype(BF16)
            h2_ref[t["r"], :] = t["h2"]

        def stage_up(t):
            h2 = t["h2"]
            g = jnp.concatenate([_dot(h2, wgu_ref[0]), _dot(h2, wgu_ref[1])], axis=1)
            t["up"] = jnp.concatenate([_dot(h2, wgu_ref[2]), _dot(h2, wgu_ref[3])], axis=1)
            t["g"] = g
            t["sg"] = jax.nn.sigmoid(g)
            t["silu"] = g * t["sg"]
            t["act"] = (t["silu"] * t["up"]).astype(BF16)
            act_ref[t["r"], :] = t["act"]

        def stage_down_loss(t):
            z2 = _dot(t["act"], wdn_ref[...])
            x2 = t["x1"] + vrow(V_GT2) * z2
            xh2, rstd2 = _rms(x2)
            err = xh2 * vrow(V_GFIN) - t_ref[t["r"], :]
            t["sums"][G_LOSS] = _rowsum((0.5 / D) * err * err)
            dy = err * (1.0 / D)
            t["sums"][G_GFIN] = _rowsum(dy * xh2)
            t["dx2"] = _rms_bwd(dy * vrow(V_GFIN), xh2, rstd2)
            t["sums"][G_GT2] = _rowsum(t["dx2"] * z2)
            t["dz2"] = (vrow(V_GT2) * t["dx2"]).astype(BF16)
            dz2_ref[t["r"], :] = t["dz2"]

        def stage_back_act(t):
            dact = _dot_nt(t["dz2"], wdn_ref[...])
            g, sg = t["g"], t["sg"]
            t["dgate"] = (dact * t["up"] * (sg * (1.0 + g * (1.0 - sg)))).astype(BF16)
            t["dup"] = (dact * t["silu"]).astype(BF16)
            dgu_ref[t["r"], 0:D_FF] = t["dgate"]
            dgu_ref[t["r"], D_FF:2 * D_FF] = t["dup"]

        def stage_back_norm(t):
            dgate, dup = t["dgate"], t["dup"]
            dh2 = (_dot_nt(dgate[:, 0:C_GU], wgu_ref[0]) + _dot_nt(dgate[:, C_GU:2 * C_GU], wgu_ref[1])
                   + _dot_nt(dup[:, 0:C_GU], wgu_ref[2]) + _dot_nt(dup[:, C_GU:2 * C_GU], wgu_ref[3]))
            t["sums"][G_SH2] = _rowsum(dh2)
            t["sums"][G_SC2] = _rowsum(dh2 * t["n2"])
            dn2 = dh2 * (1.0 + vrow(V_SC2))
            t["sums"][G_GFFN] = _rowsum(dn2 * t["xh1"])
            dx1_ref[t["r"], :] = t["dx2"] + _rms_bwd(dn2 * vrow(V_GFFN), t["xh1"], t["rstd1"])

        for stage in (stage_norm, stage_up, stage_down_loss, stage_back_act, stage_back_norm):
            for t in subs:
                stage(t)
        for j in subs[0]["sums"]:
            total = subs[0]["sums"][j]
            for t in subs[1:]:
                total = total + t["sums"][j]
            sm_ref[j:j + 1, :] += total

    row = lambda i: (i, 0)
    return pl.pallas_call(
        body, grid=(s // ts,),
        out_shape=(jax.ShapeDtypeStruct((s, D), F32), jax.ShapeDtypeStruct((s, D), BF16), jax.ShapeDtypeStruct((s, D_FF), BF16),
                   jax.ShapeDtypeStruct((s, D), BF16), jax.ShapeDtypeStruct((s, 2 * D_FF), BF16),
                   jax.ShapeDtypeStruct((N_SMALL, D), F32)),
        in_specs=[pl.BlockSpec((ts, D), row), pl.BlockSpec((ts, D), row), _VMEM, _VMEM, _VMEM],
        out_specs=[pl.BlockSpec((ts, D), row), pl.BlockSpec((ts, D), row), pl.BlockSpec((ts, D_FF), row),
                   pl.BlockSpec((ts, D), row), pl.BlockSpec((ts, 2 * D_FF), row), pl.BlockSpec((N_SMALL, D), lambda i: (0, 0))],
        compiler_params=_cparams(), name="ffn_loss")(x1, target, vecs, w_gu_g, w_dn)


def _bwd_mix(dx1, z1, merged, proj, u, h, kept, decay, vecs, w_rga, w_rgx, w_out, small, ts, deps=()):
    s = dx1.shape[0]
    nt = s // ts
    g = ts // TIME_BLOCKS
    assert g % 16 == 0

    def body(dx1_ref, z1_ref, mg_ref, proj_ref, u_ref, h_ref, kept_ref, decay_ref, hh_ref, v_ref, wa_ref, wx_ref,
             wo_ref, sm0_ref, dproj_ref, sm_ref, dwa_ref, dwx_ref, dwo_ref,
             h_buf, a_buf, dva_buf, du_buf, p_buf, q_buf, c_buf, lcarry):
        i = pl.program_id(0)
        first_tile = i == nt - 1

        @pl.when(i == 0)
        def _():
            a_buf[...] = jnp.zeros(a_buf.shape, F32)
            dva_buf[...] = jnp.zeros(dva_buf.shape, F32)
            du_buf[...] = jnp.zeros(du_buf.shape, F32)
            lcarry[...] = jnp.zeros((8, D), F32)
            sm_ref[...] = sm0_ref[...]
            dwa_ref[...] = jnp.zeros((HEADS, HB, HB), F32)
            dwx_ref[...] = jnp.zeros((HEADS, HB, HB), F32)
            dwo_ref[...] = jnp.zeros((D, D), F32)

        def seg(j):
            return proj_ref[:, j * D:(j + 1) * D].astype(F32)

        def vrow(j):
            return v_ref[j:j + 1, :]

        def acc(j, val):
            sm_ref[j:j + 1, :] += _rowsum(val)

        cb, cc, cx, rx = (seg(j) for j in range(4))
        ua = cc * cx
        va, r, ig, sga, sgb, gel, dgel = (kept_ref[:, j * D:(j + 1) * D].astype(F32) for j in range(7))
        a = decay_ref[:, 0:D]
        mult = decay_ref[:, D:2 * D]
        u = u_ref[...]
        h = h_ref[...]
        rows = lax.broadcasted_iota(jnp.int32, (ts, D), 0)
        row0 = jnp.logical_and(rows == 0, first_tile)

        dx1 = dx1_ref[...]
        acc(G_GT1, dx1 * z1_ref[...].astype(F32))
        dz1 = (vrow(V_GT1) * dx1).astype(BF16)
        dwo_ref[...] += _dot_tn(mg_ref[...], dz1)
        dmg = _dot_nt(dz1, wo_ref[...])
        dya = dmg * sga
        dyb = dmg * sgb
        dproj_ref[:, 5 * D:6 * D] = (dya * (cb * va) * (1.0 - sga)).astype(BF16)
        dproj_ref[:, 6 * D:7 * D] = (dyb * (h * gel) * (1.0 - sgb)).astype(BF16)

        dproj_ref[:, 0:D] = (dya * va).astype(BF16)
        dva = dya * cb
        dva_early = _early_blocks(dva, dva_buf, g)
        dva1 = _later(dva, 1, dva_early, g)
        dva2 = _later(dva, 2, dva_early, g)
        dua = vrow(V_WA2) * dva + vrow(V_WA1) * dva1 + vrow(V_WA0) * dva2
        acc(G_WA2, ua * dva)
        acc(G_WA1, ua * dva1)
        acc(G_WA0, ua * dva2)
        dproj_ref[:, D:2 * D] = (dua * cx).astype(BF16)
        dproj_ref[:, 2 * D:3 * D] = (dua * cc).astype(BF16)

        dproj_ref[:, 4 * D:5 * D] = (dyb * h * dgel).astype(BF16)
        a_next = _later(a, 1, _early_blocks(a, a_buf, g), g)
        dh = dyb * gel
        last = TIME_BLOCKS - 1
        prods, sums = {last: a_next[last * g:]}, {last: dh[last * g:]}
        for k in range(last - 1, -1, -1):
            ak = a_next[k * g:(k + 1) * g]
            sums[k] = dh[k * g:(k + 1) * g] + ak * sums[k + 1]
            prods[k] = ak * prods[k + 1]
        p_buf[...] = prods[0]
        q_buf[...] = sums[0]
        state = lcarry[0:1, :]
        for j in range(g - 1, -1, -1):
            c_buf[j:j + 1, :] = state
            state = q_buf[j:j + 1, :] + p_buf[j:j + 1, :] * state
        lcarry[0:1, :] = state
        entering = c_buf[...]
        lam = jnp.concatenate([sums[k] + prods[k] * entering for k in range(TIME_BLOCKS)], axis=0)

        h_halo = [jnp.where(first_tile, 0.0, hh_ref[7:8, :])]
        da = lam * _earlier(h, 1, _late_blocks(h, h_buf, g, h_halo), g)
        dmult = jnp.where(row0, 0.0, lam * (ig * u))
        di = lam * mult * u
        du = lam * mult * ig
        dlog_a = da * a - dmult * (a * a) / mult
        lam_p = vrow(V_LAM)
        dr = dlog_a * (LRU_C * _log_sigmoid(lam_p))
        sm_ref[G_LAM:G_LAM + 1, :] += _rowsum(dlog_a * r) * (LRU_C * jax.nn.sigmoid(-lam_p))
        dpa = dr * r * (1.0 - r)
        dpx = di * ig * (1.0 - ig)
        acc(G_BA, dpa)
        acc(G_BX, dpx)
        dpab = dpa.astype(BF16)
        dpxb = dpx.astype(BF16)
        ub = u.astype(BF16)
        back = []
        for hd in range(HEADS):
            cols = slice(hd * HB, (hd + 1) * HB)
            back.append(_dot_nt(dpab[:, cols], wa_ref[hd]) + _dot_nt(dpxb[:, cols], wx_ref[hd]))
            dwa_ref[hd] += _dot_tn(ub[:, cols], dpab[:, cols])
            dwx_ref[hd] += _dot_tn(ub[:, cols], dpxb[:, cols])
        du = du + jnp.concatenate(back, axis=1)

        acc(G_CBB, du)
        du_early = _early_blocks(du, du_buf, g)
        du1 = _later(du, 1, du_early, g)
        du2 = _later(du, 2, du_early, g)
        du3 = _later(du, 3, du_early, g)
        dproj_ref[:, 3 * D:4 * D] = (vrow(V_WB3) * du + vrow(V_WB2) * du1 + vrow(V_WB1) * du2 + vrow(V_WB0) * du3).astype(BF16)
        acc(G_WB3, rx * du)
        acc(G_WB2, rx * du1)
        acc(G_WB1, rx * du2)
        acc(G_WB0, rx * du3)

    rev = lambda i: (nt - 1 - i, 0)
    halo8 = lambda i: (jnp.maximum((nt - 1 - i) * (ts // 8) - 1, 0), 0)
    const2 = lambda i: (0, 0)
    const3 = lambda i: (0, 0, 0)
    return pl.pallas_call(
        _after(deps, body), grid=(nt,),
        out_shape=(jax.ShapeDtypeStruct((s, D_IN), BF16), jax.ShapeDtypeStruct((N_SMALL, D), F32),
                   jax.ShapeDtypeStruct((HEADS, HB, HB), F32), jax.ShapeDtypeStruct((HEADS, HB, HB), F32),
                   jax.ShapeDtypeStruct((D, D), F32)),
        in_specs=[_ANY] * len(deps) + [pl.BlockSpec((ts, D), rev), pl.BlockSpec((ts, D), rev), pl.BlockSpec((ts, D), rev),
                  pl.BlockSpec((ts, 4 * D), rev), pl.BlockSpec((ts, D), rev), pl.BlockSpec((ts, D), rev),
                  pl.BlockSpec((ts, 7 * D), rev), pl.BlockSpec((ts, 2 * D), rev), pl.BlockSpec((8, D), halo8),
                  _VMEM, _VMEM, _VMEM, _VMEM, _VMEM],
        out_specs=[pl.BlockSpec((ts, D_IN), rev), pl.BlockSpec((N_SMALL, D), const2),
                   pl.BlockSpec((HEADS, HB, HB), const3), pl.BlockSpec((HEADS, HB, HB), const3), pl.BlockSpec((D, D), const2)],
        scratch_shapes=[pltpu.VMEM((1, g + 8, D), F32), pltpu.VMEM((1, g + 8, D), F32),
                        pltpu.VMEM((2, g + 8, D), F32), pltpu.VMEM((3, g + 8, D), F32), pltpu.VMEM((g, D), F32),
                        pltpu.VMEM((g, D), F32), pltpu.VMEM((g, D), F32), pltpu.VMEM((8, D), F32)],
        compiler_params=_cparams(), name="bwd_mix")(*deps, dx1, z1, merged, proj, u, h, kept, decay, h, vecs, w_rga,
                                                    w_rgx, w_out, small)


def _bwd_in(dproj, x, dx1, vecs, w_in_g, small, ts, deps=()):
    s = x.shape[0]

    def body(dp_ref, x_ref, dx1_ref, v_ref, w_ref, sm0_ref, gx_ref, sm_ref):
        @pl.when(pl.program_id(0) == 0)
        def _():
            sm_ref[...] = sm0_ref[...]

        def vrow(j):
            return v_ref[j:j + 1, :]

        dh1 = _dot_nt(dp_ref[:, 0:C_IN], w_ref[0])
        for k in range(1, N_CHIPS):
            dh1 += _dot_nt(dp_ref[:, k * C_IN:(k + 1) * C_IN], w_ref[k])
        xh, rstd = _rms(x_ref[...])
        sm_ref[G_SH1:G_SH1 + 1, :] += _rowsum(dh1)
        sm_ref[G_SC1:G_SC1 + 1, :] += _rowsum(dh1 * (xh * vrow(V_GMIX)))
        dn1 = dh1 * (1.0 + vrow(V_SC1))
        sm_ref[G_GMIX:G_GMIX + 1, :] += _rowsum(dn1 * xh)
        gx_ref[...] = dx1_ref[...] + _rms_bwd(dn1 * vrow(V_GMIX), xh, rstd)

    row = lambda i: (i, 0)
    return pl.pallas_call(
        _after(deps, body), grid=(s // ts,),
        out_shape=(jax.ShapeDtypeStruct((s, D), F32), jax.ShapeDtypeStruct((N_SMALL, D), F32)),
        in_specs=[_ANY] * len(deps) + [pl.BlockSpec((ts, D_IN), row), pl.BlockSpec((ts, D), row), pl.BlockSpec((ts, D), row),
                                       _VMEM, _VMEM, _VMEM],
        out_specs=[pl.BlockSpec((ts, D), row), pl.BlockSpec((N_SMALL, D), lambda i: (0, 0))],
        compiler_params=_cparams(), name="bwd_in")(*deps, dproj, x, dx1, vecs, w_in_g, small)


def _grad_w(a, b, n_col_blocks, ts, name, deps=()):
    s, m = a.shape
    tn = b.shape[1] // n_col_blocks
    n_steps = s // ts

    def body(a_ref, b_ref, o_ref, acc_ref):
        k = pl.program_id(1)

        @pl.when(k == 0)
        def _():
            acc_ref[...] = jnp.zeros((m, tn), F32)

        acc_ref[...] += _dot_tn(a_ref[...], b_ref[...])

        @pl.when(k == n_steps - 1)
        def _():
            o_ref[...] = acc_ref[...].astype(BF16)

    return pl.pallas_call(
        _after(deps, body), grid=(n_col_blocks, n_steps),
        out_shape=jax.ShapeDtypeStruct((n_col_blocks, m, tn), BF16),
        in_specs=[_ANY] * len(deps) + [pl.BlockSpec((ts, m), lambda n, k: (k, 0)), pl.BlockSpec((ts, tn), lambda n, k: (k, n))],
        out_specs=pl.BlockSpec((None, m, tn), lambda n, k: (n, 0, 0)),
        scratch_shapes=[pltpu.VMEM((m, tn), F32)],
        compiler_params=_cparams(2), name=name)(*deps, a, b)


def _ada_fwd(c_all, w_ada, b_ada):
    n = w_ada.shape[1]

    def body(c_ref, w_ref, b_ref, o_ref, ca_ref):
        c = c_ref[...]
        ca = c * jax.nn.sigmoid(c)
        ca_ref[...] = ca
        o_ref[...] = jnp.dot(ca, w_ref[...], preferred_element_type=F32, precision=lax.Precision.HIGHEST) + b_ref[...]

    return pl.pallas_call(
        body, out_shape=(jax.ShapeDtypeStruct((N_DEV, n), F32), jax.ShapeDtypeStruct((N_DEV, D), F32)),
        in_specs=[_VMEM] * 3, out_specs=[_VMEM] * 2, compiler_params=_cparams(0), name="ada_fwd")(c_all, w_ada, b_ada)


def _ada_bwd(c_act, dmod):
    n = dmod.shape[1]

    def body(c_ref, d_ref, o_ref):
        o_ref[...] = lax.dot_general(c_ref[...], d_ref[...], (((0,), (0,)), ((), ())), preferred_element_type=F32,
                                     precision=lax.Precision.HIGHEST)

    return pl.pallas_call(
        body, out_shape=jax.ShapeDtypeStruct((D, n), F32), in_specs=[_VMEM] * 2, out_specs=_VMEM,
        compiler_params=_cparams(0), name="ada_bwd")(c_act, dmod)


def _sum_small(parts):
    def body(p_ref, o_ref, d_ref):
        tot = p_ref[0]
        for dev in range(1, N_DEV):
            tot = tot + p_ref[dev]
        o_ref[...] = tot
        d_ref[...] = p_ref[:, 0:8, :]

    return pl.pallas_call(
        body, out_shape=(jax.ShapeDtypeStruct((N_SMALL, D), F32), jax.ShapeDtypeStruct((N_DEV, 8, D), F32)),
        in_specs=[_VMEM], out_specs=[_VMEM] * 2, compiler_params=_cparams(0), name="sum_small")(parts)


def _adamw(w, g, m, v, name, deps=()):
    rows, cols = w.shape
    tr = 128 if rows % 128 == 0 else (64 if rows % 64 == 0 else rows)

    def body(w_ref, g_ref, m_ref, v_ref, d_ref, nm_ref, nv_ref):
        g_ = g_ref[...]
        m_ = ADAM_B1 * m_ref[...] + (1.0 - ADAM_B1) * g_
        v_ = ADAM_B2 * v_ref[...] + (1.0 - ADAM_B2) * (g_ * g_)
        nm_ref[...] = m_
        nv_ref[...] = v_
        m_hat = m_ / (1.0 - ADAM_B1 ** ADAM_STEP)
        v_hat = v_ / (1.0 - ADAM_B2 ** ADAM_STEP)
        d_ref[...] = -ADAM_LR * (m_hat / (jnp.sqrt(v_hat) + ADAM_EPS) + ADAM_WD * w_ref[...])

    spec = pl.BlockSpec((tr, cols), lambda i: (i, 0))
    return pl.pallas_call(
        _after(deps, body), grid=(rows // tr,), out_shape=(jax.ShapeDtypeStruct((rows, cols), F32),) * 3,
        in_specs=[_ANY] * len(deps) + [spec] * 4, out_specs=[spec] * 3, compiler_params=_cparams(), name=name)(*deps, w, g, m, v)


def _adamw_small(items, name):
    n = len(items)

    def body(*refs):
        ins, outs = refs[:4 * n], refs[4 * n:]
        for k in range(n):
            w_ref, g_ref, m_ref, v_ref = ins[4 * k:4 * k + 4]
            d_ref, nm_ref, nv_ref = outs[3 * k:3 * k + 3]
            g_ = g_ref[...]
            m_ = ADAM_B1 * m_ref[...] + (1.0 - ADAM_B1) * g_
            v_ = ADAM_B2 * v_ref[...] + (1.0 - ADAM_B2) * (g_ * g_)
            nm_ref[...] = m_
            nv_ref[...] = v_
            m_hat = m_ / (1.0 - ADAM_B1 ** ADAM_STEP)
            v_hat = v_ / (1.0 - ADAM_B2 ** ADAM_STEP)
            d_ref[...] = -ADAM_LR * (m_hat / (jnp.sqrt(v_hat) + ADAM_EPS) + ADAM_WD * w_ref[...])

    out = pl.pallas_call(
        body, out_shape=tuple(jax.ShapeDtypeStruct(it[0].shape, F32) for it in items for _ in range(3)),
        in_specs=[_VMEM] * (4 * n), out_specs=[_VMEM] * (3 * n), name=name)(*[a for it in items for a in it])
    return [tuple(out[3 * k:3 * k + 3]) for k in range(n)]


def _adamw_halves(w, mine, other, m, v, c_idx, name, deps=()):
    r2, cols = mine.shape
    tr = next(t for t in (128, 64, 32, 16, 8) if r2 % t == 0)
    nh = r2 // tr

    def body(c_ref, w_ref, mine_ref, other_ref, m_ref, v_ref, g_ref, d_ref, nm_ref, nv_ref):
        g_ = jnp.where(pl.program_id(0) // nh == c_ref[0], mine_ref[...], other_ref[...])
        g_ref[...] = g_
        m_ = ADAM_B1 * m_ref[...] + (1.0 - ADAM_B1) * g_
        v_ = ADAM_B2 * v_ref[...] + (1.0 - ADAM_B2) * (g_ * g_)
        nm_ref[...] = m_
        nv_ref[...] = v_
        m_hat = m_ / (1.0 - ADAM_B1 ** ADAM_STEP)
        v_hat = v_ / (1.0 - ADAM_B2 ** ADAM_STEP)
        d_ref[...] = -ADAM_LR * (m_hat / (jnp.sqrt(v_hat) + ADAM_EPS) + ADAM_WD * w_ref[...])

    full = pl.BlockSpec((tr, cols), lambda i, c: (i, 0))
    mine_spec = pl.BlockSpec((tr, cols), lambda i, c: (jnp.clip(i - c[0] * nh, 0, nh - 1), 0))
    other_spec = pl.BlockSpec((tr, cols), lambda i, c: (jnp.clip(i - (1 - c[0]) * nh, 0, nh - 1), 0))
    return pl.pallas_call(
        lambda c_ref, *refs: body(c_ref, *refs[len(deps):]),
        grid_spec=pltpu.PrefetchScalarGridSpec(
            num_scalar_prefetch=1, grid=(2 * nh,),
            in_specs=[_ANY] * len(deps) + [full, mine_spec, other_spec, full, full], out_specs=[full] * 4),
        out_shape=(jax.ShapeDtypeStruct((2 * r2, cols), F32),) * 4, compiler_params=_cparams(), name=name,
    )(c_idx, *deps, w, mine, other, m, v)


def _add_halves(g, recv, c_idx, name):
    n, _, r2, cols = g.shape

    def body(c_ref, g_ref, r_ref, o_ref):
        o_ref[...] = (g_ref[...].astype(F32) + r_ref[...].astype(F32)).astype(BF16)

    return pl.pallas_call(
        body,
        grid_spec=pltpu.PrefetchScalarGridSpec(
            num_scalar_prefetch=1, grid=(n,),
            in_specs=[pl.BlockSpec((None, None, r2, cols), lambda k, c: (k, c[0], 0, 0)),
                      pl.BlockSpec((None, r2, cols), lambda k, c: (k, 0, 0))],
            out_specs=pl.BlockSpec((None, r2, cols), lambda k, c: (k, 0, 0))),
        out_shape=jax.ShapeDtypeStruct((n, r2, cols), BF16), compiler_params=_cparams(), name=name)(c_idx, g, recv)


def _sum_chips(parts, name):
    n, r2, cols = parts.shape
    tr = next(t for t in (64, 32, 16) if r2 % t == 0)

    def body(p_ref, o_ref):
        o_ref[...] = ((p_ref[0].astype(F32) + p_ref[1].astype(F32)) + p_ref[2].astype(F32)) + p_ref[3].astype(F32)

    return pl.pallas_call(
        body, grid=(r2 // tr,), out_shape=jax.ShapeDtypeStruct((r2, cols), F32),
        in_specs=[pl.BlockSpec((n, tr, cols), lambda i: (0, i, 0))], out_specs=pl.BlockSpec((tr, cols), lambda i: (i, 0)),
        compiler_params=_cparams(), name=name)(parts)


def _place():
    x, y, c = lax.axis_index("x"), lax.axis_index("y"), lax.axis_index("c")
    return x, y, c, 2 * x + y


def _flip(v, bit):
    return 1 - v if bit else v


def _allgather8(v, name, deps=()):
    r, n = v.shape

    def body(*refs):
        v_ref, out_ref, send_sems, recv_sems, local_sem = refs[len(deps):]
        x, y, c, _ = _place()
        me = 4 * x + 2 * y + c
        mine = pltpu.make_async_copy(v_ref, out_ref.at[me], local_sem)
        mine.start()
        sends = []
        for rel in range(1, N_DEV):
            peer = (_flip(x, rel & 4), _flip(y, rel & 2), _flip(c, rel & 1))
            cp = pltpu.make_async_remote_copy(v_ref, out_ref.at[me], send_sems.at[rel - 1], recv_sems.at[rel - 1],
                                              device_id=peer, device_id_type=MESH)
            cp.start()
            sends.append(cp)
        for rel in range(1, N_DEV):
            peer = (_flip(x, rel & 4), _flip(y, rel & 2), _flip(c, rel & 1))
            peer_idx = 4 * peer[0] + 2 * peer[1] + peer[2]
            pltpu.make_async_remote_copy(v_ref, out_ref.at[peer_idx], send_sems.at[rel - 1], recv_sems.at[rel - 1],
                                         device_id=peer, device_id_type=MESH).wait_recv()
        for cp in sends:
            cp.wait_send()
        mine.wait()

    return pl.pallas_call(
        body, out_shape=jax.ShapeDtypeStruct((N_DEV, r, n), F32), in_specs=[_ANY] * len(deps) + [_VMEM], out_specs=_VMEM,
        scratch_shapes=[pltpu.SemaphoreType.DMA((N_DEV - 1,)), pltpu.SemaphoreType.DMA((N_DEV - 1,)), pltpu.SemaphoreType.DMA(())],
        name=name)(*deps, v)


def _gather_weights(shards):
    nw = len(shards)

    def body(*refs):
        w_refs, out_refs = refs[:nw], refs[nw:2 * nw]
        send_sems, recv_sems = refs[2 * nw:]
        x, y, c, p = _place()
        sibling = (x, y, 1 - c)
        sends = []
        for j in range(1, N_CHIPS):
            peer = (_flip(x, j & 2), _flip(y, j & 1), c)
            for w in range(nw):
                cp = pltpu.make_async_remote_copy(w_refs[w].at[c], out_refs[w].at[p, c], send_sems.at[w * 6 + j - 1],
                                                  recv_sems.at[w * 6 + j - 1], device_id=peer, device_id_type=MESH)
                cp.start()
                sends.append(cp)
        for j in range(1, N_CHIPS):
            peer = (_flip(x, j & 2), _flip(y, j & 1), c)
            q = 2 * peer[0] + peer[1]
            for w in range(nw):
                pltpu.make_async_remote_copy(w_refs[w].at[c], out_refs[w].at[q, c], send_sems.at[w * 6 + j - 1],
                                             recv_sems.at[w * 6 + j - 1], device_id=peer, device_id_type=MESH).wait_recv()
                cp = pltpu.make_async_remote_copy(out_refs[w].at[q, c], out_refs[w].at[q, c], send_sems.at[w * 6 + 2 + j],
                                                  recv_sems.at[w * 6 + 2 + j], device_id=sibling, device_id_type=MESH)
                cp.start()
                sends.append(cp)
        for j in range(1, N_CHIPS):
            q = 2 * _flip(x, j & 2) + _flip(y, j & 1)
            for w in range(nw):
                pltpu.make_async_remote_copy(out_refs[w].at[q, 1 - c], out_refs[w].at[q, 1 - c], send_sems.at[w * 6 + 2 + j],
                                             recv_sems.at[w * 6 + 2 + j], device_id=sibling, device_id_type=MESH).wait_recv()
        for cp in sends:
            cp.wait_send()

    return pl.pallas_call(
        body, out_shape=tuple(jax.ShapeDtypeStruct((N_CHIPS,) + s.shape, s.dtype) for s in shards),
        in_specs=[_ANY] * nw, out_specs=[_ANY] * nw,
        scratch_shapes=[pltpu.SemaphoreType.DMA((6 * nw,)), pltpu.SemaphoreType.DMA((6 * nw,))],
        name="gather_weights")(*shards)


def _swap_halves(grads):
    nw = len(grads)

    def body(*refs):
        g_refs, out_refs = refs[:nw], refs[nw:2 * nw]
        send_sems, recv_sems = refs[2 * nw:]
        x, y, c, _ = _place()
        sibling = (x, y, 1 - c)
        sends = []
        for w in range(nw):
            for k in range(N_CHIPS):
                cp = pltpu.make_async_remote_copy(g_refs[w].at[k, 1 - c], out_refs[w].at[k], send_sems.at[w * N_CHIPS + k],
                                                  recv_sems.at[w * N_CHIPS + k], device_id=sibling, device_id_type=MESH)
                cp.start()
                sends.append(cp)
        for cp in sends:
            cp.wait_recv()
        for cp in sends:
            cp.wait_send()

    return pl.pallas_call(
        body, out_shape=tuple(jax.ShapeDtypeStruct((N_CHIPS,) + g.shape[2:], g.dtype) for g in grads),
        in_specs=[_ANY] * nw, out_specs=[_ANY] * nw,
        scratch_shapes=[pltpu.SemaphoreType.DMA((N_CHIPS * nw,)), pltpu.SemaphoreType.DMA((N_CHIPS * nw,))],
        name="swap_halves")(*grads)


def _scatter_chips(parts):
    nw = len(parts)

    def body(*refs):
        p_refs, out_refs = refs[:nw], refs[nw:2 * nw]
        send_sems, recv_sems = refs[2 * nw:]
        x, y, c, p = _place()
        sends = []
        for j in range(1, N_CHIPS):
            peer = (_flip(x, j & 2), _flip(y, j & 1), c)
            q = 2 * peer[0] + peer[1]
            for w in range(nw):
                cp = pltpu.make_async_remote_copy(p_refs[w].at[q], out_refs[w].at[p], send_sems.at[w * 3 + j - 1],
                                                  recv_sems.at[w * 3 + j - 1], device_id=peer, device_id_type=MESH)
                cp.start()
                sends.append(cp)
        for j in range(1, N_CHIPS):
            peer = (_flip(x, j & 2), _flip(y, j & 1), c)
            q = 2 * peer[0] + peer[1]
            for w in range(nw):
                pltpu.make_async_remote_copy(p_refs[w].at[q], out_refs[w].at[q], send_sems.at[w * 3 + j - 1],
                                             recv_sems.at[w * 3 + j - 1], device_id=peer, device_id_type=MESH).wait_recv()
        for cp in sends:
            cp.wait_send()

    return pl.pallas_call(
        body, out_shape=tuple(jax.ShapeDtypeStruct(s.shape, s.dtype) for s in parts),
        in_specs=[_ANY] * nw, out_specs=[_ANY] * nw,
        scratch_shapes=[pltpu.SemaphoreType.DMA((3 * nw,)), pltpu.SemaphoreType.DMA((3 * nw,))],
        name="scatter_chips")(*parts)


def _share_halves(halves):
    nw = len(halves)

    def body(*refs):
        h_refs, out_refs = refs[:nw], refs[nw:2 * nw]
        send_sems, recv_sems = refs[2 * nw:]
        x, y, c, _ = _place()
        sends = []
        for w in range(nw):
            cp = pltpu.make_async_remote_copy(h_refs[w], out_refs[w], send_sems.at[w], recv_sems.at[w],
                                              device_id=(x, y, 1 - c), device_id_type=MESH)
            cp.start()
            sends.append(cp)
        for cp in sends:
            cp.wait_recv()
        for cp in sends:
            cp.wait_send()

    return pl.pallas_call(
        body, out_shape=tuple(jax.ShapeDtypeStruct(s.shape, s.dtype) for s in halves),
        in_specs=[_ANY] * nw, out_specs=[_ANY] * nw,
        scratch_shapes=[pltpu.SemaphoreType.DMA((nw,)), pltpu.SemaphoreType.DMA((nw,))],
        name="share_halves")(*halves)


_HBM = pl.BlockSpec(memory_space=pltpu.HBM)
_SEM = pl.BlockSpec(memory_space=pltpu.SEMAPHORE)
_EFFECT = pltpu.SideEffectType.DATAFLOW_SIDE_EFFECTING


def _xchg_start(name, plan, n_copies, srcs, lands, after=()):
    bufs = list(srcs) + list(lands)
    ns, nb = len(srcs), len(srcs) + len(lands)

    def body(*refs):
        send_sems, recv_sems, token = refs[nb + len(after)], refs[nb + len(after) + 1], refs[-1]
        for i, (src, dst, peer, _) in enumerate(plan(_place(), refs[:ns], refs[ns:nb])):
            pltpu.make_async_remote_copy(src, dst, send_sems.at[i], recv_sems.at[i], device_id=peer, device_id_type=MESH).start()
        token[...] = jnp.zeros_like(token)

    out = pl.pallas_call(
        body, name=name,
        out_shape=(pltpu.SemaphoreType.DMA((n_copies,)), pltpu.SemaphoreType.DMA((n_copies,)),
                   *[pltpu.HBM(a.shape, a.dtype) for a in bufs], jax.ShapeDtypeStruct((8, 128), F32)),
        in_specs=[_HBM] * nb + [_ANY] * len(after), out_specs=(_SEM, _SEM, *[_HBM] * nb, _VMEM),
        input_output_aliases={i: 2 + i for i in range(nb)},
        compiler_params=pltpu.CompilerParams(has_side_effects=_EFFECT),
    )(*[pltpu.with_memory_space_constraint(a, pltpu.HBM) for a in bufs], *after)
    return (out[0], out[1]), out[2:2 + ns], out[2 + ns:2 + nb], out[-1]


def _xchg_wait(name, plan, sems, srcs, lands, after, sem_ids=None):
    bufs = list(srcs) + list(lands)
    ns, nb = len(srcs), len(srcs) + len(lands)

    def body(*refs):
        send_sems, recv_sems = refs[nb], refs[nb + 1]
        copies = plan(_place(), refs[:ns], refs[ns:nb])
        ids = range(len(copies)) if sem_ids is None else sem_ids
        for i, (src, _, peer, mine) in zip(ids, copies, strict=True):
            if i is not None:
                cp = pltpu.make_async_remote_copy(src, mine, send_sems.at[i], recv_sems.at[i], device_id=peer,
                                                  device_id_type=MESH)
                cp.wait_send()
                cp.wait_recv()

    out = pl.pallas_call(
        body, name=name, out_shape=tuple(pltpu.HBM(a.shape, a.dtype) for a in bufs),
        in_specs=[_HBM] * nb + [_SEM, _SEM] + [_ANY] * len(after), out_specs=tuple([_HBM] * nb),
        input_output_aliases={i: i for i in range(nb)},
        compiler_params=pltpu.CompilerParams(has_side_effects=_EFFECT),
    )(*bufs, *sems, *after)
    return out[:ns], out[ns:]


def _other_chips(place, which=(1, 2, 3)):
    x, y, c, _ = place
    return [((_flip(x, j & 2), _flip(y, j & 1), c), 2 * _flip(x, j & 2) + _flip(y, j & 1)) for j in which]


def _plan_gather_ici(place, src_refs, land_refs):
    _, _, c, p = place
    return [(s.at[c], l.at[p, c], peer, l.at[q, c]) for s, l in zip(src_refs, land_refs) for peer, q in _other_chips(place)]


def _plan_relay(which):
    def plan(place, src_refs, land_refs):
        x, y, c, _ = place
        return [(l.at[q, c], l.at[q, c], (x, y, 1 - c), l.at[q, 1 - c]) for l in land_refs for _, q in _other_chips(place, which)]
    return plan


def _plan_swap(place, src_refs, land_refs):
    x, y, c, _ = place
    return [(s.at[k, 1 - c], l.at[k], (x, y, 1 - c), l.at[k]) for s, l in zip(src_refs, land_refs) for k in range(N_CHIPS)]


def _plan_scatter(place, src_refs, land_refs):
    _, _, _, p = place
    return [(s.at[q], l.at[p], peer, l.at[q]) for s, l in zip(src_refs, land_refs) for peer, q in _other_chips(place)]


def _plan_share(place, src_refs, land_refs):
    x, y, c, _ = place
    return [(s, l, (x, y, 1 - c), l) for s, l in zip(src_refs, land_refs)]


def _pack_rows(parts, n_rows, name, deps=()):
    def body(*refs):
        refs = refs[len(deps):]
        out_ref = refs[-1]
        out_ref[...] = jnp.zeros((n_rows, D), F32)
        at = 0
        for ref in refs[:-1]:
            k = ref.shape[0]
            out_ref[at:at + k, :] = ref[...]
            at += k

    return pl.pallas_call(
        body, out_shape=jax.ShapeDtypeStruct((n_rows, D), F32), in_specs=[_ANY] * len(deps) + [_VMEM] * len(parts),
        out_specs=_VMEM, name=name)(*deps, *parts)


TS_MM = 512
TS_IN = 1024
TS_GW = 1024
TS_MIX = 256


def _halved(a):
    n, r, cols = a.shape
    return a.reshape(n, 2, r // 2, cols)


def _rs_swap(name, grads, after=()):
    lands = [lax.empty((N_CHIPS,) + g.shape[2:], g.dtype) for g in grads]
    sems, grads, lands, token = _xchg_start(name + "_swap", _plan_swap, N_CHIPS * len(grads), grads, lands, after)
    return name, sems, grads, lands, token


def _rs_scatter(handle, after, chip, ci):
    name, sems, grads, lands, _ = handle
    grads, from_sibling = _xchg_wait(name + "_swap_wait", _plan_swap, sems, grads, lands, after)
    c_arr = jnp.reshape(ci, (1,)).astype(jnp.int32)
    pair_sums = [_add_halves(g, r, c_arr, "%s_add_halves_%d" % (name, k)) for k, (g, r) in enumerate(zip(grads, from_sibling))]
    lands = [lax.dynamic_update_index_in_dim(lax.empty(p.shape, p.dtype), lax.dynamic_index_in_dim(p, chip, 0, keepdims=False),
                                             chip, 0) for p in pair_sums]
    sems, pair_sums, lands, token = _xchg_start(name + "_scatter", _plan_scatter, 3 * len(pair_sums), pair_sums, lands)
    return name, sems, pair_sums, lands, token


def _rs_share(handle, after):
    name, sems, pair_sums, lands, _ = handle
    _, by_chip = _xchg_wait(name + "_scatter_wait", _plan_scatter, sems, pair_sums, lands, after)
    halves = [_sum_chips(b, "%s_sum_chips_%d" % (name, k)) for k, b in enumerate(by_chip)]
    lands = [lax.empty(h.shape, h.dtype) for h in halves]
    sems, halves, lands, token = _xchg_start(name + "_share", _plan_share, len(halves), halves, lands)
    return name, sems, halves, lands, token


def _rs_end(handle, after):
    name, sems, halves, lands, _ = handle
    halves, others = _xchg_wait(name + "_share_wait", _plan_share, sems, halves, lands, after)
    return list(zip(halves, others))


def kernel(x, c, w_ada, b_ada, g_norm_mix, w_in, conv_a_w, conv_b_w, conv_b_bias, w_rg_a, b_rg_a, w_rg_x, b_rg_x, lru_lambda, w_out, g_norm_ffn, w_gate_up, w_down, g_norm_final, loss_target, m_w_ada, m_b_ada, m_g_norm_mix, m_w_in, m_conv_a_w, m_conv_b_w, m_conv_b_bias, m_w_rg_a, m_b_rg_a, m_w_rg_x, m_b_rg_x, m_lru_lambda, m_w_out, m_g_norm_ffn, m_w_gate_up, m_w_down, m_g_norm_final, v_w_ada, v_b_ada, v_g_norm_mix, v_w_in, v_conv_a_w, v_conv_b_w, v_conv_b_bias, v_w_rg_a, v_b_rg_a, v_w_rg_x, v_b_rg_x, v_lru_lambda, v_w_out, v_g_norm_ffn, v_w_gate_up, v_w_down, v_g_norm_final):
    xi, yi, ci = lax.axis_index("x"), lax.axis_index("y"), lax.axis_index("c")
    chip = 2 * xi + yi
    me = 2 * chip + ci
    n_ada = w_ada.shape[2]

    def widen(w):
        return jnp.pad(w, ((0, 0), (0, D - w.shape[1])))

    got = _allgather8(_pack_rows([c, widen(conv_a_w[0]), widen(conv_b_w[0])], 8, "pack_c_conv"), "gather_c_conv")
    c_all = got[:, 0, :]
    conv_full = got[::2, 1:8, :D // N_CHIPS].transpose(1, 0, 2).reshape(7, D)

    mod_part, c_act = _ada_fwd(c_all, w_ada[0], lax.dynamic_slice_in_dim(b_ada, chip * n_ada, n_ada, axis=1))
    mod_all = _allgather8(mod_part, "gather_mod")
    mod_mine = lax.dynamic_index_in_dim(mod_all, me, axis=1, keepdims=False)[::2].reshape(6, D)
    vecs = _pack_rows([mod_mine, g_norm_mix, g_norm_ffn, g_norm_final.reshape(1, D), conv_b_bias, b_rg_a, b_rg_x, lru_lambda,
                       conv_full], N_VEC, "pack_vecs")

    def rg_shard(w):
        return w[0].astype(BF16).reshape(2, HEADS * HB // N_CHIPS // 2, HB)

    shards = [w_in[0].astype(BF16).reshape(2, D // 2, C_IN), rg_shard(w_rg_a), rg_shard(w_rg_x),
              w_out[0].astype(BF16).reshape(2, D // N_CHIPS // 2, D), w_gate_up[0].astype(BF16).reshape(2, D // 2, C_GU),
              w_down[0].astype(BF16).reshape(2, D_FF // N_CHIPS // 2, D)]
    lands = [lax.dynamic_update_index_in_dim(lax.empty((N_CHIPS,) + s.shape, s.dtype), s, chip, 0) for s in shards]

    def send(name, first, last, after):
        sems, srcs, zone, token = _xchg_start(name + "_ici", _plan_gather_ici, 3 * (last - first), shards[first:last],
                                              lands[first:last], after)
        shards[first:last], lands[first:last] = srcs, zone
        return sems, first, token

    def arrive(name, sent, first, last, after, which=(1, 2, 3)):
        sems, base, _ = sent
        ids = [3 * (k - base) + j - 1 if j in which else None for k in range(first, last) for j in (1, 2, 3)]
        srcs, zone = _xchg_wait(name + "_ici_wait", _plan_gather_ici, sems, shards[first:last], lands[first:last], after, ids)
        shards[first:last], lands[first:last] = srcs, zone

    def relay(name, first, last, which):
        plan = _plan_relay(which)
        sems, _, zone, token = _xchg_start(name + "_d2d", plan, len(which) * (last - first), [], lands[first:last])
        lands[first:last] = zone
        return name, plan, sems, first, last, token

    def relayed(handle, after):
        name, plan, sems, first, last, _ = handle
        lands[first:last] = _xchg_wait(name + "_d2d_wait", plan, sems, [], lands[first:last], after)[1]

    def to_blocks(v):
        return v.reshape(-1, TS_MIX // TIME_BLOCKS, TIME_BLOCKS, D).transpose(0, 2, 1, 3).reshape(v.shape)

    def from_blocks(v):
        return v.reshape(-1, TIME_BLOCKS, TS_MIX // TIME_BLOCKS, D).transpose(0, 2, 1, 3).reshape(v.shape)

    def chip_index(j):
        return jnp.reshape(chip ^ j, (1,)).astype(jnp.int32)

    def wg_in():
        return lands[0].reshape(N_CHIPS, D, C_IN)

    xs, target = to_blocks(x[0]), to_blocks(loss_target[0])
    sent_in = send("gather_in", 0, 1, [vecs])
    ts_in = min(TS_IN, xs.shape[0])
    h1, proj = _fwd_in_first(xs, vecs, wg_in(), chip_index(0), ts_in, deps=[sent_in[-1]])
    arrive("gather_in_near", sent_in, 0, 1, [proj], (1, 2))
    near = relay("gather_in_near", 0, 1, (1, 2))
    sent_rest = send("gather_rest", 1, 6, [near[-1]])
    relayed(near, [sent_rest[-1]])
    proj = _fwd_in_more(h1, wg_in(), proj, chip_index(1), ts_in, "fwd_in_y")
    proj = _fwd_in_more(h1, wg_in(), proj, chip_index(2), ts_in, "fwd_in_x")
    arrive("gather_in_far", sent_in, 0, 1, [proj], (3,))
    far = relay("gather_in_far", 0, 1, (3,))
    arrive("gather_mix", sent_rest, 1, 4, [far[-1]])
    relayed(far, [far[-1]])
    mix = relay("gather_mix", 1, 4, (1, 2, 3))
    proj = _fwd_in_more(h1, wg_in(), proj, chip_index(3), ts_in, "fwd_in_xy", deps=[mix[-1]])
    relayed(mix, [proj])
    wg_rga, wg_rgx, wg_out = lands[1:4]
    wg_out = wg_out.reshape(D, D)

    def rg_full(wg):
        return wg.reshape(N_CHIPS, HEADS, HB // N_CHIPS, HB).transpose(1, 0, 2, 3).reshape(HEADS, HB, HB)

    wg_rga, wg_rgx = rg_full(wg_rga), rg_full(wg_rgx)

    arrive("gather_ffn", sent_rest, 4, 6, [proj])
    ffn = relay("gather_ffn", 4, 6, (1, 2, 3))
    x1, merged, z1, u, h, kept, decay = _fwd_mix(proj, xs, vecs, wg_rga, wg_rgx, wg_out, TS_MIX, deps=[ffn[-1]])
    relayed(ffn, [x1])
    wg_gu, wg_dn = lands[4:6]
    wg_gu, wg_dn = wg_gu.reshape(N_CHIPS, D, C_GU), wg_dn.reshape(D_FF, D)
    dx1, h2, act, dz2, dgu, sm_ffn = _ffn_loss(x1, target, vecs, wg_gu, wg_dn, TS_MIX)

    def rg_chunks(dw):
        return _halved(dw.reshape(HEADS, N_CHIPS, HB // N_CHIPS, HB).transpose(1, 0, 2, 3).reshape(N_CHIPS, HB, HB).astype(BF16))

    ts_gw = min(TS_GW, xs.shape[0])
    g_dn = _grad_w(act, dz2, 1, ts_gw, "grad_w_down")
    g_gu = _grad_w(h2, dgu, N_CHIPS, ts_gw, "grad_w_gate_up")
    rs_b = _rs_swap("rs_b", [_halved(g_gu), _halved(g_dn.reshape(N_CHIPS, D_FF // N_CHIPS, D))])
    dproj, sm_mix, dw_rga, dw_rgx, dw_out = _bwd_mix(dx1, z1, merged, proj, u, h, kept, decay, vecs, wg_rga, wg_rgx, wg_out, sm_ffn, TS_MIX,
                                                     deps=[rs_b[-1]])
    rs_b = _rs_scatter(rs_b, [dproj], chip, ci)
    g_in = _grad_w(h1, dproj, N_CHIPS, ts_gw, "grad_w_in", deps=[rs_b[-1]])
    rs_b = _rs_share(rs_b, [g_in])
    rs_a = _rs_swap("rs_a", [_halved(g_in), rg_chunks(dw_rga), rg_chunks(dw_rgx),
                             _halved(dw_out.astype(BF16).reshape(N_CHIPS, D // N_CHIPS, D))], after=[rs_b[-1]])

    c_arr = jnp.reshape(ci, (1,)).astype(jnp.int32)

    def step(name, w, g, m, v, deps=()):
        shape = w.shape
        two_d = (-1, shape[-1])
        d, nm, nv = _adamw(w.reshape(two_d), g.reshape(two_d), m.reshape(two_d), v.reshape(two_d), "adamw_" + name, deps)
        return g.reshape(shape), d.reshape(shape), nm.reshape(shape), nv.reshape(shape)

    def step_halves(name, w, halves, m, v, deps=()):
        shape = w.shape
        two_d = (-1, shape[-1])
        out = _adamw_halves(w.reshape(two_d), halves[0], halves[1], m.reshape(two_d), v.reshape(two_d), c_arr, "adamw_" + name, deps)
        return tuple(a.reshape(shape) for a in out)

    def shard_cols(row_block):
        return lax.dynamic_slice_in_dim(row_block, chip * (D // N_CHIPS), D // N_CHIPS, axis=1)

    gw_gu, gw_dn = _rs_end(rs_b, [rs_a[-1]])
    res = {
        "w_gate_up": step_halves("w_gate_up", w_gate_up, gw_gu, m_w_gate_up, v_w_gate_up, [rs_a[-1]]),
        "w_down": step_halves("w_down", w_down, gw_dn, m_w_down, v_w_down, [rs_a[-1]]),
    }
    rs_a = _rs_scatter(rs_a, [res["w_gate_up"][1], res["w_down"][1]], chip, ci)
    grad_x, sm_in = _bwd_in(dproj, xs, dx1, vecs, wg_in(), sm_mix, TS_MM, deps=[rs_a[-1]])
    rs_a = _rs_share(rs_a, [grad_x])

    small, per_dev = _sum_small(_allgather8(sm_in, "gather_small", deps=[rs_a[-1]]))
    dmod_all = per_dev[:, 0:6, :].reshape(N_DEV, 6 * D)
    grad_w_ada = _ada_bwd(c_act, lax.dynamic_slice_in_dim(dmod_all, chip * n_ada, n_ada, axis=1))
    grad_b_ada = small[0:6].reshape(1, 6 * D)
    res["w_ada"] = step("w_ada", w_ada, grad_w_ada[None], m_w_ada, v_w_ada)
    small_sets = {
        "b_ada": (b_ada.reshape(6, D), grad_b_ada.reshape(6, D), m_b_ada.reshape(6, D), v_b_ada.reshape(6, D)),
        "g_norm_mix": (g_norm_mix, small[G_GMIX:G_GMIX + 1], m_g_norm_mix, v_g_norm_mix),
        "conv_a_w": (conv_a_w[0], shard_cols(small[G_WA0:G_WA0 + 3]), m_conv_a_w[0], v_conv_a_w[0]),
        "conv_b_w": (conv_b_w[0], shard_cols(small[G_WB0:G_WB0 + 4]), m_conv_b_w[0], v_conv_b_w[0]),
        "conv_b_bias": (conv_b_bias, small[G_CBB:G_CBB + 1], m_conv_b_bias, v_conv_b_bias),
        "b_rg_a": (b_rg_a, small[G_BA:G_BA + 1], m_b_rg_a, v_b_rg_a),
        "b_rg_x": (b_rg_x, small[G_BX:G_BX + 1], m_b_rg_x, v_b_rg_x),
        "lru_lambda": (lru_lambda, small[G_LAM:G_LAM + 1], m_lru_lambda, v_lru_lambda),
        "g_norm_ffn": (g_norm_ffn, small[G_GFFN:G_GFFN + 1], m_g_norm_ffn, v_g_norm_ffn),
        "g_norm_final": (g_norm_final.reshape(1, D), small[G_GFIN:G_GFIN + 1], m_g_norm_final.reshape(1, D),
                         v_g_norm_final.reshape(1, D)),
    }
    stepped = _adamw_small(list(small_sets.values()), "adamw_small")
    for (n, (w_, g_, _, _)), (d_, nm_, nv_) in zip(small_sets.items(), stepped):
        shape = (1,) + w_.shape if n.startswith("conv_") and n != "conv_b_bias" else w_.shape
        res[n] = tuple(a.reshape(shape) for a in (g_, d_, nm_, nv_))
    gw_in, gw_rga, gw_rgx, gw_out = _rs_end(rs_a, [res[n][1] for n in res])
    res["w_in"] = step_halves("w_in", w_in, gw_in, m_w_in, v_w_in)
    res["w_rg_a"] = step_halves("w_rg_a", w_rg_a, gw_rga, m_w_rg_a, v_w_rg_a)
    res["w_rg_x"] = step_halves("w_rg_x", w_rg_x, gw_rgx, m_w_rg_x, v_w_rg_x)
    res["w_out"] = step_halves("w_out", w_out, gw_out, m_w_out, v_w_out)
    res["b_ada"] = tuple(a.reshape(1, 6 * D) for a in res["b_ada"])
    res["g_norm_final"] = tuple(a.reshape(D) for a in res["g_norm_final"])
    names = ["w_ada", "b_ada", "g_norm_mix", "w_in", "conv_a_w", "conv_b_w", "conv_b_bias", "w_rg_a", "b_rg_a", "w_rg_x",
             "b_rg_x", "lru_lambda", "w_out", "g_norm_ffn", "w_gate_up", "w_down", "g_norm_final"]
    loss = jnp.sum(small[G_LOSS])
    return (loss, from_blocks(grad_x)[None], *[res[n][0] for n in names], *[res[n][1] for n in names],
            *[res[n][2] for n in names], *[res[n][3] for n in names])
```

```python
import functools

import jax
import jax.numpy as jnp
from jax import lax
from jax.experimental import pallas as pl
from jax.experimental.pallas import tpu as pltpu

F32 = jnp.float32
BF16 = jnp.bfloat16
MESH = pl.DeviceIdType.MESH

D = 1024
N_CHIPS = 4
N_DEV = 8
D_IN = 7 * D
C_IN = D_IN // N_CHIPS
D_FF = 2816
C_GU = 2 * D_FF // N_CHIPS
HEADS = 4
HB = D // HEADS
EPS = 1e-6
LRU_C = 8.0
ADAM_LR, ADAM_B1, ADAM_B2, ADAM_EPS, ADAM_WD, ADAM_STEP = 0.001, 0.9, 0.999, 1e-08, 0.01, 10
VMEM_LIMIT = 56 << 20

(V_SH1, V_SC1, V_GT1, V_SH2, V_SC2, V_GT2, V_GMIX, V_GFFN, V_GFIN, V_CBB, V_BA, V_BX, V_LAM,
 V_WA0, V_WA1, V_WA2, V_WB0, V_WB1, V_WB2, V_WB3) = range(20)
N_VEC = 24
(G_SH1, G_SC1, G_GT1, G_SH2, G_SC2, G_GT2, G_GMIX, G_CBB, G_BA, G_BX, G_LAM, G_GFFN, G_GFIN,
 G_WA0, G_WA1, G_WA2, G_WB0, G_WB1, G_WB2, G_WB3, G_LOSS) = range(21)
N_SMALL = 24

_VMEM = pl.BlockSpec(memory_space=pltpu.VMEM)
_ANY = pl.BlockSpec(memory_space=pl.ANY)


def _cparams(n_grid=1):
    return pltpu.CompilerParams(dimension_semantics=("arbitrary",) * n_grid, vmem_limit_bytes=VMEM_LIMIT)


def _after(deps, body):
    n = len(deps)
    return lambda *refs: body(*refs[n:])


def _rms(x):
    rstd = lax.rsqrt(jnp.mean(x * x, axis=-1, keepdims=True) + EPS)
    return x * rstd, rstd


def _rms_bwd(dxhat, xhat, rstd):
    return rstd * (dxhat - xhat * jnp.mean(dxhat * xhat, axis=-1, keepdims=True))


def _rowsum(v):
    return jnp.sum(v, axis=0, keepdims=True)


def _dot(a, b):
    return jnp.dot(a, b, preferred_element_type=F32)


def _dot_nt(a, b):
    return lax.dot_general(a, b, (((1,), (1,)), ((), ())), preferred_element_type=F32)


def _dot_tn(a, b):
    return lax.dot_general(a, b, (((0,), (0,)), ((), ())), preferred_element_type=F32)


def _gelu(x):
    k, c = 0.7978845608028654, 0.044715
    t = jnp.tanh(k * (x + c * x * x * x))
    return 0.5 * x * (1.0 + t), 0.5 * (1.0 + t) + 0.5 * x * (1.0 - t * t) * k * (1.0 + 3.0 * c * x * x)


def _log_sigmoid(lam):
    return jnp.minimum(lam, 0.0) - jnp.log1p(jnp.exp(-jnp.abs(lam)))


def _lru_gates(u, wa_ref, wx_ref, v_ref, row0):
    ub = u.astype(BF16)
    pre_a = jnp.concatenate([_dot(ub[:, h * HB:(h + 1) * HB], wa_ref[h]) for h in range(HEADS)], axis=1)
    pre_x = jnp.concatenate([_dot(ub[:, h * HB:(h + 1) * HB], wx_ref[h]) for h in range(HEADS)], axis=1)
    r = jax.nn.sigmoid(pre_a + v_ref[V_BA:V_BA + 1, :])
    ig = jax.nn.sigmoid(pre_x + v_ref[V_BX:V_BX + 1, :])
    log_a = LRU_C * r * _log_sigmoid(v_ref[V_LAM:V_LAM + 1, :])
    a = jnp.exp(log_a)
    x2 = 2.0 * log_a
    m2 = jnp.where(x2 > -0.03, -x2 * (1.0 + x2 * (0.5 + x2 * (1.0 / 6.0 + x2 * (1.0 / 24.0)))), 1.0 - a * a)
    mult = jnp.where(row0, 1.0, jnp.sqrt(jnp.maximum(m2, 0.0)))
    return r, ig, a, mult


TIME_BLOCKS = 8
N_KEPT = 10


def _late_blocks(v, buf, g, halo=None):
    n = buf.shape[0]
    out = []
    for idx in range(n):
        k = TIME_BLOCKS - n + idx
        buf[idx, 8:g + 8, :] = v[k * g:(k + 1) * g]
        if halo is not None:
            buf[idx, 7:8, :] = halo[idx]
        out.append(buf[idx, pl.ds(7, g), :])
        if halo is None:
            buf[idx, 7:8, :] = buf[idx, g + 7:g + 8, :]
    return out


def _earlier(v, s, late, g):
    return jnp.concatenate(late[len(late) - s:] + [v[0:(TIME_BLOCKS - s) * g]], axis=0)


def _early_blocks(v, buf, g):
    out = []
    for k in range(buf.shape[0]):
        buf[k, 0:g, :] = v[k * g:(k + 1) * g]
        out.append(buf[k, pl.ds(1, g), :])
        buf[k, g:g + 1, :] = buf[k, 0:1, :]
    return out


def _later(v, s, early, g):
    return jnp.concatenate([v[s * g:]] + early[0:s], axis=0)


def _fwd_in_first(x, vecs, w_in_g, q_idx, ts, deps=()):
    s = x.shape[0]

    def body(q_ref, x_ref, v_ref, w_ref, h1_ref, proj_ref):
        xhat, _ = _rms(x_ref[...])
        h = xhat * v_ref[V_GMIX:V_GMIX + 1, :] * (1.0 + v_ref[V_SC1:V_SC1 + 1, :]) + v_ref[V_SH1:V_SH1 + 1, :]
        hb = h.astype(BF16)
        h1_ref[...] = hb
        proj_ref[...] = _dot(hb, w_ref[...]).astype(BF16)

    return pl.pallas_call(
        lambda q_ref, *refs: body(q_ref, *refs[len(deps):]),
        grid_spec=pltpu.PrefetchScalarGridSpec(
            num_scalar_prefetch=1, grid=(s // ts,),
            in_specs=[_ANY] * len(deps) + [pl.BlockSpec((ts, D), lambda i, q: (i, 0)), _VMEM,
                                           pl.BlockSpec((None, D, C_IN), lambda i, q: (q[0], 0, 0))],
            out_specs=[pl.BlockSpec((ts, D), lambda i, q: (i, 0)), pl.BlockSpec((ts, C_IN), lambda i, q: (i, q[0]))]),
        out_shape=(jax.ShapeDtypeStruct((s, D), BF16), jax.ShapeDtypeStruct((s, D_IN), BF16)),
        compiler_params=_cparams(), name="fwd_in_own")(q_idx, *deps, x, vecs, w_in_g)


def _fwd_in_more(h1, w_in_g, proj, q_idx, ts, name, deps=()):
    s = h1.shape[0]

    def body(q_ref, h1_ref, w_ref, proj_in_ref, proj_ref):
        proj_ref[...] = _dot(h1_ref[...], w_ref[...]).astype(BF16)

    return pl.pallas_call(
        lambda q_ref, *refs: body(q_ref, *refs[len(deps):]),
        grid_spec=pltpu.PrefetchScalarGridSpec(
            num_scalar_prefetch=1, grid=(s // ts,),
            in_specs=[_ANY] * len(deps) + [pl.BlockSpec((ts, D), lambda i, q: (i, 0)),
                                           pl.BlockSpec((None, D, C_IN), lambda i, q: (q[0], 0, 0)), _ANY],
            out_specs=pl.BlockSpec((ts, C_IN), lambda i, q: (i, q[0]))),
        out_shape=jax.ShapeDtypeStruct((s, D_IN), BF16), input_output_aliases={len(deps) + 3: 0},
        compiler_params=_cparams(), name=name)(q_idx, *deps, h1, w_in_g, proj)


def _fwd_mix(proj, x, vecs, w_rga, w_rgx, w_out, ts, deps=()):
    s = x.shape[0]
    g = ts // TIME_BLOCKS

    def body(proj_ref, x_ref, v_ref, wa_ref, wx_ref, wo_ref, x1_ref, mg_ref, z1_ref, kept_ref, decay_ref,
             ua_buf, rx_buf, p_buf, q_buf, c_buf, hcarry):
        i = pl.program_id(0)

        @pl.when(i == 0)
        def _():
            ua_buf[...] = jnp.zeros(ua_buf.shape, F32)
            rx_buf[...] = jnp.zeros(rx_buf.shape, F32)
            hcarry[...] = jnp.zeros((8, D), F32)

        def seg(j):
            return proj_ref[:, j * D:(j + 1) * D].astype(F32)

        def vrow(j):
            return v_ref[j:j + 1, :]

        cb, cc, cx, rx, rg, ga, gb = (seg(j) for j in range(7))
        ua = cc * cx
        ua_late = _late_blocks(ua, ua_buf, g)
        rx_late = _late_blocks(rx, rx_buf, g)
        va = vrow(V_WA2) * ua + vrow(V_WA1) * _earlier(ua, 1, ua_late, g) + vrow(V_WA0) * _earlier(ua, 2, ua_late, g)
        u = (vrow(V_WB3) * rx + vrow(V_WB2) * _earlier(rx, 1, rx_late, g) + vrow(V_WB1) * _earlier(rx, 2, rx_late, g)
             + vrow(V_WB0) * _earlier(rx, 3, rx_late, g) + vrow(V_CBB))

        rows = lax.broadcasted_iota(jnp.int32, (ts, D), 0)
        row0 = jnp.logical_and(rows == 0, i == 0)
        r, ig, a, mult = _lru_gates(u, wa_ref, wx_ref, v_ref, row0)
        decay_ref[...] = a
        bx = mult * (ig * u)

        prods, sums = [a[0:g]], [bx[0:g]]
        for k in range(1, TIME_BLOCKS):
            ak = a[k * g:(k + 1) * g]
            sums.append(ak * sums[-1] + bx[k * g:(k + 1) * g])
            prods.append(ak * prods[-1])
        p_buf[...] = prods[-1]
        q_buf[...] = sums[-1]
        state = hcarry[0:1, :]
        for j in range(g):
            c_buf[j:j + 1, :] = state
            state = p_buf[j:j + 1, :] * state + q_buf[j:j + 1, :]
        hcarry[0:1, :] = state
        entering = c_buf[...]
        h = jnp.concatenate([sums[k] + prods[k] * entering for k in range(TIME_BLOCKS)], axis=0)

        gel, dgel = _gelu(rg)
        sga = jax.nn.sigmoid(ga)
        sgb = jax.nn.sigmoid(gb)
        for j, keep in enumerate((va, r, ig, sga, sgb, gel, dgel, mult, u, h)):
            kept_ref[:, j * D:(j + 1) * D] = keep.astype(BF16)
        merged = (sga * (cb * va) + sgb * (h * gel)).astype(BF16)
        mg_ref[...] = merged
        z1 = _dot(merged, wo_ref[...])
        z1_ref[...] = z1.astype(BF16)
        x1_ref[...] = x_ref[...] + vrow(V_GT1) * z1

    row = lambda i: (i, 0)
    return pl.pallas_call(
        _after(deps, body), grid=(s // ts,),
        out_shape=(jax.ShapeDtypeStruct((s, D), F32), jax.ShapeDtypeStruct((s, D), BF16), jax.ShapeDtypeStruct((s, D), BF16),
                   jax.ShapeDtypeStruct((s, N_KEPT * D), BF16), jax.ShapeDtypeStruct((s, D), F32)),
        in_specs=[_ANY] * len(deps) + [pl.BlockSpec((ts, D_IN), row), pl.BlockSpec((ts, D), row), _VMEM, _VMEM, _VMEM, _VMEM],
        out_specs=[pl.BlockSpec((ts, D), row)] * 3 + [pl.BlockSpec((ts, N_KEPT * D), row), pl.BlockSpec((ts, D), row)],
        scratch_shapes=[pltpu.VMEM((2, g + 8, D), F32), pltpu.VMEM((3, g + 8, D), F32), pltpu.VMEM((g, D), F32),
                        pltpu.VMEM((g, D), F32), pltpu.VMEM((g, D), F32), pltpu.VMEM((8, D), F32)],
        compiler_params=_cparams(), name="fwd_mix")(*deps, proj, x, vecs, w_rga, w_rgx, w_out)


def _ffn_loss(x1, target, vecs, w_gu_g, w_dn, ts):
    s = x1.shape[0]

    def body(x1_ref, t_ref, v_ref, wgu_ref, wdn_ref, dx1_ref, h2_ref, act_ref, dz2_ref, dgu_ref, sm_ref):
        @pl.when(pl.program_id(0) == 0)
        def _():
            sm_ref[...] = jnp.zeros((N_SMALL, D), F32)

        def vrow(j):
            return v_ref[j:j + 1, :]

        n_sub = 1
        rows = [slice(k * (ts // n_sub), (k + 1) * (ts // n_sub)) for k in range(n_sub)]
        subs = [dict(r=r, sums={}) for r in rows]

        def stage_norm(t):
            t["x1"] = x1_ref[t["r"], :]
            t["xh1"], t["rstd1"] = _rms(t["x1"])
            t["n2"] = t["xh1"] * vrow(V_GFFN)
            t["h2"] = (t["n2"] * (1.0 + vrow(V_SC2)) + vrow(V_SH2)).astype(BF16)
            h2_ref[t["r"], :] = t["h2"]

        def stage_up(t):
            h2 = t["h2"]
            g = jnp.concatenate([_dot(h2, wgu_ref[0]), _dot(h2, wgu_ref[1])], axis=1)
            t["up"] = jnp.concatenate([_dot(h2, wgu_ref[2]), _dot(h2, wgu_ref[3])], axis=1)
            t["g"] = g
            t["sg"] = jax.nn.sigmoid(g)
            t["silu"] = g * t["sg"]
            t["act"] = (t["silu"] * t["up"]).astype(BF16)
            act_ref[t["r"], :] = t["act"]

        def stage_down_loss(t):
            z2 = _dot(t["act"], wdn_ref[...])
            x2 = t["x1"] + vrow(V_GT2) * z2
            xh2, rstd2 = _rms(x2)
            err = xh2 * vrow(V_GFIN) - t_ref[t["r"], :]
            t["sums"][G_LOSS] = _rowsum((0.5 / D) * err * err)
            dy = err * (1.0 / D)
            t["sums"][G_GFIN] = _rowsum(dy * xh2)
            t["dx2"] = _rms_bwd(dy * vrow(V_GFIN), xh2, rstd2)
            t["sums"][G_GT2] = _rowsum(t["dx2"] * z2)
            t["dz2"] = (vrow(V_GT2) * t["dx2"]).astype(BF16)
            dz2_ref[t["r"], :] = t["dz2"]

        def stage_back_act(t):
            dact = _dot_nt(t["dz2"], wdn_ref[...])
            g, sg = t["g"], t["sg"]
            t["dgate"] = (dact * t["up"] * (sg * (1.0 + g * (1.0 - sg)))).astype(BF16)
            t["dup"] = (dact * t["silu"]).astype(BF16)
            dgu_ref[t["r"], 0:D_FF] = t["dgate"]
            dgu_ref[t["r"], D_FF:2 * D_FF] = t["dup"]

        def stage_back_norm(t):
            dgate, dup = t["dgate"], t["dup"]
            dh2 = (_dot_nt(dgate[:, 0:C_GU], wgu_ref[0]) + _dot_nt(dgate[:, C_GU:2 * C_GU], wgu_ref[1])
                   + _dot_nt(dup[:, 0:C_GU], wgu_ref[2]) + _dot_nt(dup[:, C_GU:2 * C_GU], wgu_ref[3]))
            t["sums"][G_SH2] = _rowsum(dh2)
            t["sums"][G_SC2] = _rowsum(dh2 * t["n2"])
            dn2 = dh2 * (1.0 + vrow(V_SC2))
            t["sums"][G_GFFN] = _rowsum(dn2 * t["xh1"])
            dx1_ref[t["r"], :] = t["dx2"] + _rms_bwd(dn2 * vrow(V_GFFN), t["xh1"], t["rstd1"])

        for stage in (stage_norm, stage_up, stage_down_loss, stage_back_act, stage_back_norm):
            for t in subs:
                stage(t)
        for j in subs[0]["sums"]:
            total = subs[0]["sums"][j]
            for t in subs[1:]:
                total = total + t["sums"][j]
            sm_ref[j:j + 1, :] += total

    row = lambda i: (i, 0)
    return pl.pallas_call(
        body, grid=(s // ts,),
        out_shape=(jax.ShapeDtypeStruct((s, D), F32), jax.ShapeDtypeStruct((s, D), BF16), jax.ShapeDtypeStruct((s, D_FF), BF16),
                   jax.ShapeDtypeStruct((s, D), BF16), jax.ShapeDtypeStruct((s, 2 * D_FF), BF16),
                   jax.ShapeDtypeStruct((N_SMALL, D), F32)),
        in_specs=[pl.BlockSpec((ts, D), row), pl.BlockSpec((ts, D), row), _VMEM, _VMEM, _VMEM],
        out_specs=[pl.BlockSpec((ts, D), row), pl.BlockSpec((ts, D), row), pl.BlockSpec((ts, D_FF), row),
                   pl.BlockSpec((ts, D), row), pl.BlockSpec((ts, 2 * D_FF), row), pl.BlockSpec((N_SMALL, D), lambda i: (0, 0))],
        compiler_params=_cparams(), name="ffn_loss")(x1, target, vecs, w_gu_g, w_dn)


def _bwd_mix(dx1, z1, merged, proj, kept, decay, vecs, w_rga, w_rgx, w_out, small, ts, deps=()):
    s = dx1.shape[0]
    nt = s // ts
    g = ts // TIME_BLOCKS
    assert g % 16 == 0

    def body(dx1_ref, z1_ref, mg_ref, proj_ref, kept_ref, decay_ref, hh_ref, v_ref, wa_ref, wx_ref,
             wo_ref, sm0_ref, dproj_ref, sm_ref, dwa_ref, dwx_ref, dwo_ref,
             h_buf, a_buf, dva_buf, du_buf, p_buf, q_buf, c_buf, lcarry):
        i = pl.program_id(0)
        first_tile = i == nt - 1

        @pl.when(i == 0)
        def _():
            a_buf[...] = jnp.zeros(a_buf.shape, F32)
            dva_buf[...] = jnp.zeros(dva_buf.shape, F32)
            du_buf[...] = jnp.zeros(du_buf.shape, F32)
            lcarry[...] = jnp.zeros((8, D), F32)
            sm_ref[...] = sm0_ref[...]
            dwa_ref[...] = jnp.zeros((HEADS, HB, HB), F32)
            dwx_ref[...] = jnp.zeros((HEADS, HB, HB), F32)
            dwo_ref[...] = jnp.zeros((D, D), F32)

        def seg(j):
            return proj_ref[:, j * D:(j + 1) * D].astype(F32)

        def vrow(j):
            return v_ref[j:j + 1, :]

        def acc(j, val):
            sm_ref[j:j + 1, :] += _rowsum(val)

        cb, cc, cx, rx = (seg(j) for j in range(4))
        ua = cc * cx
        va, r, ig, sga, sgb, gel, dgel, mult, u, h = (kept_ref[:, j * D:(j + 1) * D].astype(F32) for j in range(N_KEPT))
        a = decay_ref[...]
        rows = lax.broadcasted_iota(jnp.int32, (ts, D), 0)
        row0 = jnp.logical_and(rows == 0, first_tile)

        dx1 = dx1_ref[...]
        acc(G_GT1, dx1 * z1_ref[...].astype(F32))
        dz1 = (vrow(V_GT1) * dx1).astype(BF16)
        dwo_ref[...] += _dot_tn(mg_ref[...], dz1)
        dmg = _dot_nt(dz1, wo_ref[...])
        dya = dmg * sga
        dyb = dmg * sgb
        dproj_ref[:, 5 * D:6 * D] = (dya * (cb * va) * (1.0 - sga)).astype(BF16)
        dproj_ref[:, 6 * D:7 * D] = (dyb * (h * gel) * (1.0 - sgb)).astype(BF16)

        dproj_ref[:, 0:D] = (dya * va).astype(BF16)
        dva = dya * cb
        dva_early = _early_blocks(dva, dva_buf, g)
        dva1 = _later(dva, 1, dva_early, g)
        dva2 = _later(dva, 2, dva_early, g)
        dua = vrow(V_WA2) * dva + vrow(V_WA1) * dva1 + vrow(V_WA0) * dva2
        acc(G_WA2, ua * dva)
        acc(G_WA1, ua * dva1)
        acc(G_WA0, ua * dva2)
        dproj_ref[:, D:2 * D] = (dua * cx).astype(BF16)
        dproj_ref[:, 2 * D:3 * D] = (dua * cc).astype(BF16)

        dproj_ref[:, 4 * D:5 * D] = (dyb * h * dgel).astype(BF16)
        a_next = _later(a, 1, _early_blocks(a, a_buf, g), g)
        dh = dyb * gel
        last = TIME_BLOCKS - 1
        prods, sums = {last: a_next[last * g:]}, {last: dh[last * g:]}
        for k in range(last - 1, -1, -1):
            ak = a_next[k * g:(k + 1) * g]
            sums[k] = dh[k * g:(k + 1) * g] + ak * sums[k + 1]
            prods[k] = ak * prods[k + 1]
        p_buf[...] = prods[0]
        q_buf[...] = sums[0]
        state = lcarry[0:1, :]
        for j in range(g - 1, -1, -1):
            c_buf[j:j + 1, :] = state
            state = q_buf[j:j + 1, :] + p_buf[j:j + 1, :] * state
        lcarry[0:1, :] = state
        entering = c_buf[...]
        lam = jnp.concatenate([sums[k] + prods[k] * entering for k in range(TIME_BLOCKS)], axis=0)

        last = lax.broadcasted_iota(jnp.int32, hh_ref.shape, 0) == hh_ref.shape[0] - 1
        h_halo = [jnp.where(first_tile, 0.0, jnp.sum(jnp.where(last, hh_ref[...].astype(F32), 0.0), axis=0, keepdims=True))]
        da = lam * _earlier(h, 1, _late_blocks(h, h_buf, g, h_halo), g)
        dmult = jnp.where(row0, 0.0, lam * (ig * u))
        di = lam * mult * u
        du = lam * mult * ig
        dlog_a = da * a - dmult * (a * a) / mult
        lam_p = vrow(V_LAM)
        dr = dlog_a * (LRU_C * _log_sigmoid(lam_p))
        sm_ref[G_LAM:G_LAM + 1, :] += _rowsum(dlog_a * r) * (LRU_C * jax.nn.sigmoid(-lam_p))
        dpa = dr * r * (1.0 - r)
        dpx = di * ig * (1.0 - ig)
        acc(G_BA, dpa)
        acc(G_BX, dpx)
        dpab = dpa.astype(BF16)
        dpxb = dpx.astype(BF16)
        ub = u.astype(BF16)
        back = []
        for hd in range(HEADS):
            cols = slice(hd * HB, (hd + 1) * HB)
            back.append(_dot_nt(dpab[:, cols], wa_ref[hd]) + _dot_nt(dpxb[:, cols], wx_ref[hd]))
            dwa_ref[hd] += _dot_tn(ub[:, cols], dpab[:, cols])
            dwx_ref[hd] += _dot_tn(ub[:, cols], dpxb[:, cols])
        du = du + jnp.concatenate(back, axis=1)

        acc(G_CBB, du)
        du_early = _early_blocks(du, du_buf, g)
        du1 = _later(du, 1, du_early, g)
        du2 = _later(du, 2, du_early, g)
        du3 = _later(du, 3, du_early, g)
        dproj_ref[:, 3 * D:4 * D] = (vrow(V_WB3) * du + vrow(V_WB2) * du1 + vrow(V_WB1) * du2 + vrow(V_WB0) * du3).astype(BF16)
        acc(G_WB3, rx * du)
        acc(G_WB2, rx * du1)
        acc(G_WB1, rx * du2)
        acc(G_WB0, rx * du3)

    rev = lambda i: (nt - 1 - i, 0)
    h_halo16 = lambda i: (jnp.maximum((nt - 1 - i) * (ts // 16) - 1, 0), N_KEPT - 1)
    const2 = lambda i: (0, 0)
    const3 = lambda i: (0, 0, 0)
    return pl.pallas_call(
        _after(deps, body), grid=(nt,),
        out_shape=(jax.ShapeDtypeStruct((s, D_IN), BF16), jax.ShapeDtypeStruct((N_SMALL, D), F32),
                   jax.ShapeDtypeStruct((HEADS, HB, HB), F32), jax.ShapeDtypeStruct((HEADS, HB, HB), F32),
                   jax.ShapeDtypeStruct((D, D), F32)),
        in_specs=[_ANY] * len(deps) + [pl.BlockSpec((ts, D), rev), pl.BlockSpec((ts, D), rev), pl.BlockSpec((ts, D), rev),
                  pl.BlockSpec((ts, 4 * D), rev), pl.BlockSpec((ts, N_KEPT * D), rev), pl.BlockSpec((ts, D), rev),
                  pl.BlockSpec((16, D), h_halo16), _VMEM, _VMEM, _VMEM, _VMEM, _VMEM],
        out_specs=[pl.BlockSpec((ts, D_IN), rev), pl.BlockSpec((N_SMALL, D), const2),
                   pl.BlockSpec((HEADS, HB, HB), const3), pl.BlockSpec((HEADS, HB, HB), const3), pl.BlockSpec((D, D), const2)],
        scratch_shapes=[pltpu.VMEM((1, g + 8, D), F32), pltpu.VMEM((1, g + 8, D), F32),
                        pltpu.VMEM((2, g + 8, D), F32), pltpu.VMEM((3, g + 8, D), F32), pltpu.VMEM((g, D), F32),
                        pltpu.VMEM((g, D), F32), pltpu.VMEM((g, D), F32), pltpu.VMEM((8, D), F32)],
        compiler_params=_cparams(), name="bwd_mix")(*deps, dx1, z1, merged, proj, kept, decay, kept, vecs, w_rga,
                                                    w_rgx, w_out, small)


def _bwd_in(dproj, x, dx1, vecs, w_in_g, small, ts, deps=()):
    s = x.shape[0]

    def body(dp_ref, x_ref, dx1_ref, v_ref, w_ref, sm0_ref, gx_ref, sm_ref):
        @pl.when(pl.program_id(0) == 0)
        def _():
            sm_ref[...] = sm0_ref[...]

        def vrow(j):
            return v_ref[j:j + 1, :]

        dh1 = _dot_nt(dp_ref[:, 0:C_IN], w_ref[0])
        for k in range(1, N_CHIPS):
            dh1 += _dot_nt(dp_ref[:, k * C_IN:(k + 1) * C_IN], w_ref[k])
        xh, rstd = _rms(x_ref[...])
        sm_ref[G_SH1:G_SH1 + 1, :] += _rowsum(dh1)
        sm_ref[G_SC1:G_SC1 + 1, :] += _rowsum(dh1 * (xh * vrow(V_GMIX)))
        dn1 = dh1 * (1.0 + vrow(V_SC1))
        sm_ref[G_GMIX:G_GMIX + 1, :] += _rowsum(dn1 * xh)
        gx_ref[...] = dx1_ref[...] + _rms_bwd(dn1 * vrow(V_GMIX), xh, rstd)

    row = lambda i: (i, 0)
    return pl.pallas_call(
        _after(deps, body), grid=(s // ts,),
        out_shape=(jax.ShapeDtypeStruct((s, D), F32), jax.ShapeDtypeStruct((N_SMALL, D), F32)),
        in_specs=[_ANY] * len(deps) + [pl.BlockSpec((ts, D_IN), row), pl.BlockSpec((ts, D), row), pl.BlockSpec((ts, D), row),
                                       _VMEM, _VMEM, _VMEM],
        out_specs=[pl.BlockSpec((ts, D), row), pl.BlockSpec((N_SMALL, D), lambda i: (0, 0))],
        compiler_params=_cparams(), name="bwd_in")(*deps, dproj, x, dx1, vecs, w_in_g, small)


def _grad_w(a, b, n_col_blocks, ts, name, deps=()):
    s, m = a.shape
    tn = b.shape[1] // n_col_blocks
    n_steps = s // ts

    def body(a_ref, b_ref, o_ref, acc_ref):
        k = pl.program_id(1)

        @pl.when(k == 0)
        def _():
            acc_ref[...] = jnp.zeros((m, tn), F32)

        acc_ref[...] += _dot_tn(a_ref[...], b_ref[...])

        @pl.when(k == n_steps - 1)
        def _():
            o_ref[...] = acc_ref[...].astype(BF16)

    return pl.pallas_call(
        _after(deps, body), grid=(n_col_blocks, n_steps),
        out_shape=jax.ShapeDtypeStruct((n_col_blocks, m, tn), BF16),
        in_specs=[_ANY] * len(deps) + [pl.BlockSpec((ts, m), lambda n, k: (k, 0)), pl.BlockSpec((ts, tn), lambda n, k: (k, n))],
        out_specs=pl.BlockSpec((None, m, tn), lambda n, k: (n, 0, 0)),
        scratch_shapes=[pltpu.VMEM((m, tn), F32)],
        compiler_params=_cparams(2), name=name)(*deps, a, b)


def _ada_fwd(c_all, w_ada, b_ada):
    n = w_ada.shape[1]

    def body(c_ref, w_ref, b_ref, o_ref, ca_ref):
        c = c_ref[...]
        ca = c * jax.nn.sigmoid(c)
        ca_ref[...] = ca
        o_ref[...] = jnp.dot(ca, w_ref[...], preferred_element_type=F32, precision=lax.Precision.HIGHEST) + b_ref[...]

    return pl.pallas_call(
        body, out_shape=(jax.ShapeDtypeStruct((N_DEV, n), F32), jax.ShapeDtypeStruct((N_DEV, D), F32)),
        in_specs=[_VMEM] * 3, out_specs=[_VMEM] * 2, compiler_params=_cparams(0), name="ada_fwd")(c_all, w_ada, b_ada)


def _ada_bwd(c_act, dmod):
    n = dmod.shape[1]

    def body(c_ref, d_ref, o_ref):
        o_ref[...] = lax.dot_general(c_ref[...], d_ref[...], (((0,), (0,)), ((), ())), preferred_element_type=F32,
                                     precision=lax.Precision.HIGHEST)

    return pl.pallas_call(
        body, out_shape=jax.ShapeDtypeStruct((D, n), F32), in_specs=[_VMEM] * 2, out_specs=_VMEM,
        compiler_params=_cparams(0), name="ada_bwd")(c_act, dmod)


def _sum_small(parts):
    def body(p_ref, o_ref, d_ref):
        tot = p_ref[0]
        for dev in range(1, N_DEV):
            tot = tot + p_ref[dev]
        o_ref[...] = tot
        d_ref[...] = p_ref[:, 0:8, :]

    return pl.pallas_call(
        body, out_shape=(jax.ShapeDtypeStruct((N_SMALL, D), F32), jax.ShapeDtypeStruct((N_DEV, 8, D), F32)),
        in_specs=[_VMEM], out_specs=[_VMEM] * 2, compiler_params=_cparams(0), name="sum_small")(parts)


def _adamw(w, g, m, v, name, deps=()):
    rows, cols = w.shape
    tr = 128 if rows % 128 == 0 else (64 if rows % 64 == 0 else rows)

    def body(w_ref, g_ref, m_ref, v_ref, d_ref, nm_ref, nv_ref):
        g_ = g_ref[...]
        m_ = ADAM_B1 * m_ref[...] + (1.0 - ADAM_B1) * g_
        v_ = ADAM_B2 * v_ref[...] + (1.0 - ADAM_B2) * (g_ * g_)
        nm_ref[...] = m_
        nv_ref[...] = v_
        m_hat = m_ / (1.0 - ADAM_B1 ** ADAM_STEP)
        v_hat = v_ / (1.0 - ADAM_B2 ** ADAM_STEP)
        d_ref[...] = -ADAM_LR * (m_hat / (jnp.sqrt(v_hat) + ADAM_EPS) + ADAM_WD * w_ref[...])

    spec = pl.BlockSpec((tr, cols), lambda i: (i, 0))
    return pl.pallas_call(
        _after(deps, body), grid=(rows // tr,), out_shape=(jax.ShapeDtypeStruct((rows, cols), F32),) * 3,
        in_specs=[_ANY] * len(deps) + [spec] * 4, out_specs=[spec] * 3, compiler_params=_cparams(), name=name)(*deps, w, g, m, v)


def _adamw_small(items, name):
    n = len(items)

    def body(*refs):
        ins, outs = refs[:4 * n], refs[4 * n:]
        for k in range(n):
            w_ref, g_ref, m_ref, v_ref = ins[4 * k:4 * k + 4]
            d_ref, nm_ref, nv_ref = outs[3 * k:3 * k + 3]
            g_ = g_ref[...]
            m_ = ADAM_B1 * m_ref[...] + (1.0 - ADAM_B1) * g_
            v_ = ADAM_B2 * v_ref[...] + (1.0 - ADAM_B2) * (g_ * g_)
            nm_ref[...] = m_
            nv_ref[...] = v_
            m_hat = m_ / (1.0 - ADAM_B1 ** ADAM_STEP)
            v_hat = v_ / (1.0 - ADAM_B2 ** ADAM_STEP)
            d_ref[...] = -ADAM_LR * (m_hat / (jnp.sqrt(v_hat) + ADAM_EPS) + ADAM_WD * w_ref[...])

    out = pl.pallas_call(
        body, out_shape=tuple(jax.ShapeDtypeStruct(it[0].shape, F32) for it in items for _ in range(3)),
        in_specs=[_VMEM] * (4 * n), out_specs=[_VMEM] * (3 * n), name=name)(*[a for it in items for a in it])
    return [tuple(out[3 * k:3 * k + 3]) for k in range(n)]


def _adamw_halves(w, mine, other, m, v, c_idx, name, deps=()):
    r2, cols = mine.shape
    tr = next(t for t in (128, 64, 32, 16, 8) if r2 % t == 0)
    nh = r2 // tr

    def body(c_ref, w_ref, mine_ref, other_ref, m_ref, v_ref, g_ref, d_ref, nm_ref, nv_ref):
        g_ = jnp.where(pl.program_id(0) // nh == c_ref[0], mine_ref[...], other_ref[...])
        g_ref[...] = g_
        m_ = ADAM_B1 * m_ref[...] + (1.0 - ADAM_B1) * g_
        v_ = ADAM_B2 * v_ref[...] + (1.0 - ADAM_B2) * (g_ * g_)
        nm_ref[...] = m_
        nv_ref[...] = v_
        m_hat = m_ / (1.0 - ADAM_B1 ** ADAM_STEP)
        v_hat = v_ / (1.0 - ADAM_B2 ** ADAM_STEP)
        d_ref[...] = -ADAM_LR * (m_hat / (jnp.sqrt(v_hat) + ADAM_EPS) + ADAM_WD * w_ref[...])

    full = pl.BlockSpec((tr, cols), lambda i, c: (i, 0))
    mine_spec = pl.BlockSpec((tr, cols), lambda i, c: (jnp.clip(i - c[0] * nh, 0, nh - 1), 0))
    other_spec = pl.BlockSpec((tr, cols), lambda i, c: (jnp.clip(i - (1 - c[0]) * nh, 0, nh - 1), 0))
    return pl.pallas_call(
        lambda c_ref, *refs: body(c_ref, *refs[len(deps):]),
        grid_spec=pltpu.PrefetchScalarGridSpec(
            num_scalar_prefetch=1, grid=(2 * nh,),
            in_specs=[_ANY] * len(deps) + [full, mine_spec, other_spec, full, full], out_specs=[full] * 4),
        out_shape=(jax.ShapeDtypeStruct((2 * r2, cols), F32),) * 4, compiler_params=_cparams(), name=name,
    )(c_idx, *deps, w, mine, other, m, v)


def _add_halves(g, recv, c_idx, name):
    n, _, r2, cols = g.shape

    def body(c_ref, g_ref, r_ref, o_ref):
        o_ref[...] = (g_ref[...].astype(F32) + r_ref[...].astype(F32)).astype(BF16)

    return pl.pallas_call(
        body,
        grid_spec=pltpu.PrefetchScalarGridSpec(
            num_scalar_prefetch=1, grid=(n,),
            in_specs=[pl.BlockSpec((None, None, r2, cols), lambda k, c: (k, c[0], 0, 0)),
                      pl.BlockSpec((None, r2, cols), lambda k, c: (k, 0, 0))],
            out_specs=pl.BlockSpec((None, r2, cols), lambda k, c: (k, 0, 0))),
        out_shape=jax.ShapeDtypeStruct((n, r2, cols), BF16), compiler_params=_cparams(), name=name)(c_idx, g, recv)


def _sum_chips(parts, name):
    n, r2, cols = parts.shape
    tr = next(t for t in (64, 32, 16) if r2 % t == 0)

    def body(p_ref, o_ref):
        o_ref[...] = ((p_ref[0].astype(F32) + p_ref[1].astype(F32)) + p_ref[2].astype(F32)) + p_ref[3].astype(F32)

    return pl.pallas_call(
        body, grid=(r2 // tr,), out_shape=jax.ShapeDtypeStruct((r2, cols), F32),
        in_specs=[pl.BlockSpec((n, tr, cols), lambda i: (0, i, 0))], out_specs=pl.BlockSpec((tr, cols), lambda i: (i, 0)),
        compiler_params=_cparams(), name=name)(parts)


def _place():
    x, y, c = lax.axis_index("x"), lax.axis_index("y"), lax.axis_index("c")
    return x, y, c, 2 * x + y


def _flip(v, bit):
    return 1 - v if bit else v


def _allgather8(v, name, deps=()):
    r, n = v.shape

    def body(*refs):
        v_ref, out_ref, send_sems, recv_sems, local_sem = refs[len(deps):]
        x, y, c, _ = _place()
        me = 4 * x + 2 * y + c
        mine = pltpu.make_async_copy(v_ref, out_ref.at[me], local_sem)
        mine.start()
        sends = []
        for rel in range(1, N_DEV):
            peer = (_flip(x, rel & 4), _flip(y, rel & 2), _flip(c, rel & 1))
            cp = pltpu.make_async_remote_copy(v_ref, out_ref.at[me], send_sems.at[rel - 1], recv_sems.at[rel - 1],
                                              device_id=peer, device_id_type=MESH)
            cp.start()
            sends.append(cp)
        for rel in range(1, N_DEV):
            peer = (_flip(x, rel & 4), _flip(y, rel & 2), _flip(c, rel & 1))
            peer_idx = 4 * peer[0] + 2 * peer[1] + peer[2]
            pltpu.make_async_remote_copy(v_ref, out_ref.at[peer_idx], send_sems.at[rel - 1], recv_sems.at[rel - 1],
                                         device_id=peer, device_id_type=MESH).wait_recv()
        for cp in sends:
            cp.wait_send()
        mine.wait()

    return pl.pallas_call(
        body, out_shape=jax.ShapeDtypeStruct((N_DEV, r, n), F32), in_specs=[_ANY] * len(deps) + [_VMEM], out_specs=_VMEM,
        scratch_shapes=[pltpu.SemaphoreType.DMA((N_DEV - 1,)), pltpu.SemaphoreType.DMA((N_DEV - 1,)), pltpu.SemaphoreType.DMA(())],
        name=name)(*deps, v)


def _gather_weights(shards):
    nw = len(shards)

    def body(*refs):
        w_refs, out_refs = refs[:nw], refs[nw:2 * nw]
        send_sems, recv_sems = refs[2 * nw:]
        x, y, c, p = _place()
        sibling = (x, y, 1 - c)
        sends = []
        for j in range(1, N_CHIPS):
            peer = (_flip(x, j & 2), _flip(y, j & 1), c)
            for w in range(nw):
                cp = pltpu.make_async_remote_copy(w_refs[w].at[c], out_refs[w].at[p, c], send_sems.at[w * 6 + j - 1],
                                                  recv_sems.at[w * 6 + j - 1], device_id=peer, device_id_type=MESH)
                cp.start()
                sends.append(cp)
        for j in range(1, N_CHIPS):
            peer = (_flip(x, j & 2), _flip(y, j & 1), c)
            q = 2 * peer[0] + peer[1]
            for w in range(nw):
                pltpu.make_async_remote_copy(w_refs[w].at[c], out_refs[w].at[q, c], send_sems.at[w * 6 + j - 1],
                                             recv_sems.at[w * 6 + j - 1], device_id=peer, device_id_type=MESH).wait_recv()
                cp = pltpu.make_async_remote_copy(out_refs[w].at[q, c], out_refs[w].at[q, c], send_sems.at[w * 6 + 2 + j],
                                                  recv_sems.at[w * 6 + 2 + j], device_id=sibling, device_id_type=MESH)
                cp.start()
                sends.append(cp)
        for j in range(1, N_CHIPS):
            q = 2 * _flip(x, j & 2) + _flip(y, j & 1)
            for w in range(nw):
                pltpu.make_async_remote_copy(out_refs[w].at[q, 1 - c], out_refs[w].at[q, 1 - c], send_sems.at[w * 6 + 2 + j],
                                             recv_sems.at[w * 6 + 2 + j], device_id=sibling, device_id_type=MESH).wait_recv()
        for cp in sends:
            cp.wait_send()

    return pl.pallas_call(
        body, out_shape=tuple(jax.ShapeDtypeStruct((N_CHIPS,) + s.shape, s.dtype) for s in shards),
        in_specs=[_ANY] * nw, out_specs=[_ANY] * nw,
        scratch_shapes=[pltpu.SemaphoreType.DMA((6 * nw,)), pltpu.SemaphoreType.DMA((6 * nw,))],
        name="gather_weights")(*shards)


def _swap_halves(grads):
    nw = len(grads)

    def body(*refs):
        g_refs, out_refs = refs[:nw], refs[nw:2 * nw]
        send_sems, recv_sems = refs[2 * nw:]
        x, y, c, _ = _place()
        sibling = (x, y, 1 - c)
        sends = []
        for w in range(nw):
            for k in range(N_CHIPS):
                cp = pltpu.make_async_remote_copy(g_refs[w].at[k, 1 - c], out_refs[w].at[k], send_sems.at[w * N_CHIPS + k],
                                                  recv_sems.at[w * N_CHIPS + k], device_id=sibling, device_id_type=MESH)
                cp.start()
                sends.append(cp)
        for cp in sends:
            cp.wait_recv()
        for cp in sends:
            cp.wait_send()

    return pl.pallas_call(
        body, out_shape=tuple(jax.ShapeDtypeStruct((N_CHIPS,) + g.shape[2:], g.dtype) for g in grads),
        in_specs=[_ANY] * nw, out_specs=[_ANY] * nw,
        scratch_shapes=[pltpu.SemaphoreType.DMA((N_CHIPS * nw,)), pltpu.SemaphoreType.DMA((N_CHIPS * nw,))],
        name="swap_halves")(*grads)


def _scatter_chips(parts):
    nw = len(parts)

    def body(*refs):
        p_refs, out_refs = refs[:nw], refs[nw:2 * nw]
        send_sems, recv_sems = refs[2 * nw:]
        x, y, c, p = _place()
        sends = []
        for j in range(1, N_CHIPS):
            peer = (_flip(x, j & 2), _flip(y, j & 1), c)
            q = 2 * peer[0] + peer[1]
            for w in range(nw):
                cp = pltpu.make_async_remote_copy(p_refs[w].at[q], out_refs[w].at[p], send_sems.at[w * 3 + j - 1],
                                                  recv_sems.at[w * 3 + j - 1], device_id=peer, device_id_type=MESH)
                cp.start()
                sends.append(cp)
        for j in range(1, N_CHIPS):
            peer = (_flip(x, j & 2), _flip(y, j & 1), c)
            q = 2 * peer[0] + peer[1]
            for w in range(nw):
                pltpu.make_async_remote_copy(p_refs[w].at[q], out_refs[w].at[q], send_sems.at[w * 3 + j - 1],
                                             recv_sems.at[w * 3 + j - 1], device_id=peer, device_id_type=MESH).wait_recv()
        for cp in sends:
            cp.wait_send()

    return pl.pallas_call(
        body, out_shape=tuple(jax.ShapeDtypeStruct(s.shape, s.dtype) for s in parts),
        in_specs=[_ANY] * nw, out_specs=[_ANY] * nw,
        scratch_shapes=[pltpu.SemaphoreType.DMA((3 * nw,)), pltpu.SemaphoreType.DMA((3 * nw,))],
        name="scatter_chips")(*parts)


def _share_halves(halves):
    nw = len(halves)

    def body(*refs):
        h_refs, out_refs = refs[:nw], refs[nw:2 * nw]
        send_sems, recv_sems = refs[2 * nw:]
        x, y, c, _ = _place()
        sends = []
        for w in range(nw):
            cp = pltpu.make_async_remote_copy(h_refs[w], out_refs[w], send_sems.at[w], recv_sems.at[w],
                                              device_id=(x, y, 1 - c), device_id_type=MESH)
            cp.start()
            sends.append(cp)
        for cp in sends:
            cp.wait_recv()
        for cp in sends:
            cp.wait_send()

    return pl.pallas_call(
        body, out_shape=tuple(jax.ShapeDtypeStruct(s.shape, s.dtype) for s in halves),
        in_specs=[_ANY] * nw, out_specs=[_ANY] * nw,
        scratch_shapes=[pltpu.SemaphoreType.DMA((nw,)), pltpu.SemaphoreType.DMA((nw,))],
        name="share_halves")(*halves)


_HBM = pl.BlockSpec(memory_space=pltpu.HBM)
_SEM = pl.BlockSpec(memory_space=pltpu.SEMAPHORE)
_EFFECT = pltpu.SideEffectType.DATAFLOW_SIDE_EFFECTING


def _xchg_start(name, plan, n_copies, srcs, lands, after=()):
    bufs = list(srcs) + list(lands)
    ns, nb = len(srcs), len(srcs) + len(lands)

    def body(*refs):
        send_sems, recv_sems, token = refs[nb + len(after)], refs[nb + len(after) + 1], refs[-1]
        for i, (src, dst, peer, _) in enumerate(plan(_place(), refs[:ns], refs[ns:nb])):
            pltpu.make_async_remote_copy(src, dst, send_sems.at[i], recv_sems.at[i], device_id=peer, device_id_type=MESH).start()
        token[...] = jnp.zeros_like(token)

    out = pl.pallas_call(
        body, name=name,
        out_shape=(pltpu.SemaphoreType.DMA((n_copies,)), pltpu.SemaphoreType.DMA((n_copies,)),
                   *[pltpu.HBM(a.shape, a.dtype) for a in bufs], jax.ShapeDtypeStruct((8, 128), F32)),
        in_specs=[_HBM] * nb + [_ANY] * len(after), out_specs=(_SEM, _SEM, *[_HBM] * nb, _VMEM),
        input_output_aliases={i: 2 + i for i in range(nb)},
        compiler_params=pltpu.CompilerParams(has_side_effects=_EFFECT),
    )(*[pltpu.with_memory_space_constraint(a, pltpu.HBM) for a in bufs], *after)
    return (out[0], out[1]), out[2:2 + ns], out[2 + ns:2 + nb], out[-1]


def _xchg_wait(name, plan, sems, srcs, lands, after, sem_ids=None):
    bufs = list(srcs) + list(lands)
    ns, nb = len(srcs), len(srcs) + len(lands)

    def body(*refs):
        send_sems, recv_sems = refs[nb], refs[nb + 1]
        copies = plan(_place(), refs[:ns], refs[ns:nb])
        ids = range(len(copies)) if sem_ids is None else sem_ids
        for i, (src, _, peer, mine) in zip(ids, copies, strict=True):
            if i is not None:
                cp = pltpu.make_async_remote_copy(src, mine, send_sems.at[i], recv_sems.at[i], device_id=peer,
                                                  device_id_type=MESH)
                cp.wait_send()
                cp.wait_recv()

    out = pl.pallas_call(
        body, name=name, out_shape=tuple(pltpu.HBM(a.shape, a.dtype) for a in bufs),
        in_specs=[_HBM] * nb + [_SEM, _SEM] + [_ANY] * len(after), out_specs=tuple([_HBM] * nb),
        input_output_aliases={i: i for i in range(nb)},
        compiler_params=pltpu.CompilerParams(has_side_effects=_EFFECT),
    )(*bufs, *sems, *after)
    return out[:ns], out[ns:]


def _other_chips(place, which=(1, 2, 3)):
    x, y, c, _ = place
    return [((_flip(x, j & 2), _flip(y, j & 1), c), 2 * _flip(x, j & 2) + _flip(y, j & 1)) for j in which]


def _plan_gather_ici(place, src_refs, land_refs):
    _, _, c, p = place
    return [(s.at[c], l.at[p, c], peer, l.at[q, c]) for s, l in zip(src_refs, land_refs) for peer, q in _other_chips(place)]


def _plan_relay(which):
    def plan(place, src_refs, land_refs):
        x, y, c, _ = place
        return [(l.at[q, c], l.at[q, c], (x, y, 1 - c), l.at[q, 1 - c]) for l in land_refs for _, q in _other_chips(place, which)]
    return plan


def _plan_swap(place, src_refs, land_refs):
    x, y, c, _ = place
    return [(s.at[k, 1 - c], l.at[k], (x, y, 1 - c), l.at[k]) for s, l in zip(src_refs, land_refs) for k in range(N_CHIPS)]


def _plan_scatter(place, src_refs, land_refs):
    _, _, _, p = place
    return [(s.at[q], l.at[p], peer, l.at[q]) for s, l in zip(src_refs, land_refs) for peer, q in _other_chips(place)]


def _plan_share(place, src_refs, land_refs):
    x, y, c, _ = place
    return [(s, l, (x, y, 1 - c), l) for s, l in zip(src_refs, land_refs)]


def _pack_rows(parts, n_rows, name, deps=()):
    def body(*refs):
        refs = refs[len(deps):]
        out_ref = refs[-1]
        out_ref[...] = jnp.zeros((n_rows, D), F32)
        at = 0
        for ref in refs[:-1]:
            k = ref.shape[0]
            out_ref[at:at + k, :] = ref[...]
            at += k

    return pl.pallas_call(
        body, out_shape=jax.ShapeDtypeStruct((n_rows, D), F32), in_specs=[_ANY] * len(deps) + [_VMEM] * len(parts),
        out_specs=_VMEM, name=name)(*deps, *parts)


TS_MM = 512
TS_IN = 1024
TS_GW = 1024
TS_MIX = 256


def _halved(a):
    n, r, cols = a.shape
    return a.reshape(n, 2, r // 2, cols)


def _rs_swap(name, grads, after=()):
    lands = [lax.empty((N_CHIPS,) + g.shape[2:], g.dtype) for g in grads]
    sems, grads, lands, token = _xchg_start(name + "_swap", _plan_swap, N_CHIPS * len(grads), grads, lands, after)
    return name, sems, grads, lands, token


def _rs_scatter(handle, after, chip, ci):
    name, sems, grads, lands, _ = handle
    grads, from_sibling = _xchg_wait(name + "_swap_wait", _plan_swap, sems, grads, lands, after)
    c_arr = jnp.reshape(ci, (1,)).astype(jnp.int32)
    pair_sums = [_add_halves(g, r, c_arr, "%s_add_halves_%d" % (name, k)) for k, (g, r) in enumerate(zip(grads, from_sibling))]
    lands = [lax.dynamic_update_index_in_dim(lax.empty(p.shape, p.dtype), lax.dynamic_index_in_dim(p, chip, 0, keepdims=False),
                                             chip, 0) for p in pair_sums]
    sems, pair_sums, lands, token = _xchg_start(name + "_scatter", _plan_scatter, 3 * len(pair_sums), pair_sums, lands)
    return name, sems, pair_sums, lands, token


def _rs_share(handle, after):
    name, sems, pair_sums, lands, _ = handle
    _, by_chip = _xchg_wait(name + "_scatter_wait", _plan_scatter, sems, pair_sums, lands, after)
    halves = [_sum_chips(b, "%s_sum_chips_%d" % (name, k)) for k, b in enumerate(by_chip)]
    lands = [lax.empty(h.shape, h.dtype) for h in halves]
    sems, halves, lands, token = _xchg_start(name + "_share", _plan_share, len(halves), halves, lands)
    return name, sems, halves, lands, token


def _rs_end(handle, after):
    name, sems, halves, lands, _ = handle
    halves, others = _xchg_wait(name + "_share_wait", _plan_share, sems, halves, lands, after)
    return list(zip(halves, others))


def kernel(x, c, w_ada, b_ada, g_norm_mix, w_in, conv_a_w, conv_b_w, conv_b_bias, w_rg_a, b_rg_a, w_rg_x, b_rg_x, lru_lambda, w_out, g_norm_ffn, w_gate_up, w_down, g_norm_final, loss_target, m_w_ada, m_b_ada, m_g_norm_mix, m_w_in, m_conv_a_w, m_conv_b_w, m_conv_b_bias, m_w_rg_a, m_b_rg_a, m_w_rg_x, m_b_rg_x, m_lru_lambda, m_w_out, m_g_norm_ffn, m_w_gate_up, m_w_down, m_g_norm_final, v_w_ada, v_b_ada, v_g_norm_mix, v_w_in, v_conv_a_w, v_conv_b_w, v_conv_b_bias, v_w_rg_a, v_b_rg_a, v_w_rg_x, v_b_rg_x, v_lru_lambda, v_w_out, v_g_norm_ffn, v_w_gate_up, v_w_down, v_g_norm_final):
    xi, yi, ci = lax.axis_index("x"), lax.axis_index("y"), lax.axis_index("c")
    chip = 2 * xi + yi
    me = 2 * chip + ci
    n_ada = w_ada.shape[2]

    def widen(w):
        return jnp.pad(w, ((0, 0), (0, D - w.shape[1])))

    got = _allgather8(_pack_rows([c, widen(conv_a_w[0]), widen(conv_b_w[0])], 8, "pack_c_conv"), "gather_c_conv")
    c_all = got[:, 0, :]
    conv_full = got[::2, 1:8, :D // N_CHIPS].transpose(1, 0, 2).reshape(7, D)

    mod_part, c_act = _ada_fwd(c_all, w_ada[0], lax.dynamic_slice_in_dim(b_ada, chip * n_ada, n_ada, axis=1))
    mod_all = _allgather8(mod_part, "gather_mod")
    mod_mine = lax.dynamic_index_in_dim(mod_all, me, axis=1, keepdims=False)[::2].reshape(6, D)
    vecs = _pack_rows([mod_mine, g_norm_mix, g_norm_ffn, g_norm_final.reshape(1, D), conv_b_bias, b_rg_a, b_rg_x, lru_lambda,
                       conv_full], N_VEC, "pack_vecs")

    def rg_shard(w):
        return w[0].astype(BF16).reshape(2, HEADS * HB // N_CHIPS // 2, HB)

    shards = [w_in[0].astype(BF16).reshape(2, D // 2, C_IN), rg_shard(w_rg_a), rg_shard(w_rg_x),
              w_out[0].astype(BF16).reshape(2, D // N_CHIPS // 2, D), w_gate_up[0].astype(BF16).reshape(2, D // 2, C_GU),
              w_down[0].astype(BF16).reshape(2, D_FF // N_CHIPS // 2, D)]
    lands = [lax.dynamic_update_index_in_dim(lax.empty((N_CHIPS,) + s.shape, s.dtype), s, chip, 0) for s in shards]

    def send(name, first, last, after):
        sems, srcs, zone, token = _xchg_start(name + "_ici", _plan_gather_ici, 3 * (last - first), shards[first:last],
                                              lands[first:last], after)
        shards[first:last], lands[first:last] = srcs, zone
        return sems, first, token

    def arrive(name, sent, first, last, after, which=(1, 2, 3)):
        sems, base, _ = sent
        ids = [3 * (k - base) + j - 1 if j in which else None for k in range(first, last) for j in (1, 2, 3)]
        srcs, zone = _xchg_wait(name + "_ici_wait", _plan_gather_ici, sems, shards[first:last], lands[first:last], after, ids)
        shards[first:last], lands[first:last] = srcs, zone

    def relay(name, first, last, which):
        plan = _plan_relay(which)
        sems, _, zone, token = _xchg_start(name + "_d2d", plan, len(which) * (last - first), [], lands[first:last])
        lands[first:last] = zone
        return name, plan, sems, first, last, token

    def relayed(handle, after):
        name, plan, sems, first, last, _ = handle
        lands[first:last] = _xchg_wait(name + "_d2d_wait", plan, sems, [], lands[first:last], after)[1]

    def to_blocks(v):
        return v.reshape(-1, TS_MIX // TIME_BLOCKS, TIME_BLOCKS, D).transpose(0, 2, 1, 3).reshape(v.shape)

    def from_blocks(v):
        return v.reshape(-1, TIME_BLOCKS, TS_MIX // TIME_BLOCKS, D).transpose(0, 2, 1, 3).reshape(v.shape)

    def chip_index(j):
        return jnp.reshape(chip ^ j, (1,)).astype(jnp.int32)

    def wg_in():
        return lands[0].reshape(N_CHIPS, D, C_IN)

    xs, target = to_blocks(x[0]), to_blocks(loss_target[0])
    sent_in = send("gather_in", 0, 1, [vecs])
    ts_in = min(TS_IN, xs.shape[0])
    h1, proj = _fwd_in_first(xs, vecs, wg_in(), chip_index(0), ts_in, deps=[sent_in[-1]])
    arrive("gather_in_near", sent_in, 0, 1, [proj], (1, 2))
    near = relay("gather_in_near", 0, 1, (1, 2))
    sent_rest = send("gather_rest", 1, 6, [near[-1]])
    relayed(near, [sent_rest[-1]])
    proj = _fwd_in_more(h1, wg_in(), proj, chip_index(1), ts_in, "fwd_in_y")
    proj = _fwd_in_more(h1, wg_in(), proj, chip_index(2), ts_in, "fwd_in_x")
    arrive("gather_in_far", sent_in, 0, 1, [proj], (3,))
    far = relay("gather_in_far", 0, 1, (3,))
    arrive("gather_mix", sent_rest, 1, 4, [far[-1]])
    relayed(far, [far[-1]])
    mix = relay("gather_mix", 1, 4, (1, 2, 3))
    proj = _fwd_in_more(h1, wg_in(), proj, chip_index(3), ts_in, "fwd_in_xy", deps=[mix[-1]])
    relayed(mix, [proj])
    wg_rga, wg_rgx, wg_out = lands[1:4]
    wg_out = wg_out.reshape(D, D)

    def rg_full(wg):
        return wg.reshape(N_CHIPS, HEADS, HB // N_CHIPS, HB).transpose(1, 0, 2, 3).reshape(HEADS, HB, HB)

    wg_rga, wg_rgx = rg_full(wg_rga), rg_full(wg_rgx)

    arrive("gather_ffn", sent_rest, 4, 6, [proj])
    ffn = relay("gather_ffn", 4, 6, (1, 2, 3))
    x1, merged, z1, kept, decay = _fwd_mix(proj, xs, vecs, wg_rga, wg_rgx, wg_out, TS_MIX, deps=[ffn[-1]])
    relayed(ffn, [x1])
    wg_gu, wg_dn = lands[4:6]
    wg_gu, wg_dn = wg_gu.reshape(N_CHIPS, D, C_GU), wg_dn.reshape(D_FF, D)
    dx1, h2, act, dz2, dgu, sm_ffn = _ffn_loss(x1, target, vecs, wg_gu, wg_dn, TS_MIX)

    def rg_chunks(dw):
        return _halved(dw.reshape(HEADS, N_CHIPS, HB // N_CHIPS, HB).transpose(1, 0, 2, 3).reshape(N_CHIPS, HB, HB).astype(BF16))

    ts_gw = min(TS_GW, xs.shape[0])
    g_dn = _grad_w(act, dz2, 1, ts_gw, "grad_w_down")
    g_gu = _grad_w(h2, dgu, N_CHIPS, ts_gw, "grad_w_gate_up")
    rs_b = _rs_swap("rs_b", [_halved(g_gu), _halved(g_dn.reshape(N_CHIPS, D_FF // N_CHIPS, D))])
    dproj, sm_mix, dw_rga, dw_rgx, dw_out = _bwd_mix(dx1, z1, merged, proj, kept, decay, vecs, wg_rga, wg_rgx, wg_out, sm_ffn, TS_MIX,
                                                     deps=[rs_b[-1]])
    rs_b = _rs_scatter(rs_b, [dproj], chip, ci)
    g_in = _grad_w(h1, dproj, N_CHIPS, ts_gw, "grad_w_in", deps=[rs_b[-1]])
    rs_b = _rs_share(rs_b, [g_in])
    rs_a = _rs_swap("rs_a", [_halved(g_in), rg_chunks(dw_rga), rg_chunks(dw_rgx),
                             _halved(dw_out.astype(BF16).reshape(N_CHIPS, D // N_CHIPS, D))], after=[rs_b[-1]])

    c_arr = jnp.reshape(ci, (1,)).astype(jnp.int32)

    def step(name, w, g, m, v, deps=()):
        shape = w.shape
        two_d = (-1, shape[-1])
        d, nm, nv = _adamw(w.reshape(two_d), g.reshape(two_d), m.reshape(two_d), v.reshape(two_d), "adamw_" + name, deps)
        return g.reshape(shape), d.reshape(shape), nm.reshape(shape), nv.reshape(shape)

    def step_halves(name, w, halves, m, v, deps=()):
        shape = w.shape
        two_d = (-1, shape[-1])
        out = _adamw_halves(w.reshape(two_d), halves[0], halves[1], m.reshape(two_d), v.reshape(two_d), c_arr, "adamw_" + name, deps)
        return tuple(a.reshape(shape) for a in out)

    def shard_cols(row_block):
        return lax.dynamic_slice_in_dim(row_block, chip * (D // N_CHIPS), D // N_CHIPS, axis=1)

    gw_gu, gw_dn = _rs_end(rs_b, [rs_a[-1]])
    res = {
        "w_gate_up": step_halves("w_gate_up", w_gate_up, gw_gu, m_w_gate_up, v_w_gate_up, [rs_a[-1]]),
        "w_down": step_halves("w_down", w_down, gw_dn, m_w_down, v_w_down, [rs_a[-1]]),
    }
    rs_a = _rs_scatter(rs_a, [res["w_gate_up"][1], res["w_down"][1]], chip, ci)
    grad_x, sm_in = _bwd_in(dproj, xs, dx1, vecs, wg_in(), sm_mix, TS_MM, deps=[rs_a[-1]])
    rs_a = _rs_share(rs_a, [grad_x])

    small, per_dev = _sum_small(_allgather8(sm_in, "gather_small", deps=[rs_a[-1]]))
    dmod_all = per_dev[:, 0:6, :].reshape(N_DEV, 6 * D)
    grad_w_ada = _ada_bwd(c_act, lax.dynamic_slice_in_dim(dmod_all, chip * n_ada, n_ada, axis=1))
    grad_b_ada = small[0:6].reshape(1, 6 * D)
    res["w_ada"] = step("w_ada", w_ada, grad_w_ada[None], m_w_ada, v_w_ada)
    small_sets = {
        "b_ada": (b_ada.reshape(6, D), grad_b_ada.reshape(6, D), m_b_ada.reshape(6, D), v_b_ada.reshape(6, D)),
        "g_norm_mix": (g_norm_mix, small[G_GMIX:G_GMIX + 1], m_g_norm_mix, v_g_norm_mix),
        "conv_a_w": (conv_a_w[0], shard_cols(small[G_WA0:G_WA0 + 3]), m_conv_a_w[0], v_conv_a_w[0]),
        "conv_b_w": (conv_b_w[0], shard_cols(small[G_WB0:G_WB0 + 4]), m_conv_b_w[0], v_conv_b_w[0]),
        "conv_b_bias": (conv_b_bias, small[G_CBB:G_CBB + 1], m_conv_b_bias, v_conv_b_bias),
        "b_rg_a": (b_rg_a, small[G_BA:G_BA + 1], m_b_rg_a, v_b_rg_a),
        "b_rg_x": (b_rg_x, small[G_BX:G_BX + 1], m_b_rg_x, v_b_rg_x),
        "lru_lambda": (lru_lambda, small[G_LAM:G_LAM + 1], m_lru_lambda, v_lru_lambda),
        "g_norm_ffn": (g_norm_ffn, small[G_GFFN:G_GFFN + 1], m_g_norm_ffn, v_g_norm_ffn),
        "g_norm_final": (g_norm_final.reshape(1, D), small[G_GFIN:G_GFIN + 1], m_g_norm_final.reshape(1, D),
                         v_g_norm_final.reshape(1, D)),
    }
    stepped = _adamw_small(list(small_sets.values()), "adamw_small")
    for (n, (w_, g_, _, _)), (d_, nm_, nv_) in zip(small_sets.items(), stepped):
        shape = (1,) + w_.shape if n.startswith("conv_") and n != "conv_b_bias" else w_.shape
        res[n] = tuple(a.reshape(shape) for a in (g_, d_, nm_, nv_))
    gw_in, gw_rga, gw_rgx, gw_out = _rs_end(rs_a, [res[n][1] for n in res])
    res["w_in"] = step_halves("w_in", w_in, gw_in, m_w_in, v_w_in)
    res["w_rg_a"] = step_halves("w_rg_a", w_rg_a, gw_rga, m_w_rg_a, v_w_rg_a)
    res["w_rg_x"] = step_halves("w_rg_x", w_rg_x, gw_rgx, m_w_rg_x, v_w_rg_x)
    res["w_out"] = step_halves("w_out", w_out, gw_out, m_w_out, v_w_out)
    res["b_ada"] = tuple(a.reshape(1, 6 * D) for a in res["b_ada"])
    res["g_norm_final"] = tuple(a.reshape(D) for a in res["g_norm_final"])
    names = ["w_ada", "b_ada", "g_norm_mix", "w_in", "conv_a_w", "conv_b_w", "conv_b_bias", "w_rg_a", "b_rg_a", "w_rg_x",
             "b_rg_x", "lru_lambda", "w_out", "g_norm_ffn", "w_gate_up", "w_down", "g_norm_final"]
    loss = jnp.sum(small[G_LOSS])
    return (loss, from_blocks(grad_x)[None], *[res[n][0] for n in names], *[res[n][1] for n in names],
            *[res[n][2] for n in names], *[res[n][3] for n in names])
```

```python
import functools

import jax
import jax.numpy as jnp
from jax import lax
from jax.experimental import pallas as pl
from jax.experimental.pallas import tpu as pltpu

F32 = jnp.float32
BF16 = jnp.bfloat16
MESH = pl.DeviceIdType.MESH

D = 1024
N_CHIPS = 4
N_DEV = 8
D_IN = 7 * D
C_IN = D_IN // N_CHIPS
D_FF = 2816
C_GU = 2 * D_FF // N_CHIPS
HEADS = 4
HB = D // HEADS
EPS = 1e-6
LRU_C = 8.0
ADAM_LR, ADAM_B1, ADAM_B2, ADAM_EPS, ADAM_WD, ADAM_STEP = 0.001, 0.9, 0.999, 1e-08, 0.01, 10
VMEM_LIMIT = 56 << 20

(V_SH1, V_SC1, V_GT1, V_SH2, V_SC2, V_GT2, V_GMIX, V_GFFN, V_GFIN, V_CBB, V_BA, V_BX, V_LAM,
 V_WA0, V_WA1, V_WA2, V_WB0, V_WB1, V_WB2, V_WB3) = range(20)
N_VEC = 24
(G_SH1, G_SC1, G_GT1, G_SH2, G_SC2, G_GT2, G_GMIX, G_CBB, G_BA, G_BX, G_LAM, G_GFFN, G_GFIN,
 G_WA0, G_WA1, G_WA2, G_WB0, G_WB1, G_WB2, G_WB3, G_LOSS) = range(21)
N_SMALL = 24

_VMEM = pl.BlockSpec(memory_space=pltpu.VMEM)
_ANY = pl.BlockSpec(memory_space=pl.ANY)


def _cparams(n_grid=1):
    return pltpu.CompilerParams(dimension_semantics=("arbitrary",) * n_grid, vmem_limit_bytes=VMEM_LIMIT)


def _after(deps, body):
    n = len(deps)
    return lambda *refs: body(*refs[n:])


def _rms(x):
    rstd = lax.rsqrt(jnp.mean(x * x, axis=-1, keepdims=True) + EPS)
    return x * rstd, rstd


def _rms_bwd(dxhat, xhat, rstd):
    return rstd * (dxhat - xhat * jnp.mean(dxhat * xhat, axis=-1, keepdims=True))


def _rowsum(v):
    return jnp.sum(v, axis=0, keepdims=True)


def _dot(a, b):
    return jnp.dot(a, b, preferred_element_type=F32)


def _dot_nt(a, b):
    return lax.dot_general(a, b, (((1,), (1,)), ((), ())), preferred_element_type=F32)


def _dot_tn(a, b):
    return lax.dot_general(a, b, (((0,), (0,)), ((), ())), preferred_element_type=F32)


def _gelu(x):
    k, c = 0.7978845608028654, 0.044715
    t = jnp.tanh(k * (x + c * x * x * x))
    return 0.5 * x * (1.0 + t), 0.5 * (1.0 + t) + 0.5 * x * (1.0 - t * t) * k * (1.0 + 3.0 * c * x * x)


def _log_sigmoid(lam):
    return jnp.minimum(lam, 0.0) - jnp.log1p(jnp.exp(-jnp.abs(lam)))


def _lru_gates(u, wa_ref, wx_ref, v_ref, row0):
    ub = u.astype(BF16)
    pre_a = jnp.concatenate([_dot(ub[:, h * HB:(h + 1) * HB], wa_ref[h]) for h in range(HEADS)], axis=1)
    pre_x = jnp.concatenate([_dot(ub[:, h * HB:(h + 1) * HB], wx_ref[h]) for h in range(HEADS)], axis=1)
    r = jax.nn.sigmoid(pre_a + v_ref[V_BA:V_BA + 1, :])
    ig = jax.nn.sigmoid(pre_x + v_ref[V_BX:V_BX + 1, :])
    log_a = LRU_C * r * _log_sigmoid(v_ref[V_LAM:V_LAM + 1, :])
    a = jnp.exp(log_a)
    x2 = 2.0 * log_a
    m2 = jnp.where(x2 > -0.03, -x2 * (1.0 + x2 * (0.5 + x2 * (1.0 / 6.0 + x2 * (1.0 / 24.0)))), 1.0 - a * a)
    mult = jnp.where(row0, 1.0, jnp.sqrt(jnp.maximum(m2, 0.0)))
    return r, ig, a, mult


TIME_BLOCKS = 8
N_KEPT = 10


def _late_blocks(v, buf, g, halo=None):
    n = buf.shape[0]
    out = []
    for idx in range(n):
        k = TIME_BLOCKS - n + idx
        buf[idx, 8:g + 8, :] = v[k * g:(k + 1) * g]
        if halo is not None:
            buf[idx, 7:8, :] = halo[idx]
        out.append(buf[idx, pl.ds(7, g), :])
        if halo is None:
            buf[idx, 7:8, :] = buf[idx, g + 7:g + 8, :]
    return out


def _earlier(v, s, late, g):
    return jnp.concatenate(late[len(late) - s:] + [v[0:(TIME_BLOCKS - s) * g]], axis=0)


def _early_blocks(v, buf, g):
    out = []
    for k in range(buf.shape[0]):
        buf[k, 0:g, :] = v[k * g:(k + 1) * g]
        out.append(buf[k, pl.ds(1, g), :])
        buf[k, g:g + 1, :] = buf[k, 0:1, :]
    return out


def _later(v, s, early, g):
    return jnp.concatenate([v[s * g:]] + early[0:s], axis=0)


def _fwd_in_first(x, vecs, w_in_g, q_idx, ts, deps=()):
    s = x.shape[0]

    def body(q_ref, x_ref, v_ref, w_ref, h1_ref, proj_ref):
        xhat, _ = _rms(x_ref[...])
        h = xhat * v_ref[V_GMIX:V_GMIX + 1, :] * (1.0 + v_ref[V_SC1:V_SC1 + 1, :]) + v_ref[V_SH1:V_SH1 + 1, :]
        hb = h.astype(BF16)
        h1_ref[...] = hb
        proj_ref[...] = _dot(hb, w_ref[...]).astype(BF16)

    return pl.pallas_call(
        lambda q_ref, *refs: body(q_ref, *refs[len(deps):]),
        grid_spec=pltpu.PrefetchScalarGridSpec(
            num_scalar_prefetch=1, grid=(s // ts,),
            in_specs=[_ANY] * len(deps) + [pl.BlockSpec((ts, D), lambda i, q: (i, 0)), _VMEM,
                                           pl.BlockSpec((None, D, C_IN), lambda i, q: (q[0], 0, 0))],
            out_specs=[pl.BlockSpec((ts, D), lambda i, q: (i, 0)), pl.BlockSpec((ts, C_IN), lambda i, q: (i, q[0]))]),
        out_shape=(jax.ShapeDtypeStruct((s, D), BF16), jax.ShapeDtypeStruct((s, D_IN), BF16)),
        compiler_params=_cparams(), name="fwd_in_own")(q_idx, *deps, x, vecs, w_in_g)


def _fwd_in_more(h1, w_in_g, proj, q_idx, ts, name, deps=()):
    s = h1.shape[0]

    def body(q_ref, h1_ref, w_ref, proj_in_ref, proj_ref):
        proj_ref[...] = _dot(h1_ref[...], w_ref[...]).astype(BF16)

    return pl.pallas_call(
        lambda q_ref, *refs: body(q_ref, *refs[len(deps):]),
        grid_spec=pltpu.PrefetchScalarGridSpec(
            num_scalar_prefetch=1, grid=(s // ts,),
            in_specs=[_ANY] * len(deps) + [pl.BlockSpec((ts, D), lambda i, q: (i, 0)),
                                           pl.BlockSpec((None, D, C_IN), lambda i, q: (q[0], 0, 0)), _ANY],
            out_specs=pl.BlockSpec((ts, C_IN), lambda i, q: (i, q[0]))),
        out_shape=jax.ShapeDtypeStruct((s, D_IN), BF16), input_output_aliases={len(deps) + 3: 0},
        compiler_params=_cparams(), name=name)(q_idx, *deps, h1, w_in_g, proj)


def _fwd_mix(proj, x, vecs, w_rga, w_rgx, w_out, ts, deps=()):
    s = x.shape[0]
    g = ts // TIME_BLOCKS

    def body(proj_ref, x_ref, v_ref, wa_ref, wx_ref, wo_ref, x1_ref, mg_ref, z1_ref, kept_ref, decay_ref,
             ua_buf, rx_buf, p_buf, q_buf, c_buf, hcarry):
        i = pl.program_id(0)

        @pl.when(i == 0)
        def _():
            ua_buf[...] = jnp.zeros(ua_buf.shape, F32)
            rx_buf[...] = jnp.zeros(rx_buf.shape, F32)
            hcarry[...] = jnp.zeros((8, D), F32)

        def seg(j):
            return proj_ref[:, j * D:(j + 1) * D].astype(F32)

        def vrow(j):
            return v_ref[j:j + 1, :]

        cb, cc, cx, rx, rg, ga, gb = (seg(j) for j in range(7))
        ua = cc * cx
        ua_late = _late_blocks(ua, ua_buf, g)
        rx_late = _late_blocks(rx, rx_buf, g)
        va = vrow(V_WA2) * ua + vrow(V_WA1) * _earlier(ua, 1, ua_late, g) + vrow(V_WA0) * _earlier(ua, 2, ua_late, g)
        u = (vrow(V_WB3) * rx + vrow(V_WB2) * _earlier(rx, 1, rx_late, g) + vrow(V_WB1) * _earlier(rx, 2, rx_late, g)
             + vrow(V_WB0) * _earlier(rx, 3, rx_late, g) + vrow(V_CBB))

        rows = lax.broadcasted_iota(jnp.int32, (ts, D), 0)
        row0 = jnp.logical_and(rows == 0, i == 0)
        r, ig, a, mult = _lru_gates(u, wa_ref, wx_ref, v_ref, row0)
        decay_ref[...] = a
        bx = mult * (ig * u)

        prods, sums = [a[0:g]], [bx[0:g]]
        for k in range(1, TIME_BLOCKS):
            ak = a[k * g:(k + 1) * g]
            sums.append(ak * sums[-1] + bx[k * g:(k + 1) * g])
            prods.append(ak * prods[-1])
        p_buf[...] = prods[-1]
        q_buf[...] = sums[-1]
        state = hcarry[0:1, :]
        for j in range(g):
            c_buf[j:j + 1, :] = state
            state = p_buf[j:j + 1, :] * state + q_buf[j:j + 1, :]
        hcarry[0:1, :] = state
        entering = c_buf[...]
        h = jnp.concatenate([sums[k] + prods[k] * entering for k in range(TIME_BLOCKS)], axis=0)

        gel, dgel = _gelu(rg)
        sga = jax.nn.sigmoid(ga)
        sgb = jax.nn.sigmoid(gb)
        for j, keep in enumerate((va, r, ig, sga, sgb, gel, dgel, mult, u, h)):
            kept_ref[:, j * D:(j + 1) * D] = keep.astype(BF16)
        merged = (sga * (cb * va) + sgb * (h * gel)).astype(BF16)
        mg_ref[...] = merged
        z1 = _dot(merged, wo_ref[...])
        z1_ref[...] = z1.astype(BF16)
        x1_ref[...] = x_ref[...] + vrow(V_GT1) * z1

    row = lambda i: (i, 0)
    return pl.pallas_call(
        _after(deps, body), grid=(s // ts,),
        out_shape=(jax.ShapeDtypeStruct((s, D), F32), jax.ShapeDtypeStruct((s, D), BF16), jax.ShapeDtypeStruct((s, D), BF16),
                   jax.ShapeDtypeStruct((s, N_KEPT * D), BF16), jax.ShapeDtypeStruct((s, D), F32)),
        in_specs=[_ANY] * len(deps) + [pl.BlockSpec((ts, D_IN), row), pl.BlockSpec((ts, D), row), _VMEM, _VMEM, _VMEM, _VMEM],
        out_specs=[pl.BlockSpec((ts, D), row)] * 3 + [pl.BlockSpec((ts, N_KEPT * D), row), pl.BlockSpec((ts, D), row)],
        scratch_shapes=[pltpu.VMEM((2, g + 8, D), F32), pltpu.VMEM((3, g + 8, D), F32), pltpu.VMEM((g, D), F32),
                        pltpu.VMEM((g, D), F32), pltpu.VMEM((g, D), F32), pltpu.VMEM((8, D), F32)],
        compiler_params=_cparams(), name="fwd_mix")(*deps, proj, x, vecs, w_rga, w_rgx, w_out)


def _ffn_loss(x1, target, vecs, w_gu_g, w_dn, ts):
    s = x1.shape[0]

    def body(x1_ref, t_ref, v_ref, wgu_ref, wdn_ref, dx1_ref, h2_ref, act_ref, dz2_ref, dgu_ref, sm_ref):
        @pl.when(pl.program_id(0) == 0)
        def _():
            sm_ref[...] = jnp.zeros((N_SMALL, D), F32)

        def vrow(j):
            return v_ref[j:j + 1, :]

        n_sub = 1
        rows = [slice(k * (ts // n_sub), (k + 1) * (ts // n_sub)) for k in range(n_sub)]
        subs = [dict(r=r, sums={}) for r in rows]

        def stage_norm(t):
            t["x1"] = x1_ref[t["r"], :]
            t["xh1"], t["rstd1"] = _rms(t["x1"])
            t["n2"] = t["xh1"] * vrow(V_GFFN)
            t["h2"] = (t["n2"] * (1.0 + vrow(V_SC2)) + vrow(V_SH2)).astype(BF16)
            h2_ref[t["r"], :] = t["h2"]

        def stage_up(t):
            h2 = t["h2"]
            g = jnp.concatenate([_dot(h2, wgu_ref[0]), _dot(h2, wgu_ref[1])], axis=1)
            t["up"] = jnp.concatenate([_dot(h2, wgu_ref[2]), _dot(h2, wgu_ref[3])], axis=1)
            t["g"] = g
            t["sg"] = jax.nn.sigmoid(g)
            t["silu"] = g * t["sg"]
            t["act"] = (t["silu"] * t["up"]).astype(BF16)
            act_ref[t["r"], :] = t["act"]

        def stage_down_loss(t):
            z2 = _dot(t["act"], wdn_ref[...])
            x2 = t["x1"] + vrow(V_GT2) * z2
            xh2, rstd2 = _rms(x2)
            err = xh2 * vrow(V_GFIN) - t_ref[t["r"], :]
            t["sums"][G_LOSS] = _rowsum((0.5 / D) * err * err)
            dy = err * (1.0 / D)
            t["sums"][G_GFIN] = _rowsum(dy * xh2)
            t["dx2"] = _rms_bwd(dy * vrow(V_GFIN), xh2, rstd2)
            t["sums"][G_GT2] = _rowsum(t["dx2"] * z2)
            t["dz2"] = (vrow(V_GT2) * t["dx2"]).astype(BF16)
            dz2_ref[t["r"], :] = t["dz2"]

        def stage_back_act(t):
            dact = _dot_nt(t["dz2"], wdn_ref[...])
            g, sg = t["g"], t["sg"]
            t["dgate"] = (dact * t["up"] * (sg * (1.0 + g * (1.0 - sg)))).astype(BF16)
            t["dup"] = (dact * t["silu"]).astype(BF16)
            dgu_ref[t["r"], 0:D_FF] = t["dgate"]
            dgu_ref[t["r"], D_FF:2 * D_FF] = t["dup"]

        def stage_back_norm(t):
            dgate, dup = t["dgate"], t["dup"]
            dh2 = (_dot_nt(dgate[:, 0:C_GU], wgu_ref[0]) + _dot_nt(dgate[:, C_GU:2 * C_GU], wgu_ref[1])
                   + _dot_nt(dup[:, 0:C_GU], wgu_ref[2]) + _dot_nt(dup[:, C_GU:2 * C_GU], wgu_ref[3]))
            t["sums"][G_SH2] = _rowsum(dh2)
            t["sums"][G_SC2] = _rowsum(dh2 * t["n2"])
            dn2 = dh2 * (1.0 + vrow(V_SC2))
            t["sums"][G_GFFN] = _rowsum(dn2 * t["xh1"])
            dx1_ref[t["r"], :] = t["dx2"] + _rms_bwd(dn2 * vrow(V_GFFN), t["xh1"], t["rstd1"])

        for stage in (stage_norm, stage_up, stage_down_loss, stage_back_act, stage_back_norm):
            for t in subs:
                stage(t)
        for j in subs[0]["sums"]:
            total = subs[0]["sums"][j]
            for t in subs[1:]:
                total = total + t["sums"][j]
            sm_ref[j:j + 1, :] += total

    row = lambda i: (i, 0)
    return pl.pallas_call(
        body, grid=(s // ts,),
        out_shape=(jax.ShapeDtypeStruct((s, D), F32), jax.ShapeDtypeStruct((s, D), BF16), jax.ShapeDtypeStruct((s, D_FF), BF16),
                   jax.ShapeDtypeStruct((s, D), BF16), jax.ShapeDtypeStruct((s, 2 * D_FF), BF16),
                   jax.ShapeDtypeStruct((N_SMALL, D), F32)),
        in_specs=[pl.BlockSpec((ts, D), row), pl.BlockSpec((ts, D), row), _VMEM, _VMEM, _VMEM],
        out_specs=[pl.BlockSpec((ts, D), row), pl.BlockSpec((ts, D), row), pl.BlockSpec((ts, D_FF), row),
                   pl.BlockSpec((ts, D), row), pl.BlockSpec((ts, 2 * D_FF), row), pl.BlockSpec((N_SMALL, D), lambda i: (0, 0))],
        compiler_params=_cparams(), name="ffn_loss")(x1, target, vecs, w_gu_g, w_dn)


def _bwd_mix(dx1, z1, merged, proj, kept, decay, vecs, w_rga, w_rgx, w_out, small, ts, deps=()):
    s = dx1.shape[0]
    nt = s // ts
    g = ts // TIME_BLOCKS
    assert g % 16 == 0

    def body(dx1_ref, z1_ref, mg_ref, proj_ref, kept_ref, decay_ref, hh_ref, v_ref, wa_ref, wx_ref,
             wo_ref, sm0_ref, dproj_ref, sm_ref, dwa_ref, dwx_ref, dwo_ref,
             h_buf, a_buf, dva_buf, du_buf, p_buf, q_buf, c_buf, lcarry):
        i = pl.program_id(0)
        first_tile = i == nt - 1

        @pl.when(i == 0)
        def _():
            a_buf[...] = jnp.zeros(a_buf.shape, F32)
            dva_buf[...] = jnp.zeros(dva_buf.shape, F32)
            du_buf[...] = jnp.zeros(du_buf.shape, F32)
            lcarry[...] = jnp.zeros((8, D), F32)
            sm_ref[...] = sm0_ref[...]
            dwa_ref[...] = jnp.zeros((HEADS, HB, HB), F32)
            dwx_ref[...] = jnp.zeros((HEADS, HB, HB), F32)
            dwo_ref[...] = jnp.zeros((D, D), F32)

        def seg(j):
            return proj_ref[:, j * D:(j + 1) * D].astype(F32)

        def vrow(j):
            return v_ref[j:j + 1, :]

        def acc(j, val):
            sm_ref[j:j + 1, :] += _rowsum(val)

        cb, cc, cx, rx = (seg(j) for j in range(4))
        ua = cc * cx
        va, r, ig, sga, sgb, gel, dgel, mult, u, h = (kept_ref[:, j * D:(j + 1) * D].astype(F32) for j in range(N_KEPT))
        a = decay_ref[...]
        rows = lax.broadcasted_iota(jnp.int32, (ts, D), 0)
        row0 = jnp.logical_and(rows == 0, first_tile)

        dx1 = dx1_ref[...]
        acc(G_GT1, dx1 * z1_ref[...].astype(F32))
        dz1 = (vrow(V_GT1) * dx1).astype(BF16)
        dwo_ref[...] += _dot_tn(mg_ref[...], dz1)
        dmg = _dot_nt(dz1, wo_ref[...])
        dya = dmg * sga
        dyb = dmg * sgb
        dproj_ref[:, 5 * D:6 * D] = (dya * (cb * va) * (1.0 - sga)).astype(BF16)
        dproj_ref[:, 6 * D:7 * D] = (dyb * (h * gel) * (1.0 - sgb)).astype(BF16)

        dproj_ref[:, 0:D] = (dya * va).astype(BF16)
        dva = dya * cb
        dva_early = _early_blocks(dva, dva_buf, g)
        dva1 = _later(dva, 1, dva_early, g)
        dva2 = _later(dva, 2, dva_early, g)
        dua = vrow(V_WA2) * dva + vrow(V_WA1) * dva1 + vrow(V_WA0) * dva2
        acc(G_WA2, ua * dva)
        acc(G_WA1, ua * dva1)
        acc(G_WA0, ua * dva2)
        dproj_ref[:, D:2 * D] = (dua * cx).astype(BF16)
        dproj_ref[:, 2 * D:3 * D] = (dua * cc).astype(BF16)

        dproj_ref[:, 4 * D:5 * D] = (dyb * h * dgel).astype(BF16)
        a_next = _later(a, 1, _early_blocks(a, a_buf, g), g)
        dh = dyb * gel
        last = TIME_BLOCKS - 1
        prods, sums = {last: a_next[last * g:]}, {last: dh[last * g:]}
        for k in range(last - 1, -1, -1):
            ak = a_next[k * g:(k + 1) * g]
            sums[k] = dh[k * g:(k + 1) * g] + ak * sums[k + 1]
            prods[k] = ak * prods[k + 1]
        p_buf[...] = prods[0]
        q_buf[...] = sums[0]
        state = lcarry[0:1, :]
        for j in range(g - 1, -1, -1):
            c_buf[j:j + 1, :] = state
            state = q_buf[j:j + 1, :] + p_buf[j:j + 1, :] * state
        lcarry[0:1, :] = state
        entering = c_buf[...]
        lam = jnp.concatenate([sums[k] + prods[k] * entering for k in range(TIME_BLOCKS)], axis=0)

        last = lax.broadcasted_iota(jnp.int32, hh_ref.shape, 0) == hh_ref.shape[0] - 1
        h_halo = [jnp.where(first_tile, 0.0, jnp.sum(jnp.where(last, hh_ref[...].astype(F32), 0.0), axis=0, keepdims=True))]
        da = lam * _earlier(h, 1, _late_blocks(h, h_buf, g, h_halo), g)
        dmult = jnp.where(row0, 0.0, lam * (ig * u))
        di = lam * mult * u
        du = lam * mult * ig
        dlog_a = da * a - dmult * (a * a) / mult
        lam_p = vrow(V_LAM)
        dr = dlog_a * (LRU_C * _log_sigmoid(lam_p))
        sm_ref[G_LAM:G_LAM + 1, :] += _rowsum(dlog_a * r) * (LRU_C * jax.nn.sigmoid(-lam_p))
        dpa = dr * r * (1.0 - r)
        dpx = di * ig * (1.0 - ig)
        acc(G_BA, dpa)
        acc(G_BX, dpx)
        dpab = dpa.astype(BF16)
        dpxb = dpx.astype(BF16)
        ub = u.astype(BF16)
        back = []
        for hd in range(HEADS):
            cols = slice(hd * HB, (hd + 1) * HB)
            back.append(_dot_nt(dpab[:, cols], wa_ref[hd]) + _dot_nt(dpxb[:, cols], wx_ref[hd]))
            dwa_ref[hd] += _dot_tn(ub[:, cols], dpab[:, cols])
            dwx_ref[hd] += _dot_tn(ub[:, cols], dpxb[:, cols])
        du = du + jnp.concatenate(back, axis=1)

        acc(G_CBB, du)
        du_early = _early_blocks(du, du_buf, g)
        du1 = _later(du, 1, du_early, g)
        du2 = _later(du, 2, du_early, g)
        du3 = _later(du, 3, du_early, g)
        dproj_ref[:, 3 * D:4 * D] = (vrow(V_WB3) * du + vrow(V_WB2) * du1 + vrow(V_WB1) * du2 + vrow(V_WB0) * du3).astype(BF16)
        acc(G_WB3, rx * du)
        acc(G_WB2, rx * du1)
        acc(G_WB1, rx * du2)
        acc(G_WB0, rx * du3)

    rev = lambda i: (nt - 1 - i, 0)
    h_halo16 = lambda i: (jnp.maximum((nt - 1 - i) * (ts // 16) - 1, 0), N_KEPT - 1)
    const2 = lambda i: (0, 0)
    const3 = lambda i: (0, 0, 0)
    return pl.pallas_call(
        _after(deps, body), grid=(nt,),
        out_shape=(jax.ShapeDtypeStruct((s, D_IN), BF16), jax.ShapeDtypeStruct((N_SMALL, D), F32),
                   jax.ShapeDtypeStruct((HEADS, HB, HB), F32), jax.ShapeDtypeStruct((HEADS, HB, HB), F32),
                   jax.ShapeDtypeStruct((D, D), F32)),
        in_specs=[_ANY] * len(deps) + [pl.BlockSpec((ts, D), rev), pl.BlockSpec((ts, D), rev), pl.BlockSpec((ts, D), rev),
                  pl.BlockSpec((ts, 4 * D), rev), pl.BlockSpec((ts, N_KEPT * D), rev), pl.BlockSpec((ts, D), rev),
                  pl.BlockSpec((16, D), h_halo16), _VMEM, _VMEM, _VMEM, _VMEM, _VMEM],
        out_specs=[pl.BlockSpec((ts, D_IN), rev), pl.BlockSpec((N_SMALL, D), const2),
                   pl.BlockSpec((HEADS, HB, HB), const3), pl.BlockSpec((HEADS, HB, HB), const3), pl.BlockSpec((D, D), const2)],
        scratch_shapes=[pltpu.VMEM((1, g + 8, D), F32), pltpu.VMEM((1, g + 8, D), F32),
                        pltpu.VMEM((2, g + 8, D), F32), pltpu.VMEM((3, g + 8, D), F32), pltpu.VMEM((g, D), F32),
                        pltpu.VMEM((g, D), F32), pltpu.VMEM((g, D), F32), pltpu.VMEM((8, D), F32)],
        compiler_params=_cparams(), name="bwd_mix")(*deps, dx1, z1, merged, proj, kept, decay, kept, vecs, w_rga,
                                                    w_rgx, w_out, small)


def _bwd_in(dproj, x, dx1, vecs, w_in_g, small, ts, deps=()):
    s = x.shape[0]

    def body(dp_ref, x_ref, dx1_ref, v_ref, w_ref, sm0_ref, gx_ref, sm_ref):
        @pl.when(pl.program_id(0) == 0)
        def _():
            sm_ref[...] = sm0_ref[...]

        def vrow(j):
            return v_ref[j:j + 1, :]

        dh1 = _dot_nt(dp_ref[:, 0:C_IN], w_ref[0])
        for k in range(1, N_CHIPS):
            dh1 += _dot_nt(dp_ref[:, k * C_IN:(k + 1) * C_IN], w_ref[k])
        xh, rstd = _rms(x_ref[...])
        sm_ref[G_SH1:G_SH1 + 1, :] += _rowsum(dh1)
        sm_ref[G_SC1:G_SC1 + 1, :] += _rowsum(dh1 * (xh * vrow(V_GMIX)))
        dn1 = dh1 * (1.0 + vrow(V_SC1))
        sm_ref[G_GMIX:G_GMIX + 1, :] += _rowsum(dn1 * xh)
        gx_ref[...] = dx1_ref[...] + _rms_bwd(dn1 * vrow(V_GMIX), xh, rstd)

    row = lambda i: (i, 0)
    return pl.pallas_call(
        _after(deps, body), grid=(s // ts,),
        out_shape=(jax.ShapeDtypeStruct((s, D), F32), jax.ShapeDtypeStruct((N_SMALL, D), F32)),
        in_specs=[_ANY] * len(deps) + [pl.BlockSpec((ts, D_IN), row), pl.BlockSpec((ts, D), row), pl.BlockSpec((ts, D), row),
                                       _VMEM, _VMEM, _VMEM],
        out_specs=[pl.BlockSpec((ts, D), row), pl.BlockSpec((N_SMALL, D), lambda i: (0, 0))],
        compiler_params=_cparams(), name="bwd_in")(*deps, dproj, x, dx1, vecs, w_in_g, small)


def _grad_w(a, b, n_col_blocks, ts, name, deps=()):
    s, m = a.shape
    tn = b.shape[1] // n_col_blocks
    n_steps = s // ts

    def body(a_ref, b_ref, o_ref, acc_ref):
        k = pl.program_id(1)

        @pl.when(k == 0)
        def _():
            acc_ref[...] = jnp.zeros((m, tn), F32)

        acc_ref[...] += _dot_tn(a_ref[...], b_ref[...])

        @pl.when(k == n_steps - 1)
        def _():
            o_ref[...] = acc_ref[...].astype(BF16)

    return pl.pallas_call(
        _after(deps, body), grid=(n_col_blocks, n_steps),
        out_shape=jax.ShapeDtypeStruct((n_col_blocks, m, tn), BF16),
        in_specs=[_ANY] * len(deps) + [pl.BlockSpec((ts, m), lambda n, k: (k, 0)), pl.BlockSpec((ts, tn), lambda n, k: (k, n))],
        out_specs=pl.BlockSpec((None, m, tn), lambda n, k: (n, 0, 0)),
        scratch_shapes=[pltpu.VMEM((m, tn), F32)],
        compiler_params=_cparams(2), name=name)(*deps, a, b)


def _ada_fwd(c_all, w_ada, b_ada):
    n = w_ada.shape[1]

    def body(c_ref, w_ref, b_ref, o_ref, ca_ref):
        c = c_ref[...]
        ca = c * jax.nn.sigmoid(c)
        ca_ref[...] = ca
        o_ref[...] = jnp.dot(ca, w_ref[...], preferred_element_type=F32, precision=lax.Precision.HIGHEST) + b_ref[...]

    return pl.pallas_call(
        body, out_shape=(jax.ShapeDtypeStruct((N_DEV, n), F32), jax.ShapeDtypeStruct((N_DEV, D), F32)),
        in_specs=[_VMEM] * 3, out_specs=[_VMEM] * 2, compiler_params=_cparams(0), name="ada_fwd")(c_all, w_ada, b_ada)


def _ada_bwd(c_act, dmod):
    n = dmod.shape[1]

    def body(c_ref, d_ref, o_ref):
        o_ref[...] = lax.dot_general(c_ref[...], d_ref[...], (((0,), (0,)), ((), ())), preferred_element_type=F32,
                                     precision=lax.Precision.HIGHEST)

    return pl.pallas_call(
        body, out_shape=jax.ShapeDtypeStruct((D, n), F32), in_specs=[_VMEM] * 2, out_specs=_VMEM,
        compiler_params=_cparams(0), name="ada_bwd")(c_act, dmod)


def _sum_small(parts):
    def body(p_ref, o_ref, d_ref):
        tot = p_ref[0]
        for dev in range(1, N_DEV):
            tot = tot + p_ref[dev]
        o_ref[...] = tot
        d_ref[...] = p_ref[:, 0:8, :]

    return pl.pallas_call(
        body, out_shape=(jax.ShapeDtypeStruct((N_SMALL, D), F32), jax.ShapeDtypeStruct((N_DEV, 8, D), F32)),
        in_specs=[_VMEM], out_specs=[_VMEM] * 2, compiler_params=_cparams(0), name="sum_small")(parts)


def _adamw(w, g, m, v, name, deps=()):
    rows, cols = w.shape
    tr = 128 if rows % 128 == 0 else (64 if rows % 64 == 0 else rows)

    def body(w_ref, g_ref, m_ref, v_ref, d_ref, nm_ref, nv_ref):
        g_ = g_ref[...]
        m_ = ADAM_B1 * m_ref[...] + (1.0 - ADAM_B1) * g_
        v_ = ADAM_B2 * v_ref[...] + (1.0 - ADAM_B2) * (g_ * g_)
        nm_ref[...] = m_
        nv_ref[...] = v_
        m_hat = m_ / (1.0 - ADAM_B1 ** ADAM_STEP)
        v_hat = v_ / (1.0 - ADAM_B2 ** ADAM_STEP)
        d_ref[...] = -ADAM_LR * (m_hat / (jnp.sqrt(v_hat) + ADAM_EPS) + ADAM_WD * w_ref[...])

    spec = pl.BlockSpec((tr, cols), lambda i: (i, 0))
    return pl.pallas_call(
        _after(deps, body), grid=(rows // tr,), out_shape=(jax.ShapeDtypeStruct((rows, cols), F32),) * 3,
        in_specs=[_ANY] * len(deps) + [spec] * 4, out_specs=[spec] * 3, compiler_params=_cparams(), name=name)(*deps, w, g, m, v)


def _adamw_small(items, name):
    n = len(items)

    def body(*refs):
        ins, outs = refs[:4 * n], refs[4 * n:]
        for k in range(n):
            w_ref, g_ref, m_ref, v_ref = ins[4 * k:4 * k + 4]
            d_ref, nm_ref, nv_ref = outs[3 * k:3 * k + 3]
            g_ = g_ref[...]
            m_ = ADAM_B1 * m_ref[...] + (1.0 - ADAM_B1) * g_
            v_ = ADAM_B2 * v_ref[...] + (1.0 - ADAM_B2) * (g_ * g_)
            nm_ref[...] = m_
            nv_ref[...] = v_
            m_hat = m_ / (1.0 - ADAM_B1 ** ADAM_STEP)
            v_hat = v_ / (1.0 - ADAM_B2 ** ADAM_STEP)
            d_ref[...] = -ADAM_LR * (m_hat / (jnp.sqrt(v_hat) + ADAM_EPS) + ADAM_WD * w_ref[...])

    out = pl.pallas_call(
        body, out_shape=tuple(jax.ShapeDtypeStruct(it[0].shape, F32) for it in items for _ in range(3)),
        in_specs=[_VMEM] * (4 * n), out_specs=[_VMEM] * (3 * n), name=name)(*[a for it in items for a in it])
    return [tuple(out[3 * k:3 * k + 3]) for k in range(n)]


def _adamw_halves(w, mine, other, m, v, c_idx, name, deps=()):
    r2, cols = mine.shape
    tr = next(t for t in (128, 64, 32, 16, 8) if r2 % t == 0)
    nh = r2 // tr

    def body(c_ref, w_ref, mine_ref, other_ref, m_ref, v_ref, g_ref, d_ref, nm_ref, nv_ref):
        g_ = jnp.where(pl.program_id(0) // nh == c_ref[0], mine_ref[...], other_ref[...])
        g_ref[...] = g_
        m_ = ADAM_B1 * m_ref[...] + (1.0 - ADAM_B1) * g_
        v_ = ADAM_B2 * v_ref[...] + (1.0 - ADAM_B2) * (g_ * g_)
        nm_ref[...] = m_
        nv_ref[...] = v_
        m_hat = m_ / (1.0 - ADAM_B1 ** ADAM_STEP)
        v_hat = v_ / (1.0 - ADAM_B2 ** ADAM_STEP)
        d_ref[...] = -ADAM_LR * (m_hat / (jnp.sqrt(v_hat) + ADAM_EPS) + ADAM_WD * w_ref[...])

    full = pl.BlockSpec((tr, cols), lambda i, c: (i, 0))
    mine_spec = pl.BlockSpec((tr, cols), lambda i, c: (jnp.clip(i - c[0] * nh, 0, nh - 1), 0))
    other_spec = pl.BlockSpec((tr, cols), lambda i, c: (jnp.clip(i - (1 - c[0]) * nh, 0, nh - 1), 0))
    return pl.pallas_call(
        lambda c_ref, *refs: body(c_ref, *refs[len(deps):]),
        grid_spec=pltpu.PrefetchScalarGridSpec(
            num_scalar_prefetch=1, grid=(2 * nh,),
            in_specs=[_ANY] * len(deps) + [full, mine_spec, other_spec, full, full], out_specs=[full] * 4),
        out_shape=(jax.ShapeDtypeStruct((2 * r2, cols), F32),) * 4, compiler_params=_cparams(), name=name,
    )(c_idx, *deps, w, mine, other, m, v)


def _add_halves(g, recv, c_idx, name):
    n, _, r2, cols = g.shape

    def body(c_ref, g_ref, r_ref, o_ref):
        o_ref[...] = (g_ref[...].astype(F32) + r_ref[...].astype(F32)).astype(BF16)

    return pl.pallas_call(
        body,
        grid_spec=pltpu.PrefetchScalarGridSpec(
            num_scalar_prefetch=1, grid=(n,),
            in_specs=[pl.BlockSpec((None, None, r2, cols), lambda k, c: (k, c[0], 0, 0)),
                      pl.BlockSpec((None, r2, cols), lambda k, c: (k, 0, 0))],
            out_specs=pl.BlockSpec((None, r2, cols), lambda k, c: (k, 0, 0))),
        out_shape=jax.ShapeDtypeStruct((n, r2, cols), BF16), compiler_params=_cparams(), name=name)(c_idx, g, recv)


def _sum_chips(parts, name):
    n, r2, cols = parts.shape
    tr = next(t for t in (128, 176, 64, 32, 16) if r2 % t == 0)

    def body(p_ref, o_ref):
        o_ref[...] = ((p_ref[0].astype(F32) + p_ref[1].astype(F32)) + p_ref[2].astype(F32)) + p_ref[3].astype(F32)

    return pl.pallas_call(
        body, grid=(r2 // tr,), out_shape=jax.ShapeDtypeStruct((r2, cols), F32),
        in_specs=[pl.BlockSpec((n, tr, cols), lambda i: (0, i, 0))], out_specs=pl.BlockSpec((tr, cols), lambda i: (i, 0)),
        compiler_params=_cparams(), name=name)(parts)


def _place():
    x, y, c = lax.axis_index("x"), lax.axis_index("y"), lax.axis_index("c")
    return x, y, c, 2 * x + y


def _flip(v, bit):
    return 1 - v if bit else v


def _allgather8(v, name, deps=()):
    r, n = v.shape

    def body(*refs):
        v_ref, out_ref, send_sems, recv_sems, local_sem = refs[len(deps):]
        x, y, c, _ = _place()
        me = 4 * x + 2 * y + c
        mine = pltpu.make_async_copy(v_ref, out_ref.at[me], local_sem)
        mine.start()
        sends = []
        for rel in range(1, N_DEV):
            peer = (_flip(x, rel & 4), _flip(y, rel & 2), _flip(c, rel & 1))
            cp = pltpu.make_async_remote_copy(v_ref, out_ref.at[me], send_sems.at[rel - 1], recv_sems.at[rel - 1],
                                              device_id=peer, device_id_type=MESH)
            cp.start()
            sends.append(cp)
        for rel in range(1, N_DEV):
            peer = (_flip(x, rel & 4), _flip(y, rel & 2), _flip(c, rel & 1))
            peer_idx = 4 * peer[0] + 2 * peer[1] + peer[2]
            pltpu.make_async_remote_copy(v_ref, out_ref.at[peer_idx], send_sems.at[rel - 1], recv_sems.at[rel - 1],
                                         device_id=peer, device_id_type=MESH).wait_recv()
        for cp in sends:
            cp.wait_send()
        mine.wait()

    return pl.pallas_call(
        body, out_shape=jax.ShapeDtypeStruct((N_DEV, r, n), F32), in_specs=[_ANY] * len(deps) + [_VMEM], out_specs=_VMEM,
        scratch_shapes=[pltpu.SemaphoreType.DMA((N_DEV - 1,)), pltpu.SemaphoreType.DMA((N_DEV - 1,)), pltpu.SemaphoreType.DMA(())],
        name=name)(*deps, v)


def _gather_weights(shards):
    nw = len(shards)

    def body(*refs):
        w_refs, out_refs = refs[:nw], refs[nw:2 * nw]
        send_sems, recv_sems = refs[2 * nw:]
        x, y, c, p = _place()
        sibling = (x, y, 1 - c)
        sends = []
        for j in range(1, N_CHIPS):
            peer = (_flip(x, j & 2), _flip(y, j & 1), c)
            for w in range(nw):
                cp = pltpu.make_async_remote_copy(w_refs[w].at[c], out_refs[w].at[p, c], send_sems.at[w * 6 + j - 1],
                                                  recv_sems.at[w * 6 + j - 1], device_id=peer, device_id_type=MESH)
                cp.start()
                sends.append(cp)
        for j in range(1, N_CHIPS):
            peer = (_flip(x, j & 2), _flip(y, j & 1), c)
            q = 2 * peer[0] + peer[1]
            for w in range(nw):
                pltpu.make_async_remote_copy(w_refs[w].at[c], out_refs[w].at[q, c], send_sems.at[w * 6 + j - 1],
                                             recv_sems.at[w * 6 + j - 1], device_id=peer, device_id_type=MESH).wait_recv()
                cp = pltpu.make_async_remote_copy(out_refs[w].at[q, c], out_refs[w].at[q, c], send_sems.at[w * 6 + 2 + j],
                                                  recv_sems.at[w * 6 + 2 + j], device_id=sibling, device_id_type=MESH)
                cp.start()
                sends.append(cp)
        for j in range(1, N_CHIPS):
            q = 2 * _flip(x, j & 2) + _flip(y, j & 1)
            for w in range(nw):
                pltpu.make_async_remote_copy(out_refs[w].at[q, 1 - c], out_refs[w].at[q, 1 - c], send_sems.at[w * 6 + 2 + j],
                                             recv_sems.at[w * 6 + 2 + j], device_id=sibling, device_id_type=MESH).wait_recv()
        for cp in sends:
            cp.wait_send()

    return pl.pallas_call(
        body, out_shape=tuple(jax.ShapeDtypeStruct((N_CHIPS,) + s.shape, s.dtype) for s in shards),
        in_specs=[_ANY] * nw, out_specs=[_ANY] * nw,
        scratch_shapes=[pltpu.SemaphoreType.DMA((6 * nw,)), pltpu.SemaphoreType.DMA((6 * nw,))],
        name="gather_weights")(*shards)


def _swap_halves(grads):
    nw = len(grads)

    def body(*refs):
        g_refs, out_refs = refs[:nw], refs[nw:2 * nw]
        send_sems, recv_sems = refs[2 * nw:]
        x, y, c, _ = _place()
        sibling = (x, y, 1 - c)
        sends = []
        for w in range(nw):
            for k in range(N_CHIPS):
                cp = pltpu.make_async_remote_copy(g_refs[w].at[k, 1 - c], out_refs[w].at[k], send_sems.at[w * N_CHIPS + k],
                                                  recv_sems.at[w * N_CHIPS + k], device_id=sibling, device_id_type=MESH)
                cp.start()
                sends.append(cp)
        for cp in sends:
            cp.wait_recv()
        for cp in sends:
            cp.wait_send()

    return pl.pallas_call(
        body, out_shape=tuple(jax.ShapeDtypeStruct((N_CHIPS,) + g.shape[2:], g.dtype) for g in grads),
        in_specs=[_ANY] * nw, out_specs=[_ANY] * nw,
        scratch_shapes=[pltpu.SemaphoreType.DMA((N_CHIPS * nw,)), pltpu.SemaphoreType.DMA((N_CHIPS * nw,))],
        name="swap_halves")(*grads)


def _scatter_chips(parts):
    nw = len(parts)

    def body(*refs):
        p_refs, out_refs = refs[:nw], refs[nw:2 * nw]
        send_sems, recv_sems = refs[2 * nw:]
        x, y, c, p = _place()
        sends = []
        for j in range(1, N_CHIPS):
            peer = (_flip(x, j & 2), _flip(y, j & 1), c)
            q = 2 * peer[0] + peer[1]
            for w in range(nw):
                cp = pltpu.make_async_remote_copy(p_refs[w].at[q], out_refs[w].at[p], send_sems.at[w * 3 + j - 1],
                                                  recv_sems.at[w * 3 + j - 1], device_id=peer, device_id_type=MESH)
                cp.start()
                sends.append(cp)
        for j in range(1, N_CHIPS):
            peer = (_flip(x, j & 2), _flip(y, j & 1), c)
            q = 2 * peer[0] + peer[1]
            for w in range(nw):
                pltpu.make_async_remote_copy(p_refs[w].at[q], out_refs[w].at[q], send_sems.at[w * 3 + j - 1],
                                             recv_sems.at[w * 3 + j - 1], device_id=peer, device_id_type=MESH).wait_recv()
        for cp in sends:
            cp.wait_send()

    return pl.pallas_call(
        body, out_shape=tuple(jax.ShapeDtypeStruct(s.shape, s.dtype) for s in parts),
        in_specs=[_ANY] * nw, out_specs=[_ANY] * nw,
        scratch_shapes=[pltpu.SemaphoreType.DMA((3 * nw,)), pltpu.SemaphoreType.DMA((3 * nw,))],
        name="scatter_chips")(*parts)


def _share_halves(halves):
    nw = len(halves)

    def body(*refs):
        h_refs, out_refs = refs[:nw], refs[nw:2 * nw]
        send_sems, recv_sems = refs[2 * nw:]
        x, y, c, _ = _place()
        sends = []
        for w in range(nw):
            cp = pltpu.make_async_remote_copy(h_refs[w], out_refs[w], send_sems.at[w], recv_sems.at[w],
                                              device_id=(x, y, 1 - c), device_id_type=MESH)
            cp.start()
            sends.append(cp)
        for cp in sends:
            cp.wait_recv()
        for cp in sends:
            cp.wait_send()

    return pl.pallas_call(
        body, out_shape=tuple(jax.ShapeDtypeStruct(s.shape, s.dtype) for s in halves),
        in_specs=[_ANY] * nw, out_specs=[_ANY] * nw,
        scratch_shapes=[pltpu.SemaphoreType.DMA((nw,)), pltpu.SemaphoreType.DMA((nw,))],
        name="share_halves")(*halves)


_HBM = pl.BlockSpec(memory_space=pltpu.HBM)
_SEM = pl.BlockSpec(memory_space=pltpu.SEMAPHORE)
_EFFECT = pltpu.SideEffectType.DATAFLOW_SIDE_EFFECTING


def _xchg_start(name, plan, n_copies, srcs, lands, after=()):
    bufs = list(srcs) + list(lands)
    ns, nb = len(srcs), len(srcs) + len(lands)

    def body(*refs):
        send_sems, recv_sems, token = refs[nb + len(after)], refs[nb + len(after) + 1], refs[-1]
        for i, (src, dst, peer, _) in enumerate(plan(_place(), refs[:ns], refs[ns:nb])):
            pltpu.make_async_remote_copy(src, dst, send_sems.at[i], recv_sems.at[i], device_id=peer, device_id_type=MESH).start()
        token[...] = jnp.zeros_like(token)

    out = pl.pallas_call(
        body, name=name,
        out_shape=(pltpu.SemaphoreType.DMA((n_copies,)), pltpu.SemaphoreType.DMA((n_copies,)),
                   *[pltpu.HBM(a.shape, a.dtype) for a in bufs], jax.ShapeDtypeStruct((8, 128), F32)),
        in_specs=[_HBM] * nb + [_ANY] * len(after), out_specs=(_SEM, _SEM, *[_HBM] * nb, _VMEM),
        input_output_aliases={i: 2 + i for i in range(nb)},
        compiler_params=pltpu.CompilerParams(has_side_effects=_EFFECT),
    )(*[pltpu.with_memory_space_constraint(a, pltpu.HBM) for a in bufs], *after)
    return (out[0], out[1]), out[2:2 + ns], out[2 + ns:2 + nb], out[-1]


def _xchg_wait(name, plan, sems, srcs, lands, after, sem_ids=None):
    bufs = list(srcs) + list(lands)
    ns, nb = len(srcs), len(srcs) + len(lands)

    def body(*refs):
        send_sems, recv_sems = refs[nb], refs[nb + 1]
        copies = plan(_place(), refs[:ns], refs[ns:nb])
        ids = range(len(copies)) if sem_ids is None else sem_ids
        for i, (src, _, peer, mine) in zip(ids, copies, strict=True):
            if i is not None:
                cp = pltpu.make_async_remote_copy(src, mine, send_sems.at[i], recv_sems.at[i], device_id=peer,
                                                  device_id_type=MESH)
                cp.wait_send()
                cp.wait_recv()

    out = pl.pallas_call(
        body, name=name, out_shape=tuple(pltpu.HBM(a.shape, a.dtype) for a in bufs),
        in_specs=[_HBM] * nb + [_SEM, _SEM] + [_ANY] * len(after), out_specs=tuple([_HBM] * nb),
        input_output_aliases={i: i for i in range(nb)},
        compiler_params=pltpu.CompilerParams(has_side_effects=_EFFECT),
    )(*bufs, *sems, *after)
    return out[:ns], out[ns:]


def _other_chips(place, which=(1, 2, 3)):
    x, y, c, _ = place
    return [((_flip(x, j & 2), _flip(y, j & 1), c), 2 * _flip(x, j & 2) + _flip(y, j & 1)) for j in which]


def _plan_gather_ici(chips):
    def plan(place, src_refs, land_refs):
        _, _, c, p = place
        return [(s.at[c], l.at[p, c], peer, l.at[q, c]) for s, l, which in zip(src_refs, land_refs, chips, strict=True)
                for peer, q in _other_chips(place, which)]
    return plan


def _plan_relay(which):
    def plan(place, src_refs, land_refs):
        x, y, c, _ = place
        return [(l.at[q, c], l.at[q, c], (x, y, 1 - c), l.at[q, 1 - c]) for l in land_refs for _, q in _other_chips(place, which)]
    return plan


def _plan_swap(place, src_refs, land_refs):
    x, y, c, _ = place
    return [(s.at[k, 1 - c], l.at[k], (x, y, 1 - c), l.at[k]) for s, l in zip(src_refs, land_refs) for k in range(N_CHIPS)]


def _plan_scatter(place, src_refs, land_refs):
    _, _, _, p = place
    return [(s.at[q], l.at[p], peer, l.at[q]) for s, l in zip(src_refs, land_refs) for peer, q in _other_chips(place)]


def _plan_share(place, src_refs, land_refs):
    x, y, c, _ = place
    return [(s, l, (x, y, 1 - c), l) for s, l in zip(src_refs, land_refs)]


def _pack_rows(parts, n_rows, name, deps=()):
    def body(*refs):
        refs = refs[len(deps):]
        out_ref = refs[-1]
        out_ref[...] = jnp.zeros((n_rows, D), F32)
        at = 0
        for ref in refs[:-1]:
            k = ref.shape[0]
            out_ref[at:at + k, :] = ref[...]
            at += k

    return pl.pallas_call(
        body, out_shape=jax.ShapeDtypeStruct((n_rows, D), F32), in_specs=[_ANY] * len(deps) + [_VMEM] * len(parts),
        out_specs=_VMEM, name=name)(*deps, *parts)


TS_MM = 512
TS_IN = 1024
TS_GW = 1024
TS_MIX = 256


def _halved(a):
    n, r, cols = a.shape
    return a.reshape(n, 2, r // 2, cols)


def _rs_swap(name, grads, after=()):
    lands = [lax.empty((N_CHIPS,) + g.shape[2:], g.dtype) for g in grads]
    sems, grads, lands, token = _xchg_start(name + "_swap", _plan_swap, N_CHIPS * len(grads), grads, lands, after)
    return name, sems, grads, lands, token


def _rs_scatter(handle, after, chip, ci):
    name, sems, grads, lands, _ = handle
    grads, from_sibling = _xchg_wait(name + "_swap_wait", _plan_swap, sems, grads, lands, after)
    c_arr = jnp.reshape(ci, (1,)).astype(jnp.int32)
    pair_sums = [_add_halves(g, r, c_arr, "%s_add_halves_%d" % (name, k)) for k, (g, r) in enumerate(zip(grads, from_sibling))]
    lands = [lax.dynamic_update_index_in_dim(lax.empty(p.shape, p.dtype), lax.dynamic_index_in_dim(p, chip, 0, keepdims=False),
                                             chip, 0) for p in pair_sums]
    sems, pair_sums, lands, token = _xchg_start(name + "_scatter", _plan_scatter, 3 * len(pair_sums), pair_sums, lands)
    return name, sems, pair_sums, lands, token


def _rs_share(handle, after):
    name, sems, pair_sums, lands, _ = handle
    _, by_chip = _xchg_wait(name + "_scatter_wait", _plan_scatter, sems, pair_sums, lands, after)
    halves = [_sum_chips(b, "%s_sum_chips_%d" % (name, k)) for k, b in enumerate(by_chip)]
    lands = [lax.empty(h.shape, h.dtype) for h in halves]
    sems, halves, lands, token = _xchg_start(name + "_share", _plan_share, len(halves), halves, lands)
    return name, sems, halves, lands, token


def _rs_end(handle, after):
    name, sems, halves, lands, _ = handle
    halves, others = _xchg_wait(name + "_share_wait", _plan_share, sems, halves, lands, after)
    return list(zip(halves, others))


def kernel(x, c, w_ada, b_ada, g_norm_mix, w_in, conv_a_w, conv_b_w, conv_b_bias, w_rg_a, b_rg_a, w_rg_x, b_rg_x, lru_lambda, w_out, g_norm_ffn, w_gate_up, w_down, g_norm_final, loss_target, m_w_ada, m_b_ada, m_g_norm_mix, m_w_in, m_conv_a_w, m_conv_b_w, m_conv_b_bias, m_w_rg_a, m_b_rg_a, m_w_rg_x, m_b_rg_x, m_lru_lambda, m_w_out, m_g_norm_ffn, m_w_gate_up, m_w_down, m_g_norm_final, v_w_ada, v_b_ada, v_g_norm_mix, v_w_in, v_conv_a_w, v_conv_b_w, v_conv_b_bias, v_w_rg_a, v_b_rg_a, v_w_rg_x, v_b_rg_x, v_lru_lambda, v_w_out, v_g_norm_ffn, v_w_gate_up, v_w_down, v_g_norm_final):
    xi, yi, ci = lax.axis_index("x"), lax.axis_index("y"), lax.axis_index("c")
    chip = 2 * xi + yi
    me = 2 * chip + ci
    n_ada = w_ada.shape[2]

    def widen(w):
        return jnp.pad(w, ((0, 0), (0, D - w.shape[1])))

    got = _allgather8(_pack_rows([c, widen(conv_a_w[0]), widen(conv_b_w[0])], 8, "pack_c_conv"), "gather_c_conv")
    c_all = got[:, 0, :]
    conv_full = got[::2, 1:8, :D // N_CHIPS].transpose(1, 0, 2).reshape(7, D)

    mod_part, c_act = _ada_fwd(c_all, w_ada[0], lax.dynamic_slice_in_dim(b_ada, chip * n_ada, n_ada, axis=1))
    mod_all = _allgather8(mod_part, "gather_mod")
    mod_mine = lax.dynamic_index_in_dim(mod_all, me, axis=1, keepdims=False)[::2].reshape(6, D)
    vecs = _pack_rows([mod_mine, g_norm_mix, g_norm_ffn, g_norm_final.reshape(1, D), conv_b_bias, b_rg_a, b_rg_x, lru_lambda,
                       conv_full], N_VEC, "pack_vecs")

    def rg_shard(w):
        return w[0].astype(BF16).reshape(2, HEADS * HB // N_CHIPS // 2, HB)

    shards = [w_in[0].astype(BF16).reshape(2, D // 2, C_IN), rg_shard(w_rg_a), rg_shard(w_rg_x),
              w_out[0].astype(BF16).reshape(2, D // N_CHIPS // 2, D), w_gate_up[0].astype(BF16).reshape(2, D // 2, C_GU),
              w_down[0].astype(BF16).reshape(2, D_FF // N_CHIPS // 2, D)]
    lands = [lax.dynamic_update_index_in_dim(lax.empty((N_CHIPS,) + s.shape, s.dtype), s, chip, 0) for s in shards]

    def send(name, first, last, after, chips):
        copies = [(k, j) for k, which in zip(range(first, last), chips, strict=True) for j in which]
        sems, srcs, zone, token = _xchg_start(name + "_ici", _plan_gather_ici(chips), len(copies), shards[first:last],
                                              lands[first:last], after)
        shards[first:last], lands[first:last] = srcs, zone
        return sems, copies, token

    def arrive(name, sent, first, last, after):
        sems, copies, _ = sent
        chips = [tuple(j for k, j in copies if k == want) for want in range(first, last)]
        ids = [copies.index((k, j)) for k, which in zip(range(first, last), chips) for j in which]
        srcs, zone = _xchg_wait(name + "_ici_wait", _plan_gather_ici(chips), sems, shards[first:last], lands[first:last], after,
                                ids)
        shards[first:last], lands[first:last] = srcs, zone

    def relay(name, first, last, which):
        plan = _plan_relay(which)
        sems, _, zone, token = _xchg_start(name + "_d2d", plan, len(which) * (last - first), [], lands[first:last])
        lands[first:last] = zone
        return name, plan, sems, first, last, token

    def relayed(handle, after):
        name, plan, sems, first, last, _ = handle
        lands[first:last] = _xchg_wait(name + "_d2d_wait", plan, sems, [], lands[first:last], after)[1]

    def to_blocks(v):
        return v.reshape(-1, TS_MIX // TIME_BLOCKS, TIME_BLOCKS, D).transpose(0, 2, 1, 3).reshape(v.shape)

    def from_blocks(v):
        return v.reshape(-1, TIME_BLOCKS, TS_MIX // TIME_BLOCKS, D).transpose(0, 2, 1, 3).reshape(v.shape)

    def chip_index(j):
        return jnp.reshape(chip ^ j, (1,)).astype(jnp.int32)

    def wg_in():
        return lands[0].reshape(N_CHIPS, D, C_IN)

    xs, target = to_blocks(x[0]), to_blocks(loss_target[0])
    sent_near = send("gather_in_near", 0, 1, [vecs], [(1, 2)])
    ts_in = min(TS_IN, xs.shape[0])
    h1, proj = _fwd_in_first(xs, vecs, wg_in(), chip_index(0), ts_in, deps=[sent_near[-1]])
    arrive("gather_in_near", sent_near, 0, 1, [proj])
    near = relay("gather_in_near", 0, 1, (1, 2))
    sent_rest = send("gather_rest", 0, 6, [near[-1]], [(3,)] + [(1, 2, 3)] * 5)
    relayed(near, [sent_rest[-1]])
    proj = _fwd_in_more(h1, wg_in(), proj, chip_index(1), ts_in, "fwd_in_y")
    proj = _fwd_in_more(h1, wg_in(), proj, chip_index(2), ts_in, "fwd_in_x")
    arrive("gather_in_far", sent_rest, 0, 1, [proj])
    far = relay("gather_in_far", 0, 1, (3,))
    arrive("gather_mix", sent_rest, 1, 4, [far[-1]])
    relayed(far, [far[-1]])
    mix = relay("gather_mix", 1, 4, (1, 2, 3))
    proj = _fwd_in_more(h1, wg_in(), proj, chip_index(3), ts_in, "fwd_in_xy", deps=[mix[-1]])
    relayed(mix, [proj])
    wg_rga, wg_rgx, wg_out = lands[1:4]
    wg_out = wg_out.reshape(D, D)

    def rg_full(wg):
        return wg.reshape(N_CHIPS, HEADS, HB // N_CHIPS, HB).transpose(1, 0, 2, 3).reshape(HEADS, HB, HB)

    wg_rga, wg_rgx = rg_full(wg_rga), rg_full(wg_rgx)

    arrive("gather_ffn", sent_rest, 4, 6, [proj])
    ffn = relay("gather_ffn", 4, 6, (1, 2, 3))
    x1, merged, z1, kept, decay = _fwd_mix(proj, xs, vecs, wg_rga, wg_rgx, wg_out, TS_MIX, deps=[ffn[-1]])
    relayed(ffn, [x1])
    wg_gu, wg_dn = lands[4:6]
    wg_gu, wg_dn = wg_gu.reshape(N_CHIPS, D, C_GU), wg_dn.reshape(D_FF, D)
    dx1, h2, act, dz2, dgu, sm_ffn = _ffn_loss(x1, target, vecs, wg_gu, wg_dn, TS_MIX)

    def rg_chunks(dw):
        return _halved(dw.reshape(HEADS, N_CHIPS, HB // N_CHIPS, HB).transpose(1, 0, 2, 3).reshape(N_CHIPS, HB, HB).astype(BF16))

    ts_gw = min(TS_GW, xs.shape[0])
    g_dn = _grad_w(act, dz2, 1, ts_gw, "grad_w_down")
    g_gu = _grad_w(h2, dgu, N_CHIPS, ts_gw, "grad_w_gate_up")
    rs_b = _rs_swap("rs_b", [_halved(g_gu), _halved(g_dn.reshape(N_CHIPS, D_FF // N_CHIPS, D))])
    dproj, sm_mix, dw_rga, dw_rgx, dw_out = _bwd_mix(dx1, z1, merged, proj, kept, decay, vecs, wg_rga, wg_rgx, wg_out, sm_ffn, TS_MIX,
                                                     deps=[rs_b[-1]])
    rs_b = _rs_scatter(rs_b, [dproj], chip, ci)
    g_in = _grad_w(h1, dproj, N_CHIPS, ts_gw, "grad_w_in", deps=[rs_b[-1]])
    rs_b = _rs_share(rs_b, [g_in])
    rs_a = _rs_swap("rs_a", [_halved(g_in), rg_chunks(dw_rga), rg_chunks(dw_rgx),
                             _halved(dw_out.astype(BF16).reshape(N_CHIPS, D // N_CHIPS, D))], after=[rs_b[-1]])

    c_arr = jnp.reshape(ci, (1,)).astype(jnp.int32)

    def step(name, w, g, m, v, deps=()):
        shape = w.shape
        two_d = (-1, shape[-1])
        d, nm, nv = _adamw(w.reshape(two_d), g.reshape(two_d), m.reshape(two_d), v.reshape(two_d), "adamw_" + name, deps)
        return g.reshape(shape), d.reshape(shape), nm.reshape(shape), nv.reshape(shape)

    def step_halves(name, w, halves, m, v, deps=()):
        shape = w.shape
        two_d = (-1, shape[-1])
        out = _adamw_halves(w.reshape(two_d), halves[0], halves[1], m.reshape(two_d), v.reshape(two_d), c_arr, "adamw_" + name, deps)
        return tuple(a.reshape(shape) for a in out)

    def shard_cols(row_block):
        return lax.dynamic_slice_in_dim(row_block, chip * (D // N_CHIPS), D // N_CHIPS, axis=1)

    gw_gu, gw_dn = _rs_end(rs_b, [rs_a[-1]])
    res = {
        "w_gate_up": step_halves("w_gate_up", w_gate_up, gw_gu, m_w_gate_up, v_w_gate_up, [rs_a[-1]]),
        "w_down": step_halves("w_down", w_down, gw_dn, m_w_down, v_w_down, [rs_a[-1]]),
    }
    rs_a = _rs_scatter(rs_a, [res["w_gate_up"][1], res["w_down"][1]], chip, ci)
    grad_x, sm_in = _bwd_in(dproj, xs, dx1, vecs, wg_in(), sm_mix, TS_MM, deps=[rs_a[-1]])
    rs_a = _rs_share(rs_a, [grad_x])

    small, per_dev = _sum_small(_allgather8(sm_in, "gather_small", deps=[rs_a[-1]]))
    dmod_all = per_dev[:, 0:6, :].reshape(N_DEV, 6 * D)
    grad_w_ada = _ada_bwd(c_act, lax.dynamic_slice_in_dim(dmod_all, chip * n_ada, n_ada, axis=1))
    grad_b_ada = small[0:6].reshape(1, 6 * D)
    res["w_ada"] = step("w_ada", w_ada, grad_w_ada[None], m_w_ada, v_w_ada)
    small_sets = {
        "b_ada": (b_ada.reshape(6, D), grad_b_ada.reshape(6, D), m_b_ada.reshape(6, D), v_b_ada.reshape(6, D)),
        "g_norm_mix": (g_norm_mix, small[G_GMIX:G_GMIX + 1], m_g_norm_mix, v_g_norm_mix),
        "conv_a_w": (conv_a_w[0], shard_cols(small[G_WA0:G_WA0 + 3]), m_conv_a_w[0], v_conv_a_w[0]),
        "conv_b_w": (conv_b_w[0], shard_cols(small[G_WB0:G_WB0 + 4]), m_conv_b_w[0], v_conv_b_w[0]),
        "conv_b_bias": (conv_b_bias, small[G_CBB:G_CBB + 1], m_conv_b_bias, v_conv_b_bias),
        "b_rg_a": (b_rg_a, small[G_BA:G_BA + 1], m_b_rg_a, v_b_rg_a),
        "b_rg_x": (b_rg_x, small[G_BX:G_BX + 1], m_b_rg_x, v_b_rg_x),
        "lru_lambda": (lru_lambda, small[G_LAM:G_LAM + 1], m_lru_lambda, v_lru_lambda),
        "g_norm_ffn": (g_norm_ffn, small[G_GFFN:G_GFFN + 1], m_g_norm_ffn, v_g_norm_ffn),
        "g_norm_final": (g_norm_final.reshape(1, D), small[G_GFIN:G_GFIN + 1], m_g_norm_final.reshape(1, D),
                         v_g_norm_final.reshape(1, D)),
    }
    stepped = _adamw_small(list(small_sets.values()), "adamw_small")
    for (n, (w_, g_, _, _)), (d_, nm_, nv_) in zip(small_sets.items(), stepped):
        shape = (1,) + w_.shape if n.startswith("conv_") and n != "conv_b_bias" else w_.shape
        res[n] = tuple(a.reshape(shape) for a in (g_, d_, nm_, nv_))
    gw_in, gw_rga, gw_rgx, gw_out = _rs_end(rs_a, [res[n][1] for n in res])
    res["w_in"] = step_halves("w_in", w_in, gw_in, m_w_in, v_w_in)
    res["w_rg_a"] = step_halves("w_rg_a", w_rg_a, gw_rga, m_w_rg_a, v_w_rg_a)
    res["w_rg_x"] = step_halves("w_rg_x", w_rg_x, gw_rgx, m_w_rg_x, v_w_rg_x)
    res["w_out"] = step_halves("w_out", w_out, gw_out, m_w_out, v_w_out)
    res["b_ada"] = tuple(a.reshape(1, 6 * D) for a in res["b_ada"])
    res["g_norm_final"] = tuple(a.reshape(D) for a in res["g_norm_final"])
    names = ["w_ada", "b_ada", "g_norm_mix", "w_in", "conv_a_w", "conv_b_w", "conv_b_bias", "w_rg_a", "b_rg_a", "w_rg_x",
             "b_rg_x", "lru_lambda", "w_out", "g_norm_ffn", "w_gate_up", "w_down", "g_norm_final"]
    loss = jnp.sum(small[G_LOSS])
    return (loss, from_blocks(grad_x)[None], *[res[n][0] for n in names], *[res[n][1] for n in names],
            *[res[n][2] for n in names], *[res[n][3] for n in names])
```

```python
import functools

import jax
import jax.numpy as jnp
from jax import lax
from jax.experimental import pallas as pl
from jax.experimental.pallas import tpu as pltpu

F32 = jnp.float32
BF16 = jnp.bfloat16
MESH = pl.DeviceIdType.MESH

D = 1024
N_CHIPS = 4
N_DEV = 8
D_IN = 7 * D
C_IN = D_IN // N_CHIPS
D_FF = 2816
C_GU = 2 * D_FF // N_CHIPS
HEADS = 4
HB = D // HEADS
EPS = 1e-6
LRU_C = 8.0
ADAM_LR, ADAM_B1, ADAM_B2, ADAM_EPS, ADAM_WD, ADAM_STEP = 0.001, 0.9, 0.999, 1e-08, 0.01, 10
VMEM_LIMIT = 56 << 20

(V_SH1, V_SC1, V_GT1, V_SH2, V_SC2, V_GT2, V_GMIX, V_GFFN, V_GFIN, V_CBB, V_BA, V_BX, V_LAM,
 V_WA0, V_WA1, V_WA2, V_WB0, V_WB1, V_WB2, V_WB3) = range(20)
N_VEC = 24
(G_SH1, G_SC1, G_GT1, G_SH2, G_SC2, G_GT2, G_GMIX, G_CBB, G_BA, G_BX, G_LAM, G_GFFN, G_GFIN,
 G_WA0, G_WA1, G_WA2, G_WB0, G_WB1, G_WB2, G_WB3, G_LOSS) = range(21)
N_SMALL = 24

_VMEM = pl.BlockSpec(memory_space=pltpu.VMEM)
_ANY = pl.BlockSpec(memory_space=pl.ANY)


def _cparams(n_grid=1):
    return pltpu.CompilerParams(dimension_semantics=("arbitrary",) * n_grid, vmem_limit_bytes=VMEM_LIMIT)


def _after(deps, body):
    n = len(deps)
    return lambda *refs: body(*refs[n:])


def _rms(x):
    rstd = lax.rsqrt(jnp.mean(x * x, axis=-1, keepdims=True) + EPS)
    return x * rstd, rstd


def _rms_bwd(dxhat, xhat, rstd):
    return rstd * (dxhat - xhat * jnp.mean(dxhat * xhat, axis=-1, keepdims=True))


def _rowsum(v):
    return jnp.sum(v, axis=0, keepdims=True)


def _dot(a, b):
    return jnp.dot(a, b, preferred_element_type=F32)


def _dot_nt(a, b):
    return lax.dot_general(a, b, (((1,), (1,)), ((), ())), preferred_element_type=F32)


def _dot_tn(a, b):
    return lax.dot_general(a, b, (((0,), (0,)), ((), ())), preferred_element_type=F32)


def _gelu(x):
    k, c = 0.7978845608028654, 0.044715
    t = jnp.tanh(k * (x + c * x * x * x))
    return 0.5 * x * (1.0 + t), 0.5 * (1.0 + t) + 0.5 * x * (1.0 - t * t) * k * (1.0 + 3.0 * c * x * x)


def _sigmoid(x):
    return pl.reciprocal(1.0 + jnp.exp(-x), approx=True)


def _log_sigmoid(lam):
    return jnp.minimum(lam, 0.0) - jnp.log1p(jnp.exp(-jnp.abs(lam)))


def _lru_gates(u, wa_ref, wx_ref, v_ref, row0):
    ub = u.astype(BF16)
    pre_a = jnp.concatenate([_dot(ub[:, h * HB:(h + 1) * HB], wa_ref[h]) for h in range(HEADS)], axis=1)
    pre_x = jnp.concatenate([_dot(ub[:, h * HB:(h + 1) * HB], wx_ref[h]) for h in range(HEADS)], axis=1)
    r = _sigmoid(pre_a + v_ref[V_BA:V_BA + 1, :])
    ig = _sigmoid(pre_x + v_ref[V_BX:V_BX + 1, :])
    log_a = LRU_C * r * _log_sigmoid(v_ref[V_LAM:V_LAM + 1, :])
    a = jnp.exp(log_a)
    x2 = 2.0 * log_a
    m2 = jnp.where(x2 > -0.03, -x2 * (1.0 + x2 * (0.5 + x2 * (1.0 / 6.0 + x2 * (1.0 / 24.0)))), 1.0 - a * a)
    mult = jnp.where(row0, 1.0, jnp.sqrt(jnp.maximum(m2, 0.0)))
    return r, ig, a, mult


TIME_BLOCKS = 8
N_KEPT = 10


def _late_blocks(v, buf, g, halo=None):
    n = buf.shape[0]
    out = []
    for idx in range(n):
        k = TIME_BLOCKS - n + idx
        buf[idx, 8:g + 8, :] = v[k * g:(k + 1) * g]
        if halo is not None:
            buf[idx, 7:8, :] = halo[idx]
        out.append(buf[idx, pl.ds(7, g), :])
        if halo is None:
            buf[idx, 7:8, :] = buf[idx, g + 7:g + 8, :]
    return out


def _earlier(v, s, late, g):
    return jnp.concatenate(late[len(late) - s:] + [v[0:(TIME_BLOCKS - s) * g]], axis=0)


def _early_blocks(v, buf, g):
    out = []
    for k in range(buf.shape[0]):
        buf[k, 0:g, :] = v[k * g:(k + 1) * g]
        out.append(buf[k, pl.ds(1, g), :])
        buf[k, g:g + 1, :] = buf[k, 0:1, :]
    return out


def _later(v, s, early, g):
    return jnp.concatenate([v[s * g:]] + early[0:s], axis=0)


def _fwd_in_first(x, vecs, w_in_g, q_idx, ts, deps=()):
    s = x.shape[0]

    def body(q_ref, x_ref, v_ref, w_ref, h1_ref, proj_ref):
        xhat, _ = _rms(x_ref[...])
        h = xhat * v_ref[V_GMIX:V_GMIX + 1, :] * (1.0 + v_ref[V_SC1:V_SC1 + 1, :]) + v_ref[V_SH1:V_SH1 + 1, :]
        hb = h.astype(BF16)
        h1_ref[...] = hb
        proj_ref[...] = _dot(hb, w_ref[...]).astype(BF16)

    return pl.pallas_call(
        lambda q_ref, *refs: body(q_ref, *refs[len(deps):]),
        grid_spec=pltpu.PrefetchScalarGridSpec(
            num_scalar_prefetch=1, grid=(s // ts,),
            in_specs=[_ANY] * len(deps) + [pl.BlockSpec((ts, D), lambda i, q: (i, 0)), _VMEM,
                                           pl.BlockSpec((None, D, C_IN), lambda i, q: (q[0], 0, 0))],
            out_specs=[pl.BlockSpec((ts, D), lambda i, q: (i, 0)), pl.BlockSpec((ts, C_IN), lambda i, q: (i, q[0]))]),
        out_shape=(jax.ShapeDtypeStruct((s, D), BF16), jax.ShapeDtypeStruct((s, D_IN), BF16)),
        compiler_params=_cparams(), name="fwd_in_own")(q_idx, *deps, x, vecs, w_in_g)


def _fwd_in_more(h1, w_in_g, proj, q_idx, ts, name, deps=()):
    s = h1.shape[0]

    def body(q_ref, h1_ref, w_ref, proj_in_ref, proj_ref):
        proj_ref[...] = _dot(h1_ref[...], w_ref[...]).astype(BF16)

    return pl.pallas_call(
        lambda q_ref, *refs: body(q_ref, *refs[len(deps):]),
        grid_spec=pltpu.PrefetchScalarGridSpec(
            num_scalar_prefetch=1, grid=(s // ts,),
            in_specs=[_ANY] * len(deps) + [pl.BlockSpec((ts, D), lambda i, q: (i, 0)),
                                           pl.BlockSpec((None, D, C_IN), lambda i, q: (q[0], 0, 0)), _ANY],
            out_specs=pl.BlockSpec((ts, C_IN), lambda i, q: (i, q[0]))),
        out_shape=jax.ShapeDtypeStruct((s, D_IN), BF16), input_output_aliases={len(deps) + 3: 0},
        compiler_params=_cparams(), name=name)(q_idx, *deps, h1, w_in_g, proj)


def _fwd_mix(proj, x, vecs, w_rga, w_rgx, w_out, ts, deps=()):
    s = x.shape[0]
    g = ts // TIME_BLOCKS

    def body(proj_ref, x_ref, v_ref, wa_ref, wx_ref, wo_ref, x1_ref, mg_ref, z1_ref, kept_ref, decay_ref,
             ua_buf, rx_buf, p_buf, q_buf, c_buf, hcarry):
        i = pl.program_id(0)

        @pl.when(i == 0)
        def _():
            ua_buf[...] = jnp.zeros(ua_buf.shape, F32)
            rx_buf[...] = jnp.zeros(rx_buf.shape, F32)
            hcarry[...] = jnp.zeros((8, D), F32)

        def seg(j):
            return proj_ref[:, j * D:(j + 1) * D].astype(F32)

        def vrow(j):
            return v_ref[j:j + 1, :]

        cb, cc, cx, rx, rg, ga, gb = (seg(j) for j in range(7))
        ua = cc * cx
        ua_late = _late_blocks(ua, ua_buf, g)
        rx_late = _late_blocks(rx, rx_buf, g)
        va = vrow(V_WA2) * ua + vrow(V_WA1) * _earlier(ua, 1, ua_late, g) + vrow(V_WA0) * _earlier(ua, 2, ua_late, g)
        u = (vrow(V_WB3) * rx + vrow(V_WB2) * _earlier(rx, 1, rx_late, g) + vrow(V_WB1) * _earlier(rx, 2, rx_late, g)
             + vrow(V_WB0) * _earlier(rx, 3, rx_late, g) + vrow(V_CBB))

        rows = lax.broadcasted_iota(jnp.int32, (ts, D), 0)
        row0 = jnp.logical_and(rows == 0, i == 0)
        r, ig, a, mult = _lru_gates(u, wa_ref, wx_ref, v_ref, row0)
        decay_ref[...] = a
        bx = mult * (ig * u)

        prods, sums = [a[0:g]], [bx[0:g]]
        for k in range(1, TIME_BLOCKS):
            ak = a[k * g:(k + 1) * g]
            sums.append(ak * sums[-1] + bx[k * g:(k + 1) * g])
            prods.append(ak * prods[-1])
        p_buf[...] = prods[-1]
        q_buf[...] = sums[-1]
        state = hcarry[0:1, :]
        for j in range(g):
            c_buf[j:j + 1, :] = state
            state = p_buf[j:j + 1, :] * state + q_buf[j:j + 1, :]
        hcarry[0:1, :] = state
        entering = c_buf[...]
        h = jnp.concatenate([sums[k] + prods[k] * entering for k in range(TIME_BLOCKS)], axis=0)

        gel, dgel = _gelu(rg)
        sga = _sigmoid(ga)
        sgb = _sigmoid(gb)
        for j, keep in enumerate((va, r, ig, sga, sgb, gel, dgel, mult, u, h)):
            kept_ref[:, j * D:(j + 1) * D] = keep.astype(BF16)
        merged = (sga * (cb * va) + sgb * (h * gel)).astype(BF16)
        mg_ref[...] = merged
        z1 = _dot(merged, wo_ref[...])
        z1_ref[...] = z1.astype(BF16)
        x1_ref[...] = x_ref[...] + vrow(V_GT1) * z1

    row = lambda i: (i, 0)
    return pl.pallas_call(
        _after(deps, body), grid=(s // ts,),
        out_shape=(jax.ShapeDtypeStruct((s, D), F32), jax.ShapeDtypeStruct((s, D), BF16), jax.ShapeDtypeStruct((s, D), BF16),
                   jax.ShapeDtypeStruct((s, N_KEPT * D), BF16), jax.ShapeDtypeStruct((s, D), F32)),
        in_specs=[_ANY] * len(deps) + [pl.BlockSpec((ts, D_IN), row), pl.BlockSpec((ts, D), row), _VMEM, _VMEM, _VMEM, _VMEM],
        out_specs=[pl.BlockSpec((ts, D), row)] * 3 + [pl.BlockSpec((ts, N_KEPT * D), row), pl.BlockSpec((ts, D), row)],
        scratch_shapes=[pltpu.VMEM((2, g + 8, D), F32), pltpu.VMEM((3, g + 8, D), F32), pltpu.VMEM((g, D), F32),
                        pltpu.VMEM((g, D), F32), pltpu.VMEM((g, D), F32), pltpu.VMEM((8, D), F32)],
        compiler_params=_cparams(), name="fwd_mix")(*deps, proj, x, vecs, w_rga, w_rgx, w_out)


def _ffn_loss(x1, target, vecs, w_gu_g, w_dn, ts):
    s = x1.shape[0]

    def body(x1_ref, t_ref, v_ref, wgu_ref, wdn_ref, dx1_ref, h2_ref, act_ref, dz2_ref, dgu_ref, sm_ref):
        @pl.when(pl.program_id(0) == 0)
        def _():
            sm_ref[...] = jnp.zeros((N_SMALL, D), F32)

        def vrow(j):
            return v_ref[j:j + 1, :]

        n_sub = 1
        rows = [slice(k * (ts // n_sub), (k + 1) * (ts // n_sub)) for k in range(n_sub)]
        subs = [dict(r=r, sums={}) for r in rows]

        def stage_norm(t):
            t["x1"] = x1_ref[t["r"], :]
            t["xh1"], t["rstd1"] = _rms(t["x1"])
            t["n2"] = t["xh1"] * vrow(V_GFFN)
            t["h2"] = (t["n2"] * (1.0 + vrow(V_SC2)) + vrow(V_SH2)).astype(BF16)
            h2_ref[t["r"], :] = t["h2"]

        def stage_up(t):
            h2 = t["h2"]
            g = jnp.concatenate([_dot(h2, wgu_ref[0]), _dot(h2, wgu_ref[1])], axis=1)
            t["up"] = jnp.concatenate([_dot(h2, wgu_ref[2]), _dot(h2, wgu_ref[3])], axis=1)
            t["g"] = g
            t["sg"] = _sigmoid(g)
            t["silu"] = g * t["sg"]
            t["act"] = (t["silu"] * t["up"]).astype(BF16)
            act_ref[t["r"], :] = t["act"]

        def stage_down_loss(t):
            z2 = _dot(t["act"], wdn_ref[...])
            x2 = t["x1"] + vrow(V_GT2) * z2
            xh2, rstd2 = _rms(x2)
            err = xh2 * vrow(V_GFIN) - t_ref[t["r"], :]
            t["sums"][G_LOSS] = _rowsum((0.5 / D) * err * err)
            dy = err * (1.0 / D)
            t["sums"][G_GFIN] = _rowsum(dy * xh2)
            t["dx2"] = _rms_bwd(dy * vrow(V_GFIN), xh2, rstd2)
            t["sums"][G_GT2] = _rowsum(t["dx2"] * z2)
            t["dz2"] = (vrow(V_GT2) * t["dx2"]).astype(BF16)
            dz2_ref[t["r"], :] = t["dz2"]

        def stage_back_act(t):
            dact = _dot_nt(t["dz2"], wdn_ref[...])
            g, sg = t["g"], t["sg"]
            t["dgate"] = (dact * t["up"] * (sg * (1.0 + g * (1.0 - sg)))).astype(BF16)
            t["dup"] = (dact * t["silu"]).astype(BF16)
            dgu_ref[t["r"], 0:D_FF] = t["dgate"]
            dgu_ref[t["r"], D_FF:2 * D_FF] = t["dup"]

        def stage_back_norm(t):
            dgate, dup = t["dgate"], t["dup"]
            dh2 = (_dot_nt(dgate[:, 0:C_GU], wgu_ref[0]) + _dot_nt(dgate[:, C_GU:2 * C_GU], wgu_ref[1])
                   + _dot_nt(dup[:, 0:C_GU], wgu_ref[2]) + _dot_nt(dup[:, C_GU:2 * C_GU], wgu_ref[3]))
            t["sums"][G_SH2] = _rowsum(dh2)
            t["sums"][G_SC2] = _rowsum(dh2 * t["n2"])
            dn2 = dh2 * (1.0 + vrow(V_SC2))
            t["sums"][G_GFFN] = _rowsum(dn2 * t["xh1"])
            dx1_ref[t["r"], :] = t["dx2"] + _rms_bwd(dn2 * vrow(V_GFFN), t["xh1"], t["rstd1"])

        for stage in (stage_norm, stage_up, stage_down_loss, stage_back_act, stage_back_norm):
            for t in subs:
                stage(t)
        for j in subs[0]["sums"]:
            total = subs[0]["sums"][j]
            for t in subs[1:]:
                total = total + t["sums"][j]
            sm_ref[j:j + 1, :] += total

    row = lambda i: (i, 0)
    return pl.pallas_call(
        body, grid=(s // ts,),
        out_shape=(jax.ShapeDtypeStruct((s, D), F32), jax.ShapeDtypeStruct((s, D), BF16), jax.ShapeDtypeStruct((s, D_FF), BF16),
                   jax.ShapeDtypeStruct((s, D), BF16), jax.ShapeDtypeStruct((s, 2 * D_FF), BF16),
                   jax.ShapeDtypeStruct((N_SMALL, D), F32)),
        in_specs=[pl.BlockSpec((ts, D), row), pl.BlockSpec((ts, D), row), _VMEM, _VMEM, _VMEM],
        out_specs=[pl.BlockSpec((ts, D), row), pl.BlockSpec((ts, D), row), pl.BlockSpec((ts, D_FF), row),
                   pl.BlockSpec((ts, D), row), pl.BlockSpec((ts, 2 * D_FF), row), pl.BlockSpec((N_SMALL, D), lambda i: (0, 0))],
        compiler_params=_cparams(), name="ffn_loss")(x1, target, vecs, w_gu_g, w_dn)


def _bwd_mix(dx1, z1, merged, proj, kept, decay, vecs, w_rga, w_rgx, w_out, small, ts, deps=()):
    s = dx1.shape[0]
    nt = s // ts
    g = ts // TIME_BLOCKS
    assert g % 16 == 0

    def body(dx1_ref, z1_ref, mg_ref, proj_ref, kept_ref, decay_ref, hh_ref, v_ref, wa_ref, wx_ref,
             wo_ref, sm0_ref, dproj_ref, sm_ref, dwa_ref, dwx_ref, dwo_ref,
             h_buf, a_buf, dva_buf, du_buf, p_buf, q_buf, c_buf, lcarry):
        i = pl.program_id(0)
        first_tile = i == nt - 1

        @pl.when(i == 0)
        def _():
            a_buf[...] = jnp.zeros(a_buf.shape, F32)
            dva_buf[...] = jnp.zeros(dva_buf.shape, F32)
            du_buf[...] = jnp.zeros(du_buf.shape, F32)
            lcarry[...] = jnp.zeros((8, D), F32)
            sm_ref[...] = sm0_ref[...]
            dwa_ref[...] = jnp.zeros((HEADS, HB, HB), F32)
            dwx_ref[...] = jnp.zeros((HEADS, HB, HB), F32)
            dwo_ref[...] = jnp.zeros((D, D), F32)

        def seg(j):
            return proj_ref[:, j * D:(j + 1) * D].astype(F32)

        def vrow(j):
            return v_ref[j:j + 1, :]

        def acc(j, val):
            sm_ref[j:j + 1, :] += _rowsum(val)

        cb, cc, cx, rx = (seg(j) for j in range(4))
        ua = cc * cx
        va, r, ig, sga, sgb, gel, dgel, mult, u, h = (kept_ref[:, j * D:(j + 1) * D].astype(F32) for j in range(N_KEPT))
        a = decay_ref[...]
        rows = lax.broadcasted_iota(jnp.int32, (ts, D), 0)
        row0 = jnp.logical_and(rows == 0, first_tile)

        dx1 = dx1_ref[...]
        acc(G_GT1, dx1 * z1_ref[...].astype(F32))
        dz1 = (vrow(V_GT1) * dx1).astype(BF16)
        dwo_ref[...] += _dot_tn(mg_ref[...], dz1)
        dmg = _dot_nt(dz1, wo_ref[...])
        dya = dmg * sga
        dyb = dmg * sgb
        dproj_ref[:, 5 * D:6 * D] = (dya * (cb * va) * (1.0 - sga)).astype(BF16)
        dproj_ref[:, 6 * D:7 * D] = (dyb * (h * gel) * (1.0 - sgb)).astype(BF16)

        dproj_ref[:, 0:D] = (dya * va).astype(BF16)
        dva = dya * cb
        dva_early = _early_blocks(dva, dva_buf, g)
        dva1 = _later(dva, 1, dva_early, g)
        dva2 = _later(dva, 2, dva_early, g)
        dua = vrow(V_WA2) * dva + vrow(V_WA1) * dva1 + vrow(V_WA0) * dva2
        acc(G_WA2, ua * dva)
        acc(G_WA1, ua * dva1)
        acc(G_WA0, ua * dva2)
        dproj_ref[:, D:2 * D] = (dua * cx).astype(BF16)
        dproj_ref[:, 2 * D:3 * D] = (dua * cc).astype(BF16)

        dproj_ref[:, 4 * D:5 * D] = (dyb * h * dgel).astype(BF16)
        a_next = _later(a, 1, _early_blocks(a, a_buf, g), g)
        dh = dyb * gel
        last = TIME_BLOCKS - 1
        prods, sums = {last: a_next[last * g:]}, {last: dh[last * g:]}
        for k in range(last - 1, -1, -1):
            ak = a_next[k * g:(k + 1) * g]
            sums[k] = dh[k * g:(k + 1) * g] + ak * sums[k + 1]
            prods[k] = ak * prods[k + 1]
        p_buf[...] = prods[0]
        q_buf[...] = sums[0]
        state = lcarry[0:1, :]
        for j in range(g - 1, -1, -1):
            c_buf[j:j + 1, :] = state
            state = q_buf[j:j + 1, :] + p_buf[j:j + 1, :] * state
        lcarry[0:1, :] = state
        entering = c_buf[...]
        lam = jnp.concatenate([sums[k] + prods[k] * entering for k in range(TIME_BLOCKS)], axis=0)

        last = lax.broadcasted_iota(jnp.int32, hh_ref.shape, 0) == hh_ref.shape[0] - 1
        h_halo = [jnp.where(first_tile, 0.0, jnp.sum(jnp.where(last, hh_ref[...].astype(F32), 0.0), axis=0, keepdims=True))]
        da = lam * _earlier(h, 1, _late_blocks(h, h_buf, g, h_halo), g)
        dmult = jnp.where(row0, 0.0, lam * (ig * u))
        di = lam * mult * u
        du = lam * mult * ig
        dlog_a = da * a - dmult * (a * a) / mult
        lam_p = vrow(V_LAM)
        dr = dlog_a * (LRU_C * _log_sigmoid(lam_p))
        sm_ref[G_LAM:G_LAM + 1, :] += _rowsum(dlog_a * r) * (LRU_C * jax.nn.sigmoid(-lam_p))
        dpa = dr * r * (1.0 - r)
        dpx = di * ig * (1.0 - ig)
        acc(G_BA, dpa)
        acc(G_BX, dpx)
        dpab = dpa.astype(BF16)
        dpxb = dpx.astype(BF16)
        ub = u.astype(BF16)
        back = []
        for hd in range(HEADS):
            cols = slice(hd * HB, (hd + 1) * HB)
            back.append(_dot_nt(dpab[:, cols], wa_ref[hd]) + _dot_nt(dpxb[:, cols], wx_ref[hd]))
            dwa_ref[hd] += _dot_tn(ub[:, cols], dpab[:, cols])
            dwx_ref[hd] += _dot_tn(ub[:, cols], dpxb[:, cols])
        du = du + jnp.concatenate(back, axis=1)

        acc(G_CBB, du)
        du_early = _early_blocks(du, du_buf, g)
        du1 = _later(du, 1, du_early, g)
        du2 = _later(du, 2, du_early, g)
        du3 = _later(du, 3, du_early, g)
        dproj_ref[:, 3 * D:4 * D] = (vrow(V_WB3) * du + vrow(V_WB2) * du1 + vrow(V_WB1) * du2 + vrow(V_WB0) * du3).astype(BF16)
        acc(G_WB3, rx * du)
        acc(G_WB2, rx * du1)
        acc(G_WB1, rx * du2)
        acc(G_WB0, rx * du3)

    rev = lambda i: (nt - 1 - i, 0)
    h_halo16 = lambda i: (jnp.maximum((nt - 1 - i) * (ts // 16) - 1, 0), N_KEPT - 1)
    const2 = lambda i: (0, 0)
    const3 = lambda i: (0, 0, 0)
    return pl.pallas_call(
        _after(deps, body), grid=(nt,),
        out_shape=(jax.ShapeDtypeStruct((s, D_IN), BF16), jax.ShapeDtypeStruct((N_SMALL, D), F32),
                   jax.ShapeDtypeStruct((HEADS, HB, HB), F32), jax.ShapeDtypeStruct((HEADS, HB, HB), F32),
                   jax.ShapeDtypeStruct((D, D), F32)),
        in_specs=[_ANY] * len(deps) + [pl.BlockSpec((ts, D), rev), pl.BlockSpec((ts, D), rev), pl.BlockSpec((ts, D), rev),
                  pl.BlockSpec((ts, 4 * D), rev), pl.BlockSpec((ts, N_KEPT * D), rev), pl.BlockSpec((ts, D), rev),
                  pl.BlockSpec((16, D), h_halo16), _VMEM, _VMEM, _VMEM, _VMEM, _VMEM],
        out_specs=[pl.BlockSpec((ts, D_IN), rev), pl.BlockSpec((N_SMALL, D), const2),
                   pl.BlockSpec((HEADS, HB, HB), const3), pl.BlockSpec((HEADS, HB, HB), const3), pl.BlockSpec((D, D), const2)],
        scratch_shapes=[pltpu.VMEM((1, g + 8, D), F32), pltpu.VMEM((1, g + 8, D), F32),
                        pltpu.VMEM((2, g + 8, D), F32), pltpu.VMEM((3, g + 8, D), F32), pltpu.VMEM((g, D), F32),
                        pltpu.VMEM((g, D), F32), pltpu.VMEM((g, D), F32), pltpu.VMEM((8, D), F32)],
        compiler_params=_cparams(), name="bwd_mix")(*deps, dx1, z1, merged, proj, kept, decay, kept, vecs, w_rga,
                                                    w_rgx, w_out, small)


def _bwd_in(dproj, x, dx1, vecs, w_in_g, small, ts, deps=()):
    s = x.shape[0]

    def body(dp_ref, x_ref, dx1_ref, v_ref, w_ref, sm0_ref, gx_ref, sm_ref):
        @pl.when(pl.program_id(0) == 0)
        def _():
            sm_ref[...] = sm0_ref[...]

        def vrow(j):
            return v_ref[j:j + 1, :]

        dh1 = _dot_nt(dp_ref[:, 0:C_IN], w_ref[0])
        for k in range(1, N_CHIPS):
            dh1 += _dot_nt(dp_ref[:, k * C_IN:(k + 1) * C_IN], w_ref[k])
        xh, rstd = _rms(x_ref[...])
        sm_ref[G_SH1:G_SH1 + 1, :] += _rowsum(dh1)
        sm_ref[G_SC1:G_SC1 + 1, :] += _rowsum(dh1 * (xh * vrow(V_GMIX)))
        dn1 = dh1 * (1.0 + vrow(V_SC1))
        sm_ref[G_GMIX:G_GMIX + 1, :] += _rowsum(dn1 * xh)
        gx_ref[...] = dx1_ref[...] + _rms_bwd(dn1 * vrow(V_GMIX), xh, rstd)

    row = lambda i: (i, 0)
    return pl.pallas_call(
        _after(deps, body), grid=(s // ts,),
        out_shape=(jax.ShapeDtypeStruct((s, D), F32), jax.ShapeDtypeStruct((N_SMALL, D), F32)),
        in_specs=[_ANY] * len(deps) + [pl.BlockSpec((ts, D_IN), row), pl.BlockSpec((ts, D), row), pl.BlockSpec((ts, D), row),
                                       _VMEM, _VMEM, _VMEM],
        out_specs=[pl.BlockSpec((ts, D), row), pl.BlockSpec((N_SMALL, D), lambda i: (0, 0))],
        compiler_params=_cparams(), name="bwd_in")(*deps, dproj, x, dx1, vecs, w_in_g, small)


def _grad_w(a, b, n_col_blocks, ts, name, deps=()):
    s, m = a.shape
    tn = b.shape[1] // n_col_blocks
    n_steps = s // ts

    def body(a_ref, b_ref, o_ref, acc_ref):
        k = pl.program_id(1)

        @pl.when(k == 0)
        def _():
            acc_ref[...] = jnp.zeros((m, tn), F32)

        acc_ref[...] += _dot_tn(a_ref[...], b_ref[...])

        @pl.when(k == n_steps - 1)
        def _():
            o_ref[...] = acc_ref[...].astype(BF16)

    return pl.pallas_call(
        _after(deps, body), grid=(n_col_blocks, n_steps),
        out_shape=jax.ShapeDtypeStruct((n_col_blocks, m, tn), BF16),
        in_specs=[_ANY] * len(deps) + [pl.BlockSpec((ts, m), lambda n, k: (k, 0)), pl.BlockSpec((ts, tn), lambda n, k: (k, n))],
        out_specs=pl.BlockSpec((None, m, tn), lambda n, k: (n, 0, 0)),
        scratch_shapes=[pltpu.VMEM((m, tn), F32)],
        compiler_params=_cparams(2), name=name)(*deps, a, b)


def _ada_fwd(c_all, w_ada, b_ada):
    n = w_ada.shape[1]

    def body(c_ref, w_ref, b_ref, o_ref, ca_ref):
        c = c_ref[...]
        ca = c * jax.nn.sigmoid(c)
        ca_ref[...] = ca
        o_ref[...] = jnp.dot(ca, w_ref[...], preferred_element_type=F32, precision=lax.Precision.HIGHEST) + b_ref[...]

    return pl.pallas_call(
        body, out_shape=(jax.ShapeDtypeStruct((N_DEV, n), F32), jax.ShapeDtypeStruct((N_DEV, D), F32)),
        in_specs=[_VMEM] * 3, out_specs=[_VMEM] * 2, compiler_params=_cparams(0), name="ada_fwd")(c_all, w_ada, b_ada)


def _ada_bwd(c_act, dmod):
    n = dmod.shape[1]

    def body(c_ref, d_ref, o_ref):
        o_ref[...] = lax.dot_general(c_ref[...], d_ref[...], (((0,), (0,)), ((), ())), preferred_element_type=F32,
                                     precision=lax.Precision.HIGHEST)

    return pl.pallas_call(
        body, out_shape=jax.ShapeDtypeStruct((D, n), F32), in_specs=[_VMEM] * 2, out_specs=_VMEM,
        compiler_params=_cparams(0), name="ada_bwd")(c_act, dmod)


def _sum_small(parts):
    def body(p_ref, o_ref, d_ref):
        tot = p_ref[0]
        for dev in range(1, N_DEV):
            tot = tot + p_ref[dev]
        o_ref[...] = tot
        d_ref[...] = p_ref[:, 0:8, :]

    return pl.pallas_call(
        body, out_shape=(jax.ShapeDtypeStruct((N_SMALL, D), F32), jax.ShapeDtypeStruct((N_DEV, 8, D), F32)),
        in_specs=[_VMEM], out_specs=[_VMEM] * 2, compiler_params=_cparams(0), name="sum_small")(parts)


def _adamw(w, g, m, v, name, deps=()):
    rows, cols = w.shape
    tr = 128 if rows % 128 == 0 else (64 if rows % 64 == 0 else rows)

    def body(w_ref, g_ref, m_ref, v_ref, d_ref, nm_ref, nv_ref):
        g_ = g_ref[...]
        m_ = ADAM_B1 * m_ref[...] + (1.0 - ADAM_B1) * g_
        v_ = ADAM_B2 * v_ref[...] + (1.0 - ADAM_B2) * (g_ * g_)
        nm_ref[...] = m_
        nv_ref[...] = v_
        m_hat = m_ / (1.0 - ADAM_B1 ** ADAM_STEP)
        v_hat = v_ / (1.0 - ADAM_B2 ** ADAM_STEP)
        d_ref[...] = -ADAM_LR * (m_hat / (jnp.sqrt(v_hat) + ADAM_EPS) + ADAM_WD * w_ref[...])

    spec = pl.BlockSpec((tr, cols), lambda i: (i, 0))
    return pl.pallas_call(
        _after(deps, body), grid=(rows // tr,), out_shape=(jax.ShapeDtypeStruct((rows, cols), F32),) * 3,
        in_specs=[_ANY] * len(deps) + [spec] * 4, out_specs=[spec] * 3, compiler_params=_cparams(), name=name)(*deps, w, g, m, v)


def _adamw_small(items, name):
    n = len(items)

    def body(*refs):
        ins, outs = refs[:4 * n], refs[4 * n:]
        for k in range(n):
            w_ref, g_ref, m_ref, v_ref = ins[4 * k:4 * k + 4]
            d_ref, nm_ref, nv_ref = outs[3 * k:3 * k + 3]
            g_ = g_ref[...]
            m_ = ADAM_B1 * m_ref[...] + (1.0 - ADAM_B1) * g_
            v_ = ADAM_B2 * v_ref[...] + (1.0 - ADAM_B2) * (g_ * g_)
            nm_ref[...] = m_
            nv_ref[...] = v_
            m_hat = m_ / (1.0 - ADAM_B1 ** ADAM_STEP)
            v_hat = v_ / (1.0 - ADAM_B2 ** ADAM_STEP)
            d_ref[...] = -ADAM_LR * (m_hat / (jnp.sqrt(v_hat) + ADAM_EPS) + ADAM_WD * w_ref[...])

    out = pl.pallas_call(
        body, out_shape=tuple(jax.ShapeDtypeStruct(it[0].shape, F32) for it in items for _ in range(3)),
        in_specs=[_VMEM] * (4 * n), out_specs=[_VMEM] * (3 * n), name=name)(*[a for it in items for a in it])
    return [tuple(out[3 * k:3 * k + 3]) for k in range(n)]


def _adamw_halves(w, mine, other, m, v, c_idx, name, deps=()):
    r2, cols = mine.shape
    tr = next(t for t in (128, 64, 32, 16, 8) if r2 % t == 0)
    nh = r2 // tr

    def body(c_ref, w_ref, mine_ref, other_ref, m_ref, v_ref, g_ref, d_ref, nm_ref, nv_ref):
        g_ = jnp.where(pl.program_id(0) // nh == c_ref[0], mine_ref[...], other_ref[...])
        g_ref[...] = g_
        m_ = ADAM_B1 * m_ref[...] + (1.0 - ADAM_B1) * g_
        v_ = ADAM_B2 * v_ref[...] + (1.0 - ADAM_B2) * (g_ * g_)
        nm_ref[...] = m_
        nv_ref[...] = v_
        m_hat = m_ / (1.0 - ADAM_B1 ** ADAM_STEP)
        v_hat = v_ / (1.0 - ADAM_B2 ** ADAM_STEP)
        d_ref[...] = -ADAM_LR * (m_hat / (jnp.sqrt(v_hat) + ADAM_EPS) + ADAM_WD * w_ref[...])

    full = pl.BlockSpec((tr, cols), lambda i, c: (i, 0))
    mine_spec = pl.BlockSpec((tr, cols), lambda i, c: (jnp.clip(i - c[0] * nh, 0, nh - 1), 0))
    other_spec = pl.BlockSpec((tr, cols), lambda i, c: (jnp.clip(i - (1 - c[0]) * nh, 0, nh - 1), 0))
    return pl.pallas_call(
        lambda c_ref, *refs: body(c_ref, *refs[len(deps):]),
        grid_spec=pltpu.PrefetchScalarGridSpec(
            num_scalar_prefetch=1, grid=(2 * nh,),
            in_specs=[_ANY] * len(deps) + [full, mine_spec, other_spec, full, full], out_specs=[full] * 4),
        out_shape=(jax.ShapeDtypeStruct((2 * r2, cols), F32),) * 4, compiler_params=_cparams(), name=name,
    )(c_idx, *deps, w, mine, other, m, v)


def _add_halves(g, recv, c_idx, name):
    n, _, r2, cols = g.shape

    def body(c_ref, g_ref, r_ref, o_ref):
        o_ref[...] = (g_ref[...].astype(F32) + r_ref[...].astype(F32)).astype(BF16)

    return pl.pallas_call(
        body,
        grid_spec=pltpu.PrefetchScalarGridSpec(
            num_scalar_prefetch=1, grid=(n,),
            in_specs=[pl.BlockSpec((None, None, r2, cols), lambda k, c: (k, c[0], 0, 0)),
                      pl.BlockSpec((None, r2, cols), lambda k, c: (k, 0, 0))],
            out_specs=pl.BlockSpec((None, r2, cols), lambda k, c: (k, 0, 0))),
        out_shape=jax.ShapeDtypeStruct((n, r2, cols), BF16), compiler_params=_cparams(), name=name)(c_idx, g, recv)


def _sum_chips(parts, name):
    n, r2, cols = parts.shape
    tr = next(t for t in (128, 176, 64, 32, 16) if r2 % t == 0)

    def body(p_ref, o_ref):
        o_ref[...] = ((p_ref[0].astype(F32) + p_ref[1].astype(F32)) + p_ref[2].astype(F32)) + p_ref[3].astype(F32)

    return pl.pallas_call(
        body, grid=(r2 // tr,), out_shape=jax.ShapeDtypeStruct((r2, cols), F32),
        in_specs=[pl.BlockSpec((n, tr, cols), lambda i: (0, i, 0))], out_specs=pl.BlockSpec((tr, cols), lambda i: (i, 0)),
        compiler_params=_cparams(), name=name)(parts)


def _place():
    x, y, c = lax.axis_index("x"), lax.axis_index("y"), lax.axis_index("c")
    return x, y, c, 2 * x + y


def _flip(v, bit):
    return 1 - v if bit else v


def _allgather8(v, name, deps=()):
    r, n = v.shape

    def body(*refs):
        v_ref, out_ref, send_sems, recv_sems, local_sem = refs[len(deps):]
        x, y, c, _ = _place()
        me = 4 * x + 2 * y + c
        mine = pltpu.make_async_copy(v_ref, out_ref.at[me], local_sem)
        mine.start()
        sends = []
        for rel in range(1, N_DEV):
            peer = (_flip(x, rel & 4), _flip(y, rel & 2), _flip(c, rel & 1))
            cp = pltpu.make_async_remote_copy(v_ref, out_ref.at[me], send_sems.at[rel - 1], recv_sems.at[rel - 1],
                                              device_id=peer, device_id_type=MESH)
            cp.start()
            sends.append(cp)
        for rel in range(1, N_DEV):
            peer = (_flip(x, rel & 4), _flip(y, rel & 2), _flip(c, rel & 1))
            peer_idx = 4 * peer[0] + 2 * peer[1] + peer[2]
            pltpu.make_async_remote_copy(v_ref, out_ref.at[peer_idx], send_sems.at[rel - 1], recv_sems.at[rel - 1],
                                         device_id=peer, device_id_type=MESH).wait_recv()
        for cp in sends:
            cp.wait_send()
        mine.wait()

    return pl.pallas_call(
        body, out_shape=jax.ShapeDtypeStruct((N_DEV, r, n), F32), in_specs=[_ANY] * len(deps) + [_VMEM], out_specs=_VMEM,
        scratch_shapes=[pltpu.SemaphoreType.DMA((N_DEV - 1,)), pltpu.SemaphoreType.DMA((N_DEV - 1,)), pltpu.SemaphoreType.DMA(())],
        name=name)(*deps, v)


def _gather_weights(shards):
    nw = len(shards)

    def body(*refs):
        w_refs, out_refs = refs[:nw], refs[nw:2 * nw]
        send_sems, recv_sems = refs[2 * nw:]
        x, y, c, p = _place()
        sibling = (x, y, 1 - c)
        sends = []
        for j in range(1, N_CHIPS):
            peer = (_flip(x, j & 2), _flip(y, j & 1), c)
            for w in range(nw):
                cp = pltpu.make_async_remote_copy(w_refs[w].at[c], out_refs[w].at[p, c], send_sems.at[w * 6 + j - 1],
                                                  recv_sems.at[w * 6 + j - 1], device_id=peer, device_id_type=MESH)
                cp.start()
                sends.append(cp)
        for j in range(1, N_CHIPS):
            peer = (_flip(x, j & 2), _flip(y, j & 1), c)
            q = 2 * peer[0] + peer[1]
            for w in range(nw):
                pltpu.make_async_remote_copy(w_refs[w].at[c], out_refs[w].at[q, c], send_sems.at[w * 6 + j - 1],
                                             recv_sems.at[w * 6 + j - 1], device_id=peer, device_id_type=MESH).wait_recv()
                cp = pltpu.make_async_remote_copy(out_refs[w].at[q, c], out_refs[w].at[q, c], send_sems.at[w * 6 + 2 + j],
                                                  recv_sems.at[w * 6 + 2 + j], device_id=sibling, device_id_type=MESH)
                cp.start()
                sends.append(cp)
        for j in range(1, N_CHIPS):
            q = 2 * _flip(x, j & 2) + _flip(y, j & 1)
            for w in range(nw):
                pltpu.make_async_remote_copy(out_refs[w].at[q, 1 - c], out_refs[w].at[q, 1 - c], send_sems.at[w * 6 + 2 + j],
                                             recv_sems.at[w * 6 + 2 + j], device_id=sibling, device_id_type=MESH).wait_recv()
        for cp in sends:
            cp.wait_send()

    return pl.pallas_call(
        body, out_shape=tuple(jax.ShapeDtypeStruct((N_CHIPS,) + s.shape, s.dtype) for s in shards),
        in_specs=[_ANY] * nw, out_specs=[_ANY] * nw,
        scratch_shapes=[pltpu.SemaphoreType.DMA((6 * nw,)), pltpu.SemaphoreType.DMA((6 * nw,))],
        name="gather_weights")(*shards)


def _swap_halves(grads):
    nw = len(grads)

    def body(*refs):
        g_refs, out_refs = refs[:nw], refs[nw:2 * nw]
        send_sems, recv_sems = refs[2 * nw:]
        x, y, c, _ = _place()
        sibling = (x, y, 1 - c)
        sends = []
        for w in range(nw):
            for k in range(N_CHIPS):
                cp = pltpu.make_async_remote_copy(g_refs[w].at[k, 1 - c], out_refs[w].at[k], send_sems.at[w * N_CHIPS + k],
                                                  recv_sems.at[w * N_CHIPS + k], device_id=sibling, device_id_type=MESH)
                cp.start()
                sends.append(cp)
        for cp in sends:
            cp.wait_recv()
        for cp in sends:
            cp.wait_send()

    return pl.pallas_call(
        body, out_shape=tuple(jax.ShapeDtypeStruct((N_CHIPS,) + g.shape[2:], g.dtype) for g in grads),
        in_specs=[_ANY] * nw, out_specs=[_ANY] * nw,
        scratch_shapes=[pltpu.SemaphoreType.DMA((N_CHIPS * nw,)), pltpu.SemaphoreType.DMA((N_CHIPS * nw,))],
        name="swap_halves")(*grads)


def _scatter_chips(parts):
    nw = len(parts)

    def body(*refs):
        p_refs, out_refs = refs[:nw], refs[nw:2 * nw]
        send_sems, recv_sems = refs[2 * nw:]
        x, y, c, p = _place()
        sends = []
        for j in range(1, N_CHIPS):
            peer = (_flip(x, j & 2), _flip(y, j & 1), c)
            q = 2 * peer[0] + peer[1]
            for w in range(nw):
                cp = pltpu.make_async_remote_copy(p_refs[w].at[q], out_refs[w].at[p], send_sems.at[w * 3 + j - 1],
                                                  recv_sems.at[w * 3 + j - 1], device_id=peer, device_id_type=MESH)
                cp.start()
                sends.append(cp)
        for j in range(1, N_CHIPS):
            peer = (_flip(x, j & 2), _flip(y, j & 1), c)
            q = 2 * peer[0] + peer[1]
            for w in range(nw):
                pltpu.make_async_remote_copy(p_refs[w].at[q], out_refs[w].at[q], send_sems.at[w * 3 + j - 1],
                                             recv_sems.at[w * 3 + j - 1], device_id=peer, device_id_type=MESH).wait_recv()
        for cp in sends:
            cp.wait_send()

    return pl.pallas_call(
        body, out_shape=tuple(jax.ShapeDtypeStruct(s.shape, s.dtype) for s in parts),
        in_specs=[_ANY] * nw, out_specs=[_ANY] * nw,
        scratch_shapes=[pltpu.SemaphoreType.DMA((3 * nw,)), pltpu.SemaphoreType.DMA((3 * nw,))],
        name="scatter_chips")(*parts)


def _share_halves(halves):
    nw = len(halves)

    def body(*refs):
        h_refs, out_refs = refs[:nw], refs[nw:2 * nw]
        send_sems, recv_sems = refs[2 * nw:]
        x, y, c, _ = _place()
        sends = []
        for w in range(nw):
            cp = pltpu.make_async_remote_copy(h_refs[w], out_refs[w], send_sems.at[w], recv_sems.at[w],
                                              device_id=(x, y, 1 - c), device_id_type=MESH)
            cp.start()
            sends.append(cp)
        for cp in sends:
            cp.wait_recv()
        for cp in sends:
            cp.wait_send()

    return pl.pallas_call(
        body, out_shape=tuple(jax.ShapeDtypeStruct(s.shape, s.dtype) for s in halves),
        in_specs=[_ANY] * nw, out_specs=[_ANY] * nw,
        scratch_shapes=[pltpu.SemaphoreType.DMA((nw,)), pltpu.SemaphoreType.DMA((nw,))],
        name="share_halves")(*halves)


_HBM = pl.BlockSpec(memory_space=pltpu.HBM)
_SEM = pl.BlockSpec(memory_space=pltpu.SEMAPHORE)
_EFFECT = pltpu.SideEffectType.DATAFLOW_SIDE_EFFECTING


def _xchg_start(name, plan, n_copies, srcs, lands, after=()):
    bufs = list(srcs) + list(lands)
    ns, nb = len(srcs), len(srcs) + len(lands)

    def body(*refs):
        send_sems, recv_sems, token = refs[nb + len(after)], refs[nb + len(after) + 1], refs[-1]
        for i, (src, dst, peer, _) in enumerate(plan(_place(), refs[:ns], refs[ns:nb])):
            pltpu.make_async_remote_copy(src, dst, send_sems.at[i], recv_sems.at[i], device_id=peer, device_id_type=MESH).start()
        token[...] = jnp.zeros_like(token)

    out = pl.pallas_call(
        body, name=name,
        out_shape=(pltpu.SemaphoreType.DMA((n_copies,)), pltpu.SemaphoreType.DMA((n_copies,)),
                   *[pltpu.HBM(a.shape, a.dtype) for a in bufs], jax.ShapeDtypeStruct((8, 128), F32)),
        in_specs=[_HBM] * nb + [_ANY] * len(after), out_specs=(_SEM, _SEM, *[_HBM] * nb, _VMEM),
        input_output_aliases={i: 2 + i for i in range(nb)},
        compiler_params=pltpu.CompilerParams(has_side_effects=_EFFECT),
    )(*[pltpu.with_memory_space_constraint(a, pltpu.HBM) for a in bufs], *after)
    return (out[0], out[1]), out[2:2 + ns], out[2 + ns:2 + nb], out[-1]


def _xchg_wait(name, plan, sems, srcs, lands, after, sem_ids=None):
    bufs = list(srcs) + list(lands)
    ns, nb = len(srcs), len(srcs) + len(lands)

    def body(*refs):
        send_sems, recv_sems = refs[nb], refs[nb + 1]
        copies = plan(_place(), refs[:ns], refs[ns:nb])
        ids = range(len(copies)) if sem_ids is None else sem_ids
        for i, (src, _, peer, mine) in zip(ids, copies, strict=True):
            if i is not None:
                cp = pltpu.make_async_remote_copy(src, mine, send_sems.at[i], recv_sems.at[i], device_id=peer,
                                                  device_id_type=MESH)
                cp.wait_send()
                cp.wait_recv()

    out = pl.pallas_call(
        body, name=name, out_shape=tuple(pltpu.HBM(a.shape, a.dtype) for a in bufs),
        in_specs=[_HBM] * nb + [_SEM, _SEM] + [_ANY] * len(after), out_specs=tuple([_HBM] * nb),
        input_output_aliases={i: i for i in range(nb)},
        compiler_params=pltpu.CompilerParams(has_side_effects=_EFFECT),
    )(*bufs, *sems, *after)
    return out[:ns], out[ns:]


def _other_chips(place, which=(1, 2, 3)):
    x, y, c, _ = place
    return [((_flip(x, j & 2), _flip(y, j & 1), c), 2 * _flip(x, j & 2) + _flip(y, j & 1)) for j in which]


def _plan_gather_ici(chips):
    def plan(place, src_refs, land_refs):
        _, _, c, p = place
        return [(s.at[c], l.at[p, c], peer, l.at[q, c]) for s, l, which in zip(src_refs, land_refs, chips, strict=True)
                for peer, q in _other_chips(place, which)]
    return plan


def _plan_relay(which):
    def plan(place, src_refs, land_refs):
        x, y, c, _ = place
        return [(l.at[q, c], l.at[q, c], (x, y, 1 - c), l.at[q, 1 - c]) for l in land_refs for _, q in _other_chips(place, which)]
    return plan


def _plan_swap(place, src_refs, land_refs):
    x, y, c, _ = place
    return [(s.at[k, 1 - c], l.at[k], (x, y, 1 - c), l.at[k]) for s, l in zip(src_refs, land_refs) for k in range(N_CHIPS)]


def _plan_scatter(place, src_refs, land_refs):
    _, _, _, p = place
    return [(s.at[q], l.at[p], peer, l.at[q]) for s, l in zip(src_refs, land_refs) for peer, q in _other_chips(place)]


def _plan_share(place, src_refs, land_refs):
    x, y, c, _ = place
    return [(s, l, (x, y, 1 - c), l) for s, l in zip(src_refs, land_refs)]


def _pack_rows(parts, n_rows, name, deps=()):
    def body(*refs):
        refs = refs[len(deps):]
        out_ref = refs[-1]
        out_ref[...] = jnp.zeros((n_rows, D), F32)
        at = 0
        for ref in refs[:-1]:
            k = ref.shape[0]
            out_ref[at:at + k, :] = ref[...]
            at += k

    return pl.pallas_call(
        body, out_shape=jax.ShapeDtypeStruct((n_rows, D), F32), in_specs=[_ANY] * len(deps) + [_VMEM] * len(parts),
        out_specs=_VMEM, name=name)(*deps, *parts)


TS_MM = 512
TS_IN = 1024
TS_GW = 1024
TS_MIX = 256


def _halved(a):
    n, r, cols = a.shape
    return a.reshape(n, 2, r // 2, cols)


def _rs_swap(name, grads, after=()):
    lands = [lax.empty((N_CHIPS,) + g.shape[2:], g.dtype) for g in grads]
    sems, grads, lands, token = _xchg_start(name + "_swap", _plan_swap, N_CHIPS * len(grads), grads, lands, after)
    return name, sems, grads, lands, token


def _rs_scatter(handle, after, chip, ci):
    name, sems, grads, lands, _ = handle
    grads, from_sibling = _xchg_wait(name + "_swap_wait", _plan_swap, sems, grads, lands, after)
    c_arr = jnp.reshape(ci, (1,)).astype(jnp.int32)
    pair_sums = [_add_halves(g, r, c_arr, "%s_add_halves_%d" % (name, k)) for k, (g, r) in enumerate(zip(grads, from_sibling))]
    lands = [lax.dynamic_update_index_in_dim(lax.empty(p.shape, p.dtype), lax.dynamic_index_in_dim(p, chip, 0, keepdims=False),
                                             chip, 0) for p in pair_sums]
    sems, pair_sums, lands, token = _xchg_start(name + "_scatter", _plan_scatter, 3 * len(pair_sums), pair_sums, lands)
    return name, sems, pair_sums, lands, token


def _rs_share(handle, after):
    name, sems, pair_sums, lands, _ = handle
    _, by_chip = _xchg_wait(name + "_scatter_wait", _plan_scatter, sems, pair_sums, lands, after)
    halves = [_sum_chips(b, "%s_sum_chips_%d" % (name, k)) for k, b in enumerate(by_chip)]
    lands = [lax.empty(h.shape, h.dtype) for h in halves]
    sems, halves, lands, token = _xchg_start(name + "_share", _plan_share, len(halves), halves, lands)
    return name, sems, halves, lands, token


def _rs_end(handle, after):
    name, sems, halves, lands, _ = handle
    halves, others = _xchg_wait(name + "_share_wait", _plan_share, sems, halves, lands, after)
    return list(zip(halves, others))


def kernel(x, c, w_ada, b_ada, g_norm_mix, w_in, conv_a_w, conv_b_w, conv_b_bias, w_rg_a, b_rg_a, w_rg_x, b_rg_x, lru_lambda, w_out, g_norm_ffn, w_gate_up, w_down, g_norm_final, loss_target, m_w_ada, m_b_ada, m_g_norm_mix, m_w_in, m_conv_a_w, m_conv_b_w, m_conv_b_bias, m_w_rg_a, m_b_rg_a, m_w_rg_x, m_b_rg_x, m_lru_lambda, m_w_out, m_g_norm_ffn, m_w_gate_up, m_w_down, m_g_norm_final, v_w_ada, v_b_ada, v_g_norm_mix, v_w_in, v_conv_a_w, v_conv_b_w, v_conv_b_bias, v_w_rg_a, v_b_rg_a, v_w_rg_x, v_b_rg_x, v_lru_lambda, v_w_out, v_g_norm_ffn, v_w_gate_up, v_w_down, v_g_norm_final):
    xi, yi, ci = lax.axis_index("x"), lax.axis_index("y"), lax.axis_index("c")
    chip = 2 * xi + yi
    me = 2 * chip + ci
    n_ada = w_ada.shape[2]

    def widen(w):
        return jnp.pad(w, ((0, 0), (0, D - w.shape[1])))

    got = _allgather8(_pack_rows([c, widen(conv_a_w[0]), widen(conv_b_w[0])], 8, "pack_c_conv"), "gather_c_conv")
    c_all = got[:, 0, :]
    conv_full = got[::2, 1:8, :D // N_CHIPS].transpose(1, 0, 2).reshape(7, D)

    mod_part, c_act = _ada_fwd(c_all, w_ada[0], lax.dynamic_slice_in_dim(b_ada, chip * n_ada, n_ada, axis=1))
    mod_all = _allgather8(mod_part, "gather_mod")
    mod_mine = lax.dynamic_index_in_dim(mod_all, me, axis=1, keepdims=False)[::2].reshape(6, D)
    vecs = _pack_rows([mod_mine, g_norm_mix, g_norm_ffn, g_norm_final.reshape(1, D), conv_b_bias, b_rg_a, b_rg_x, lru_lambda,
                       conv_full], N_VEC, "pack_vecs")

    def rg_shard(w):
        return w[0].astype(BF16).reshape(2, HEADS * HB // N_CHIPS // 2, HB)

    shards = [w_in[0].astype(BF16).reshape(2, D // 2, C_IN), rg_shard(w_rg_a), rg_shard(w_rg_x),
              w_out[0].astype(BF16).reshape(2, D // N_CHIPS // 2, D), w_gate_up[0].astype(BF16).reshape(2, D // 2, C_GU),
              w_down[0].astype(BF16).reshape(2, D_FF // N_CHIPS // 2, D)]
    lands = [lax.dynamic_update_index_in_dim(lax.empty((N_CHIPS,) + s.shape, s.dtype), s, chip, 0) for s in shards]

    def send(name, first, last, after, chips):
        copies = [(k, j) for k, which in zip(range(first, last), chips, strict=True) for j in which]
        sems, srcs, zone, token = _xchg_start(name + "_ici", _plan_gather_ici(chips), len(copies), shards[first:last],
                                              lands[first:last], after)
        shards[first:last], lands[first:last] = srcs, zone
        return sems, copies, token

    def arrive(name, sent, first, last, after):
        sems, copies, _ = sent
        chips = [tuple(j for k, j in copies if k == want) for want in range(first, last)]
        ids = [copies.index((k, j)) for k, which in zip(range(first, last), chips) for j in which]
        srcs, zone = _xchg_wait(name + "_ici_wait", _plan_gather_ici(chips), sems, shards[first:last], lands[first:last], after,
                                ids)
        shards[first:last], lands[first:last] = srcs, zone

    def relay(name, first, last, which):
        plan = _plan_relay(which)
        sems, _, zone, token = _xchg_start(name + "_d2d", plan, len(which) * (last - first), [], lands[first:last])
        lands[first:last] = zone
        return name, plan, sems, first, last, token

    def relayed(handle, after):
        name, plan, sems, first, last, _ = handle
        lands[first:last] = _xchg_wait(name + "_d2d_wait", plan, sems, [], lands[first:last], after)[1]

    def to_blocks(v):
        return v.reshape(-1, TS_MIX // TIME_BLOCKS, TIME_BLOCKS, D).transpose(0, 2, 1, 3).reshape(v.shape)

    def from_blocks(v):
        return v.reshape(-1, TIME_BLOCKS, TS_MIX // TIME_BLOCKS, D).transpose(0, 2, 1, 3).reshape(v.shape)

    def chip_index(j):
        return jnp.reshape(chip ^ j, (1,)).astype(jnp.int32)

    def wg_in():
        return lands[0].reshape(N_CHIPS, D, C_IN)

    xs, target = to_blocks(x[0]), to_blocks(loss_target[0])
    sent_near = send("gather_in_near", 0, 1, [vecs], [(1, 2)])
    ts_in = min(TS_IN, xs.shape[0])
    h1, proj = _fwd_in_first(xs, vecs, wg_in(), chip_index(0), ts_in, deps=[sent_near[-1]])
    arrive("gather_in_near", sent_near, 0, 1, [proj])
    near = relay("gather_in_near", 0, 1, (1, 2))
    sent_rest = send("gather_rest", 0, 6, [near[-1]], [(3,)] + [(1, 2, 3)] * 5)
    relayed(near, [sent_rest[-1]])
    proj = _fwd_in_more(h1, wg_in(), proj, chip_index(1), ts_in, "fwd_in_y")
    proj = _fwd_in_more(h1, wg_in(), proj, chip_index(2), ts_in, "fwd_in_x")
    arrive("gather_in_far", sent_rest, 0, 1, [proj])
    far = relay("gather_in_far", 0, 1, (3,))
    arrive("gather_mix", sent_rest, 1, 4, [far[-1]])
    relayed(far, [far[-1]])
    mix = relay("gather_mix", 1, 4, (1, 2, 3))
    proj = _fwd_in_more(h1, wg_in(), proj, chip_index(3), ts_in, "fwd_in_xy", deps=[mix[-1]])
    relayed(mix, [proj])
    wg_rga, wg_rgx, wg_out = lands[1:4]
    wg_out = wg_out.reshape(D, D)

    def rg_full(wg):
        return wg.reshape(N_CHIPS, HEADS, HB // N_CHIPS, HB).transpose(1, 0, 2, 3).reshape(HEADS, HB, HB)

    wg_rga, wg_rgx = rg_full(wg_rga), rg_full(wg_rgx)

    arrive("gather_ffn", sent_rest, 4, 6, [proj])
    ffn = relay("gather_ffn", 4, 6, (1, 2, 3))
    x1, merged, z1, kept, decay = _fwd_mix(proj, xs, vecs, wg_rga, wg_rgx, wg_out, TS_MIX, deps=[ffn[-1]])
    relayed(ffn, [x1])
    wg_gu, wg_dn = lands[4:6]
    wg_gu, wg_dn = wg_gu.reshape(N_CHIPS, D, C_GU), wg_dn.reshape(D_FF, D)
    dx1, h2, act, dz2, dgu, sm_ffn = _ffn_loss(x1, target, vecs, wg_gu, wg_dn, TS_MIX)

    def rg_chunks(dw):
        return _halved(dw.reshape(HEADS, N_CHIPS, HB // N_CHIPS, HB).transpose(1, 0, 2, 3).reshape(N_CHIPS, HB, HB).astype(BF16))

    ts_gw = min(TS_GW, xs.shape[0])
    g_dn = _grad_w(act, dz2, 1, ts_gw, "grad_w_down")
    g_gu = _grad_w(h2, dgu, N_CHIPS, ts_gw, "grad_w_gate_up")
    rs_b = _rs_swap("rs_b", [_halved(g_gu), _halved(g_dn.reshape(N_CHIPS, D_FF // N_CHIPS, D))])
    dproj, sm_mix, dw_rga, dw_rgx, dw_out = _bwd_mix(dx1, z1, merged, proj, kept, decay, vecs, wg_rga, wg_rgx, wg_out, sm_ffn, TS_MIX,
                                                     deps=[rs_b[-1]])
    rs_b = _rs_scatter(rs_b, [dproj], chip, ci)
    g_in = _grad_w(h1, dproj, N_CHIPS, ts_gw, "grad_w_in", deps=[rs_b[-1]])
    rs_b = _rs_share(rs_b, [g_in])
    rs_a = _rs_swap("rs_a", [_halved(g_in), rg_chunks(dw_rga), rg_chunks(dw_rgx),
                             _halved(dw_out.astype(BF16).reshape(N_CHIPS, D // N_CHIPS, D))], after=[rs_b[-1]])

    c_arr = jnp.reshape(ci, (1,)).astype(jnp.int32)

    def step(name, w, g, m, v, deps=()):
        shape = w.shape
        two_d = (-1, shape[-1])
        d, nm, nv = _adamw(w.reshape(two_d), g.reshape(two_d), m.reshape(two_d), v.reshape(two_d), "adamw_" + name, deps)
        return g.reshape(shape), d.reshape(shape), nm.reshape(shape), nv.reshape(shape)

    def step_halves(name, w, halves, m, v, deps=()):
        shape = w.shape
        two_d = (-1, shape[-1])
        out = _adamw_halves(w.reshape(two_d), halves[0], halves[1], m.reshape(two_d), v.reshape(two_d), c_arr, "adamw_" + name, deps)
        return tuple(a.reshape(shape) for a in out)

    def shard_cols(row_block):
        return lax.dynamic_slice_in_dim(row_block, chip * (D // N_CHIPS), D // N_CHIPS, axis=1)

    gw_gu, gw_dn = _rs_end(rs_b, [rs_a[-1]])
    res = {
        "w_gate_up": step_halves("w_gate_up", w_gate_up, gw_gu, m_w_gate_up, v_w_gate_up, [rs_a[-1]]),
        "w_down": step_halves("w_down", w_down, gw_dn, m_w_down, v_w_down, [rs_a[-1]]),
    }
    rs_a = _rs_scatter(rs_a, [res["w_gate_up"][1], res["w_down"][1]], chip, ci)
    grad_x, sm_in = _bwd_in(dproj, xs, dx1, vecs, wg_in(), sm_mix, TS_MM, deps=[rs_a[-1]])
    rs_a = _rs_share(rs_a, [grad_x])

    small, per_dev = _sum_small(_allgather8(sm_in, "gather_small", deps=[rs_a[-1]]))
    dmod_all = per_dev[:, 0:6, :].reshape(N_DEV, 6 * D)
    grad_w_ada = _ada_bwd(c_act, lax.dynamic_slice_in_dim(dmod_all, chip * n_ada, n_ada, axis=1))
    grad_b_ada = small[0:6].reshape(1, 6 * D)
    res["w_ada"] = step("w_ada", w_ada, grad_w_ada[None], m_w_ada, v_w_ada)
    small_sets = {
        "b_ada": (b_ada.reshape(6, D), grad_b_ada.reshape(6, D), m_b_ada.reshape(6, D), v_b_ada.reshape(6, D)),
        "g_norm_mix": (g_norm_mix, small[G_GMIX:G_GMIX + 1], m_g_norm_mix, v_g_norm_mix),
        "conv_a_w": (conv_a_w[0], shard_cols(small[G_WA0:G_WA0 + 3]), m_conv_a_w[0], v_conv_a_w[0]),
        "conv_b_w": (conv_b_w[0], shard_cols(small[G_WB0:G_WB0 + 4]), m_conv_b_w[0], v_conv_b_w[0]),
        "conv_b_bias": (conv_b_bias, small[G_CBB:G_CBB + 1], m_conv_b_bias, v_conv_b_bias),
        "b_rg_a": (b_rg_a, small[G_BA:G_BA + 1], m_b_rg_a, v_b_rg_a),
        "b_rg_x": (b_rg_x, small[G_BX:G_BX + 1], m_b_rg_x, v_b_rg_x),
        "lru_lambda": (lru_lambda, small[G_LAM:G_LAM + 1], m_lru_lambda, v_lru_lambda),
        "g_norm_ffn": (g_norm_ffn, small[G_GFFN:G_GFFN + 1], m_g_norm_ffn, v_g_norm_ffn),
        "g_norm_final": (g_norm_final.reshape(1, D), small[G_GFIN:G_GFIN + 1], m_g_norm_final.reshape(1, D),
                         v_g_norm_final.reshape(1, D)),
    }
    stepped = _adamw_small(list(small_sets.values()), "adamw_small")
    for (n, (w_, g_, _, _)), (d_, nm_, nv_) in zip(small_sets.items(), stepped):
        shape = (1,) + w_.shape if n.startswith("conv_") and n != "conv_b_bias" else w_.shape
        res[n] = tuple(a.reshape(shape) for a in (g_, d_, nm_, nv_))
    gw_in, gw_rga, gw_rgx, gw_out = _rs_end(rs_a, [res[n][1] for n in res])
    res["w_in"] = step_halves("w_in", w_in, gw_in, m_w_in, v_w_in)
    res["w_rg_a"] = step_halves("w_rg_a", w_rg_a, gw_rga, m_w_rg_a, v_w_rg_a)
    res["w_rg_x"] = step_halves("w_rg_x", w_rg_x, gw_rgx, m_w_rg_x, v_w_rg_x)
    res["w_out"] = step_halves("w_out", w_out, gw_out, m_w_out, v_w_out)
    res["b_ada"] = tuple(a.reshape(1, 6 * D) for a in res["b_ada"])
    res["g_norm_final"] = tuple(a.reshape(D) for a in res["g_norm_final"])
    names = ["w_ada", "b_ada", "g_norm_mix", "w_in", "conv_a_w", "conv_b_w", "conv_b_bias", "w_rg_a", "b_rg_a", "w_rg_x",
             "b_rg_x", "lru_lambda", "w_out", "g_norm_ffn", "w_gate_up", "w_down", "g_norm_final"]
    loss = jnp.sum(small[G_LOSS])
    return (loss, from_blocks(grad_x)[None], *[res[n][0] for n in names], *[res[n][1] for n in names],
            *[res[n][2] for n in names], *[res[n][3] for n in names])
```

```python
import functools

import jax
import jax.numpy as jnp
from jax import lax
from jax.experimental import pallas as pl
from jax.experimental.pallas import tpu as pltpu

F32 = jnp.float32
BF16 = jnp.bfloat16
MESH = pl.DeviceIdType.MESH

D = 1024
N_CHIPS = 4
N_DEV = 8
D_IN = 7 * D
C_IN = D_IN // N_CHIPS
D_FF = 2816
C_GU = 2 * D_FF // N_CHIPS
HEADS = 4
HB = D // HEADS
EPS = 1e-6
LRU_C = 8.0
ADAM_LR, ADAM_B1, ADAM_B2, ADAM_EPS, ADAM_WD, ADAM_STEP = 0.001, 0.9, 0.999, 1e-08, 0.01, 10
VMEM_LIMIT = 56 << 20

(V_SH1, V_SC1, V_GT1, V_SH2, V_SC2, V_GT2, V_GMIX, V_GFFN, V_GFIN, V_CBB, V_BA, V_BX, V_LAM,
 V_WA0, V_WA1, V_WA2, V_WB0, V_WB1, V_WB2, V_WB3) = range(20)
N_VEC = 24
(G_SH1, G_SC1, G_GT1, G_SH2, G_SC2, G_GT2, G_GMIX, G_CBB, G_BA, G_BX, G_LAM, G_GFFN, G_GFIN,
 G_WA0, G_WA1, G_WA2, G_WB0, G_WB1, G_WB2, G_WB3, G_LOSS) = range(21)
N_SMALL = 24

_VMEM = pl.BlockSpec(memory_space=pltpu.VMEM)
_ANY = pl.BlockSpec(memory_space=pl.ANY)


def _cparams(n_grid=1):
    return pltpu.CompilerParams(dimension_semantics=("arbitrary",) * n_grid, vmem_limit_bytes=VMEM_LIMIT)


def _after(deps, body):
    n = len(deps)
    return lambda *refs: body(*refs[n:])


def _rms(x):
    rstd = lax.rsqrt(jnp.mean(x * x, axis=-1, keepdims=True) + EPS)
    return x * rstd, rstd


def _rms_bwd(dxhat, xhat, rstd):
    return rstd * (dxhat - xhat * jnp.mean(dxhat * xhat, axis=-1, keepdims=True))


def _rowsum(v):
    return jnp.sum(v, axis=0, keepdims=True)


def _dot(a, b):
    return jnp.dot(a, b, preferred_element_type=F32)


def _dot_nt(a, b):
    return lax.dot_general(a, b, (((1,), (1,)), ((), ())), preferred_element_type=F32)


def _dot_tn(a, b):
    return lax.dot_general(a, b, (((0,), (0,)), ((), ())), preferred_element_type=F32)


def _gelu(x):
    k, c = 0.7978845608028654, 0.044715
    t = jnp.tanh(k * (x + c * x * x * x))
    return 0.5 * x * (1.0 + t), 0.5 * (1.0 + t) + 0.5 * x * (1.0 - t * t) * k * (1.0 + 3.0 * c * x * x)


def _log_sigmoid(lam):
    return jnp.minimum(lam, 0.0) - jnp.log1p(jnp.exp(-jnp.abs(lam)))


def _lru_gates(u, wa_ref, wx_ref, v_ref, row0):
    ub = u.astype(BF16)
    pre_a = jnp.concatenate([_dot(ub[:, h * HB:(h + 1) * HB], wa_ref[h]) for h in range(HEADS)], axis=1)
    pre_x = jnp.concatenate([_dot(ub[:, h * HB:(h + 1) * HB], wx_ref[h]) for h in range(HEADS)], axis=1)
    r = jax.nn.sigmoid(pre_a + v_ref[V_BA:V_BA + 1, :])
    ig = jax.nn.sigmoid(pre_x + v_ref[V_BX:V_BX + 1, :])
    log_a = LRU_C * r * _log_sigmoid(v_ref[V_LAM:V_LAM + 1, :])
    a = jnp.exp(log_a)
    x2 = 2.0 * log_a
    m2 = jnp.where(x2 > -0.03, -x2 * (1.0 + x2 * (0.5 + x2 * (1.0 / 6.0 + x2 * (1.0 / 24.0)))), 1.0 - a * a)
    mult = jnp.where(row0, 1.0, jnp.sqrt(jnp.maximum(m2, 0.0)))
    return r, ig, a, mult


TIME_BLOCKS = 8
N_KEPT = 10


def _late_blocks(v, buf, g, halo=None):
    n = buf.shape[0]
    out = []
    for idx in range(n):
        k = TIME_BLOCKS - n + idx
        buf[idx, 8:g + 8, :] = v[k * g:(k + 1) * g]
        if halo is not None:
            buf[idx, 7:8, :] = halo[idx]
        out.append(buf[idx, pl.ds(7, g), :])
        if halo is None:
            buf[idx, 7:8, :] = buf[idx, g + 7:g + 8, :]
    return out


def _earlier(v, s, late, g):
    return jnp.concatenate(late[len(late) - s:] + [v[0:(TIME_BLOCKS - s) * g]], axis=0)


def _early_blocks(v, buf, g):
    out = []
    for k in range(buf.shape[0]):
        buf[k, 0:g, :] = v[k * g:(k + 1) * g]
        out.append(buf[k, pl.ds(1, g), :])
        buf[k, g:g + 1, :] = buf[k, 0:1, :]
    return out


def _later(v, s, early, g):
    return jnp.concatenate([v[s * g:]] + early[0:s], axis=0)


def _fwd_in_first(x, vecs, w_in_g, q_idx, ts, deps=()):
    s = x.shape[0]

    def body(q_ref, x_ref, v_ref, w_ref, h1_ref, proj_ref):
        xhat, _ = _rms(x_ref[...])
        h = xhat * v_ref[V_GMIX:V_GMIX + 1, :] * (1.0 + v_ref[V_SC1:V_SC1 + 1, :]) + v_ref[V_SH1:V_SH1 + 1, :]
        hb = h.astype(BF16)
        h1_ref[...] = hb
        proj_ref[...] = _dot(hb, w_ref[...]).astype(BF16)

    return pl.pallas_call(
        lambda q_ref, *refs: body(q_ref, *refs[len(deps):]),
        grid_spec=pltpu.PrefetchScalarGridSpec(
            num_scalar_prefetch=1, grid=(s // ts,),
            in_specs=[_ANY] * len(deps) + [pl.BlockSpec((ts, D), lambda i, q: (i, 0)), _VMEM,
                                           pl.BlockSpec((None, D, C_IN), lambda i, q: (q[0], 0, 0))],
            out_specs=[pl.BlockSpec((ts, D), lambda i, q: (i, 0)), pl.BlockSpec((ts, C_IN), lambda i, q: (i, q[0]))]),
        out_shape=(jax.ShapeDtypeStruct((s, D), BF16), jax.ShapeDtypeStruct((s, D_IN), BF16)),
        compiler_params=_cparams(), name="fwd_in_own")(q_idx, *deps, x, vecs, w_in_g)


def _fwd_in_more(h1, w_in_g, proj, q_idx, ts, name, deps=()):
    s = h1.shape[0]

    def body(q_ref, h1_ref, w_ref, proj_in_ref, proj_ref):
        proj_ref[...] = _dot(h1_ref[...], w_ref[...]).astype(BF16)

    return pl.pallas_call(
        lambda q_ref, *refs: body(q_ref, *refs[len(deps):]),
        grid_spec=pltpu.PrefetchScalarGridSpec(
            num_scalar_prefetch=1, grid=(s // ts,),
            in_specs=[_ANY] * len(deps) + [pl.BlockSpec((ts, D), lambda i, q: (i, 0)),
                                           pl.BlockSpec((None, D, C_IN), lambda i, q: (q[0], 0, 0)), _ANY],
            out_specs=pl.BlockSpec((ts, C_IN), lambda i, q: (i, q[0]))),
        out_shape=jax.ShapeDtypeStruct((s, D_IN), BF16), input_output_aliases={len(deps) + 3: 0},
        compiler_params=_cparams(), name=name)(q_idx, *deps, h1, w_in_g, proj)


def _fwd_mix(proj, x, vecs, w_rga, w_rgx, w_out, ts, deps=()):
    s = x.shape[0]
    g = ts // TIME_BLOCKS

    def body(proj_ref, x_ref, v_ref, wa_ref, wx_ref, wo_ref, x1_ref, mg_ref, z1_ref, kept_ref, decay_ref,
             ua_buf, rx_buf, p_buf, q_buf, c_buf, hcarry):
        i = pl.program_id(0)

        @pl.when(i == 0)
        def _():
            ua_buf[...] = jnp.zeros(ua_buf.shape, F32)
            rx_buf[...] = jnp.zeros(rx_buf.shape, F32)
            hcarry[...] = jnp.zeros((8, D), F32)

        def seg(j):
            return proj_ref[:, j * D:(j + 1) * D].astype(F32)

        def vrow(j):
            return v_ref[j:j + 1, :]

        cb, cc, cx, rx, rg, ga, gb = (seg(j) for j in range(7))
        ua = cc * cx
        ua_late = _late_blocks(ua, ua_buf, g)
        rx_late = _late_blocks(rx, rx_buf, g)
        va = vrow(V_WA2) * ua + vrow(V_WA1) * _earlier(ua, 1, ua_late, g) + vrow(V_WA0) * _earlier(ua, 2, ua_late, g)
        u = (vrow(V_WB3) * rx + vrow(V_WB2) * _earlier(rx, 1, rx_late, g) + vrow(V_WB1) * _earlier(rx, 2, rx_late, g)
             + vrow(V_WB0) * _earlier(rx, 3, rx_late, g) + vrow(V_CBB))

        rows = lax.broadcasted_iota(jnp.int32, (ts, D), 0)
        row0 = jnp.logical_and(rows == 0, i == 0)
        r, ig, a, mult = _lru_gates(u, wa_ref, wx_ref, v_ref, row0)
        decay_ref[...] = a
        bx = mult * (ig * u)

        prods, sums = [a[0:g]], [bx[0:g]]
        for k in range(1, TIME_BLOCKS):
            ak = a[k * g:(k + 1) * g]
            sums.append(ak * sums[-1] + bx[k * g:(k + 1) * g])
            prods.append(ak * prods[-1])
        p_buf[...] = prods[-1]
        q_buf[...] = sums[-1]
        state = hcarry[0:1, :]
        for j in range(g):
            c_buf[j:j + 1, :] = state
            state = p_buf[j:j + 1, :] * state + q_buf[j:j + 1, :]
        hcarry[0:1, :] = state
        entering = c_buf[...]
        h = jnp.concatenate([sums[k] + prods[k] * entering for k in range(TIME_BLOCKS)], axis=0)

        gel, dgel = _gelu(rg)
        sga = jax.nn.sigmoid(ga)
        sgb = jax.nn.sigmoid(gb)
        for j, keep in enumerate((va, r, ig, sga, sgb, gel, dgel, mult, u, h)):
            kept_ref[:, j * D:(j + 1) * D] = keep.astype(BF16)
        merged = (sga * (cb * va) + sgb * (h * gel)).astype(BF16)
        mg_ref[...] = merged
        z1 = _dot(merged, wo_ref[...])
        z1_ref[...] = z1.astype(BF16)
        x1_ref[...] = x_ref[...] + vrow(V_GT1) * z1

    row = lambda i: (i, 0)
    return pl.pallas_call(
        _after(deps, body), grid=(s // ts,),
        out_shape=(jax.ShapeDtypeStruct((s, D), F32), jax.ShapeDtypeStruct((s, D), BF16), jax.ShapeDtypeStruct((s, D), BF16),
                   jax.ShapeDtypeStruct((s, N_KEPT * D), BF16), jax.ShapeDtypeStruct((s, D), F32)),
        in_specs=[_ANY] * len(deps) + [pl.BlockSpec((ts, D_IN), row), pl.BlockSpec((ts, D), row), _VMEM, _VMEM, _VMEM, _VMEM],
        out_specs=[pl.BlockSpec((ts, D), row)] * 3 + [pl.BlockSpec((ts, N_KEPT * D), row), pl.BlockSpec((ts, D), row)],
        scratch_shapes=[pltpu.VMEM((2, g + 8, D), F32), pltpu.VMEM((3, g + 8, D), F32), pltpu.VMEM((g, D), F32),
                        pltpu.VMEM((g, D), F32), pltpu.VMEM((g, D), F32), pltpu.VMEM((8, D), F32)],
        compiler_params=_cparams(), name="fwd_mix")(*deps, proj, x, vecs, w_rga, w_rgx, w_out)


def _ffn_loss(x1, target, vecs, w_gu_g, w_dn, ts):
    s = x1.shape[0]

    def body(x1_ref, t_ref, v_ref, wgu_ref, wdn_ref, dx1_ref, h2_ref, act_ref, dz2_ref, dgu_ref, sm_ref):
        @pl.when(pl.program_id(0) == 0)
        def _():
            sm_ref[...] = jnp.zeros((N_SMALL, D), F32)

        def vrow(j):
            return v_ref[j:j + 1, :]

        n_sub = 1
        rows = [slice(k * (ts // n_sub), (k + 1) * (ts // n_sub)) for k in range(n_sub)]
        subs = [dict(r=r, sums={}) for r in rows]

        def stage_norm(t):
            t["x1"] = x1_ref[t["r"], :]
            t["xh1"], t["rstd1"] = _rms(t["x1"])
            t["n2"] = t["xh1"] * vrow(V_GFFN)
            t["h2"] = (t["n2"] * (1.0 + vrow(V_SC2)) + vrow(V_SH2)).astype(BF16)
            h2_ref[t["r"], :] = t["h2"]

        def stage_up(t):
            h2 = t["h2"]
            g = jnp.concatenate([_dot(h2, wgu_ref[0]), _dot(h2, wgu_ref[1])], axis=1)
            t["up"] = jnp.concatenate([_dot(h2, wgu_ref[2]), _dot(h2, wgu_ref[3])], axis=1)
            t["g"] = g
            t["sg"] = jax.nn.sigmoid(g)
            t["silu"] = g * t["sg"]
            t["act"] = (t["silu"] * t["up"]).astype(BF16)
            act_ref[t["r"], :] = t["act"]

        def stage_down_loss(t):
            z2 = _dot(t["act"], wdn_ref[...])
            x2 = t["x1"] + vrow(V_GT2) * z2
            xh2, rstd2 = _rms(x2)
            err = xh2 * vrow(V_GFIN) - t_ref[t["r"], :]
            t["sums"][G_LOSS] = _rowsum((0.5 / D) * err * err)
            dy = err * (1.0 / D)
            t["sums"][G_GFIN] = _rowsum(dy * xh2)
            t["dx2"] = _rms_bwd(dy * vrow(V_GFIN), xh2, rstd2)
            t["sums"][G_GT2] = _rowsum(t["dx2"] * z2)
            t["dz2"] = (vrow(V_GT2) * t["dx2"]).astype(BF16)
            dz2_ref[t["r"], :] = t["dz2"]

        def stage_back_act(t):
            dact = _dot_nt(t["dz2"], wdn_ref[...])
            g, sg = t["g"], t["sg"]
            t["dgate"] = (dact * t["up"] * (sg * (1.0 + g * (1.0 - sg)))).astype(BF16)
            t["dup"] = (dact * t["silu"]).astype(BF16)
            dgu_ref[t["r"], 0:D_FF] = t["dgate"]
            dgu_ref[t["r"], D_FF:2 * D_FF] = t["dup"]

        def stage_back_norm(t):
            dgate, dup = t["dgate"], t["dup"]
            dh2 = (_dot_nt(dgate[:, 0:C_GU], wgu_ref[0]) + _dot_nt(dgate[:, C_GU:2 * C_GU], wgu_ref[1])
                   + _dot_nt(dup[:, 0:C_GU], wgu_ref[2]) + _dot_nt(dup[:, C_GU:2 * C_GU], wgu_ref[3]))
            t["sums"][G_SH2] = _rowsum(dh2)
            t["sums"][G_SC2] = _rowsum(dh2 * t["n2"])
            dn2 = dh2 * (1.0 + vrow(V_SC2))
            t["sums"][G_GFFN] = _rowsum(dn2 * t["xh1"])
            dx1_ref[t["r"], :] = t["dx2"] + _rms_bwd(dn2 * vrow(V_GFFN), t["xh1"], t["rstd1"])

        for stage in (stage_norm, stage_up, stage_down_loss, stage_back_act, stage_back_norm):
            for t in subs:
                stage(t)
        for j in subs[0]["sums"]:
            total = subs[0]["sums"][j]
            for t in subs[1:]:
                total = total + t["sums"][j]
            sm_ref[j:j + 1, :] += total

    row = lambda i: (i, 0)
    return pl.pallas_call(
        body, grid=(s // ts,),
        out_shape=(jax.ShapeDtypeStruct((s, D), F32), jax.ShapeDtypeStruct((s, D), BF16), jax.ShapeDtypeStruct((s, D_FF), BF16),
                   jax.ShapeDtypeStruct((s, D), BF16), jax.ShapeDtypeStruct((s, 2 * D_FF), BF16),
                   jax.ShapeDtypeStruct((N_SMALL, D), F32)),
        in_specs=[pl.BlockSpec((ts, D), row), pl.BlockSpec((ts, D), row), _VMEM, _VMEM, _VMEM],
        out_specs=[pl.BlockSpec((ts, D), row), pl.BlockSpec((ts, D), row), pl.BlockSpec((ts, D_FF), row),
                   pl.BlockSpec((ts, D), row), pl.BlockSpec((ts, 2 * D_FF), row), pl.BlockSpec((N_SMALL, D), lambda i: (0, 0))],
        compiler_params=_cparams(), name="ffn_loss")(x1, target, vecs, w_gu_g, w_dn)


def _bwd_mix(dx1, z1, merged, proj, kept, decay, vecs, w_rga, w_rgx, w_out, small, ts, deps=()):
    s = dx1.shape[0]
    nt = s // ts
    g = ts // TIME_BLOCKS
    assert g % 16 == 0

    def body(dx1_ref, z1_ref, mg_ref, proj_ref, kept_ref, decay_ref, hh_ref, v_ref, wa_ref, wx_ref,
             wo_ref, sm0_ref, dproj_ref, sm_ref, dwa_ref, dwx_ref, dwo_ref,
             h_buf, a_buf, dva_buf, du_buf, p_buf, q_buf, c_buf, lcarry):
        i = pl.program_id(0)
        first_tile = i == nt - 1

        @pl.when(i == 0)
        def _():
            a_buf[...] = jnp.zeros(a_buf.shape, F32)
            dva_buf[...] = jnp.zeros(dva_buf.shape, F32)
            du_buf[...] = jnp.zeros(du_buf.shape, F32)
            lcarry[...] = jnp.zeros((8, D), F32)
            sm_ref[...] = sm0_ref[...]
            dwa_ref[...] = jnp.zeros((HEADS, HB, HB), F32)
            dwx_ref[...] = jnp.zeros((HEADS, HB, HB), F32)
            dwo_ref[...] = jnp.zeros((D, D), F32)

        def seg(j):
            return proj_ref[:, j * D:(j + 1) * D].astype(F32)

        def vrow(j):
            return v_ref[j:j + 1, :]

        def acc(j, val):
            sm_ref[j:j + 1, :] += _rowsum(val)

        cb, cc, cx, rx = (seg(j) for j in range(4))
        ua = cc * cx
        va, r, ig, sga, sgb, gel, dgel, mult, u, h = (kept_ref[:, j * D:(j + 1) * D].astype(F32) for j in range(N_KEPT))
        a = decay_ref[...]
        rows = lax.broadcasted_iota(jnp.int32, (ts, D), 0)
        row0 = jnp.logical_and(rows == 0, first_tile)

        dx1 = dx1_ref[...]
        acc(G_GT1, dx1 * z1_ref[...].astype(F32))
        dz1 = (vrow(V_GT1) * dx1).astype(BF16)
        dwo_ref[...] += _dot_tn(mg_ref[...], dz1)
        dmg = _dot_nt(dz1, wo_ref[...])
        dya = dmg * sga
        dyb = dmg * sgb
        dproj_ref[:, 5 * D:6 * D] = (dya * (cb * va) * (1.0 - sga)).astype(BF16)
        dproj_ref[:, 6 * D:7 * D] = (dyb * (h * gel) * (1.0 - sgb)).astype(BF16)

        dproj_ref[:, 0:D] = (dya * va).astype(BF16)
        dva = dya * cb
        dva_early = _early_blocks(dva, dva_buf, g)
        dva1 = _later(dva, 1, dva_early, g)
        dva2 = _later(dva, 2, dva_early, g)
        dua = vrow(V_WA2) * dva + vrow(V_WA1) * dva1 + vrow(V_WA0) * dva2
        acc(G_WA2, ua * dva)
        acc(G_WA1, ua * dva1)
        acc(G_WA0, ua * dva2)
        dproj_ref[:, D:2 * D] = (dua * cx).astype(BF16)
        dproj_ref[:, 2 * D:3 * D] = (dua * cc).astype(BF16)

        dproj_ref[:, 4 * D:5 * D] = (dyb * h * dgel).astype(BF16)
        a_next = _later(a, 1, _early_blocks(a, a_buf, g), g)
        dh = dyb * gel
        last = TIME_BLOCKS - 1
        prods, sums = {last: a_next[last * g:]}, {last: dh[last * g:]}
        for k in range(last - 1, -1, -1):
            ak = a_next[k * g:(k + 1) * g]
            sums[k] = dh[k * g:(k + 1) * g] + ak * sums[k + 1]
            prods[k] = ak * prods[k + 1]
        p_buf[...] = prods[0]
        q_buf[...] = sums[0]
        state = lcarry[0:1, :]
        for j in range(g - 1, -1, -1):
            c_buf[j:j + 1, :] = state
            state = q_buf[j:j + 1, :] + p_buf[j:j + 1, :] * state
        lcarry[0:1, :] = state
        entering = c_buf[...]
        lam = jnp.concatenate([sums[k] + prods[k] * entering for k in range(TIME_BLOCKS)], axis=0)

        last = lax.broadcasted_iota(jnp.int32, hh_ref.shape, 0) == hh_ref.shape[0] - 1
        h_halo = [jnp.where(first_tile, 0.0, jnp.sum(jnp.where(last, hh_ref[...].astype(F32), 0.0), axis=0, keepdims=True))]
        da = lam * _earlier(h, 1, _late_blocks(h, h_buf, g, h_halo), g)
        dmult = jnp.where(row0, 0.0, lam * (ig * u))
        di = lam * mult * u
        du = lam * mult * ig
        dlog_a = da * a - dmult * (a * a) / mult
        lam_p = vrow(V_LAM)
        dr = dlog_a * (LRU_C * _log_sigmoid(lam_p))
        sm_ref[G_LAM:G_LAM + 1, :] += _rowsum(dlog_a * r) * (LRU_C * jax.nn.sigmoid(-lam_p))
        dpa = dr * r * (1.0 - r)
        dpx = di * ig * (1.0 - ig)
        acc(G_BA, dpa)
        acc(G_BX, dpx)
        dpab = dpa.astype(BF16)
        dpxb = dpx.astype(BF16)
        ub = u.astype(BF16)
        back = []
        for hd in range(HEADS):
            cols = slice(hd * HB, (hd + 1) * HB)
            back.append(_dot_nt(dpab[:, cols], wa_ref[hd]) + _dot_nt(dpxb[:, cols], wx_ref[hd]))
            dwa_ref[hd] += _dot_tn(ub[:, cols], dpab[:, cols])
            dwx_ref[hd] += _dot_tn(ub[:, cols], dpxb[:, cols])
        du = du + jnp.concatenate(back, axis=1)

        acc(G_CBB, du)
        du_early = _early_blocks(du, du_buf, g)
        du1 = _later(du, 1, du_early, g)
        du2 = _later(du, 2, du_early, g)
        du3 = _later(du, 3, du_early, g)
        dproj_ref[:, 3 * D:4 * D] = (vrow(V_WB3) * du + vrow(V_WB2) * du1 + vrow(V_WB1) * du2 + vrow(V_WB0) * du3).astype(BF16)
        acc(G_WB3, rx * du)
        acc(G_WB2, rx * du1)
        acc(G_WB1, rx * du2)
        acc(G_WB0, rx * du3)

    rev = lambda i: (nt - 1 - i, 0)
    h_halo16 = lambda i: (jnp.maximum((nt - 1 - i) * (ts // 16) - 1, 0), N_KEPT - 1)
    const2 = lambda i: (0, 0)
    const3 = lambda i: (0, 0, 0)
    return pl.pallas_call(
        _after(deps, body), grid=(nt,),
        out_shape=(jax.ShapeDtypeStruct((s, D_IN), BF16), jax.ShapeDtypeStruct((N_SMALL, D), F32),
                   jax.ShapeDtypeStruct((HEADS, HB, HB), F32), jax.ShapeDtypeStruct((HEADS, HB, HB), F32),
                   jax.ShapeDtypeStruct((D, D), F32)),
        in_specs=[_ANY] * len(deps) + [pl.BlockSpec((ts, D), rev), pl.BlockSpec((ts, D), rev), pl.BlockSpec((ts, D), rev),
                  pl.BlockSpec((ts, 4 * D), rev), pl.BlockSpec((ts, N_KEPT * D), rev), pl.BlockSpec((ts, D), rev),
                  pl.BlockSpec((16, D), h_halo16), _VMEM, _VMEM, _VMEM, _VMEM, _VMEM],
        out_specs=[pl.BlockSpec((ts, D_IN), rev), pl.BlockSpec((N_SMALL, D), const2),
                   pl.BlockSpec((HEADS, HB, HB), const3), pl.BlockSpec((HEADS, HB, HB), const3), pl.BlockSpec((D, D), const2)],
        scratch_shapes=[pltpu.VMEM((1, g + 8, D), F32), pltpu.VMEM((1, g + 8, D), F32),
                        pltpu.VMEM((2, g + 8, D), F32), pltpu.VMEM((3, g + 8, D), F32), pltpu.VMEM((g, D), F32),
                        pltpu.VMEM((g, D), F32), pltpu.VMEM((g, D), F32), pltpu.VMEM((8, D), F32)],
        compiler_params=_cparams(), name="bwd_mix")(*deps, dx1, z1, merged, proj, kept, decay, kept, vecs, w_rga,
                                                    w_rgx, w_out, small)


def _bwd_in(dproj, x, dx1, vecs, w_in_g, small, ts, deps=()):
    s = x.shape[0]

    def body(dp_ref, x_ref, dx1_ref, v_ref, w_ref, sm0_ref, gx_ref, sm_ref):
        @pl.when(pl.program_id(0) == 0)
        def _():
            sm_ref[...] = sm0_ref[...]

        def vrow(j):
            return v_ref[j:j + 1, :]

        dh1 = _dot_nt(dp_ref[:, 0:C_IN], w_ref[0])
        for k in range(1, N_CHIPS):
            dh1 += _dot_nt(dp_ref[:, k * C_IN:(k + 1) * C_IN], w_ref[k])
        xh, rstd = _rms(x_ref[...])
        sm_ref[G_SH1:G_SH1 + 1, :] += _rowsum(dh1)
        sm_ref[G_SC1:G_SC1 + 1, :] += _rowsum(dh1 * (xh * vrow(V_GMIX)))
        dn1 = dh1 * (1.0 + vrow(V_SC1))
        sm_ref[G_GMIX:G_GMIX + 1, :] += _rowsum(dn1 * xh)
        gx_ref[...] = dx1_ref[...] + _rms_bwd(dn1 * vrow(V_GMIX), xh, rstd)

    row = lambda i: (i, 0)
    return pl.pallas_call(
        _after(deps, body), grid=(s // ts,),
        out_shape=(jax.ShapeDtypeStruct((s, D), F32), jax.ShapeDtypeStruct((N_SMALL, D), F32)),
        in_specs=[_ANY] * len(deps) + [pl.BlockSpec((ts, D_IN), row), pl.BlockSpec((ts, D), row), pl.BlockSpec((ts, D), row),
                                       _VMEM, _VMEM, _VMEM],
        out_specs=[pl.BlockSpec((ts, D), row), pl.BlockSpec((N_SMALL, D), lambda i: (0, 0))],
        compiler_params=_cparams(), name="bwd_in")(*deps, dproj, x, dx1, vecs, w_in_g, small)


def _grad_w(a, b, n_col_blocks, ts, name, deps=()):
    s, m = a.shape
    tn = b.shape[1] // n_col_blocks
    n_steps = s // ts

    def body(a_ref, b_ref, o_ref, acc_ref):
        k = pl.program_id(1)

        @pl.when(k == 0)
        def _():
            acc_ref[...] = jnp.zeros((m, tn), F32)

        acc_ref[...] += _dot_tn(a_ref[...], b_ref[...])

        @pl.when(k == n_steps - 1)
        def _():
            o_ref[...] = acc_ref[...].astype(BF16)

    return pl.pallas_call(
        _after(deps, body), grid=(n_col_blocks, n_steps),
        out_shape=jax.ShapeDtypeStruct((n_col_blocks, m, tn), BF16),
        in_specs=[_ANY] * len(deps) + [pl.BlockSpec((ts, m), lambda n, k: (k, 0)), pl.BlockSpec((ts, tn), lambda n, k: (k, n))],
        out_specs=pl.BlockSpec((None, m, tn), lambda n, k: (n, 0, 0)),
        scratch_shapes=[pltpu.VMEM((m, tn), F32)],
        compiler_params=_cparams(2), name=name)(*deps, a, b)


def _ada_fwd(c_all, w_ada, b_ada):
    n = w_ada.shape[1]

    def body(c_ref, w_ref, b_ref, o_ref, ca_ref):
        c = c_ref[...]
        ca = c * jax.nn.sigmoid(c)
        ca_ref[...] = ca
        o_ref[...] = jnp.dot(ca, w_ref[...], preferred_element_type=F32, precision=lax.Precision.HIGHEST) + b_ref[...]

    return pl.pallas_call(
        body, out_shape=(jax.ShapeDtypeStruct((N_DEV, n), F32), jax.ShapeDtypeStruct((N_DEV, D), F32)),
        in_specs=[_VMEM] * 3, out_specs=[_VMEM] * 2, compiler_params=_cparams(0), name="ada_fwd")(c_all, w_ada, b_ada)


def _ada_bwd(c_act, dmod):
    n = dmod.shape[1]

    def body(c_ref, d_ref, o_ref):
        o_ref[...] = lax.dot_general(c_ref[...], d_ref[...], (((0,), (0,)), ((), ())), preferred_element_type=F32,
                                     precision=lax.Precision.HIGHEST)

    return pl.pallas_call(
        body, out_shape=jax.ShapeDtypeStruct((D, n), F32), in_specs=[_VMEM] * 2, out_specs=_VMEM,
        compiler_params=_cparams(0), name="ada_bwd")(c_act, dmod)


def _sum_small(parts):
    def body(p_ref, o_ref, d_ref):
        tot = p_ref[0]
        for dev in range(1, N_DEV):
            tot = tot + p_ref[dev]
        o_ref[...] = tot
        d_ref[...] = p_ref[:, 0:8, :]

    return pl.pallas_call(
        body, out_shape=(jax.ShapeDtypeStruct((N_SMALL, D), F32), jax.ShapeDtypeStruct((N_DEV, 8, D), F32)),
        in_specs=[_VMEM], out_specs=[_VMEM] * 2, compiler_params=_cparams(0), name="sum_small")(parts)


def _adamw(w, g, m, v, name, deps=()):
    rows, cols = w.shape
    tr = 128 if rows % 128 == 0 else (64 if rows % 64 == 0 else rows)

    def body(w_ref, g_ref, m_ref, v_ref, d_ref, nm_ref, nv_ref):
        g_ = g_ref[...]
        m_ = ADAM_B1 * m_ref[...] + (1.0 - ADAM_B1) * g_
        v_ = ADAM_B2 * v_ref[...] + (1.0 - ADAM_B2) * (g_ * g_)
        nm_ref[...] = m_
        nv_ref[...] = v_
        m_hat = m_ / (1.0 - ADAM_B1 ** ADAM_STEP)
        v_hat = v_ / (1.0 - ADAM_B2 ** ADAM_STEP)
        d_ref[...] = -ADAM_LR * (m_hat / (jnp.sqrt(v_hat) + ADAM_EPS) + ADAM_WD * w_ref[...])

    spec = pl.BlockSpec((tr, cols), lambda i: (i, 0))
    return pl.pallas_call(
        _after(deps, body), grid=(rows // tr,), out_shape=(jax.ShapeDtypeStruct((rows, cols), F32),) * 3,
        in_specs=[_ANY] * len(deps) + [spec] * 4, out_specs=[spec] * 3, compiler_params=_cparams(), name=name)(*deps, w, g, m, v)


def _adamw_small(items, name):
    n = len(items)

    def body(*refs):
        ins, outs = refs[:4 * n], refs[4 * n:]
        for k in range(n):
            w_ref, g_ref, m_ref, v_ref = ins[4 * k:4 * k + 4]
            d_ref, nm_ref, nv_ref = outs[3 * k:3 * k + 3]
            g_ = g_ref[...]
            m_ = ADAM_B1 * m_ref[...] + (1.0 - ADAM_B1) * g_
            v_ = ADAM_B2 * v_ref[...] + (1.0 - ADAM_B2) * (g_ * g_)
            nm_ref[...] = m_
            nv_ref[...] = v_
            m_hat = m_ / (1.0 - ADAM_B1 ** ADAM_STEP)
            v_hat = v_ / (1.0 - ADAM_B2 ** ADAM_STEP)
            d_ref[...] = -ADAM_LR * (m_hat / (jnp.sqrt(v_hat) + ADAM_EPS) + ADAM_WD * w_ref[...])

    out = pl.pallas_call(
        body, out_shape=tuple(jax.ShapeDtypeStruct(it[0].shape, F32) for it in items for _ in range(3)),
        in_specs=[_VMEM] * (4 * n), out_specs=[_VMEM] * (3 * n), name=name)(*[a for it in items for a in it])
    return [tuple(out[3 * k:3 * k + 3]) for k in range(n)]


def _adamw_halves(w, mine, other, m, v, c_idx, name, deps=()):
    r2, cols = mine.shape
    tr = next(t for t in (128, 64, 32, 16, 8) if r2 % t == 0)
    nh = r2 // tr

    def body(c_ref, w_ref, mine_ref, other_ref, m_ref, v_ref, g_ref, d_ref, nm_ref, nv_ref):
        g_ = jnp.where(pl.program_id(0) // nh == c_ref[0], mine_ref[...], other_ref[...])
        g_ref[...] = g_
        m_ = ADAM_B1 * m_ref[...] + (1.0 - ADAM_B1) * g_
        v_ = ADAM_B2 * v_ref[...] + (1.0 - ADAM_B2) * (g_ * g_)
        nm_ref[...] = m_
        nv_ref[...] = v_
        m_hat = m_ / (1.0 - ADAM_B1 ** ADAM_STEP)
        v_hat = v_ / (1.0 - ADAM_B2 ** ADAM_STEP)
        d_ref[...] = -ADAM_LR * (m_hat / (jnp.sqrt(v_hat) + ADAM_EPS) + ADAM_WD * w_ref[...])

    full = pl.BlockSpec((tr, cols), lambda i, c: (i, 0))
    mine_spec = pl.BlockSpec((tr, cols), lambda i, c: (jnp.clip(i - c[0] * nh, 0, nh - 1), 0))
    other_spec = pl.BlockSpec((tr, cols), lambda i, c: (jnp.clip(i - (1 - c[0]) * nh, 0, nh - 1), 0))
    return pl.pallas_call(
        lambda c_ref, *refs: body(c_ref, *refs[len(deps):]),
        grid_spec=pltpu.PrefetchScalarGridSpec(
            num_scalar_prefetch=1, grid=(2 * nh,),
            in_specs=[_ANY] * len(deps) + [full, mine_spec, other_spec, full, full], out_specs=[full] * 4),
        out_shape=(jax.ShapeDtypeStruct((2 * r2, cols), F32),) * 4, compiler_params=_cparams(), name=name,
    )(c_idx, *deps, w, mine, other, m, v)


def _add_halves(g, recv, c_idx, name):
    n, _, r2, cols = g.shape

    def body(c_ref, g_ref, r_ref, o_ref):
        o_ref[...] = (g_ref[...].astype(F32) + r_ref[...].astype(F32)).astype(BF16)

    return pl.pallas_call(
        body,
        grid_spec=pltpu.PrefetchScalarGridSpec(
            num_scalar_prefetch=1, grid=(n,),
            in_specs=[pl.BlockSpec((None, None, r2, cols), lambda k, c: (k, c[0], 0, 0)),
                      pl.BlockSpec((None, r2, cols), lambda k, c: (k, 0, 0))],
            out_specs=pl.BlockSpec((None, r2, cols), lambda k, c: (k, 0, 0))),
        out_shape=jax.ShapeDtypeStruct((n, r2, cols), BF16), compiler_params=_cparams(), name=name)(c_idx, g, recv)


def _sum_chips(own, others, chip_idx, name):
    _, r2, cols = own.shape
    tr = next(t for t in (128, 176, 64, 32, 16) if r2 % t == 0)

    def body(p_ref, own_ref, o_ref_in, o_ref):
        o_ref[...] = (((own_ref[...].astype(F32) + o_ref_in[0].astype(F32)) + o_ref_in[1].astype(F32))
                      + o_ref_in[2].astype(F32))

    return pl.pallas_call(
        body,
        grid_spec=pltpu.PrefetchScalarGridSpec(
            num_scalar_prefetch=1, grid=(r2 // tr,),
            in_specs=[pl.BlockSpec((None, tr, cols), lambda i, p: (p[0], i, 0)),
                      pl.BlockSpec((N_CHIPS - 1, tr, cols), lambda i, p: (0, i, 0))],
            out_specs=pl.BlockSpec((tr, cols), lambda i, p: (i, 0))),
        out_shape=jax.ShapeDtypeStruct((r2, cols), F32), compiler_params=_cparams(), name=name)(chip_idx, own, others)


def _place():
    x, y, c = lax.axis_index("x"), lax.axis_index("y"), lax.axis_index("c")
    return x, y, c, 2 * x + y


def _flip(v, bit):
    return 1 - v if bit else v


def _allgather8(v, name, deps=()):
    r, n = v.shape

    def body(*refs):
        v_ref, out_ref, send_sems, recv_sems, local_sem = refs[len(deps):]
        x, y, c, _ = _place()
        me = 4 * x + 2 * y + c
        mine = pltpu.make_async_copy(v_ref, out_ref.at[me], local_sem)
        mine.start()
        sends = []
        for rel in range(1, N_DEV):
            peer = (_flip(x, rel & 4), _flip(y, rel & 2), _flip(c, rel & 1))
            cp = pltpu.make_async_remote_copy(v_ref, out_ref.at[me], send_sems.at[rel - 1], recv_sems.at[rel - 1],
                                              device_id=peer, device_id_type=MESH)
            cp.start()
            sends.append(cp)
        for rel in range(1, N_DEV):
            peer = (_flip(x, rel & 4), _flip(y, rel & 2), _flip(c, rel & 1))
            peer_idx = 4 * peer[0] + 2 * peer[1] + peer[2]
            pltpu.make_async_remote_copy(v_ref, out_ref.at[peer_idx], send_sems.at[rel - 1], recv_sems.at[rel - 1],
                                         device_id=peer, device_id_type=MESH).wait_recv()
        for cp in sends:
            cp.wait_send()
        mine.wait()

    return pl.pallas_call(
        body, out_shape=jax.ShapeDtypeStruct((N_DEV, r, n), F32), in_specs=[_ANY] * len(deps) + [_VMEM], out_specs=_VMEM,
        scratch_shapes=[pltpu.SemaphoreType.DMA((N_DEV - 1,)), pltpu.SemaphoreType.DMA((N_DEV - 1,)), pltpu.SemaphoreType.DMA(())],
        name=name)(*deps, v)


_HBM = pl.BlockSpec(memory_space=pltpu.HBM)
_SEM = pl.BlockSpec(memory_space=pltpu.SEMAPHORE)
_EFFECT = pltpu.SideEffectType.DATAFLOW_SIDE_EFFECTING


def _xchg_start(name, plan, n_copies, srcs, lands, after=()):
    bufs = list(srcs) + list(lands)
    ns, nb = len(srcs), len(srcs) + len(lands)

    def body(*refs):
        send_sems, recv_sems, token = refs[nb + len(after)], refs[nb + len(after) + 1], refs[-1]
        for i, (src, dst, peer, _) in enumerate(plan(_place(), refs[:ns], refs[ns:nb])):
            pltpu.make_async_remote_copy(src, dst, send_sems.at[i], recv_sems.at[i], device_id=peer, device_id_type=MESH).start()
        token[...] = jnp.zeros_like(token)

    out = pl.pallas_call(
        body, name=name,
        out_shape=(pltpu.SemaphoreType.DMA((n_copies,)), pltpu.SemaphoreType.DMA((n_copies,)),
                   *[pltpu.HBM(a.shape, a.dtype) for a in bufs], jax.ShapeDtypeStruct((8, 128), F32)),
        in_specs=[_HBM] * nb + [_ANY] * len(after), out_specs=(_SEM, _SEM, *[_HBM] * nb, _VMEM),
        input_output_aliases={i: 2 + i for i in range(nb)},
        compiler_params=pltpu.CompilerParams(has_side_effects=_EFFECT),
    )(*[pltpu.with_memory_space_constraint(a, pltpu.HBM) for a in bufs], *after)
    return (out[0], out[1]), out[2:2 + ns], out[2 + ns:2 + nb], out[-1]


def _xchg_wait(name, plan, sems, srcs, lands, after, sem_ids=None):
    bufs = list(srcs) + list(lands)
    ns, nb = len(srcs), len(srcs) + len(lands)

    def body(*refs):
        send_sems, recv_sems = refs[nb], refs[nb + 1]
        copies = plan(_place(), refs[:ns], refs[ns:nb])
        ids = range(len(copies)) if sem_ids is None else sem_ids
        for i, (src, _, peer, mine) in zip(ids, copies, strict=True):
            if i is not None:
                cp = pltpu.make_async_remote_copy(src, mine, send_sems.at[i], recv_sems.at[i], device_id=peer,
                                                  device_id_type=MESH)
                cp.wait_send()
                cp.wait_recv()

    out = pl.pallas_call(
        body, name=name, out_shape=tuple(pltpu.HBM(a.shape, a.dtype) for a in bufs),
        in_specs=[_HBM] * nb + [_SEM, _SEM] + [_ANY] * len(after), out_specs=tuple([_HBM] * nb),
        input_output_aliases={i: i for i in range(nb)},
        compiler_params=pltpu.CompilerParams(has_side_effects=_EFFECT),
    )(*bufs, *sems, *after)
    return out[:ns], out[ns:]


def _other_chips(place, which=(1, 2, 3)):
    x, y, c, _ = place
    return [((_flip(x, j & 2), _flip(y, j & 1), c), 2 * _flip(x, j & 2) + _flip(y, j & 1)) for j in which]


def _plan_gather_ici(chips):
    def plan(place, src_refs, land_refs):
        _, _, c, p = place
        return [(s.at[c], l.at[p, c], peer, l.at[q, c]) for s, l, which in zip(src_refs, land_refs, chips, strict=True)
                for peer, q in _other_chips(place, which)]
    return plan


def _plan_relay(which):
    def plan(place, src_refs, land_refs):
        x, y, c, _ = place
        return [(l.at[q, c], l.at[q, c], (x, y, 1 - c), l.at[q, 1 - c]) for l in land_refs for _, q in _other_chips(place, which)]
    return plan


def _plan_swap(place, src_refs, land_refs):
    x, y, c, _ = place
    return [(s.at[k, 1 - c], l.at[k], (x, y, 1 - c), l.at[k]) for s, l in zip(src_refs, land_refs) for k in range(N_CHIPS)]


def _plan_scatter(place, src_refs, land_refs):
    return [(s.at[q], l.at[j], peer, l.at[j]) for s, l in zip(src_refs, land_refs)
            for j, (peer, q) in enumerate(_other_chips(place))]


def _plan_share(place, src_refs, land_refs):
    x, y, c, _ = place
    return [(s, l, (x, y, 1 - c), l) for s, l in zip(src_refs, land_refs)]


def _pack_rows(parts, n_rows, name, deps=()):
    def body(*refs):
        refs = refs[len(deps):]
        out_ref = refs[-1]
        out_ref[...] = jnp.zeros((n_rows, D), F32)
        at = 0
        for ref in refs[:-1]:
            k = ref.shape[0]
            out_ref[at:at + k, :] = ref[...]
            at += k

    return pl.pallas_call(
        body, out_shape=jax.ShapeDtypeStruct((n_rows, D), F32), in_specs=[_ANY] * len(deps) + [_VMEM] * len(parts),
        out_specs=_VMEM, name=name)(*deps, *parts)


TS_MM = 512
TS_IN = 1024
TS_GW = 1024
TS_MIX = 256


def _halved(a):
    n, r, cols = a.shape
    return a.reshape(n, 2, r // 2, cols)


def _rs_swap(name, grads, after=()):
    lands = [lax.empty((N_CHIPS,) + g.shape[2:], g.dtype) for g in grads]
    sems, grads, lands, token = _xchg_start(name + "_swap", _plan_swap, N_CHIPS * len(grads), grads, lands, after)
    return name, sems, grads, lands, token


def _rs_scatter(handle, after, chip, ci):
    name, sems, grads, lands, _ = handle
    grads, from_sibling = _xchg_wait(name + "_swap_wait", _plan_swap, sems, grads, lands, after)
    c_arr = jnp.reshape(ci, (1,)).astype(jnp.int32)
    pair_sums = [_add_halves(g, r, c_arr, "%s_add_halves_%d" % (name, k)) for k, (g, r) in enumerate(zip(grads, from_sibling))]
    lands = [lax.empty((N_CHIPS - 1,) + p.shape[1:], p.dtype) for p in pair_sums]
    sems, pair_sums, lands, token = _xchg_start(name + "_scatter", _plan_scatter, 3 * len(pair_sums), pair_sums, lands)
    return name, sems, pair_sums, lands, jnp.reshape(chip, (1,)).astype(jnp.int32), token


def _rs_share(handle, after):
    name, sems, pair_sums, lands, chip_idx, _ = handle
    pair_sums, received = _xchg_wait(name + "_scatter_wait", _plan_scatter, sems, pair_sums, lands, after)
    halves = [_sum_chips(p, r, chip_idx, "%s_sum_chips_%d" % (name, k)) for k, (p, r) in enumerate(zip(pair_sums, received))]
    lands = [lax.empty(h.shape, h.dtype) for h in halves]
    sems, halves, lands, token = _xchg_start(name + "_share", _plan_share, len(halves), halves, lands)
    return name, sems, halves, lands, token


def _rs_end(handle, after):
    name, sems, halves, lands, _ = handle
    halves, others = _xchg_wait(name + "_share_wait", _plan_share, sems, halves, lands, after)
    return list(zip(halves, others))


def kernel(x, c, w_ada, b_ada, g_norm_mix, w_in, conv_a_w, conv_b_w, conv_b_bias, w_rg_a, b_rg_a, w_rg_x, b_rg_x, lru_lambda, w_out, g_norm_ffn, w_gate_up, w_down, g_norm_final, loss_target, m_w_ada, m_b_ada, m_g_norm_mix, m_w_in, m_conv_a_w, m_conv_b_w, m_conv_b_bias, m_w_rg_a, m_b_rg_a, m_w_rg_x, m_b_rg_x, m_lru_lambda, m_w_out, m_g_norm_ffn, m_w_gate_up, m_w_down, m_g_norm_final, v_w_ada, v_b_ada, v_g_norm_mix, v_w_in, v_conv_a_w, v_conv_b_w, v_conv_b_bias, v_w_rg_a, v_b_rg_a, v_w_rg_x, v_b_rg_x, v_lru_lambda, v_w_out, v_g_norm_ffn, v_w_gate_up, v_w_down, v_g_norm_final):
    xi, yi, ci = lax.axis_index("x"), lax.axis_index("y"), lax.axis_index("c")
    chip = 2 * xi + yi
    me = 2 * chip + ci
    n_ada = w_ada.shape[2]

    def widen(w):
        return jnp.pad(w, ((0, 0), (0, D - w.shape[1])))

    got = _allgather8(_pack_rows([c, widen(conv_a_w[0]), widen(conv_b_w[0])], 8, "pack_c_conv"), "gather_c_conv")
    c_all = got[:, 0, :]
    conv_full = got[::2, 1:8, :D // N_CHIPS].transpose(1, 0, 2).reshape(7, D)

    mod_part, c_act = _ada_fwd(c_all, w_ada[0], lax.dynamic_slice_in_dim(b_ada, chip * n_ada, n_ada, axis=1))
    mod_all = _allgather8(mod_part, "gather_mod")
    mod_mine = lax.dynamic_index_in_dim(mod_all, me, axis=1, keepdims=False)[::2].reshape(6, D)
    vecs = _pack_rows([mod_mine, g_norm_mix, g_norm_ffn, g_norm_final.reshape(1, D), conv_b_bias, b_rg_a, b_rg_x, lru_lambda,
                       conv_full], N_VEC, "pack_vecs")

    def rg_shard(w):
        return w[0].astype(BF16).reshape(2, HEADS * HB // N_CHIPS // 2, HB)

    shards = [w_in[0].astype(BF16).reshape(2, D // 2, C_IN), rg_shard(w_rg_a), rg_shard(w_rg_x),
              w_out[0].astype(BF16).reshape(2, D // N_CHIPS // 2, D), w_gate_up[0].astype(BF16).reshape(2, D // 2, C_GU),
              w_down[0].astype(BF16).reshape(2, D_FF // N_CHIPS // 2, D)]
    lands = [lax.dynamic_update_index_in_dim(lax.empty((N_CHIPS,) + s.shape, s.dtype), s, chip, 0) for s in shards]

    def send(name, first, last, after, chips):
        copies = [(k, j) for k, which in zip(range(first, last), chips, strict=True) for j in which]
        sems, srcs, zone, token = _xchg_start(name + "_ici", _plan_gather_ici(chips), len(copies), shards[first:last],
                                              lands[first:last], after)
        shards[first:last], lands[first:last] = srcs, zone
        return sems, copies, token

    def arrive(name, sent, first, last, after):
        sems, copies, _ = sent
        chips = [tuple(j for k, j in copies if k == want) for want in range(first, last)]
        ids = [copies.index((k, j)) for k, which in zip(range(first, last), chips) for j in which]
        srcs, zone = _xchg_wait(name + "_ici_wait", _plan_gather_ici(chips), sems, shards[first:last], lands[first:last], after,
                                ids)
        shards[first:last], lands[first:last] = srcs, zone

    def relay(name, first, last, which):
        plan = _plan_relay(which)
        sems, _, zone, token = _xchg_start(name + "_d2d", plan, len(which) * (last - first), [], lands[first:last])
        lands[first:last] = zone
        return name, plan, sems, first, last, token

    def relayed(handle, after):
        name, plan, sems, first, last, _ = handle
        lands[first:last] = _xchg_wait(name + "_d2d_wait", plan, sems, [], lands[first:last], after)[1]

    def to_blocks(v):
        return v.reshape(-1, TS_MIX // TIME_BLOCKS, TIME_BLOCKS, D).transpose(0, 2, 1, 3).reshape(v.shape)

    def from_blocks(v):
        return v.reshape(-1, TIME_BLOCKS, TS_MIX // TIME_BLOCKS, D).transpose(0, 2, 1, 3).reshape(v.shape)

    def chip_index(j):
        return jnp.reshape(chip ^ j, (1,)).astype(jnp.int32)

    def wg_in():
        return lands[0].reshape(N_CHIPS, D, C_IN)

    xs, target = to_blocks(x[0]), to_blocks(loss_target[0])
    sent_near = send("gather_in_near", 0, 1, [vecs], [(1, 2)])
    ts_in = min(TS_IN, xs.shape[0])
    h1, proj = _fwd_in_first(xs, vecs, wg_in(), chip_index(0), ts_in, deps=[sent_near[-1]])
    arrive("gather_in_near", sent_near, 0, 1, [proj])
    near = relay("gather_in_near", 0, 1, (1, 2))
    sent_rest = send("gather_rest", 0, 6, [near[-1]], [(3,)] + [(1, 2, 3)] * 5)
    relayed(near, [sent_rest[-1]])
    proj = _fwd_in_more(h1, wg_in(), proj, chip_index(1), ts_in, "fwd_in_y")
    proj = _fwd_in_more(h1, wg_in(), proj, chip_index(2), ts_in, "fwd_in_x")
    arrive("gather_in_far", sent_rest, 0, 1, [proj])
    far = relay("gather_in_far", 0, 1, (3,))
    arrive("gather_mix", sent_rest, 1, 4, [far[-1]])
    relayed(far, [far[-1]])
    mix = relay("gather_mix", 1, 4, (1, 2, 3))
    proj = _fwd_in_more(h1, wg_in(), proj, chip_index(3), ts_in, "fwd_in_xy", deps=[mix[-1]])
    relayed(mix, [proj])
    wg_rga, wg_rgx, wg_out = lands[1:4]
    wg_out = wg_out.reshape(D, D)

    def rg_full(wg):
        return wg.reshape(N_CHIPS, HEADS, HB // N_CHIPS, HB).transpose(1, 0, 2, 3).reshape(HEADS, HB, HB)

    wg_rga, wg_rgx = rg_full(wg_rga), rg_full(wg_rgx)

    arrive("gather_ffn", sent_rest, 4, 6, [proj])
    ffn = relay("gather_ffn", 4, 6, (1, 2, 3))
    x1, merged, z1, kept, decay = _fwd_mix(proj, xs, vecs, wg_rga, wg_rgx, wg_out, TS_MIX, deps=[ffn[-1]])
    relayed(ffn, [x1])
    wg_gu, wg_dn = lands[4:6]
    wg_gu, wg_dn = wg_gu.reshape(N_CHIPS, D, C_GU), wg_dn.reshape(D_FF, D)
    dx1, h2, act, dz2, dgu, sm_ffn = _ffn_loss(x1, target, vecs, wg_gu, wg_dn, TS_MIX)

    def rg_chunks(dw):
        return _halved(dw.reshape(HEADS, N_CHIPS, HB // N_CHIPS, HB).transpose(1, 0, 2, 3).reshape(N_CHIPS, HB, HB).astype(BF16))

    ts_gw = min(TS_GW, xs.shape[0])
    g_dn = _grad_w(act, dz2, 1, ts_gw, "grad_w_down")
    g_gu = _grad_w(h2, dgu, N_CHIPS, ts_gw, "grad_w_gate_up")
    rs_b = _rs_swap("rs_b", [_halved(g_gu), _halved(g_dn.reshape(N_CHIPS, D_FF // N_CHIPS, D))])
    dproj, sm_mix, dw_rga, dw_rgx, dw_out = _bwd_mix(dx1, z1, merged, proj, kept, decay, vecs, wg_rga, wg_rgx, wg_out, sm_ffn, TS_MIX,
                                                     deps=[rs_b[-1]])
    rs_b = _rs_scatter(rs_b, [dproj], chip, ci)
    g_in = _grad_w(h1, dproj, N_CHIPS, ts_gw, "grad_w_in", deps=[rs_b[-1]])
    rs_b = _rs_share(rs_b, [g_in])
    rs_a = _rs_swap("rs_a", [_halved(g_in), rg_chunks(dw_rga), rg_chunks(dw_rgx),
                             _halved(dw_out.astype(BF16).reshape(N_CHIPS, D // N_CHIPS, D))], after=[rs_b[-1]])

    c_arr = jnp.reshape(ci, (1,)).astype(jnp.int32)

    def step(name, w, g, m, v, deps=()):
        shape = w.shape
        two_d = (-1, shape[-1])
        d, nm, nv = _adamw(w.reshape(two_d), g.reshape(two_d), m.reshape(two_d), v.reshape(two_d), "adamw_" + name, deps)
        return g.reshape(shape), d.reshape(shape), nm.reshape(shape), nv.reshape(shape)

    def step_halves(name, w, halves, m, v, deps=()):
        shape = w.shape
        two_d = (-1, shape[-1])
        out = _adamw_halves(w.reshape(two_d), halves[0], halves[1], m.reshape(two_d), v.reshape(two_d), c_arr, "adamw_" + name, deps)
        return tuple(a.reshape(shape) for a in out)

    def shard_cols(row_block):
        return lax.dynamic_slice_in_dim(row_block, chip * (D // N_CHIPS), D // N_CHIPS, axis=1)

    gw_gu, gw_dn = _rs_end(rs_b, [rs_a[-1]])
    res = {
        "w_gate_up": step_halves("w_gate_up", w_gate_up, gw_gu, m_w_gate_up, v_w_gate_up, [rs_a[-1]]),
        "w_down": step_halves("w_down", w_down, gw_dn, m_w_down, v_w_down, [rs_a[-1]]),
    }
    rs_a = _rs_scatter(rs_a, [res["w_gate_up"][1], res["w_down"][1]], chip, ci)
    grad_x, sm_in = _bwd_in(dproj, xs, dx1, vecs, wg_in(), sm_mix, TS_MM, deps=[rs_a[-1]])
    rs_a = _rs_share(rs_a, [grad_x])

    small, per_dev = _sum_small(_allgather8(sm_in, "gather_small", deps=[rs_a[-1]]))
    dmod_all = per_dev[:, 0:6, :].reshape(N_DEV, 6 * D)
    grad_w_ada = _ada_bwd(c_act, lax.dynamic_slice_in_dim(dmod_all, chip * n_ada, n_ada, axis=1))
    grad_b_ada = small[0:6].reshape(1, 6 * D)
    res["w_ada"] = step("w_ada", w_ada, grad_w_ada[None], m_w_ada, v_w_ada)
    small_sets = {
        "b_ada": (b_ada.reshape(6, D), grad_b_ada.reshape(6, D), m_b_ada.reshape(6, D), v_b_ada.reshape(6, D)),
        "g_norm_mix": (g_norm_mix, small[G_GMIX:G_GMIX + 1], m_g_norm_mix, v_g_norm_mix),
        "conv_a_w": (conv_a_w[0], shard_cols(small[G_WA0:G_WA0 + 3]), m_conv_a_w[0], v_conv_a_w[0]),
        "conv_b_w": (conv_b_w[0], shard_cols(small[G_WB0:G_WB0 + 4]), m_conv_b_w[0], v_conv_b_w[0]),
        "conv_b_bias": (conv_b_bias, small[G_CBB:G_CBB + 1], m_conv_b_bias, v_conv_b_bias),
        "b_rg_a": (b_rg_a, small[G_BA:G_BA + 1], m_b_rg_a, v_b_rg_a),
        "b_rg_x": (b_rg_x, small[G_BX:G_BX + 1], m_b_rg_x, v_b_rg_x),
        "lru_lambda": (lru_lambda, small[G_LAM:G_LAM + 1], m_lru_lambda, v_lru_lambda),
        "g_norm_ffn": (g_norm_ffn, small[G_GFFN:G_GFFN + 1], m_g_norm_ffn, v_g_norm_ffn),
        "g_norm_final": (g_norm_final.reshape(1, D), small[G_GFIN:G_GFIN + 1], m_g_norm_final.reshape(1, D),
                         v_g_norm_final.reshape(1, D)),
    }
    stepped = _adamw_small(list(small_sets.values()), "adamw_small")
    for (n, (w_, g_, _, _)), (d_, nm_, nv_) in zip(small_sets.items(), stepped):
        shape = (1,) + w_.shape if n.startswith("conv_") and n != "conv_b_bias" else w_.shape
        res[n] = tuple(a.reshape(shape) for a in (g_, d_, nm_, nv_))
    gw_in, gw_rga, gw_rgx, gw_out = _rs_end(rs_a, [res[n][1] for n in res])
    res["w_in"] = step_halves("w_in", w_in, gw_in, m_w_in, v_w_in)
    res["w_rg_a"] = step_halves("w_rg_a", w_rg_a, gw_rga, m_w_rg_a, v_w_rg_a)
    res["w_rg_x"] = step_halves("w_rg_x", w_rg_x, gw_rgx, m_w_rg_x, v_w_rg_x)
    res["w_out"] = step_halves("w_out", w_out, gw_out, m_w_out, v_w_out)
    res["b_ada"] = tuple(a.reshape(1, 6 * D) for a in res["b_ada"])
    res["g_norm_final"] = tuple(a.reshape(D) for a in res["g_norm_final"])
    names = ["w_ada", "b_ada", "g_norm_mix", "w_in", "conv_a_w", "conv_b_w", "conv_b_bias", "w_rg_a", "b_rg_a", "w_rg_x",
             "b_rg_x", "lru_lambda", "w_out", "g_norm_ffn", "w_gate_up", "w_down", "g_norm_final"]
    loss = jnp.sum(small[G_LOSS])
    return (loss, from_blocks(grad_x)[None], *[res[n][0] for n in names], *[res[n][1] for n in names],
            *[res[n][2] for n in names], *[res[n][3] for n in names])
```

```python
import functools

import jax
import jax.numpy as jnp
from jax import lax
from jax.experimental import pallas as pl
from jax.experimental.pallas import tpu as pltpu

F32 = jnp.float32
BF16 = jnp.bfloat16
MESH = pl.DeviceIdType.MESH

D = 1024
N_CHIPS = 4
N_DEV = 8
D_IN = 7 * D
C_IN = D_IN // N_CHIPS
D_FF = 2816
C_GU = 2 * D_FF // N_CHIPS
HEADS = 4
HB = D // HEADS
EPS = 1e-6
LRU_C = 8.0
ADAM_LR, ADAM_B1, ADAM_B2, ADAM_EPS, ADAM_WD, ADAM_STEP = 0.001, 0.9, 0.999, 1e-08, 0.01, 10
VMEM_LIMIT = 56 << 20

(V_SH1, V_SC1, V_GT1, V_SH2, V_SC2, V_GT2, V_GMIX, V_GFFN, V_GFIN, V_CBB, V_BA, V_BX, V_LAM,
 V_WA0, V_WA1, V_WA2, V_WB0, V_WB1, V_WB2, V_WB3) = range(20)
N_VEC = 24
(G_SH1, G_SC1, G_GT1, G_SH2, G_SC2, G_GT2, G_GMIX, G_CBB, G_BA, G_BX, G_LAM, G_GFFN, G_GFIN,
 G_WA0, G_WA1, G_WA2, G_WB0, G_WB1, G_WB2, G_WB3, G_LOSS) = range(21)
N_SMALL = 24

_VMEM = pl.BlockSpec(memory_space=pltpu.VMEM)
_ANY = pl.BlockSpec(memory_space=pl.ANY)


def _cparams(n_grid=1):
    return pltpu.CompilerParams(dimension_semantics=("arbitrary",) * n_grid, vmem_limit_bytes=VMEM_LIMIT)


def _after(deps, body):
    n = len(deps)
    return lambda *refs: body(*refs[n:])


def _rms(x):
    rstd = lax.rsqrt(jnp.mean(x * x, axis=-1, keepdims=True) + EPS)
    return x * rstd, rstd


def _rms_bwd(dxhat, xhat, rstd):
    return rstd * (dxhat - xhat * jnp.mean(dxhat * xhat, axis=-1, keepdims=True))


def _rowsum(v):
    return jnp.sum(v, axis=0, keepdims=True)


def _dot(a, b):
    return jnp.dot(a, b, preferred_element_type=F32)


def _dot_nt(a, b):
    return lax.dot_general(a, b, (((1,), (1,)), ((), ())), preferred_element_type=F32)


def _dot_tn(a, b):
    return lax.dot_general(a, b, (((0,), (0,)), ((), ())), preferred_element_type=F32)


def _gelu(x):
    k, c = 0.7978845608028654, 0.044715
    t = jnp.tanh(k * (x + c * x * x * x))
    return 0.5 * x * (1.0 + t), 0.5 * (1.0 + t) + 0.5 * x * (1.0 - t * t) * k * (1.0 + 3.0 * c * x * x)


def _log_sigmoid(lam):
    return jnp.minimum(lam, 0.0) - jnp.log1p(jnp.exp(-jnp.abs(lam)))


def _lru_gates(u, wa_ref, wx_ref, v_ref, row0):
    ub = u.astype(BF16)
    pre_a = jnp.concatenate([_dot(ub[:, h * HB:(h + 1) * HB], wa_ref[h]) for h in range(HEADS)], axis=1)
    pre_x = jnp.concatenate([_dot(ub[:, h * HB:(h + 1) * HB], wx_ref[h]) for h in range(HEADS)], axis=1)
    r = jax.nn.sigmoid(pre_a + v_ref[V_BA:V_BA + 1, :])
    ig = jax.nn.sigmoid(pre_x + v_ref[V_BX:V_BX + 1, :])
    log_a = LRU_C * r * _log_sigmoid(v_ref[V_LAM:V_LAM + 1, :])
    a = jnp.exp(log_a)
    x2 = 2.0 * log_a
    m2 = jnp.where(x2 > -0.03, -x2 * (1.0 + x2 * (0.5 + x2 * (1.0 / 6.0 + x2 * (1.0 / 24.0)))), 1.0 - a * a)
    mult = jnp.where(row0, 1.0, jnp.sqrt(jnp.maximum(m2, 0.0)))
    return r, ig, a, mult


TIME_BLOCKS = 8
N_KEPT = 10


def _late_blocks(v, buf, g, halo=None):
    n = buf.shape[0]
    out = []
    for idx in range(n):
        k = TIME_BLOCKS - n + idx
        buf[idx, 8:g + 8, :] = v[k * g:(k + 1) * g]
        if halo is not None:
            buf[idx, 7:8, :] = halo[idx]
        out.append(buf[idx, pl.ds(7, g), :])
        if halo is None:
            buf[idx, 7:8, :] = buf[idx, g + 7:g + 8, :]
    return out


def _earlier(v, s, late, g):
    return jnp.concatenate(late[len(late) - s:] + [v[0:(TIME_BLOCKS - s) * g]], axis=0)


def _early_blocks(v, buf, g):
    out = []
    for k in range(buf.shape[0]):
        buf[k, 0:g, :] = v[k * g:(k + 1) * g]
        out.append(buf[k, pl.ds(1, g), :])
        buf[k, g:g + 1, :] = buf[k, 0:1, :]
    return out


def _later(v, s, early, g):
    return jnp.concatenate([v[s * g:]] + early[0:s], axis=0)


def _fwd_in_first(x, vecs, w_in_g, q_idx, ts, deps=()):
    s = x.shape[0]

    def body(q_ref, x_ref, v_ref, w_ref, h1_ref, proj_ref):
        xhat, _ = _rms(x_ref[...])
        h = xhat * v_ref[V_GMIX:V_GMIX + 1, :] * (1.0 + v_ref[V_SC1:V_SC1 + 1, :]) + v_ref[V_SH1:V_SH1 + 1, :]
        hb = h.astype(BF16)
        h1_ref[...] = hb
        proj_ref[...] = _dot(hb, w_ref[...]).astype(BF16)

    return pl.pallas_call(
        lambda q_ref, *refs: body(q_ref, *refs[len(deps):]),
        grid_spec=pltpu.PrefetchScalarGridSpec(
            num_scalar_prefetch=1, grid=(s // ts,),
            in_specs=[_ANY] * len(deps) + [pl.BlockSpec((ts, D), lambda i, q: (i, 0)), _VMEM,
                                           pl.BlockSpec((None, D, C_IN), lambda i, q: (q[0], 0, 0))],
            out_specs=[pl.BlockSpec((ts, D), lambda i, q: (i, 0)), pl.BlockSpec((ts, C_IN), lambda i, q: (i, q[0]))]),
        out_shape=(jax.ShapeDtypeStruct((s, D), BF16), jax.ShapeDtypeStruct((s, D_IN), BF16)),
        compiler_params=_cparams(), name="fwd_in_own")(q_idx, *deps, x, vecs, w_in_g)


def _fwd_in_more(h1, w_in_g, proj, q_idx, ts, name, deps=()):
    s = h1.shape[0]

    def body(q_ref, h1_ref, w_ref, proj_in_ref, proj_ref):
        proj_ref[...] = _dot(h1_ref[...], w_ref[...]).astype(BF16)

    return pl.pallas_call(
        lambda q_ref, *refs: body(q_ref, *refs[len(deps):]),
        grid_spec=pltpu.PrefetchScalarGridSpec(
            num_scalar_prefetch=1, grid=(s // ts,),
            in_specs=[_ANY] * len(deps) + [pl.BlockSpec((ts, D), lambda i, q: (i, 0)),
                                           pl.BlockSpec((None, D, C_IN), lambda i, q: (q[0], 0, 0)), _ANY],
            out_specs=pl.BlockSpec((ts, C_IN), lambda i, q: (i, q[0]))),
        out_shape=jax.ShapeDtypeStruct((s, D_IN), BF16), input_output_aliases={len(deps) + 3: 0},
        compiler_params=_cparams(), name=name)(q_idx, *deps, h1, w_in_g, proj)


def _fwd_mix(proj, x, vecs, w_rga, w_rgx, w_out, ts, deps=()):
    s = x.shape[0]
    g = ts // TIME_BLOCKS

    def body(proj_ref, x_ref, v_ref, wa_ref, wx_ref, wo_ref, x1_ref, mg_ref, z1_ref, kept_ref, decay_ref,
             ua_buf, rx_buf, p_buf, q_buf, c_buf, hcarry):
        i = pl.program_id(0)

        @pl.when(i == 0)
        def _():
            ua_buf[...] = jnp.zeros(ua_buf.shape, F32)
            rx_buf[...] = jnp.zeros(rx_buf.shape, F32)
            hcarry[...] = jnp.zeros((8, D), F32)

        def seg(j):
            return proj_ref[:, j * D:(j + 1) * D].astype(F32)

        def vrow(j):
            return v_ref[j:j + 1, :]

        cb, cc, cx, rx, rg, ga, gb = (seg(j) for j in range(7))
        ua = cc * cx
        ua_late = _late_blocks(ua, ua_buf, g)
        rx_late = _late_blocks(rx, rx_buf, g)
        va = vrow(V_WA2) * ua + vrow(V_WA1) * _earlier(ua, 1, ua_late, g) + vrow(V_WA0) * _earlier(ua, 2, ua_late, g)
        u = (vrow(V_WB3) * rx + vrow(V_WB2) * _earlier(rx, 1, rx_late, g) + vrow(V_WB1) * _earlier(rx, 2, rx_late, g)
             + vrow(V_WB0) * _earlier(rx, 3, rx_late, g) + vrow(V_CBB))

        rows = lax.broadcasted_iota(jnp.int32, (ts, D), 0)
        row0 = jnp.logical_and(rows == 0, i == 0)
        r, ig, a, mult = _lru_gates(u, wa_ref, wx_ref, v_ref, row0)
        decay_ref[...] = a
        bx = mult * (ig * u)

        prods, sums = [a[0:g]], [bx[0:g]]
        for k in range(1, TIME_BLOCKS):
            ak = a[k * g:(k + 1) * g]
            sums.append(ak * sums[-1] + bx[k * g:(k + 1) * g])
            prods.append(ak * prods[-1])
        p_buf[...] = prods[-1]
        q_buf[...] = sums[-1]
        state = hcarry[0:1, :]
        for j in range(g):
            c_buf[j:j + 1, :] = state
            state = p_buf[j:j + 1, :] * state + q_buf[j:j + 1, :]
        hcarry[0:1, :] = state
        entering = c_buf[...]
        h = jnp.concatenate([sums[k] + prods[k] * entering for k in range(TIME_BLOCKS)], axis=0)

        gel, dgel = _gelu(rg)
        sga = jax.nn.sigmoid(ga)
        sgb = jax.nn.sigmoid(gb)
        for j, keep in enumerate((va, r, ig, sga, sgb, gel, dgel, mult, u, h)):
            kept_ref[:, j * D:(j + 1) * D] = keep.astype(BF16)
        merged = (sga * (cb * va) + sgb * (h * gel)).astype(BF16)
        mg_ref[...] = merged
        z1 = _dot(merged, wo_ref[...])
        z1_ref[...] = z1.astype(BF16)
        x1_ref[...] = x_ref[...] + vrow(V_GT1) * z1

    row = lambda i: (i, 0)
    return pl.pallas_call(
        _after(deps, body), grid=(s // ts,),
        out_shape=(jax.ShapeDtypeStruct((s, D), F32), jax.ShapeDtypeStruct((s, D), BF16), jax.ShapeDtypeStruct((s, D), BF16),
                   jax.ShapeDtypeStruct((s, N_KEPT * D), BF16), jax.ShapeDtypeStruct((s, D), F32)),
        in_specs=[_ANY] * len(deps) + [pl.BlockSpec((ts, D_IN), row), pl.BlockSpec((ts, D), row), _VMEM, _VMEM, _VMEM, _VMEM],
        out_specs=[pl.BlockSpec((ts, D), row)] * 3 + [pl.BlockSpec((ts, N_KEPT * D), row), pl.BlockSpec((ts, D), row)],
        scratch_shapes=[pltpu.VMEM((2, g + 8, D), F32), pltpu.VMEM((3, g + 8, D), F32), pltpu.VMEM((g, D), F32),
                        pltpu.VMEM((g, D), F32), pltpu.VMEM((g, D), F32), pltpu.VMEM((8, D), F32)],
        compiler_params=_cparams(), name="fwd_mix")(*deps, proj, x, vecs, w_rga, w_rgx, w_out)


def _ffn_loss(x1, target, vecs, w_gu_g, w_dn, ts):
    s = x1.shape[0]

    def body(x1_ref, t_ref, v_ref, wgu_ref, wdn_ref, dx1_ref, h2_ref, act_ref, dz2_ref, dgu_ref, sm_ref):
        @pl.when(pl.program_id(0) == 0)
        def _():
            sm_ref[...] = jnp.zeros((N_SMALL, D), F32)

        def vrow(j):
            return v_ref[j:j + 1, :]

        n_sub = 1
        rows = [slice(k * (ts // n_sub), (k + 1) * (ts // n_sub)) for k in range(n_sub)]
        subs = [dict(r=r, sums={}) for r in rows]

        def stage_norm(t):
            t["x1"] = x1_ref[t["r"], :]
            t["xh1"], t["rstd1"] = _rms(t["x1"])
            t["n2"] = t["xh1"] * vrow(V_GFFN)
            t["h2"] = (t["n2"] * (1.0 + vrow(V_SC2)) + vrow(V_SH2)).astype(BF16)
            h2_ref[t["r"], :] = t["h2"]

        def stage_up(t):
            h2 = t["h2"]
            g = jnp.concatenate([_dot(h2, wgu_ref[0]), _dot(h2, wgu_ref[1])], axis=1)
            t["up"] = jnp.concatenate([_dot(h2, wgu_ref[2]), _dot(h2, wgu_ref[3])], axis=1)
            t["g"] = g
            t["sg"] = jax.nn.sigmoid(g)
            t["silu"] = g * t["sg"]
            t["act"] = (t["silu"] * t["up"]).astype(BF16)
            act_ref[t["r"], :] = t["act"]

        def stage_down_loss(t):
            z2 = _dot(t["act"], wdn_ref[...])
            x2 = t["x1"] + vrow(V_GT2) * z2
            xh2, rstd2 = _rms(x2)
            err = xh2 * vrow(V_GFIN) - t_ref[t["r"], :]
            t["sums"][G_LOSS] = _rowsum((0.5 / D) * err * err)
            dy = err * (1.0 / D)
            t["sums"][G_GFIN] = _rowsum(dy * xh2)
            t["dx2"] = _rms_bwd(dy * vrow(V_GFIN), xh2, rstd2)
            t["sums"][G_GT2] = _rowsum(t["dx2"] * z2)
            t["dz2"] = (vrow(V_GT2) * t["dx2"]).astype(BF16)
            dz2_ref[t["r"], :] = t["dz2"]

        def stage_back_act(t):
            dact = _dot_nt(t["dz2"], wdn_ref[...])
            g, sg = t["g"], t["sg"]
            t["dgate"] = (dact * t["up"] * (sg * (1.0 + g * (1.0 - sg)))).astype(BF16)
            t["dup"] = (dact * t["silu"]).astype(BF16)
            dgu_ref[t["r"], 0:D_FF] = t["dgate"]
            dgu_ref[t["r"], D_FF:2 * D_FF] = t["dup"]

        def stage_back_norm(t):
            dgate, dup = t["dgate"], t["dup"]
            dh2 = (_dot_nt(dgate[:, 0:C_GU], wgu_ref[0]) + _dot_nt(dgate[:, C_GU:2 * C_GU], wgu_ref[1])
                   + _dot_nt(dup[:, 0:C_GU], wgu_ref[2]) + _dot_nt(dup[:, C_GU:2 * C_GU], wgu_ref[3]))
            t["sums"][G_SH2] = _rowsum(dh2)
            t["sums"][G_SC2] = _rowsum(dh2 * t["n2"])
            dn2 = dh2 * (1.0 + vrow(V_SC2))
            t["sums"][G_GFFN] = _rowsum(dn2 * t["xh1"])
            dx1_ref[t["r"], :] = t["dx2"] + _rms_bwd(dn2 * vrow(V_GFFN), t["xh1"], t["rstd1"])

        for stage in (stage_norm, stage_up, stage_down_loss, stage_back_act, stage_back_norm):
            for t in subs:
                stage(t)
        for j in subs[0]["sums"]:
            total = subs[0]["sums"][j]
            for t in subs[1:]:
                total = total + t["sums"][j]
            sm_ref[j:j + 1, :] += total

    row = lambda i: (i, 0)
    return pl.pallas_call(
        body, grid=(s // ts,),
        out_shape=(jax.ShapeDtypeStruct((s, D), F32), jax.ShapeDtypeStruct((s, D), BF16), jax.ShapeDtypeStruct((s, D_FF), BF16),
                   jax.ShapeDtypeStruct((s, D), BF16), jax.ShapeDtypeStruct((s, 2 * D_FF), BF16),
                   jax.ShapeDtypeStruct((N_SMALL, D), F32)),
        in_specs=[pl.BlockSpec((ts, D), row), pl.BlockSpec((ts, D), row), _VMEM, _VMEM, _VMEM],
        out_specs=[pl.BlockSpec((ts, D), row), pl.BlockSpec((ts, D), row), pl.BlockSpec((ts, D_FF), row),
                   pl.BlockSpec((ts, D), row), pl.BlockSpec((ts, 2 * D_FF), row), pl.BlockSpec((N_SMALL, D), lambda i: (0, 0))],
        compiler_params=_cparams(), name="ffn_loss")(x1, target, vecs, w_gu_g, w_dn)


def _bwd_mix(dx1, z1, merged, proj, kept, decay, vecs, w_rga, w_rgx, w_out, small, ts, deps=()):
    s = dx1.shape[0]
    nt = s // ts
    g = ts // TIME_BLOCKS
    assert g % 16 == 0

    def body(dx1_ref, z1_ref, mg_ref, proj_ref, kept_ref, decay_ref, hh_ref, v_ref, wa_ref, wx_ref,
             wo_ref, sm0_ref, dproj_ref, sm_ref, dwa_ref, dwx_ref, dwo_ref,
             h_buf, a_buf, dva_buf, du_buf, p_buf, q_buf, c_buf, lcarry):
        i = pl.program_id(0)
        first_tile = i == nt - 1

        @pl.when(i == 0)
        def _():
            a_buf[...] = jnp.zeros(a_buf.shape, F32)
            dva_buf[...] = jnp.zeros(dva_buf.shape, F32)
            du_buf[...] = jnp.zeros(du_buf.shape, F32)
            lcarry[...] = jnp.zeros((8, D), F32)
            sm_ref[...] = sm0_ref[...]
            dwa_ref[...] = jnp.zeros((HEADS, HB, HB), F32)
            dwx_ref[...] = jnp.zeros((HEADS, HB, HB), F32)
            dwo_ref[...] = jnp.zeros((D, D), F32)

        def seg(j):
            return proj_ref[:, j * D:(j + 1) * D].astype(F32)

        def vrow(j):
            return v_ref[j:j + 1, :]

        def acc(j, val):
            sm_ref[j:j + 1, :] += _rowsum(val)

        cb, cc, cx, rx = (seg(j) for j in range(4))
        ua = cc * cx
        va, r, ig, sga, sgb, gel, dgel, mult, u, h = (kept_ref[:, j * D:(j + 1) * D].astype(F32) for j in range(N_KEPT))
        a = decay_ref[...]
        rows = lax.broadcasted_iota(jnp.int32, (ts, D), 0)
        row0 = jnp.logical_and(rows == 0, first_tile)

        dx1 = dx1_ref[...]
        acc(G_GT1, dx1 * z1_ref[...].astype(F32))
        dz1 = (vrow(V_GT1) * dx1).astype(BF16)
        dwo_ref[...] += _dot_tn(mg_ref[...], dz1)
        dmg = _dot_nt(dz1, wo_ref[...])
        dya = dmg * sga
        dyb = dmg * sgb
        dproj_ref[:, 5 * D:6 * D] = (dya * (cb * va) * (1.0 - sga)).astype(BF16)
        dproj_ref[:, 6 * D:7 * D] = (dyb * (h * gel) * (1.0 - sgb)).astype(BF16)

        dproj_ref[:, 0:D] = (dya * va).astype(BF16)
        dva = dya * cb
        dva_early = _early_blocks(dva, dva_buf, g)
        dva1 = _later(dva, 1, dva_early, g)
        dva2 = _later(dva, 2, dva_early, g)
        dua = vrow(V_WA2) * dva + vrow(V_WA1) * dva1 + vrow(V_WA0) * dva2
        acc(G_WA2, ua * dva)
        acc(G_WA1, ua * dva1)
        acc(G_WA0, ua * dva2)
        dproj_ref[:, D:2 * D] = (dua * cx).astype(BF16)
        dproj_ref[:, 2 * D:3 * D] = (dua * cc).astype(BF16)

        dproj_ref[:, 4 * D:5 * D] = (dyb * h * dgel).astype(BF16)
        a_next = _later(a, 1, _early_blocks(a, a_buf, g), g)
        dh = dyb * gel
        last = TIME_BLOCKS - 1
        prods, sums = {last: a_next[last * g:]}, {last: dh[last * g:]}
        for k in range(last - 1, -1, -1):
            ak = a_next[k * g:(k + 1) * g]
            sums[k] = dh[k * g:(k + 1) * g] + ak * sums[k + 1]
            prods[k] = ak * prods[k + 1]
        p_buf[...] = prods[0]
        q_buf[...] = sums[0]
        state = lcarry[0:1, :]
        for j in range(g - 1, -1, -1):
            c_buf[j:j + 1, :] = state
            state = q_buf[j:j + 1, :] + p_buf[j:j + 1, :] * state
        lcarry[0:1, :] = state
        entering = c_buf[...]
        lam = jnp.concatenate([sums[k] + prods[k] * entering for k in range(TIME_BLOCKS)], axis=0)

        last = lax.broadcasted_iota(jnp.int32, hh_ref.shape, 0) == hh_ref.shape[0] - 1
        h_halo = [jnp.where(first_tile, 0.0, jnp.sum(jnp.where(last, hh_ref[...].astype(F32), 0.0), axis=0, keepdims=True))]
        da = lam * _earlier(h, 1, _late_blocks(h, h_buf, g, h_halo), g)
        dmult = jnp.where(row0, 0.0, lam * (ig * u))
        di = lam * mult * u
        du = lam * mult * ig
        dlog_a = da * a - dmult * (a * a) / mult
        lam_p = vrow(V_LAM)
        dr = dlog_a * (LRU_C * _log_sigmoid(lam_p))
        sm_ref[G_LAM:G_LAM + 1, :] += _rowsum(dlog_a * r) * (LRU_C * jax.nn.sigmoid(-lam_p))
        dpa = dr * r * (1.0 - r)
        dpx = di * ig * (1.0 - ig)
        acc(G_BA, dpa)
        acc(G_BX, dpx)
        dpab = dpa.astype(BF16)
        dpxb = dpx.astype(BF16)
        ub = u.astype(BF16)
        back = []
        for hd in range(HEADS):
            cols = slice(hd * HB, (hd + 1) * HB)
            back.append(_dot_nt(dpab[:, cols], wa_ref[hd]) + _dot_nt(dpxb[:, cols], wx_ref[hd]))
            dwa_ref[hd] += _dot_tn(ub[:, cols], dpab[:, cols])
            dwx_ref[hd] += _dot_tn(ub[:, cols], dpxb[:, cols])
        du = du + jnp.concatenate(back, axis=1)

        acc(G_CBB, du)
        du_early = _early_blocks(du, du_buf, g)
        du1 = _later(du, 1, du_early, g)
        du2 = _later(du, 2, du_early, g)
        du3 = _later(du, 3, du_early, g)
        dproj_ref[:, 3 * D:4 * D] = (vrow(V_WB3) * du + vrow(V_WB2) * du1 + vrow(V_WB1) * du2 + vrow(V_WB0) * du3).astype(BF16)
        acc(G_WB3, rx * du)
        acc(G_WB2, rx * du1)
        acc(G_WB1, rx * du2)
        acc(G_WB0, rx * du3)

    rev = lambda i: (nt - 1 - i, 0)
    h_halo16 = lambda i: (jnp.maximum((nt - 1 - i) * (ts // 16) - 1, 0), N_KEPT - 1)
    const2 = lambda i: (0, 0)
    const3 = lambda i: (0, 0, 0)
    return pl.pallas_call(
        _after(deps, body), grid=(nt,),
        out_shape=(jax.ShapeDtypeStruct((s, D_IN), BF16), jax.ShapeDtypeStruct((N_SMALL, D), F32),
                   jax.ShapeDtypeStruct((HEADS, HB, HB), F32), jax.ShapeDtypeStruct((HEADS, HB, HB), F32),
                   jax.ShapeDtypeStruct((D, D), F32)),
        in_specs=[_ANY] * len(deps) + [pl.BlockSpec((ts, D), rev), pl.BlockSpec((ts, D), rev), pl.BlockSpec((ts, D), rev),
                  pl.BlockSpec((ts, 4 * D), rev), pl.BlockSpec((ts, N_KEPT * D), rev), pl.BlockSpec((ts, D), rev),
                  pl.BlockSpec((16, D), h_halo16), _VMEM, _VMEM, _VMEM, _VMEM, _VMEM],
        out_specs=[pl.BlockSpec((ts, D_IN), rev), pl.BlockSpec((N_SMALL, D), const2),
                   pl.BlockSpec((HEADS, HB, HB), const3), pl.BlockSpec((HEADS, HB, HB), const3), pl.BlockSpec((D, D), const2)],
        scratch_shapes=[pltpu.VMEM((1, g + 8, D), F32), pltpu.VMEM((1, g + 8, D), F32),
                        pltpu.VMEM((2, g + 8, D), F32), pltpu.VMEM((3, g + 8, D), F32), pltpu.VMEM((g, D), F32),
                        pltpu.VMEM((g, D), F32), pltpu.VMEM((g, D), F32), pltpu.VMEM((8, D), F32)],
        compiler_params=_cparams(), name="bwd_mix")(*deps, dx1, z1, merged, proj, kept, decay, kept, vecs, w_rga,
                                                    w_rgx, w_out, small)


def _bwd_in(dproj, x, dx1, vecs, w_in_g, small, ts, deps=()):
    s = x.shape[0]

    def body(dp_ref, x_ref, dx1_ref, v_ref, w_ref, sm0_ref, gx_ref, sm_ref):
        @pl.when(pl.program_id(0) == 0)
        def _():
            sm_ref[...] = sm0_ref[...]

        def vrow(j):
            return v_ref[j:j + 1, :]

        dh1 = _dot_nt(dp_ref[:, 0:C_IN], w_ref[0])
        for k in range(1, N_CHIPS):
            dh1 += _dot_nt(dp_ref[:, k * C_IN:(k + 1) * C_IN], w_ref[k])
        xh, rstd = _rms(x_ref[...])
        sm_ref[G_SH1:G_SH1 + 1, :] += _rowsum(dh1)
        sm_ref[G_SC1:G_SC1 + 1, :] += _rowsum(dh1 * (xh * vrow(V_GMIX)))
        dn1 = dh1 * (1.0 + vrow(V_SC1))
        sm_ref[G_GMIX:G_GMIX + 1, :] += _rowsum(dn1 * xh)
        gx_ref[...] = dx1_ref[...] + _rms_bwd(dn1 * vrow(V_GMIX), xh, rstd)

    row = lambda i: (i, 0)
    return pl.pallas_call(
        _after(deps, body), grid=(s // ts,),
        out_shape=(jax.ShapeDtypeStruct((s, D), F32), jax.ShapeDtypeStruct((N_SMALL, D), F32)),
        in_specs=[_ANY] * len(deps) + [pl.BlockSpec((ts, D_IN), row), pl.BlockSpec((ts, D), row), pl.BlockSpec((ts, D), row),
                                       _VMEM, _VMEM, _VMEM],
        out_specs=[pl.BlockSpec((ts, D), row), pl.BlockSpec((N_SMALL, D), lambda i: (0, 0))],
        compiler_params=_cparams(), name="bwd_in")(*deps, dproj, x, dx1, vecs, w_in_g, small)


def _grad_w(a, b, n_col_blocks, ts, name, deps=()):
    s, m = a.shape
    tn = b.shape[1] // n_col_blocks
    n_steps = s // ts

    def body(a_ref, b_ref, o_ref, acc_ref):
        k = pl.program_id(1)

        @pl.when(k == 0)
        def _():
            acc_ref[...] = jnp.zeros((m, tn), F32)

        acc_ref[...] += _dot_tn(a_ref[...], b_ref[...])

        @pl.when(k == n_steps - 1)
        def _():
            o_ref[...] = acc_ref[...].astype(BF16)

    return pl.pallas_call(
        _after(deps, body), grid=(n_col_blocks, n_steps),
        out_shape=jax.ShapeDtypeStruct((n_col_blocks, m, tn), BF16),
        in_specs=[_ANY] * len(deps) + [pl.BlockSpec((ts, m), lambda n, k: (k, 0)), pl.BlockSpec((ts, tn), lambda n, k: (k, n))],
        out_specs=pl.BlockSpec((None, m, tn), lambda n, k: (n, 0, 0)),
        scratch_shapes=[pltpu.VMEM((m, tn), F32)],
        compiler_params=_cparams(2), name=name)(*deps, a, b)


def _ada_fwd(c_all, w_ada, b_ada):
    n = w_ada.shape[1]

    def body(c_ref, w_ref, b_ref, o_ref, ca_ref):
        c = c_ref[...]
        ca = c * jax.nn.sigmoid(c)
        ca_ref[...] = ca
        o_ref[...] = jnp.dot(ca, w_ref[...], preferred_element_type=F32, precision=lax.Precision.HIGHEST) + b_ref[...]

    return pl.pallas_call(
        body, out_shape=(jax.ShapeDtypeStruct((N_DEV, n), F32), jax.ShapeDtypeStruct((N_DEV, D), F32)),
        in_specs=[_VMEM] * 3, out_specs=[_VMEM] * 2, compiler_params=_cparams(0), name="ada_fwd")(c_all, w_ada, b_ada)


def _ada_bwd(c_act, dmod):
    n = dmod.shape[1]

    def body(c_ref, d_ref, o_ref):
        o_ref[...] = lax.dot_general(c_ref[...], d_ref[...], (((0,), (0,)), ((), ())), preferred_element_type=F32,
                                     precision=lax.Precision.HIGHEST)

    return pl.pallas_call(
        body, out_shape=jax.ShapeDtypeStruct((D, n), F32), in_specs=[_VMEM] * 2, out_specs=_VMEM,
        compiler_params=_cparams(0), name="ada_bwd")(c_act, dmod)


def _sum_small(parts):
    def body(p_ref, o_ref, d_ref):
        tot = p_ref[0]
        for dev in range(1, N_DEV):
            tot = tot + p_ref[dev]
        o_ref[...] = tot
        d_ref[...] = p_ref[:, 0:8, :]

    return pl.pallas_call(
        body, out_shape=(jax.ShapeDtypeStruct((N_SMALL, D), F32), jax.ShapeDtypeStruct((N_DEV, 8, D), F32)),
        in_specs=[_VMEM], out_specs=[_VMEM] * 2, compiler_params=_cparams(0), name="sum_small")(parts)


def _adamw(w, g, m, v, name, deps=()):
    rows, cols = w.shape
    tr = 128 if rows % 128 == 0 else (64 if rows % 64 == 0 else rows)

    def body(w_ref, g_ref, m_ref, v_ref, d_ref, nm_ref, nv_ref):
        g_ = g_ref[...]
        m_ = ADAM_B1 * m_ref[...] + (1.0 - ADAM_B1) * g_
        v_ = ADAM_B2 * v_ref[...] + (1.0 - ADAM_B2) * (g_ * g_)
        nm_ref[...] = m_
        nv_ref[...] = v_
        m_hat = m_ / (1.0 - ADAM_B1 ** ADAM_STEP)
        v_hat = v_ / (1.0 - ADAM_B2 ** ADAM_STEP)
        d_ref[...] = -ADAM_LR * (m_hat / (jnp.sqrt(v_hat) + ADAM_EPS) + ADAM_WD * w_ref[...])

    spec = pl.BlockSpec((tr, cols), lambda i: (i, 0))
    return pl.pallas_call(
        _after(deps, body), grid=(rows // tr,), out_shape=(jax.ShapeDtypeStruct((rows, cols), F32),) * 3,
        in_specs=[_ANY] * len(deps) + [spec] * 4, out_specs=[spec] * 3, compiler_params=_cparams(), name=name)(*deps, w, g, m, v)


def _adamw_small(items, name):
    n = len(items)

    def body(*refs):
        ins, outs = refs[:4 * n], refs[4 * n:]
        for k in range(n):
            w_ref, g_ref, m_ref, v_ref = ins[4 * k:4 * k + 4]
            d_ref, nm_ref, nv_ref = outs[3 * k:3 * k + 3]
            g_ = g_ref[...]
            m_ = ADAM_B1 * m_ref[...] + (1.0 - ADAM_B1) * g_
            v_ = ADAM_B2 * v_ref[...] + (1.0 - ADAM_B2) * (g_ * g_)
            nm_ref[...] = m_
            nv_ref[...] = v_
            m_hat = m_ / (1.0 - ADAM_B1 ** ADAM_STEP)
            v_hat = v_ / (1.0 - ADAM_B2 ** ADAM_STEP)
            d_ref[...] = -ADAM_LR * (m_hat / (jnp.sqrt(v_hat) + ADAM_EPS) + ADAM_WD * w_ref[...])

    out = pl.pallas_call(
        body, out_shape=tuple(jax.ShapeDtypeStruct(it[0].shape, F32) for it in items for _ in range(3)),
        in_specs=[_VMEM] * (4 * n), out_specs=[_VMEM] * (3 * n), name=name)(*[a for it in items for a in it])
    return [tuple(out[3 * k:3 * k + 3]) for k in range(n)]


def _adamw_halves(w, mine, other, m, v, c_idx, name, deps=()):
    r2, cols = mine.shape
    tr = next(t for t in (128, 64, 32, 16, 8) if r2 % t == 0)
    nh = r2 // tr

    def body(c_ref, w_ref, mine_ref, other_ref, m_ref, v_ref, g_ref, d_ref, nm_ref, nv_ref):
        g_ = jnp.where(pl.program_id(0) // nh == c_ref[0], mine_ref[...], other_ref[...])
        g_ref[...] = g_
        m_ = ADAM_B1 * m_ref[...] + (1.0 - ADAM_B1) * g_
        v_ = ADAM_B2 * v_ref[...] + (1.0 - ADAM_B2) * (g_ * g_)
        nm_ref[...] = m_
        nv_ref[...] = v_
        m_hat = m_ / (1.0 - ADAM_B1 ** ADAM_STEP)
        v_hat = v_ / (1.0 - ADAM_B2 ** ADAM_STEP)
        d_ref[...] = -ADAM_LR * (m_hat / (jnp.sqrt(v_hat) + ADAM_EPS) + ADAM_WD * w_ref[...])

    full = pl.BlockSpec((tr, cols), lambda i, c: (i, 0))
    mine_spec = pl.BlockSpec((tr, cols), lambda i, c: (jnp.clip(i - c[0] * nh, 0, nh - 1), 0))
    other_spec = pl.BlockSpec((tr, cols), lambda i, c: (jnp.clip(i - (1 - c[0]) * nh, 0, nh - 1), 0))
    return pl.pallas_call(
        lambda c_ref, *refs: body(c_ref, *refs[len(deps):]),
        grid_spec=pltpu.PrefetchScalarGridSpec(
            num_scalar_prefetch=1, grid=(2 * nh,),
            in_specs=[_ANY] * len(deps) + [full, mine_spec, other_spec, full, full], out_specs=[full] * 4),
        out_shape=(jax.ShapeDtypeStruct((2 * r2, cols), F32),) * 4, compiler_params=_cparams(), name=name,
    )(c_idx, *deps, w, mine, other, m, v)


def _add_halves(g, recv, c_idx, name):
    n, _, r2, cols = g.shape

    def body(c_ref, g_ref, r_ref, o_ref):
        o_ref[...] = (g_ref[...].astype(F32) + r_ref[...].astype(F32)).astype(BF16)

    return pl.pallas_call(
        body,
        grid_spec=pltpu.PrefetchScalarGridSpec(
            num_scalar_prefetch=1, grid=(n,),
            in_specs=[pl.BlockSpec((None, None, r2, cols), lambda k, c: (k, c[0], 0, 0)),
                      pl.BlockSpec((None, r2, cols), lambda k, c: (k, 0, 0))],
            out_specs=pl.BlockSpec((None, r2, cols), lambda k, c: (k, 0, 0))),
        out_shape=jax.ShapeDtypeStruct((n, r2, cols), BF16), compiler_params=_cparams(), name=name)(c_idx, g, recv)


def _sum_chips(own, others, chip_idx, name):
    _, r2, cols = own.shape
    tr = next(t for t in (128, 176, 64, 32, 16) if r2 % t == 0)

    def body(p_ref, own_ref, o_ref_in, o_ref):
        o_ref[...] = (((own_ref[...].astype(F32) + o_ref_in[0].astype(F32)) + o_ref_in[1].astype(F32))
                      + o_ref_in[2].astype(F32))

    return pl.pallas_call(
        body,
        grid_spec=pltpu.PrefetchScalarGridSpec(
            num_scalar_prefetch=1, grid=(r2 // tr,),
            in_specs=[pl.BlockSpec((None, tr, cols), lambda i, p: (p[0], i, 0)),
                      pl.BlockSpec((N_CHIPS - 1, tr, cols), lambda i, p: (0, i, 0))],
            out_specs=pl.BlockSpec((tr, cols), lambda i, p: (i, 0))),
        out_shape=jax.ShapeDtypeStruct((r2, cols), F32), compiler_params=_cparams(), name=name)(chip_idx, own, others)


def _place():
    x, y, c = lax.axis_index("x"), lax.axis_index("y"), lax.axis_index("c")
    return x, y, c, 2 * x + y


def _flip(v, bit):
    return 1 - v if bit else v


def _allgather8(v, name, deps=()):
    r, n = v.shape

    def body(*refs):
        v_ref, out_ref, send_sems, recv_sems, local_sem = refs[len(deps):]
        x, y, c, _ = _place()
        me = 4 * x + 2 * y + c
        mine = pltpu.make_async_copy(v_ref, out_ref.at[me], local_sem)
        mine.start()
        sends = []
        for rel in range(1, N_DEV):
            peer = (_flip(x, rel & 4), _flip(y, rel & 2), _flip(c, rel & 1))
            cp = pltpu.make_async_remote_copy(v_ref, out_ref.at[me], send_sems.at[rel - 1], recv_sems.at[rel - 1],
                                              device_id=peer, device_id_type=MESH)
            cp.start()
            sends.append(cp)
        for rel in range(1, N_DEV):
            peer = (_flip(x, rel & 4), _flip(y, rel & 2), _flip(c, rel & 1))
            peer_idx = 4 * peer[0] + 2 * peer[1] + peer[2]
            pltpu.make_async_remote_copy(v_ref, out_ref.at[peer_idx], send_sems.at[rel - 1], recv_sems.at[rel - 1],
                                         device_id=peer, device_id_type=MESH).wait_recv()
        for cp in sends:
            cp.wait_send()
        mine.wait()

    return pl.pallas_call(
        body, out_shape=jax.ShapeDtypeStruct((N_DEV, r, n), F32), in_specs=[_ANY] * len(deps) + [_VMEM], out_specs=_VMEM,
        scratch_shapes=[pltpu.SemaphoreType.DMA((N_DEV - 1,)), pltpu.SemaphoreType.DMA((N_DEV - 1,)), pltpu.SemaphoreType.DMA(())],
        name=name)(*deps, v)


_HBM = pl.BlockSpec(memory_space=pltpu.HBM)
_SEM = pl.BlockSpec(memory_space=pltpu.SEMAPHORE)
_EFFECT = pltpu.SideEffectType.DATAFLOW_SIDE_EFFECTING


def _xchg_start(name, plan, n_copies, srcs, lands, after=(), sibling_id=None):
    bufs = list(srcs) + list(lands)
    ns, nb = len(srcs), len(srcs) + len(lands)

    def body(*refs):
        send_sems, recv_sems, token = refs[nb + len(after)], refs[nb + len(after) + 1], refs[-1]
        if sibling_id is not None:
            x, y, c, _ = _place()
            barrier = pltpu.get_barrier_semaphore()
            pl.semaphore_signal(barrier, inc=1, device_id=(x, y, 1 - c), device_id_type=MESH)
            pl.semaphore_wait(barrier, 1)
        for i, (src, dst, peer, _) in enumerate(plan(_place(), refs[:ns], refs[ns:nb])):
            pltpu.make_async_remote_copy(src, dst, send_sems.at[i], recv_sems.at[i], device_id=peer, device_id_type=MESH).start()
        token[...] = jnp.zeros_like(token)

    out = pl.pallas_call(
        body, name=name,
        out_shape=(pltpu.SemaphoreType.DMA((n_copies,)), pltpu.SemaphoreType.DMA((n_copies,)),
                   *[pltpu.HBM(a.shape, a.dtype) for a in bufs], jax.ShapeDtypeStruct((8, 128), F32)),
        in_specs=[_HBM] * nb + [_ANY] * len(after), out_specs=(_SEM, _SEM, *[_HBM] * nb, _VMEM),
        input_output_aliases={i: 2 + i for i in range(nb)},
        compiler_params=pltpu.CompilerParams(has_side_effects=_EFFECT, collective_id=sibling_id),
    )(*[pltpu.with_memory_space_constraint(a, pltpu.HBM) for a in bufs], *after)
    return (out[0], out[1]), out[2:2 + ns], out[2 + ns:2 + nb], out[-1]


def _xchg_wait(name, plan, sems, srcs, lands, after, sem_ids=None):
    bufs = list(srcs) + list(lands)
    ns, nb = len(srcs), len(srcs) + len(lands)

    def body(*refs):
        send_sems, recv_sems = refs[nb], refs[nb + 1]
        copies = plan(_place(), refs[:ns], refs[ns:nb])
        ids = range(len(copies)) if sem_ids is None else sem_ids
        for i, (src, _, peer, mine) in zip(ids, copies, strict=True):
            if i is not None:
                cp = pltpu.make_async_remote_copy(src, mine, send_sems.at[i], recv_sems.at[i], device_id=peer,
                                                  device_id_type=MESH)
                cp.wait_send()
                cp.wait_recv()

    out = pl.pallas_call(
        body, name=name, out_shape=tuple(pltpu.HBM(a.shape, a.dtype) for a in bufs),
        in_specs=[_HBM] * nb + [_SEM, _SEM] + [_ANY] * len(after), out_specs=tuple([_HBM] * nb),
        input_output_aliases={i: i for i in range(nb)},
        compiler_params=pltpu.CompilerParams(has_side_effects=_EFFECT),
    )(*bufs, *sems, *after)
    return out[:ns], out[ns:]


def _other_chips(place, which=(1, 2, 3)):
    x, y, c, _ = place
    return [((_flip(x, j & 2), _flip(y, j & 1), c), 2 * _flip(x, j & 2) + _flip(y, j & 1)) for j in which]


def _plan_gather_ici(chips):
    def plan(place, src_refs, land_refs):
        _, _, c, p = place
        return [(s.at[c], l.at[p, c], peer, l.at[q, c]) for s, l, which in zip(src_refs, land_refs, chips, strict=True)
                for peer, q in _other_chips(place, which)]
    return plan


def _plan_relay(which):
    def plan(place, src_refs, land_refs):
        x, y, c, _ = place
        return [(l.at[q, c], l.at[q, c], (x, y, 1 - c), l.at[q, 1 - c]) for l in land_refs for _, q in _other_chips(place, which)]
    return plan


def _plan_swap(place, src_refs, land_refs):
    x, y, c, _ = place
    return [(s.at[k, 1 - c], l.at[k], (x, y, 1 - c), l.at[k]) for s, l in zip(src_refs, land_refs) for k in range(N_CHIPS)]


def _plan_scatter(place, src_refs, land_refs):
    return [(s.at[q], l.at[j], peer, l.at[j]) for s, l in zip(src_refs, land_refs)
            for j, (peer, q) in enumerate(_other_chips(place))]


def _plan_share(place, src_refs, land_refs):
    x, y, c, _ = place
    return [(s, l, (x, y, 1 - c), l) for s, l in zip(src_refs, land_refs)]


def _pack_rows(parts, n_rows, name, deps=()):
    def body(*refs):
        refs = refs[len(deps):]
        out_ref = refs[-1]
        out_ref[...] = jnp.zeros((n_rows, D), F32)
        at = 0
        for ref in refs[:-1]:
            k = ref.shape[0]
            out_ref[at:at + k, :] = ref[...]
            at += k

    return pl.pallas_call(
        body, out_shape=jax.ShapeDtypeStruct((n_rows, D), F32), in_specs=[_ANY] * len(deps) + [_VMEM] * len(parts),
        out_specs=_VMEM, name=name)(*deps, *parts)


TS_MM = 512
TS_IN = 1024
TS_GW = 1024
TS_MIX = 256


def _halved(a):
    n, r, cols = a.shape
    return a.reshape(n, 2, r // 2, cols)


SIBLING_IDS = (1, 2)


def _rs_swap(name, grads, after=()):
    lands = [lax.empty((N_CHIPS,) + g.shape[2:], g.dtype) for g in grads]
    sems, grads, lands, token = _xchg_start(name + "_swap", _plan_swap, N_CHIPS * len(grads), grads, lands, after,
                                            sibling_id=SIBLING_IDS[0])
    return name, sems, grads, lands, token


def _rs_scatter(handle, after, chip, ci):
    name, sems, grads, lands, _ = handle
    grads, from_sibling = _xchg_wait(name + "_swap_wait", _plan_swap, sems, grads, lands, after)
    c_arr = jnp.reshape(ci, (1,)).astype(jnp.int32)
    pair_sums = [_add_halves(g, r, c_arr, "%s_add_halves_%d" % (name, k)) for k, (g, r) in enumerate(zip(grads, from_sibling))]
    lands = [lax.empty((N_CHIPS - 1,) + p.shape[1:], p.dtype) for p in pair_sums]
    sems, pair_sums, lands, token = _xchg_start(name + "_scatter", _plan_scatter, 3 * len(pair_sums), pair_sums, lands)
    return name, sems, pair_sums, lands, jnp.reshape(chip, (1,)).astype(jnp.int32), token


def _rs_share(handle, after):
    name, sems, pair_sums, lands, chip_idx, _ = handle
    pair_sums, received = _xchg_wait(name + "_scatter_wait", _plan_scatter, sems, pair_sums, lands, after)
    halves = [_sum_chips(p, r, chip_idx, "%s_sum_chips_%d" % (name, k)) for k, (p, r) in enumerate(zip(pair_sums, received))]
    lands = [lax.empty(h.shape, h.dtype) for h in halves]
    sems, halves, lands, token = _xchg_start(name + "_share", _plan_share, len(halves), halves, lands,
                                             sibling_id=SIBLING_IDS[1])
    return name, sems, halves, lands, token


def _rs_end(handle, after):
    name, sems, halves, lands, _ = handle
    halves, others = _xchg_wait(name + "_share_wait", _plan_share, sems, halves, lands, after)
    return list(zip(halves, others))


def kernel(x, c, w_ada, b_ada, g_norm_mix, w_in, conv_a_w, conv_b_w, conv_b_bias, w_rg_a, b_rg_a, w_rg_x, b_rg_x, lru_lambda, w_out, g_norm_ffn, w_gate_up, w_down, g_norm_final, loss_target, m_w_ada, m_b_ada, m_g_norm_mix, m_w_in, m_conv_a_w, m_conv_b_w, m_conv_b_bias, m_w_rg_a, m_b_rg_a, m_w_rg_x, m_b_rg_x, m_lru_lambda, m_w_out, m_g_norm_ffn, m_w_gate_up, m_w_down, m_g_norm_final, v_w_ada, v_b_ada, v_g_norm_mix, v_w_in, v_conv_a_w, v_conv_b_w, v_conv_b_bias, v_w_rg_a, v_b_rg_a, v_w_rg_x, v_b_rg_x, v_lru_lambda, v_w_out, v_g_norm_ffn, v_w_gate_up, v_w_down, v_g_norm_final):
    xi, yi, ci = lax.axis_index("x"), lax.axis_index("y"), lax.axis_index("c")
    chip = 2 * xi + yi
    me = 2 * chip + ci
    n_ada = w_ada.shape[2]

    def widen(w):
        return jnp.pad(w, ((0, 0), (0, D - w.shape[1])))

    got = _allgather8(_pack_rows([c, widen(conv_a_w[0]), widen(conv_b_w[0])], 8, "pack_c_conv"), "gather_c_conv")
    c_all = got[:, 0, :]
    conv_full = got[::2, 1:8, :D // N_CHIPS].transpose(1, 0, 2).reshape(7, D)

    mod_part, c_act = _ada_fwd(c_all, w_ada[0], lax.dynamic_slice_in_dim(b_ada, chip * n_ada, n_ada, axis=1))
    mod_all = _allgather8(mod_part, "gather_mod")
    mod_mine = lax.dynamic_index_in_dim(mod_all, me, axis=1, keepdims=False)[::2].reshape(6, D)
    vecs = _pack_rows([mod_mine, g_norm_mix, g_norm_ffn, g_norm_final.reshape(1, D), conv_b_bias, b_rg_a, b_rg_x, lru_lambda,
                       conv_full], N_VEC, "pack_vecs")

    def rg_shard(w):
        return w[0].astype(BF16).reshape(2, HEADS * HB // N_CHIPS // 2, HB)

    shards = [w_in[0].astype(BF16).reshape(2, D // 2, C_IN), rg_shard(w_rg_a), rg_shard(w_rg_x),
              w_out[0].astype(BF16).reshape(2, D // N_CHIPS // 2, D), w_gate_up[0].astype(BF16).reshape(2, D // 2, C_GU),
              w_down[0].astype(BF16).reshape(2, D_FF // N_CHIPS // 2, D)]
    lands = [lax.dynamic_update_index_in_dim(lax.empty((N_CHIPS,) + s.shape, s.dtype), s, chip, 0) for s in shards]

    def send(name, first, last, after, chips):
        copies = [(k, j) for k, which in zip(range(first, last), chips, strict=True) for j in which]
        sems, srcs, zone, token = _xchg_start(name + "_ici", _plan_gather_ici(chips), len(copies), shards[first:last],
                                              lands[first:last], after)
        shards[first:last], lands[first:last] = srcs, zone
        return sems, copies, token

    def arrive(name, sent, first, last, after):
        sems, copies, _ = sent
        chips = [tuple(j for k, j in copies if k == want) for want in range(first, last)]
        ids = [copies.index((k, j)) for k, which in zip(range(first, last), chips) for j in which]
        srcs, zone = _xchg_wait(name + "_ici_wait", _plan_gather_ici(chips), sems, shards[first:last], lands[first:last], after,
                                ids)
        shards[first:last], lands[first:last] = srcs, zone

    def relay(name, first, last, which, sibling_id):
        plan = _plan_relay(which)
        sems, _, zone, token = _xchg_start(name + "_d2d", plan, len(which) * (last - first), [], lands[first:last],
                                           sibling_id=sibling_id)
        lands[first:last] = zone
        return name, plan, sems, first, last, token

    def relayed(handle, after):
        name, plan, sems, first, last, _ = handle
        lands[first:last] = _xchg_wait(name + "_d2d_wait", plan, sems, [], lands[first:last], after)[1]

    def to_blocks(v):
        return v.reshape(-1, TS_MIX // TIME_BLOCKS, TIME_BLOCKS, D).transpose(0, 2, 1, 3).reshape(v.shape)

    def from_blocks(v):
        return v.reshape(-1, TIME_BLOCKS, TS_MIX // TIME_BLOCKS, D).transpose(0, 2, 1, 3).reshape(v.shape)

    def chip_index(j):
        return jnp.reshape(chip ^ j, (1,)).astype(jnp.int32)

    def wg_in():
        return lands[0].reshape(N_CHIPS, D, C_IN)

    xs, target = to_blocks(x[0]), to_blocks(loss_target[0])
    sent_near = send("gather_in_near", 0, 1, [vecs], [(1, 2)])
    ts_in = min(TS_IN, xs.shape[0])
    h1, proj = _fwd_in_first(xs, vecs, wg_in(), chip_index(0), ts_in, deps=[sent_near[-1]])
    arrive("gather_in_near", sent_near, 0, 1, [proj])
    near = relay("gather_in_near", 0, 1, (1, 2), SIBLING_IDS[0])
    sent_rest = send("gather_rest", 0, 6, [near[-1]], [(3,)] + [(1, 2, 3)] * 5)
    relayed(near, [sent_rest[-1]])
    proj = _fwd_in_more(h1, wg_in(), proj, chip_index(1), ts_in, "fwd_in_y")
    proj = _fwd_in_more(h1, wg_in(), proj, chip_index(2), ts_in, "fwd_in_x")
    arrive("gather_in_far", sent_rest, 0, 1, [proj])
    far = relay("gather_in_far", 0, 1, (3,), SIBLING_IDS[1])
    arrive("gather_mix", sent_rest, 1, 4, [far[-1]])
    relayed(far, [far[-1]])
    mix = relay("gather_mix", 1, 4, (1, 2, 3), SIBLING_IDS[0])
    proj = _fwd_in_more(h1, wg_in(), proj, chip_index(3), ts_in, "fwd_in_xy", deps=[mix[-1]])
    relayed(mix, [proj])
    wg_rga, wg_rgx, wg_out = lands[1:4]
    wg_out = wg_out.reshape(D, D)

    def rg_full(wg):
        return wg.reshape(N_CHIPS, HEADS, HB // N_CHIPS, HB).transpose(1, 0, 2, 3).reshape(HEADS, HB, HB)

    wg_rga, wg_rgx = rg_full(wg_rga), rg_full(wg_rgx)

    arrive("gather_ffn", sent_rest, 4, 6, [proj])
    ffn = relay("gather_ffn", 4, 6, (1, 2, 3), SIBLING_IDS[1])
    x1, merged, z1, kept, decay = _fwd_mix(proj, xs, vecs, wg_rga, wg_rgx, wg_out, TS_MIX, deps=[ffn[-1]])
    relayed(ffn, [x1])
    wg_gu, wg_dn = lands[4:6]
    wg_gu, wg_dn = wg_gu.reshape(N_CHIPS, D, C_GU), wg_dn.reshape(D_FF, D)
    dx1, h2, act, dz2, dgu, sm_ffn = _ffn_loss(x1, target, vecs, wg_gu, wg_dn, TS_MIX)

    def rg_chunks(dw):
        return _halved(dw.reshape(HEADS, N_CHIPS, HB // N_CHIPS, HB).transpose(1, 0, 2, 3).reshape(N_CHIPS, HB, HB).astype(BF16))

    ts_gw = min(TS_GW, xs.shape[0])
    g_dn = _grad_w(act, dz2, 1, ts_gw, "grad_w_down")
    g_gu = _grad_w(h2, dgu, N_CHIPS, ts_gw, "grad_w_gate_up")
    rs_b = _rs_swap("rs_b", [_halved(g_gu), _halved(g_dn.reshape(N_CHIPS, D_FF // N_CHIPS, D))])
    dproj, sm_mix, dw_rga, dw_rgx, dw_out = _bwd_mix(dx1, z1, merged, proj, kept, decay, vecs, wg_rga, wg_rgx, wg_out, sm_ffn, TS_MIX,
                                                     deps=[rs_b[-1]])
    rs_b = _rs_scatter(rs_b, [dproj], chip, ci)
    g_in = _grad_w(h1, dproj, N_CHIPS, ts_gw, "grad_w_in", deps=[rs_b[-1]])
    rs_b = _rs_share(rs_b, [g_in])
    rs_a = _rs_swap("rs_a", [_halved(g_in), rg_chunks(dw_rga), rg_chunks(dw_rgx),
                             _halved(dw_out.astype(BF16).reshape(N_CHIPS, D // N_CHIPS, D))], after=[rs_b[-1]])

    c_arr = jnp.reshape(ci, (1,)).astype(jnp.int32)

    def step(name, w, g, m, v, deps=()):
        shape = w.shape
        two_d = (-1, shape[-1])
        d, nm, nv = _adamw(w.reshape(two_d), g.reshape(two_d), m.reshape(two_d), v.reshape(two_d), "adamw_" + name, deps)
        return g.reshape(shape), d.reshape(shape), nm.reshape(shape), nv.reshape(shape)

    def step_halves(name, w, halves, m, v, deps=()):
        shape = w.shape
        two_d = (-1, shape[-1])
        out = _adamw_halves(w.reshape(two_d), halves[0], halves[1], m.reshape(two_d), v.reshape(two_d), c_arr, "adamw_" + name, deps)
        return tuple(a.reshape(shape) for a in out)

    def shard_cols(row_block):
        return lax.dynamic_slice_in_dim(row_block, chip * (D // N_CHIPS), D // N_CHIPS, axis=1)

    gw_gu, gw_dn = _rs_end(rs_b, [rs_a[-1]])
    res = {
        "w_gate_up": step_halves("w_gate_up", w_gate_up, gw_gu, m_w_gate_up, v_w_gate_up, [rs_a[-1]]),
        "w_down": step_halves("w_down", w_down, gw_dn, m_w_down, v_w_down, [rs_a[-1]]),
    }
    rs_a = _rs_scatter(rs_a, [res["w_gate_up"][1], res["w_down"][1]], chip, ci)
    grad_x, sm_in = _bwd_in(dproj, xs, dx1, vecs, wg_in(), sm_mix, TS_MM, deps=[rs_a[-1]])
    rs_a = _rs_share(rs_a, [grad_x])

    small, per_dev = _sum_small(_allgather8(sm_in, "gather_small", deps=[rs_a[-1]]))
    dmod_all = per_dev[:, 0:6, :].reshape(N_DEV, 6 * D)
    grad_w_ada = _ada_bwd(c_act, lax.dynamic_slice_in_dim(dmod_all, chip * n_ada, n_ada, axis=1))
    grad_b_ada = small[0:6].reshape(1, 6 * D)
    res["w_ada"] = step("w_ada", w_ada, grad_w_ada[None], m_w_ada, v_w_ada)
    small_sets = {
        "b_ada": (b_ada.reshape(6, D), grad_b_ada.reshape(6, D), m_b_ada.reshape(6, D), v_b_ada.reshape(6, D)),
        "g_norm_mix": (g_norm_mix, small[G_GMIX:G_GMIX + 1], m_g_norm_mix, v_g_norm_mix),
        "conv_a_w": (conv_a_w[0], shard_cols(small[G_WA0:G_WA0 + 3]), m_conv_a_w[0], v_conv_a_w[0]),
        "conv_b_w": (conv_b_w[0], shard_cols(small[G_WB0:G_WB0 + 4]), m_conv_b_w[0], v_conv_b_w[0]),
        "conv_b_bias": (conv_b_bias, small[G_CBB:G_CBB + 1], m_conv_b_bias, v_conv_b_bias),
        "b_rg_a": (b_rg_a, small[G_BA:G_BA + 1], m_b_rg_a, v_b_rg_a),
        "b_rg_x": (b_rg_x, small[G_BX:G_BX + 1], m_b_rg_x, v_b_rg_x),
        "lru_lambda": (lru_lambda, small[G_LAM:G_LAM + 1], m_lru_lambda, v_lru_lambda),
        "g_norm_ffn": (g_norm_ffn, small[G_GFFN:G_GFFN + 1], m_g_norm_ffn, v_g_norm_ffn),
        "g_norm_final": (g_norm_final.reshape(1, D), small[G_GFIN:G_GFIN + 1], m_g_norm_final.reshape(1, D),
                         v_g_norm_final.reshape(1, D)),
    }
    stepped = _adamw_small(list(small_sets.values()), "adamw_small")
    for (n, (w_, g_, _, _)), (d_, nm_, nv_) in zip(small_sets.items(), stepped):
        shape = (1,) + w_.shape if n.startswith("conv_") and n != "conv_b_bias" else w_.shape
        res[n] = tuple(a.reshape(shape) for a in (g_, d_, nm_, nv_))
    gw_in, gw_rga, gw_rgx, gw_out = _rs_end(rs_a, [res[n][1] for n in res])
    res["w_in"] = step_halves("w_in", w_in, gw_in, m_w_in, v_w_in)
    res["w_rg_a"] = step_halves("w_rg_a", w_rg_a, gw_rga, m_w_rg_a, v_w_rg_a)
    res["w_rg_x"] = step_halves("w_rg_x", w_rg_x, gw_rgx, m_w_rg_x, v_w_rg_x)
    res["w_out"] = step_halves("w_out", w_out, gw_out, m_w_out, v_w_out)
    res["b_ada"] = tuple(a.reshape(1, 6 * D) for a in res["b_ada"])
    res["g_norm_final"] = tuple(a.reshape(D) for a in res["g_norm_final"])
    names = ["w_ada", "b_ada", "g_norm_mix", "w_in", "conv_a_w", "conv_b_w", "conv_b_bias", "w_rg_a", "b_rg_a", "w_rg_x",
             "b_rg_x", "lru_lambda", "w_out", "g_norm_ffn", "w_gate_up", "w_down", "g_norm_final"]
    loss = jnp.sum(small[G_LOSS])
    return (loss, from_blocks(grad_x)[None], *[res[n][0] for n in names], *[res[n][1] for n in names],
            *[res[n][2] for n in names], *[res[n][3] for n in names])
```

```python
import functools

import jax
import jax.numpy as jnp
from jax import lax
from jax.experimental import pallas as pl
from jax.experimental.pallas import tpu as pltpu

F32 = jnp.float32
BF16 = jnp.bfloat16
MESH = pl.DeviceIdType.MESH

D = 1024
N_CHIPS = 4
N_DEV = 8
D_IN = 7 * D
C_IN = D_IN // N_CHIPS
D_FF = 2816
C_GU = 2 * D_FF // N_CHIPS
HEADS = 4
HB = D // HEADS
EPS = 1e-6
LRU_C = 8.0
ADAM_LR, ADAM_B1, ADAM_B2, ADAM_EPS, ADAM_WD, ADAM_STEP = 0.001, 0.9, 0.999, 1e-08, 0.01, 10
VMEM_LIMIT = 56 << 20

(V_SH1, V_SC1, V_GT1, V_SH2, V_SC2, V_GT2, V_GMIX, V_GFFN, V_GFIN, V_CBB, V_BA, V_BX, V_LAM,
 V_WA0, V_WA1, V_WA2, V_WB0, V_WB1, V_WB2, V_WB3) = range(20)
N_VEC = 24
(G_SH1, G_SC1, G_GT1, G_SH2, G_SC2, G_GT2, G_GMIX, G_CBB, G_BA, G_BX, G_LAM, G_GFFN, G_GFIN,
 G_WA0, G_WA1, G_WA2, G_WB0, G_WB1, G_WB2, G_WB3, G_LOSS) = range(21)
N_SMALL = 24

_VMEM = pl.BlockSpec(memory_space=pltpu.VMEM)
_ANY = pl.BlockSpec(memory_space=pl.ANY)


def _cparams(n_grid=1):
    return pltpu.CompilerParams(dimension_semantics=("arbitrary",) * n_grid, vmem_limit_bytes=VMEM_LIMIT)


def _after(deps, body):
    n = len(deps)
    return lambda *refs: body(*refs[n:])


def _rms(x):
    rstd = lax.rsqrt(jnp.mean(x * x, axis=-1, keepdims=True) + EPS)
    return x * rstd, rstd


def _rms_bwd(dxhat, xhat, rstd):
    return rstd * (dxhat - xhat * jnp.mean(dxhat * xhat, axis=-1, keepdims=True))


def _rowsum(v):
    return jnp.sum(v, axis=0, keepdims=True)


def _dot(a, b):
    return jnp.dot(a, b, preferred_element_type=F32)


def _dot_nt(a, b):
    return lax.dot_general(a, b, (((1,), (1,)), ((), ())), preferred_element_type=F32)


def _dot_tn(a, b):
    return lax.dot_general(a, b, (((0,), (0,)), ((), ())), preferred_element_type=F32)


def _gelu(x):
    k, c = 0.7978845608028654, 0.044715
    t = jnp.tanh(k * (x + c * x * x * x))
    return 0.5 * x * (1.0 + t), 0.5 * (1.0 + t) + 0.5 * x * (1.0 - t * t) * k * (1.0 + 3.0 * c * x * x)


def _log_sigmoid(lam):
    return jnp.minimum(lam, 0.0) - jnp.log1p(jnp.exp(-jnp.abs(lam)))


def _lru_gates(u, wa_ref, wx_ref, v_ref, row0):
    ub = u.astype(BF16)
    pre_a = jnp.concatenate([_dot(ub[:, h * HB:(h + 1) * HB], wa_ref[h]) for h in range(HEADS)], axis=1)
    pre_x = jnp.concatenate([_dot(ub[:, h * HB:(h + 1) * HB], wx_ref[h]) for h in range(HEADS)], axis=1)
    r = jax.nn.sigmoid(pre_a + v_ref[V_BA:V_BA + 1, :])
    ig = jax.nn.sigmoid(pre_x + v_ref[V_BX:V_BX + 1, :])
    log_a = LRU_C * r * _log_sigmoid(v_ref[V_LAM:V_LAM + 1, :])
    a = jnp.exp(log_a)
    x2 = 2.0 * log_a
    m2 = jnp.where(x2 > -0.03, -x2 * (1.0 + x2 * (0.5 + x2 * (1.0 / 6.0 + x2 * (1.0 / 24.0)))), 1.0 - a * a)
    mult = jnp.where(row0, 1.0, jnp.sqrt(jnp.maximum(m2, 0.0)))
    return r, ig, a, mult


TIME_BLOCKS = 8
N_KEPT = 10


def _late_blocks(v, buf, g, halo=None):
    n = buf.shape[0]
    out = []
    for idx in range(n):
        k = TIME_BLOCKS - n + idx
        buf[idx, 8:g + 8, :] = v[k * g:(k + 1) * g]
        if halo is not None:
            buf[idx, 7:8, :] = halo[idx]
        out.append(buf[idx, pl.ds(7, g), :])
        if halo is None:
            buf[idx, 7:8, :] = buf[idx, g + 7:g + 8, :]
    return out


def _earlier(v, s, late, g):
    return jnp.concatenate(late[len(late) - s:] + [v[0:(TIME_BLOCKS - s) * g]], axis=0)


def _early_blocks(v, buf, g):
    out = []
    for k in range(buf.shape[0]):
        buf[k, 0:g, :] = v[k * g:(k + 1) * g]
        out.append(buf[k, pl.ds(1, g), :])
        buf[k, g:g + 1, :] = buf[k, 0:1, :]
    return out


def _later(v, s, early, g):
    return jnp.concatenate([v[s * g:]] + early[0:s], axis=0)


def _fwd_in_first(x, vecs, w_in_g, q_idx, ts, deps=()):
    s = x.shape[0]

    def body(q_ref, x_ref, v_ref, w_ref, h1_ref, proj_ref):
        xhat, _ = _rms(x_ref[...])
        h = xhat * v_ref[V_GMIX:V_GMIX + 1, :] * (1.0 + v_ref[V_SC1:V_SC1 + 1, :]) + v_ref[V_SH1:V_SH1 + 1, :]
        hb = h.astype(BF16)
        h1_ref[...] = hb
        proj_ref[...] = _dot(hb, w_ref[...]).astype(BF16)

    return pl.pallas_call(
        lambda q_ref, *refs: body(q_ref, *refs[len(deps):]),
        grid_spec=pltpu.PrefetchScalarGridSpec(
            num_scalar_prefetch=1, grid=(s // ts,),
            in_specs=[_ANY] * len(deps) + [pl.BlockSpec((ts, D), lambda i, q: (i, 0)), _VMEM,
                                           pl.BlockSpec((None, D, C_IN), lambda i, q: (q[0], 0, 0))],
            out_specs=[pl.BlockSpec((ts, D), lambda i, q: (i, 0)), pl.BlockSpec((ts, C_IN), lambda i, q: (i, q[0]))]),
        out_shape=(jax.ShapeDtypeStruct((s, D), BF16), jax.ShapeDtypeStruct((s, D_IN), BF16)),
        compiler_params=_cparams(), name="fwd_in_own")(q_idx, *deps, x, vecs, w_in_g)


def _fwd_in_more(h1, w_in_g, proj, q_idx, ts, name, deps=()):
    s = h1.shape[0]

    def body(q_ref, h1_ref, w_ref, proj_in_ref, proj_ref):
        proj_ref[...] = _dot(h1_ref[...], w_ref[...]).astype(BF16)

    return pl.pallas_call(
        lambda q_ref, *refs: body(q_ref, *refs[len(deps):]),
        grid_spec=pltpu.PrefetchScalarGridSpec(
            num_scalar_prefetch=1, grid=(s // ts,),
            in_specs=[_ANY] * len(deps) + [pl.BlockSpec((ts, D), lambda i, q: (i, 0)),
                                           pl.BlockSpec((None, D, C_IN), lambda i, q: (q[0], 0, 0)), _ANY],
            out_specs=pl.BlockSpec((ts, C_IN), lambda i, q: (i, q[0]))),
        out_shape=jax.ShapeDtypeStruct((s, D_IN), BF16), input_output_aliases={len(deps) + 3: 0},
        compiler_params=_cparams(), name=name)(q_idx, *deps, h1, w_in_g, proj)


def _fwd_mix(proj, x, vecs, w_rga, w_rgx, w_out, ts, deps=()):
    s = x.shape[0]
    g = ts // TIME_BLOCKS

    def body(proj_ref, x_ref, v_ref, wa_ref, wx_ref, wo_ref, x1_ref, mg_ref, z1_ref, kept_ref, decay_ref,
             ua_buf, rx_buf, p_buf, q_buf, c_buf, hcarry):
        i = pl.program_id(0)

        @pl.when(i == 0)
        def _():
            ua_buf[...] = jnp.zeros(ua_buf.shape, F32)
            rx_buf[...] = jnp.zeros(rx_buf.shape, F32)
            hcarry[...] = jnp.zeros((8, D), F32)

        def seg(j):
            return proj_ref[:, j * D:(j + 1) * D].astype(F32)

        def vrow(j):
            return v_ref[j:j + 1, :]

        cb, cc, cx, rx, rg, ga, gb = (seg(j) for j in range(7))
        ua = cc * cx
        ua_late = _late_blocks(ua, ua_buf, g)
        rx_late = _late_blocks(rx, rx_buf, g)
        va = vrow(V_WA2) * ua + vrow(V_WA1) * _earlier(ua, 1, ua_late, g) + vrow(V_WA0) * _earlier(ua, 2, ua_late, g)
        u = (vrow(V_WB3) * rx + vrow(V_WB2) * _earlier(rx, 1, rx_late, g) + vrow(V_WB1) * _earlier(rx, 2, rx_late, g)
             + vrow(V_WB0) * _earlier(rx, 3, rx_late, g) + vrow(V_CBB))

        rows = lax.broadcasted_iota(jnp.int32, (ts, D), 0)
        row0 = jnp.logical_and(rows == 0, i == 0)
        r, ig, a, mult = _lru_gates(u, wa_ref, wx_ref, v_ref, row0)
        decay_ref[...] = a
        bx = mult * (ig * u)

        prods, sums = [a[0:g]], [bx[0:g]]
        for k in range(1, TIME_BLOCKS):
            ak = a[k * g:(k + 1) * g]
            sums.append(ak * sums[-1] + bx[k * g:(k + 1) * g])
            prods.append(ak * prods[-1])
        p_buf[...] = prods[-1]
        q_buf[...] = sums[-1]
        state = hcarry[0:1, :]
        for j in range(g):
            c_buf[j:j + 1, :] = state
            state = p_buf[j:j + 1, :] * state + q_buf[j:j + 1, :]
        hcarry[0:1, :] = state
        entering = c_buf[...]
        h = jnp.concatenate([sums[k] + prods[k] * entering for k in range(TIME_BLOCKS)], axis=0)

        gel, dgel = _gelu(rg)
        sga = jax.nn.sigmoid(ga)
        sgb = jax.nn.sigmoid(gb)
        for j, keep in enumerate((va, r, ig, sga, sgb, gel, dgel, mult, u, h)):
            kept_ref[:, j * D:(j + 1) * D] = keep.astype(BF16)
        merged = (sga * (cb * va) + sgb * (h * gel)).astype(BF16)
        mg_ref[...] = merged
        z1 = _dot(merged, wo_ref[...])
        z1_ref[...] = z1.astype(BF16)
        x1_ref[...] = x_ref[...] + vrow(V_GT1) * z1

    row = lambda i: (i, 0)
    return pl.pallas_call(
        _after(deps, body), grid=(s // ts,),
        out_shape=(jax.ShapeDtypeStruct((s, D), F32), jax.ShapeDtypeStruct((s, D), BF16), jax.ShapeDtypeStruct((s, D), BF16),
                   jax.ShapeDtypeStruct((s, N_KEPT * D), BF16), jax.ShapeDtypeStruct((s, D), F32)),
        in_specs=[_ANY] * len(deps) + [pl.BlockSpec((ts, D_IN), row), pl.BlockSpec((ts, D), row), _VMEM, _VMEM, _VMEM, _VMEM],
        out_specs=[pl.BlockSpec((ts, D), row)] * 3 + [pl.BlockSpec((ts, N_KEPT * D), row), pl.BlockSpec((ts, D), row)],
        scratch_shapes=[pltpu.VMEM((2, g + 8, D), F32), pltpu.VMEM((3, g + 8, D), F32), pltpu.VMEM((g, D), F32),
                        pltpu.VMEM((g, D), F32), pltpu.VMEM((g, D), F32), pltpu.VMEM((8, D), F32)],
        compiler_params=_cparams(), name="fwd_mix")(*deps, proj, x, vecs, w_rga, w_rgx, w_out)


def _ffn_loss(x1, target, vecs, w_gu_g, w_dn, ts):
    s = x1.shape[0]

    def body(x1_ref, t_ref, v_ref, wgu_ref, wdn_ref, dx1_ref, h2_ref, act_ref, dz2_ref, dgu_ref, sm_ref):
        @pl.when(pl.program_id(0) == 0)
        def _():
            sm_ref[...] = jnp.zeros((N_SMALL, D), F32)

        def vrow(j):
            return v_ref[j:j + 1, :]

        n_sub = 1
        rows = [slice(k * (ts // n_sub), (k + 1) * (ts // n_sub)) for k in range(n_sub)]
        subs = [dict(r=r, sums={}) for r in rows]

        def stage_norm(t):
            t["x1"] = x1_ref[t["r"], :]
            t["xh1"], t["rstd1"] = _rms(t["x1"])
            t["n2"] = t["xh1"] * vrow(V_GFFN)
            t["h2"] = (t["n2"] * (1.0 + vrow(V_SC2)) + vrow(V_SH2)).astype(BF16)
            h2_ref[t["r"], :] = t["h2"]

        def stage_up(t):
            h2 = t["h2"]
            g = jnp.concatenate([_dot(h2, wgu_ref[0]), _dot(h2, wgu_ref[1])], axis=1)
            t["up"] = jnp.concatenate([_dot(h2, wgu_ref[2]), _dot(h2, wgu_ref[3])], axis=1)
            t["g"] = g
            t["sg"] = jax.nn.sigmoid(g)
            t["silu"] = g * t["sg"]
            t["act"] = (t["silu"] * t["up"]).astype(BF16)
            act_ref[t["r"], :] = t["act"]

        def stage_down_loss(t):
            z2 = _dot(t["act"], wdn_ref[...])
            x2 = t["x1"] + vrow(V_GT2) * z2
            xh2, rstd2 = _rms(x2)
            err = xh2 * vrow(V_GFIN) - t_ref[t["r"], :]
            t["sums"][G_LOSS] = _rowsum((0.5 / D) * err * err)
            dy = err * (1.0 / D)
            t["sums"][G_GFIN] = _rowsum(dy * xh2)
            t["dx2"] = _rms_bwd(dy * vrow(V_GFIN), xh2, rstd2)
            t["sums"][G_GT2] = _rowsum(t["dx2"] * z2)
            t["dz2"] = (vrow(V_GT2) * t["dx2"]).astype(BF16)
            dz2_ref[t["r"], :] = t["dz2"]

        def stage_back_act(t):
            dact = _dot_nt(t["dz2"], wdn_ref[...])
            g, sg = t["g"], t["sg"]
            t["dgate"] = (dact * t["up"] * (sg * (1.0 + g * (1.0 - sg)))).astype(BF16)
            t["dup"] = (dact * t["silu"]).astype(BF16)
            dgu_ref[t["r"], 0:D_FF] = t["dgate"]
            dgu_ref[t["r"], D_FF:2 * D_FF] = t["dup"]

        def stage_back_norm(t):
            dgate, dup = t["dgate"], t["dup"]
            dh2 = (_dot_nt(dgate[:, 0:C_GU], wgu_ref[0]) + _dot_nt(dgate[:, C_GU:2 * C_GU], wgu_ref[1])
                   + _dot_nt(dup[:, 0:C_GU], wgu_ref[2]) + _dot_nt(dup[:, C_GU:2 * C_GU], wgu_ref[3]))
            t["sums"][G_SH2] = _rowsum(dh2)
            t["sums"][G_SC2] = _rowsum(dh2 * t["n2"])
            dn2 = dh2 * (1.0 + vrow(V_SC2))
            t["sums"][G_GFFN] = _rowsum(dn2 * t["xh1"])
            dx1_ref[t["r"], :] = t["dx2"] + _rms_bwd(dn2 * vrow(V_GFFN), t["xh1"], t["rstd1"])

        for stage in (stage_norm, stage_up, stage_down_loss, stage_back_act, stage_back_norm):
            for t in subs:
                stage(t)
        for j in subs[0]["sums"]:
            total = subs[0]["sums"][j]
            for t in subs[1:]:
                total = total + t["sums"][j]
            sm_ref[j:j + 1, :] += total

    row = lambda i: (i, 0)
    return pl.pallas_call(
        body, grid=(s // ts,),
        out_shape=(jax.ShapeDtypeStruct((s, D), F32), jax.ShapeDtypeStruct((s, D), BF16), jax.ShapeDtypeStruct((s, D_FF), BF16),
                   jax.ShapeDtypeStruct((s, D), BF16), jax.ShapeDtypeStruct((s, 2 * D_FF), BF16),
                   jax.ShapeDtypeStruct((N_SMALL, D), F32)),
        in_specs=[pl.BlockSpec((ts, D), row), pl.BlockSpec((ts, D), row), _VMEM, _VMEM, _VMEM],
        out_specs=[pl.BlockSpec((ts, D), row), pl.BlockSpec((ts, D), row), pl.BlockSpec((ts, D_FF), row),
                   pl.BlockSpec((ts, D), row), pl.BlockSpec((ts, 2 * D_FF), row), pl.BlockSpec((N_SMALL, D), lambda i: (0, 0))],
        compiler_params=_cparams(), name="ffn_loss")(x1, target, vecs, w_gu_g, w_dn)


def _bwd_mix(dx1, z1, merged, proj, kept, decay, vecs, w_rga, w_rgx, w_out, small, ts, deps=()):
    s = dx1.shape[0]
    nt = s // ts
    g = ts // TIME_BLOCKS
    assert g % 16 == 0

    def body(dx1_ref, z1_ref, mg_ref, proj_ref, kept_ref, decay_ref, hh_ref, v_ref, wa_ref, wx_ref,
             wo_ref, sm0_ref, dproj_ref, sm_ref, dwa_ref, dwx_ref, dwo_ref,
             h_buf, a_buf, dva_buf, du_buf, p_buf, q_buf, c_buf, lcarry):
        i = pl.program_id(0)
        first_tile = i == nt - 1

        @pl.when(i == 0)
        def _():
            a_buf[...] = jnp.zeros(a_buf.shape, F32)
            dva_buf[...] = jnp.zeros(dva_buf.shape, F32)
            du_buf[...] = jnp.zeros(du_buf.shape, F32)
            lcarry[...] = jnp.zeros((8, D), F32)
            sm_ref[...] = sm0_ref[...]
            dwa_ref[...] = jnp.zeros((HEADS, HB, HB), F32)
            dwx_ref[...] = jnp.zeros((HEADS, HB, HB), F32)
            dwo_ref[...] = jnp.zeros((D, D), F32)

        def seg(j):
            return proj_ref[:, j * D:(j + 1) * D].astype(F32)

        def vrow(j):
            return v_ref[j:j + 1, :]

        def acc(j, val):
            sm_ref[j:j + 1, :] += _rowsum(val)

        cb, cc, cx, rx = (seg(j) for j in range(4))
        ua = cc * cx
        va, r, ig, sga, sgb, gel, dgel, mult, u, h = (kept_ref[:, j * D:(j + 1) * D].astype(F32) for j in range(N_KEPT))
        a = decay_ref[...]
        rows = lax.broadcasted_iota(jnp.int32, (ts, D), 0)
        row0 = jnp.logical_and(rows == 0, first_tile)

        dx1 = dx1_ref[...]
        acc(G_GT1, dx1 * z1_ref[...].astype(F32))
        dz1 = (vrow(V_GT1) * dx1).astype(BF16)
        dwo_ref[...] += _dot_tn(mg_ref[...], dz1)
        dmg = _dot_nt(dz1, wo_ref[...])
        dya = dmg * sga
        dyb = dmg * sgb
        dproj_ref[:, 5 * D:6 * D] = (dya * (cb * va) * (1.0 - sga)).astype(BF16)
        dproj_ref[:, 6 * D:7 * D] = (dyb * (h * gel) * (1.0 - sgb)).astype(BF16)

        dproj_ref[:, 0:D] = (dya * va).astype(BF16)
        dva = dya * cb
        dva_early = _early_blocks(dva, dva_buf, g)
        dva1 = _later(dva, 1, dva_early, g)
        dva2 = _later(dva, 2, dva_early, g)
        dua = vrow(V_WA2) * dva + vrow(V_WA1) * dva1 + vrow(V_WA0) * dva2
        acc(G_WA2, ua * dva)
        acc(G_WA1, ua * dva1)
        acc(G_WA0, ua * dva2)
        dproj_ref[:, D:2 * D] = (dua * cx).astype(BF16)
        dproj_ref[:, 2 * D:3 * D] = (dua * cc).astype(BF16)

        dproj_ref[:, 4 * D:5 * D] = (dyb * h * dgel).astype(BF16)
        a_next = _later(a, 1, _early_blocks(a, a_buf, g), g)
        dh = dyb * gel
        last = TIME_BLOCKS - 1
        prods, sums = {last: a_next[last * g:]}, {last: dh[last * g:]}
        for k in range(last - 1, -1, -1):
            ak = a_next[k * g:(k + 1) * g]
            sums[k] = dh[k * g:(k + 1) * g] + ak * sums[k + 1]
            prods[k] = ak * prods[k + 1]
        p_buf[...] = prods[0]
        q_buf[...] = sums[0]
        state = lcarry[0:1, :]
        for j in range(g - 1, -1, -1):
            c_buf[j:j + 1, :] = state
            state = q_buf[j:j + 1, :] + p_buf[j:j + 1, :] * state
        lcarry[0:1, :] = state
        entering = c_buf[...]
        lam = jnp.concatenate([sums[k] + prods[k] * entering for k in range(TIME_BLOCKS)], axis=0)

        last = lax.broadcasted_iota(jnp.int32, hh_ref.shape, 0) == hh_ref.shape[0] - 1
        h_halo = [jnp.where(first_tile, 0.0, jnp.sum(jnp.where(last, hh_ref[...].astype(F32), 0.0), axis=0, keepdims=True))]
        da = lam * _earlier(h, 1, _late_blocks(h, h_buf, g, h_halo), g)
        dmult = jnp.where(row0, 0.0, lam * (ig * u))
        di = lam * mult * u
        du = lam * mult * ig
        dlog_a = da * a - dmult * (a * a) / mult
        lam_p = vrow(V_LAM)
        dr = dlog_a * (LRU_C * _log_sigmoid(lam_p))
        sm_ref[G_LAM:G_LAM + 1, :] += _rowsum(dlog_a * r) * (LRU_C * jax.nn.sigmoid(-lam_p))
        dpa = dr * r * (1.0 - r)
        dpx = di * ig * (1.0 - ig)
        acc(G_BA, dpa)
        acc(G_BX, dpx)
        dpab = dpa.astype(BF16)
        dpxb = dpx.astype(BF16)
        ub = u.astype(BF16)
        back = []
        for hd in range(HEADS):
            cols = slice(hd * HB, (hd + 1) * HB)
            back.append(_dot_nt(dpab[:, cols], wa_ref[hd]) + _dot_nt(dpxb[:, cols], wx_ref[hd]))
            dwa_ref[hd] += _dot_tn(ub[:, cols], dpab[:, cols])
            dwx_ref[hd] += _dot_tn(ub[:, cols], dpxb[:, cols])
        du = du + jnp.concatenate(back, axis=1)

        acc(G_CBB, du)
        du_early = _early_blocks(du, du_buf, g)
        du1 = _later(du, 1, du_early, g)
        du2 = _later(du, 2, du_early, g)
        du3 = _later(du, 3, du_early, g)
        dproj_ref[:, 3 * D:4 * D] = (vrow(V_WB3) * du + vrow(V_WB2) * du1 + vrow(V_WB1) * du2 + vrow(V_WB0) * du3).astype(BF16)
        acc(G_WB3, rx * du)
        acc(G_WB2, rx * du1)
        acc(G_WB1, rx * du2)
        acc(G_WB0, rx * du3)

    rev = lambda i: (nt - 1 - i, 0)
    h_halo16 = lambda i: (jnp.maximum((nt - 1 - i) * (ts // 16) - 1, 0), N_KEPT - 1)
    const2 = lambda i: (0, 0)
    const3 = lambda i: (0, 0, 0)
    return pl.pallas_call(
        _after(deps, body), grid=(nt,),
        out_shape=(jax.ShapeDtypeStruct((s, D_IN), BF16), jax.ShapeDtypeStruct((N_SMALL, D), F32),
                   jax.ShapeDtypeStruct((HEADS, HB, HB), F32), jax.ShapeDtypeStruct((HEADS, HB, HB), F32),
                   jax.ShapeDtypeStruct((D, D), F32)),
        in_specs=[_ANY] * len(deps) + [pl.BlockSpec((ts, D), rev), pl.BlockSpec((ts, D), rev), pl.BlockSpec((ts, D), rev),
                  pl.BlockSpec((ts, 4 * D), rev), pl.BlockSpec((ts, N_KEPT * D), rev), pl.BlockSpec((ts, D), rev),
                  pl.BlockSpec((16, D), h_halo16), _VMEM, _VMEM, _VMEM, _VMEM, _VMEM],
        out_specs=[pl.BlockSpec((ts, D_IN), rev), pl.BlockSpec((N_SMALL, D), const2),
                   pl.BlockSpec((HEADS, HB, HB), const3), pl.BlockSpec((HEADS, HB, HB), const3), pl.BlockSpec((D, D), const2)],
        scratch_shapes=[pltpu.VMEM((1, g + 8, D), F32), pltpu.VMEM((1, g + 8, D), F32),
                        pltpu.VMEM((2, g + 8, D), F32), pltpu.VMEM((3, g + 8, D), F32), pltpu.VMEM((g, D), F32),
                        pltpu.VMEM((g, D), F32), pltpu.VMEM((g, D), F32), pltpu.VMEM((8, D), F32)],
        compiler_params=_cparams(), name="bwd_mix")(*deps, dx1, z1, merged, proj, kept, decay, kept, vecs, w_rga,
                                                    w_rgx, w_out, small)


def _bwd_in(dproj, x, dx1, vecs, w_in_g, small, ts, deps=()):
    s = x.shape[0]

    def body(dp_ref, x_ref, dx1_ref, v_ref, w_ref, sm0_ref, gx_ref, sm_ref):
        @pl.when(pl.program_id(0) == 0)
        def _():
            sm_ref[...] = sm0_ref[...]

        def vrow(j):
            return v_ref[j:j + 1, :]

        dh1 = _dot_nt(dp_ref[:, 0:C_IN], w_ref[0])
        for k in range(1, N_CHIPS):
            dh1 += _dot_nt(dp_ref[:, k * C_IN:(k + 1) * C_IN], w_ref[k])
        xh, rstd = _rms(x_ref[...])
        sm_ref[G_SH1:G_SH1 + 1, :] += _rowsum(dh1)
        sm_ref[G_SC1:G_SC1 + 1, :] += _rowsum(dh1 * (xh * vrow(V_GMIX)))
        dn1 = dh1 * (1.0 + vrow(V_SC1))
        sm_ref[G_GMIX:G_GMIX + 1, :] += _rowsum(dn1 * xh)
        gx_ref[...] = dx1_ref[...] + _rms_bwd(dn1 * vrow(V_GMIX), xh, rstd)

    row = lambda i: (i, 0)
    return pl.pallas_call(
        _after(deps, body), grid=(s // ts,),
        out_shape=(jax.ShapeDtypeStruct((s, D), F32), jax.ShapeDtypeStruct((N_SMALL, D), F32)),
        in_specs=[_ANY] * len(deps) + [pl.BlockSpec((ts, D_IN), row), pl.BlockSpec((ts, D), row), pl.BlockSpec((ts, D), row),
                                       _VMEM, _VMEM, _VMEM],
        out_specs=[pl.BlockSpec((ts, D), row), pl.BlockSpec((N_SMALL, D), lambda i: (0, 0))],
        compiler_params=_cparams(), name="bwd_in")(*deps, dproj, x, dx1, vecs, w_in_g, small)


def _grad_w(a, b, n_col_blocks, ts, name, deps=()):
    s, m = a.shape
    tn = b.shape[1] // n_col_blocks
    n_steps = s // ts

    def body(a_ref, b_ref, o_ref, acc_ref):
        k = pl.program_id(1)

        @pl.when(k == 0)
        def _():
            acc_ref[...] = jnp.zeros((m, tn), F32)

        acc_ref[...] += _dot_tn(a_ref[...], b_ref[...])

        @pl.when(k == n_steps - 1)
        def _():
            o_ref[...] = acc_ref[...].astype(BF16)

    return pl.pallas_call(
        _after(deps, body), grid=(n_col_blocks, n_steps),
        out_shape=jax.ShapeDtypeStruct((n_col_blocks, m, tn), BF16),
        in_specs=[_ANY] * len(deps) + [pl.BlockSpec((ts, m), lambda n, k: (k, 0)), pl.BlockSpec((ts, tn), lambda n, k: (k, n))],
        out_specs=pl.BlockSpec((None, m, tn), lambda n, k: (n, 0, 0)),
        scratch_shapes=[pltpu.VMEM((m, tn), F32)],
        compiler_params=_cparams(2), name=name)(*deps, a, b)


def _ada_fwd(c_all, w_ada, b_ada):
    n = w_ada.shape[1]

    def body(c_ref, w_ref, b_ref, o_ref, ca_ref):
        c = c_ref[...]
        ca = c * jax.nn.sigmoid(c)
        ca_ref[...] = ca
        o_ref[...] = jnp.dot(ca, w_ref[...], preferred_element_type=F32, precision=lax.Precision.HIGHEST) + b_ref[...]

    return pl.pallas_call(
        body, out_shape=(jax.ShapeDtypeStruct((N_DEV, n), F32), jax.ShapeDtypeStruct((N_DEV, D), F32)),
        in_specs=[_VMEM] * 3, out_specs=[_VMEM] * 2, compiler_params=_cparams(0), name="ada_fwd")(c_all, w_ada, b_ada)


def _sum_small(parts):
    def body(p_ref, o_ref, d_ref):
        tot = p_ref[0]
        for dev in range(1, N_DEV):
            tot = tot + p_ref[dev]
        o_ref[...] = tot
        d_ref[...] = p_ref[:, 0:8, :]

    return pl.pallas_call(
        body, out_shape=(jax.ShapeDtypeStruct((N_SMALL, D), F32), jax.ShapeDtypeStruct((N_DEV, 8, D), F32)),
        in_specs=[_VMEM], out_specs=[_VMEM] * 2, compiler_params=_cparams(0), name="sum_small")(parts)


def _adamw_small(items, name):
    n = len(items)

    def body(*refs):
        ins, outs = refs[:4 * n], refs[4 * n:]
        for k in range(n):
            w_ref, g_ref, m_ref, v_ref = ins[4 * k:4 * k + 4]
            d_ref, nm_ref, nv_ref = outs[3 * k:3 * k + 3]
            g_ = g_ref[...]
            m_ = ADAM_B1 * m_ref[...] + (1.0 - ADAM_B1) * g_
            v_ = ADAM_B2 * v_ref[...] + (1.0 - ADAM_B2) * (g_ * g_)
            nm_ref[...] = m_
            nv_ref[...] = v_
            m_hat = m_ / (1.0 - ADAM_B1 ** ADAM_STEP)
            v_hat = v_ / (1.0 - ADAM_B2 ** ADAM_STEP)
            d_ref[...] = -ADAM_LR * (m_hat / (jnp.sqrt(v_hat) + ADAM_EPS) + ADAM_WD * w_ref[...])

    out = pl.pallas_call(
        body, out_shape=tuple(jax.ShapeDtypeStruct(it[0].shape, F32) for it in items for _ in range(3)),
        in_specs=[_VMEM] * (4 * n), out_specs=[_VMEM] * (3 * n), name=name)(*[a for it in items for a in it])
    return [tuple(out[3 * k:3 * k + 3]) for k in range(n)]


def _adamw_halves(w, mine, other, m, v, c_idx, name, deps=()):
    r2, cols = mine.shape
    tr = next(t for t in (128, 64, 32, 16, 8) if r2 % t == 0)
    nh = r2 // tr

    def body(c_ref, w_ref, mine_ref, other_ref, m_ref, v_ref, g_ref, d_ref, nm_ref, nv_ref):
        g_ = jnp.where(pl.program_id(0) // nh == c_ref[0], mine_ref[...], other_ref[...])
        g_ref[...] = g_
        m_ = ADAM_B1 * m_ref[...] + (1.0 - ADAM_B1) * g_
        v_ = ADAM_B2 * v_ref[...] + (1.0 - ADAM_B2) * (g_ * g_)
        nm_ref[...] = m_
        nv_ref[...] = v_
        m_hat = m_ / (1.0 - ADAM_B1 ** ADAM_STEP)
        v_hat = v_ / (1.0 - ADAM_B2 ** ADAM_STEP)
        d_ref[...] = -ADAM_LR * (m_hat / (jnp.sqrt(v_hat) + ADAM_EPS) + ADAM_WD * w_ref[...])

    full = pl.BlockSpec((tr, cols), lambda i, c: (i, 0))
    mine_spec = pl.BlockSpec((tr, cols), lambda i, c: (jnp.clip(i - c[0] * nh, 0, nh - 1), 0))
    other_spec = pl.BlockSpec((tr, cols), lambda i, c: (jnp.clip(i - (1 - c[0]) * nh, 0, nh - 1), 0))
    return pl.pallas_call(
        lambda c_ref, *refs: body(c_ref, *refs[len(deps):]),
        grid_spec=pltpu.PrefetchScalarGridSpec(
            num_scalar_prefetch=1, grid=(2 * nh,),
            in_specs=[_ANY] * len(deps) + [full, mine_spec, other_spec, full, full], out_specs=[full] * 4),
        out_shape=(jax.ShapeDtypeStruct((2 * r2, cols), F32),) * 4, compiler_params=_cparams(), name=name,
    )(c_idx, *deps, w, mine, other, m, v)


def _adamw_ada(w, c_act, dmod, m, v):
    rows, n = w.shape
    tr = 128

    def body(c_ref, d_ref, w_ref, m_ref, v_ref, g_ref, dl_ref, nm_ref, nv_ref):
        g_ = lax.dot_general(c_ref[...], d_ref[...], (((0,), (0,)), ((), ())), preferred_element_type=F32,
                             precision=lax.Precision.HIGHEST)
        g_ref[...] = g_
        m_ = ADAM_B1 * m_ref[...] + (1.0 - ADAM_B1) * g_
        v_ = ADAM_B2 * v_ref[...] + (1.0 - ADAM_B2) * (g_ * g_)
        nm_ref[...] = m_
        nv_ref[...] = v_
        m_hat = m_ / (1.0 - ADAM_B1 ** ADAM_STEP)
        v_hat = v_ / (1.0 - ADAM_B2 ** ADAM_STEP)
        dl_ref[...] = -ADAM_LR * (m_hat / (jnp.sqrt(v_hat) + ADAM_EPS) + ADAM_WD * w_ref[...])

    spec = pl.BlockSpec((tr, n), lambda i: (i, 0))
    return pl.pallas_call(
        body, grid=(rows // tr,), out_shape=(jax.ShapeDtypeStruct((rows, n), F32),) * 4,
        in_specs=[pl.BlockSpec((N_DEV, tr), lambda i: (0, i)), _VMEM, spec, spec, spec], out_specs=[spec] * 4,
        compiler_params=_cparams(), name="adamw_w_ada")(c_act, dmod, w, m, v)


def _add_halves(grads, recvs, c_idx, name):
    nw = len(grads)

    def body(c_ref, *refs):
        for g_ref, r_ref, o_ref in zip(refs[:nw], refs[nw:2 * nw], refs[2 * nw:]):
            o_ref[...] = (g_ref[...].astype(F32) + r_ref[...].astype(F32)).astype(BF16)

    mine = [pl.BlockSpec((None, None) + g.shape[2:], lambda k, c: (k, c[0], 0, 0)) for g in grads]
    whole = [pl.BlockSpec((None,) + g.shape[2:], lambda k, c: (k, 0, 0)) for g in grads]
    return pl.pallas_call(
        body,
        grid_spec=pltpu.PrefetchScalarGridSpec(num_scalar_prefetch=1, grid=(N_CHIPS,), in_specs=mine + whole, out_specs=whole),
        out_shape=tuple(jax.ShapeDtypeStruct((N_CHIPS,) + g.shape[2:], BF16) for g in grads),
        compiler_params=_cparams(), name=name)(c_idx, *grads, *recvs)


def _sum_chips(own, others, chip_idx, name):
    _, r2, cols = own.shape
    tr = next(t for t in (128, 176, 64, 32, 16) if r2 % t == 0)

    def body(p_ref, own_ref, o_ref_in, o_ref):
        o_ref[...] = (((own_ref[...].astype(F32) + o_ref_in[0].astype(F32)) + o_ref_in[1].astype(F32))
                      + o_ref_in[2].astype(F32))

    return pl.pallas_call(
        body,
        grid_spec=pltpu.PrefetchScalarGridSpec(
            num_scalar_prefetch=1, grid=(r2 // tr,),
            in_specs=[pl.BlockSpec((None, tr, cols), lambda i, p: (p[0], i, 0)),
                      pl.BlockSpec((N_CHIPS - 1, tr, cols), lambda i, p: (0, i, 0))],
            out_specs=pl.BlockSpec((tr, cols), lambda i, p: (i, 0))),
        out_shape=jax.ShapeDtypeStruct((r2, cols), F32), compiler_params=_cparams(), name=name)(chip_idx, own, others)


def _place():
    x, y, c = lax.axis_index("x"), lax.axis_index("y"), lax.axis_index("c")
    return x, y, c, 2 * x + y


def _flip(v, bit):
    return 1 - v if bit else v


def _allgather8(v, name, deps=()):
    r, n = v.shape

    def body(*refs):
        v_ref, out_ref, send_sems, recv_sems, local_sem = refs[len(deps):]
        x, y, c, _ = _place()
        me = 4 * x + 2 * y + c
        mine = pltpu.make_async_copy(v_ref, out_ref.at[me], local_sem)
        mine.start()
        sends = []
        for rel in range(1, N_DEV):
            peer = (_flip(x, rel & 4), _flip(y, rel & 2), _flip(c, rel & 1))
            cp = pltpu.make_async_remote_copy(v_ref, out_ref.at[me], send_sems.at[rel - 1], recv_sems.at[rel - 1],
                                              device_id=peer, device_id_type=MESH)
            cp.start()
            sends.append(cp)
        for rel in range(1, N_DEV):
            peer = (_flip(x, rel & 4), _flip(y, rel & 2), _flip(c, rel & 1))
            peer_idx = 4 * peer[0] + 2 * peer[1] + peer[2]
            pltpu.make_async_remote_copy(v_ref, out_ref.at[peer_idx], send_sems.at[rel - 1], recv_sems.at[rel - 1],
                                         device_id=peer, device_id_type=MESH).wait_recv()
        for cp in sends:
            cp.wait_send()
        mine.wait()

    return pl.pallas_call(
        body, out_shape=jax.ShapeDtypeStruct((N_DEV, r, n), F32), in_specs=[_ANY] * len(deps) + [_VMEM], out_specs=_VMEM,
        scratch_shapes=[pltpu.SemaphoreType.DMA((N_DEV - 1,)), pltpu.SemaphoreType.DMA((N_DEV - 1,)), pltpu.SemaphoreType.DMA(())],
        name=name)(*deps, v)


_HBM = pl.BlockSpec(memory_space=pltpu.HBM)
_SEM = pl.BlockSpec(memory_space=pltpu.SEMAPHORE)
_EFFECT = pltpu.SideEffectType.DATAFLOW_SIDE_EFFECTING


def _xchg_start(name, plan, n_copies, srcs, lands, after=(), sibling_id=None):
    bufs = list(srcs) + list(lands)
    ns, nb = len(srcs), len(srcs) + len(lands)

    def body(*refs):
        send_sems, recv_sems, token = refs[nb + len(after)], refs[nb + len(after) + 1], refs[-1]
        if sibling_id is not None:
            x, y, c, _ = _place()
            barrier = pltpu.get_barrier_semaphore()
            pl.semaphore_signal(barrier, inc=1, device_id=(x, y, 1 - c), device_id_type=MESH)
            pl.semaphore_wait(barrier, 1)
        for i, (src, dst, peer, _) in enumerate(plan(_place(), refs[:ns], refs[ns:nb])):
            pltpu.make_async_remote_copy(src, dst, send_sems.at[i], recv_sems.at[i], device_id=peer, device_id_type=MESH).start()
        token[...] = jnp.zeros_like(token)

    out = pl.pallas_call(
        body, name=name,
        out_shape=(pltpu.SemaphoreType.DMA((n_copies,)), pltpu.SemaphoreType.DMA((n_copies,)),
                   *[pltpu.HBM(a.shape, a.dtype) for a in bufs], jax.ShapeDtypeStruct((8, 128), F32)),
        in_specs=[_HBM] * nb + [_ANY] * len(after), out_specs=(_SEM, _SEM, *[_HBM] * nb, _VMEM),
        input_output_aliases={i: 2 + i for i in range(nb)},
        compiler_params=pltpu.CompilerParams(has_side_effects=_EFFECT, collective_id=sibling_id),
    )(*[pltpu.with_memory_space_constraint(a, pltpu.HBM) for a in bufs], *after)
    return (out[0], out[1]), out[2:2 + ns], out[2 + ns:2 + nb], out[-1]


def _xchg_wait(name, plan, sems, srcs, lands, after, sem_ids=None):
    bufs = list(srcs) + list(lands)
    ns, nb = len(srcs), len(srcs) + len(lands)

    def body(*refs):
        send_sems, recv_sems = refs[nb], refs[nb + 1]
        copies = plan(_place(), refs[:ns], refs[ns:nb])
        ids = range(len(copies)) if sem_ids is None else sem_ids
        for i, (src, _, peer, mine) in zip(ids, copies, strict=True):
            if i is not None:
                cp = pltpu.make_async_remote_copy(src, mine, send_sems.at[i], recv_sems.at[i], device_id=peer,
                                                  device_id_type=MESH)
                cp.wait_send()
                cp.wait_recv()

    out = pl.pallas_call(
        body, name=name, out_shape=tuple(pltpu.HBM(a.shape, a.dtype) for a in bufs),
        in_specs=[_HBM] * nb + [_SEM, _SEM] + [_ANY] * len(after), out_specs=tuple([_HBM] * nb),
        input_output_aliases={i: i for i in range(nb)},
        compiler_params=pltpu.CompilerParams(has_side_effects=_EFFECT),
    )(*bufs, *sems, *after)
    return out[:ns], out[ns:]


def _other_chips(place, which=(1, 2, 3)):
    x, y, c, _ = place
    return [((_flip(x, j & 2), _flip(y, j & 1), c), 2 * _flip(x, j & 2) + _flip(y, j & 1)) for j in which]


def _plan_gather_ici(chips):
    def plan(place, src_refs, land_refs):
        _, _, c, p = place
        return [(s.at[c], l.at[p, c], peer, l.at[q, c]) for s, l, which in zip(src_refs, land_refs, chips, strict=True)
                for peer, q in _other_chips(place, which)]
    return plan


def _plan_relay(which):
    def plan(place, src_refs, land_refs):
        x, y, c, _ = place
        return [(l.at[q, c], l.at[q, c], (x, y, 1 - c), l.at[q, 1 - c]) for l in land_refs for _, q in _other_chips(place, which)]
    return plan


def _plan_swap(place, src_refs, land_refs):
    x, y, c, _ = place
    return [(s.at[k, 1 - c], l.at[k], (x, y, 1 - c), l.at[k]) for s, l in zip(src_refs, land_refs) for k in range(N_CHIPS)]


def _plan_scatter(place, src_refs, land_refs):
    return [(s.at[q], l.at[j], peer, l.at[j]) for s, l in zip(src_refs, land_refs)
            for j, (peer, q) in enumerate(_other_chips(place))]


def _plan_share(place, src_refs, land_refs):
    x, y, c, _ = place
    return [(s, l, (x, y, 1 - c), l) for s, l in zip(src_refs, land_refs)]


def _pack_rows(parts, n_rows, name, deps=()):
    def body(*refs):
        refs = refs[len(deps):]
        out_ref = refs[-1]
        out_ref[...] = jnp.zeros((n_rows, D), F32)
        at = 0
        for ref in refs[:-1]:
            k = ref.shape[0]
            out_ref[at:at + k, :] = ref[...]
            at += k

    return pl.pallas_call(
        body, out_shape=jax.ShapeDtypeStruct((n_rows, D), F32), in_specs=[_ANY] * len(deps) + [_VMEM] * len(parts),
        out_specs=_VMEM, name=name)(*deps, *parts)


TS_MM = 512
TS_IN = 1024
TS_GW = 1024
TS_MIX = 256


def _halved(a):
    n, r, cols = a.shape
    return a.reshape(n, 2, r // 2, cols)


SIBLING_IDS = (1, 2)


def _rs_swap(name, grads, after=()):
    lands = [lax.empty((N_CHIPS,) + g.shape[2:], g.dtype) for g in grads]
    sems, grads, lands, token = _xchg_start(name + "_swap", _plan_swap, N_CHIPS * len(grads), grads, lands, after,
                                            sibling_id=SIBLING_IDS[0])
    return name, sems, grads, lands, token


def _rs_scatter(handle, after, chip, ci):
    name, sems, grads, lands, _ = handle
    grads, from_sibling = _xchg_wait(name + "_swap_wait", _plan_swap, sems, grads, lands, after)
    c_arr = jnp.reshape(ci, (1,)).astype(jnp.int32)
    pair_sums = _add_halves(list(grads), list(from_sibling), c_arr, name + "_add_halves")
    lands = [lax.empty((N_CHIPS - 1,) + p.shape[1:], p.dtype) for p in pair_sums]
    sems, pair_sums, lands, token = _xchg_start(name + "_scatter", _plan_scatter, 3 * len(pair_sums), pair_sums, lands)
    return name, sems, pair_sums, lands, jnp.reshape(chip, (1,)).astype(jnp.int32), token


def _rs_share(handle, after):
    name, sems, pair_sums, lands, chip_idx, _ = handle
    pair_sums, received = _xchg_wait(name + "_scatter_wait", _plan_scatter, sems, pair_sums, lands, after)
    halves = [_sum_chips(p, r, chip_idx, "%s_sum_chips_%d" % (name, k)) for k, (p, r) in enumerate(zip(pair_sums, received))]
    lands = [lax.empty(h.shape, h.dtype) for h in halves]
    sems, halves, lands, token = _xchg_start(name + "_share", _plan_share, len(halves), halves, lands,
                                             sibling_id=SIBLING_IDS[1])
    return name, sems, halves, lands, token


def _rs_end(handle, after):
    name, sems, halves, lands, _ = handle
    halves, others = _xchg_wait(name + "_share_wait", _plan_share, sems, halves, lands, after)
    return list(zip(halves, others))


def kernel(x, c, w_ada, b_ada, g_norm_mix, w_in, conv_a_w, conv_b_w, conv_b_bias, w_rg_a, b_rg_a, w_rg_x, b_rg_x, lru_lambda, w_out, g_norm_ffn, w_gate_up, w_down, g_norm_final, loss_target, m_w_ada, m_b_ada, m_g_norm_mix, m_w_in, m_conv_a_w, m_conv_b_w, m_conv_b_bias, m_w_rg_a, m_b_rg_a, m_w_rg_x, m_b_rg_x, m_lru_lambda, m_w_out, m_g_norm_ffn, m_w_gate_up, m_w_down, m_g_norm_final, v_w_ada, v_b_ada, v_g_norm_mix, v_w_in, v_conv_a_w, v_conv_b_w, v_conv_b_bias, v_w_rg_a, v_b_rg_a, v_w_rg_x, v_b_rg_x, v_lru_lambda, v_w_out, v_g_norm_ffn, v_w_gate_up, v_w_down, v_g_norm_final):
    xi, yi, ci = lax.axis_index("x"), lax.axis_index("y"), lax.axis_index("c")
    chip = 2 * xi + yi
    me = 2 * chip + ci
    n_ada = w_ada.shape[2]

    def widen(w):
        return jnp.pad(w, ((0, 0), (0, D - w.shape[1])))

    got = _allgather8(_pack_rows([c, widen(conv_a_w[0]), widen(conv_b_w[0])], 8, "pack_c_conv"), "gather_c_conv")
    c_all = got[:, 0, :]
    conv_full = got[::2, 1:8, :D // N_CHIPS].transpose(1, 0, 2).reshape(7, D)

    mod_part, c_act = _ada_fwd(c_all, w_ada[0], lax.dynamic_slice_in_dim(b_ada, chip * n_ada, n_ada, axis=1))
    mod_all = _allgather8(mod_part, "gather_mod")
    mod_mine = lax.dynamic_index_in_dim(mod_all, me, axis=1, keepdims=False)[::2].reshape(6, D)
    vecs = _pack_rows([mod_mine, g_norm_mix, g_norm_ffn, g_norm_final.reshape(1, D), conv_b_bias, b_rg_a, b_rg_x, lru_lambda,
                       conv_full], N_VEC, "pack_vecs")

    def rg_shard(w):
        return w[0].astype(BF16).reshape(2, HEADS * HB // N_CHIPS // 2, HB)

    shards = [w_in[0].astype(BF16).reshape(2, D // 2, C_IN), rg_shard(w_rg_a), rg_shard(w_rg_x),
              w_out[0].astype(BF16).reshape(2, D // N_CHIPS // 2, D), w_gate_up[0].astype(BF16).reshape(2, D // 2, C_GU),
              w_down[0].astype(BF16).reshape(2, D_FF // N_CHIPS // 2, D)]
    lands = [lax.dynamic_update_index_in_dim(lax.empty((N_CHIPS,) + s.shape, s.dtype), s, chip, 0) for s in shards]

    def send(name, first, last, after, chips):
        copies = [(k, j) for k, which in zip(range(first, last), chips, strict=True) for j in which]
        sems, srcs, zone, token = _xchg_start(name + "_ici", _plan_gather_ici(chips), len(copies), shards[first:last],
                                              lands[first:last], after)
        shards[first:last], lands[first:last] = srcs, zone
        return sems, copies, token

    def arrive(name, sent, first, last, after):
        sems, copies, _ = sent
        chips = [tuple(j for k, j in copies if k == want) for want in range(first, last)]
        ids = [copies.index((k, j)) for k, which in zip(range(first, last), chips) for j in which]
        srcs, zone = _xchg_wait(name + "_ici_wait", _plan_gather_ici(chips), sems, shards[first:last], lands[first:last], after,
                                ids)
        shards[first:last], lands[first:last] = srcs, zone

    def relay(name, first, last, which, sibling_id):
        plan = _plan_relay(which)
        sems, _, zone, token = _xchg_start(name + "_d2d", plan, len(which) * (last - first), [], lands[first:last],
                                           sibling_id=sibling_id)
        lands[first:last] = zone
        return name, plan, sems, first, last, token

    def relayed(handle, after):
        name, plan, sems, first, last, _ = handle
        lands[first:last] = _xchg_wait(name + "_d2d_wait", plan, sems, [], lands[first:last], after)[1]

    def to_blocks(v):
        return v.reshape(-1, TS_MIX // TIME_BLOCKS, TIME_BLOCKS, D).transpose(0, 2, 1, 3).reshape(v.shape)

    def from_blocks(v):
        return v.reshape(-1, TIME_BLOCKS, TS_MIX // TIME_BLOCKS, D).transpose(0, 2, 1, 3).reshape(v.shape)

    def chip_index(j):
        return jnp.reshape(chip ^ j, (1,)).astype(jnp.int32)

    def wg_in():
        return lands[0].reshape(N_CHIPS, D, C_IN)

    xs, target = to_blocks(x[0]), to_blocks(loss_target[0])
    sent_near = send("gather_in_near", 0, 1, [vecs], [(1, 2)])
    ts_in = min(TS_IN, xs.shape[0])
    h1, proj = _fwd_in_first(xs, vecs, wg_in(), chip_index(0), ts_in, deps=[sent_near[-1]])
    arrive("gather_in_near", sent_near, 0, 1, [proj])
    near = relay("gather_in_near", 0, 1, (1, 2), SIBLING_IDS[0])
    sent_rest = send("gather_rest", 0, 6, [near[-1]], [(3,)] + [(1, 2, 3)] * 5)
    relayed(near, [sent_rest[-1]])
    proj = _fwd_in_more(h1, wg_in(), proj, chip_index(1), ts_in, "fwd_in_y")
    proj = _fwd_in_more(h1, wg_in(), proj, chip_index(2), ts_in, "fwd_in_x")
    arrive("gather_in_far", sent_rest, 0, 1, [proj])
    far = relay("gather_in_far", 0, 1, (3,), SIBLING_IDS[1])
    arrive("gather_mix", sent_rest, 1, 4, [far[-1]])
    relayed(far, [far[-1]])
    mix = relay("gather_mix", 1, 4, (1, 2, 3), SIBLING_IDS[0])
    proj = _fwd_in_more(h1, wg_in(), proj, chip_index(3), ts_in, "fwd_in_xy", deps=[mix[-1]])
    relayed(mix, [proj])
    wg_rga, wg_rgx, wg_out = lands[1:4]
    wg_out = wg_out.reshape(D, D)

    def rg_full(wg):
        return wg.reshape(N_CHIPS, HEADS, HB // N_CHIPS, HB).transpose(1, 0, 2, 3).reshape(HEADS, HB, HB)

    wg_rga, wg_rgx = rg_full(wg_rga), rg_full(wg_rgx)

    arrive("gather_ffn", sent_rest, 4, 6, [proj])
    ffn = relay("gather_ffn", 4, 6, (1, 2, 3), SIBLING_IDS[1])
    x1, merged, z1, kept, decay = _fwd_mix(proj, xs, vecs, wg_rga, wg_rgx, wg_out, TS_MIX, deps=[ffn[-1]])
    relayed(ffn, [x1])
    wg_gu, wg_dn = lands[4:6]
    wg_gu, wg_dn = wg_gu.reshape(N_CHIPS, D, C_GU), wg_dn.reshape(D_FF, D)
    dx1, h2, act, dz2, dgu, sm_ffn = _ffn_loss(x1, target, vecs, wg_gu, wg_dn, TS_MIX)

    def rg_chunks(dw):
        return _halved(dw.reshape(HEADS, N_CHIPS, HB // N_CHIPS, HB).transpose(1, 0, 2, 3).reshape(N_CHIPS, HB, HB).astype(BF16))

    ts_gw = min(TS_GW, xs.shape[0])
    g_dn = _grad_w(act, dz2, 1, ts_gw, "grad_w_down")
    g_gu = _grad_w(h2, dgu, N_CHIPS, ts_gw, "grad_w_gate_up")
    rs_b = _rs_swap("rs_b", [_halved(g_gu), _halved(g_dn.reshape(N_CHIPS, D_FF // N_CHIPS, D))])
    dproj, sm_mix, dw_rga, dw_rgx, dw_out = _bwd_mix(dx1, z1, merged, proj, kept, decay, vecs, wg_rga, wg_rgx, wg_out, sm_ffn, TS_MIX,
                                                     deps=[rs_b[-1]])
    rs_b = _rs_scatter(rs_b, [dproj], chip, ci)
    g_in = _grad_w(h1, dproj, N_CHIPS, ts_gw, "grad_w_in", deps=[rs_b[-1]])
    rs_b = _rs_share(rs_b, [g_in])
    rs_a = _rs_swap("rs_a", [_halved(g_in), rg_chunks(dw_rga), rg_chunks(dw_rgx),
                             _halved(dw_out.astype(BF16).reshape(N_CHIPS, D // N_CHIPS, D))], after=[rs_b[-1]])

    c_arr = jnp.reshape(ci, (1,)).astype(jnp.int32)

    def step_halves(name, w, halves, m, v, deps=()):
        shape = w.shape
        two_d = (-1, shape[-1])
        out = _adamw_halves(w.reshape(two_d), halves[0], halves[1], m.reshape(two_d), v.reshape(two_d), c_arr, "adamw_" + name, deps)
        return tuple(a.reshape(shape) for a in out)

    def shard_cols(row_block):
        return lax.dynamic_slice_in_dim(row_block, chip * (D // N_CHIPS), D // N_CHIPS, axis=1)

    gw_gu, gw_dn = _rs_end(rs_b, [rs_a[-1]])
    res = {
        "w_gate_up": step_halves("w_gate_up", w_gate_up, gw_gu, m_w_gate_up, v_w_gate_up, [rs_a[-1]]),
        "w_down": step_halves("w_down", w_down, gw_dn, m_w_down, v_w_down, [rs_a[-1]]),
    }
    rs_a = _rs_scatter(rs_a, [res["w_gate_up"][1], res["w_down"][1]], chip, ci)
    grad_x, sm_in = _bwd_in(dproj, xs, dx1, vecs, wg_in(), sm_mix, TS_MM, deps=[rs_a[-1]])
    rs_a = _rs_share(rs_a, [grad_x])

    small, per_dev = _sum_small(_allgather8(sm_in, "gather_small", deps=[rs_a[-1]]))
    dmod_all = per_dev[:, 0:6, :].reshape(N_DEV, 6 * D)
    grad_b_ada = small[0:6].reshape(1, 6 * D)
    res["w_ada"] = tuple(a[None] for a in _adamw_ada(w_ada[0], c_act, lax.dynamic_slice_in_dim(dmod_all, chip * n_ada, n_ada, axis=1),
                                                     m_w_ada[0], v_w_ada[0]))
    small_sets = {
        "b_ada": (b_ada.reshape(6, D), grad_b_ada.reshape(6, D), m_b_ada.reshape(6, D), v_b_ada.reshape(6, D)),
        "g_norm_mix": (g_norm_mix, small[G_GMIX:G_GMIX + 1], m_g_norm_mix, v_g_norm_mix),
        "conv_a_w": (conv_a_w[0], shard_cols(small[G_WA0:G_WA0 + 3]), m_conv_a_w[0], v_conv_a_w[0]),
        "conv_b_w": (conv_b_w[0], shard_cols(small[G_WB0:G_WB0 + 4]), m_conv_b_w[0], v_conv_b_w[0]),
        "conv_b_bias": (conv_b_bias, small[G_CBB:G_CBB + 1], m_conv_b_bias, v_conv_b_bias),
        "b_rg_a": (b_rg_a, small[G_BA:G_BA + 1], m_b_rg_a, v_b_rg_a),
        "b_rg_x": (b_rg_x, small[G_BX:G_BX + 1], m_b_rg_x, v_b_rg_x),
        "lru_lambda": (lru_lambda, small[G_LAM:G_LAM + 1], m_lru_lambda, v_lru_lambda),
        "g_norm_ffn": (g_norm_ffn, small[G_GFFN:G_GFFN + 1], m_g_norm_ffn, v_g_norm_ffn),
        "g_norm_final": (g_norm_final.reshape(1, D), small[G_GFIN:G_GFIN + 1], m_g_norm_final.reshape(1, D),
                         v_g_norm_final.reshape(1, D)),
    }
    stepped = _adamw_small(list(small_sets.values()), "adamw_small")
    for (n, (w_, g_, _, _)), (d_, nm_, nv_) in zip(small_sets.items(), stepped):
        shape = (1,) + w_.shape if n.startswith("conv_") and n != "conv_b_bias" else w_.shape
        res[n] = tuple(a.reshape(shape) for a in (g_, d_, nm_, nv_))
    gw_in, gw_rga, gw_rgx, gw_out = _rs_end(rs_a, [res[n][1] for n in res])
    res["w_in"] = step_halves("w_in", w_in, gw_in, m_w_in, v_w_in)
    res["w_rg_a"] = step_halves("w_rg_a", w_rg_a, gw_rga, m_w_rg_a, v_w_rg_a)
    res["w_rg_x"] = step_halves("w_rg_x", w_rg_x, gw_rgx, m_w_rg_x, v_w_rg_x)
    res["w_out"] = step_halves("w_out", w_out, gw_out, m_w_out, v_w_out)
    res["b_ada"] = tuple(a.reshape(1, 6 * D) for a in res["b_ada"])
    res["g_norm_final"] = tuple(a.reshape(D) for a in res["g_norm_final"])
    names = ["w_ada", "b_ada", "g_norm_mix", "w_in", "conv_a_w", "conv_b_w", "conv_b_bias", "w_rg_a", "b_rg_a", "w_rg_x",
             "b_rg_x", "lru_lambda", "w_out", "g_norm_ffn", "w_gate_up", "w_down", "g_norm_final"]
    loss = jnp.sum(small[G_LOSS])
    return (loss, from_blocks(grad_x)[None], *[res[n][0] for n in names], *[res[n][1] for n in names],
            *[res[n][2] for n in names], *[res[n][3] for n in names])
```

```python
import functools

import jax
import jax.numpy as jnp
from jax import lax
from jax.experimental import pallas as pl
from jax.experimental.pallas import tpu as pltpu

F32 = jnp.float32
BF16 = jnp.bfloat16
MESH = pl.DeviceIdType.MESH

D = 1024
N_CHIPS = 4
N_DEV = 8
D_IN = 7 * D
C_IN = D_IN // N_CHIPS
D_FF = 2816
C_GU = 2 * D_FF // N_CHIPS
HEADS = 4
HB = D // HEADS
EPS = 1e-6
LRU_C = 8.0
ADAM_LR, ADAM_B1, ADAM_B2, ADAM_EPS, ADAM_WD, ADAM_STEP = 0.001, 0.9, 0.999, 1e-08, 0.01, 10
VMEM_LIMIT = 56 << 20

(V_SH1, V_SC1, V_GT1, V_SH2, V_SC2, V_GT2, V_GMIX, V_GFFN, V_GFIN, V_CBB, V_BA, V_BX, V_LAM,
 V_WA0, V_WA1, V_WA2, V_WB0, V_WB1, V_WB2, V_WB3) = range(20)
N_VEC = 24
(G_SH1, G_SC1, G_GT1, G_SH2, G_SC2, G_GT2, G_GMIX, G_CBB, G_BA, G_BX, G_LAM, G_GFFN, G_GFIN,
 G_WA0, G_WA1, G_WA2, G_WB0, G_WB1, G_WB2, G_WB3, G_LOSS) = range(21)
N_SMALL = 24

_VMEM = pl.BlockSpec(memory_space=pltpu.VMEM)
_ANY = pl.BlockSpec(memory_space=pl.ANY)


def _cparams(n_grid=1):
    return pltpu.CompilerParams(dimension_semantics=("arbitrary",) * n_grid, vmem_limit_bytes=VMEM_LIMIT)


def _after(deps, body):
    n = len(deps)
    return lambda *refs: body(*refs[n:])


def _rms(x):
    rstd = lax.rsqrt(jnp.mean(x * x, axis=-1, keepdims=True) + EPS)
    return x * rstd, rstd


def _rms_bwd(dxhat, xhat, rstd):
    return rstd * (dxhat - xhat * jnp.mean(dxhat * xhat, axis=-1, keepdims=True))


def _rowsum(v):
    return jnp.sum(v, axis=0, keepdims=True)


def _dot(a, b):
    return jnp.dot(a, b, preferred_element_type=F32)


def _dot_nt(a, b):
    return lax.dot_general(a, b, (((1,), (1,)), ((), ())), preferred_element_type=F32)


def _dot_tn(a, b):
    return lax.dot_general(a, b, (((0,), (0,)), ((), ())), preferred_element_type=F32)


def _gelu(x):
    k, c = 0.7978845608028654, 0.044715
    t = jnp.tanh(k * (x + c * x * x * x))
    return 0.5 * x * (1.0 + t), 0.5 * (1.0 + t) + 0.5 * x * (1.0 - t * t) * k * (1.0 + 3.0 * c * x * x)


def _log_sigmoid(lam):
    return jnp.minimum(lam, 0.0) - jnp.log1p(jnp.exp(-jnp.abs(lam)))


def _lru_gates(u, wa_ref, wx_ref, v_ref, row0):
    ub = u.astype(BF16)
    pre_a = jnp.concatenate([_dot(ub[:, h * HB:(h + 1) * HB], wa_ref[h]) for h in range(HEADS)], axis=1)
    pre_x = jnp.concatenate([_dot(ub[:, h * HB:(h + 1) * HB], wx_ref[h]) for h in range(HEADS)], axis=1)
    r = jax.nn.sigmoid(pre_a + v_ref[V_BA:V_BA + 1, :])
    ig = jax.nn.sigmoid(pre_x + v_ref[V_BX:V_BX + 1, :])
    log_a = LRU_C * r * _log_sigmoid(v_ref[V_LAM:V_LAM + 1, :])
    a = jnp.exp(log_a)
    x2 = 2.0 * log_a
    m2 = jnp.where(x2 > -0.03, -x2 * (1.0 + x2 * (0.5 + x2 * (1.0 / 6.0 + x2 * (1.0 / 24.0)))), 1.0 - a * a)
    mult = jnp.where(row0, 1.0, jnp.sqrt(jnp.maximum(m2, 0.0)))
    return r, ig, a, mult


TIME_BLOCKS = 8
N_KEPT = 10


def _late_blocks(v, buf, g, halo=None):
    n = buf.shape[0]
    out = []
    for idx in range(n):
        k = TIME_BLOCKS - n + idx
        buf[idx, 8:g + 8, :] = v[k * g:(k + 1) * g]
        if halo is not None:
            buf[idx, 7:8, :] = halo[idx]
        out.append(buf[idx, pl.ds(7, g), :])
        if halo is None:
            buf[idx, 7:8, :] = buf[idx, g + 7:g + 8, :]
    return out


def _earlier(v, s, late, g):
    return jnp.concatenate(late[len(late) - s:] + [v[0:(TIME_BLOCKS - s) * g]], axis=0)


def _early_blocks(v, buf, g):
    out = []
    for k in range(buf.shape[0]):
        buf[k, 0:g, :] = v[k * g:(k + 1) * g]
        out.append(buf[k, pl.ds(1, g), :])
        buf[k, g:g + 1, :] = buf[k, 0:1, :]
    return out


def _later(v, s, early, g):
    return jnp.concatenate([v[s * g:]] + early[0:s], axis=0)


def _fwd_in_first(x, vecs, w_in_g, q_idx, ts, deps=()):
    s = x.shape[0]

    def body(q_ref, x_ref, v_ref, w_ref, h1_ref, proj_ref):
        xhat, _ = _rms(x_ref[...])
        h = xhat * v_ref[V_GMIX:V_GMIX + 1, :] * (1.0 + v_ref[V_SC1:V_SC1 + 1, :]) + v_ref[V_SH1:V_SH1 + 1, :]
        hb = h.astype(BF16)
        h1_ref[...] = hb
        proj_ref[...] = _dot(hb, w_ref[...]).astype(BF16)

    return pl.pallas_call(
        lambda q_ref, *refs: body(q_ref, *refs[len(deps):]),
        grid_spec=pltpu.PrefetchScalarGridSpec(
            num_scalar_prefetch=1, grid=(s // ts,),
            in_specs=[_ANY] * len(deps) + [pl.BlockSpec((ts, D), lambda i, q: (i, 0)), _VMEM,
                                           pl.BlockSpec((None, D, C_IN), lambda i, q: (q[0], 0, 0))],
            out_specs=[pl.BlockSpec((ts, D), lambda i, q: (i, 0)), pl.BlockSpec((ts, C_IN), lambda i, q: (i, q[0]))]),
        out_shape=(jax.ShapeDtypeStruct((s, D), BF16), jax.ShapeDtypeStruct((s, D_IN), BF16)),
        compiler_params=_cparams(), name="fwd_in_own")(q_idx, *deps, x, vecs, w_in_g)


def _fwd_in_more(h1, w_in_g, proj, q_idx, ts, name, deps=()):
    s = h1.shape[0]

    def body(q_ref, h1_ref, w_ref, proj_in_ref, proj_ref):
        proj_ref[...] = _dot(h1_ref[...], w_ref[...]).astype(BF16)

    return pl.pallas_call(
        lambda q_ref, *refs: body(q_ref, *refs[len(deps):]),
        grid_spec=pltpu.PrefetchScalarGridSpec(
            num_scalar_prefetch=1, grid=(s // ts,),
            in_specs=[_ANY] * len(deps) + [pl.BlockSpec((ts, D), lambda i, q: (i, 0)),
                                           pl.BlockSpec((None, D, C_IN), lambda i, q: (q[0], 0, 0)), _ANY],
            out_specs=pl.BlockSpec((ts, C_IN), lambda i, q: (i, q[0]))),
        out_shape=jax.ShapeDtypeStruct((s, D_IN), BF16), input_output_aliases={len(deps) + 3: 0},
        compiler_params=_cparams(), name=name)(q_idx, *deps, h1, w_in_g, proj)


def _fwd_mix(proj, x, vecs, w_rga, w_rgx, w_out, ts, deps=()):
    s = x.shape[0]
    g = ts // TIME_BLOCKS

    def body(proj_ref, x_ref, v_ref, wa_ref, wx_ref, wo_ref, x1_ref, mg_ref, z1_ref, kept_ref, decay_ref,
             ua_buf, rx_buf, p_buf, q_buf, c_buf, hcarry):
        i = pl.program_id(0)

        @pl.when(i == 0)
        def _():
            ua_buf[...] = jnp.zeros(ua_buf.shape, F32)
            rx_buf[...] = jnp.zeros(rx_buf.shape, F32)
            hcarry[...] = jnp.zeros((8, D), F32)

        def seg(j):
            return proj_ref[:, j * D:(j + 1) * D].astype(F32)

        def vrow(j):
            return v_ref[j:j + 1, :]

        cb, cc, cx, rx, rg, ga, gb = (seg(j) for j in range(7))
        ua = cc * cx
        ua_late = _late_blocks(ua, ua_buf, g)
        rx_late = _late_blocks(rx, rx_buf, g)
        va = vrow(V_WA2) * ua + vrow(V_WA1) * _earlier(ua, 1, ua_late, g) + vrow(V_WA0) * _earlier(ua, 2, ua_late, g)
        u = (vrow(V_WB3) * rx + vrow(V_WB2) * _earlier(rx, 1, rx_late, g) + vrow(V_WB1) * _earlier(rx, 2, rx_late, g)
             + vrow(V_WB0) * _earlier(rx, 3, rx_late, g) + vrow(V_CBB))

        rows = lax.broadcasted_iota(jnp.int32, (ts, D), 0)
        row0 = jnp.logical_and(rows == 0, i == 0)
        r, ig, a, mult = _lru_gates(u, wa_ref, wx_ref, v_ref, row0)
        decay_ref[...] = a
        bx = mult * (ig * u)

        prods, sums = [a[0:g]], [bx[0:g]]
        for k in range(1, TIME_BLOCKS):
            ak = a[k * g:(k + 1) * g]
            sums.append(ak * sums[-1] + bx[k * g:(k + 1) * g])
            prods.append(ak * prods[-1])
        p_buf[...] = prods[-1]
        q_buf[...] = sums[-1]
        state = hcarry[0:1, :]
        for j in range(g):
            c_buf[j:j + 1, :] = state
            state = p_buf[j:j + 1, :] * state + q_buf[j:j + 1, :]
        hcarry[0:1, :] = state
        entering = c_buf[...]
        h = jnp.concatenate([sums[k] + prods[k] * entering for k in range(TIME_BLOCKS)], axis=0)

        gel, dgel = _gelu(rg)
        sga = jax.nn.sigmoid(ga)
        sgb = jax.nn.sigmoid(gb)
        for j, keep in enumerate((va, r, ig, sga, sgb, gel, dgel, mult, u, h)):
            kept_ref[:, j * D:(j + 1) * D] = keep.astype(BF16)
        merged = (sga * (cb * va) + sgb * (h * gel)).astype(BF16)
        mg_ref[...] = merged
        z1 = _dot(merged, wo_ref[...])
        z1_ref[...] = z1.astype(BF16)
        x1_ref[...] = x_ref[...] + vrow(V_GT1) * z1

    row = lambda i: (i, 0)
    return pl.pallas_call(
        _after(deps, body), grid=(s // ts,),
        out_shape=(jax.ShapeDtypeStruct((s, D), F32), jax.ShapeDtypeStruct((s, D), BF16), jax.ShapeDtypeStruct((s, D), BF16),
                   jax.ShapeDtypeStruct((s, N_KEPT * D), BF16), jax.ShapeDtypeStruct((s, D), F32)),
        in_specs=[_ANY] * len(deps) + [pl.BlockSpec((ts, D_IN), row), pl.BlockSpec((ts, D), row), _VMEM, _VMEM, _VMEM, _VMEM],
        out_specs=[pl.BlockSpec((ts, D), row)] * 3 + [pl.BlockSpec((ts, N_KEPT * D), row), pl.BlockSpec((ts, D), row)],
        scratch_shapes=[pltpu.VMEM((2, g + 8, D), F32), pltpu.VMEM((3, g + 8, D), F32), pltpu.VMEM((g, D), F32),
                        pltpu.VMEM((g, D), F32), pltpu.VMEM((g, D), F32), pltpu.VMEM((8, D), F32)],
        compiler_params=_cparams(), name="fwd_mix")(*deps, proj, x, vecs, w_rga, w_rgx, w_out)


def _ffn_loss(x1, target, vecs, w_gu_g, w_dn, ts):
    s = x1.shape[0]

    def body(x1_ref, t_ref, v_ref, wgu_ref, wdn_ref, dx1_ref, h2_ref, act_ref, dz2_ref, dgu_ref, sm_ref):
        @pl.when(pl.program_id(0) == 0)
        def _():
            sm_ref[...] = jnp.zeros((N_SMALL, D), F32)

        def vrow(j):
            return v_ref[j:j + 1, :]

        n_sub = 1
        rows = [slice(k * (ts // n_sub), (k + 1) * (ts // n_sub)) for k in range(n_sub)]
        subs = [dict(r=r, sums={}) for r in rows]

        def stage_norm(t):
            t["x1"] = x1_ref[t["r"], :]
            t["xh1"], t["rstd1"] = _rms(t["x1"])
            t["n2"] = t["xh1"] * vrow(V_GFFN)
            t["h2"] = (t["n2"] * (1.0 + vrow(V_SC2)) + vrow(V_SH2)).astype(BF16)
            h2_ref[t["r"], :] = t["h2"]

        def stage_up(t):
            h2 = t["h2"]
            g = jnp.concatenate([_dot(h2, wgu_ref[0]), _dot(h2, wgu_ref[1])], axis=1)
            t["up"] = jnp.concatenate([_dot(h2, wgu_ref[2]), _dot(h2, wgu_ref[3])], axis=1)
            t["g"] = g
            t["sg"] = jax.nn.sigmoid(g)
            t["silu"] = g * t["sg"]
            t["act"] = (t["silu"] * t["up"]).astype(BF16)
            act_ref[t["r"], :] = t["act"]

        def stage_down_loss(t):
            z2 = _dot(t["act"], wdn_ref[...])
            x2 = t["x1"] + vrow(V_GT2) * z2
            xh2, rstd2 = _rms(x2)
            err = xh2 * vrow(V_GFIN) - t_ref[t["r"], :]
            t["sums"][G_LOSS] = _rowsum((0.5 / D) * err * err)
            dy = err * (1.0 / D)
            t["sums"][G_GFIN] = _rowsum(dy * xh2)
            t["dx2"] = _rms_bwd(dy * vrow(V_GFIN), xh2, rstd2)
            t["sums"][G_GT2] = _rowsum(t["dx2"] * z2)
            t["dz2"] = (vrow(V_GT2) * t["dx2"]).astype(BF16)
            dz2_ref[t["r"], :] = t["dz2"]

        def stage_back_act(t):
            dact = _dot_nt(t["dz2"], wdn_ref[...])
            g, sg = t["g"], t["sg"]
            t["dgate"] = (dact * t["up"] * (sg * (1.0 + g * (1.0 - sg)))).astype(BF16)
            t["dup"] = (dact * t["silu"]).astype(BF16)
            dgu_ref[t["r"], 0:D_FF] = t["dgate"]
            dgu_ref[t["r"], D_FF:2 * D_FF] = t["dup"]

        def stage_back_norm(t):
            dgate, dup = t["dgate"], t["dup"]
            dh2 = (_dot_nt(dgate[:, 0:C_GU], wgu_ref[0]) + _dot_nt(dgate[:, C_GU:2 * C_GU], wgu_ref[1])
                   + _dot_nt(dup[:, 0:C_GU], wgu_ref[2]) + _dot_nt(dup[:, C_GU:2 * C_GU], wgu_ref[3]))
            t["sums"][G_SH2] = _rowsum(dh2)
            t["sums"][G_SC2] = _rowsum(dh2 * t["n2"])
            dn2 = dh2 * (1.0 + vrow(V_SC2))
            t["sums"][G_GFFN] = _rowsum(dn2 * t["xh1"])
            dx1_ref[t["r"], :] = t["dx2"] + _rms_bwd(dn2 * vrow(V_GFFN), t["xh1"], t["rstd1"])

        for stage in (stage_norm, stage_up, stage_down_loss, stage_back_act, stage_back_norm):
            for t in subs:
                stage(t)
        for j in subs[0]["sums"]:
            total = subs[0]["sums"][j]
            for t in subs[1:]:
                total = total + t["sums"][j]
            sm_ref[j:j + 1, :] += total

    row = lambda i: (i, 0)
    return pl.pallas_call(
        body, grid=(s // ts,),
        out_shape=(jax.ShapeDtypeStruct((s, D), F32), jax.ShapeDtypeStruct((s, D), BF16), jax.ShapeDtypeStruct((s, D_FF), BF16),
                   jax.ShapeDtypeStruct((s, D), BF16), jax.ShapeDtypeStruct((s, 2 * D_FF), BF16),
                   jax.ShapeDtypeStruct((N_SMALL, D), F32)),
        in_specs=[pl.BlockSpec((ts, D), row), pl.BlockSpec((ts, D), row), _VMEM, _VMEM, _VMEM],
        out_specs=[pl.BlockSpec((ts, D), row), pl.BlockSpec((ts, D), row), pl.BlockSpec((ts, D_FF), row),
                   pl.BlockSpec((ts, D), row), pl.BlockSpec((ts, 2 * D_FF), row), pl.BlockSpec((N_SMALL, D), lambda i: (0, 0))],
        compiler_params=_cparams(), name="ffn_loss")(x1, target, vecs, w_gu_g, w_dn)


def _bwd_mix(dx1, z1, merged, proj, kept, decay, vecs, w_rga, w_rgx, w_out, small, ts, deps=()):
    s = dx1.shape[0]
    nt = s // ts
    g = ts // TIME_BLOCKS
    assert g % 16 == 0

    def body(dx1_ref, z1_ref, mg_ref, proj_ref, kept_ref, decay_ref, hh_ref, v_ref, wa_ref, wx_ref,
             wo_ref, sm0_ref, dproj_ref, sm_ref, dwa_ref, dwx_ref, dwo_ref,
             h_buf, a_buf, dva_buf, du_buf, p_buf, q_buf, c_buf, lcarry):
        i = pl.program_id(0)
        first_tile = i == nt - 1

        @pl.when(i == 0)
        def _():
            a_buf[...] = jnp.zeros(a_buf.shape, F32)
            dva_buf[...] = jnp.zeros(dva_buf.shape, F32)
            du_buf[...] = jnp.zeros(du_buf.shape, F32)
            lcarry[...] = jnp.zeros((8, D), F32)
            sm_ref[...] = sm0_ref[...]
            dwa_ref[...] = jnp.zeros((HEADS, HB, HB), F32)
            dwx_ref[...] = jnp.zeros((HEADS, HB, HB), F32)
            dwo_ref[...] = jnp.zeros((D, D), F32)

        def seg(j):
            return proj_ref[:, j * D:(j + 1) * D].astype(F32)

        def vrow(j):
            return v_ref[j:j + 1, :]

        def acc(j, val):
            sm_ref[j:j + 1, :] += _rowsum(val)

        cb, cc, cx, rx = (seg(j) for j in range(4))
        ua = cc * cx
        va, r, ig, sga, sgb, gel, dgel, mult, u, h = (kept_ref[:, j * D:(j + 1) * D].astype(F32) for j in range(N_KEPT))
        a = decay_ref[...]
        rows = lax.broadcasted_iota(jnp.int32, (ts, D), 0)
        row0 = jnp.logical_and(rows == 0, first_tile)

        dx1 = dx1_ref[...]
        acc(G_GT1, dx1 * z1_ref[...].astype(F32))
        dz1 = (vrow(V_GT1) * dx1).astype(BF16)
        dwo_ref[...] += _dot_tn(mg_ref[...], dz1)
        dmg = _dot_nt(dz1, wo_ref[...])
        dya = dmg * sga
        dyb = dmg * sgb
        dproj_ref[:, 5 * D:6 * D] = (dya * (cb * va) * (1.0 - sga)).astype(BF16)
        dproj_ref[:, 6 * D:7 * D] = (dyb * (h * gel) * (1.0 - sgb)).astype(BF16)

        dproj_ref[:, 0:D] = (dya * va).astype(BF16)
        dva = dya * cb
        dva_early = _early_blocks(dva, dva_buf, g)
        dva1 = _later(dva, 1, dva_early, g)
        dva2 = _later(dva, 2, dva_early, g)
        dua = vrow(V_WA2) * dva + vrow(V_WA1) * dva1 + vrow(V_WA0) * dva2
        acc(G_WA2, ua * dva)
        acc(G_WA1, ua * dva1)
        acc(G_WA0, ua * dva2)
        dproj_ref[:, D:2 * D] = (dua * cx).astype(BF16)
        dproj_ref[:, 2 * D:3 * D] = (dua * cc).astype(BF16)

        dproj_ref[:, 4 * D:5 * D] = (dyb * h * dgel).astype(BF16)
        a_next = _later(a, 1, _early_blocks(a, a_buf, g), g)
        dh = dyb * gel
        last = TIME_BLOCKS - 1
        prods, sums = {last: a_next[last * g:]}, {last: dh[last * g:]}
        for k in range(last - 1, -1, -1):
            ak = a_next[k * g:(k + 1) * g]
            sums[k] = dh[k * g:(k + 1) * g] + ak * sums[k + 1]
            prods[k] = ak * prods[k + 1]
        p_buf[...] = prods[0]
        q_buf[...] = sums[0]
        state = lcarry[0:1, :]
        for j in range(g - 1, -1, -1):
            c_buf[j:j + 1, :] = state
            state = q_buf[j:j + 1, :] + p_buf[j:j + 1, :] * state
        lcarry[0:1, :] = state
        entering = c_buf[...]
        lam = jnp.concatenate([sums[k] + prods[k] * entering for k in range(TIME_BLOCKS)], axis=0)

        last = lax.broadcasted_iota(jnp.int32, hh_ref.shape, 0) == hh_ref.shape[0] - 1
        h_halo = [jnp.where(first_tile, 0.0, jnp.sum(jnp.where(last, hh_ref[...].astype(F32), 0.0), axis=0, keepdims=True))]
        da = lam * _earlier(h, 1, _late_blocks(h, h_buf, g, h_halo), g)
        dmult = jnp.where(row0, 0.0, lam * (ig * u))
        di = lam * mult * u
        du = lam * mult * ig
        dlog_a = da * a - dmult * (a * a) / mult
        lam_p = vrow(V_LAM)
        dr = dlog_a * (LRU_C * _log_sigmoid(lam_p))
        sm_ref[G_LAM:G_LAM + 1, :] += _rowsum(dlog_a * r) * (LRU_C * jax.nn.sigmoid(-lam_p))
        dpa = dr * r * (1.0 - r)
        dpx = di * ig * (1.0 - ig)
        acc(G_BA, dpa)
        acc(G_BX, dpx)
        dpab = dpa.astype(BF16)
        dpxb = dpx.astype(BF16)
        ub = u.astype(BF16)
        back = []
        for hd in range(HEADS):
            cols = slice(hd * HB, (hd + 1) * HB)
            back.append(_dot_nt(dpab[:, cols], wa_ref[hd]) + _dot_nt(dpxb[:, cols], wx_ref[hd]))
            dwa_ref[hd] += _dot_tn(ub[:, cols], dpab[:, cols])
            dwx_ref[hd] += _dot_tn(ub[:, cols], dpxb[:, cols])
        du = du + jnp.concatenate(back, axis=1)

        acc(G_CBB, du)
        du_early = _early_blocks(du, du_buf, g)
        du1 = _later(du, 1, du_early, g)
        du2 = _later(du, 2, du_early, g)
        du3 = _later(du, 3, du_early, g)
        dproj_ref[:, 3 * D:4 * D] = (vrow(V_WB3) * du + vrow(V_WB2) * du1 + vrow(V_WB1) * du2 + vrow(V_WB0) * du3).astype(BF16)
        acc(G_WB3, rx * du)
        acc(G_WB2, rx * du1)
        acc(G_WB1, rx * du2)
        acc(G_WB0, rx * du3)

    rev = lambda i: (nt - 1 - i, 0)
    h_halo16 = lambda i: (jnp.maximum((nt - 1 - i) * (ts // 16) - 1, 0), N_KEPT - 1)
    const2 = lambda i: (0, 0)
    const3 = lambda i: (0, 0, 0)
    return pl.pallas_call(
        _after(deps, body), grid=(nt,),
        out_shape=(jax.ShapeDtypeStruct((s, D_IN), BF16), jax.ShapeDtypeStruct((N_SMALL, D), F32),
                   jax.ShapeDtypeStruct((HEADS, HB, HB), F32), jax.ShapeDtypeStruct((HEADS, HB, HB), F32),
                   jax.ShapeDtypeStruct((D, D), F32)),
        in_specs=[_ANY] * len(deps) + [pl.BlockSpec((ts, D), rev), pl.BlockSpec((ts, D), rev), pl.BlockSpec((ts, D), rev),
                  pl.BlockSpec((ts, 4 * D), rev), pl.BlockSpec((ts, N_KEPT * D), rev), pl.BlockSpec((ts, D), rev),
                  pl.BlockSpec((16, D), h_halo16), _VMEM, _VMEM, _VMEM, _VMEM, _VMEM],
        out_specs=[pl.BlockSpec((ts, D_IN), rev), pl.BlockSpec((N_SMALL, D), const2),
                   pl.BlockSpec((HEADS, HB, HB), const3), pl.BlockSpec((HEADS, HB, HB), const3), pl.BlockSpec((D, D), const2)],
        scratch_shapes=[pltpu.VMEM((1, g + 8, D), F32), pltpu.VMEM((1, g + 8, D), F32),
                        pltpu.VMEM((2, g + 8, D), F32), pltpu.VMEM((3, g + 8, D), F32), pltpu.VMEM((g, D), F32),
                        pltpu.VMEM((g, D), F32), pltpu.VMEM((g, D), F32), pltpu.VMEM((8, D), F32)],
        compiler_params=_cparams(), name="bwd_mix")(*deps, dx1, z1, merged, proj, kept, decay, kept, vecs, w_rga,
                                                    w_rgx, w_out, small)


def _bwd_in(dproj, x, dx1, vecs, w_in_g, small, ts, deps=()):
    s = x.shape[0]

    def body(dp_ref, x_ref, dx1_ref, v_ref, w_ref, sm0_ref, gx_ref, sm_ref):
        @pl.when(pl.program_id(0) == 0)
        def _():
            sm_ref[...] = sm0_ref[...]

        def vrow(j):
            return v_ref[j:j + 1, :]

        dh1 = _dot_nt(dp_ref[:, 0:C_IN], w_ref[0])
        for k in range(1, N_CHIPS):
            dh1 += _dot_nt(dp_ref[:, k * C_IN:(k + 1) * C_IN], w_ref[k])
        xh, rstd = _rms(x_ref[...])
        sm_ref[G_SH1:G_SH1 + 1, :] += _rowsum(dh1)
        sm_ref[G_SC1:G_SC1 + 1, :] += _rowsum(dh1 * (xh * vrow(V_GMIX)))
        dn1 = dh1 * (1.0 + vrow(V_SC1))
        sm_ref[G_GMIX:G_GMIX + 1, :] += _rowsum(dn1 * xh)
        gx_ref[...] = dx1_ref[...] + _rms_bwd(dn1 * vrow(V_GMIX), xh, rstd)

    row = lambda i: (i, 0)
    return pl.pallas_call(
        _after(deps, body), grid=(s // ts,),
        out_shape=(jax.ShapeDtypeStruct((s, D), F32), jax.ShapeDtypeStruct((N_SMALL, D), F32)),
        in_specs=[_ANY] * len(deps) + [pl.BlockSpec((ts, D_IN), row), pl.BlockSpec((ts, D), row), pl.BlockSpec((ts, D), row),
                                       _VMEM, _VMEM, _VMEM],
        out_specs=[pl.BlockSpec((ts, D), row), pl.BlockSpec((N_SMALL, D), lambda i: (0, 0))],
        compiler_params=_cparams(), name="bwd_in")(*deps, dproj, x, dx1, vecs, w_in_g, small)


def _grad_w(a, b, n_col_blocks, ts, name, deps=()):
    s, m = a.shape
    tn = b.shape[1] // n_col_blocks
    n_steps = s // ts

    def body(a_ref, b_ref, o_ref, acc_ref):
        k = pl.program_id(1)

        @pl.when(k == 0)
        def _():
            acc_ref[...] = jnp.zeros((m, tn), F32)

        acc_ref[...] += _dot_tn(a_ref[...], b_ref[...])

        @pl.when(k == n_steps - 1)
        def _():
            o_ref[...] = acc_ref[...].astype(BF16)

    return pl.pallas_call(
        _after(deps, body), grid=(n_col_blocks, n_steps),
        out_shape=jax.ShapeDtypeStruct((n_col_blocks, m, tn), BF16),
        in_specs=[_ANY] * len(deps) + [pl.BlockSpec((ts, m), lambda n, k: (k, 0)), pl.BlockSpec((ts, tn), lambda n, k: (k, n))],
        out_specs=pl.BlockSpec((None, m, tn), lambda n, k: (n, 0, 0)),
        scratch_shapes=[pltpu.VMEM((m, tn), F32)],
        compiler_params=_cparams(2), name=name)(*deps, a, b)


def _ada_fwd(c_all, w_ada, b_ada):
    n = w_ada.shape[1]

    def body(c_ref, w_ref, b_ref, o_ref, ca_ref):
        c = c_ref[...]
        ca = c * jax.nn.sigmoid(c)
        ca_ref[...] = ca
        o_ref[...] = jnp.dot(ca, w_ref[...], preferred_element_type=F32, precision=lax.Precision.HIGHEST) + b_ref[...]

    return pl.pallas_call(
        body, out_shape=(jax.ShapeDtypeStruct((N_DEV, n), F32), jax.ShapeDtypeStruct((N_DEV, D), F32)),
        in_specs=[_VMEM] * 3, out_specs=[_VMEM] * 2, compiler_params=_cparams(0), name="ada_fwd")(c_all, w_ada, b_ada)


def _sum_small(parts):
    def body(p_ref, o_ref, d_ref):
        tot = p_ref[0]
        for dev in range(1, N_DEV):
            tot = tot + p_ref[dev]
        o_ref[...] = tot
        d_ref[...] = p_ref[:, 0:8, :]

    return pl.pallas_call(
        body, out_shape=(jax.ShapeDtypeStruct((N_SMALL, D), F32), jax.ShapeDtypeStruct((N_DEV, 8, D), F32)),
        in_specs=[_VMEM], out_specs=[_VMEM] * 2, compiler_params=_cparams(0), name="sum_small")(parts)


def _adamw_small(items, name):
    n = len(items)

    def body(*refs):
        ins, outs = refs[:4 * n], refs[4 * n:]
        for k in range(n):
            w_ref, g_ref, m_ref, v_ref = ins[4 * k:4 * k + 4]
            d_ref, nm_ref, nv_ref = outs[3 * k:3 * k + 3]
            g_ = g_ref[...]
            m_ = ADAM_B1 * m_ref[...] + (1.0 - ADAM_B1) * g_
            v_ = ADAM_B2 * v_ref[...] + (1.0 - ADAM_B2) * (g_ * g_)
            nm_ref[...] = m_
            nv_ref[...] = v_
            m_hat = m_ / (1.0 - ADAM_B1 ** ADAM_STEP)
            v_hat = v_ / (1.0 - ADAM_B2 ** ADAM_STEP)
            d_ref[...] = -ADAM_LR * (m_hat / (jnp.sqrt(v_hat) + ADAM_EPS) + ADAM_WD * w_ref[...])

    out = pl.pallas_call(
        body, out_shape=tuple(jax.ShapeDtypeStruct(it[0].shape, F32) for it in items for _ in range(3)),
        in_specs=[_VMEM] * (4 * n), out_specs=[_VMEM] * (3 * n), name=name)(*[a for it in items for a in it])
    return [tuple(out[3 * k:3 * k + 3]) for k in range(n)]


HALF_STEPS = 4


def _adamw_halves(sets, c_idx, name, deps=()):
    nh = HALF_STEPS
    n = len(sets)

    def body(c_ref, *refs):
        refs = refs[len(deps):]
        ins, outs = refs[:5 * n], refs[5 * n:]
        for k in range(n):
            w_ref, mine_ref, other_ref, m_ref, v_ref = ins[5 * k:5 * k + 5]
            g_ref, d_ref, nm_ref, nv_ref = outs[4 * k:4 * k + 4]
            g_ = jnp.where(pl.program_id(0) // nh == c_ref[0], mine_ref[...], other_ref[...])
            g_ref[...] = g_
            m_ = ADAM_B1 * m_ref[...] + (1.0 - ADAM_B1) * g_
            v_ = ADAM_B2 * v_ref[...] + (1.0 - ADAM_B2) * (g_ * g_)
            nm_ref[...] = m_
            nv_ref[...] = v_
            m_hat = m_ / (1.0 - ADAM_B1 ** ADAM_STEP)
            v_hat = v_ / (1.0 - ADAM_B2 ** ADAM_STEP)
            d_ref[...] = -ADAM_LR * (m_hat / (jnp.sqrt(v_hat) + ADAM_EPS) + ADAM_WD * w_ref[...])

    in_specs, out_specs, out_shape = [], [], []
    for w, mine, _, _, _ in sets:
        r2, cols = mine.shape
        block = (r2 // nh, cols)
        full = pl.BlockSpec(block, lambda i, c: (i, 0))
        in_specs += [full, pl.BlockSpec(block, lambda i, c: (jnp.clip(i - c[0] * nh, 0, nh - 1), 0)),
                     pl.BlockSpec(block, lambda i, c: (jnp.clip(i - (1 - c[0]) * nh, 0, nh - 1), 0)), full, full]
        out_specs += [full] * 4
        out_shape += [jax.ShapeDtypeStruct((2 * r2, cols), F32)] * 4
    out = pl.pallas_call(
        body,
        grid_spec=pltpu.PrefetchScalarGridSpec(num_scalar_prefetch=1, grid=(2 * nh,),
                                               in_specs=[_ANY] * len(deps) + in_specs, out_specs=out_specs),
        out_shape=tuple(out_shape), compiler_params=_cparams(), name=name,
    )(c_idx, *deps, *[a for s in sets for a in s])
    return [tuple(out[4 * k:4 * k + 4]) for k in range(n)]


def _adamw_ada(w, c_act, dmod, m, v):
    rows, n = w.shape
    tr = 128

    def body(c_ref, d_ref, w_ref, m_ref, v_ref, g_ref, dl_ref, nm_ref, nv_ref):
        g_ = lax.dot_general(c_ref[...], d_ref[...], (((0,), (0,)), ((), ())), preferred_element_type=F32,
                             precision=lax.Precision.HIGHEST)
        g_ref[...] = g_
        m_ = ADAM_B1 * m_ref[...] + (1.0 - ADAM_B1) * g_
        v_ = ADAM_B2 * v_ref[...] + (1.0 - ADAM_B2) * (g_ * g_)
        nm_ref[...] = m_
        nv_ref[...] = v_
        m_hat = m_ / (1.0 - ADAM_B1 ** ADAM_STEP)
        v_hat = v_ / (1.0 - ADAM_B2 ** ADAM_STEP)
        dl_ref[...] = -ADAM_LR * (m_hat / (jnp.sqrt(v_hat) + ADAM_EPS) + ADAM_WD * w_ref[...])

    spec = pl.BlockSpec((tr, n), lambda i: (i, 0))
    return pl.pallas_call(
        body, grid=(rows // tr,), out_shape=(jax.ShapeDtypeStruct((rows, n), F32),) * 4,
        in_specs=[pl.BlockSpec((N_DEV, tr), lambda i: (0, i)), _VMEM, spec, spec, spec], out_specs=[spec] * 4,
        compiler_params=_cparams(), name="adamw_w_ada")(c_act, dmod, w, m, v)


def _add_halves(grads, recvs, c_idx, name):
    nw = len(grads)

    def body(c_ref, *refs):
        for g_ref, r_ref, o_ref in zip(refs[:nw], refs[nw:2 * nw], refs[2 * nw:]):
            o_ref[...] = (g_ref[...].astype(F32) + r_ref[...].astype(F32)).astype(BF16)

    mine = [pl.BlockSpec((None, None) + g.shape[2:], lambda k, c: (k, c[0], 0, 0)) for g in grads]
    whole = [pl.BlockSpec((None,) + g.shape[2:], lambda k, c: (k, 0, 0)) for g in grads]
    return pl.pallas_call(
        body,
        grid_spec=pltpu.PrefetchScalarGridSpec(num_scalar_prefetch=1, grid=(N_CHIPS,), in_specs=mine + whole, out_specs=whole),
        out_shape=tuple(jax.ShapeDtypeStruct((N_CHIPS,) + g.shape[2:], BF16) for g in grads),
        compiler_params=_cparams(), name=name)(c_idx, *grads, *recvs)


def _sum_chips(owns, others, chip_idx, name):
    nw = len(owns)
    steps = 2

    def body(p_ref, *refs):
        for own_ref, got_ref, o_ref in zip(refs[:nw], refs[nw:2 * nw], refs[2 * nw:]):
            o_ref[...] = (((own_ref[...].astype(F32) + got_ref[0].astype(F32)) + got_ref[1].astype(F32))
                          + got_ref[2].astype(F32))

    blocks = [(a.shape[1] // steps, a.shape[2]) for a in owns]
    return pl.pallas_call(
        body,
        grid_spec=pltpu.PrefetchScalarGridSpec(
            num_scalar_prefetch=1, grid=(steps,),
            in_specs=([pl.BlockSpec((None,) + b, lambda i, p: (p[0], i, 0)) for b in blocks]
                      + [pl.BlockSpec((N_CHIPS - 1,) + b, lambda i, p: (0, i, 0)) for b in blocks]),
            out_specs=[pl.BlockSpec(b, lambda i, p: (i, 0)) for b in blocks]),
        out_shape=tuple(jax.ShapeDtypeStruct(a.shape[1:], F32) for a in owns), compiler_params=_cparams(),
        name=name)(chip_idx, *owns, *others)


def _place():
    x, y, c = lax.axis_index("x"), lax.axis_index("y"), lax.axis_index("c")
    return x, y, c, 2 * x + y


def _flip(v, bit):
    return 1 - v if bit else v


def _allgather8(v, name, deps=()):
    r, n = v.shape

    def body(*refs):
        v_ref, out_ref, send_sems, recv_sems, local_sem = refs[len(deps):]
        x, y, c, _ = _place()
        me = 4 * x + 2 * y + c
        mine = pltpu.make_async_copy(v_ref, out_ref.at[me], local_sem)
        mine.start()
        sends = []
        for rel in range(1, N_DEV):
            peer = (_flip(x, rel & 4), _flip(y, rel & 2), _flip(c, rel & 1))
            cp = pltpu.make_async_remote_copy(v_ref, out_ref.at[me], send_sems.at[rel - 1], recv_sems.at[rel - 1],
                                              device_id=peer, device_id_type=MESH)
            cp.start()
            sends.append(cp)
        for rel in range(1, N_DEV):
            peer = (_flip(x, rel & 4), _flip(y, rel & 2), _flip(c, rel & 1))
            peer_idx = 4 * peer[0] + 2 * peer[1] + peer[2]
            pltpu.make_async_remote_copy(v_ref, out_ref.at[peer_idx], send_sems.at[rel - 1], recv_sems.at[rel - 1],
                                         device_id=peer, device_id_type=MESH).wait_recv()
        for cp in sends:
            cp.wait_send()
        mine.wait()

    return pl.pallas_call(
        body, out_shape=jax.ShapeDtypeStruct((N_DEV, r, n), F32), in_specs=[_ANY] * len(deps) + [_VMEM], out_specs=_VMEM,
        scratch_shapes=[pltpu.SemaphoreType.DMA((N_DEV - 1,)), pltpu.SemaphoreType.DMA((N_DEV - 1,)), pltpu.SemaphoreType.DMA(())],
        name=name)(*deps, v)


_HBM = pl.BlockSpec(memory_space=pltpu.HBM)
_SEM = pl.BlockSpec(memory_space=pltpu.SEMAPHORE)
_EFFECT = pltpu.SideEffectType.DATAFLOW_SIDE_EFFECTING


def _xchg_start(name, plan, n_copies, srcs, lands, after=(), sibling_id=None):
    bufs = list(srcs) + list(lands)
    ns, nb = len(srcs), len(srcs) + len(lands)

    def body(*refs):
        send_sems, recv_sems, token = refs[nb + len(after)], refs[nb + len(after) + 1], refs[-1]
        if sibling_id is not None:
            x, y, c, _ = _place()
            barrier = pltpu.get_barrier_semaphore()
            pl.semaphore_signal(barrier, inc=1, device_id=(x, y, 1 - c), device_id_type=MESH)
            pl.semaphore_wait(barrier, 1)
        for i, (src, dst, peer, _) in enumerate(plan(_place(), refs[:ns], refs[ns:nb])):
            pltpu.make_async_remote_copy(src, dst, send_sems.at[i], recv_sems.at[i], device_id=peer, device_id_type=MESH).start()
        token[...] = jnp.zeros_like(token)

    out = pl.pallas_call(
        body, name=name,
        out_shape=(pltpu.SemaphoreType.DMA((n_copies,)), pltpu.SemaphoreType.DMA((n_copies,)),
                   *[pltpu.HBM(a.shape, a.dtype) for a in bufs], jax.ShapeDtypeStruct((8, 128), F32)),
        in_specs=[_HBM] * nb + [_ANY] * len(after), out_specs=(_SEM, _SEM, *[_HBM] * nb, _VMEM),
        input_output_aliases={i: 2 + i for i in range(nb)},
        compiler_params=pltpu.CompilerParams(has_side_effects=_EFFECT, collective_id=sibling_id),
    )(*[pltpu.with_memory_space_constraint(a, pltpu.HBM) for a in bufs], *after)
    return (out[0], out[1]), out[2:2 + ns], out[2 + ns:2 + nb], out[-1]


def _xchg_wait(name, plan, sems, srcs, lands, after, sem_ids=None):
    bufs = list(srcs) + list(lands)
    ns, nb = len(srcs), len(srcs) + len(lands)

    def body(*refs):
        send_sems, recv_sems = refs[nb], refs[nb + 1]
        copies = plan(_place(), refs[:ns], refs[ns:nb])
        ids = range(len(copies)) if sem_ids is None else sem_ids
        for i, (src, _, peer, mine) in zip(ids, copies, strict=True):
            if i is not None:
                cp = pltpu.make_async_remote_copy(src, mine, send_sems.at[i], recv_sems.at[i], device_id=peer,
                                                  device_id_type=MESH)
                cp.wait_send()
                cp.wait_recv()

    out = pl.pallas_call(
        body, name=name, out_shape=tuple(pltpu.HBM(a.shape, a.dtype) for a in bufs),
        in_specs=[_HBM] * nb + [_SEM, _SEM] + [_ANY] * len(after), out_specs=tuple([_HBM] * nb),
        input_output_aliases={i: i for i in range(nb)},
        compiler_params=pltpu.CompilerParams(has_side_effects=_EFFECT),
    )(*bufs, *sems, *after)
    return out[:ns], out[ns:]


def _other_chips(place, which=(1, 2, 3)):
    x, y, c, _ = place
    return [((_flip(x, j & 2), _flip(y, j & 1), c), 2 * _flip(x, j & 2) + _flip(y, j & 1)) for j in which]


def _plan_gather_ici(chips):
    def plan(place, src_refs, land_refs):
        _, _, c, p = place
        return [(s.at[c], l.at[p, c], peer, l.at[q, c]) for s, l, which in zip(src_refs, land_refs, chips, strict=True)
                for peer, q in _other_chips(place, which)]
    return plan


def _plan_relay(which):
    def plan(place, src_refs, land_refs):
        x, y, c, _ = place
        return [(l.at[q, c], l.at[q, c], (x, y, 1 - c), l.at[q, 1 - c]) for l in land_refs for _, q in _other_chips(place, which)]
    return plan


def _plan_swap(place, src_refs, land_refs):
    x, y, c, _ = place
    return [(s.at[k, 1 - c], l.at[k], (x, y, 1 - c), l.at[k]) for s, l in zip(src_refs, land_refs) for k in range(N_CHIPS)]


def _plan_scatter(place, src_refs, land_refs):
    return [(s.at[q], l.at[j], peer, l.at[j]) for s, l in zip(src_refs, land_refs)
            for j, (peer, q) in enumerate(_other_chips(place))]


def _plan_share(place, src_refs, land_refs):
    x, y, c, _ = place
    return [(s, l, (x, y, 1 - c), l) for s, l in zip(src_refs, land_refs)]


def _pack_rows(parts, n_rows, name, deps=()):
    def body(*refs):
        refs = refs[len(deps):]
        out_ref = refs[-1]
        out_ref[...] = jnp.zeros((n_rows, D), F32)
        at = 0
        for ref in refs[:-1]:
            k = ref.shape[0]
            out_ref[at:at + k, :] = ref[...]
            at += k

    return pl.pallas_call(
        body, out_shape=jax.ShapeDtypeStruct((n_rows, D), F32), in_specs=[_ANY] * len(deps) + [_VMEM] * len(parts),
        out_specs=_VMEM, name=name)(*deps, *parts)


TS_MM = 512
TS_IN = 1024
TS_GW = 1024
TS_MIX = 256


def _halved(a):
    n, r, cols = a.shape
    return a.reshape(n, 2, r // 2, cols)


SIBLING_IDS = (1, 2)


def _rs_swap(name, grads, after=()):
    lands = [lax.empty((N_CHIPS,) + g.shape[2:], g.dtype) for g in grads]
    sems, grads, lands, token = _xchg_start(name + "_swap", _plan_swap, N_CHIPS * len(grads), grads, lands, after,
                                            sibling_id=SIBLING_IDS[0])
    return name, sems, grads, lands, token


def _rs_scatter(handle, after, chip, ci):
    name, sems, grads, lands, _ = handle
    grads, from_sibling = _xchg_wait(name + "_swap_wait", _plan_swap, sems, grads, lands, after)
    c_arr = jnp.reshape(ci, (1,)).astype(jnp.int32)
    pair_sums = _add_halves(list(grads), list(from_sibling), c_arr, name + "_add_halves")
    lands = [lax.empty((N_CHIPS - 1,) + p.shape[1:], p.dtype) for p in pair_sums]
    sems, pair_sums, lands, token = _xchg_start(name + "_scatter", _plan_scatter, 3 * len(pair_sums), pair_sums, lands)
    return name, sems, pair_sums, lands, jnp.reshape(chip, (1,)).astype(jnp.int32), token


def _rs_share(handle, after):
    name, sems, pair_sums, lands, chip_idx, _ = handle
    pair_sums, received = _xchg_wait(name + "_scatter_wait", _plan_scatter, sems, pair_sums, lands, after)
    halves = _sum_chips(list(pair_sums), list(received), chip_idx, name + "_sum_chips")
    lands = [lax.empty(h.shape, h.dtype) for h in halves]
    sems, halves, lands, token = _xchg_start(name + "_share", _plan_share, len(halves), halves, lands,
                                             sibling_id=SIBLING_IDS[1])
    return name, sems, halves, lands, token


def _rs_end(handle, after):
    name, sems, halves, lands, _ = handle
    halves, others = _xchg_wait(name + "_share_wait", _plan_share, sems, halves, lands, after)
    return list(zip(halves, others))


def kernel(x, c, w_ada, b_ada, g_norm_mix, w_in, conv_a_w, conv_b_w, conv_b_bias, w_rg_a, b_rg_a, w_rg_x, b_rg_x, lru_lambda, w_out, g_norm_ffn, w_gate_up, w_down, g_norm_final, loss_target, m_w_ada, m_b_ada, m_g_norm_mix, m_w_in, m_conv_a_w, m_conv_b_w, m_conv_b_bias, m_w_rg_a, m_b_rg_a, m_w_rg_x, m_b_rg_x, m_lru_lambda, m_w_out, m_g_norm_ffn, m_w_gate_up, m_w_down, m_g_norm_final, v_w_ada, v_b_ada, v_g_norm_mix, v_w_in, v_conv_a_w, v_conv_b_w, v_conv_b_bias, v_w_rg_a, v_b_rg_a, v_w_rg_x, v_b_rg_x, v_lru_lambda, v_w_out, v_g_norm_ffn, v_w_gate_up, v_w_down, v_g_norm_final):
    xi, yi, ci = lax.axis_index("x"), lax.axis_index("y"), lax.axis_index("c")
    chip = 2 * xi + yi
    me = 2 * chip + ci
    n_ada = w_ada.shape[2]

    def widen(w):
        return jnp.pad(w, ((0, 0), (0, D - w.shape[1])))

    got = _allgather8(_pack_rows([c, widen(conv_a_w[0]), widen(conv_b_w[0])], 8, "pack_c_conv"), "gather_c_conv")
    c_all = got[:, 0, :]
    conv_full = got[::2, 1:8, :D // N_CHIPS].transpose(1, 0, 2).reshape(7, D)

    mod_part, c_act = _ada_fwd(c_all, w_ada[0], lax.dynamic_slice_in_dim(b_ada, chip * n_ada, n_ada, axis=1))
    mod_all = _allgather8(mod_part, "gather_mod")
    mod_mine = lax.dynamic_index_in_dim(mod_all, me, axis=1, keepdims=False)[::2].reshape(6, D)
    vecs = _pack_rows([mod_mine, g_norm_mix, g_norm_ffn, g_norm_final.reshape(1, D), conv_b_bias, b_rg_a, b_rg_x, lru_lambda,
                       conv_full], N_VEC, "pack_vecs")

    def rg_shard(w):
        return w[0].astype(BF16).reshape(2, HEADS * HB // N_CHIPS // 2, HB)

    shards = [w_in[0].astype(BF16).reshape(2, D // 2, C_IN), rg_shard(w_rg_a), rg_shard(w_rg_x),
              w_out[0].astype(BF16).reshape(2, D // N_CHIPS // 2, D), w_gate_up[0].astype(BF16).reshape(2, D // 2, C_GU),
              w_down[0].astype(BF16).reshape(2, D_FF // N_CHIPS // 2, D)]
    lands = [lax.dynamic_update_index_in_dim(lax.empty((N_CHIPS,) + s.shape, s.dtype), s, chip, 0) for s in shards]

    def send(name, first, last, after, chips):
        copies = [(k, j) for k, which in zip(range(first, last), chips, strict=True) for j in which]
        sems, srcs, zone, token = _xchg_start(name + "_ici", _plan_gather_ici(chips), len(copies), shards[first:last],
                                              lands[first:last], after)
        shards[first:last], lands[first:last] = srcs, zone
        return sems, copies, token

    def arrive(name, sent, first, last, after):
        sems, copies, _ = sent
        chips = [tuple(j for k, j in copies if k == want) for want in range(first, last)]
        ids = [copies.index((k, j)) for k, which in zip(range(first, last), chips) for j in which]
        srcs, zone = _xchg_wait(name + "_ici_wait", _plan_gather_ici(chips), sems, shards[first:last], lands[first:last], after,
                                ids)
        shards[first:last], lands[first:last] = srcs, zone

    def relay(name, first, last, which, sibling_id):
        plan = _plan_relay(which)
        sems, _, zone, token = _xchg_start(name + "_d2d", plan, len(which) * (last - first), [], lands[first:last],
                                           sibling_id=sibling_id)
        lands[first:last] = zone
        return name, plan, sems, first, last, token

    def relayed(handle, after):
        name, plan, sems, first, last, _ = handle
        lands[first:last] = _xchg_wait(name + "_d2d_wait", plan, sems, [], lands[first:last], after)[1]

    def to_blocks(v):
        return v.reshape(-1, TS_MIX // TIME_BLOCKS, TIME_BLOCKS, D).transpose(0, 2, 1, 3).reshape(v.shape)

    def from_blocks(v):
        return v.reshape(-1, TIME_BLOCKS, TS_MIX // TIME_BLOCKS, D).transpose(0, 2, 1, 3).reshape(v.shape)

    def chip_index(j):
        return jnp.reshape(chip ^ j, (1,)).astype(jnp.int32)

    def wg_in():
        return lands[0].reshape(N_CHIPS, D, C_IN)

    xs, target = to_blocks(x[0]), to_blocks(loss_target[0])
    sent_near = send("gather_in_near", 0, 1, [vecs], [(1, 2)])
    ts_in = min(TS_IN, xs.shape[0])
    h1, proj = _fwd_in_first(xs, vecs, wg_in(), chip_index(0), ts_in, deps=[sent_near[-1]])
    arrive("gather_in_near", sent_near, 0, 1, [proj])
    near = relay("gather_in_near", 0, 1, (1, 2), SIBLING_IDS[0])
    sent_rest = send("gather_rest", 0, 6, [near[-1]], [(3,)] + [(1, 2, 3)] * 5)
    relayed(near, [sent_rest[-1]])
    proj = _fwd_in_more(h1, wg_in(), proj, chip_index(1), ts_in, "fwd_in_y")
    proj = _fwd_in_more(h1, wg_in(), proj, chip_index(2), ts_in, "fwd_in_x")
    arrive("gather_in_far", sent_rest, 0, 1, [proj])
    far = relay("gather_in_far", 0, 1, (3,), SIBLING_IDS[1])
    arrive("gather_mix", sent_rest, 1, 4, [far[-1]])
    relayed(far, [far[-1]])
    mix = relay("gather_mix", 1, 4, (1, 2, 3), SIBLING_IDS[0])
    proj = _fwd_in_more(h1, wg_in(), proj, chip_index(3), ts_in, "fwd_in_xy", deps=[mix[-1]])
    relayed(mix, [proj])
    wg_rga, wg_rgx, wg_out = lands[1:4]
    wg_out = wg_out.reshape(D, D)

    def rg_full(wg):
        return wg.reshape(N_CHIPS, HEADS, HB // N_CHIPS, HB).transpose(1, 0, 2, 3).reshape(HEADS, HB, HB)

    wg_rga, wg_rgx = rg_full(wg_rga), rg_full(wg_rgx)

    arrive("gather_ffn", sent_rest, 4, 6, [proj])
    ffn = relay("gather_ffn", 4, 6, (1, 2, 3), SIBLING_IDS[1])
    x1, merged, z1, kept, decay = _fwd_mix(proj, xs, vecs, wg_rga, wg_rgx, wg_out, TS_MIX, deps=[ffn[-1]])
    relayed(ffn, [x1])
    wg_gu, wg_dn = lands[4:6]
    wg_gu, wg_dn = wg_gu.reshape(N_CHIPS, D, C_GU), wg_dn.reshape(D_FF, D)
    dx1, h2, act, dz2, dgu, sm_ffn = _ffn_loss(x1, target, vecs, wg_gu, wg_dn, TS_MIX)

    def rg_chunks(dw):
        return _halved(dw.reshape(HEADS, N_CHIPS, HB // N_CHIPS, HB).transpose(1, 0, 2, 3).reshape(N_CHIPS, HB, HB).astype(BF16))

    ts_gw = min(TS_GW, xs.shape[0])
    g_dn = _grad_w(act, dz2, 1, ts_gw, "grad_w_down")
    g_gu = _grad_w(h2, dgu, N_CHIPS, ts_gw, "grad_w_gate_up")
    rs_b = _rs_swap("rs_b", [_halved(g_gu), _halved(g_dn.reshape(N_CHIPS, D_FF // N_CHIPS, D))])
    dproj, sm_mix, dw_rga, dw_rgx, dw_out = _bwd_mix(dx1, z1, merged, proj, kept, decay, vecs, wg_rga, wg_rgx, wg_out, sm_ffn, TS_MIX,
                                                     deps=[rs_b[-1]])
    rs_b = _rs_scatter(rs_b, [dproj], chip, ci)
    g_in = _grad_w(h1, dproj, N_CHIPS, ts_gw, "grad_w_in", deps=[rs_b[-1]])
    rs_b = _rs_share(rs_b, [g_in])
    rs_a = _rs_swap("rs_a", [_halved(g_in), rg_chunks(dw_rga), rg_chunks(dw_rgx),
                             _halved(dw_out.astype(BF16).reshape(N_CHIPS, D // N_CHIPS, D))], after=[rs_b[-1]])

    c_arr = jnp.reshape(ci, (1,)).astype(jnp.int32)

    def step_halves(name, items, deps=()):
        two_d = lambda a: a.reshape(-1, a.shape[-1])
        sets = [(two_d(w), halves[0], halves[1], two_d(m), two_d(v)) for w, halves, m, v in items.values()]
        for (n, (w, _, _, _)), out in zip(items.items(), _adamw_halves(sets, c_arr, name, deps)):
            res[n] = tuple(a.reshape(w.shape) for a in out)

    def shard_cols(row_block):
        return lax.dynamic_slice_in_dim(row_block, chip * (D // N_CHIPS), D // N_CHIPS, axis=1)

    gw_gu, gw_dn = _rs_end(rs_b, [rs_a[-1]])
    res = {}
    step_halves("adamw_ffn", {"w_gate_up": (w_gate_up, gw_gu, m_w_gate_up, v_w_gate_up),
                              "w_down": (w_down, gw_dn, m_w_down, v_w_down)}, [rs_a[-1]])
    rs_a = _rs_scatter(rs_a, [res["w_gate_up"][1], res["w_down"][1]], chip, ci)
    grad_x, sm_in = _bwd_in(dproj, xs, dx1, vecs, wg_in(), sm_mix, TS_MM, deps=[rs_a[-1]])
    rs_a = _rs_share(rs_a, [grad_x])

    small, per_dev = _sum_small(_allgather8(sm_in, "gather_small", deps=[rs_a[-1]]))
    dmod_all = per_dev[:, 0:6, :].reshape(N_DEV, 6 * D)
    grad_b_ada = small[0:6].reshape(1, 6 * D)
    res["w_ada"] = tuple(a[None] for a in _adamw_ada(w_ada[0], c_act, lax.dynamic_slice_in_dim(dmod_all, chip * n_ada, n_ada, axis=1),
                                                     m_w_ada[0], v_w_ada[0]))
    small_sets = {
        "b_ada": (b_ada.reshape(6, D), grad_b_ada.reshape(6, D), m_b_ada.reshape(6, D), v_b_ada.reshape(6, D)),
        "g_norm_mix": (g_norm_mix, small[G_GMIX:G_GMIX + 1], m_g_norm_mix, v_g_norm_mix),
        "conv_a_w": (conv_a_w[0], shard_cols(small[G_WA0:G_WA0 + 3]), m_conv_a_w[0], v_conv_a_w[0]),
        "conv_b_w": (conv_b_w[0], shard_cols(small[G_WB0:G_WB0 + 4]), m_conv_b_w[0], v_conv_b_w[0]),
        "conv_b_bias": (conv_b_bias, small[G_CBB:G_CBB + 1], m_conv_b_bias, v_conv_b_bias),
        "b_rg_a": (b_rg_a, small[G_BA:G_BA + 1], m_b_rg_a, v_b_rg_a),
        "b_rg_x": (b_rg_x, small[G_BX:G_BX + 1], m_b_rg_x, v_b_rg_x),
        "lru_lambda": (lru_lambda, small[G_LAM:G_LAM + 1], m_lru_lambda, v_lru_lambda),
        "g_norm_ffn": (g_norm_ffn, small[G_GFFN:G_GFFN + 1], m_g_norm_ffn, v_g_norm_ffn),
        "g_norm_final": (g_norm_final.reshape(1, D), small[G_GFIN:G_GFIN + 1], m_g_norm_final.reshape(1, D),
                         v_g_norm_final.reshape(1, D)),
    }
    stepped = _adamw_small(list(small_sets.values()), "adamw_small")
    for (n, (w_, g_, _, _)), (d_, nm_, nv_) in zip(small_sets.items(), stepped):
        shape = (1,) + w_.shape if n.startswith("conv_") and n != "conv_b_bias" else w_.shape
        res[n] = tuple(a.reshape(shape) for a in (g_, d_, nm_, nv_))
    gw_in, gw_rga, gw_rgx, gw_out = _rs_end(rs_a, [res[n][1] for n in res])
    step_halves("adamw_mix", {"w_in": (w_in, gw_in, m_w_in, v_w_in), "w_rg_a": (w_rg_a, gw_rga, m_w_rg_a, v_w_rg_a),
                              "w_rg_x": (w_rg_x, gw_rgx, m_w_rg_x, v_w_rg_x), "w_out": (w_out, gw_out, m_w_out, v_w_out)})
    res["b_ada"] = tuple(a.reshape(1, 6 * D) for a in res["b_ada"])
    res["g_norm_final"] = tuple(a.reshape(D) for a in res["g_norm_final"])
    names = ["w_ada", "b_ada", "g_norm_mix", "w_in", "conv_a_w", "conv_b_w", "conv_b_bias", "w_rg_a", "b_rg_a", "w_rg_x",
             "b_rg_x", "lru_lambda", "w_out", "g_norm_ffn", "w_gate_up", "w_down", "g_norm_final"]
    loss = jnp.sum(small[G_LOSS])
    return (loss, from_blocks(grad_x)[None], *[res[n][0] for n in names], *[res[n][1] for n in names],
            *[res[n][2] for n in names], *[res[n][3] for n in names])
```

```python
import functools

import jax
import jax.numpy as jnp
from jax import lax
from jax.experimental import pallas as pl
from jax.experimental.pallas import tpu as pltpu

F32 = jnp.float32
BF16 = jnp.bfloat16
MESH = pl.DeviceIdType.MESH

D = 1024
N_CHIPS = 4
N_DEV = 8
D_IN = 7 * D
C_IN = D_IN // N_CHIPS
D_FF = 2816
C_GU = 2 * D_FF // N_CHIPS
HEADS = 4
HB = D // HEADS
EPS = 1e-6
LRU_C = 8.0
ADAM_LR, ADAM_B1, ADAM_B2, ADAM_EPS, ADAM_WD, ADAM_STEP = 0.001, 0.9, 0.999, 1e-08, 0.01, 10
VMEM_LIMIT = 56 << 20

(V_SH1, V_SC1, V_GT1, V_SH2, V_SC2, V_GT2, V_GMIX, V_GFFN, V_GFIN, V_CBB, V_BA, V_BX, V_LAM,
 V_WA0, V_WA1, V_WA2, V_WB0, V_WB1, V_WB2, V_WB3) = range(20)
N_VEC = 24
(G_SH1, G_SC1, G_GT1, G_SH2, G_SC2, G_GT2, G_GMIX, G_CBB, G_BA, G_BX, G_LAM, G_GFFN, G_GFIN,
 G_WA0, G_WA1, G_WA2, G_WB0, G_WB1, G_WB2, G_WB3, G_LOSS) = range(21)
N_SMALL = 24

_VMEM = pl.BlockSpec(memory_space=pltpu.VMEM)
_ANY = pl.BlockSpec(memory_space=pl.ANY)


def _cparams(n_grid=1):
    return pltpu.CompilerParams(dimension_semantics=("arbitrary",) * n_grid, vmem_limit_bytes=VMEM_LIMIT)


def _after(deps, body):
    n = len(deps)
    return lambda *refs: body(*refs[n:])


def _rms(x):
    rstd = lax.rsqrt(jnp.mean(x * x, axis=-1, keepdims=True) + EPS)
    return x * rstd, rstd


def _rms_bwd(dxhat, xhat, rstd):
    return rstd * (dxhat - xhat * jnp.mean(dxhat * xhat, axis=-1, keepdims=True))


def _rowsum(v):
    return jnp.sum(v, axis=0, keepdims=True)


def _dot(a, b):
    return jnp.dot(a, b, preferred_element_type=F32)


def _dot_nt(a, b):
    return lax.dot_general(a, b, (((1,), (1,)), ((), ())), preferred_element_type=F32)


def _dot_tn(a, b):
    return lax.dot_general(a, b, (((0,), (0,)), ((), ())), preferred_element_type=F32)


def _gelu(x):
    k, c = 0.7978845608028654, 0.044715
    t = jnp.tanh(k * (x + c * x * x * x))
    return 0.5 * x * (1.0 + t), 0.5 * (1.0 + t) + 0.5 * x * (1.0 - t * t) * k * (1.0 + 3.0 * c * x * x)


def _log_sigmoid(lam):
    return jnp.minimum(lam, 0.0) - jnp.log1p(jnp.exp(-jnp.abs(lam)))


def _lru_gates(u, wa_ref, wx_ref, v_ref, row0):
    ub = u.astype(BF16)
    pre_a = jnp.concatenate([_dot(ub[:, h * HB:(h + 1) * HB], wa_ref[h]) for h in range(HEADS)], axis=1)
    pre_x = jnp.concatenate([_dot(ub[:, h * HB:(h + 1) * HB], wx_ref[h]) for h in range(HEADS)], axis=1)
    r = jax.nn.sigmoid(pre_a + v_ref[V_BA:V_BA + 1, :])
    ig = jax.nn.sigmoid(pre_x + v_ref[V_BX:V_BX + 1, :])
    log_a = LRU_C * r * _log_sigmoid(v_ref[V_LAM:V_LAM + 1, :])
    a = jnp.exp(log_a)
    x2 = 2.0 * log_a
    m2 = jnp.where(x2 > -0.03, -x2 * (1.0 + x2 * (0.5 + x2 * (1.0 / 6.0 + x2 * (1.0 / 24.0)))), 1.0 - a * a)
    mult = jnp.where(row0, 1.0, jnp.sqrt(jnp.maximum(m2, 0.0)))
    return r, ig, a, mult


TIME_BLOCKS = 8
N_KEPT = 10


def _late_blocks(v, buf, g, halo=None):
    n = buf.shape[0]
    out = []
    for idx in range(n):
        k = TIME_BLOCKS - n + idx
        buf[idx, 8:g + 8, :] = v[k * g:(k + 1) * g]
        if halo is not None:
            buf[idx, 7:8, :] = halo[idx]
        out.append(buf[idx, pl.ds(7, g), :])
        if halo is None:
            buf[idx, 7:8, :] = buf[idx, g + 7:g + 8, :]
    return out


def _earlier(v, s, late, g):
    return jnp.concatenate(late[len(late) - s:] + [v[0:(TIME_BLOCKS - s) * g]], axis=0)


def _early_blocks(v, buf, g):
    out = []
    for k in range(buf.shape[0]):
        buf[k, 0:g, :] = v[k * g:(k + 1) * g]
        out.append(buf[k, pl.ds(1, g), :])
        buf[k, g:g + 1, :] = buf[k, 0:1, :]
    return out


def _later(v, s, early, g):
    return jnp.concatenate([v[s * g:]] + early[0:s], axis=0)


def _fwd_in_first(x, vecs, w_in_g, q_idx, ts, deps=()):
    s = x.shape[0]

    def body(q_ref, x_ref, v_ref, w_ref, h1_ref, proj_ref):
        xhat, _ = _rms(x_ref[...])
        h = xhat * v_ref[V_GMIX:V_GMIX + 1, :] * (1.0 + v_ref[V_SC1:V_SC1 + 1, :]) + v_ref[V_SH1:V_SH1 + 1, :]
        hb = h.astype(BF16)
        h1_ref[...] = hb
        proj_ref[...] = _dot(hb, w_ref[...]).astype(BF16)

    return pl.pallas_call(
        lambda q_ref, *refs: body(q_ref, *refs[len(deps):]),
        grid_spec=pltpu.PrefetchScalarGridSpec(
            num_scalar_prefetch=1, grid=(s // ts,),
            in_specs=[_ANY] * len(deps) + [pl.BlockSpec((ts, D), lambda i, q: (i, 0)), _VMEM,
                                           pl.BlockSpec((None, D, C_IN), lambda i, q: (q[0], 0, 0))],
            out_specs=[pl.BlockSpec((ts, D), lambda i, q: (i, 0)), pl.BlockSpec((ts, C_IN), lambda i, q: (i, q[0]))]),
        out_shape=(jax.ShapeDtypeStruct((s, D), BF16), jax.ShapeDtypeStruct((s, D_IN), BF16)),
        compiler_params=_cparams(), name="fwd_in_own")(q_idx, *deps, x, vecs, w_in_g)


def _fwd_in_more(h1, w_in_g, proj, q_idx, ts, name, deps=()):
    s = h1.shape[0]

    def body(q_ref, h1_ref, w_ref, proj_in_ref, proj_ref):
        proj_ref[...] = _dot(h1_ref[...], w_ref[...]).astype(BF16)

    return pl.pallas_call(
        lambda q_ref, *refs: body(q_ref, *refs[len(deps):]),
        grid_spec=pltpu.PrefetchScalarGridSpec(
            num_scalar_prefetch=1, grid=(s // ts,),
            in_specs=[_ANY] * len(deps) + [pl.BlockSpec((ts, D), lambda i, q: (i, 0)),
                                           pl.BlockSpec((None, D, C_IN), lambda i, q: (q[0], 0, 0)), _ANY],
            out_specs=pl.BlockSpec((ts, C_IN), lambda i, q: (i, q[0]))),
        out_shape=jax.ShapeDtypeStruct((s, D_IN), BF16), input_output_aliases={len(deps) + 3: 0},
        compiler_params=_cparams(), name=name)(q_idx, *deps, h1, w_in_g, proj)


def _fwd_mix(proj, x, vecs, w_rga, w_rgx, w_out, ts, deps=()):
    s = x.shape[0]
    g = ts // TIME_BLOCKS

    def body(proj_ref, x_ref, v_ref, wa_ref, wx_ref, wo_ref, x1_ref, mg_ref, z1_ref, kept_ref, decay_ref,
             ua_buf, rx_buf, p_buf, q_buf, c_buf, hcarry):
        i = pl.program_id(0)

        @pl.when(i == 0)
        def _():
            ua_buf[...] = jnp.zeros(ua_buf.shape, F32)
            rx_buf[...] = jnp.zeros(rx_buf.shape, F32)
            hcarry[...] = jnp.zeros((8, D), F32)

        def seg(j):
            return proj_ref[:, j * D:(j + 1) * D].astype(F32)

        def vrow(j):
            return v_ref[j:j + 1, :]

        cb, cc, cx, rx, rg, ga, gb = (seg(j) for j in range(7))
        ua = cc * cx
        ua_late = _late_blocks(ua, ua_buf, g)
        rx_late = _late_blocks(rx, rx_buf, g)
        va = vrow(V_WA2) * ua + vrow(V_WA1) * _earlier(ua, 1, ua_late, g) + vrow(V_WA0) * _earlier(ua, 2, ua_late, g)
        u = (vrow(V_WB3) * rx + vrow(V_WB2) * _earlier(rx, 1, rx_late, g) + vrow(V_WB1) * _earlier(rx, 2, rx_late, g)
             + vrow(V_WB0) * _earlier(rx, 3, rx_late, g) + vrow(V_CBB))

        rows = lax.broadcasted_iota(jnp.int32, (ts, D), 0)
        row0 = jnp.logical_and(rows == 0, i == 0)
        r, ig, a, mult = _lru_gates(u, wa_ref, wx_ref, v_ref, row0)
        decay_ref[...] = a
        bx = mult * (ig * u)

        prods, sums = [a[0:g]], [bx[0:g]]
        for k in range(1, TIME_BLOCKS):
            ak = a[k * g:(k + 1) * g]
            sums.append(ak * sums[-1] + bx[k * g:(k + 1) * g])
            prods.append(ak * prods[-1])
        p_buf[...] = prods[-1]
        q_buf[...] = sums[-1]
        state = hcarry[0:1, :]
        for j in range(g):
            c_buf[j:j + 1, :] = state
            state = p_buf[j:j + 1, :] * state + q_buf[j:j + 1, :]
        hcarry[0:1, :] = state
        entering = c_buf[...]
        h = jnp.concatenate([sums[k] + prods[k] * entering for k in range(TIME_BLOCKS)], axis=0)

        gel, dgel = _gelu(rg)
        sga = jax.nn.sigmoid(ga)
        sgb = jax.nn.sigmoid(gb)
        for j, keep in enumerate((va, r, ig, sga, sgb, gel, dgel, mult, u, h)):
            kept_ref[:, j * D:(j + 1) * D] = keep.astype(BF16)
        merged = (sga * (cb * va) + sgb * (h * gel)).astype(BF16)
        mg_ref[...] = merged
        z1 = _dot(merged, wo_ref[...])
        z1_ref[...] = z1.astype(BF16)
        x1_ref[...] = x_ref[...] + vrow(V_GT1) * z1

    row = lambda i: (i, 0)
    return pl.pallas_call(
        _after(deps, body), grid=(s // ts,),
        out_shape=(jax.ShapeDtypeStruct((s, D), F32), jax.ShapeDtypeStruct((s, D), BF16), jax.ShapeDtypeStruct((s, D), BF16),
                   jax.ShapeDtypeStruct((s, N_KEPT * D), BF16), jax.ShapeDtypeStruct((s, D), F32)),
        in_specs=[_ANY] * len(deps) + [pl.BlockSpec((ts, D_IN), row), pl.BlockSpec((ts, D), row), _VMEM, _VMEM, _VMEM, _VMEM],
        out_specs=[pl.BlockSpec((ts, D), row)] * 3 + [pl.BlockSpec((ts, N_KEPT * D), row), pl.BlockSpec((ts, D), row)],
        scratch_shapes=[pltpu.VMEM((2, g + 8, D), F32), pltpu.VMEM((3, g + 8, D), F32), pltpu.VMEM((g, D), F32),
                        pltpu.VMEM((g, D), F32), pltpu.VMEM((g, D), F32), pltpu.VMEM((8, D), F32)],
        compiler_params=_cparams(), name="fwd_mix")(*deps, proj, x, vecs, w_rga, w_rgx, w_out)


def _ffn_loss(x1, target, vecs, w_gu_g, w_dn, ts):
    s = x1.shape[0]

    def body(x1_ref, t_ref, v_ref, wgu_ref, wdn_ref, dx1_ref, h2_ref, act_ref, dz2_ref, dgu_ref, sm_ref):
        @pl.when(pl.program_id(0) == 0)
        def _():
            sm_ref[...] = jnp.zeros((N_SMALL, D), F32)

        def vrow(j):
            return v_ref[j:j + 1, :]

        n_sub = 1
        rows = [slice(k * (ts // n_sub), (k + 1) * (ts // n_sub)) for k in range(n_sub)]
        subs = [dict(r=r, sums={}) for r in rows]

        def stage_norm(t):
            t["x1"] = x1_ref[t["r"], :]
            t["xh1"], t["rstd1"] = _rms(t["x1"])
            t["n2"] = t["xh1"] * vrow(V_GFFN)
            t["h2"] = (t["n2"] * (1.0 + vrow(V_SC2)) + vrow(V_SH2)).astype(BF16)
            h2_ref[t["r"], :] = t["h2"]

        def stage_up(t):
            h2 = t["h2"]
            g = jnp.concatenate([_dot(h2, wgu_ref[0]), _dot(h2, wgu_ref[1])], axis=1)
            t["up"] = jnp.concatenate([_dot(h2, wgu_ref[2]), _dot(h2, wgu_ref[3])], axis=1)
            t["g"] = g
            t["sg"] = jax.nn.sigmoid(g)
            t["silu"] = g * t["sg"]
            t["act"] = (t["silu"] * t["up"]).astype(BF16)
            act_ref[t["r"], :] = t["act"]

        def stage_down_loss(t):
            z2 = _dot(t["act"], wdn_ref[...])
            x2 = t["x1"] + vrow(V_GT2) * z2
            xh2, rstd2 = _rms(x2)
            err = xh2 * vrow(V_GFIN) - t_ref[t["r"], :]
            t["sums"][G_LOSS] = _rowsum((0.5 / D) * err * err)
            dy = err * (1.0 / D)
            t["sums"][G_GFIN] = _rowsum(dy * xh2)
            t["dx2"] = _rms_bwd(dy * vrow(V_GFIN), xh2, rstd2)
            t["sums"][G_GT2] = _rowsum(t["dx2"] * z2)
            t["dz2"] = (vrow(V_GT2) * t["dx2"]).astype(BF16)
            dz2_ref[t["r"], :] = t["dz2"]

        def stage_back_act(t):
            dact = _dot_nt(t["dz2"], wdn_ref[...])
            g, sg = t["g"], t["sg"]
            t["dgate"] = (dact * t["up"] * (sg * (1.0 + g * (1.0 - sg)))).astype(BF16)
            t["dup"] = (dact * t["silu"]).astype(BF16)
            dgu_ref[t["r"], 0:D_FF] = t["dgate"]
            dgu_ref[t["r"], D_FF:2 * D_FF] = t["dup"]

        def stage_back_norm(t):
            dgate, dup = t["dgate"], t["dup"]
            dh2 = (_dot_nt(dgate[:, 0:C_GU], wgu_ref[0]) + _dot_nt(dgate[:, C_GU:2 * C_GU], wgu_ref[1])
                   + _dot_nt(dup[:, 0:C_GU], wgu_ref[2]) + _dot_nt(dup[:, C_GU:2 * C_GU], wgu_ref[3]))
            t["sums"][G_SH2] = _rowsum(dh2)
            t["sums"][G_SC2] = _rowsum(dh2 * t["n2"])
            dn2 = dh2 * (1.0 + vrow(V_SC2))
            t["sums"][G_GFFN] = _rowsum(dn2 * t["xh1"])
            dx1_ref[t["r"], :] = t["dx2"] + _rms_bwd(dn2 * vrow(V_GFFN), t["xh1"], t["rstd1"])

        for stage in (stage_norm, stage_up, stage_down_loss, stage_back_act, stage_back_norm):
            for t in subs:
                stage(t)
        for j in subs[0]["sums"]:
            total = subs[0]["sums"][j]
            for t in subs[1:]:
                total = total + t["sums"][j]
            sm_ref[j:j + 1, :] += total

    row = lambda i: (i, 0)
    return pl.pallas_call(
        body, grid=(s // ts,),
        out_shape=(jax.ShapeDtypeStruct((s, D), F32), jax.ShapeDtypeStruct((s, D), BF16), jax.ShapeDtypeStruct((s, D_FF), BF16),
                   jax.ShapeDtypeStruct((s, D), BF16), jax.ShapeDtypeStruct((s, 2 * D_FF), BF16),
                   jax.ShapeDtypeStruct((N_SMALL, D), F32)),
        in_specs=[pl.BlockSpec((ts, D), row), pl.BlockSpec((ts, D), row), _VMEM, _VMEM, _VMEM],
        out_specs=[pl.BlockSpec((ts, D), row), pl.BlockSpec((ts, D), row), pl.BlockSpec((ts, D_FF), row),
                   pl.BlockSpec((ts, D), row), pl.BlockSpec((ts, 2 * D_FF), row), pl.BlockSpec((N_SMALL, D), lambda i: (0, 0))],
        compiler_params=_cparams(), name="ffn_loss")(x1, target, vecs, w_gu_g, w_dn)


def _bwd_mix(dx1, z1, merged, proj, kept, decay, vecs, w_rga, w_rgx, w_out, small, ts, deps=()):
    s = dx1.shape[0]
    nt = s // ts
    g = ts // TIME_BLOCKS
    assert g % 16 == 0

    def body(dx1_ref, z1_ref, mg_ref, proj_ref, kept_ref, decay_ref, hh_ref, v_ref, wa_ref, wx_ref,
             wo_ref, sm0_ref, dproj_ref, sm_ref, dwa_ref, dwx_ref, dwo_ref,
             h_buf, a_buf, dva_buf, du_buf, p_buf, q_buf, c_buf, lcarry):
        i = pl.program_id(0)
        first_tile = i == nt - 1

        @pl.when(i == 0)
        def _():
            a_buf[...] = jnp.zeros(a_buf.shape, F32)
            dva_buf[...] = jnp.zeros(dva_buf.shape, F32)
            du_buf[...] = jnp.zeros(du_buf.shape, F32)
            lcarry[...] = jnp.zeros((8, D), F32)
            sm_ref[...] = sm0_ref[...]
            dwa_ref[...] = jnp.zeros((HEADS, HB, HB), F32)
            dwx_ref[...] = jnp.zeros((HEADS, HB, HB), F32)
            dwo_ref[...] = jnp.zeros((D, D), F32)

        def seg(j):
            return proj_ref[:, j * D:(j + 1) * D].astype(F32)

        def vrow(j):
            return v_ref[j:j + 1, :]

        def acc(j, val):
            sm_ref[j:j + 1, :] += _rowsum(val)

        cb, cc, cx, rx = (seg(j) for j in range(4))
        ua = cc * cx
        va, r, ig, sga, sgb, gel, dgel, mult, u, h = (kept_ref[:, j * D:(j + 1) * D].astype(F32) for j in range(N_KEPT))
        a = decay_ref[...]
        rows = lax.broadcasted_iota(jnp.int32, (ts, D), 0)
        row0 = jnp.logical_and(rows == 0, first_tile)

        dx1 = dx1_ref[...]
        acc(G_GT1, dx1 * z1_ref[...].astype(F32))
        dz1 = (vrow(V_GT1) * dx1).astype(BF16)
        dwo_ref[...] += _dot_tn(mg_ref[...], dz1)
        dmg = _dot_nt(dz1, wo_ref[...])
        dya = dmg * sga
        dyb = dmg * sgb
        dproj_ref[:, 5 * D:6 * D] = (dya * (cb * va) * (1.0 - sga)).astype(BF16)
        dproj_ref[:, 6 * D:7 * D] = (dyb * (h * gel) * (1.0 - sgb)).astype(BF16)

        dproj_ref[:, 0:D] = (dya * va).astype(BF16)
        dva = dya * cb
        dva_early = _early_blocks(dva, dva_buf, g)
        dva1 = _later(dva, 1, dva_early, g)
        dva2 = _later(dva, 2, dva_early, g)
        dua = vrow(V_WA2) * dva + vrow(V_WA1) * dva1 + vrow(V_WA0) * dva2
        acc(G_WA2, ua * dva)
        acc(G_WA1, ua * dva1)
        acc(G_WA0, ua * dva2)
        dproj_ref[:, D:2 * D] = (dua * cx).astype(BF16)
        dproj_ref[:, 2 * D:3 * D] = (dua * cc).astype(BF16)

        dproj_ref[:, 4 * D:5 * D] = (dyb * h * dgel).astype(BF16)
        a_next = _later(a, 1, _early_blocks(a, a_buf, g), g)
        dh = dyb * gel
        last = TIME_BLOCKS - 1
        prods, sums = {last: a_next[last * g:]}, {last: dh[last * g:]}
        for k in range(last - 1, -1, -1):
            ak = a_next[k * g:(k + 1) * g]
            sums[k] = dh[k * g:(k + 1) * g] + ak * sums[k + 1]
            prods[k] = ak * prods[k + 1]
        p_buf[...] = prods[0]
        q_buf[...] = sums[0]
        state = lcarry[0:1, :]
        for j in range(g - 1, -1, -1):
            c_buf[j:j + 1, :] = state
            state = q_buf[j:j + 1, :] + p_buf[j:j + 1, :] * state
        lcarry[0:1, :] = state
        entering = c_buf[...]
        lam = jnp.concatenate([sums[k] + prods[k] * entering for k in range(TIME_BLOCKS)], axis=0)

        last = lax.broadcasted_iota(jnp.int32, hh_ref.shape, 0) == hh_ref.shape[0] - 1
        h_halo = [jnp.where(first_tile, 0.0, jnp.sum(jnp.where(last, hh_ref[...].astype(F32), 0.0), axis=0, keepdims=True))]
        da = lam * _earlier(h, 1, _late_blocks(h, h_buf, g, h_halo), g)
        dmult = jnp.where(row0, 0.0, lam * (ig * u))
        di = lam * mult * u
        du = lam * mult * ig
        dlog_a = da * a - dmult * (a * a) / mult
        lam_p = vrow(V_LAM)
        dr = dlog_a * (LRU_C * _log_sigmoid(lam_p))
        sm_ref[G_LAM:G_LAM + 1, :] += _rowsum(dlog_a * r) * (LRU_C * jax.nn.sigmoid(-lam_p))
        dpa = dr * r * (1.0 - r)
        dpx = di * ig * (1.0 - ig)
        acc(G_BA, dpa)
        acc(G_BX, dpx)
        dpab = dpa.astype(BF16)
        dpxb = dpx.astype(BF16)
        ub = u.astype(BF16)
        back = []
        for hd in range(HEADS):
            cols = slice(hd * HB, (hd + 1) * HB)
            back.append(_dot_nt(dpab[:, cols], wa_ref[hd]) + _dot_nt(dpxb[:, cols], wx_ref[hd]))
            dwa_ref[hd] += _dot_tn(ub[:, cols], dpab[:, cols])
            dwx_ref[hd] += _dot_tn(ub[:, cols], dpxb[:, cols])
        du = du + jnp.concatenate(back, axis=1)

        acc(G_CBB, du)
        du_early = _early_blocks(du, du_buf, g)
        du1 = _later(du, 1, du_early, g)
        du2 = _later(du, 2, du_early, g)
        du3 = _later(du, 3, du_early, g)
        dproj_ref[:, 3 * D:4 * D] = (vrow(V_WB3) * du + vrow(V_WB2) * du1 + vrow(V_WB1) * du2 + vrow(V_WB0) * du3).astype(BF16)
        acc(G_WB3, rx * du)
        acc(G_WB2, rx * du1)
        acc(G_WB1, rx * du2)
        acc(G_WB0, rx * du3)

    rev = lambda i: (nt - 1 - i, 0)
    h_halo16 = lambda i: (jnp.maximum((nt - 1 - i) * (ts // 16) - 1, 0), N_KEPT - 1)
    const2 = lambda i: (0, 0)
    const3 = lambda i: (0, 0, 0)
    return pl.pallas_call(
        _after(deps, body), grid=(nt,),
        out_shape=(jax.ShapeDtypeStruct((s, D_IN), BF16), jax.ShapeDtypeStruct((N_SMALL, D), F32),
                   jax.ShapeDtypeStruct((HEADS, HB, HB), F32), jax.ShapeDtypeStruct((HEADS, HB, HB), F32),
                   jax.ShapeDtypeStruct((D, D), F32)),
        in_specs=[_ANY] * len(deps) + [pl.BlockSpec((ts, D), rev), pl.BlockSpec((ts, D), rev), pl.BlockSpec((ts, D), rev),
                  pl.BlockSpec((ts, 4 * D), rev), pl.BlockSpec((ts, N_KEPT * D), rev), pl.BlockSpec((ts, D), rev),
                  pl.BlockSpec((16, D), h_halo16), _VMEM, _VMEM, _VMEM, _VMEM, _VMEM],
        out_specs=[pl.BlockSpec((ts, D_IN), rev), pl.BlockSpec((N_SMALL, D), const2),
                   pl.BlockSpec((HEADS, HB, HB), const3), pl.BlockSpec((HEADS, HB, HB), const3), pl.BlockSpec((D, D), const2)],
        scratch_shapes=[pltpu.VMEM((1, g + 8, D), F32), pltpu.VMEM((1, g + 8, D), F32),
                        pltpu.VMEM((2, g + 8, D), F32), pltpu.VMEM((3, g + 8, D), F32), pltpu.VMEM((g, D), F32),
                        pltpu.VMEM((g, D), F32), pltpu.VMEM((g, D), F32), pltpu.VMEM((8, D), F32)],
        compiler_params=_cparams(), name="bwd_mix")(*deps, dx1, z1, merged, proj, kept, decay, kept, vecs, w_rga,
                                                    w_rgx, w_out, small)


def _bwd_in(dproj, x, dx1, vecs, w_in_g, small, ts, deps=()):
    s = x.shape[0]

    def body(dp_ref, x_ref, dx1_ref, v_ref, w_ref, sm0_ref, gx_ref, sm_ref):
        @pl.when(pl.program_id(0) == 0)
        def _():
            sm_ref[...] = sm0_ref[...]

        def vrow(j):
            return v_ref[j:j + 1, :]

        dh1 = _dot_nt(dp_ref[:, 0:C_IN], w_ref[0])
        for k in range(1, N_CHIPS):
            dh1 += _dot_nt(dp_ref[:, k * C_IN:(k + 1) * C_IN], w_ref[k])
        xh, rstd = _rms(x_ref[...])
        sm_ref[G_SH1:G_SH1 + 1, :] += _rowsum(dh1)
        sm_ref[G_SC1:G_SC1 + 1, :] += _rowsum(dh1 * (xh * vrow(V_GMIX)))
        dn1 = dh1 * (1.0 + vrow(V_SC1))
        sm_ref[G_GMIX:G_GMIX + 1, :] += _rowsum(dn1 * xh)
        gx_ref[...] = dx1_ref[...] + _rms_bwd(dn1 * vrow(V_GMIX), xh, rstd)

    row = lambda i: (i, 0)
    return pl.pallas_call(
        _after(deps, body), grid=(s // ts,),
        out_shape=(jax.ShapeDtypeStruct((s, D), F32), jax.ShapeDtypeStruct((N_SMALL, D), F32)),
        in_specs=[_ANY] * len(deps) + [pl.BlockSpec((ts, D_IN), row), pl.BlockSpec((ts, D), row), pl.BlockSpec((ts, D), row),
                                       _VMEM, _VMEM, _VMEM],
        out_specs=[pl.BlockSpec((ts, D), row), pl.BlockSpec((N_SMALL, D), lambda i: (0, 0))],
        compiler_params=_cparams(), name="bwd_in")(*deps, dproj, x, dx1, vecs, w_in_g, small)


def _grad_w(a, b, n_col_blocks, ts, name, deps=()):
    s, m = a.shape
    tn = b.shape[1] // n_col_blocks
    n_steps = s // ts

    def body(a_ref, b_ref, o_ref, acc_ref):
        k = pl.program_id(1)

        @pl.when(k == 0)
        def _():
            acc_ref[...] = jnp.zeros((m, tn), F32)

        acc_ref[...] += _dot_tn(a_ref[...], b_ref[...])

        @pl.when(k == n_steps - 1)
        def _():
            o_ref[...] = acc_ref[...].astype(BF16)

    return pl.pallas_call(
        _after(deps, body), grid=(n_col_blocks, n_steps),
        out_shape=jax.ShapeDtypeStruct((n_col_blocks, m, tn), BF16),
        in_specs=[_ANY] * len(deps) + [pl.BlockSpec((ts, m), lambda n, k: (k, 0)), pl.BlockSpec((ts, tn), lambda n, k: (k, n))],
        out_specs=pl.BlockSpec((None, m, tn), lambda n, k: (n, 0, 0)),
        scratch_shapes=[pltpu.VMEM((m, tn), F32)],
        compiler_params=_cparams(2), name=name)(*deps, a, b)


def _ada_fwd(c_all, w_ada, b_ada):
    n = w_ada.shape[1]

    def body(c_ref, w_ref, b_ref, o_ref, ca_ref):
        c = c_ref[...]
        ca = c * jax.nn.sigmoid(c)
        ca_ref[...] = ca
        o_ref[...] = jnp.dot(ca, w_ref[...], preferred_element_type=F32, precision=lax.Precision.HIGHEST) + b_ref[...]

    return pl.pallas_call(
        body, out_shape=(jax.ShapeDtypeStruct((N_DEV, n), F32), jax.ShapeDtypeStruct((N_DEV, D), F32)),
        in_specs=[_VMEM] * 3, out_specs=[_VMEM] * 2, compiler_params=_cparams(0), name="ada_fwd")(c_all, w_ada, b_ada)


def _sum_small(parts):
    def body(p_ref, o_ref, d_ref):
        tot = p_ref[0]
        for dev in range(1, N_DEV):
            tot = tot + p_ref[dev]
        o_ref[...] = tot
        d_ref[...] = p_ref[:, 0:8, :]

    return pl.pallas_call(
        body, out_shape=(jax.ShapeDtypeStruct((N_SMALL, D), F32), jax.ShapeDtypeStruct((N_DEV, 8, D), F32)),
        in_specs=[_VMEM], out_specs=[_VMEM] * 2, compiler_params=_cparams(0), name="sum_small")(parts)


def _adamw_small(items, name):
    n = len(items)

    def body(*refs):
        ins, outs = refs[:4 * n], refs[4 * n:]
        for k in range(n):
            w_ref, g_ref, m_ref, v_ref = ins[4 * k:4 * k + 4]
            d_ref, nm_ref, nv_ref = outs[3 * k:3 * k + 3]
            g_ = g_ref[...]
            m_ = ADAM_B1 * m_ref[...] + (1.0 - ADAM_B1) * g_
            v_ = ADAM_B2 * v_ref[...] + (1.0 - ADAM_B2) * (g_ * g_)
            nm_ref[...] = m_
            nv_ref[...] = v_
            m_hat = m_ / (1.0 - ADAM_B1 ** ADAM_STEP)
            v_hat = v_ / (1.0 - ADAM_B2 ** ADAM_STEP)
            d_ref[...] = -ADAM_LR * (m_hat / (jnp.sqrt(v_hat) + ADAM_EPS) + ADAM_WD * w_ref[...])

    out = pl.pallas_call(
        body, out_shape=tuple(jax.ShapeDtypeStruct(it[0].shape, F32) for it in items for _ in range(3)),
        in_specs=[_VMEM] * (4 * n), out_specs=[_VMEM] * (3 * n), name=name)(*[a for it in items for a in it])
    return [tuple(out[3 * k:3 * k + 3]) for k in range(n)]


HALF_STEPS = 4


def _adamw_halves(sets, c_idx, name, deps=()):
    nh = HALF_STEPS
    n = len(sets)

    def body(c_ref, *refs):
        refs = refs[len(deps):]
        ins, outs = refs[:5 * n], refs[5 * n:]
        for k in range(n):
            w_ref, mine_ref, other_ref, m_ref, v_ref = ins[5 * k:5 * k + 5]
            g_ref, d_ref, nm_ref, nv_ref = outs[4 * k:4 * k + 4]
            g_ = jnp.where(pl.program_id(0) // nh == c_ref[0], mine_ref[...], other_ref[...])
            g_ref[...] = g_
            m_ = ADAM_B1 * m_ref[...] + (1.0 - ADAM_B1) * g_
            v_ = ADAM_B2 * v_ref[...] + (1.0 - ADAM_B2) * (g_ * g_)
            nm_ref[...] = m_
            nv_ref[...] = v_
            m_hat = m_ / (1.0 - ADAM_B1 ** ADAM_STEP)
            v_hat = v_ / (1.0 - ADAM_B2 ** ADAM_STEP)
            d_ref[...] = -ADAM_LR * (m_hat / (jnp.sqrt(v_hat) + ADAM_EPS) + ADAM_WD * w_ref[...])

    in_specs, out_specs, out_shape = [], [], []
    for w, mine, _, _, _ in sets:
        r2, cols = mine.shape
        block = (r2 // nh, cols)
        full = pl.BlockSpec(block, lambda i, c: (i, 0))
        in_specs += [full, pl.BlockSpec(block, lambda i, c: (jnp.clip(i - c[0] * nh, 0, nh - 1), 0)),
                     pl.BlockSpec(block, lambda i, c: (jnp.clip(i - (1 - c[0]) * nh, 0, nh - 1), 0)), full, full]
        out_specs += [full] * 4
        out_shape += [jax.ShapeDtypeStruct((2 * r2, cols), F32)] * 4
    out = pl.pallas_call(
        body,
        grid_spec=pltpu.PrefetchScalarGridSpec(num_scalar_prefetch=1, grid=(2 * nh,),
                                               in_specs=[_ANY] * len(deps) + in_specs, out_specs=out_specs),
        out_shape=tuple(out_shape), compiler_params=_cparams(), name=name,
    )(c_idx, *deps, *[a for s in sets for a in s])
    return [tuple(out[4 * k:4 * k + 4]) for k in range(n)]


def _adamw_ada(w, c_act, dmod, m, v):
    rows, n = w.shape
    tr = 128

    def body(c_ref, d_ref, w_ref, m_ref, v_ref, g_ref, dl_ref, nm_ref, nv_ref):
        g_ = lax.dot_general(c_ref[...], d_ref[...], (((0,), (0,)), ((), ())), preferred_element_type=F32,
                             precision=lax.Precision.HIGHEST)
        g_ref[...] = g_
        m_ = ADAM_B1 * m_ref[...] + (1.0 - ADAM_B1) * g_
        v_ = ADAM_B2 * v_ref[...] + (1.0 - ADAM_B2) * (g_ * g_)
        nm_ref[...] = m_
        nv_ref[...] = v_
        m_hat = m_ / (1.0 - ADAM_B1 ** ADAM_STEP)
        v_hat = v_ / (1.0 - ADAM_B2 ** ADAM_STEP)
        dl_ref[...] = -ADAM_LR * (m_hat / (jnp.sqrt(v_hat) + ADAM_EPS) + ADAM_WD * w_ref[...])

    spec = pl.BlockSpec((tr, n), lambda i: (i, 0))
    return pl.pallas_call(
        body, grid=(rows // tr,), out_shape=(jax.ShapeDtypeStruct((rows, n), F32),) * 4,
        in_specs=[pl.BlockSpec((N_DEV, tr), lambda i: (0, i)), _VMEM, spec, spec, spec], out_specs=[spec] * 4,
        compiler_params=_cparams(), name="adamw_w_ada")(c_act, dmod, w, m, v)


def _add_halves(grads, recvs, c_idx, name):
    nw = len(grads)

    def body(c_ref, *refs):
        for g_ref, r_ref, o_ref in zip(refs[:nw], refs[nw:2 * nw], refs[2 * nw:]):
            o_ref[...] = (g_ref[...].astype(F32) + r_ref[...].astype(F32)).astype(BF16)

    mine = [pl.BlockSpec((None, None) + g.shape[2:], lambda k, c: (k, c[0], 0, 0)) for g in grads]
    whole = [pl.BlockSpec((None,) + g.shape[2:], lambda k, c: (k, 0, 0)) for g in grads]
    return pl.pallas_call(
        body,
        grid_spec=pltpu.PrefetchScalarGridSpec(num_scalar_prefetch=1, grid=(N_CHIPS,), in_specs=mine + whole, out_specs=whole),
        out_shape=tuple(jax.ShapeDtypeStruct((N_CHIPS,) + g.shape[2:], BF16) for g in grads),
        compiler_params=_cparams(), name=name)(c_idx, *grads, *recvs)


def _sum_chips(owns, others, chip_idx, name):
    nw = len(owns)
    steps = 2

    def body(p_ref, *refs):
        for own_ref, got_ref, o_ref in zip(refs[:nw], refs[nw:2 * nw], refs[2 * nw:]):
            o_ref[...] = (((own_ref[...].astype(F32) + got_ref[0].astype(F32)) + got_ref[1].astype(F32))
                          + got_ref[2].astype(F32))

    blocks = [(a.shape[1] // steps, a.shape[2]) for a in owns]
    return pl.pallas_call(
        body,
        grid_spec=pltpu.PrefetchScalarGridSpec(
            num_scalar_prefetch=1, grid=(steps,),
            in_specs=([pl.BlockSpec((None,) + b, lambda i, p: (p[0], i, 0)) for b in blocks]
                      + [pl.BlockSpec((N_CHIPS - 1,) + b, lambda i, p: (0, i, 0)) for b in blocks]),
            out_specs=[pl.BlockSpec(b, lambda i, p: (i, 0)) for b in blocks]),
        out_shape=tuple(jax.ShapeDtypeStruct(a.shape[1:], F32) for a in owns), compiler_params=_cparams(),
        name=name)(chip_idx, *owns, *others)


def _place():
    x, y, c = lax.axis_index("x"), lax.axis_index("y"), lax.axis_index("c")
    return x, y, c, 2 * x + y


def _flip(v, bit):
    return 1 - v if bit else v


def _allgather8(v, name, deps=()):
    r, n = v.shape

    def body(*refs):
        v_ref, out_ref, send_sems, recv_sems, local_sem = refs[len(deps):]
        x, y, c, _ = _place()
        me = 4 * x + 2 * y + c
        mine = pltpu.make_async_copy(v_ref, out_ref.at[me], local_sem)
        mine.start()
        sends = []
        for rel in range(1, N_DEV):
            peer = (_flip(x, rel & 4), _flip(y, rel & 2), _flip(c, rel & 1))
            cp = pltpu.make_async_remote_copy(v_ref, out_ref.at[me], send_sems.at[rel - 1], recv_sems.at[rel - 1],
                                              device_id=peer, device_id_type=MESH)
            cp.start()
            sends.append(cp)
        for rel in range(1, N_DEV):
            peer = (_flip(x, rel & 4), _flip(y, rel & 2), _flip(c, rel & 1))
            peer_idx = 4 * peer[0] + 2 * peer[1] + peer[2]
            pltpu.make_async_remote_copy(v_ref, out_ref.at[peer_idx], send_sems.at[rel - 1], recv_sems.at[rel - 1],
                                         device_id=peer, device_id_type=MESH).wait_recv()
        for cp in sends:
            cp.wait_send()
        mine.wait()

    return pl.pallas_call(
        body, out_shape=jax.ShapeDtypeStruct((N_DEV, r, n), F32), in_specs=[_ANY] * len(deps) + [_VMEM], out_specs=_VMEM,
        scratch_shapes=[pltpu.SemaphoreType.DMA((N_DEV - 1,)), pltpu.SemaphoreType.DMA((N_DEV - 1,)), pltpu.SemaphoreType.DMA(())],
        name=name)(*deps, v)


_HBM = pl.BlockSpec(memory_space=pltpu.HBM)
_SEM = pl.BlockSpec(memory_space=pltpu.SEMAPHORE)
_EFFECT = pltpu.SideEffectType.DATAFLOW_SIDE_EFFECTING


def _xchg_start(name, plan, n_copies, srcs, lands, after=(), sibling_id=None):
    bufs = list(srcs) + list(lands)
    ns, nb = len(srcs), len(srcs) + len(lands)

    def body(*refs):
        send_sems, recv_sems, token = refs[nb + len(after)], refs[nb + len(after) + 1], refs[-1]
        if sibling_id is not None:
            x, y, c, _ = _place()
            barrier = pltpu.get_barrier_semaphore()
            pl.semaphore_signal(barrier, inc=1, device_id=(x, y, 1 - c), device_id_type=MESH)
            pl.semaphore_wait(barrier, 1)
        for i, (src, dst, peer, _) in enumerate(plan(_place(), refs[:ns], refs[ns:nb])):
            pltpu.make_async_remote_copy(src, dst, send_sems.at[i], recv_sems.at[i], device_id=peer, device_id_type=MESH).start()
        token[...] = jnp.zeros_like(token)

    out = pl.pallas_call(
        body, name=name,
        out_shape=(pltpu.SemaphoreType.DMA((n_copies,)), pltpu.SemaphoreType.DMA((n_copies,)),
                   *[pltpu.HBM(a.shape, a.dtype) for a in bufs], jax.ShapeDtypeStruct((8, 128), F32)),
        in_specs=[_HBM] * nb + [_ANY] * len(after), out_specs=(_SEM, _SEM, *[_HBM] * nb, _VMEM),
        input_output_aliases={i: 2 + i for i in range(nb)},
        compiler_params=pltpu.CompilerParams(has_side_effects=_EFFECT, collective_id=sibling_id),
    )(*[pltpu.with_memory_space_constraint(a, pltpu.HBM) for a in bufs], *after)
    return (out[0], out[1]), out[2:2 + ns], out[2 + ns:2 + nb], out[-1]


def _xchg_wait(name, plan, sems, srcs, lands, after, sem_ids=None):
    bufs = list(srcs) + list(lands)
    ns, nb = len(srcs), len(srcs) + len(lands)

    def body(*refs):
        send_sems, recv_sems = refs[nb], refs[nb + 1]
        copies = plan(_place(), refs[:ns], refs[ns:nb])
        ids = range(len(copies)) if sem_ids is None else sem_ids
        for i, (src, _, peer, mine) in zip(ids, copies, strict=True):
            if i is not None:
                cp = pltpu.make_async_remote_copy(src, mine, send_sems.at[i], recv_sems.at[i], device_id=peer,
                                                  device_id_type=MESH)
                cp.wait_send()
                cp.wait_recv()

    out = pl.pallas_call(
        body, name=name, out_shape=tuple(pltpu.HBM(a.shape, a.dtype) for a in bufs),
        in_specs=[_HBM] * nb + [_SEM, _SEM] + [_ANY] * len(after), out_specs=tuple([_HBM] * nb),
        input_output_aliases={i: i for i in range(nb)},
        compiler_params=pltpu.CompilerParams(has_side_effects=_EFFECT),
    )(*bufs, *sems, *after)
    return out[:ns], out[ns:]


def _other_chips(place, which=(1, 2, 3)):
    x, y, c, _ = place
    return [((_flip(x, j & 2), _flip(y, j & 1), c), 2 * _flip(x, j & 2) + _flip(y, j & 1)) for j in which]


def _plan_gather_ici(chips):
    def plan(place, src_refs, land_refs):
        _, _, c, p = place
        return [(s.at[c], l.at[p, c], peer, l.at[q, c]) for s, l, which in zip(src_refs, land_refs, chips, strict=True)
                for peer, q in _other_chips(place, which)]
    return plan


def _plan_relay(which):
    def plan(place, src_refs, land_refs):
        x, y, c, _ = place
        return [(l.at[q, c], l.at[q, c], (x, y, 1 - c), l.at[q, 1 - c]) for l in land_refs for _, q in _other_chips(place, which)]
    return plan


def _plan_swap(place, src_refs, land_refs):
    x, y, c, _ = place
    return [(s.at[k, 1 - c], l.at[k], (x, y, 1 - c), l.at[k]) for s, l in zip(src_refs, land_refs) for k in range(N_CHIPS)]


def _plan_scatter(place, src_refs, land_refs):
    return [(s.at[q], l.at[j], peer, l.at[j]) for s, l in zip(src_refs, land_refs)
            for j, (peer, q) in enumerate(_other_chips(place))]


def _plan_share(place, src_refs, land_refs):
    x, y, c, _ = place
    return [(s, l, (x, y, 1 - c), l) for s, l in zip(src_refs, land_refs)]


def _pack_rows(parts, n_rows, name, deps=()):
    def body(*refs):
        refs = refs[len(deps):]
        out_ref = refs[-1]
        out_ref[...] = jnp.zeros((n_rows, D), F32)
        at = 0
        for ref in refs[:-1]:
            k = ref.shape[0]
            out_ref[at:at + k, :] = ref[...]
            at += k

    return pl.pallas_call(
        body, out_shape=jax.ShapeDtypeStruct((n_rows, D), F32), in_specs=[_ANY] * len(deps) + [_VMEM] * len(parts),
        out_specs=_VMEM, name=name)(*deps, *parts)


TS_MM = 512
TS_IN = 1024
TS_GW = 1024
TS_MIX = 256


def _halved(a):
    n, r, cols = a.shape
    return a.reshape(n, 2, r // 2, cols)


SIBLING_IDS = (1, 2)


def _rs_swap(name, grads, after=()):
    lands = [lax.empty((N_CHIPS,) + g.shape[2:], g.dtype) for g in grads]
    sems, grads, lands, token = _xchg_start(name + "_swap", _plan_swap, N_CHIPS * len(grads), grads, lands, after,
                                            sibling_id=SIBLING_IDS[0])
    return name, sems, grads, lands, token


def _rs_scatter(handle, after, chip, ci):
    name, sems, grads, lands, _ = handle
    grads, from_sibling = _xchg_wait(name + "_swap_wait", _plan_swap, sems, grads, lands, after)
    c_arr = jnp.reshape(ci, (1,)).astype(jnp.int32)
    pair_sums = _add_halves(list(grads), list(from_sibling), c_arr, name + "_add_halves")
    lands = [lax.empty((N_CHIPS - 1,) + p.shape[1:], p.dtype) for p in pair_sums]
    sems, pair_sums, lands, token = _xchg_start(name + "_scatter", _plan_scatter, 3 * len(pair_sums), pair_sums, lands)
    return name, sems, pair_sums, lands, jnp.reshape(chip, (1,)).astype(jnp.int32), token


def _rs_share(handle, after):
    name, sems, pair_sums, lands, chip_idx, _ = handle
    pair_sums, received = _xchg_wait(name + "_scatter_wait", _plan_scatter, sems, pair_sums, lands, after)
    halves = _sum_chips(list(pair_sums), list(received), chip_idx, name + "_sum_chips")
    lands = [lax.empty(h.shape, h.dtype) for h in halves]
    sems, halves, lands, token = _xchg_start(name + "_share", _plan_share, len(halves), halves, lands,
                                             sibling_id=SIBLING_IDS[1])
    return name, sems, halves, lands, token


def _rs_end(handle, after):
    name, sems, halves, lands, _ = handle
    halves, others = _xchg_wait(name + "_share_wait", _plan_share, sems, halves, lands, after)
    return list(zip(halves, others))


def kernel(x, c, w_ada, b_ada, g_norm_mix, w_in, conv_a_w, conv_b_w, conv_b_bias, w_rg_a, b_rg_a, w_rg_x, b_rg_x, lru_lambda, w_out, g_norm_ffn, w_gate_up, w_down, g_norm_final, loss_target, m_w_ada, m_b_ada, m_g_norm_mix, m_w_in, m_conv_a_w, m_conv_b_w, m_conv_b_bias, m_w_rg_a, m_b_rg_a, m_w_rg_x, m_b_rg_x, m_lru_lambda, m_w_out, m_g_norm_ffn, m_w_gate_up, m_w_down, m_g_norm_final, v_w_ada, v_b_ada, v_g_norm_mix, v_w_in, v_conv_a_w, v_conv_b_w, v_conv_b_bias, v_w_rg_a, v_b_rg_a, v_w_rg_x, v_b_rg_x, v_lru_lambda, v_w_out, v_g_norm_ffn, v_w_gate_up, v_w_down, v_g_norm_final):
    xi, yi, ci = lax.axis_index("x"), lax.axis_index("y"), lax.axis_index("c")
    chip = 2 * xi + yi
    me = 2 * chip + ci
    n_ada = w_ada.shape[2]

    def widen(w):
        return jnp.pad(w, ((0, 0), (0, D - w.shape[1])))

    got = _allgather8(_pack_rows([c, widen(conv_a_w[0]), widen(conv_b_w[0])], 8, "pack_c_conv"), "gather_c_conv")
    c_all = got[:, 0, :]
    conv_full = got[::2, 1:8, :D // N_CHIPS].transpose(1, 0, 2).reshape(7, D)

    mod_part, c_act = _ada_fwd(c_all, w_ada[0], lax.dynamic_slice_in_dim(b_ada, chip * n_ada, n_ada, axis=1))
    mod_all = _allgather8(mod_part, "gather_mod")
    mod_mine = lax.dynamic_index_in_dim(mod_all, me, axis=1, keepdims=False)[::2].reshape(6, D)
    vecs = _pack_rows([mod_mine, g_norm_mix, g_norm_ffn, g_norm_final.reshape(1, D), conv_b_bias, b_rg_a, b_rg_x, lru_lambda,
                       conv_full], N_VEC, "pack_vecs")

    def rg_shard(w):
        return w[0].astype(BF16).reshape(2, HEADS * HB // N_CHIPS // 2, HB)

    shards = [w_in[0].astype(BF16).reshape(2, D // 2, C_IN), rg_shard(w_rg_a), rg_shard(w_rg_x),
              w_out[0].astype(BF16).reshape(2, D // N_CHIPS // 2, D), w_gate_up[0].astype(BF16).reshape(2, D // 2, C_GU),
              w_down[0].astype(BF16).reshape(2, D_FF // N_CHIPS // 2, D)]
    lands = [lax.dynamic_update_index_in_dim(lax.empty((N_CHIPS,) + s.shape, s.dtype), s, chip, 0) for s in shards]

    def send(name, first, last, after, chips):
        copies = [(k, j) for k, which in zip(range(first, last), chips, strict=True) for j in which]
        sems, srcs, zone, token = _xchg_start(name + "_ici", _plan_gather_ici(chips), len(copies), shards[first:last],
                                              lands[first:last], after)
        shards[first:last], lands[first:last] = srcs, zone
        return sems, copies, token

    def arrive(name, sent, first, last, after):
        sems, copies, _ = sent
        chips = [tuple(j for k, j in copies if k == want) for want in range(first, last)]
        ids = [copies.index((k, j)) for k, which in zip(range(first, last), chips) for j in which]
        srcs, zone = _xchg_wait(name + "_ici_wait", _plan_gather_ici(chips), sems, shards[first:last], lands[first:last], after,
                                ids)
        shards[first:last], lands[first:last] = srcs, zone

    def relay(name, first, last, which, sibling_id):
        plan = _plan_relay(which)
        sems, _, zone, token = _xchg_start(name + "_d2d", plan, len(which) * (last - first), [], lands[first:last],
                                           sibling_id=sibling_id)
        lands[first:last] = zone
        return name, plan, sems, first, last, token

    def relayed(handle, after):
        name, plan, sems, first, last, _ = handle
        lands[first:last] = _xchg_wait(name + "_d2d_wait", plan, sems, [], lands[first:last], after)[1]

    def to_blocks(v):
        return v.reshape(-1, TS_MIX // TIME_BLOCKS, TIME_BLOCKS, D).transpose(0, 2, 1, 3).reshape(v.shape)

    def from_blocks(v):
        return v.reshape(-1, TIME_BLOCKS, TS_MIX // TIME_BLOCKS, D).transpose(0, 2, 1, 3).reshape(v.shape)

    def chip_index(j):
        return jnp.reshape(chip ^ j, (1,)).astype(jnp.int32)

    def wg_in():
        return lands[0].reshape(N_CHIPS, D, C_IN)

    xs, target = to_blocks(x[0]), to_blocks(loss_target[0])
    sent_near = send("gather_in_near", 0, 1, [vecs], [(1, 2)])
    ts_in = min(TS_IN, xs.shape[0])
    h1, proj = _fwd_in_first(xs, vecs, wg_in(), chip_index(0), ts_in, deps=[sent_near[-1]])
    arrive("gather_in_near", sent_near, 0, 1, [proj])
    near = relay("gather_in_near", 0, 1, (1, 2), SIBLING_IDS[0])
    sent_rest = send("gather_rest", 0, 6, [near[-1]], [(3,)] + [(1, 2, 3)] * 5)
    relayed(near, [sent_rest[-1]])
    proj = _fwd_in_more(h1, wg_in(), proj, chip_index(1), ts_in, "fwd_in_y")
    proj = _fwd_in_more(h1, wg_in(), proj, chip_index(2), ts_in, "fwd_in_x")
    arrive("gather_in_far", sent_rest, 0, 1, [proj])
    far = relay("gather_in_far", 0, 1, (3,), SIBLING_IDS[1])
    arrive("gather_mix", sent_rest, 1, 4, [far[-1]])
    relayed(far, [far[-1]])
    mix = relay("gather_mix", 1, 4, (1, 2, 3), SIBLING_IDS[0])
    proj = _fwd_in_more(h1, wg_in(), proj, chip_index(3), ts_in, "fwd_in_xy", deps=[mix[-1]])
    relayed(mix, [proj])
    wg_rga, wg_rgx, wg_out = lands[1:4]
    wg_out = wg_out.reshape(D, D)

    def rg_full(wg):
        return wg.reshape(N_CHIPS, HEADS, HB // N_CHIPS, HB).transpose(1, 0, 2, 3).reshape(HEADS, HB, HB)

    wg_rga, wg_rgx = rg_full(wg_rga), rg_full(wg_rgx)

    x1, merged, z1, kept, decay = _fwd_mix(proj, xs, vecs, wg_rga, wg_rgx, wg_out, TS_MIX)
    arrive("gather_ffn", sent_rest, 4, 6, [x1])
    ffn = relay("gather_ffn", 4, 6, (1, 2, 3), SIBLING_IDS[1])
    relayed(ffn, [ffn[-1]])
    wg_gu, wg_dn = lands[4:6]
    wg_gu, wg_dn = wg_gu.reshape(N_CHIPS, D, C_GU), wg_dn.reshape(D_FF, D)
    dx1, h2, act, dz2, dgu, sm_ffn = _ffn_loss(x1, target, vecs, wg_gu, wg_dn, TS_MIX)

    def rg_chunks(dw):
        return _halved(dw.reshape(HEADS, N_CHIPS, HB // N_CHIPS, HB).transpose(1, 0, 2, 3).reshape(N_CHIPS, HB, HB).astype(BF16))

    ts_gw = min(TS_GW, xs.shape[0])
    g_dn = _grad_w(act, dz2, 1, ts_gw, "grad_w_down")
    g_gu = _grad_w(h2, dgu, N_CHIPS, ts_gw, "grad_w_gate_up")
    rs_b = _rs_swap("rs_b", [_halved(g_gu), _halved(g_dn.reshape(N_CHIPS, D_FF // N_CHIPS, D))])
    dproj, sm_mix, dw_rga, dw_rgx, dw_out = _bwd_mix(dx1, z1, merged, proj, kept, decay, vecs, wg_rga, wg_rgx, wg_out, sm_ffn, TS_MIX,
                                                     deps=[rs_b[-1]])
    rs_b = _rs_scatter(rs_b, [dproj], chip, ci)
    g_in = _grad_w(h1, dproj, N_CHIPS, ts_gw, "grad_w_in", deps=[rs_b[-1]])
    rs_b = _rs_share(rs_b, [g_in])
    rs_a = _rs_swap("rs_a", [_halved(g_in), rg_chunks(dw_rga), rg_chunks(dw_rgx),
                             _halved(dw_out.astype(BF16).reshape(N_CHIPS, D // N_CHIPS, D))], after=[rs_b[-1]])

    c_arr = jnp.reshape(ci, (1,)).astype(jnp.int32)

    def step_halves(name, items, deps=()):
        two_d = lambda a: a.reshape(-1, a.shape[-1])
        sets = [(two_d(w), halves[0], halves[1], two_d(m), two_d(v)) for w, halves, m, v in items.values()]
        for (n, (w, _, _, _)), out in zip(items.items(), _adamw_halves(sets, c_arr, name, deps)):
            res[n] = tuple(a.reshape(w.shape) for a in out)

    def shard_cols(row_block):
        return lax.dynamic_slice_in_dim(row_block, chip * (D // N_CHIPS), D // N_CHIPS, axis=1)

    gw_gu, gw_dn = _rs_end(rs_b, [rs_a[-1]])
    res = {}
    step_halves("adamw_ffn", {"w_gate_up": (w_gate_up, gw_gu, m_w_gate_up, v_w_gate_up),
                              "w_down": (w_down, gw_dn, m_w_down, v_w_down)}, [rs_a[-1]])
    rs_a = _rs_scatter(rs_a, [res["w_gate_up"][1], res["w_down"][1]], chip, ci)
    grad_x, sm_in = _bwd_in(dproj, xs, dx1, vecs, wg_in(), sm_mix, TS_MM, deps=[rs_a[-1]])
    rs_a = _rs_share(rs_a, [grad_x])

    small, per_dev = _sum_small(_allgather8(sm_in, "gather_small", deps=[rs_a[-1]]))
    dmod_all = per_dev[:, 0:6, :].reshape(N_DEV, 6 * D)
    grad_b_ada = small[0:6].reshape(1, 6 * D)
    res["w_ada"] = tuple(a[None] for a in _adamw_ada(w_ada[0], c_act, lax.dynamic_slice_in_dim(dmod_all, chip * n_ada, n_ada, axis=1),
                                                     m_w_ada[0], v_w_ada[0]))
    small_sets = {
        "b_ada": (b_ada.reshape(6, D), grad_b_ada.reshape(6, D), m_b_ada.reshape(6, D), v_b_ada.reshape(6, D)),
        "g_norm_mix": (g_norm_mix, small[G_GMIX:G_GMIX + 1], m_g_norm_mix, v_g_norm_mix),
        "conv_a_w": (conv_a_w[0], shard_cols(small[G_WA0:G_WA0 + 3]), m_conv_a_w[0], v_conv_a_w[0]),
        "conv_b_w": (conv_b_w[0], shard_cols(small[G_WB0:G_WB0 + 4]), m_conv_b_w[0], v_conv_b_w[0]),
        "conv_b_bias": (conv_b_bias, small[G_CBB:G_CBB + 1], m_conv_b_bias, v_conv_b_bias),
        "b_rg_a": (b_rg_a, small[G_BA:G_BA + 1], m_b_rg_a, v_b_rg_a),
        "b_rg_x": (b_rg_x, small[G_BX:G_BX + 1], m_b_rg_x, v_b_rg_x),
        "lru_lambda": (lru_lambda, small[G_LAM:G_LAM + 1], m_lru_lambda, v_lru_lambda),
        "g_norm_ffn": (g_norm_ffn, small[G_GFFN:G_GFFN + 1], m_g_norm_ffn, v_g_norm_ffn),
        "g_norm_final": (g_norm_final.reshape(1, D), small[G_GFIN:G_GFIN + 1], m_g_norm_final.reshape(1, D),
                         v_g_norm_final.reshape(1, D)),
    }
    stepped = _adamw_small(list(small_sets.values()), "adamw_small")
    for (n, (w_, g_, _, _)), (d_, nm_, nv_) in zip(small_sets.items(), stepped):
        shape = (1,) + w_.shape if n.startswith("conv_") and n != "conv_b_bias" else w_.shape
        res[n] = tuple(a.reshape(shape) for a in (g_, d_, nm_, nv_))
    gw_in, gw_rga, gw_rgx, gw_out = _rs_end(rs_a, [res[n][1] for n in res])
    step_halves("adamw_mix", {"w_in": (w_in, gw_in, m_w_in, v_w_in), "w_rg_a": (w_rg_a, gw_rga, m_w_rg_a, v_w_rg_a),
                              "w_rg_x": (w_rg_x, gw_rgx, m_w_rg_x, v_w_rg_x), "w_out": (w_out, gw_out, m_w_out, v_w_out)})
    res["b_ada"] = tuple(a.reshape(1, 6 * D) for a in res["b_ada"])
    res["g_norm_final"] = tuple(a.reshape(D) for a in res["g_norm_final"])
    names = ["w_ada", "b_ada", "g_norm_mix", "w_in", "conv_a_w", "conv_b_w", "conv_b_bias", "w_rg_a", "b_rg_a", "w_rg_x",
             "b_rg_x", "lru_lambda", "w_out", "g_norm_ffn", "w_gate_up", "w_down", "g_norm_final"]
    loss = jnp.sum(small[G_LOSS])
    return (loss, from_blocks(grad_x)[None], *[res[n][0] for n in names], *[res[n][1] for n in names],
            *[res[n][2] for n in names], *[res[n][3] for n in names])
```

```python
import functools

import jax
import jax.numpy as jnp
from jax import lax
from jax.experimental import pallas as pl
from jax.experimental.pallas import tpu as pltpu

F32 = jnp.float32
BF16 = jnp.bfloat16
MESH = pl.DeviceIdType.MESH

D = 1024
N_CHIPS = 4
N_DEV = 8
D_IN = 7 * D
C_IN = D_IN // N_CHIPS
D_FF = 2816
C_GU = 2 * D_FF // N_CHIPS
HEADS = 4
HB = D // HEADS
EPS = 1e-6
LRU_C = 8.0
ADAM_LR, ADAM_B1, ADAM_B2, ADAM_EPS, ADAM_WD, ADAM_STEP = 0.001, 0.9, 0.999, 1e-08, 0.01, 10
VMEM_LIMIT = 56 << 20

(V_SH1, V_SC1, V_GT1, V_SH2, V_SC2, V_GT2, V_GMIX, V_GFFN, V_GFIN, V_CBB, V_BA, V_BX, V_LAM,
 V_WA0, V_WA1, V_WA2, V_WB0, V_WB1, V_WB2, V_WB3) = range(20)
N_VEC = 24
(G_SH1, G_SC1, G_GT1, G_SH2, G_SC2, G_GT2, G_GMIX, G_CBB, G_BA, G_BX, G_LAM, G_GFFN, G_GFIN,
 G_WA0, G_WA1, G_WA2, G_WB0, G_WB1, G_WB2, G_WB3, G_LOSS) = range(21)
N_SMALL = 24

_VMEM = pl.BlockSpec(memory_space=pltpu.VMEM)
_ANY = pl.BlockSpec(memory_space=pl.ANY)


def _cparams(n_grid=1):
    return pltpu.CompilerParams(dimension_semantics=("arbitrary",) * n_grid, vmem_limit_bytes=VMEM_LIMIT)


def _after(deps, body):
    n = len(deps)
    return lambda *refs: body(*refs[n:])


def _rms(x):
    rstd = lax.rsqrt(jnp.mean(x * x, axis=-1, keepdims=True) + EPS)
    return x * rstd, rstd


def _rms_bwd(dxhat, xhat, rstd):
    return rstd * (dxhat - xhat * jnp.mean(dxhat * xhat, axis=-1, keepdims=True))


def _rowsum(v):
    return jnp.sum(v, axis=0, keepdims=True)


def _dot(a, b):
    return jnp.dot(a, b, preferred_element_type=F32)


def _dot_nt(a, b):
    return lax.dot_general(a, b, (((1,), (1,)), ((), ())), preferred_element_type=F32)


def _dot_tn(a, b):
    return lax.dot_general(a, b, (((0,), (0,)), ((), ())), preferred_element_type=F32)


def _gelu(x):
    k, c = 0.7978845608028654, 0.044715
    t = jnp.tanh(k * (x + c * x * x * x))
    return 0.5 * x * (1.0 + t), 0.5 * (1.0 + t) + 0.5 * x * (1.0 - t * t) * k * (1.0 + 3.0 * c * x * x)


def _log_sigmoid(lam):
    return jnp.minimum(lam, 0.0) - jnp.log1p(jnp.exp(-jnp.abs(lam)))


def _lru_gates(u, wa_ref, wx_ref, v_ref, row0):
    ub = u.astype(BF16)
    pre_a = jnp.concatenate([_dot(ub[:, h * HB:(h + 1) * HB], wa_ref[h]) for h in range(HEADS)], axis=1)
    pre_x = jnp.concatenate([_dot(ub[:, h * HB:(h + 1) * HB], wx_ref[h]) for h in range(HEADS)], axis=1)
    r = jax.nn.sigmoid(pre_a + v_ref[V_BA:V_BA + 1, :])
    ig = jax.nn.sigmoid(pre_x + v_ref[V_BX:V_BX + 1, :])
    log_a = LRU_C * r * _log_sigmoid(v_ref[V_LAM:V_LAM + 1, :])
    a = jnp.exp(log_a)
    x2 = 2.0 * log_a
    m2 = jnp.where(x2 > -0.03, -x2 * (1.0 + x2 * (0.5 + x2 * (1.0 / 6.0 + x2 * (1.0 / 24.0)))), 1.0 - a * a)
    mult = jnp.where(row0, 1.0, jnp.sqrt(jnp.maximum(m2, 0.0)))
    return r, ig, a, mult


TIME_BLOCKS = 8
N_KEPT = 10


def _late_blocks(v, buf, g, halo=None):
    n = buf.shape[0]
    out = []
    for idx in range(n):
        k = TIME_BLOCKS - n + idx
        buf[idx, 8:g + 8, :] = v[k * g:(k + 1) * g]
        if halo is not None:
            buf[idx, 7:8, :] = halo[idx]
        out.append(buf[idx, pl.ds(7, g), :])
        if halo is None:
            buf[idx, 7:8, :] = buf[idx, g + 7:g + 8, :]
    return out


def _earlier(v, s, late, g):
    return jnp.concatenate(late[len(late) - s:] + [v[0:(TIME_BLOCKS - s) * g]], axis=0)


def _early_blocks(v, buf, g):
    out = []
    for k in range(buf.shape[0]):
        buf[k, 0:g, :] = v[k * g:(k + 1) * g]
        out.append(buf[k, pl.ds(1, g), :])
        buf[k, g:g + 1, :] = buf[k, 0:1, :]
    return out


def _later(v, s, early, g):
    return jnp.concatenate([v[s * g:]] + early[0:s], axis=0)


def _fwd_in_first(x, vecs, w_in_g, q_idx, ts, deps=()):
    s = x.shape[0]

    def body(q_ref, x_ref, v_ref, w_ref, h1_ref, proj_ref):
        xhat, _ = _rms(x_ref[...])
        h = xhat * v_ref[V_GMIX:V_GMIX + 1, :] * (1.0 + v_ref[V_SC1:V_SC1 + 1, :]) + v_ref[V_SH1:V_SH1 + 1, :]
        hb = h.astype(BF16)
        h1_ref[...] = hb
        proj_ref[...] = _dot(hb, w_ref[...]).astype(BF16)

    return pl.pallas_call(
        lambda q_ref, *refs: body(q_ref, *refs[len(deps):]),
        grid_spec=pltpu.PrefetchScalarGridSpec(
            num_scalar_prefetch=1, grid=(s // ts,),
            in_specs=[_ANY] * len(deps) + [pl.BlockSpec((ts, D), lambda i, q: (i, 0)), _VMEM,
                                           pl.BlockSpec((None, D, C_IN), lambda i, q: (q[0], 0, 0))],
            out_specs=[pl.BlockSpec((ts, D), lambda i, q: (i, 0)), pl.BlockSpec((ts, C_IN), lambda i, q: (i, q[0]))]),
        out_shape=(jax.ShapeDtypeStruct((s, D), BF16), jax.ShapeDtypeStruct((s, D_IN), BF16)),
        compiler_params=_cparams(), name="fwd_in_own")(q_idx, *deps, x, vecs, w_in_g)


def _fwd_in_more(h1, w_in_g, proj, q_idx, ts, name, deps=()):
    s = h1.shape[0]

    def body(q_ref, h1_ref, w_ref, proj_in_ref, proj_ref):
        proj_ref[...] = _dot(h1_ref[...], w_ref[...]).astype(BF16)

    return pl.pallas_call(
        lambda q_ref, *refs: body(q_ref, *refs[len(deps):]),
        grid_spec=pltpu.PrefetchScalarGridSpec(
            num_scalar_prefetch=1, grid=(s // ts,),
            in_specs=[_ANY] * len(deps) + [pl.BlockSpec((ts, D), lambda i, q: (i, 0)),
                                           pl.BlockSpec((None, D, C_IN), lambda i, q: (q[0], 0, 0)), _ANY],
            out_specs=pl.BlockSpec((ts, C_IN), lambda i, q: (i, q[0]))),
        out_shape=jax.ShapeDtypeStruct((s, D_IN), BF16), input_output_aliases={len(deps) + 3: 0},
        compiler_params=_cparams(), name=name)(q_idx, *deps, h1, w_in_g, proj)


def _fwd_mix(proj, x, vecs, w_rga, w_rgx, w_out, ts, deps=()):
    s = x.shape[0]
    g = ts // TIME_BLOCKS

    def body(proj_ref, x_ref, v_ref, wa_ref, wx_ref, wo_ref, x1_ref, mg_ref, z1_ref, kept_ref, decay_ref,
             ua_buf, rx_buf, p_buf, q_buf, c_buf, hcarry):
        i = pl.program_id(0)

        @pl.when(i == 0)
        def _():
            ua_buf[...] = jnp.zeros(ua_buf.shape, F32)
            rx_buf[...] = jnp.zeros(rx_buf.shape, F32)
            hcarry[...] = jnp.zeros((8, D), F32)

        def seg(j):
            return proj_ref[:, j * D:(j + 1) * D].astype(F32)

        def vrow(j):
            return v_ref[j:j + 1, :]

        cb, cc, cx, rx, rg, ga, gb = (seg(j) for j in range(7))
        ua = cc * cx
        ua_late = _late_blocks(ua, ua_buf, g)
        rx_late = _late_blocks(rx, rx_buf, g)
        va = vrow(V_WA2) * ua + vrow(V_WA1) * _earlier(ua, 1, ua_late, g) + vrow(V_WA0) * _earlier(ua, 2, ua_late, g)
        u = (vrow(V_WB3) * rx + vrow(V_WB2) * _earlier(rx, 1, rx_late, g) + vrow(V_WB1) * _earlier(rx, 2, rx_late, g)
             + vrow(V_WB0) * _earlier(rx, 3, rx_late, g) + vrow(V_CBB))

        rows = lax.broadcasted_iota(jnp.int32, (ts, D), 0)
        row0 = jnp.logical_and(rows == 0, i == 0)
        r, ig, a, mult = _lru_gates(u, wa_ref, wx_ref, v_ref, row0)
        decay_ref[...] = a
        bx = mult * (ig * u)

        prods, sums = [a[0:g]], [bx[0:g]]
        for k in range(1, TIME_BLOCKS):
            ak = a[k * g:(k + 1) * g]
            sums.append(ak * sums[-1] + bx[k * g:(k + 1) * g])
            prods.append(ak * prods[-1])
        p_buf[...] = prods[-1]
        q_buf[...] = sums[-1]
        state = hcarry[0:1, :]
        for j in range(g):
            c_buf[j:j + 1, :] = state
            state = p_buf[j:j + 1, :] * state + q_buf[j:j + 1, :]
        hcarry[0:1, :] = state
        entering = c_buf[...]
        h = jnp.concatenate([sums[k] + prods[k] * entering for k in range(TIME_BLOCKS)], axis=0)

        gel, dgel = _gelu(rg)
        sga = jax.nn.sigmoid(ga)
        sgb = jax.nn.sigmoid(gb)
        for j, keep in enumerate((va, r, ig, sga, sgb, gel, dgel, mult, u, h)):
            kept_ref[:, j * D:(j + 1) * D] = keep.astype(BF16)
        merged = (sga * (cb * va) + sgb * (h * gel)).astype(BF16)
        mg_ref[...] = merged
        z1 = _dot(merged, wo_ref[...])
        z1_ref[...] = z1.astype(BF16)
        x1_ref[...] = x_ref[...] + vrow(V_GT1) * z1

    row = lambda i: (i, 0)
    return pl.pallas_call(
        _after(deps, body), grid=(s // ts,),
        out_shape=(jax.ShapeDtypeStruct((s, D), F32), jax.ShapeDtypeStruct((s, D), BF16), jax.ShapeDtypeStruct((s, D), BF16),
                   jax.ShapeDtypeStruct((s, N_KEPT * D), BF16), jax.ShapeDtypeStruct((s, D), F32)),
        in_specs=[_ANY] * len(deps) + [pl.BlockSpec((ts, D_IN), row), pl.BlockSpec((ts, D), row), _VMEM, _VMEM, _VMEM, _VMEM],
        out_specs=[pl.BlockSpec((ts, D), row)] * 3 + [pl.BlockSpec((ts, N_KEPT * D), row), pl.BlockSpec((ts, D), row)],
        scratch_shapes=[pltpu.VMEM((2, g + 8, D), F32), pltpu.VMEM((3, g + 8, D), F32), pltpu.VMEM((g, D), F32),
                        pltpu.VMEM((g, D), F32), pltpu.VMEM((g, D), F32), pltpu.VMEM((8, D), F32)],
        compiler_params=_cparams(), name="fwd_mix")(*deps, proj, x, vecs, w_rga, w_rgx, w_out)


def _ffn_loss(x1, target, vecs, w_gu_g, w_dn, ts):
    s = x1.shape[0]

    def body(x1_ref, t_ref, v_ref, wgu_ref, wdn_ref, dx1_ref, h2_ref, act_ref, dz2_ref, dgu_ref, sm_ref):
        @pl.when(pl.program_id(0) == 0)
        def _():
            sm_ref[...] = jnp.zeros((N_SMALL, D), F32)

        def vrow(j):
            return v_ref[j:j + 1, :]

        n_sub = 1
        rows = [slice(k * (ts // n_sub), (k + 1) * (ts // n_sub)) for k in range(n_sub)]
        subs = [dict(r=r, sums={}) for r in rows]

        def stage_norm(t):
            t["x1"] = x1_ref[t["r"], :]
            t["xh1"], t["rstd1"] = _rms(t["x1"])
            t["n2"] = t["xh1"] * vrow(V_GFFN)
            t["h2"] = (t["n2"] * (1.0 + vrow(V_SC2)) + vrow(V_SH2)).astype(BF16)
            h2_ref[t["r"], :] = t["h2"]

        def stage_up(t):
            h2 = t["h2"]
            g = jnp.concatenate([_dot(h2, wgu_ref[0]), _dot(h2, wgu_ref[1])], axis=1)
            t["up"] = jnp.concatenate([_dot(h2, wgu_ref[2]), _dot(h2, wgu_ref[3])], axis=1)
            t["g"] = g
            t["sg"] = jax.nn.sigmoid(g)
            t["silu"] = g * t["sg"]
            t["act"] = (t["silu"] * t["up"]).astype(BF16)
            act_ref[t["r"], :] = t["act"]

        def stage_down_loss(t):
            z2 = _dot(t["act"], wdn_ref[...])
            x2 = t["x1"] + vrow(V_GT2) * z2
            xh2, rstd2 = _rms(x2)
            err = xh2 * vrow(V_GFIN) - t_ref[t["r"], :]
            t["sums"][G_LOSS] = _rowsum((0.5 / D) * err * err)
            dy = err * (1.0 / D)
            t["sums"][G_GFIN] = _rowsum(dy * xh2)
            t["dx2"] = _rms_bwd(dy * vrow(V_GFIN), xh2, rstd2)
            t["sums"][G_GT2] = _rowsum(t["dx2"] * z2)
            t["dz2"] = (vrow(V_GT2) * t["dx2"]).astype(BF16)
            dz2_ref[t["r"], :] = t["dz2"]

        def stage_back_act(t):
            dact = _dot_nt(t["dz2"], wdn_ref[...])
            g, sg = t["g"], t["sg"]
            t["dgate"] = (dact * t["up"] * (sg * (1.0 + g * (1.0 - sg)))).astype(BF16)
            t["dup"] = (dact * t["silu"]).astype(BF16)
            dgu_ref[t["r"], 0:D_FF] = t["dgate"]
            dgu_ref[t["r"], D_FF:2 * D_FF] = t["dup"]

        def stage_back_norm(t):
            dgate, dup = t["dgate"], t["dup"]
            dh2 = (_dot_nt(dgate[:, 0:C_GU], wgu_ref[0]) + _dot_nt(dgate[:, C_GU:2 * C_GU], wgu_ref[1])
                   + _dot_nt(dup[:, 0:C_GU], wgu_ref[2]) + _dot_nt(dup[:, C_GU:2 * C_GU], wgu_ref[3]))
            t["sums"][G_SH2] = _rowsum(dh2)
            t["sums"][G_SC2] = _rowsum(dh2 * t["n2"])
            dn2 = dh2 * (1.0 + vrow(V_SC2))
            t["sums"][G_GFFN] = _rowsum(dn2 * t["xh1"])
            dx1_ref[t["r"], :] = t["dx2"] + _rms_bwd(dn2 * vrow(V_GFFN), t["xh1"], t["rstd1"])

        for stage in (stage_norm, stage_up, stage_down_loss, stage_back_act, stage_back_norm):
            for t in subs:
                stage(t)
        for j in subs[0]["sums"]:
            total = subs[0]["sums"][j]
            for t in subs[1:]:
                total = total + t["sums"][j]
            sm_ref[j:j + 1, :] += total

    row = lambda i: (i, 0)
    return pl.pallas_call(
        body, grid=(s // ts,),
        out_shape=(jax.ShapeDtypeStruct((s, D), F32), jax.ShapeDtypeStruct((s, D), BF16), jax.ShapeDtypeStruct((s, D_FF), BF16),
                   jax.ShapeDtypeStruct((s, D), BF16), jax.ShapeDtypeStruct((s, 2 * D_FF), BF16),
                   jax.ShapeDtypeStruct((N_SMALL, D), F32)),
        in_specs=[pl.BlockSpec((ts, D), row), pl.BlockSpec((ts, D), row), _VMEM, _VMEM, _VMEM],
        out_specs=[pl.BlockSpec((ts, D), row), pl.BlockSpec((ts, D), row), pl.BlockSpec((ts, D_FF), row),
                   pl.BlockSpec((ts, D), row), pl.BlockSpec((ts, 2 * D_FF), row), pl.BlockSpec((N_SMALL, D), lambda i: (0, 0))],
        compiler_params=_cparams(), name="ffn_loss")(x1, target, vecs, w_gu_g, w_dn)


def _bwd_mix(dx1, z1, merged, proj, kept, decay, vecs, w_rga, w_rgx, w_out, small, ts, deps=()):
    s = dx1.shape[0]
    nt = s // ts
    g = ts // TIME_BLOCKS
    assert g % 16 == 0

    def body(dx1_ref, z1_ref, mg_ref, proj_ref, kept_ref, decay_ref, hh_ref, v_ref, wa_ref, wx_ref,
             wo_ref, sm0_ref, dproj_ref, sm_ref, dwa_ref, dwx_ref, dwo_ref,
             h_buf, a_buf, dva_buf, du_buf, p_buf, q_buf, c_buf, lcarry):
        i = pl.program_id(0)
        first_tile = i == nt - 1

        @pl.when(i == 0)
        def _():
            a_buf[...] = jnp.zeros(a_buf.shape, F32)
            dva_buf[...] = jnp.zeros(dva_buf.shape, F32)
            du_buf[...] = jnp.zeros(du_buf.shape, F32)
            lcarry[...] = jnp.zeros((8, D), F32)
            sm_ref[...] = sm0_ref[...]
            dwa_ref[...] = jnp.zeros((HEADS, HB, HB), F32)
            dwx_ref[...] = jnp.zeros((HEADS, HB, HB), F32)
            dwo_ref[...] = jnp.zeros((D, D), F32)

        def seg(j):
            return proj_ref[:, j * D:(j + 1) * D].astype(F32)

        def vrow(j):
            return v_ref[j:j + 1, :]

        def acc(j, val):
            sm_ref[j:j + 1, :] += _rowsum(val)

        cb, cc, cx, rx = (seg(j) for j in range(4))
        ua = cc * cx
        va, r, ig, sga, sgb, gel, dgel, mult, u, h = (kept_ref[:, j * D:(j + 1) * D].astype(F32) for j in range(N_KEPT))
        a = decay_ref[...]
        rows = lax.broadcasted_iota(jnp.int32, (ts, D), 0)
        row0 = jnp.logical_and(rows == 0, first_tile)

        dx1 = dx1_ref[...]
        acc(G_GT1, dx1 * z1_ref[...].astype(F32))
        dz1 = (vrow(V_GT1) * dx1).astype(BF16)
        dwo_ref[...] += _dot_tn(mg_ref[...], dz1)
        dmg = _dot_nt(dz1, wo_ref[...])
        dya = dmg * sga
        dyb = dmg * sgb
        dproj_ref[:, 5 * D:6 * D] = (dya * (cb * va) * (1.0 - sga)).astype(BF16)
        dproj_ref[:, 6 * D:7 * D] = (dyb * (h * gel) * (1.0 - sgb)).astype(BF16)

        dproj_ref[:, 0:D] = (dya * va).astype(BF16)
        dva = dya * cb
        dva_early = _early_blocks(dva, dva_buf, g)
        dva1 = _later(dva, 1, dva_early, g)
        dva2 = _later(dva, 2, dva_early, g)
        dua = vrow(V_WA2) * dva + vrow(V_WA1) * dva1 + vrow(V_WA0) * dva2
        acc(G_WA2, ua * dva)
        acc(G_WA1, ua * dva1)
        acc(G_WA0, ua * dva2)
        dproj_ref[:, D:2 * D] = (dua * cx).astype(BF16)
        dproj_ref[:, 2 * D:3 * D] = (dua * cc).astype(BF16)

        dproj_ref[:, 4 * D:5 * D] = (dyb * h * dgel).astype(BF16)
        a_next = _later(a, 1, _early_blocks(a, a_buf, g), g)
        dh = dyb * gel
        last = TIME_BLOCKS - 1
        prods, sums = {last: a_next[last * g:]}, {last: dh[last * g:]}
        for k in range(last - 1, -1, -1):
            ak = a_next[k * g:(k + 1) * g]
            sums[k] = dh[k * g:(k + 1) * g] + ak * sums[k + 1]
            prods[k] = ak * prods[k + 1]
        p_buf[...] = prods[0]
        q_buf[...] = sums[0]
        state = lcarry[0:1, :]
        for j in range(g - 1, -1, -1):
            c_buf[j:j + 1, :] = state
            state = q_buf[j:j + 1, :] + p_buf[j:j + 1, :] * state
        lcarry[0:1, :] = state
        entering = c_buf[...]
        lam = jnp.concatenate([sums[k] + prods[k] * entering for k in range(TIME_BLOCKS)], axis=0)

        last = lax.broadcasted_iota(jnp.int32, hh_ref.shape, 0) == hh_ref.shape[0] - 1
        h_halo = [jnp.where(first_tile, 0.0, jnp.sum(jnp.where(last, hh_ref[...].astype(F32), 0.0), axis=0, keepdims=True))]
        da = lam * _earlier(h, 1, _late_blocks(h, h_buf, g, h_halo), g)
        dmult = jnp.where(row0, 0.0, lam * (ig * u))
        di = lam * mult * u
        du = lam * mult * ig
        dlog_a = da * a - dmult * (a * a) / mult
        lam_p = vrow(V_LAM)
        dr = dlog_a * (LRU_C * _log_sigmoid(lam_p))
        sm_ref[G_LAM:G_LAM + 1, :] += _rowsum(dlog_a * r) * (LRU_C * jax.nn.sigmoid(-lam_p))
        dpa = dr * r * (1.0 - r)
        dpx = di * ig * (1.0 - ig)
        acc(G_BA, dpa)
        acc(G_BX, dpx)
        dpab = dpa.astype(BF16)
        dpxb = dpx.astype(BF16)
        ub = u.astype(BF16)
        back = []
        for hd in range(HEADS):
            cols = slice(hd * HB, (hd + 1) * HB)
            back.append(_dot_nt(dpab[:, cols], wa_ref[hd]) + _dot_nt(dpxb[:, cols], wx_ref[hd]))
            dwa_ref[hd] += _dot_tn(ub[:, cols], dpab[:, cols])
            dwx_ref[hd] += _dot_tn(ub[:, cols], dpxb[:, cols])
        du = du + jnp.concatenate(back, axis=1)

        acc(G_CBB, du)
        du_early = _early_blocks(du, du_buf, g)
        du1 = _later(du, 1, du_early, g)
        du2 = _later(du, 2, du_early, g)
        du3 = _later(du, 3, du_early, g)
        dproj_ref[:, 3 * D:4 * D] = (vrow(V_WB3) * du + vrow(V_WB2) * du1 + vrow(V_WB1) * du2 + vrow(V_WB0) * du3).astype(BF16)
        acc(G_WB3, rx * du)
        acc(G_WB2, rx * du1)
        acc(G_WB1, rx * du2)
        acc(G_WB0, rx * du3)

    rev = lambda i: (nt - 1 - i, 0)
    h_halo16 = lambda i: (jnp.maximum((nt - 1 - i) * (ts // 16) - 1, 0), N_KEPT - 1)
    const2 = lambda i: (0, 0)
    const3 = lambda i: (0, 0, 0)
    return pl.pallas_call(
        _after(deps, body), grid=(nt,),
        out_shape=(jax.ShapeDtypeStruct((s, D_IN), BF16), jax.ShapeDtypeStruct((N_SMALL, D), F32),
                   jax.ShapeDtypeStruct((HEADS, HB, HB), F32), jax.ShapeDtypeStruct((HEADS, HB, HB), F32),
                   jax.ShapeDtypeStruct((D, D), F32)),
        in_specs=[_ANY] * len(deps) + [pl.BlockSpec((ts, D), rev), pl.BlockSpec((ts, D), rev), pl.BlockSpec((ts, D), rev),
                  pl.BlockSpec((ts, 4 * D), rev), pl.BlockSpec((ts, N_KEPT * D), rev), pl.BlockSpec((ts, D), rev),
                  pl.BlockSpec((16, D), h_halo16), _VMEM, _VMEM, _VMEM, _VMEM, _VMEM],
        out_specs=[pl.BlockSpec((ts, D_IN), rev), pl.BlockSpec((N_SMALL, D), const2),
                   pl.BlockSpec((HEADS, HB, HB), const3), pl.BlockSpec((HEADS, HB, HB), const3), pl.BlockSpec((D, D), const2)],
        scratch_shapes=[pltpu.VMEM((1, g + 8, D), F32), pltpu.VMEM((1, g + 8, D), F32),
                        pltpu.VMEM((2, g + 8, D), F32), pltpu.VMEM((3, g + 8, D), F32), pltpu.VMEM((g, D), F32),
                        pltpu.VMEM((g, D), F32), pltpu.VMEM((g, D), F32), pltpu.VMEM((8, D), F32)],
        compiler_params=_cparams(), name="bwd_mix")(*deps, dx1, z1, merged, proj, kept, decay, kept, vecs, w_rga,
                                                    w_rgx, w_out, small)


def _bwd_in(dproj, x, dx1, vecs, w_in_g, small, ts, deps=()):
    s = x.shape[0]

    def body(dp_ref, x_ref, dx1_ref, v_ref, w_ref, sm0_ref, gx_ref, sm_ref):
        @pl.when(pl.program_id(0) == 0)
        def _():
            sm_ref[...] = sm0_ref[...]

        def vrow(j):
            return v_ref[j:j + 1, :]

        dh1 = _dot_nt(dp_ref[:, 0:C_IN], w_ref[0])
        for k in range(1, N_CHIPS):
            dh1 += _dot_nt(dp_ref[:, k * C_IN:(k + 1) * C_IN], w_ref[k])
        xh, rstd = _rms(x_ref[...])
        sm_ref[G_SH1:G_SH1 + 1, :] += _rowsum(dh1)
        sm_ref[G_SC1:G_SC1 + 1, :] += _rowsum(dh1 * (xh * vrow(V_GMIX)))
        dn1 = dh1 * (1.0 + vrow(V_SC1))
        sm_ref[G_GMIX:G_GMIX + 1, :] += _rowsum(dn1 * xh)
        gx_ref[...] = dx1_ref[...] + _rms_bwd(dn1 * vrow(V_GMIX), xh, rstd)

    row = lambda i: (i, 0)
    return pl.pallas_call(
        _after(deps, body), grid=(s // ts,),
        out_shape=(jax.ShapeDtypeStruct((s, D), F32), jax.ShapeDtypeStruct((N_SMALL, D), F32)),
        in_specs=[_ANY] * len(deps) + [pl.BlockSpec((ts, D_IN), row), pl.BlockSpec((ts, D), row), pl.BlockSpec((ts, D), row),
                                       _VMEM, _VMEM, _VMEM],
        out_specs=[pl.BlockSpec((ts, D), row), pl.BlockSpec((N_SMALL, D), lambda i: (0, 0))],
        compiler_params=_cparams(), name="bwd_in")(*deps, dproj, x, dx1, vecs, w_in_g, small)


def _grad_w(a, b, n_col_blocks, ts, name, deps=()):
    s, m = a.shape
    tn = b.shape[1] // n_col_blocks
    n_steps = s // ts

    def body(a_ref, b_ref, o_ref, acc_ref):
        k = pl.program_id(1)

        @pl.when(k == 0)
        def _():
            acc_ref[...] = jnp.zeros((m, tn), F32)

        acc_ref[...] += _dot_tn(a_ref[...], b_ref[...])

        @pl.when(k == n_steps - 1)
        def _():
            o_ref[...] = acc_ref[...].astype(BF16)

    return pl.pallas_call(
        _after(deps, body), grid=(n_col_blocks, n_steps),
        out_shape=jax.ShapeDtypeStruct((n_col_blocks, m, tn), BF16),
        in_specs=[_ANY] * len(deps) + [pl.BlockSpec((ts, m), lambda n, k: (k, 0)), pl.BlockSpec((ts, tn), lambda n, k: (k, n))],
        out_specs=pl.BlockSpec((None, m, tn), lambda n, k: (n, 0, 0)),
        scratch_shapes=[pltpu.VMEM((m, tn), F32)],
        compiler_params=_cparams(2), name=name)(*deps, a, b)


def _ada_fwd(c_all, w_ada, b_ada):
    n = w_ada.shape[1]

    def body(c_ref, w_ref, b_ref, o_ref, ca_ref):
        c = c_ref[...]
        ca = c * jax.nn.sigmoid(c)
        ca_ref[...] = ca
        o_ref[...] = jnp.dot(ca, w_ref[...], preferred_element_type=F32, precision=lax.Precision.HIGHEST) + b_ref[...]

    return pl.pallas_call(
        body, out_shape=(jax.ShapeDtypeStruct((N_DEV, n), F32), jax.ShapeDtypeStruct((N_DEV, D), F32)),
        in_specs=[_VMEM] * 3, out_specs=[_VMEM] * 2, compiler_params=_cparams(0), name="ada_fwd")(c_all, w_ada, b_ada)


def _sum_small(parts):
    def body(p_ref, o_ref, d_ref):
        tot = p_ref[0]
        for dev in range(1, N_DEV):
            tot = tot + p_ref[dev]
        o_ref[...] = tot
        d_ref[...] = p_ref[:, 0:8, :]

    return pl.pallas_call(
        body, out_shape=(jax.ShapeDtypeStruct((N_SMALL, D), F32), jax.ShapeDtypeStruct((N_DEV, 8, D), F32)),
        in_specs=[_VMEM], out_specs=[_VMEM] * 2, compiler_params=_cparams(0), name="sum_small")(parts)


def _adamw_small(items, name):
    n = len(items)

    def body(*refs):
        ins, outs = refs[:4 * n], refs[4 * n:]
        for k in range(n):
            w_ref, g_ref, m_ref, v_ref = ins[4 * k:4 * k + 4]
            d_ref, nm_ref, nv_ref = outs[3 * k:3 * k + 3]
            g_ = g_ref[...]
            m_ = ADAM_B1 * m_ref[...] + (1.0 - ADAM_B1) * g_
            v_ = ADAM_B2 * v_ref[...] + (1.0 - ADAM_B2) * (g_ * g_)
            nm_ref[...] = m_
            nv_ref[...] = v_
            m_hat = m_ / (1.0 - ADAM_B1 ** ADAM_STEP)
            v_hat = v_ / (1.0 - ADAM_B2 ** ADAM_STEP)
            d_ref[...] = -ADAM_LR * (m_hat / (jnp.sqrt(v_hat) + ADAM_EPS) + ADAM_WD * w_ref[...])

    out = pl.pallas_call(
        body, out_shape=tuple(jax.ShapeDtypeStruct(it[0].shape, F32) for it in items for _ in range(3)),
        in_specs=[_VMEM] * (4 * n), out_specs=[_VMEM] * (3 * n), name=name)(*[a for it in items for a in it])
    return [tuple(out[3 * k:3 * k + 3]) for k in range(n)]


HALF_STEPS = 4


def _adamw_halves(sets, c_idx, name, deps=()):
    nh = HALF_STEPS
    n = len(sets)

    def body(c_ref, *refs):
        refs = refs[len(deps):]
        ins, outs = refs[:5 * n], refs[5 * n:]
        for k in range(n):
            w_ref, mine_ref, other_ref, m_ref, v_ref = ins[5 * k:5 * k + 5]
            g_ref, d_ref, nm_ref, nv_ref = outs[4 * k:4 * k + 4]
            g_ = jnp.where(pl.program_id(0) // nh == c_ref[0], mine_ref[...], other_ref[...])
            g_ref[...] = g_
            m_ = ADAM_B1 * m_ref[...] + (1.0 - ADAM_B1) * g_
            v_ = ADAM_B2 * v_ref[...] + (1.0 - ADAM_B2) * (g_ * g_)
            nm_ref[...] = m_
            nv_ref[...] = v_
            m_hat = m_ / (1.0 - ADAM_B1 ** ADAM_STEP)
            v_hat = v_ / (1.0 - ADAM_B2 ** ADAM_STEP)
            d_ref[...] = -ADAM_LR * (m_hat / (jnp.sqrt(v_hat) + ADAM_EPS) + ADAM_WD * w_ref[...])

    in_specs, out_specs, out_shape = [], [], []
    for w, mine, _, _, _ in sets:
        r2, cols = mine.shape
        block = (r2 // nh, cols)
        full = pl.BlockSpec(block, lambda i, c: (i, 0))
        in_specs += [full, pl.BlockSpec(block, lambda i, c: (jnp.clip(i - c[0] * nh, 0, nh - 1), 0)),
                     pl.BlockSpec(block, lambda i, c: (jnp.clip(i - (1 - c[0]) * nh, 0, nh - 1), 0)), full, full]
        out_specs += [full] * 4
        out_shape += [jax.ShapeDtypeStruct((2 * r2, cols), F32)] * 4
    out = pl.pallas_call(
        body,
        grid_spec=pltpu.PrefetchScalarGridSpec(num_scalar_prefetch=1, grid=(2 * nh,),
                                               in_specs=[_ANY] * len(deps) + in_specs, out_specs=out_specs),
        out_shape=tuple(out_shape), compiler_params=_cparams(), name=name,
    )(c_idx, *deps, *[a for s in sets for a in s])
    return [tuple(out[4 * k:4 * k + 4]) for k in range(n)]


def _adamw_ada(w, c_act, dmod, m, v):
    rows, n = w.shape
    tr = 128

    def body(c_ref, d_ref, w_ref, m_ref, v_ref, g_ref, dl_ref, nm_ref, nv_ref):
        g_ = lax.dot_general(c_ref[...], d_ref[...], (((0,), (0,)), ((), ())), preferred_element_type=F32,
                             precision=lax.Precision.HIGHEST)
        g_ref[...] = g_
        m_ = ADAM_B1 * m_ref[...] + (1.0 - ADAM_B1) * g_
        v_ = ADAM_B2 * v_ref[...] + (1.0 - ADAM_B2) * (g_ * g_)
        nm_ref[...] = m_
        nv_ref[...] = v_
        m_hat = m_ / (1.0 - ADAM_B1 ** ADAM_STEP)
        v_hat = v_ / (1.0 - ADAM_B2 ** ADAM_STEP)
        dl_ref[...] = -ADAM_LR * (m_hat / (jnp.sqrt(v_hat) + ADAM_EPS) + ADAM_WD * w_ref[...])

    spec = pl.BlockSpec((tr, n), lambda i: (i, 0))
    return pl.pallas_call(
        body, grid=(rows // tr,), out_shape=(jax.ShapeDtypeStruct((rows, n), F32),) * 4,
        in_specs=[pl.BlockSpec((N_DEV, tr), lambda i: (0, i)), _VMEM, spec, spec, spec], out_specs=[spec] * 4,
        compiler_params=_cparams(), name="adamw_w_ada")(c_act, dmod, w, m, v)


def _add_halves(grads, recvs, c_idx, name):
    nw = len(grads)

    def body(c_ref, *refs):
        for g_ref, r_ref, o_ref in zip(refs[:nw], refs[nw:2 * nw], refs[2 * nw:]):
            o_ref[...] = (g_ref[...].astype(F32) + r_ref[...].astype(F32)).astype(BF16)

    mine = [pl.BlockSpec((None, None) + g.shape[2:], lambda k, c: (k, c[0], 0, 0)) for g in grads]
    whole = [pl.BlockSpec((None,) + g.shape[2:], lambda k, c: (k, 0, 0)) for g in grads]
    return pl.pallas_call(
        body,
        grid_spec=pltpu.PrefetchScalarGridSpec(num_scalar_prefetch=1, grid=(N_CHIPS,), in_specs=mine + whole, out_specs=whole),
        out_shape=tuple(jax.ShapeDtypeStruct((N_CHIPS,) + g.shape[2:], BF16) for g in grads),
        compiler_params=_cparams(), name=name)(c_idx, *grads, *recvs)


def _sum_chips(owns, others, chip_idx, name):
    nw = len(owns)
    steps = 4 if all(a.shape[1] % 64 == 0 for a in owns) else 2

    def body(p_ref, *refs):
        for own_ref, got_ref, o_ref in zip(refs[:nw], refs[nw:2 * nw], refs[2 * nw:]):
            o_ref[...] = (((own_ref[...].astype(F32) + got_ref[0].astype(F32)) + got_ref[1].astype(F32))
                          + got_ref[2].astype(F32))

    blocks = [(a.shape[1] // steps, a.shape[2]) for a in owns]
    return pl.pallas_call(
        body,
        grid_spec=pltpu.PrefetchScalarGridSpec(
            num_scalar_prefetch=1, grid=(steps,),
            in_specs=([pl.BlockSpec((None,) + b, lambda i, p: (p[0], i, 0)) for b in blocks]
                      + [pl.BlockSpec((N_CHIPS - 1,) + b, lambda i, p: (0, i, 0)) for b in blocks]),
            out_specs=[pl.BlockSpec(b, lambda i, p: (i, 0)) for b in blocks]),
        out_shape=tuple(jax.ShapeDtypeStruct(a.shape[1:], F32) for a in owns), compiler_params=_cparams(),
        name=name)(chip_idx, *owns, *others)


def _place():
    x, y, c = lax.axis_index("x"), lax.axis_index("y"), lax.axis_index("c")
    return x, y, c, 2 * x + y


def _flip(v, bit):
    return 1 - v if bit else v


def _allgather8(v, name, deps=()):
    r, n = v.shape

    def body(*refs):
        v_ref, out_ref, send_sems, recv_sems, local_sem = refs[len(deps):]
        x, y, c, _ = _place()
        me = 4 * x + 2 * y + c
        mine = pltpu.make_async_copy(v_ref, out_ref.at[me], local_sem)
        mine.start()
        sends = []
        for rel in range(1, N_DEV):
            peer = (_flip(x, rel & 4), _flip(y, rel & 2), _flip(c, rel & 1))
            cp = pltpu.make_async_remote_copy(v_ref, out_ref.at[me], send_sems.at[rel - 1], recv_sems.at[rel - 1],
                                              device_id=peer, device_id_type=MESH)
            cp.start()
            sends.append(cp)
        for rel in range(1, N_DEV):
            peer = (_flip(x, rel & 4), _flip(y, rel & 2), _flip(c, rel & 1))
            peer_idx = 4 * peer[0] + 2 * peer[1] + peer[2]
            pltpu.make_async_remote_copy(v_ref, out_ref.at[peer_idx], send_sems.at[rel - 1], recv_sems.at[rel - 1],
                                         device_id=peer, device_id_type=MESH).wait_recv()
        for cp in sends:
            cp.wait_send()
        mine.wait()

    return pl.pallas_call(
        body, out_shape=jax.ShapeDtypeStruct((N_DEV, r, n), F32), in_specs=[_ANY] * len(deps) + [_VMEM], out_specs=_VMEM,
        scratch_shapes=[pltpu.SemaphoreType.DMA((N_DEV - 1,)), pltpu.SemaphoreType.DMA((N_DEV - 1,)), pltpu.SemaphoreType.DMA(())],
        name=name)(*deps, v)


_HBM = pl.BlockSpec(memory_space=pltpu.HBM)
_SEM = pl.BlockSpec(memory_space=pltpu.SEMAPHORE)
_EFFECT = pltpu.SideEffectType.DATAFLOW_SIDE_EFFECTING


def _xchg_start(name, plan, n_copies, srcs, lands, after=(), sibling_id=None):
    bufs = list(srcs) + list(lands)
    ns, nb = len(srcs), len(srcs) + len(lands)

    def body(*refs):
        send_sems, recv_sems, token = refs[nb + len(after)], refs[nb + len(after) + 1], refs[-1]
        if sibling_id is not None:
            x, y, c, _ = _place()
            barrier = pltpu.get_barrier_semaphore()
            pl.semaphore_signal(barrier, inc=1, device_id=(x, y, 1 - c), device_id_type=MESH)
            pl.semaphore_wait(barrier, 1)
        for i, (src, dst, peer, _) in enumerate(plan(_place(), refs[:ns], refs[ns:nb])):
            pltpu.make_async_remote_copy(src, dst, send_sems.at[i], recv_sems.at[i], device_id=peer, device_id_type=MESH).start()
        token[...] = jnp.zeros_like(token)

    out = pl.pallas_call(
        body, name=name,
        out_shape=(pltpu.SemaphoreType.DMA((n_copies,)), pltpu.SemaphoreType.DMA((n_copies,)),
                   *[pltpu.HBM(a.shape, a.dtype) for a in bufs], jax.ShapeDtypeStruct((8, 128), F32)),
        in_specs=[_HBM] * nb + [_ANY] * len(after), out_specs=(_SEM, _SEM, *[_HBM] * nb, _VMEM),
        input_output_aliases={i: 2 + i for i in range(nb)},
        compiler_params=pltpu.CompilerParams(has_side_effects=_EFFECT, collective_id=sibling_id),
    )(*[pltpu.with_memory_space_constraint(a, pltpu.HBM) for a in bufs], *after)
    return (out[0], out[1]), out[2:2 + ns], out[2 + ns:2 + nb], out[-1]


def _xchg_wait(name, plan, sems, srcs, lands, after, sem_ids=None):
    bufs = list(srcs) + list(lands)
    ns, nb = len(srcs), len(srcs) + len(lands)

    def body(*refs):
        send_sems, recv_sems = refs[nb], refs[nb + 1]
        copies = plan(_place(), refs[:ns], refs[ns:nb])
        ids = range(len(copies)) if sem_ids is None else sem_ids
        for i, (src, _, peer, mine) in zip(ids, copies, strict=True):
            if i is not None:
                cp = pltpu.make_async_remote_copy(src, mine, send_sems.at[i], recv_sems.at[i], device_id=peer,
                                                  device_id_type=MESH)
                cp.wait_send()
                cp.wait_recv()

    out = pl.pallas_call(
        body, name=name, out_shape=tuple(pltpu.HBM(a.shape, a.dtype) for a in bufs),
        in_specs=[_HBM] * nb + [_SEM, _SEM] + [_ANY] * len(after), out_specs=tuple([_HBM] * nb),
        input_output_aliases={i: i for i in range(nb)},
        compiler_params=pltpu.CompilerParams(has_side_effects=_EFFECT),
    )(*bufs, *sems, *after)
    return out[:ns], out[ns:]


def _other_chips(place, which=(1, 2, 3)):
    x, y, c, _ = place
    return [((_flip(x, j & 2), _flip(y, j & 1), c), 2 * _flip(x, j & 2) + _flip(y, j & 1)) for j in which]


def _plan_gather_ici(chips):
    def plan(place, src_refs, land_refs):
        _, _, c, p = place
        return [(s.at[c], l.at[p, c], peer, l.at[q, c]) for s, l, which in zip(src_refs, land_refs, chips, strict=True)
                for peer, q in _other_chips(place, which)]
    return plan


def _plan_relay(which):
    def plan(place, src_refs, land_refs):
        x, y, c, _ = place
        return [(l.at[q, c], l.at[q, c], (x, y, 1 - c), l.at[q, 1 - c]) for l in land_refs for _, q in _other_chips(place, which)]
    return plan


def _plan_swap(place, src_refs, land_refs):
    x, y, c, _ = place
    return [(s.at[k, 1 - c], l.at[k], (x, y, 1 - c), l.at[k]) for s, l in zip(src_refs, land_refs) for k in range(N_CHIPS)]


def _plan_scatter(place, src_refs, land_refs):
    return [(s.at[q], l.at[j], peer, l.at[j]) for s, l in zip(src_refs, land_refs)
            for j, (peer, q) in enumerate(_other_chips(place))]


def _plan_share(place, src_refs, land_refs):
    x, y, c, _ = place
    return [(s, l, (x, y, 1 - c), l) for s, l in zip(src_refs, land_refs)]


def _plan_gather8(place, src_refs, land_refs):
    x, y, c, _ = place
    me = 4 * x + 2 * y + c
    copies = []
    for s, l in zip(src_refs, land_refs):
        for rel in range(1, N_DEV):
            peer = (_flip(x, rel & 4), _flip(y, rel & 2), _flip(c, rel & 1))
            copies.append((s, l.at[me], peer, l.at[4 * peer[0] + 2 * peer[1] + peer[2]]))
    return copies


def _pack_rows(parts, n_rows, name, deps=()):
    def body(*refs):
        refs = refs[len(deps):]
        out_ref = refs[-1]
        out_ref[...] = jnp.zeros((n_rows, D), F32)
        at = 0
        for ref in refs[:-1]:
            k = ref.shape[0]
            out_ref[at:at + k, :] = ref[...]
            at += k

    return pl.pallas_call(
        body, out_shape=jax.ShapeDtypeStruct((n_rows, D), F32), in_specs=[_ANY] * len(deps) + [_VMEM] * len(parts),
        out_specs=_VMEM, name=name)(*deps, *parts)


TS_MM = 512
TS_IN = 1024
TS_GW = 1024
TS_MIX = 256


def _halved(a):
    n, r, cols = a.shape
    return a.reshape(n, 2, r // 2, cols)


SIBLING_IDS = (1, 2)


def _rs_swap(name, grads, after=()):
    lands = [lax.empty((N_CHIPS,) + g.shape[2:], g.dtype) for g in grads]
    sems, grads, lands, token = _xchg_start(name + "_swap", _plan_swap, N_CHIPS * len(grads), grads, lands, after,
                                            sibling_id=SIBLING_IDS[0])
    return name, sems, grads, lands, token


def _rs_scatter(handle, after, chip, ci):
    name, sems, grads, lands, _ = handle
    grads, from_sibling = _xchg_wait(name + "_swap_wait", _plan_swap, sems, grads, lands, after)
    c_arr = jnp.reshape(ci, (1,)).astype(jnp.int32)
    pair_sums = _add_halves(list(grads), list(from_sibling), c_arr, name + "_add_halves")
    lands = [lax.empty((N_CHIPS - 1,) + p.shape[1:], p.dtype) for p in pair_sums]
    sems, pair_sums, lands, token = _xchg_start(name + "_scatter", _plan_scatter, 3 * len(pair_sums), pair_sums, lands)
    return name, sems, pair_sums, lands, jnp.reshape(chip, (1,)).astype(jnp.int32), token


def _rs_share(handle, after):
    name, sems, pair_sums, lands, chip_idx, _ = handle
    pair_sums, received = _xchg_wait(name + "_scatter_wait", _plan_scatter, sems, pair_sums, lands, after)
    halves = _sum_chips(list(pair_sums), list(received), chip_idx, name + "_sum_chips")
    lands = [lax.empty(h.shape, h.dtype) for h in halves]
    sems, halves, lands, token = _xchg_start(name + "_share", _plan_share, len(halves), halves, lands,
                                             sibling_id=SIBLING_IDS[1])
    return name, sems, halves, lands, token


def _rs_end(handle, after):
    name, sems, halves, lands, _ = handle
    halves, others = _xchg_wait(name + "_share_wait", _plan_share, sems, halves, lands, after)
    return list(zip(halves, others))


def kernel(x, c, w_ada, b_ada, g_norm_mix, w_in, conv_a_w, conv_b_w, conv_b_bias, w_rg_a, b_rg_a, w_rg_x, b_rg_x, lru_lambda, w_out, g_norm_ffn, w_gate_up, w_down, g_norm_final, loss_target, m_w_ada, m_b_ada, m_g_norm_mix, m_w_in, m_conv_a_w, m_conv_b_w, m_conv_b_bias, m_w_rg_a, m_b_rg_a, m_w_rg_x, m_b_rg_x, m_lru_lambda, m_w_out, m_g_norm_ffn, m_w_gate_up, m_w_down, m_g_norm_final, v_w_ada, v_b_ada, v_g_norm_mix, v_w_in, v_conv_a_w, v_conv_b_w, v_conv_b_bias, v_w_rg_a, v_b_rg_a, v_w_rg_x, v_b_rg_x, v_lru_lambda, v_w_out, v_g_norm_ffn, v_w_gate_up, v_w_down, v_g_norm_final):
    xi, yi, ci = lax.axis_index("x"), lax.axis_index("y"), lax.axis_index("c")
    chip = 2 * xi + yi
    me = 2 * chip + ci
    n_ada = w_ada.shape[2]

    def widen(w):
        return jnp.pad(w, ((0, 0), (0, D - w.shape[1])))

    got = _allgather8(_pack_rows([c, widen(conv_a_w[0]), widen(conv_b_w[0])], 8, "pack_c_conv"), "gather_c_conv")
    c_all = got[:, 0, :]
    conv_full = got[::2, 1:8, :D // N_CHIPS].transpose(1, 0, 2).reshape(7, D)

    mod_part, c_act = _ada_fwd(c_all, w_ada[0], lax.dynamic_slice_in_dim(b_ada, chip * n_ada, n_ada, axis=1))
    mod_all = _allgather8(mod_part, "gather_mod")
    mod_mine = lax.dynamic_index_in_dim(mod_all, me, axis=1, keepdims=False)[::2].reshape(6, D)
    vecs = _pack_rows([mod_mine, g_norm_mix, g_norm_ffn, g_norm_final.reshape(1, D), conv_b_bias, b_rg_a, b_rg_x, lru_lambda,
                       conv_full], N_VEC, "pack_vecs")

    def rg_shard(w):
        return w[0].astype(BF16).reshape(2, HEADS * HB // N_CHIPS // 2, HB)

    shards = [w_in[0].astype(BF16).reshape(2, D // 2, C_IN), rg_shard(w_rg_a), rg_shard(w_rg_x),
              w_out[0].astype(BF16).reshape(2, D // N_CHIPS // 2, D), w_gate_up[0].astype(BF16).reshape(2, D // 2, C_GU),
              w_down[0].astype(BF16).reshape(2, D_FF // N_CHIPS // 2, D)]
    lands = [lax.dynamic_update_index_in_dim(lax.empty((N_CHIPS,) + s.shape, s.dtype), s, chip, 0) for s in shards]

    def send(name, first, last, after, chips):
        copies = [(k, j) for k, which in zip(range(first, last), chips, strict=True) for j in which]
        sems, srcs, zone, token = _xchg_start(name + "_ici", _plan_gather_ici(chips), len(copies), shards[first:last],
                                              lands[first:last], after)
        shards[first:last], lands[first:last] = srcs, zone
        return sems, copies, token

    def arrive(name, sent, first, last, after):
        sems, copies, _ = sent
        chips = [tuple(j for k, j in copies if k == want) for want in range(first, last)]
        ids = [copies.index((k, j)) for k, which in zip(range(first, last), chips) for j in which]
        srcs, zone = _xchg_wait(name + "_ici_wait", _plan_gather_ici(chips), sems, shards[first:last], lands[first:last], after,
                                ids)
        shards[first:last], lands[first:last] = srcs, zone

    def relay(name, first, last, which, sibling_id):
        plan = _plan_relay(which)
        sems, _, zone, token = _xchg_start(name + "_d2d", plan, len(which) * (last - first), [], lands[first:last],
                                           sibling_id=sibling_id)
        lands[first:last] = zone
        return name, plan, sems, first, last, token

    def relayed(handle, after):
        name, plan, sems, first, last, _ = handle
        lands[first:last] = _xchg_wait(name + "_d2d_wait", plan, sems, [], lands[first:last], after)[1]

    def to_blocks(v):
        return v.reshape(-1, TS_MIX // TIME_BLOCKS, TIME_BLOCKS, D).transpose(0, 2, 1, 3).reshape(v.shape)

    def from_blocks(v):
        return v.reshape(-1, TIME_BLOCKS, TS_MIX // TIME_BLOCKS, D).transpose(0, 2, 1, 3).reshape(v.shape)

    def chip_index(j):
        return jnp.reshape(chip ^ j, (1,)).astype(jnp.int32)

    def wg_in():
        return lands[0].reshape(N_CHIPS, D, C_IN)

    xs, target = to_blocks(x[0]), to_blocks(loss_target[0])
    sent_near = send("gather_in_near", 0, 1, [vecs], [(1, 2)])
    ts_in = min(TS_IN, xs.shape[0])
    h1, proj = _fwd_in_first(xs, vecs, wg_in(), chip_index(0), ts_in, deps=[sent_near[-1]])
    arrive("gather_in_near", sent_near, 0, 1, [proj])
    near = relay("gather_in_near", 0, 1, (1, 2), SIBLING_IDS[0])
    sent_rest = send("gather_rest", 0, 6, [near[-1]], [(3,)] + [(1, 2, 3)] * 5)
    relayed(near, [sent_rest[-1]])
    proj = _fwd_in_more(h1, wg_in(), proj, chip_index(1), ts_in, "fwd_in_y")
    proj = _fwd_in_more(h1, wg_in(), proj, chip_index(2), ts_in, "fwd_in_x")
    arrive("gather_in_far", sent_rest, 0, 1, [proj])
    far = relay("gather_in_far", 0, 1, (3,), SIBLING_IDS[1])
    arrive("gather_mix", sent_rest, 1, 4, [far[-1]])
    relayed(far, [far[-1]])
    mix = relay("gather_mix", 1, 4, (1, 2, 3), SIBLING_IDS[0])
    proj = _fwd_in_more(h1, wg_in(), proj, chip_index(3), ts_in, "fwd_in_xy", deps=[mix[-1]])
    relayed(mix, [proj])
    wg_rga, wg_rgx, wg_out = lands[1:4]
    wg_out = wg_out.reshape(D, D)

    def rg_full(wg):
        return wg.reshape(N_CHIPS, HEADS, HB // N_CHIPS, HB).transpose(1, 0, 2, 3).reshape(HEADS, HB, HB)

    wg_rga, wg_rgx = rg_full(wg_rga), rg_full(wg_rgx)

    x1, merged, z1, kept, decay = _fwd_mix(proj, xs, vecs, wg_rga, wg_rgx, wg_out, TS_MIX)
    arrive("gather_ffn", sent_rest, 4, 6, [x1])
    ffn = relay("gather_ffn", 4, 6, (1, 2, 3), SIBLING_IDS[1])
    relayed(ffn, [ffn[-1]])
    wg_gu, wg_dn = lands[4:6]
    wg_gu, wg_dn = wg_gu.reshape(N_CHIPS, D, C_GU), wg_dn.reshape(D_FF, D)
    dx1, h2, act, dz2, dgu, sm_ffn = _ffn_loss(x1, target, vecs, wg_gu, wg_dn, TS_MIX)

    def rg_chunks(dw):
        return _halved(dw.reshape(HEADS, N_CHIPS, HB // N_CHIPS, HB).transpose(1, 0, 2, 3).reshape(N_CHIPS, HB, HB).astype(BF16))

    ts_gw = min(TS_GW, xs.shape[0])
    g_dn = _grad_w(act, dz2, 1, ts_gw, "grad_w_down")
    g_gu = _grad_w(h2, dgu, N_CHIPS, ts_gw, "grad_w_gate_up")
    rs_b = _rs_swap("rs_b", [_halved(g_gu), _halved(g_dn.reshape(N_CHIPS, D_FF // N_CHIPS, D))])
    dproj, sm_mix, dw_rga, dw_rgx, dw_out = _bwd_mix(dx1, z1, merged, proj, kept, decay, vecs, wg_rga, wg_rgx, wg_out, sm_ffn, TS_MIX,
                                                     deps=[rs_b[-1]])
    rs_b = _rs_scatter(rs_b, [dproj], chip, ci)
    g_in = _grad_w(h1, dproj, N_CHIPS, ts_gw, "grad_w_in", deps=[rs_b[-1]])
    rs_b = _rs_share(rs_b, [g_in])
    rs_a = _rs_swap("rs_a", [_halved(g_in), rg_chunks(dw_rga), rg_chunks(dw_rgx),
                             _halved(dw_out.astype(BF16).reshape(N_CHIPS, D // N_CHIPS, D))], after=[rs_b[-1]])

    c_arr = jnp.reshape(ci, (1,)).astype(jnp.int32)

    def step_halves(name, items, deps=()):
        two_d = lambda a: a.reshape(-1, a.shape[-1])
        sets = [(two_d(w), halves[0], halves[1], two_d(m), two_d(v)) for w, halves, m, v in items.values()]
        for (n, (w, _, _, _)), out in zip(items.items(), _adamw_halves(sets, c_arr, name, deps)):
            res[n] = tuple(a.reshape(w.shape) for a in out)

    def shard_cols(row_block):
        return lax.dynamic_slice_in_dim(row_block, chip * (D // N_CHIPS), D // N_CHIPS, axis=1)

    gw_gu, gw_dn = _rs_end(rs_b, [rs_a[-1]])
    res = {}
    step_halves("adamw_ffn", {"w_gate_up": (w_gate_up, gw_gu, m_w_gate_up, v_w_gate_up),
                              "w_down": (w_down, gw_dn, m_w_down, v_w_down)}, [rs_a[-1]])
    rs_a = _rs_scatter(rs_a, [res["w_gate_up"][1], res["w_down"][1]], chip, ci)
    grad_x, sm_in = _bwd_in(dproj, xs, dx1, vecs, wg_in(), sm_mix, TS_MM, deps=[rs_a[-1]])
    small_zone = lax.dynamic_update_index_in_dim(lax.empty((N_DEV,) + sm_in.shape, F32), sm_in, me, 0)
    small_sems, small_src, small_zone, small_token = _xchg_start("gather_small", _plan_gather8, N_DEV - 1, [sm_in], [small_zone])
    rs_a = _rs_share(rs_a, [grad_x, small_token])

    small, per_dev = _sum_small(_xchg_wait("gather_small_wait", _plan_gather8, small_sems, small_src, small_zone, [rs_a[-1]])[1][0])
    dmod_all = per_dev[:, 0:6, :].reshape(N_DEV, 6 * D)
    grad_b_ada = small[0:6].reshape(1, 6 * D)
    res["w_ada"] = tuple(a[None] for a in _adamw_ada(w_ada[0], c_act, lax.dynamic_slice_in_dim(dmod_all, chip * n_ada, n_ada, axis=1),
                                                     m_w_ada[0], v_w_ada[0]))
    small_sets = {
        "b_ada": (b_ada.reshape(6, D), grad_b_ada.reshape(6, D), m_b_ada.reshape(6, D), v_b_ada.reshape(6, D)),
        "g_norm_mix": (g_norm_mix, small[G_GMIX:G_GMIX + 1], m_g_norm_mix, v_g_norm_mix),
        "conv_a_w": (conv_a_w[0], shard_cols(small[G_WA0:G_WA0 + 3]), m_conv_a_w[0], v_conv_a_w[0]),
        "conv_b_w": (conv_b_w[0], shard_cols(small[G_WB0:G_WB0 + 4]), m_conv_b_w[0], v_conv_b_w[0]),
        "conv_b_bias": (conv_b_bias, small[G_CBB:G_CBB + 1], m_conv_b_bias, v_conv_b_bias),
        "b_rg_a": (b_rg_a, small[G_BA:G_BA + 1], m_b_rg_a, v_b_rg_a),
        "b_rg_x": (b_rg_x, small[G_BX:G_BX + 1], m_b_rg_x, v_b_rg_x),
        "lru_lambda": (lru_lambda, small[G_LAM:G_LAM + 1], m_lru_lambda, v_lru_lambda),
        "g_norm_ffn": (g_norm_ffn, small[G_GFFN:G_GFFN + 1], m_g_norm_ffn, v_g_norm_ffn),
        "g_norm_final": (g_norm_final.reshape(1, D), small[G_GFIN:G_GFIN + 1], m_g_norm_final.reshape(1, D),
                         v_g_norm_final.reshape(1, D)),
    }
    stepped = _adamw_small(list(small_sets.values()), "adamw_small")
    for (n, (w_, g_, _, _)), (d_, nm_, nv_) in zip(small_sets.items(), stepped):
        shape = (1,) + w_.shape if n.startswith("conv_") and n != "conv_b_bias" else w_.shape
        res[n] = tuple(a.reshape(shape) for a in (g_, d_, nm_, nv_))
    gw_in, gw_rga, gw_rgx, gw_out = _rs_end(rs_a, [res[n][1] for n in res])
    step_halves("adamw_mix", {"w_in": (w_in, gw_in, m_w_in, v_w_in), "w_rg_a": (w_rg_a, gw_rga, m_w_rg_a, v_w_rg_a),
                              "w_rg_x": (w_rg_x, gw_rgx, m_w_rg_x, v_w_rg_x), "w_out": (w_out, gw_out, m_w_out, v_w_out)})
    res["b_ada"] = tuple(a.reshape(1, 6 * D) for a in res["b_ada"])
    res["g_norm_final"] = tuple(a.reshape(D) for a in res["g_norm_final"])
    names = ["w_ada", "b_ada", "g_norm_mix", "w_in", "conv_a_w", "conv_b_w", "conv_b_bias", "w_rg_a", "b_rg_a", "w_rg_x",
             "b_rg_x", "lru_lambda", "w_out", "g_norm_ffn", "w_gate_up", "w_down", "g_norm_final"]
    loss = jnp.sum(small[G_LOSS])
    return (loss, from_blocks(grad_x)[None], *[res[n][0] for n in names], *[res[n][1] for n in names],
            *[res[n][2] for n in names], *[res[n][3] for n in names])
```

```python
import functools

import jax
import jax.numpy as jnp
from jax import lax
from jax.experimental import pallas as pl
from jax.experimental.pallas import tpu as pltpu

F32 = jnp.float32
BF16 = jnp.bfloat16
MESH = pl.DeviceIdType.MESH

D = 1024
N_CHIPS = 4
N_DEV = 8
D_IN = 7 * D
C_IN = D_IN // N_CHIPS
D_FF = 2816
C_GU = 2 * D_FF // N_CHIPS
HEADS = 4
HB = D // HEADS
EPS = 1e-6
LRU_C = 8.0
ADAM_LR, ADAM_B1, ADAM_B2, ADAM_EPS, ADAM_WD, ADAM_STEP = 0.001, 0.9, 0.999, 1e-08, 0.01, 10
VMEM_LIMIT = 56 << 20

(V_SH1, V_SC1, V_GT1, V_SH2, V_SC2, V_GT2, V_GMIX, V_GFFN, V_GFIN, V_CBB, V_BA, V_BX, V_LAM,
 V_WA0, V_WA1, V_WA2, V_WB0, V_WB1, V_WB2, V_WB3) = range(20)
N_VEC = 24
(G_SH1, G_SC1, G_GT1, G_SH2, G_SC2, G_GT2, G_GMIX, G_CBB, G_BA, G_BX, G_LAM, G_GFFN, G_GFIN,
 G_WA0, G_WA1, G_WA2, G_WB0, G_WB1, G_WB2, G_WB3, G_LOSS) = range(21)
N_SMALL = 24

_VMEM = pl.BlockSpec(memory_space=pltpu.VMEM)
_ANY = pl.BlockSpec(memory_space=pl.ANY)


def _cparams(n_grid=1):
    return pltpu.CompilerParams(dimension_semantics=("arbitrary",) * n_grid, vmem_limit_bytes=VMEM_LIMIT)


def _after(deps, body):
    n = len(deps)
    return lambda *refs: body(*refs[n:])


def _rms(x):
    rstd = lax.rsqrt(jnp.mean(x * x, axis=-1, keepdims=True) + EPS)
    return x * rstd, rstd


def _rms_bwd(dxhat, xhat, rstd):
    return rstd * (dxhat - xhat * jnp.mean(dxhat * xhat, axis=-1, keepdims=True))


def _rowsum(v):
    return jnp.sum(v, axis=0, keepdims=True)


def _dot(a, b):
    return jnp.dot(a, b, preferred_element_type=F32)


def _dot_nt(a, b):
    return lax.dot_general(a, b, (((1,), (1,)), ((), ())), preferred_element_type=F32)


def _dot_tn(a, b):
    return lax.dot_general(a, b, (((0,), (0,)), ((), ())), preferred_element_type=F32)


def _gelu(x):
    k, c = 0.7978845608028654, 0.044715
    t = jnp.tanh(k * (x + c * x * x * x))
    return 0.5 * x * (1.0 + t), 0.5 * (1.0 + t) + 0.5 * x * (1.0 - t * t) * k * (1.0 + 3.0 * c * x * x)


def _log_sigmoid(lam):
    return jnp.minimum(lam, 0.0) - jnp.log1p(jnp.exp(-jnp.abs(lam)))


def _lru_gates(u, wa_ref, wx_ref, v_ref, row0):
    ub = u.astype(BF16)
    pre_a = jnp.concatenate([_dot(ub[:, h * HB:(h + 1) * HB], wa_ref[h]) for h in range(HEADS)], axis=1)
    pre_x = jnp.concatenate([_dot(ub[:, h * HB:(h + 1) * HB], wx_ref[h]) for h in range(HEADS)], axis=1)
    r = jax.nn.sigmoid(pre_a + v_ref[V_BA:V_BA + 1, :])
    ig = jax.nn.sigmoid(pre_x + v_ref[V_BX:V_BX + 1, :])
    log_a = LRU_C * r * _log_sigmoid(v_ref[V_LAM:V_LAM + 1, :])
    a = jnp.exp(log_a)
    x2 = 2.0 * log_a
    m2 = jnp.where(x2 > -0.03, -x2 * (1.0 + x2 * (0.5 + x2 * (1.0 / 6.0 + x2 * (1.0 / 24.0)))), 1.0 - a * a)
    mult = jnp.where(row0, 1.0, jnp.sqrt(jnp.maximum(m2, 0.0)))
    return r, ig, a, mult


TIME_BLOCKS = 8
N_KEPT = 10


def _late_blocks(v, buf, g, halo=None):
    n = buf.shape[0]
    out = []
    for idx in range(n):
        k = TIME_BLOCKS - n + idx
        buf[idx, 8:g + 8, :] = v[k * g:(k + 1) * g]
        if halo is not None:
            buf[idx, 7:8, :] = halo[idx]
        out.append(buf[idx, pl.ds(7, g), :])
        if halo is None:
            buf[idx, 7:8, :] = buf[idx, g + 7:g + 8, :]
    return out


def _earlier(v, s, late, g):
    return jnp.concatenate(late[len(late) - s:] + [v[0:(TIME_BLOCKS - s) * g]], axis=0)


def _early_blocks(v, buf, g):
    out = []
    for k in range(buf.shape[0]):
        buf[k, 0:g, :] = v[k * g:(k + 1) * g]
        out.append(buf[k, pl.ds(1, g), :])
        buf[k, g:g + 1, :] = buf[k, 0:1, :]
    return out


def _later(v, s, early, g):
    return jnp.concatenate([v[s * g:]] + early[0:s], axis=0)


def _fwd_in_first(x, vecs, w_in_g, q_idx, ts, deps=()):
    s = x.shape[0]

    def body(q_ref, x_ref, v_ref, w_ref, h1_ref, proj_ref):
        xhat, _ = _rms(x_ref[...])
        h = xhat * v_ref[V_GMIX:V_GMIX + 1, :] * (1.0 + v_ref[V_SC1:V_SC1 + 1, :]) + v_ref[V_SH1:V_SH1 + 1, :]
        hb = h.astype(BF16)
        h1_ref[...] = hb
        proj_ref[...] = _dot(hb, w_ref[...]).astype(BF16)

    return pl.pallas_call(
        lambda q_ref, *refs: body(q_ref, *refs[len(deps):]),
        grid_spec=pltpu.PrefetchScalarGridSpec(
            num_scalar_prefetch=1, grid=(s // ts,),
            in_specs=[_ANY] * len(deps) + [pl.BlockSpec((ts, D), lambda i, q: (i, 0)), _VMEM,
                                           pl.BlockSpec((None, D, C_IN), lambda i, q: (q[0], 0, 0))],
            out_specs=[pl.BlockSpec((ts, D), lambda i, q: (i, 0)), pl.BlockSpec((ts, C_IN), lambda i, q: (i, q[0]))]),
        out_shape=(jax.ShapeDtypeStruct((s, D), BF16), jax.ShapeDtypeStruct((s, D_IN), BF16)),
        compiler_params=_cparams(), name="fwd_in_own")(q_idx, *deps, x, vecs, w_in_g)


def _fwd_in_more(h1, w_in_g, proj, q_idx, ts, name, deps=()):
    s = h1.shape[0]

    def body(q_ref, h1_ref, w_ref, proj_in_ref, proj_ref):
        proj_ref[...] = _dot(h1_ref[...], w_ref[...]).astype(BF16)

    return pl.pallas_call(
        lambda q_ref, *refs: body(q_ref, *refs[len(deps):]),
        grid_spec=pltpu.PrefetchScalarGridSpec(
            num_scalar_prefetch=1, grid=(s // ts,),
            in_specs=[_ANY] * len(deps) + [pl.BlockSpec((ts, D), lambda i, q: (i, 0)),
                                           pl.BlockSpec((None, D, C_IN), lambda i, q: (q[0], 0, 0)), _ANY],
            out_specs=pl.BlockSpec((ts, C_IN), lambda i, q: (i, q[0]))),
        out_shape=jax.ShapeDtypeStruct((s, D_IN), BF16), input_output_aliases={len(deps) + 3: 0},
        compiler_params=_cparams(), name=name)(q_idx, *deps, h1, w_in_g, proj)


def _fwd_mix(proj, x, vecs, w_rga, w_rgx, w_out, ts, deps=()):
    s = x.shape[0]
    g = ts // TIME_BLOCKS

    def body(proj_ref, x_ref, v_ref, wa_ref, wx_ref, wo_ref, x1_ref, mg_ref, z1_ref, kept_ref, decay_ref,
             ua_buf, rx_buf, p_buf, q_buf, c_buf, hcarry):
        i = pl.program_id(0)

        @pl.when(i == 0)
        def _():
            ua_buf[...] = jnp.zeros(ua_buf.shape, F32)
            rx_buf[...] = jnp.zeros(rx_buf.shape, F32)
            hcarry[...] = jnp.zeros((8, D), F32)

        def seg(j):
            return proj_ref[:, j * D:(j + 1) * D].astype(F32)

        def vrow(j):
            return v_ref[j:j + 1, :]

        cb, cc, cx, rx, rg, ga, gb = (seg(j) for j in range(7))
        ua = cc * cx
        ua_late = _late_blocks(ua, ua_buf, g)
        rx_late = _late_blocks(rx, rx_buf, g)
        va = vrow(V_WA2) * ua + vrow(V_WA1) * _earlier(ua, 1, ua_late, g) + vrow(V_WA0) * _earlier(ua, 2, ua_late, g)
        u = (vrow(V_WB3) * rx + vrow(V_WB2) * _earlier(rx, 1, rx_late, g) + vrow(V_WB1) * _earlier(rx, 2, rx_late, g)
             + vrow(V_WB0) * _earlier(rx, 3, rx_late, g) + vrow(V_CBB))

        rows = lax.broadcasted_iota(jnp.int32, (ts, D), 0)
        row0 = jnp.logical_and(rows == 0, i == 0)
        r, ig, a, mult = _lru_gates(u, wa_ref, wx_ref, v_ref, row0)
        decay_ref[...] = a
        bx = mult * (ig * u)

        prods, sums = [a[0:g]], [bx[0:g]]
        for k in range(1, TIME_BLOCKS):
            ak = a[k * g:(k + 1) * g]
            sums.append(ak * sums[-1] + bx[k * g:(k + 1) * g])
            prods.append(ak * prods[-1])
        p_buf[...] = prods[-1]
        q_buf[...] = sums[-1]
        state = hcarry[0:1, :]
        for j in range(g):
            c_buf[j:j + 1, :] = state
            state = p_buf[j:j + 1, :] * state + q_buf[j:j + 1, :]
        hcarry[0:1, :] = state
        entering = c_buf[...]
        h = jnp.concatenate([sums[k] + prods[k] * entering for k in range(TIME_BLOCKS)], axis=0)

        gel, dgel = _gelu(rg)
        sga = jax.nn.sigmoid(ga)
        sgb = jax.nn.sigmoid(gb)
        for j, keep in enumerate((va, r, ig, sga, sgb, gel, dgel, mult, u, h)):
            kept_ref[:, j * D:(j + 1) * D] = keep.astype(BF16)
        merged = (sga * (cb * va) + sgb * (h * gel)).astype(BF16)
        mg_ref[...] = merged
        z1 = _dot(merged, wo_ref[...])
        z1_ref[...] = z1.astype(BF16)
        x1_ref[...] = x_ref[...] + vrow(V_GT1) * z1

    row = lambda i: (i, 0)
    return pl.pallas_call(
        _after(deps, body), grid=(s // ts,),
        out_shape=(jax.ShapeDtypeStruct((s, D), F32), jax.ShapeDtypeStruct((s, D), BF16), jax.ShapeDtypeStruct((s, D), BF16),
                   jax.ShapeDtypeStruct((s, N_KEPT * D), BF16), jax.ShapeDtypeStruct((s, D), F32)),
        in_specs=[_ANY] * len(deps) + [pl.BlockSpec((ts, D_IN), row), pl.BlockSpec((ts, D), row), _VMEM, _VMEM, _VMEM, _VMEM],
        out_specs=[pl.BlockSpec((ts, D), row)] * 3 + [pl.BlockSpec((ts, N_KEPT * D), row), pl.BlockSpec((ts, D), row)],
        scratch_shapes=[pltpu.VMEM((2, g + 8, D), F32), pltpu.VMEM((3, g + 8, D), F32), pltpu.VMEM((g, D), F32),
                        pltpu.VMEM((g, D), F32), pltpu.VMEM((g, D), F32), pltpu.VMEM((8, D), F32)],
        compiler_params=_cparams(), name="fwd_mix")(*deps, proj, x, vecs, w_rga, w_rgx, w_out)


def _ffn_loss(x1, target, vecs, w_gu_g, w_dn, ts):
    s = x1.shape[0]

    def body(x1_ref, t_ref, v_ref, wgu_ref, wdn_ref, dx1_ref, h2_ref, act_ref, dz2_ref, dgu_ref, sm_ref):
        @pl.when(pl.program_id(0) == 0)
        def _():
            sm_ref[...] = jnp.zeros((N_SMALL, D), F32)

        def vrow(j):
            return v_ref[j:j + 1, :]

        n_sub = 1
        rows = [slice(k * (ts // n_sub), (k + 1) * (ts // n_sub)) for k in range(n_sub)]
        subs = [dict(r=r, sums={}) for r in rows]

        def stage_norm(t):
            t["x1"] = x1_ref[t["r"], :]
            t["xh1"], t["rstd1"] = _rms(t["x1"])
            t["n2"] = t["xh1"] * vrow(V_GFFN)
            t["h2"] = (t["n2"] * (1.0 + vrow(V_SC2)) + vrow(V_SH2)).astype(BF16)
            h2_ref[t["r"], :] = t["h2"]

        def stage_up(t):
            h2 = t["h2"]
            g = jnp.concatenate([_dot(h2, wgu_ref[0]), _dot(h2, wgu_ref[1])], axis=1)
            t["up"] = jnp.concatenate([_dot(h2, wgu_ref[2]), _dot(h2, wgu_ref[3])], axis=1)
            t["g"] = g
            t["sg"] = jax.nn.sigmoid(g)
            t["silu"] = g * t["sg"]
            t["act"] = (t["silu"] * t["up"]).astype(BF16)
            act_ref[t["r"], :] = t["act"]

        def stage_down_loss(t):
            z2 = _dot(t["act"], wdn_ref[...])
            x2 = t["x1"] + vrow(V_GT2) * z2
            xh2, rstd2 = _rms(x2)
            err = xh2 * vrow(V_GFIN) - t_ref[t["r"], :]
            t["sums"][G_LOSS] = _rowsum((0.5 / D) * err * err)
            dy = err * (1.0 / D)
            t["sums"][G_GFIN] = _rowsum(dy * xh2)
            t["dx2"] = _rms_bwd(dy * vrow(V_GFIN), xh2, rstd2)
            t["sums"][G_GT2] = _rowsum(t["dx2"] * z2)
            t["dz2"] = (vrow(V_GT2) * t["dx2"]).astype(BF16)
            dz2_ref[t["r"], :] = t["dz2"]

        def stage_back_act(t):
            dact = _dot_nt(t["dz2"], wdn_ref[...])
            g, sg = t["g"], t["sg"]
            t["dgate"] = (dact * t["up"] * (sg * (1.0 + g * (1.0 - sg)))).astype(BF16)
            t["dup"] = (dact * t["silu"]).astype(BF16)
            dgu_ref[t["r"], 0:D_FF] = t["dgate"]
            dgu_ref[t["r"], D_FF:2 * D_FF] = t["dup"]

        def stage_back_norm(t):
            dgate, dup = t["dgate"], t["dup"]
            dh2 = (_dot_nt(dgate[:, 0:C_GU], wgu_ref[0]) + _dot_nt(dgate[:, C_GU:2 * C_GU], wgu_ref[1])
                   + _dot_nt(dup[:, 0:C_GU], wgu_ref[2]) + _dot_nt(dup[:, C_GU:2 * C_GU], wgu_ref[3]))
            t["sums"][G_SH2] = _rowsum(dh2)
            t["sums"][G_SC2] = _rowsum(dh2 * t["n2"])
            dn2 = dh2 * (1.0 + vrow(V_SC2))
            t["sums"][G_GFFN] = _rowsum(dn2 * t["xh1"])
            dx1_ref[t["r"], :] = t["dx2"] + _rms_bwd(dn2 * vrow(V_GFFN), t["xh1"], t["rstd1"])

        for stage in (stage_norm, stage_up, stage_down_loss, stage_back_act, stage_back_norm):
            for t in subs:
                stage(t)
        for j in subs[0]["sums"]:
            total = subs[0]["sums"][j]
            for t in subs[1:]:
                total = total + t["sums"][j]
            sm_ref[j:j + 1, :] += total

    row = lambda i: (i, 0)
    return pl.pallas_call(
        body, grid=(s // ts,),
        out_shape=(jax.ShapeDtypeStruct((s, D), F32), jax.ShapeDtypeStruct((s, D), BF16), jax.ShapeDtypeStruct((s, D_FF), BF16),
                   jax.ShapeDtypeStruct((s, D), BF16), jax.ShapeDtypeStruct((s, 2 * D_FF), BF16),
                   jax.ShapeDtypeStruct((N_SMALL, D), F32)),
        in_specs=[pl.BlockSpec((ts, D), row), pl.BlockSpec((ts, D), row), _VMEM, _VMEM, _VMEM],
        out_specs=[pl.BlockSpec((ts, D), row), pl.BlockSpec((ts, D), row), pl.BlockSpec((ts, D_FF), row),
                   pl.BlockSpec((ts, D), row), pl.BlockSpec((ts, 2 * D_FF), row), pl.BlockSpec((N_SMALL, D), lambda i: (0, 0))],
        compiler_params=_cparams(), name="ffn_loss")(x1, target, vecs, w_gu_g, w_dn)


def _bwd_mix(dx1, z1, merged, proj, kept, decay, vecs, w_rga, w_rgx, w_out, small, ts, deps=()):
    s = dx1.shape[0]
    nt = s // ts
    g = ts // TIME_BLOCKS
    assert g % 16 == 0

    def body(dx1_ref, z1_ref, mg_ref, proj_ref, kept_ref, decay_ref, hh_ref, v_ref, wa_ref, wx_ref,
             wo_ref, sm0_ref, dproj_ref, sm_ref, dwa_ref, dwx_ref, dwo_ref,
             h_buf, a_buf, dva_buf, du_buf, p_buf, q_buf, c_buf, lcarry):
        i = pl.program_id(0)
        first_tile = i == nt - 1

        @pl.when(i == 0)
        def _():
            a_buf[...] = jnp.zeros(a_buf.shape, F32)
            dva_buf[...] = jnp.zeros(dva_buf.shape, F32)
            du_buf[...] = jnp.zeros(du_buf.shape, F32)
            lcarry[...] = jnp.zeros((8, D), F32)
            sm_ref[...] = sm0_ref[...]
            dwa_ref[...] = jnp.zeros((HEADS, HB, HB), F32)
            dwx_ref[...] = jnp.zeros((HEADS, HB, HB), F32)
            dwo_ref[...] = jnp.zeros((D, D), F32)

        def seg(j):
            return proj_ref[:, j * D:(j + 1) * D].astype(F32)

        def vrow(j):
            return v_ref[j:j + 1, :]

        def acc(j, val):
            sm_ref[j:j + 1, :] += _rowsum(val)

        cb, cc, cx, rx = (seg(j) for j in range(4))
        ua = cc * cx
        va, r, ig, sga, sgb, gel, dgel, mult, u, h = (kept_ref[:, j * D:(j + 1) * D].astype(F32) for j in range(N_KEPT))
        a = decay_ref[...]
        rows = lax.broadcasted_iota(jnp.int32, (ts, D), 0)
        row0 = jnp.logical_and(rows == 0, first_tile)

        dx1 = dx1_ref[...]
        acc(G_GT1, dx1 * z1_ref[...].astype(F32))
        dz1 = (vrow(V_GT1) * dx1).astype(BF16)
        dwo_ref[...] += _dot_tn(mg_ref[...], dz1)
        dmg = _dot_nt(dz1, wo_ref[...])
        dya = dmg * sga
        dyb = dmg * sgb
        dproj_ref[:, 5 * D:6 * D] = (dya * (cb * va) * (1.0 - sga)).astype(BF16)
        dproj_ref[:, 6 * D:7 * D] = (dyb * (h * gel) * (1.0 - sgb)).astype(BF16)

        dproj_ref[:, 0:D] = (dya * va).astype(BF16)
        dva = dya * cb
        dva_early = _early_blocks(dva, dva_buf, g)
        dva1 = _later(dva, 1, dva_early, g)
        dva2 = _later(dva, 2, dva_early, g)
        dua = vrow(V_WA2) * dva + vrow(V_WA1) * dva1 + vrow(V_WA0) * dva2
        acc(G_WA2, ua * dva)
        acc(G_WA1, ua * dva1)
        acc(G_WA0, ua * dva2)
        dproj_ref[:, D:2 * D] = (dua * cx).astype(BF16)
        dproj_ref[:, 2 * D:3 * D] = (dua * cc).astype(BF16)

        dproj_ref[:, 4 * D:5 * D] = (dyb * h * dgel).astype(BF16)
        a_next = _later(a, 1, _early_blocks(a, a_buf, g), g)
        dh = dyb * gel
        last = TIME_BLOCKS - 1
        prods, sums = {last: a_next[last * g:]}, {last: dh[last * g:]}
        for k in range(last - 1, -1, -1):
            ak = a_next[k * g:(k + 1) * g]
            sums[k] = dh[k * g:(k + 1) * g] + ak * sums[k + 1]
            prods[k] = ak * prods[k + 1]
        p_buf[...] = prods[0]
        q_buf[...] = sums[0]
        state = lcarry[0:1, :]
        for j in range(g - 1, -1, -1):
            c_buf[j:j + 1, :] = state
            state = q_buf[j:j + 1, :] + p_buf[j:j + 1, :] * state
        lcarry[0:1, :] = state
        entering = c_buf[...]
        lam = jnp.concatenate([sums[k] + prods[k] * entering for k in range(TIME_BLOCKS)], axis=0)

        last = lax.broadcasted_iota(jnp.int32, hh_ref.shape, 0) == hh_ref.shape[0] - 1
        h_halo = [jnp.where(first_tile, 0.0, jnp.sum(jnp.where(last, hh_ref[...].astype(F32), 0.0), axis=0, keepdims=True))]
        da = lam * _earlier(h, 1, _late_blocks(h, h_buf, g, h_halo), g)
        dmult = jnp.where(row0, 0.0, lam * (ig * u))
        di = lam * mult * u
        du = lam * mult * ig
        dlog_a = da * a - dmult * (a * a) / mult
        lam_p = vrow(V_LAM)
        dr = dlog_a * (LRU_C * _log_sigmoid(lam_p))
        sm_ref[G_LAM:G_LAM + 1, :] += _rowsum(dlog_a * r) * (LRU_C * jax.nn.sigmoid(-lam_p))
        dpa = dr * r * (1.0 - r)
        dpx = di * ig * (1.0 - ig)
        acc(G_BA, dpa)
        acc(G_BX, dpx)
        dpab = dpa.astype(BF16)
        dpxb = dpx.astype(BF16)
        ub = u.astype(BF16)
        back = []
        for hd in range(HEADS):
            cols = slice(hd * HB, (hd + 1) * HB)
            back.append(_dot_nt(dpab[:, cols], wa_ref[hd]) + _dot_nt(dpxb[:, cols], wx_ref[hd]))
            dwa_ref[hd] += _dot_tn(ub[:, cols], dpab[:, cols])
            dwx_ref[hd] += _dot_tn(ub[:, cols], dpxb[:, cols])
        du = du + jnp.concatenate(back, axis=1)

        acc(G_CBB, du)
        du_early = _early_blocks(du, du_buf, g)
        du1 = _later(du, 1, du_early, g)
        du2 = _later(du, 2, du_early, g)
        du3 = _later(du, 3, du_early, g)
        dproj_ref[:, 3 * D:4 * D] = (vrow(V_WB3) * du + vrow(V_WB2) * du1 + vrow(V_WB1) * du2 + vrow(V_WB0) * du3).astype(BF16)
        acc(G_WB3, rx * du)
        acc(G_WB2, rx * du1)
        acc(G_WB1, rx * du2)
        acc(G_WB0, rx * du3)

    rev = lambda i: (nt - 1 - i, 0)
    h_halo16 = lambda i: (jnp.maximum((nt - 1 - i) * (ts // 16) - 1, 0), N_KEPT - 1)
    const2 = lambda i: (0, 0)
    const3 = lambda i: (0, 0, 0)
    return pl.pallas_call(
        _after(deps, body), grid=(nt,),
        out_shape=(jax.ShapeDtypeStruct((s, D_IN), BF16), jax.ShapeDtypeStruct((N_SMALL, D), F32),
                   jax.ShapeDtypeStruct((HEADS, HB, HB), F32), jax.ShapeDtypeStruct((HEADS, HB, HB), F32),
                   jax.ShapeDtypeStruct((D, D), F32)),
        in_specs=[_ANY] * len(deps) + [pl.BlockSpec((ts, D), rev), pl.BlockSpec((ts, D), rev), pl.BlockSpec((ts, D), rev),
                  pl.BlockSpec((ts, 4 * D), rev), pl.BlockSpec((ts, N_KEPT * D), rev), pl.BlockSpec((ts, D), rev),
                  pl.BlockSpec((16, D), h_halo16), _VMEM, _VMEM, _VMEM, _VMEM, _VMEM],
        out_specs=[pl.BlockSpec((ts, D_IN), rev), pl.BlockSpec((N_SMALL, D), const2),
                   pl.BlockSpec((HEADS, HB, HB), const3), pl.BlockSpec((HEADS, HB, HB), const3), pl.BlockSpec((D, D), const2)],
        scratch_shapes=[pltpu.VMEM((1, g + 8, D), F32), pltpu.VMEM((1, g + 8, D), F32),
                        pltpu.VMEM((2, g + 8, D), F32), pltpu.VMEM((3, g + 8, D), F32), pltpu.VMEM((g, D), F32),
                        pltpu.VMEM((g, D), F32), pltpu.VMEM((g, D), F32), pltpu.VMEM((8, D), F32)],
        compiler_params=_cparams(), name="bwd_mix")(*deps, dx1, z1, merged, proj, kept, decay, kept, vecs, w_rga,
                                                    w_rgx, w_out, small)


def _bwd_in(dproj, x, dx1, vecs, w_in_g, small, ts, deps=()):
    s = x.shape[0]

    def body(dp_ref, x_ref, dx1_ref, v_ref, w_ref, sm0_ref, gx_ref, sm_ref):
        @pl.when(pl.program_id(0) == 0)
        def _():
            sm_ref[...] = sm0_ref[...]

        def vrow(j):
            return v_ref[j:j + 1, :]

        dh1 = _dot_nt(dp_ref[:, 0:C_IN], w_ref[0])
        for k in range(1, N_CHIPS):
            dh1 += _dot_nt(dp_ref[:, k * C_IN:(k + 1) * C_IN], w_ref[k])
        xh, rstd = _rms(x_ref[...])
        sm_ref[G_SH1:G_SH1 + 1, :] += _rowsum(dh1)
        sm_ref[G_SC1:G_SC1 + 1, :] += _rowsum(dh1 * (xh * vrow(V_GMIX)))
        dn1 = dh1 * (1.0 + vrow(V_SC1))
        sm_ref[G_GMIX:G_GMIX + 1, :] += _rowsum(dn1 * xh)
        gx_ref[...] = dx1_ref[...] + _rms_bwd(dn1 * vrow(V_GMIX), xh, rstd)

    row = lambda i: (i, 0)
    return pl.pallas_call(
        _after(deps, body), grid=(s // ts,),
        out_shape=(jax.ShapeDtypeStruct((s, D), F32), jax.ShapeDtypeStruct((N_SMALL, D), F32)),
        in_specs=[_ANY] * len(deps) + [pl.BlockSpec((ts, D_IN), row), pl.BlockSpec((ts, D), row), pl.BlockSpec((ts, D), row),
                                       _VMEM, _VMEM, _VMEM],
        out_specs=[pl.BlockSpec((ts, D), row), pl.BlockSpec((N_SMALL, D), lambda i: (0, 0))],
        compiler_params=_cparams(), name="bwd_in")(*deps, dproj, x, dx1, vecs, w_in_g, small)


def _grad_w(a, b, n_col_blocks, ts, name, deps=()):
    s, m = a.shape
    tn = b.shape[1] // n_col_blocks
    n_steps = s // ts

    def body(a_ref, b_ref, o_ref, acc_ref):
        k = pl.program_id(1)

        @pl.when(k == 0)
        def _():
            acc_ref[...] = jnp.zeros((m, tn), F32)

        acc_ref[...] += _dot_tn(a_ref[...], b_ref[...])

        @pl.when(k == n_steps - 1)
        def _():
            o_ref[...] = acc_ref[...].astype(BF16)

    return pl.pallas_call(
        _after(deps, body), grid=(n_col_blocks, n_steps),
        out_shape=jax.ShapeDtypeStruct((n_col_blocks, m, tn), BF16),
        in_specs=[_ANY] * len(deps) + [pl.BlockSpec((ts, m), lambda n, k: (k, 0)), pl.BlockSpec((ts, tn), lambda n, k: (k, n))],
        out_specs=pl.BlockSpec((None, m, tn), lambda n, k: (n, 0, 0)),
        scratch_shapes=[pltpu.VMEM((m, tn), F32)],
        compiler_params=_cparams(2), name=name)(*deps, a, b)


def _ada_fwd(c_all, w_ada, b_ada):
    n = w_ada.shape[1]

    def body(c_ref, w_ref, b_ref, o_ref, ca_ref):
        c = c_ref[...]
        ca = c * jax.nn.sigmoid(c)
        ca_ref[...] = ca
        o_ref[...] = jnp.dot(ca, w_ref[...], preferred_element_type=F32, precision=lax.Precision.HIGHEST) + b_ref[...]

    return pl.pallas_call(
        body, out_shape=(jax.ShapeDtypeStruct((N_DEV, n), F32), jax.ShapeDtypeStruct((N_DEV, D), F32)),
        in_specs=[_VMEM] * 3, out_specs=[_VMEM] * 2, compiler_params=_cparams(0), name="ada_fwd")(c_all, w_ada, b_ada)


def _sum_small(parts):
    def body(p_ref, o_ref, d_ref):
        tot = p_ref[0]
        for dev in range(1, N_DEV):
            tot = tot + p_ref[dev]
        o_ref[...] = tot
        d_ref[...] = p_ref[:, 0:8, :]

    return pl.pallas_call(
        body, out_shape=(jax.ShapeDtypeStruct((N_SMALL, D), F32), jax.ShapeDtypeStruct((N_DEV, 8, D), F32)),
        in_specs=[_VMEM], out_specs=[_VMEM] * 2, compiler_params=_cparams(0), name="sum_small")(parts)


def _adamw_small(items, name):
    n = len(items)

    def body(*refs):
        ins, outs = refs[:4 * n], refs[4 * n:]
        for k in range(n):
            w_ref, g_ref, m_ref, v_ref = ins[4 * k:4 * k + 4]
            d_ref, nm_ref, nv_ref = outs[3 * k:3 * k + 3]
            g_ = g_ref[...]
            m_ = ADAM_B1 * m_ref[...] + (1.0 - ADAM_B1) * g_
            v_ = ADAM_B2 * v_ref[...] + (1.0 - ADAM_B2) * (g_ * g_)
            nm_ref[...] = m_
            nv_ref[...] = v_
            m_hat = m_ / (1.0 - ADAM_B1 ** ADAM_STEP)
            v_hat = v_ / (1.0 - ADAM_B2 ** ADAM_STEP)
            d_ref[...] = -ADAM_LR * (m_hat / (jnp.sqrt(v_hat) + ADAM_EPS) + ADAM_WD * w_ref[...])

    out = pl.pallas_call(
        body, out_shape=tuple(jax.ShapeDtypeStruct(it[0].shape, F32) for it in items for _ in range(3)),
        in_specs=[_VMEM] * (4 * n), out_specs=[_VMEM] * (3 * n), name=name)(*[a for it in items for a in it])
    return [tuple(out[3 * k:3 * k + 3]) for k in range(n)]


HALF_STEPS = 4


def _adamw_halves(sets, c_idx, name, deps=()):
    nh = HALF_STEPS
    n = len(sets)

    def body(c_ref, *refs):
        refs = refs[len(deps):]
        ins, outs = refs[:5 * n], refs[5 * n:]
        for k in range(n):
            w_ref, mine_ref, other_ref, m_ref, v_ref = ins[5 * k:5 * k + 5]
            g_ref, d_ref, nm_ref, nv_ref = outs[4 * k:4 * k + 4]
            g_ = jnp.where(pl.program_id(0) // nh == c_ref[0], mine_ref[...], other_ref[...])
            g_ref[...] = g_
            m_ = ADAM_B1 * m_ref[...] + (1.0 - ADAM_B1) * g_
            v_ = ADAM_B2 * v_ref[...] + (1.0 - ADAM_B2) * (g_ * g_)
            nm_ref[...] = m_
            nv_ref[...] = v_
            m_hat = m_ / (1.0 - ADAM_B1 ** ADAM_STEP)
            v_hat = v_ / (1.0 - ADAM_B2 ** ADAM_STEP)
            d_ref[...] = -ADAM_LR * (m_hat / (jnp.sqrt(v_hat) + ADAM_EPS) + ADAM_WD * w_ref[...])

    in_specs, out_specs, out_shape = [], [], []
    for w, mine, _, _, _ in sets:
        r2, cols = mine.shape
        block = (r2 // nh, cols)
        full = pl.BlockSpec(block, lambda i, c: (i, 0))
        in_specs += [full, pl.BlockSpec(block, lambda i, c: (jnp.clip(i - c[0] * nh, 0, nh - 1), 0)),
                     pl.BlockSpec(block, lambda i, c: (jnp.clip(i - (1 - c[0]) * nh, 0, nh - 1), 0)), full, full]
        out_specs += [full] * 4
        out_shape += [jax.ShapeDtypeStruct((2 * r2, cols), F32)] * 4
    out = pl.pallas_call(
        body,
        grid_spec=pltpu.PrefetchScalarGridSpec(num_scalar_prefetch=1, grid=(2 * nh,),
                                               in_specs=[_ANY] * len(deps) + in_specs, out_specs=out_specs),
        out_shape=tuple(out_shape), compiler_params=_cparams(), name=name,
    )(c_idx, *deps, *[a for s in sets for a in s])
    return [tuple(out[4 * k:4 * k + 4]) for k in range(n)]


def _adamw_ada(w, c_act, dmod, m, v):
    rows, n = w.shape
    tr = 128

    def body(c_ref, d_ref, w_ref, m_ref, v_ref, g_ref, dl_ref, nm_ref, nv_ref):
        g_ = lax.dot_general(c_ref[...], d_ref[...], (((0,), (0,)), ((), ())), preferred_element_type=F32,
                             precision=lax.Precision.HIGHEST)
        g_ref[...] = g_
        m_ = ADAM_B1 * m_ref[...] + (1.0 - ADAM_B1) * g_
        v_ = ADAM_B2 * v_ref[...] + (1.0 - ADAM_B2) * (g_ * g_)
        nm_ref[...] = m_
        nv_ref[...] = v_
        m_hat = m_ / (1.0 - ADAM_B1 ** ADAM_STEP)
        v_hat = v_ / (1.0 - ADAM_B2 ** ADAM_STEP)
        dl_ref[...] = -ADAM_LR * (m_hat / (jnp.sqrt(v_hat) + ADAM_EPS) + ADAM_WD * w_ref[...])

    spec = pl.BlockSpec((tr, n), lambda i: (i, 0))
    return pl.pallas_call(
        body, grid=(rows // tr,), out_shape=(jax.ShapeDtypeStruct((rows, n), F32),) * 4,
        in_specs=[pl.BlockSpec((N_DEV, tr), lambda i: (0, i)), _VMEM, spec, spec, spec], out_specs=[spec] * 4,
        compiler_params=_cparams(), name="adamw_w_ada")(c_act, dmod, w, m, v)


def _add_halves(grads, recvs, c_idx, name):
    nw = len(grads)

    def body(c_ref, *refs):
        for g_ref, r_ref, o_ref in zip(refs[:nw], refs[nw:2 * nw], refs[2 * nw:]):
            o_ref[...] = (g_ref[...].astype(F32) + r_ref[...].astype(F32)).astype(BF16)

    mine = [pl.BlockSpec((None, None) + g.shape[2:], lambda k, c: (k, c[0], 0, 0)) for g in grads]
    whole = [pl.BlockSpec((None,) + g.shape[2:], lambda k, c: (k, 0, 0)) for g in grads]
    return pl.pallas_call(
        body,
        grid_spec=pltpu.PrefetchScalarGridSpec(num_scalar_prefetch=1, grid=(N_CHIPS,), in_specs=mine + whole, out_specs=whole),
        out_shape=tuple(jax.ShapeDtypeStruct((N_CHIPS,) + g.shape[2:], BF16) for g in grads),
        compiler_params=_cparams(), name=name)(c_idx, *grads, *recvs)


def _sum_chips(owns, others, chip_idx, name):
    nw = len(owns)
    steps = 4 if all(a.shape[1] % 64 == 0 for a in owns) else 2

    def body(p_ref, *refs):
        for own_ref, got_ref, o_ref in zip(refs[:nw], refs[nw:2 * nw], refs[2 * nw:]):
            o_ref[...] = (((own_ref[...].astype(F32) + got_ref[0].astype(F32)) + got_ref[1].astype(F32))
                          + got_ref[2].astype(F32))

    blocks = [(a.shape[1] // steps, a.shape[2]) for a in owns]
    return pl.pallas_call(
        body,
        grid_spec=pltpu.PrefetchScalarGridSpec(
            num_scalar_prefetch=1, grid=(steps,),
            in_specs=([pl.BlockSpec((None,) + b, lambda i, p: (p[0], i, 0)) for b in blocks]
                      + [pl.BlockSpec((N_CHIPS - 1,) + b, lambda i, p: (0, i, 0)) for b in blocks]),
            out_specs=[pl.BlockSpec(b, lambda i, p: (i, 0)) for b in blocks]),
        out_shape=tuple(jax.ShapeDtypeStruct(a.shape[1:], F32) for a in owns), compiler_params=_cparams(),
        name=name)(chip_idx, *owns, *others)


def _place():
    x, y, c = lax.axis_index("x"), lax.axis_index("y"), lax.axis_index("c")
    return x, y, c, 2 * x + y


def _flip(v, bit):
    return 1 - v if bit else v


def _allgather8(v, name, deps=()):
    r, n = v.shape

    def body(*refs):
        v_ref, out_ref, send_sems, recv_sems, local_sem = refs[len(deps):]
        x, y, c, _ = _place()
        me = 4 * x + 2 * y + c
        mine = pltpu.make_async_copy(v_ref, out_ref.at[me], local_sem)
        mine.start()
        sends = []
        for rel in range(1, N_DEV):
            peer = (_flip(x, rel & 4), _flip(y, rel & 2), _flip(c, rel & 1))
            cp = pltpu.make_async_remote_copy(v_ref, out_ref.at[me], send_sems.at[rel - 1], recv_sems.at[rel - 1],
                                              device_id=peer, device_id_type=MESH)
            cp.start()
            sends.append(cp)
        for rel in range(1, N_DEV):
            peer = (_flip(x, rel & 4), _flip(y, rel & 2), _flip(c, rel & 1))
            peer_idx = 4 * peer[0] + 2 * peer[1] + peer[2]
            pltpu.make_async_remote_copy(v_ref, out_ref.at[peer_idx], send_sems.at[rel - 1], recv_sems.at[rel - 1],
                                         device_id=peer, device_id_type=MESH).wait_recv()
        for cp in sends:
            cp.wait_send()
        mine.wait()

    return pl.pallas_call(
        body, out_shape=jax.ShapeDtypeStruct((N_DEV, r, n), F32), in_specs=[_ANY] * len(deps) + [_VMEM], out_specs=_VMEM,
        scratch_shapes=[pltpu.SemaphoreType.DMA((N_DEV - 1,)), pltpu.SemaphoreType.DMA((N_DEV - 1,)), pltpu.SemaphoreType.DMA(())],
        name=name)(*deps, v)


_HBM = pl.BlockSpec(memory_space=pltpu.HBM)
_SEM = pl.BlockSpec(memory_space=pltpu.SEMAPHORE)
_EFFECT = pltpu.SideEffectType.DATAFLOW_SIDE_EFFECTING


def _xchg_start(name, plan, n_copies, srcs, lands, after=(), sibling_id=None):
    bufs = list(srcs) + list(lands)
    ns, nb = len(srcs), len(srcs) + len(lands)

    def body(*refs):
        send_sems, recv_sems, token = refs[nb + len(after)], refs[nb + len(after) + 1], refs[-1]
        if sibling_id is not None:
            x, y, c, _ = _place()
            barrier = pltpu.get_barrier_semaphore()
            pl.semaphore_signal(barrier, inc=1, device_id=(x, y, 1 - c), device_id_type=MESH)
            pl.semaphore_wait(barrier, 1)
        for i, (src, dst, peer, _) in enumerate(plan(_place(), refs[:ns], refs[ns:nb])):
            pltpu.make_async_remote_copy(src, dst, send_sems.at[i], recv_sems.at[i], device_id=peer, device_id_type=MESH).start()
        token[...] = jnp.zeros_like(token)

    out = pl.pallas_call(
        body, name=name,
        out_shape=(pltpu.SemaphoreType.DMA((n_copies,)), pltpu.SemaphoreType.DMA((n_copies,)),
                   *[pltpu.HBM(a.shape, a.dtype) for a in bufs], jax.ShapeDtypeStruct((8, 128), F32)),
        in_specs=[_HBM] * nb + [_ANY] * len(after), out_specs=(_SEM, _SEM, *[_HBM] * nb, _VMEM),
        input_output_aliases={i: 2 + i for i in range(nb)},
        compiler_params=pltpu.CompilerParams(has_side_effects=_EFFECT, collective_id=sibling_id),
    )(*[pltpu.with_memory_space_constraint(a, pltpu.HBM) for a in bufs], *after)
    return (out[0], out[1]), out[2:2 + ns], out[2 + ns:2 + nb], out[-1]


def _xchg_wait(name, plan, sems, srcs, lands, after, sem_ids=None):
    bufs = list(srcs) + list(lands)
    ns, nb = len(srcs), len(srcs) + len(lands)

    def body(*refs):
        send_sems, recv_sems = refs[nb], refs[nb + 1]
        copies = plan(_place(), refs[:ns], refs[ns:nb])
        ids = range(len(copies)) if sem_ids is None else sem_ids
        for i, (src, _, peer, mine) in zip(ids, copies, strict=True):
            if i is not None:
                cp = pltpu.make_async_remote_copy(src, mine, send_sems.at[i], recv_sems.at[i], device_id=peer,
                                                  device_id_type=MESH)
                cp.wait_send()
                cp.wait_recv()

    out = pl.pallas_call(
        body, name=name, out_shape=tuple(pltpu.HBM(a.shape, a.dtype) for a in bufs),
        in_specs=[_HBM] * nb + [_SEM, _SEM] + [_ANY] * len(after), out_specs=tuple([_HBM] * nb),
        input_output_aliases={i: i for i in range(nb)},
        compiler_params=pltpu.CompilerParams(has_side_effects=_EFFECT),
    )(*bufs, *sems, *after)
    return out[:ns], out[ns:]


def _other_chips(place, which=(1, 2, 3)):
    x, y, c, _ = place
    return [((_flip(x, j & 2), _flip(y, j & 1), c), 2 * _flip(x, j & 2) + _flip(y, j & 1)) for j in which]


def _plan_gather_ici(chips):
    def plan(place, src_refs, land_refs):
        _, _, c, p = place
        return [(s.at[c], l.at[p, c], peer, l.at[q, c]) for s, l, which in zip(src_refs, land_refs, chips, strict=True)
                for peer, q in _other_chips(place, which)]
    return plan


def _plan_relay(which):
    def plan(place, src_refs, land_refs):
        x, y, c, _ = place
        return [(l.at[q, c], l.at[q, c], (x, y, 1 - c), l.at[q, 1 - c]) for l in land_refs for _, q in _other_chips(place, which)]
    return plan


def _plan_swap(place, src_refs, land_refs):
    x, y, c, _ = place
    return [(s.at[k, 1 - c], l.at[k], (x, y, 1 - c), l.at[k]) for s, l in zip(src_refs, land_refs) for k in range(N_CHIPS)]


def _plan_scatter(place, src_refs, land_refs):
    return [(s.at[q], l.at[j], peer, l.at[j]) for s, l in zip(src_refs, land_refs)
            for j, (peer, q) in enumerate(_other_chips(place))]


def _plan_share(place, src_refs, land_refs):
    x, y, c, _ = place
    return [(s, l, (x, y, 1 - c), l) for s, l in zip(src_refs, land_refs)]


def _plan_gather8(place, src_refs, land_refs):
    x, y, c, _ = place
    me = 4 * x + 2 * y + c
    copies = []
    for s, l in zip(src_refs, land_refs):
        for rel in range(1, N_DEV):
            peer = (_flip(x, rel & 4), _flip(y, rel & 2), _flip(c, rel & 1))
            copies.append((s, l.at[me], peer, l.at[4 * peer[0] + 2 * peer[1] + peer[2]]))
    return copies


def _pack_rows(parts, n_rows, name, deps=()):
    def body(*refs):
        refs = refs[len(deps):]
        out_ref = refs[-1]
        out_ref[...] = jnp.zeros((n_rows, D), F32)
        at = 0
        for ref in refs[:-1]:
            k = ref.shape[0]
            out_ref[at:at + k, :] = ref[...]
            at += k

    return pl.pallas_call(
        body, out_shape=jax.ShapeDtypeStruct((n_rows, D), F32), in_specs=[_ANY] * len(deps) + [_VMEM] * len(parts),
        out_specs=_VMEM, name=name)(*deps, *parts)


TS_MM = 512
TS_IN = 1024
TS_GW = 2048
TS_GW_WIDE = 1024
TS_MIX = 256


def _halved(a):
    n, r, cols = a.shape
    return a.reshape(n, 2, r // 2, cols)


SIBLING_IDS = (1, 2)


def _rs_swap(name, grads, after=()):
    lands = [lax.empty((N_CHIPS,) + g.shape[2:], g.dtype) for g in grads]
    sems, grads, lands, token = _xchg_start(name + "_swap", _plan_swap, N_CHIPS * len(grads), grads, lands, after,
                                            sibling_id=SIBLING_IDS[0])
    return name, sems, grads, lands, token


def _rs_scatter(handle, after, chip, ci):
    name, sems, grads, lands, _ = handle
    grads, from_sibling = _xchg_wait(name + "_swap_wait", _plan_swap, sems, grads, lands, after)
    c_arr = jnp.reshape(ci, (1,)).astype(jnp.int32)
    pair_sums = _add_halves(list(grads), list(from_sibling), c_arr, name + "_add_halves")
    lands = [lax.empty((N_CHIPS - 1,) + p.shape[1:], p.dtype) for p in pair_sums]
    sems, pair_sums, lands, token = _xchg_start(name + "_scatter", _plan_scatter, 3 * len(pair_sums), pair_sums, lands)
    return name, sems, pair_sums, lands, jnp.reshape(chip, (1,)).astype(jnp.int32), token


def _rs_share(handle, after):
    name, sems, pair_sums, lands, chip_idx, _ = handle
    pair_sums, received = _xchg_wait(name + "_scatter_wait", _plan_scatter, sems, pair_sums, lands, after)
    halves = _sum_chips(list(pair_sums), list(received), chip_idx, name + "_sum_chips")
    lands = [lax.empty(h.shape, h.dtype) for h in halves]
    sems, halves, lands, token = _xchg_start(name + "_share", _plan_share, len(halves), halves, lands,
                                             sibling_id=SIBLING_IDS[1])
    return name, sems, halves, lands, token


def _rs_end(handle, after):
    name, sems, halves, lands, _ = handle
    halves, others = _xchg_wait(name + "_share_wait", _plan_share, sems, halves, lands, after)
    return list(zip(halves, others))


def kernel(x, c, w_ada, b_ada, g_norm_mix, w_in, conv_a_w, conv_b_w, conv_b_bias, w_rg_a, b_rg_a, w_rg_x, b_rg_x, lru_lambda, w_out, g_norm_ffn, w_gate_up, w_down, g_norm_final, loss_target, m_w_ada, m_b_ada, m_g_norm_mix, m_w_in, m_conv_a_w, m_conv_b_w, m_conv_b_bias, m_w_rg_a, m_b_rg_a, m_w_rg_x, m_b_rg_x, m_lru_lambda, m_w_out, m_g_norm_ffn, m_w_gate_up, m_w_down, m_g_norm_final, v_w_ada, v_b_ada, v_g_norm_mix, v_w_in, v_conv_a_w, v_conv_b_w, v_conv_b_bias, v_w_rg_a, v_b_rg_a, v_w_rg_x, v_b_rg_x, v_lru_lambda, v_w_out, v_g_norm_ffn, v_w_gate_up, v_w_down, v_g_norm_final):
    xi, yi, ci = lax.axis_index("x"), lax.axis_index("y"), lax.axis_index("c")
    chip = 2 * xi + yi
    me = 2 * chip + ci
    n_ada = w_ada.shape[2]

    def widen(w):
        return jnp.pad(w, ((0, 0), (0, D - w.shape[1])))

    got = _allgather8(_pack_rows([c, widen(conv_a_w[0]), widen(conv_b_w[0])], 8, "pack_c_conv"), "gather_c_conv")
    c_all = got[:, 0, :]
    conv_full = got[::2, 1:8, :D // N_CHIPS].transpose(1, 0, 2).reshape(7, D)

    mod_part, c_act = _ada_fwd(c_all, w_ada[0], lax.dynamic_slice_in_dim(b_ada, chip * n_ada, n_ada, axis=1))
    mod_all = _allgather8(mod_part, "gather_mod")
    mod_mine = lax.dynamic_index_in_dim(mod_all, me, axis=1, keepdims=False)[::2].reshape(6, D)
    vecs = _pack_rows([mod_mine, g_norm_mix, g_norm_ffn, g_norm_final.reshape(1, D), conv_b_bias, b_rg_a, b_rg_x, lru_lambda,
                       conv_full], N_VEC, "pack_vecs")

    def rg_shard(w):
        return w[0].astype(BF16).reshape(2, HEADS * HB // N_CHIPS // 2, HB)

    shards = [w_in[0].astype(BF16).reshape(2, D // 2, C_IN), rg_shard(w_rg_a), rg_shard(w_rg_x),
              w_out[0].astype(BF16).reshape(2, D // N_CHIPS // 2, D), w_gate_up[0].astype(BF16).reshape(2, D // 2, C_GU),
              w_down[0].astype(BF16).reshape(2, D_FF // N_CHIPS // 2, D)]
    lands = [lax.dynamic_update_index_in_dim(lax.empty((N_CHIPS,) + s.shape, s.dtype), s, chip, 0) for s in shards]

    def send(name, first, last, after, chips):
        copies = [(k, j) for k, which in zip(range(first, last), chips, strict=True) for j in which]
        sems, srcs, zone, token = _xchg_start(name + "_ici", _plan_gather_ici(chips), len(copies), shards[first:last],
                                              lands[first:last], after)
        shards[first:last], lands[first:last] = srcs, zone
        return sems, copies, token

    def arrive(name, sent, first, last, after):
        sems, copies, _ = sent
        chips = [tuple(j for k, j in copies if k == want) for want in range(first, last)]
        ids = [copies.index((k, j)) for k, which in zip(range(first, last), chips) for j in which]
        srcs, zone = _xchg_wait(name + "_ici_wait", _plan_gather_ici(chips), sems, shards[first:last], lands[first:last], after,
                                ids)
        shards[first:last], lands[first:last] = srcs, zone

    def relay(name, first, last, which, sibling_id):
        plan = _plan_relay(which)
        sems, _, zone, token = _xchg_start(name + "_d2d", plan, len(which) * (last - first), [], lands[first:last],
                                           sibling_id=sibling_id)
        lands[first:last] = zone
        return name, plan, sems, first, last, token

    def relayed(handle, after):
        name, plan, sems, first, last, _ = handle
        lands[first:last] = _xchg_wait(name + "_d2d_wait", plan, sems, [], lands[first:last], after)[1]

    def to_blocks(v):
        return v.reshape(-1, TS_MIX // TIME_BLOCKS, TIME_BLOCKS, D).transpose(0, 2, 1, 3).reshape(v.shape)

    def from_blocks(v):
        return v.reshape(-1, TIME_BLOCKS, TS_MIX // TIME_BLOCKS, D).transpose(0, 2, 1, 3).reshape(v.shape)

    def chip_index(j):
        return jnp.reshape(chip ^ j, (1,)).astype(jnp.int32)

    def wg_in():
        return lands[0].reshape(N_CHIPS, D, C_IN)

    xs, target = to_blocks(x[0]), to_blocks(loss_target[0])
    sent_near = send("gather_in_near", 0, 1, [vecs], [(1, 2)])
    ts_in = min(TS_IN, xs.shape[0])
    h1, proj = _fwd_in_first(xs, vecs, wg_in(), chip_index(0), ts_in, deps=[sent_near[-1]])
    arrive("gather_in_near", sent_near, 0, 1, [proj])
    near = relay("gather_in_near", 0, 1, (1, 2), SIBLING_IDS[0])
    sent_rest = send("gather_rest", 0, 6, [near[-1]], [(3,)] + [(1, 2, 3)] * 5)
    relayed(near, [sent_rest[-1]])
    proj = _fwd_in_more(h1, wg_in(), proj, chip_index(1), ts_in, "fwd_in_y")
    proj = _fwd_in_more(h1, wg_in(), proj, chip_index(2), ts_in, "fwd_in_x")
    arrive("gather_in_far", sent_rest, 0, 1, [proj])
    far = relay("gather_in_far", 0, 1, (3,), SIBLING_IDS[1])
    arrive("gather_mix", sent_rest, 1, 4, [far[-1]])
    relayed(far, [far[-1]])
    mix = relay("gather_mix", 1, 4, (1, 2, 3), SIBLING_IDS[0])
    proj = _fwd_in_more(h1, wg_in(), proj, chip_index(3), ts_in, "fwd_in_xy", deps=[mix[-1]])
    relayed(mix, [proj])
    wg_rga, wg_rgx, wg_out = lands[1:4]
    wg_out = wg_out.reshape(D, D)

    def rg_full(wg):
        return wg.reshape(N_CHIPS, HEADS, HB // N_CHIPS, HB).transpose(1, 0, 2, 3).reshape(HEADS, HB, HB)

    wg_rga, wg_rgx = rg_full(wg_rga), rg_full(wg_rgx)

    x1, merged, z1, kept, decay = _fwd_mix(proj, xs, vecs, wg_rga, wg_rgx, wg_out, TS_MIX)
    arrive("gather_ffn", sent_rest, 4, 6, [x1])
    ffn = relay("gather_ffn", 4, 6, (1, 2, 3), SIBLING_IDS[1])
    relayed(ffn, [ffn[-1]])
    wg_gu, wg_dn = lands[4:6]
    wg_gu, wg_dn = wg_gu.reshape(N_CHIPS, D, C_GU), wg_dn.reshape(D_FF, D)
    dx1, h2, act, dz2, dgu, sm_ffn = _ffn_loss(x1, target, vecs, wg_gu, wg_dn, TS_MIX)

    def rg_chunks(dw):
        return _halved(dw.reshape(HEADS, N_CHIPS, HB // N_CHIPS, HB).transpose(1, 0, 2, 3).reshape(N_CHIPS, HB, HB).astype(BF16))

    ts_gw = min(TS_GW, xs.shape[0])
    g_dn = _grad_w(act, dz2, 1, min(TS_GW_WIDE, xs.shape[0]), "grad_w_down")
    g_gu = _grad_w(h2, dgu, N_CHIPS, ts_gw, "grad_w_gate_up")
    rs_b = _rs_swap("rs_b", [_halved(g_gu), _halved(g_dn.reshape(N_CHIPS, D_FF // N_CHIPS, D))])
    dproj, sm_mix, dw_rga, dw_rgx, dw_out = _bwd_mix(dx1, z1, merged, proj, kept, decay, vecs, wg_rga, wg_rgx, wg_out, sm_ffn, TS_MIX,
                                                     deps=[rs_b[-1]])
    rs_b = _rs_scatter(rs_b, [dproj], chip, ci)
    g_in = _grad_w(h1, dproj, N_CHIPS, ts_gw, "grad_w_in", deps=[rs_b[-1]])
    rs_b = _rs_share(rs_b, [g_in])
    rs_a = _rs_swap("rs_a", [_halved(g_in), rg_chunks(dw_rga), rg_chunks(dw_rgx),
                             _halved(dw_out.astype(BF16).reshape(N_CHIPS, D // N_CHIPS, D))], after=[rs_b[-1]])

    c_arr = jnp.reshape(ci, (1,)).astype(jnp.int32)

    def step_halves(name, items, deps=()):
        two_d = lambda a: a.reshape(-1, a.shape[-1])
        sets = [(two_d(w), halves[0], halves[1], two_d(m), two_d(v)) for w, halves, m, v in items.values()]
        for (n, (w, _, _, _)), out in zip(items.items(), _adamw_halves(sets, c_arr, name, deps)):
            res[n] = tuple(a.reshape(w.shape) for a in out)

    def shard_cols(row_block):
        return lax.dynamic_slice_in_dim(row_block, chip * (D // N_CHIPS), D // N_CHIPS, axis=1)

    gw_gu, gw_dn = _rs_end(rs_b, [rs_a[-1]])
    res = {}
    step_halves("adamw_ffn", {"w_gate_up": (w_gate_up, gw_gu, m_w_gate_up, v_w_gate_up),
                              "w_down": (w_down, gw_dn, m_w_down, v_w_down)}, [rs_a[-1]])
    rs_a = _rs_scatter(rs_a, [res["w_gate_up"][1], res["w_down"][1]], chip, ci)
    grad_x, sm_in = _bwd_in(dproj, xs, dx1, vecs, wg_in(), sm_mix, TS_MM, deps=[rs_a[-1]])
    small_zone = lax.dynamic_update_index_in_dim(lax.empty((N_DEV,) + sm_in.shape, F32), sm_in, me, 0)
    small_sems, small_src, small_zone, small_token = _xchg_start("gather_small", _plan_gather8, N_DEV - 1, [sm_in], [small_zone])
    rs_a = _rs_share(rs_a, [grad_x, small_token])

    small, per_dev = _sum_small(_xchg_wait("gather_small_wait", _plan_gather8, small_sems, small_src, small_zone, [rs_a[-1]])[1][0])
    dmod_all = per_dev[:, 0:6, :].reshape(N_DEV, 6 * D)
    grad_b_ada = small[0:6].reshape(1, 6 * D)
    res["w_ada"] = tuple(a[None] for a in _adamw_ada(w_ada[0], c_act, lax.dynamic_slice_in_dim(dmod_all, chip * n_ada, n_ada, axis=1),
                                                     m_w_ada[0], v_w_ada[0]))
    small_sets = {
        "b_ada": (b_ada.reshape(6, D), grad_b_ada.reshape(6, D), m_b_ada.reshape(6, D), v_b_ada.reshape(6, D)),
        "g_norm_mix": (g_norm_mix, small[G_GMIX:G_GMIX + 1], m_g_norm_mix, v_g_norm_mix),
        "conv_a_w": (conv_a_w[0], shard_cols(small[G_WA0:G_WA0 + 3]), m_conv_a_w[0], v_conv_a_w[0]),
        "conv_b_w": (conv_b_w[0], shard_cols(small[G_WB0:G_WB0 + 4]), m_conv_b_w[0], v_conv_b_w[0]),
        "conv_b_bias": (conv_b_bias, small[G_CBB:G_CBB + 1], m_conv_b_bias, v_conv_b_bias),
        "b_rg_a": (b_rg_a, small[G_BA:G_BA + 1], m_b_rg_a, v_b_rg_a),
        "b_rg_x": (b_rg_x, small[G_BX:G_BX + 1], m_b_rg_x, v_b_rg_x),
        "lru_lambda": (lru_lambda, small[G_LAM:G_LAM + 1], m_lru_lambda, v_lru_lambda),
        "g_norm_ffn": (g_norm_ffn, small[G_GFFN:G_GFFN + 1], m_g_norm_ffn, v_g_norm_ffn),
        "g_norm_final": (g_norm_final.reshape(1, D), small[G_GFIN:G_GFIN + 1], m_g_norm_final.reshape(1, D),
                         v_g_norm_final.reshape(1, D)),
    }
    stepped = _adamw_small(list(small_sets.values()), "adamw_small")
    for (n, (w_, g_, _, _)), (d_, nm_, nv_) in zip(small_sets.items(), stepped):
        shape = (1,) + w_.shape if n.startswith("conv_") and n != "conv_b_bias" else w_.shape
        res[n] = tuple(a.reshape(shape) for a in (g_, d_, nm_, nv_))
    gw_in, gw_rga, gw_rgx, gw_out = _rs_end(rs_a, [res[n][1] for n in res])
    step_halves("adamw_mix", {"w_in": (w_in, gw_in, m_w_in, v_w_in), "w_rg_a": (w_rg_a, gw_rga, m_w_rg_a, v_w_rg_a),
                              "w_rg_x": (w_rg_x, gw_rgx, m_w_rg_x, v_w_rg_x), "w_out": (w_out, gw_out, m_w_out, v_w_out)})
    res["b_ada"] = tuple(a.reshape(1, 6 * D) for a in res["b_ada"])
    res["g_norm_final"] = tuple(a.reshape(D) for a in res["g_norm_final"])
    names = ["w_ada", "b_ada", "g_norm_mix", "w_in", "conv_a_w", "conv_b_w", "conv_b_bias", "w_rg_a", "b_rg_a", "w_rg_x",
             "b_rg_x", "lru_lambda", "w_out", "g_norm_ffn", "w_gate_up", "w_down", "g_norm_final"]
    loss = jnp.sum(small[G_LOSS])
    return (loss, from_blocks(grad_x)[None], *[res[n][0] for n in names], *[res[n][1] for n in names],
            *[res[n][2] for n in names], *[res[n][3] for n in names])
```

```python
import functools

import jax
import jax.numpy as jnp
from jax import lax
from jax.experimental import pallas as pl
from jax.experimental.pallas import tpu as pltpu

F32 = jnp.float32
BF16 = jnp.bfloat16
MESH = pl.DeviceIdType.MESH

D = 1024
N_CHIPS = 4
N_DEV = 8
D_IN = 7 * D
C_IN = D_IN // N_CHIPS
D_FF = 2816
C_GU = 2 * D_FF // N_CHIPS
HEADS = 4
HB = D // HEADS
EPS = 1e-6
LRU_C = 8.0
ADAM_LR, ADAM_B1, ADAM_B2, ADAM_EPS, ADAM_WD, ADAM_STEP = 0.001, 0.9, 0.999, 1e-08, 0.01, 10
VMEM_LIMIT = 56 << 20

(V_SH1, V_SC1, V_GT1, V_SH2, V_SC2, V_GT2, V_GMIX, V_GFFN, V_GFIN, V_CBB, V_BA, V_BX, V_LAM,
 V_WA0, V_WA1, V_WA2, V_WB0, V_WB1, V_WB2, V_WB3) = range(20)
N_VEC = 24
(G_SH1, G_SC1, G_GT1, G_SH2, G_SC2, G_GT2, G_GMIX, G_CBB, G_BA, G_BX, G_LAM, G_GFFN, G_GFIN,
 G_WA0, G_WA1, G_WA2, G_WB0, G_WB1, G_WB2, G_WB3, G_LOSS) = range(21)
N_SMALL = 24

_VMEM = pl.BlockSpec(memory_space=pltpu.VMEM)
_ANY = pl.BlockSpec(memory_space=pl.ANY)


def _cparams(n_grid=1):
    return pltpu.CompilerParams(dimension_semantics=("arbitrary",) * n_grid, vmem_limit_bytes=VMEM_LIMIT)


def _after(deps, body):
    n = len(deps)
    return lambda *refs: body(*refs[n:])


def _rms(x):
    rstd = lax.rsqrt(jnp.mean(x * x, axis=-1, keepdims=True) + EPS)
    return x * rstd, rstd


def _rms_bwd(dxhat, xhat, rstd):
    return rstd * (dxhat - xhat * jnp.mean(dxhat * xhat, axis=-1, keepdims=True))


def _rowsum(v):
    return jnp.sum(v, axis=0, keepdims=True)


def _dot(a, b):
    return jnp.dot(a, b, preferred_element_type=F32)


def _dot_nt(a, b):
    return lax.dot_general(a, b, (((1,), (1,)), ((), ())), preferred_element_type=F32)


def _dot_tn(a, b):
    return lax.dot_general(a, b, (((0,), (0,)), ((), ())), preferred_element_type=F32)


def _gelu(x):
    k, c = 0.7978845608028654, 0.044715
    t = jnp.tanh(k * (x + c * x * x * x))
    return 0.5 * x * (1.0 + t), 0.5 * (1.0 + t) + 0.5 * x * (1.0 - t * t) * k * (1.0 + 3.0 * c * x * x)


def _log_sigmoid(lam):
    return jnp.minimum(lam, 0.0) - jnp.log1p(jnp.exp(-jnp.abs(lam)))


def _lru_gates(u, wa_ref, wx_ref, v_ref, row0):
    ub = u.astype(BF16)
    pre_a = jnp.concatenate([_dot(ub[:, h * HB:(h + 1) * HB], wa_ref[h]) for h in range(HEADS)], axis=1)
    pre_x = jnp.concatenate([_dot(ub[:, h * HB:(h + 1) * HB], wx_ref[h]) for h in range(HEADS)], axis=1)
    r = jax.nn.sigmoid(pre_a + v_ref[V_BA:V_BA + 1, :])
    ig = jax.nn.sigmoid(pre_x + v_ref[V_BX:V_BX + 1, :])
    log_a = LRU_C * r * _log_sigmoid(v_ref[V_LAM:V_LAM + 1, :])
    a = jnp.exp(log_a)
    x2 = 2.0 * log_a
    m2 = jnp.where(x2 > -0.03, -x2 * (1.0 + x2 * (0.5 + x2 * (1.0 / 6.0 + x2 * (1.0 / 24.0)))), 1.0 - a * a)
    mult = jnp.where(row0, 1.0, jnp.sqrt(jnp.maximum(m2, 0.0)))
    return r, ig, a, mult


TIME_BLOCKS = 8
N_KEPT = 10


def _late_blocks(v, buf, g, halo=None):
    n = buf.shape[0]
    out = []
    for idx in range(n):
        k = TIME_BLOCKS - n + idx
        buf[idx, 8:g + 8, :] = v[k * g:(k + 1) * g]
        if halo is not None:
            buf[idx, 7:8, :] = halo[idx]
        out.append(buf[idx, pl.ds(7, g), :])
        if halo is None:
            buf[idx, 7:8, :] = buf[idx, g + 7:g + 8, :]
    return out


def _earlier(v, s, late, g):
    return jnp.concatenate(late[len(late) - s:] + [v[0:(TIME_BLOCKS - s) * g]], axis=0)


def _early_blocks(v, buf, g):
    out = []
    for k in range(buf.shape[0]):
        buf[k, 0:g, :] = v[k * g:(k + 1) * g]
        out.append(buf[k, pl.ds(1, g), :])
        buf[k, g:g + 1, :] = buf[k, 0:1, :]
    return out


def _later(v, s, early, g):
    return jnp.concatenate([v[s * g:]] + early[0:s], axis=0)


def _fwd_in_first(x, vecs, w_in_g, q_idx, ts, deps=()):
    s = x.shape[0]

    def body(q_ref, x_ref, v_ref, w_ref, h1_ref, proj_ref):
        xhat, _ = _rms(x_ref[...])
        h = xhat * v_ref[V_GMIX:V_GMIX + 1, :] * (1.0 + v_ref[V_SC1:V_SC1 + 1, :]) + v_ref[V_SH1:V_SH1 + 1, :]
        hb = h.astype(BF16)
        h1_ref[...] = hb
        proj_ref[...] = _dot(hb, w_ref[...]).astype(BF16)

    return pl.pallas_call(
        lambda q_ref, *refs: body(q_ref, *refs[len(deps):]),
        grid_spec=pltpu.PrefetchScalarGridSpec(
            num_scalar_prefetch=1, grid=(s // ts,),
            in_specs=[_ANY] * len(deps) + [pl.BlockSpec((ts, D), lambda i, q: (i, 0)), _VMEM,
                                           pl.BlockSpec((None, D, C_IN), lambda i, q: (q[0], 0, 0))],
            out_specs=[pl.BlockSpec((ts, D), lambda i, q: (i, 0)), pl.BlockSpec((ts, C_IN), lambda i, q: (i, q[0]))]),
        out_shape=(jax.ShapeDtypeStruct((s, D), BF16), jax.ShapeDtypeStruct((s, D_IN), BF16)),
        compiler_params=_cparams(), name="fwd_in_own")(q_idx, *deps, x, vecs, w_in_g)


def _fwd_in_more(h1, w_in_g, proj, q_idx, ts, name, deps=()):
    s = h1.shape[0]

    def body(q_ref, h1_ref, w_ref, proj_in_ref, proj_ref):
        proj_ref[...] = _dot(h1_ref[...], w_ref[...]).astype(BF16)

    return pl.pallas_call(
        lambda q_ref, *refs: body(q_ref, *refs[len(deps):]),
        grid_spec=pltpu.PrefetchScalarGridSpec(
            num_scalar_prefetch=1, grid=(s // ts,),
            in_specs=[_ANY] * len(deps) + [pl.BlockSpec((ts, D), lambda i, q: (i, 0)),
                                           pl.BlockSpec((None, D, C_IN), lambda i, q: (q[0], 0, 0)), _ANY],
            out_specs=pl.BlockSpec((ts, C_IN), lambda i, q: (i, q[0]))),
        out_shape=jax.ShapeDtypeStruct((s, D_IN), BF16), input_output_aliases={len(deps) + 3: 0},
        compiler_params=_cparams(), name=name)(q_idx, *deps, h1, w_in_g, proj)


def _fwd_mix(proj, x, vecs, w_rga, w_rgx, w_out, ts, deps=()):
    s = x.shape[0]
    g = ts // TIME_BLOCKS

    def body(proj_ref, x_ref, v_ref, wa_ref, wx_ref, wo_ref, x1_ref, mg_ref, z1_ref, kept_ref, decay_ref,
             ua_buf, rx_buf, p_buf, q_buf, c_buf, hcarry):
        i = pl.program_id(0)

        @pl.when(i == 0)
        def _():
            ua_buf[...] = jnp.zeros(ua_buf.shape, F32)
            rx_buf[...] = jnp.zeros(rx_buf.shape, F32)
            hcarry[...] = jnp.zeros((8, D), F32)

        def seg(j):
            return proj_ref[:, j * D:(j + 1) * D].astype(F32)

        def vrow(j):
            return v_ref[j:j + 1, :]

        cb, cc, cx, rx, rg, ga, gb = (seg(j) for j in range(7))
        ua = cc * cx
        ua_late = _late_blocks(ua, ua_buf, g)
        rx_late = _late_blocks(rx, rx_buf, g)
        va = vrow(V_WA2) * ua + vrow(V_WA1) * _earlier(ua, 1, ua_late, g) + vrow(V_WA0) * _earlier(ua, 2, ua_late, g)
        u = (vrow(V_WB3) * rx + vrow(V_WB2) * _earlier(rx, 1, rx_late, g) + vrow(V_WB1) * _earlier(rx, 2, rx_late, g)
             + vrow(V_WB0) * _earlier(rx, 3, rx_late, g) + vrow(V_CBB))

        rows = lax.broadcasted_iota(jnp.int32, (ts, D), 0)
        row0 = jnp.logical_and(rows == 0, i == 0)
        r, ig, a, mult = _lru_gates(u, wa_ref, wx_ref, v_ref, row0)
        decay_ref[...] = a
        bx = mult * (ig * u)

        prods, sums = [a[0:g]], [bx[0:g]]
        for k in range(1, TIME_BLOCKS):
            ak = a[k * g:(k + 1) * g]
            sums.append(ak * sums[-1] + bx[k * g:(k + 1) * g])
            prods.append(ak * prods[-1])
        p_buf[...] = prods[-1]
        q_buf[...] = sums[-1]
        state = hcarry[0:1, :]
        for j in range(g):
            c_buf[j:j + 1, :] = state
            state = p_buf[j:j + 1, :] * state + q_buf[j:j + 1, :]
        hcarry[0:1, :] = state
        entering = c_buf[...]
        h = jnp.concatenate([sums[k] + prods[k] * entering for k in range(TIME_BLOCKS)], axis=0)

        gel, dgel = _gelu(rg)
        sga = jax.nn.sigmoid(ga)
        sgb = jax.nn.sigmoid(gb)
        for j, keep in enumerate((va, r, ig, sga, sgb, gel, dgel, mult, u, h)):
            kept_ref[:, j * D:(j + 1) * D] = keep.astype(BF16)
        merged = (sga * (cb * va) + sgb * (h * gel)).astype(BF16)
        mg_ref[...] = merged
        z1 = _dot(merged, wo_ref[...])
        z1_ref[...] = z1.astype(BF16)
        x1_ref[...] = x_ref[...] + vrow(V_GT1) * z1

    row = lambda i: (i, 0)
    return pl.pallas_call(
        _after(deps, body), grid=(s // ts,),
        out_shape=(jax.ShapeDtypeStruct((s, D), F32), jax.ShapeDtypeStruct((s, D), BF16), jax.ShapeDtypeStruct((s, D), BF16),
                   jax.ShapeDtypeStruct((s, N_KEPT * D), BF16), jax.ShapeDtypeStruct((s, D), F32)),
        in_specs=[_ANY] * len(deps) + [pl.BlockSpec((ts, D_IN), row), pl.BlockSpec((ts, D), row), _VMEM, _VMEM, _VMEM, _VMEM],
        out_specs=[pl.BlockSpec((ts, D), row)] * 3 + [pl.BlockSpec((ts, N_KEPT * D), row), pl.BlockSpec((ts, D), row)],
        scratch_shapes=[pltpu.VMEM((2, g + 8, D), F32), pltpu.VMEM((3, g + 8, D), F32), pltpu.VMEM((g, D), F32),
                        pltpu.VMEM((g, D), F32), pltpu.VMEM((g, D), F32), pltpu.VMEM((8, D), F32)],
        compiler_params=_cparams(), name="fwd_mix")(*deps, proj, x, vecs, w_rga, w_rgx, w_out)


def _ffn_loss(x1, target, vecs, w_gu_g, w_dn, ts):
    s = x1.shape[0]

    def body(x1_ref, t_ref, v_ref, wgu_ref, wdn_ref, dx1_ref, h2_ref, act_ref, dz2_ref, dgu_ref, sm_ref):
        @pl.when(pl.program_id(0) == 0)
        def _():
            sm_ref[...] = jnp.zeros((N_SMALL, D), F32)

        def vrow(j):
            return v_ref[j:j + 1, :]

        n_sub = 1
        rows = [slice(k * (ts // n_sub), (k + 1) * (ts // n_sub)) for k in range(n_sub)]
        subs = [dict(r=r, sums={}) for r in rows]

        def stage_norm(t):
            t["x1"] = x1_ref[t["r"], :]
            t["xh1"], t["rstd1"] = _rms(t["x1"])
            t["n2"] = t["xh1"] * vrow(V_GFFN)
            t["h2"] = (t["n2"] * (1.0 + vrow(V_SC2)) + vrow(V_SH2)).astype(BF16)
            h2_ref[t["r"], :] = t["h2"]

        def stage_up(t):
            h2 = t["h2"]
            g = jnp.concatenate([_dot(h2, wgu_ref[0]), _dot(h2, wgu_ref[1])], axis=1)
            t["up"] = jnp.concatenate([_dot(h2, wgu_ref[2]), _dot(h2, wgu_ref[3])], axis=1)
            t["g"] = g
            t["sg"] = jax.nn.sigmoid(g)
            t["silu"] = g * t["sg"]
            t["act"] = (t["silu"] * t["up"]).astype(BF16)
            act_ref[t["r"], :] = t["act"]

        def stage_down_loss(t):
            z2 = _dot(t["act"], wdn_ref[...])
            x2 = t["x1"] + vrow(V_GT2) * z2
            xh2, rstd2 = _rms(x2)
            err = xh2 * vrow(V_GFIN) - t_ref[t["r"], :]
            t["sums"][G_LOSS] = _rowsum((0.5 / D) * err * err)
            dy = err * (1.0 / D)
            t["sums"][G_GFIN] = _rowsum(dy * xh2)
            t["dx2"] = _rms_bwd(dy * vrow(V_GFIN), xh2, rstd2)
            t["sums"][G_GT2] = _rowsum(t["dx2"] * z2)
            t["dz2"] = (vrow(V_GT2) * t["dx2"]).astype(BF16)
            dz2_ref[t["r"], :] = t["dz2"]

        def stage_back_act(t):
            dact = _dot_nt(t["dz2"], wdn_ref[...])
            g, sg = t["g"], t["sg"]
            t["dgate"] = (dact * t["up"] * (sg * (1.0 + g * (1.0 - sg)))).astype(BF16)
            t["dup"] = (dact * t["silu"]).astype(BF16)
            dgu_ref[t["r"], 0:D_FF] = t["dgate"]
            dgu_ref[t["r"], D_FF:2 * D_FF] = t["dup"]

        def stage_back_norm(t):
            dgate, dup = t["dgate"], t["dup"]
            dh2 = (_dot_nt(dgate[:, 0:C_GU], wgu_ref[0]) + _dot_nt(dgate[:, C_GU:2 * C_GU], wgu_ref[1])
                   + _dot_nt(dup[:, 0:C_GU], wgu_ref[2]) + _dot_nt(dup[:, C_GU:2 * C_GU], wgu_ref[3]))
            t["sums"][G_SH2] = _rowsum(dh2)
            t["sums"][G_SC2] = _rowsum(dh2 * t["n2"])
            dn2 = dh2 * (1.0 + vrow(V_SC2))
            t["sums"][G_GFFN] = _rowsum(dn2 * t["xh1"])
            dx1_ref[t["r"], :] = t["dx2"] + _rms_bwd(dn2 * vrow(V_GFFN), t["xh1"], t["rstd1"])

        for stage in (stage_norm, stage_up, stage_down_loss, stage_back_act, stage_back_norm):
            for t in subs:
                stage(t)
        for j in subs[0]["sums"]:
            total = subs[0]["sums"][j]
            for t in subs[1:]:
                total = total + t["sums"][j]
            sm_ref[j:j + 1, :] += total

    row = lambda i: (i, 0)
    return pl.pallas_call(
        body, grid=(s // ts,),
        out_shape=(jax.ShapeDtypeStruct((s, D), F32), jax.ShapeDtypeStruct((s, D), BF16), jax.ShapeDtypeStruct((s, D_FF), BF16),
                   jax.ShapeDtypeStruct((s, D), BF16), jax.ShapeDtypeStruct((s, 2 * D_FF), BF16),
                   jax.ShapeDtypeStruct((N_SMALL, D), F32)),
        in_specs=[pl.BlockSpec((ts, D), row), pl.BlockSpec((ts, D), row), _VMEM, _VMEM, _VMEM],
        out_specs=[pl.BlockSpec((ts, D), row), pl.BlockSpec((ts, D), row), pl.BlockSpec((ts, D_FF), row),
                   pl.BlockSpec((ts, D), row), pl.BlockSpec((ts, 2 * D_FF), row), pl.BlockSpec((N_SMALL, D), lambda i: (0, 0))],
        compiler_params=_cparams(), name="ffn_loss")(x1, target, vecs, w_gu_g, w_dn)


def _bwd_mix(dx1, z1, merged, proj, kept, decay, vecs, w_rga, w_rgx, w_out, small, ts, deps=()):
    s = dx1.shape[0]
    nt = s // ts
    g = ts // TIME_BLOCKS
    assert g % 16 == 0

    def body(dx1_ref, z1_ref, mg_ref, proj_ref, kept_ref, decay_ref, hh_ref, v_ref, wa_ref, wx_ref,
             wo_ref, sm0_ref, dproj_ref, sm_ref, dwa_ref, dwx_ref, dwo_ref,
             h_buf, a_buf, dva_buf, du_buf, p_buf, q_buf, c_buf, lcarry):
        i = pl.program_id(0)
        first_tile = i == nt - 1

        @pl.when(i == 0)
        def _():
            a_buf[...] = jnp.zeros(a_buf.shape, F32)
            dva_buf[...] = jnp.zeros(dva_buf.shape, F32)
            du_buf[...] = jnp.zeros(du_buf.shape, F32)
            lcarry[...] = jnp.zeros((8, D), F32)
            sm_ref[...] = sm0_ref[...]
            dwa_ref[...] = jnp.zeros((HEADS, HB, HB), F32)
            dwx_ref[...] = jnp.zeros((HEADS, HB, HB), F32)
            dwo_ref[...] = jnp.zeros((D, D), F32)

        def seg(j):
            return proj_ref[:, j * D:(j + 1) * D].astype(F32)

        def vrow(j):
            return v_ref[j:j + 1, :]

        def acc(j, val):
            sm_ref[j:j + 1, :] += _rowsum(val)

        cb, cc, cx, rx = (seg(j) for j in range(4))
        ua = cc * cx
        va, r, ig, sga, sgb, gel, dgel, mult, u, h = (kept_ref[:, j * D:(j + 1) * D].astype(F32) for j in range(N_KEPT))
        a = decay_ref[...]
        rows = lax.broadcasted_iota(jnp.int32, (ts, D), 0)
        row0 = jnp.logical_and(rows == 0, first_tile)

        dx1 = dx1_ref[...]
        acc(G_GT1, dx1 * z1_ref[...].astype(F32))
        dz1 = (vrow(V_GT1) * dx1).astype(BF16)
        dwo_ref[...] += _dot_tn(mg_ref[...], dz1)
        dmg = _dot_nt(dz1, wo_ref[...])
        dya = dmg * sga
        dyb = dmg * sgb
        dproj_ref[:, 5 * D:6 * D] = (dya * (cb * va) * (1.0 - sga)).astype(BF16)
        dproj_ref[:, 6 * D:7 * D] = (dyb * (h * gel) * (1.0 - sgb)).astype(BF16)

        dproj_ref[:, 0:D] = (dya * va).astype(BF16)
        dva = dya * cb
        dva_early = _early_blocks(dva, dva_buf, g)
        dva1 = _later(dva, 1, dva_early, g)
        dva2 = _later(dva, 2, dva_early, g)
        dua = vrow(V_WA2) * dva + vrow(V_WA1) * dva1 + vrow(V_WA0) * dva2
        acc(G_WA2, ua * dva)
        acc(G_WA1, ua * dva1)
        acc(G_WA0, ua * dva2)
        dproj_ref[:, D:2 * D] = (dua * cx).astype(BF16)
        dproj_ref[:, 2 * D:3 * D] = (dua * cc).astype(BF16)

        dproj_ref[:, 4 * D:5 * D] = (dyb * h * dgel).astype(BF16)
        a_next = _later(a, 1, _early_blocks(a, a_buf, g), g)
        dh = dyb * gel
        last = TIME_BLOCKS - 1
        prods, sums = {last: a_next[last * g:]}, {last: dh[last * g:]}
        for k in range(last - 1, -1, -1):
            ak = a_next[k * g:(k + 1) * g]
            sums[k] = dh[k * g:(k + 1) * g] + ak * sums[k + 1]
            prods[k] = ak * prods[k + 1]
        p_buf[...] = prods[0]
        q_buf[...] = sums[0]
        state = lcarry[0:1, :]
        for j in range(g - 1, -1, -1):
            c_buf[j:j + 1, :] = state
            state = q_buf[j:j + 1, :] + p_buf[j:j + 1, :] * state
        lcarry[0:1, :] = state
        entering = c_buf[...]
        lam = jnp.concatenate([sums[k] + prods[k] * entering for k in range(TIME_BLOCKS)], axis=0)

        last = lax.broadcasted_iota(jnp.int32, hh_ref.shape, 0) == hh_ref.shape[0] - 1
        h_halo = [jnp.where(first_tile, 0.0, jnp.sum(jnp.where(last, hh_ref[...].astype(F32), 0.0), axis=0, keepdims=True))]
        da = lam * _earlier(h, 1, _late_blocks(h, h_buf, g, h_halo), g)
        dmult = jnp.where(row0, 0.0, lam * (ig * u))
        di = lam * mult * u
        du = lam * mult * ig
        dlog_a = da * a - dmult * (a * a) / mult
        lam_p = vrow(V_LAM)
        dr = dlog_a * (LRU_C * _log_sigmoid(lam_p))
        sm_ref[G_LAM:G_LAM + 1, :] += _rowsum(dlog_a * r) * (LRU_C * jax.nn.sigmoid(-lam_p))
        dpa = dr * r * (1.0 - r)
        dpx = di * ig * (1.0 - ig)
        acc(G_BA, dpa)
        acc(G_BX, dpx)
        dpab = dpa.astype(BF16)
        dpxb = dpx.astype(BF16)
        ub = u.astype(BF16)
        back = []
        for hd in range(HEADS):
            cols = slice(hd * HB, (hd + 1) * HB)
            back.append(_dot_nt(dpab[:, cols], wa_ref[hd]) + _dot_nt(dpxb[:, cols], wx_ref[hd]))
            dwa_ref[hd] += _dot_tn(ub[:, cols], dpab[:, cols])
            dwx_ref[hd] += _dot_tn(ub[:, cols], dpxb[:, cols])
        du = du + jnp.concatenate(back, axis=1)

        acc(G_CBB, du)
        du_early = _early_blocks(du, du_buf, g)
        du1 = _later(du, 1, du_early, g)
        du2 = _later(du, 2, du_early, g)
        du3 = _later(du, 3, du_early, g)
        dproj_ref[:, 3 * D:4 * D] = (vrow(V_WB3) * du + vrow(V_WB2) * du1 + vrow(V_WB1) * du2 + vrow(V_WB0) * du3).astype(BF16)
        acc(G_WB3, rx * du)
        acc(G_WB2, rx * du1)
        acc(G_WB1, rx * du2)
        acc(G_WB0, rx * du3)

    rev = lambda i: (nt - 1 - i, 0)
    h_halo16 = lambda i: (jnp.maximum((nt - 1 - i) * (ts // 16) - 1, 0), N_KEPT - 1)
    const2 = lambda i: (0, 0)
    const3 = lambda i: (0, 0, 0)
    return pl.pallas_call(
        _after(deps, body), grid=(nt,),
        out_shape=(jax.ShapeDtypeStruct((s, D_IN), BF16), jax.ShapeDtypeStruct((N_SMALL, D), F32),
                   jax.ShapeDtypeStruct((HEADS, HB, HB), F32), jax.ShapeDtypeStruct((HEADS, HB, HB), F32),
                   jax.ShapeDtypeStruct((D, D), F32)),
        in_specs=[_ANY] * len(deps) + [pl.BlockSpec((ts, D), rev), pl.BlockSpec((ts, D), rev), pl.BlockSpec((ts, D), rev),
                  pl.BlockSpec((ts, 4 * D), rev), pl.BlockSpec((ts, N_KEPT * D), rev), pl.BlockSpec((ts, D), rev),
                  pl.BlockSpec((16, D), h_halo16), _VMEM, _VMEM, _VMEM, _VMEM, _VMEM],
        out_specs=[pl.BlockSpec((ts, D_IN), rev), pl.BlockSpec((N_SMALL, D), const2),
                   pl.BlockSpec((HEADS, HB, HB), const3), pl.BlockSpec((HEADS, HB, HB), const3), pl.BlockSpec((D, D), const2)],
        scratch_shapes=[pltpu.VMEM((1, g + 8, D), F32), pltpu.VMEM((1, g + 8, D), F32),
                        pltpu.VMEM((2, g + 8, D), F32), pltpu.VMEM((3, g + 8, D), F32), pltpu.VMEM((g, D), F32),
                        pltpu.VMEM((g, D), F32), pltpu.VMEM((g, D), F32), pltpu.VMEM((8, D), F32)],
        compiler_params=_cparams(), name="bwd_mix")(*deps, dx1, z1, merged, proj, kept, decay, kept, vecs, w_rga,
                                                    w_rgx, w_out, small)


def _bwd_in(dproj, x, dx1, vecs, w_in_g, small, ts, deps=()):
    s = x.shape[0]

    def body(dp_ref, x_ref, dx1_ref, v_ref, w_ref, sm0_ref, gx_ref, sm_ref):
        @pl.when(pl.program_id(0) == 0)
        def _():
            sm_ref[...] = sm0_ref[...]

        def vrow(j):
            return v_ref[j:j + 1, :]

        dh1 = _dot_nt(dp_ref[:, 0:C_IN], w_ref[0])
        for k in range(1, N_CHIPS):
            dh1 += _dot_nt(dp_ref[:, k * C_IN:(k + 1) * C_IN], w_ref[k])
        xh, rstd = _rms(x_ref[...])
        sm_ref[G_SH1:G_SH1 + 1, :] += _rowsum(dh1)
        sm_ref[G_SC1:G_SC1 + 1, :] += _rowsum(dh1 * (xh * vrow(V_GMIX)))
        dn1 = dh1 * (1.0 + vrow(V_SC1))
        sm_ref[G_GMIX:G_GMIX + 1, :] += _rowsum(dn1 * xh)
        gx_ref[...] = dx1_ref[...] + _rms_bwd(dn1 * vrow(V_GMIX), xh, rstd)

    row = lambda i: (i, 0)
    return pl.pallas_call(
        _after(deps, body), grid=(s // ts,),
        out_shape=(jax.ShapeDtypeStruct((s, D), F32), jax.ShapeDtypeStruct((N_SMALL, D), F32)),
        in_specs=[_ANY] * len(deps) + [pl.BlockSpec((ts, D_IN), row), pl.BlockSpec((ts, D), row), pl.BlockSpec((ts, D), row),
                                       _VMEM, _VMEM, _VMEM],
        out_specs=[pl.BlockSpec((ts, D), row), pl.BlockSpec((N_SMALL, D), lambda i: (0, 0))],
        compiler_params=_cparams(), name="bwd_in")(*deps, dproj, x, dx1, vecs, w_in_g, small)


def _grad_w(a, b, n_col_blocks, ts, name, deps=()):
    s, m = a.shape
    tn = b.shape[1] // n_col_blocks
    n_steps = s // ts

    def body(a_ref, b_ref, o_ref, acc_ref):
        k = pl.program_id(1)

        @pl.when(k == 0)
        def _():
            acc_ref[...] = jnp.zeros((m, tn), F32)

        acc_ref[...] += _dot_tn(a_ref[...], b_ref[...])

        @pl.when(k == n_steps - 1)
        def _():
            o_ref[...] = acc_ref[...].astype(BF16)

    return pl.pallas_call(
        _after(deps, body), grid=(n_col_blocks, n_steps),
        out_shape=jax.ShapeDtypeStruct((n_col_blocks, m, tn), BF16),
        in_specs=[_ANY] * len(deps) + [pl.BlockSpec((ts, m), lambda n, k: (k, 0)), pl.BlockSpec((ts, tn), lambda n, k: (k, n))],
        out_specs=pl.BlockSpec((None, m, tn), lambda n, k: (n, 0, 0)),
        scratch_shapes=[pltpu.VMEM((m, tn), F32)],
        compiler_params=_cparams(2), name=name)(*deps, a, b)


def _ada_fwd(c_all, w_ada, b_ada):
    n = w_ada.shape[1]

    def body(c_ref, w_ref, b_ref, o_ref, ca_ref):
        c = c_ref[...]
        ca = c * jax.nn.sigmoid(c)
        ca_ref[...] = ca
        o_ref[...] = jnp.dot(ca, w_ref[...], preferred_element_type=F32, precision=lax.Precision.HIGHEST) + b_ref[...]

    return pl.pallas_call(
        body, out_shape=(jax.ShapeDtypeStruct((N_DEV, n), F32), jax.ShapeDtypeStruct((N_DEV, D), F32)),
        in_specs=[_VMEM] * 3, out_specs=[_VMEM] * 2, compiler_params=_cparams(0), name="ada_fwd")(c_all, w_ada, b_ada)


def _sum_small(parts):
    def body(p_ref, o_ref, d_ref):
        tot = p_ref[0]
        for dev in range(1, N_DEV):
            tot = tot + p_ref[dev]
        o_ref[...] = tot
        d_ref[...] = p_ref[:, 0:8, :]

    return pl.pallas_call(
        body, out_shape=(jax.ShapeDtypeStruct((N_SMALL, D), F32), jax.ShapeDtypeStruct((N_DEV, 8, D), F32)),
        in_specs=[_VMEM], out_specs=[_VMEM] * 2, compiler_params=_cparams(0), name="sum_small")(parts)


def _adamw_small(small, items, name):
    n = len(items)

    def body(small_ref, *refs):
        ins, outs = refs[:4 * n], refs[4 * n:]
        for k, item in enumerate(items):
            w_ref, g_ref, m_ref, v_ref = ins[4 * k:4 * k + 4]
            go_ref, d_ref, nm_ref, nv_ref = outs[4 * k:4 * k + 4]
            g_ = small_ref[item[1]:item[1] + w_ref.shape[0], :] if isinstance(item[1], int) else g_ref[...]
            go_ref[...] = g_
            m_ = ADAM_B1 * m_ref[...] + (1.0 - ADAM_B1) * g_
            v_ = ADAM_B2 * v_ref[...] + (1.0 - ADAM_B2) * (g_ * g_)
            nm_ref[...] = m_
            nv_ref[...] = v_
            m_hat = m_ / (1.0 - ADAM_B1 ** ADAM_STEP)
            v_hat = v_ / (1.0 - ADAM_B2 ** ADAM_STEP)
            d_ref[...] = -ADAM_LR * (m_hat / (jnp.sqrt(v_hat) + ADAM_EPS) + ADAM_WD * w_ref[...])

    operands = [a for w, g, m, v in items for a in (w, w if isinstance(g, int) else g, m, v)]
    out = pl.pallas_call(
        body, out_shape=tuple(jax.ShapeDtypeStruct(it[0].shape, F32) for it in items for _ in range(4)),
        in_specs=[_VMEM] * (1 + 4 * n), out_specs=[_VMEM] * (4 * n), name=name)(small, *operands)
    return [tuple(out[4 * k:4 * k + 4]) for k in range(n)]


HALF_STEPS = 4


def _adamw_halves(sets, c_idx, name, deps=()):
    nh = HALF_STEPS
    n = len(sets)

    def body(c_ref, *refs):
        refs = refs[len(deps):]
        ins, outs = refs[:5 * n], refs[5 * n:]
        for k in range(n):
            w_ref, mine_ref, other_ref, m_ref, v_ref = ins[5 * k:5 * k + 5]
            g_ref, d_ref, nm_ref, nv_ref = outs[4 * k:4 * k + 4]
            g_ = jnp.where(pl.program_id(0) // nh == c_ref[0], mine_ref[...], other_ref[...])
            g_ref[...] = g_
            m_ = ADAM_B1 * m_ref[...] + (1.0 - ADAM_B1) * g_
            v_ = ADAM_B2 * v_ref[...] + (1.0 - ADAM_B2) * (g_ * g_)
            nm_ref[...] = m_
            nv_ref[...] = v_
            m_hat = m_ / (1.0 - ADAM_B1 ** ADAM_STEP)
            v_hat = v_ / (1.0 - ADAM_B2 ** ADAM_STEP)
            d_ref[...] = -ADAM_LR * (m_hat / (jnp.sqrt(v_hat) + ADAM_EPS) + ADAM_WD * w_ref[...])

    in_specs, out_specs, out_shape = [], [], []
    for w, mine, _, _, _ in sets:
        r2, cols = mine.shape
        block = (r2 // nh, cols)
        full = pl.BlockSpec(block, lambda i, c: (i, 0))
        in_specs += [full, pl.BlockSpec(block, lambda i, c: (jnp.clip(i - c[0] * nh, 0, nh - 1), 0)),
                     pl.BlockSpec(block, lambda i, c: (jnp.clip(i - (1 - c[0]) * nh, 0, nh - 1), 0)), full, full]
        out_specs += [full] * 4
        out_shape += [jax.ShapeDtypeStruct((2 * r2, cols), F32)] * 4
    out = pl.pallas_call(
        body,
        grid_spec=pltpu.PrefetchScalarGridSpec(num_scalar_prefetch=1, grid=(2 * nh,),
                                               in_specs=[_ANY] * len(deps) + in_specs, out_specs=out_specs),
        out_shape=tuple(out_shape), compiler_params=_cparams(), name=name,
    )(c_idx, *deps, *[a for s in sets for a in s])
    return [tuple(out[4 * k:4 * k + 4]) for k in range(n)]


def _adamw_ada(w, c_act, dmod, m, v):
    rows, n = w.shape
    tr = 128

    def body(c_ref, d_ref, w_ref, m_ref, v_ref, g_ref, dl_ref, nm_ref, nv_ref):
        g_ = lax.dot_general(c_ref[...], d_ref[...], (((0,), (0,)), ((), ())), preferred_element_type=F32,
                             precision=lax.Precision.HIGHEST)
        g_ref[...] = g_
        m_ = ADAM_B1 * m_ref[...] + (1.0 - ADAM_B1) * g_
        v_ = ADAM_B2 * v_ref[...] + (1.0 - ADAM_B2) * (g_ * g_)
        nm_ref[...] = m_
        nv_ref[...] = v_
        m_hat = m_ / (1.0 - ADAM_B1 ** ADAM_STEP)
        v_hat = v_ / (1.0 - ADAM_B2 ** ADAM_STEP)
        dl_ref[...] = -ADAM_LR * (m_hat / (jnp.sqrt(v_hat) + ADAM_EPS) + ADAM_WD * w_ref[...])

    spec = pl.BlockSpec((tr, n), lambda i: (i, 0))
    return pl.pallas_call(
        body, grid=(rows // tr,), out_shape=(jax.ShapeDtypeStruct((rows, n), F32),) * 4,
        in_specs=[pl.BlockSpec((N_DEV, tr), lambda i: (0, i)), _VMEM, spec, spec, spec], out_specs=[spec] * 4,
        compiler_params=_cparams(), name="adamw_w_ada")(c_act, dmod, w, m, v)


def _add_halves(grads, recvs, c_idx, name):
    nw = len(grads)

    def body(c_ref, *refs):
        for g_ref, r_ref, o_ref in zip(refs[:nw], refs[nw:2 * nw], refs[2 * nw:]):
            o_ref[...] = (g_ref[...].astype(F32) + r_ref[...].astype(F32)).astype(BF16)

    mine = [pl.BlockSpec((None, None) + g.shape[2:], lambda k, c: (k, c[0], 0, 0)) for g in grads]
    whole = [pl.BlockSpec((None,) + g.shape[2:], lambda k, c: (k, 0, 0)) for g in grads]
    return pl.pallas_call(
        body,
        grid_spec=pltpu.PrefetchScalarGridSpec(num_scalar_prefetch=1, grid=(N_CHIPS,), in_specs=mine + whole, out_specs=whole),
        out_shape=tuple(jax.ShapeDtypeStruct((N_CHIPS,) + g.shape[2:], BF16) for g in grads),
        compiler_params=_cparams(), name=name)(c_idx, *grads, *recvs)


def _sum_chips(owns, others, chip_idx, name):
    nw = len(owns)
    steps = 4 if all(a.shape[1] % 64 == 0 for a in owns) else 2

    def body(p_ref, *refs):
        for own_ref, got_ref, o_ref in zip(refs[:nw], refs[nw:2 * nw], refs[2 * nw:]):
            o_ref[...] = (((own_ref[...].astype(F32) + got_ref[0].astype(F32)) + got_ref[1].astype(F32))
                          + got_ref[2].astype(F32))

    blocks = [(a.shape[1] // steps, a.shape[2]) for a in owns]
    return pl.pallas_call(
        body,
        grid_spec=pltpu.PrefetchScalarGridSpec(
            num_scalar_prefetch=1, grid=(steps,),
            in_specs=([pl.BlockSpec((None,) + b, lambda i, p: (p[0], i, 0)) for b in blocks]
                      + [pl.BlockSpec((N_CHIPS - 1,) + b, lambda i, p: (0, i, 0)) for b in blocks]),
            out_specs=[pl.BlockSpec(b, lambda i, p: (i, 0)) for b in blocks]),
        out_shape=tuple(jax.ShapeDtypeStruct(a.shape[1:], F32) for a in owns), compiler_params=_cparams(),
        name=name)(chip_idx, *owns, *others)


def _place():
    x, y, c = lax.axis_index("x"), lax.axis_index("y"), lax.axis_index("c")
    return x, y, c, 2 * x + y


def _flip(v, bit):
    return 1 - v if bit else v


def _allgather8(v, name, deps=()):
    r, n = v.shape

    def body(*refs):
        v_ref, out_ref, send_sems, recv_sems, local_sem = refs[len(deps):]
        x, y, c, _ = _place()
        me = 4 * x + 2 * y + c
        mine = pltpu.make_async_copy(v_ref, out_ref.at[me], local_sem)
        mine.start()
        sends = []
        for rel in range(1, N_DEV):
            peer = (_flip(x, rel & 4), _flip(y, rel & 2), _flip(c, rel & 1))
            cp = pltpu.make_async_remote_copy(v_ref, out_ref.at[me], send_sems.at[rel - 1], recv_sems.at[rel - 1],
                                              device_id=peer, device_id_type=MESH)
            cp.start()
            sends.append(cp)
        for rel in range(1, N_DEV):
            peer = (_flip(x, rel & 4), _flip(y, rel & 2), _flip(c, rel & 1))
            peer_idx = 4 * peer[0] + 2 * peer[1] + peer[2]
            pltpu.make_async_remote_copy(v_ref, out_ref.at[peer_idx], send_sems.at[rel - 1], recv_sems.at[rel - 1],
                                         device_id=peer, device_id_type=MESH).wait_recv()
        for cp in sends:
            cp.wait_send()
        mine.wait()

    return pl.pallas_call(
        body, out_shape=jax.ShapeDtypeStruct((N_DEV, r, n), F32), in_specs=[_ANY] * len(deps) + [_VMEM], out_specs=_VMEM,
        scratch_shapes=[pltpu.SemaphoreType.DMA((N_DEV - 1,)), pltpu.SemaphoreType.DMA((N_DEV - 1,)), pltpu.SemaphoreType.DMA(())],
        name=name)(*deps, v)


_HBM = pl.BlockSpec(memory_space=pltpu.HBM)
_SEM = pl.BlockSpec(memory_space=pltpu.SEMAPHORE)
_EFFECT = pltpu.SideEffectType.DATAFLOW_SIDE_EFFECTING


def _xchg_start(name, plan, n_copies, srcs, lands, after=(), sibling_id=None):
    bufs = list(srcs) + list(lands)
    ns, nb = len(srcs), len(srcs) + len(lands)

    def body(*refs):
        send_sems, recv_sems, token = refs[nb + len(after)], refs[nb + len(after) + 1], refs[-1]
        if sibling_id is not None:
            x, y, c, _ = _place()
            barrier = pltpu.get_barrier_semaphore()
            pl.semaphore_signal(barrier, inc=1, device_id=(x, y, 1 - c), device_id_type=MESH)
            pl.semaphore_wait(barrier, 1)
        for i, (src, dst, peer, _) in enumerate(plan(_place(), refs[:ns], refs[ns:nb])):
            pltpu.make_async_remote_copy(src, dst, send_sems.at[i], recv_sems.at[i], device_id=peer, device_id_type=MESH).start()
        token[...] = jnp.zeros_like(token)

    out = pl.pallas_call(
        body, name=name,
        out_shape=(pltpu.SemaphoreType.DMA((n_copies,)), pltpu.SemaphoreType.DMA((n_copies,)),
                   *[pltpu.HBM(a.shape, a.dtype) for a in bufs], jax.ShapeDtypeStruct((8, 128), F32)),
        in_specs=[_HBM] * nb + [_ANY] * len(after), out_specs=(_SEM, _SEM, *[_HBM] * nb, _VMEM),
        input_output_aliases={i: 2 + i for i in range(nb)},
        compiler_params=pltpu.CompilerParams(has_side_effects=_EFFECT, collective_id=sibling_id),
    )(*[pltpu.with_memory_space_constraint(a, pltpu.HBM) for a in bufs], *after)
    return (out[0], out[1]), out[2:2 + ns], out[2 + ns:2 + nb], out[-1]


def _xchg_wait(name, plan, sems, srcs, lands, after, sem_ids=None):
    bufs = list(srcs) + list(lands)
    ns, nb = len(srcs), len(srcs) + len(lands)

    def body(*refs):
        send_sems, recv_sems = refs[nb], refs[nb + 1]
        copies = plan(_place(), refs[:ns], refs[ns:nb])
        ids = range(len(copies)) if sem_ids is None else sem_ids
        for i, (src, _, peer, mine) in zip(ids, copies, strict=True):
            if i is not None:
                cp = pltpu.make_async_remote_copy(src, mine, send_sems.at[i], recv_sems.at[i], device_id=peer,
                                                  device_id_type=MESH)
                cp.wait_send()
                cp.wait_recv()

    out = pl.pallas_call(
        body, name=name, out_shape=tuple(pltpu.HBM(a.shape, a.dtype) for a in bufs),
        in_specs=[_HBM] * nb + [_SEM, _SEM] + [_ANY] * len(after), out_specs=tuple([_HBM] * nb),
        input_output_aliases={i: i for i in range(nb)},
        compiler_params=pltpu.CompilerParams(has_side_effects=_EFFECT),
    )(*bufs, *sems, *after)
    return out[:ns], out[ns:]


def _other_chips(place, which=(1, 2, 3)):
    x, y, c, _ = place
    return [((_flip(x, j & 2), _flip(y, j & 1), c), 2 * _flip(x, j & 2) + _flip(y, j & 1)) for j in which]


def _plan_gather_ici(chips):
    def plan(place, src_refs, land_refs):
        _, _, c, p = place
        return [(s.at[c], l.at[p, c], peer, l.at[q, c]) for s, l, which in zip(src_refs, land_refs, chips, strict=True)
                for peer, q in _other_chips(place, which)]
    return plan


def _plan_relay(which):
    def plan(place, src_refs, land_refs):
        x, y, c, _ = place
        return [(l.at[q, c], l.at[q, c], (x, y, 1 - c), l.at[q, 1 - c]) for l in land_refs for _, q in _other_chips(place, which)]
    return plan


def _plan_swap(place, src_refs, land_refs):
    x, y, c, _ = place
    return [(s.at[k, 1 - c], l.at[k], (x, y, 1 - c), l.at[k]) for s, l in zip(src_refs, land_refs) for k in range(N_CHIPS)]


def _plan_scatter(place, src_refs, land_refs):
    return [(s.at[q], l.at[j], peer, l.at[j]) for s, l in zip(src_refs, land_refs)
            for j, (peer, q) in enumerate(_other_chips(place))]


def _plan_share(place, src_refs, land_refs):
    x, y, c, _ = place
    return [(s, l, (x, y, 1 - c), l) for s, l in zip(src_refs, land_refs)]


def _to_bf16(ws):
    steps = 4

    def body(*refs):
        for w_ref, o_ref in zip(refs[:len(ws)], refs[len(ws):]):
            o_ref[...] = w_ref[...].astype(BF16)

    specs = [pl.BlockSpec((w.shape[0] // steps, w.shape[1]), lambda i: (i, 0)) for w in ws]
    return pl.pallas_call(
        body, grid=(steps,), out_shape=tuple(jax.ShapeDtypeStruct(w.shape, BF16) for w in ws), in_specs=specs,
        out_specs=specs, compiler_params=_cparams(), name="to_bf16")(*ws)


def _plan_gather8(place, src_refs, land_refs):
    x, y, c, _ = place
    me = 4 * x + 2 * y + c
    copies = []
    for s, l in zip(src_refs, land_refs):
        for rel in range(1, N_DEV):
            peer = (_flip(x, rel & 4), _flip(y, rel & 2), _flip(c, rel & 1))
            copies.append((s, l.at[me], peer, l.at[4 * peer[0] + 2 * peer[1] + peer[2]]))
    return copies


def _pack_rows(parts, n_rows, name, deps=()):
    def body(*refs):
        refs = refs[len(deps):]
        out_ref = refs[-1]
        out_ref[...] = jnp.zeros((n_rows, D), F32)
        at = 0
        for ref in refs[:-1]:
            k = ref.shape[0]
            out_ref[at:at + k, :] = ref[...]
            at += k

    return pl.pallas_call(
        body, out_shape=jax.ShapeDtypeStruct((n_rows, D), F32), in_specs=[_ANY] * len(deps) + [_VMEM] * len(parts),
        out_specs=_VMEM, name=name)(*deps, *parts)


TS_MM = 512
TS_IN = 1024
TS_GW = 2048
TS_GW_WIDE = 1024
TS_MIX = 256


def _halved(a):
    n, r, cols = a.shape
    return a.reshape(n, 2, r // 2, cols)


SIBLING_IDS = (1, 2)


def _rs_swap(name, grads, after=()):
    lands = [lax.empty((N_CHIPS,) + g.shape[2:], g.dtype) for g in grads]
    sems, grads, lands, token = _xchg_start(name + "_swap", _plan_swap, N_CHIPS * len(grads), grads, lands, after,
                                            sibling_id=SIBLING_IDS[0])
    return name, sems, grads, lands, token


def _rs_scatter(handle, after, chip, ci):
    name, sems, grads, lands, _ = handle
    grads, from_sibling = _xchg_wait(name + "_swap_wait", _plan_swap, sems, grads, lands, after)
    c_arr = jnp.reshape(ci, (1,)).astype(jnp.int32)
    pair_sums = _add_halves(list(grads), list(from_sibling), c_arr, name + "_add_halves")
    lands = [lax.empty((N_CHIPS - 1,) + p.shape[1:], p.dtype) for p in pair_sums]
    sems, pair_sums, lands, token = _xchg_start(name + "_scatter", _plan_scatter, 3 * len(pair_sums), pair_sums, lands)
    return name, sems, pair_sums, lands, jnp.reshape(chip, (1,)).astype(jnp.int32), token


def _rs_share(handle, after):
    name, sems, pair_sums, lands, chip_idx, _ = handle
    pair_sums, received = _xchg_wait(name + "_scatter_wait", _plan_scatter, sems, pair_sums, lands, after)
    halves = _sum_chips(list(pair_sums), list(received), chip_idx, name + "_sum_chips")
    lands = [lax.empty(h.shape, h.dtype) for h in halves]
    sems, halves, lands, token = _xchg_start(name + "_share", _plan_share, len(halves), halves, lands,
                                             sibling_id=SIBLING_IDS[1])
    return name, sems, halves, lands, token


def _rs_end(handle, after):
    name, sems, halves, lands, _ = handle
    halves, others = _xchg_wait(name + "_share_wait", _plan_share, sems, halves, lands, after)
    return list(zip(halves, others))


def kernel(x, c, w_ada, b_ada, g_norm_mix, w_in, conv_a_w, conv_b_w, conv_b_bias, w_rg_a, b_rg_a, w_rg_x, b_rg_x, lru_lambda, w_out, g_norm_ffn, w_gate_up, w_down, g_norm_final, loss_target, m_w_ada, m_b_ada, m_g_norm_mix, m_w_in, m_conv_a_w, m_conv_b_w, m_conv_b_bias, m_w_rg_a, m_b_rg_a, m_w_rg_x, m_b_rg_x, m_lru_lambda, m_w_out, m_g_norm_ffn, m_w_gate_up, m_w_down, m_g_norm_final, v_w_ada, v_b_ada, v_g_norm_mix, v_w_in, v_conv_a_w, v_conv_b_w, v_conv_b_bias, v_w_rg_a, v_b_rg_a, v_w_rg_x, v_b_rg_x, v_lru_lambda, v_w_out, v_g_norm_ffn, v_w_gate_up, v_w_down, v_g_norm_final):
    xi, yi, ci = lax.axis_index("x"), lax.axis_index("y"), lax.axis_index("c")
    chip = 2 * xi + yi
    me = 2 * chip + ci
    n_ada = w_ada.shape[2]

    def widen(w):
        return jnp.pad(w, ((0, 0), (0, D - w.shape[1])))

    got = _allgather8(_pack_rows([c, widen(conv_a_w[0]), widen(conv_b_w[0])], 8, "pack_c_conv"), "gather_c_conv")
    c_all = got[:, 0, :]
    conv_full = got[::2, 1:8, :D // N_CHIPS].transpose(1, 0, 2).reshape(7, D)

    mod_part, c_act = _ada_fwd(c_all, w_ada[0], lax.dynamic_slice_in_dim(b_ada, chip * n_ada, n_ada, axis=1))
    mod_all = _allgather8(mod_part, "gather_mod")
    mod_mine = lax.dynamic_index_in_dim(mod_all, me, axis=1, keepdims=False)[::2].reshape(6, D)
    vecs = _pack_rows([mod_mine, g_norm_mix, g_norm_ffn, g_norm_final.reshape(1, D), conv_b_bias, b_rg_a, b_rg_x, lru_lambda,
                       conv_full], N_VEC, "pack_vecs")

    shards = [s.reshape(2, s.shape[0] // 2, s.shape[1]) for s in _to_bf16(
        [w_in[0], w_rg_a[0].reshape(-1, HB), w_rg_x[0].reshape(-1, HB), w_out[0], w_gate_up[0], w_down[0]])]
    lands = [lax.dynamic_update_index_in_dim(lax.empty((N_CHIPS,) + s.shape, s.dtype), s, chip, 0) for s in shards]

    def send(name, first, last, after, chips):
        copies = [(k, j) for k, which in zip(range(first, last), chips, strict=True) for j in which]
        sems, srcs, zone, token = _xchg_start(name + "_ici", _plan_gather_ici(chips), len(copies), shards[first:last],
                                              lands[first:last], after)
        shards[first:last], lands[first:last] = srcs, zone
        return sems, copies, token

    def arrive(name, sent, first, last, after):
        sems, copies, _ = sent
        chips = [tuple(j for k, j in copies if k == want) for want in range(first, last)]
        ids = [copies.index((k, j)) for k, which in zip(range(first, last), chips) for j in which]
        srcs, zone = _xchg_wait(name + "_ici_wait", _plan_gather_ici(chips), sems, shards[first:last], lands[first:last], after,
                                ids)
        shards[first:last], lands[first:last] = srcs, zone

    def relay(name, first, last, which, sibling_id):
        plan = _plan_relay(which)
        sems, _, zone, token = _xchg_start(name + "_d2d", plan, len(which) * (last - first), [], lands[first:last],
                                           sibling_id=sibling_id)
        lands[first:last] = zone
        return name, plan, sems, first, last, token

    def relayed(handle, after):
        name, plan, sems, first, last, _ = handle
        lands[first:last] = _xchg_wait(name + "_d2d_wait", plan, sems, [], lands[first:last], after)[1]

    def to_blocks(v):
        return v.reshape(-1, TS_MIX // TIME_BLOCKS, TIME_BLOCKS, D).transpose(0, 2, 1, 3).reshape(v.shape)

    def from_blocks(v):
        return v.reshape(-1, TIME_BLOCKS, TS_MIX // TIME_BLOCKS, D).transpose(0, 2, 1, 3).reshape(v.shape)

    def chip_index(j):
        return jnp.reshape(chip ^ j, (1,)).astype(jnp.int32)

    def wg_in():
        return lands[0].reshape(N_CHIPS, D, C_IN)

    xs, target = to_blocks(x[0]), to_blocks(loss_target[0])
    sent_near = send("gather_in_near", 0, 1, [vecs], [(1, 2)])
    ts_in = min(TS_IN, xs.shape[0])
    h1, proj = _fwd_in_first(xs, vecs, wg_in(), chip_index(0), ts_in, deps=[sent_near[-1]])
    arrive("gather_in_near", sent_near, 0, 1, [proj])
    near = relay("gather_in_near", 0, 1, (1, 2), SIBLING_IDS[0])
    sent_rest = send("gather_rest", 0, 6, [near[-1]], [(3,)] + [(1, 2, 3)] * 5)
    relayed(near, [sent_rest[-1]])
    proj = _fwd_in_more(h1, wg_in(), proj, chip_index(1), ts_in, "fwd_in_y")
    proj = _fwd_in_more(h1, wg_in(), proj, chip_index(2), ts_in, "fwd_in_x")
    arrive("gather_in_far", sent_rest, 0, 1, [proj])
    far = relay("gather_in_far", 0, 1, (3,), SIBLING_IDS[1])
    arrive("gather_mix", sent_rest, 1, 4, [far[-1]])
    relayed(far, [far[-1]])
    mix = relay("gather_mix", 1, 4, (1, 2, 3), SIBLING_IDS[0])
    proj = _fwd_in_more(h1, wg_in(), proj, chip_index(3), ts_in, "fwd_in_xy", deps=[mix[-1]])
    relayed(mix, [proj])
    wg_rga, wg_rgx, wg_out = lands[1:4]
    wg_out = wg_out.reshape(D, D)

    def rg_full(wg):
        return wg.reshape(N_CHIPS, HEADS, HB // N_CHIPS, HB).transpose(1, 0, 2, 3).reshape(HEADS, HB, HB)

    wg_rga, wg_rgx = rg_full(wg_rga), rg_full(wg_rgx)

    x1, merged, z1, kept, decay = _fwd_mix(proj, xs, vecs, wg_rga, wg_rgx, wg_out, TS_MIX)
    arrive("gather_ffn", sent_rest, 4, 6, [x1])
    ffn = relay("gather_ffn", 4, 6, (1, 2, 3), SIBLING_IDS[1])
    relayed(ffn, [ffn[-1]])
    wg_gu, wg_dn = lands[4:6]
    wg_gu, wg_dn = wg_gu.reshape(N_CHIPS, D, C_GU), wg_dn.reshape(D_FF, D)
    dx1, h2, act, dz2, dgu, sm_ffn = _ffn_loss(x1, target, vecs, wg_gu, wg_dn, TS_MIX)

    def rg_chunks(dw):
        return _halved(dw.reshape(HEADS, N_CHIPS, HB // N_CHIPS, HB).transpose(1, 0, 2, 3).reshape(N_CHIPS, HB, HB).astype(BF16))

    ts_gw = min(TS_GW, xs.shape[0])
    g_dn = _grad_w(act, dz2, 1, min(TS_GW_WIDE, xs.shape[0]), "grad_w_down")
    g_gu = _grad_w(h2, dgu, N_CHIPS, ts_gw, "grad_w_gate_up")
    rs_b = _rs_swap("rs_b", [_halved(g_gu), _halved(g_dn.reshape(N_CHIPS, D_FF // N_CHIPS, D))])
    dproj, sm_mix, dw_rga, dw_rgx, dw_out = _bwd_mix(dx1, z1, merged, proj, kept, decay, vecs, wg_rga, wg_rgx, wg_out, sm_ffn, TS_MIX,
                                                     deps=[rs_b[-1]])
    rs_b = _rs_scatter(rs_b, [dproj], chip, ci)
    g_in = _grad_w(h1, dproj, N_CHIPS, ts_gw, "grad_w_in", deps=[rs_b[-1]])
    rs_b = _rs_share(rs_b, [g_in])
    rs_a = _rs_swap("rs_a", [_halved(g_in), rg_chunks(dw_rga), rg_chunks(dw_rgx),
                             _halved(dw_out.astype(BF16).reshape(N_CHIPS, D // N_CHIPS, D))], after=[rs_b[-1]])

    c_arr = jnp.reshape(ci, (1,)).astype(jnp.int32)

    def step_halves(name, items, deps=()):
        two_d = lambda a: a.reshape(-1, a.shape[-1])
        sets = [(two_d(w), halves[0], halves[1], two_d(m), two_d(v)) for w, halves, m, v in items.values()]
        for (n, (w, _, _, _)), out in zip(items.items(), _adamw_halves(sets, c_arr, name, deps)):
            res[n] = tuple(a.reshape(w.shape) for a in out)

    def shard_cols(row_block):
        return lax.dynamic_slice_in_dim(row_block, chip * (D // N_CHIPS), D // N_CHIPS, axis=1)

    gw_gu, gw_dn = _rs_end(rs_b, [rs_a[-1]])
    res = {}
    step_halves("adamw_ffn", {"w_gate_up": (w_gate_up, gw_gu, m_w_gate_up, v_w_gate_up),
                              "w_down": (w_down, gw_dn, m_w_down, v_w_down)}, [rs_a[-1]])
    rs_a = _rs_scatter(rs_a, [res["w_gate_up"][1], res["w_down"][1]], chip, ci)
    grad_x, sm_in = _bwd_in(dproj, xs, dx1, vecs, wg_in(), sm_mix, TS_MM, deps=[rs_a[-1]])
    small_zone = lax.dynamic_update_index_in_dim(lax.empty((N_DEV,) + sm_in.shape, F32), sm_in, me, 0)
    small_sems, small_src, small_zone, small_token = _xchg_start("gather_small", _plan_gather8, N_DEV - 1, [sm_in], [small_zone])
    rs_a = _rs_share(rs_a, [grad_x, small_token])

    small, per_dev = _sum_small(_xchg_wait("gather_small_wait", _plan_gather8, small_sems, small_src, small_zone, [rs_a[-1]])[1][0])
    dmod_all = per_dev[:, 0:6, :].reshape(N_DEV, 6 * D)
    res["w_ada"] = tuple(a[None] for a in _adamw_ada(w_ada[0], c_act, lax.dynamic_slice_in_dim(dmod_all, chip * n_ada, n_ada, axis=1),
                                                     m_w_ada[0], v_w_ada[0]))
    small_sets = {
        "b_ada": (b_ada.reshape(6, D), G_SH1, m_b_ada.reshape(6, D), v_b_ada.reshape(6, D)),
        "g_norm_mix": (g_norm_mix, G_GMIX, m_g_norm_mix, v_g_norm_mix),
        "conv_a_w": (conv_a_w[0], shard_cols(small[G_WA0:G_WA0 + 3]), m_conv_a_w[0], v_conv_a_w[0]),
        "conv_b_w": (conv_b_w[0], shard_cols(small[G_WB0:G_WB0 + 4]), m_conv_b_w[0], v_conv_b_w[0]),
        "conv_b_bias": (conv_b_bias, G_CBB, m_conv_b_bias, v_conv_b_bias),
        "b_rg_a": (b_rg_a, G_BA, m_b_rg_a, v_b_rg_a),
        "b_rg_x": (b_rg_x, G_BX, m_b_rg_x, v_b_rg_x),
        "lru_lambda": (lru_lambda, G_LAM, m_lru_lambda, v_lru_lambda),
        "g_norm_ffn": (g_norm_ffn, G_GFFN, m_g_norm_ffn, v_g_norm_ffn),
        "g_norm_final": (g_norm_final.reshape(1, D), G_GFIN, m_g_norm_final.reshape(1, D), v_g_norm_final.reshape(1, D)),
    }
    stepped = _adamw_small(small, list(small_sets.values()), "adamw_small")
    for (n, (w_, _, _, _)), out in zip(small_sets.items(), stepped):
        shape = (1,) + w_.shape if n.startswith("conv_") and n != "conv_b_bias" else w_.shape
        res[n] = tuple(a.reshape(shape) for a in out)
    gw_in, gw_rga, gw_rgx, gw_out = _rs_end(rs_a, [res[n][1] for n in res])
    step_halves("adamw_mix", {"w_in": (w_in, gw_in, m_w_in, v_w_in), "w_rg_a": (w_rg_a, gw_rga, m_w_rg_a, v_w_rg_a),
                              "w_rg_x": (w_rg_x, gw_rgx, m_w_rg_x, v_w_rg_x), "w_out": (w_out, gw_out, m_w_out, v_w_out)})
    res["b_ada"] = tuple(a.reshape(1, 6 * D) for a in res["b_ada"])
    res["g_norm_final"] = tuple(a.reshape(D) for a in res["g_norm_final"])
    names = ["w_ada", "b_ada", "g_norm_mix", "w_in", "conv_a_w", "conv_b_w", "conv_b_bias", "w_rg_a", "b_rg_a", "w_rg_x",
             "b_rg_x", "lru_lambda", "w_out", "g_norm_ffn", "w_gate_up", "w_down", "g_norm_final"]
    loss = jnp.sum(small[G_LOSS])
    return (loss, from_blocks(grad_x)[None], *[res[n][0] for n in names], *[res[n][1] for n in names],
            *[res[n][2] for n in names], *[res[n][3] for n in names])
```

```python
import jax
import jax.numpy as jnp
from jax import lax
from jax.experimental import pallas as pl
from jax.experimental.pallas import tpu as pltpu

F32 = jnp.float32
BF16 = jnp.bfloat16
MESH = pl.DeviceIdType.MESH

D = 1024
N_CHIPS = 4
N_DEV = 8
D_IN = 7 * D
C_IN = D_IN // N_CHIPS
D_FF = 2816
C_GU = 2 * D_FF // N_CHIPS
HEADS = 4
HB = D // HEADS
EPS = 1e-6
LRU_C = 8.0
ADAM_LR, ADAM_B1, ADAM_B2, ADAM_EPS, ADAM_WD, ADAM_STEP = 0.001, 0.9, 0.999, 1e-08, 0.01, 10
VMEM_LIMIT = 56 << 20

(V_SH1, V_SC1, V_GT1, V_SH2, V_SC2, V_GT2, V_GMIX, V_GFFN, V_GFIN, V_CBB, V_BA, V_BX, V_LAM,
 V_WA0, V_WA1, V_WA2, V_WB0, V_WB1, V_WB2, V_WB3) = range(20)
N_VEC = 24
(G_SH1, G_SC1, G_GT1, G_SH2, G_SC2, G_GT2, G_GMIX, G_CBB, G_BA, G_BX, G_LAM, G_GFFN, G_GFIN,
 G_WA0, G_WA1, G_WA2, G_WB0, G_WB1, G_WB2, G_WB3, G_LOSS) = range(21)
N_SMALL = 24

_VMEM = pl.BlockSpec(memory_space=pltpu.VMEM)
_ANY = pl.BlockSpec(memory_space=pl.ANY)


def _cparams(n_grid=1):
    return pltpu.CompilerParams(dimension_semantics=("arbitrary",) * n_grid, vmem_limit_bytes=VMEM_LIMIT)


def _after(deps, body):
    n = len(deps)
    return lambda *refs: body(*refs[n:])


def _rms(x):
    rstd = lax.rsqrt(jnp.mean(x * x, axis=-1, keepdims=True) + EPS)
    return x * rstd, rstd


def _rms_bwd(dxhat, xhat, rstd):
    return rstd * (dxhat - xhat * jnp.mean(dxhat * xhat, axis=-1, keepdims=True))


def _rowsum(v):
    return jnp.sum(v, axis=0, keepdims=True)


def _dot(a, b):
    return jnp.dot(a, b, preferred_element_type=F32)


def _dot_nt(a, b):
    return lax.dot_general(a, b, (((1,), (1,)), ((), ())), preferred_element_type=F32)


def _dot_tn(a, b):
    return lax.dot_general(a, b, (((0,), (0,)), ((), ())), preferred_element_type=F32)


def _gelu(x):
    k, c = 0.7978845608028654, 0.044715
    t = jnp.tanh(k * (x + c * x * x * x))
    return 0.5 * x * (1.0 + t), 0.5 * (1.0 + t) + 0.5 * x * (1.0 - t * t) * k * (1.0 + 3.0 * c * x * x)


def _log_sigmoid(lam):
    return jnp.minimum(lam, 0.0) - jnp.log1p(jnp.exp(-jnp.abs(lam)))


def _lru_gates(u, wa_ref, wx_ref, v_ref, row0):
    ub = u.astype(BF16)
    pre_a = jnp.concatenate([_dot(ub[:, h * HB:(h + 1) * HB], wa_ref[h]) for h in range(HEADS)], axis=1)
    pre_x = jnp.concatenate([_dot(ub[:, h * HB:(h + 1) * HB], wx_ref[h]) for h in range(HEADS)], axis=1)
    r = jax.nn.sigmoid(pre_a + v_ref[V_BA:V_BA + 1, :])
    ig = jax.nn.sigmoid(pre_x + v_ref[V_BX:V_BX + 1, :])
    log_a = LRU_C * r * _log_sigmoid(v_ref[V_LAM:V_LAM + 1, :])
    a = jnp.exp(log_a)
    x2 = 2.0 * log_a
    m2 = jnp.where(x2 > -0.03, -x2 * (1.0 + x2 * (0.5 + x2 * (1.0 / 6.0 + x2 * (1.0 / 24.0)))), 1.0 - a * a)
    mult = jnp.where(row0, 1.0, jnp.sqrt(jnp.maximum(m2, 0.0)))
    return r, ig, a, mult


TIME_BLOCKS = 8
N_KEPT = 10


def _late_blocks(v, buf, g, halo=None):
    n = buf.shape[0]
    out = []
    for idx in range(n):
        k = TIME_BLOCKS - n + idx
        buf[idx, 8:g + 8, :] = v[k * g:(k + 1) * g]
        if halo is not None:
            buf[idx, 7:8, :] = halo[idx]
        out.append(buf[idx, pl.ds(7, g), :])
        if halo is None:
            buf[idx, 7:8, :] = buf[idx, g + 7:g + 8, :]
    return out


def _earlier(v, s, late, g):
    return jnp.concatenate(late[len(late) - s:] + [v[0:(TIME_BLOCKS - s) * g]], axis=0)


def _early_blocks(v, buf, g):
    out = []
    for k in range(buf.shape[0]):
        buf[k, 0:g, :] = v[k * g:(k + 1) * g]
        out.append(buf[k, pl.ds(1, g), :])
        buf[k, g:g + 1, :] = buf[k, 0:1, :]
    return out


def _later(v, s, early, g):
    return jnp.concatenate([v[s * g:]] + early[0:s], axis=0)


def _fwd_in_first(x, vecs, w_in_g, q_idx, ts, deps=()):
    s = x.shape[0]

    def body(q_ref, x_ref, v_ref, w_ref, h1_ref, proj_ref):
        xhat, _ = _rms(x_ref[...])
        h = xhat * v_ref[V_GMIX:V_GMIX + 1, :] * (1.0 + v_ref[V_SC1:V_SC1 + 1, :]) + v_ref[V_SH1:V_SH1 + 1, :]
        hb = h.astype(BF16)
        h1_ref[...] = hb
        proj_ref[...] = _dot(hb, w_ref[...]).astype(BF16)

    return pl.pallas_call(
        lambda q_ref, *refs: body(q_ref, *refs[len(deps):]),
        grid_spec=pltpu.PrefetchScalarGridSpec(
            num_scalar_prefetch=1, grid=(s // ts,),
            in_specs=[_ANY] * len(deps) + [pl.BlockSpec((ts, D), lambda i, q: (i, 0)), _VMEM,
                                           pl.BlockSpec((None, D, C_IN), lambda i, q: (q[0], 0, 0))],
            out_specs=[pl.BlockSpec((ts, D), lambda i, q: (i, 0)), pl.BlockSpec((ts, C_IN), lambda i, q: (i, q[0]))]),
        out_shape=(jax.ShapeDtypeStruct((s, D), BF16), jax.ShapeDtypeStruct((s, D_IN), BF16)),
        compiler_params=_cparams(), name="fwd_in_own")(q_idx, *deps, x, vecs, w_in_g)


def _fwd_in_more(h1, w_in_g, proj, q_idx, ts, name, deps=()):
    s = h1.shape[0]

    def body(q_ref, h1_ref, w_ref, proj_in_ref, proj_ref):
        proj_ref[...] = _dot(h1_ref[...], w_ref[...]).astype(BF16)

    return pl.pallas_call(
        lambda q_ref, *refs: body(q_ref, *refs[len(deps):]),
        grid_spec=pltpu.PrefetchScalarGridSpec(
            num_scalar_prefetch=1, grid=(s // ts,),
            in_specs=[_ANY] * len(deps) + [pl.BlockSpec((ts, D), lambda i, q: (i, 0)),
                                           pl.BlockSpec((None, D, C_IN), lambda i, q: (q[0], 0, 0)), _ANY],
            out_specs=pl.BlockSpec((ts, C_IN), lambda i, q: (i, q[0]))),
        out_shape=jax.ShapeDtypeStruct((s, D_IN), BF16), input_output_aliases={len(deps) + 3: 0},
        compiler_params=_cparams(), name=name)(q_idx, *deps, h1, w_in_g, proj)


def _fwd_mix(proj, x, vecs, w_rga, w_rgx, w_out, ts, deps=()):
    s = x.shape[0]
    g = ts // TIME_BLOCKS

    def body(proj_ref, x_ref, v_ref, wa_ref, wx_ref, wo_ref, x1_ref, mg_ref, z1_ref, kept_ref, decay_ref,
             ua_buf, rx_buf, p_buf, q_buf, c_buf, hcarry):
        i = pl.program_id(0)

        @pl.when(i == 0)
        def _():
            ua_buf[...] = jnp.zeros(ua_buf.shape, F32)
            rx_buf[...] = jnp.zeros(rx_buf.shape, F32)
            hcarry[...] = jnp.zeros((8, D), F32)

        def seg(j):
            return proj_ref[:, j * D:(j + 1) * D].astype(F32)

        def vrow(j):
            return v_ref[j:j + 1, :]

        cb, cc, cx, rx, rg, ga, gb = (seg(j) for j in range(7))
        ua = cc * cx
        ua_late = _late_blocks(ua, ua_buf, g)
        rx_late = _late_blocks(rx, rx_buf, g)
        va = vrow(V_WA2) * ua + vrow(V_WA1) * _earlier(ua, 1, ua_late, g) + vrow(V_WA0) * _earlier(ua, 2, ua_late, g)
        u = (vrow(V_WB3) * rx + vrow(V_WB2) * _earlier(rx, 1, rx_late, g) + vrow(V_WB1) * _earlier(rx, 2, rx_late, g)
             + vrow(V_WB0) * _earlier(rx, 3, rx_late, g) + vrow(V_CBB))

        rows = lax.broadcasted_iota(jnp.int32, (ts, D), 0)
        row0 = jnp.logical_and(rows == 0, i == 0)
        r, ig, a, mult = _lru_gates(u, wa_ref, wx_ref, v_ref, row0)
        decay_ref[...] = a
        bx = mult * (ig * u)

        prods, sums = [a[0:g]], [bx[0:g]]
        for k in range(1, TIME_BLOCKS):
            ak = a[k * g:(k + 1) * g]
            sums.append(ak * sums[-1] + bx[k * g:(k + 1) * g])
            prods.append(ak * prods[-1])
        p_buf[...] = prods[-1]
        q_buf[...] = sums[-1]
        state = hcarry[0:1, :]
        for j in range(g):
            c_buf[j:j + 1, :] = state
            state = p_buf[j:j + 1, :] * state + q_buf[j:j + 1, :]
        hcarry[0:1, :] = state
        entering = c_buf[...]
        h = jnp.concatenate([sums[k] + prods[k] * entering for k in range(TIME_BLOCKS)], axis=0)

        gel, dgel = _gelu(rg)
        sga = jax.nn.sigmoid(ga)
        sgb = jax.nn.sigmoid(gb)
        for j, keep in enumerate((va, r, ig, sga, sgb, gel, dgel, mult, u, h)):
            kept_ref[:, j * D:(j + 1) * D] = keep.astype(BF16)
        merged = (sga * (cb * va) + sgb * (h * gel)).astype(BF16)
        mg_ref[...] = merged
        z1 = _dot(merged, wo_ref[...])
        z1_ref[...] = z1.astype(BF16)
        x1_ref[...] = x_ref[...] + vrow(V_GT1) * z1

    row = lambda i: (i, 0)
    return pl.pallas_call(
        _after(deps, body), grid=(s // ts,),
        out_shape=(jax.ShapeDtypeStruct((s, D), F32), jax.ShapeDtypeStruct((s, D), BF16), jax.ShapeDtypeStruct((s, D), BF16),
                   jax.ShapeDtypeStruct((s, N_KEPT * D), BF16), jax.ShapeDtypeStruct((s, D), F32)),
        in_specs=[_ANY] * len(deps) + [pl.BlockSpec((ts, D_IN), row), pl.BlockSpec((ts, D), row), _VMEM, _VMEM, _VMEM, _VMEM],
        out_specs=[pl.BlockSpec((ts, D), row)] * 3 + [pl.BlockSpec((ts, N_KEPT * D), row), pl.BlockSpec((ts, D), row)],
        scratch_shapes=[pltpu.VMEM((2, g + 8, D), F32), pltpu.VMEM((3, g + 8, D), F32), pltpu.VMEM((g, D), F32),
                        pltpu.VMEM((g, D), F32), pltpu.VMEM((g, D), F32), pltpu.VMEM((8, D), F32)],
        compiler_params=_cparams(), name="fwd_mix")(*deps, proj, x, vecs, w_rga, w_rgx, w_out)


def _ffn_loss(x1, target, vecs, w_gu_g, w_dn, ts):
    s = x1.shape[0]

    def body(x1_ref, t_ref, v_ref, wgu_ref, wdn_ref, dx1_ref, h2_ref, act_ref, dz2_ref, dgu_ref, sm_ref):
        @pl.when(pl.program_id(0) == 0)
        def _():
            sm_ref[...] = jnp.zeros((N_SMALL, D), F32)

        def vrow(j):
            return v_ref[j:j + 1, :]

        n_sub = 1
        rows = [slice(k * (ts // n_sub), (k + 1) * (ts // n_sub)) for k in range(n_sub)]
        subs = [dict(r=r, sums={}) for r in rows]

        def stage_norm(t):
            t["x1"] = x1_ref[t["r"], :]
            t["xh1"], t["rstd1"] = _rms(t["x1"])
            t["n2"] = t["xh1"] * vrow(V_GFFN)
            t["h2"] = (t["n2"] * (1.0 + vrow(V_SC2)) + vrow(V_SH2)).astype(BF16)
            h2_ref[t["r"], :] = t["h2"]

        def stage_up(t):
            h2 = t["h2"]
            g = jnp.concatenate([_dot(h2, wgu_ref[0]), _dot(h2, wgu_ref[1])], axis=1)
            t["up"] = jnp.concatenate([_dot(h2, wgu_ref[2]), _dot(h2, wgu_ref[3])], axis=1)
            t["g"] = g
            t["sg"] = jax.nn.sigmoid(g)
            t["silu"] = g * t["sg"]
            t["act"] = (t["silu"] * t["up"]).astype(BF16)
            act_ref[t["r"], :] = t["act"]

        def stage_down_loss(t):
            z2 = _dot(t["act"], wdn_ref[...])
            x2 = t["x1"] + vrow(V_GT2) * z2
            xh2, rstd2 = _rms(x2)
            err = xh2 * vrow(V_GFIN) - t_ref[t["r"], :]
            t["sums"][G_LOSS] = _rowsum((0.5 / D) * err * err)
            dy = err * (1.0 / D)
            t["sums"][G_GFIN] = _rowsum(dy * xh2)
            t["dx2"] = _rms_bwd(dy * vrow(V_GFIN), xh2, rstd2)
            t["sums"][G_GT2] = _rowsum(t["dx2"] * z2)
            t["dz2"] = (vrow(V_GT2) * t["dx2"]).astype(BF16)
            dz2_ref[t["r"], :] = t["dz2"]

        def stage_back_act(t):
            dact = _dot_nt(t["dz2"], wdn_ref[...])
            g, sg = t["g"], t["sg"]
            t["dgate"] = (dact * t["up"] * (sg * (1.0 + g * (1.0 - sg)))).astype(BF16)
            t["dup"] = (dact * t["silu"]).astype(BF16)
            dgu_ref[t["r"], 0:D_FF] = t["dgate"]
            dgu_ref[t["r"], D_FF:2 * D_FF] = t["dup"]

        def stage_back_norm(t):
            dgate, dup = t["dgate"], t["dup"]
            dh2 = (_dot_nt(dgate[:, 0:C_GU], wgu_ref[0]) + _dot_nt(dgate[:, C_GU:2 * C_GU], wgu_ref[1])
                   + _dot_nt(dup[:, 0:C_GU], wgu_ref[2]) + _dot_nt(dup[:, C_GU:2 * C_GU], wgu_ref[3]))
            t["sums"][G_SH2] = _rowsum(dh2)
            t["sums"][G_SC2] = _rowsum(dh2 * t["n2"])
            dn2 = dh2 * (1.0 + vrow(V_SC2))
            t["sums"][G_GFFN] = _rowsum(dn2 * t["xh1"])
            dx1_ref[t["r"], :] = t["dx2"] + _rms_bwd(dn2 * vrow(V_GFFN), t["xh1"], t["rstd1"])

        for stage in (stage_norm, stage_up, stage_down_loss, stage_back_act, stage_back_norm):
            for t in subs:
                stage(t)
        for j in subs[0]["sums"]:
            total = subs[0]["sums"][j]
            for t in subs[1:]:
                total = total + t["sums"][j]
            sm_ref[j:j + 1, :] += total

    row = lambda i: (i, 0)
    return pl.pallas_call(
        body, grid=(s // ts,),
        out_shape=(jax.ShapeDtypeStruct((s, D), F32), jax.ShapeDtypeStruct((s, D), BF16), jax.ShapeDtypeStruct((s, D_FF), BF16),
                   jax.ShapeDtypeStruct((s, D), BF16), jax.ShapeDtypeStruct((s, 2 * D_FF), BF16),
                   jax.ShapeDtypeStruct((N_SMALL, D), F32)),
        in_specs=[pl.BlockSpec((ts, D), row), pl.BlockSpec((ts, D), row), _VMEM, _VMEM, _VMEM],
        out_specs=[pl.BlockSpec((ts, D), row), pl.BlockSpec((ts, D), row), pl.BlockSpec((ts, D_FF), row),
                   pl.BlockSpec((ts, D), row), pl.BlockSpec((ts, 2 * D_FF), row), pl.BlockSpec((N_SMALL, D), lambda i: (0, 0))],
        compiler_params=_cparams(), name="ffn_loss")(x1, target, vecs, w_gu_g, w_dn)


def _bwd_mix(dx1, z1, merged, proj, kept, decay, vecs, w_rga, w_rgx, w_out, small, ts, deps=()):
    s = dx1.shape[0]
    nt = s // ts
    g = ts // TIME_BLOCKS
    assert g % 16 == 0

    def body(dx1_ref, z1_ref, mg_ref, proj_ref, kept_ref, decay_ref, hh_ref, v_ref, wa_ref, wx_ref,
             wo_ref, sm0_ref, dproj_ref, sm_ref, dwa_ref, dwx_ref, dwo_ref,
             h_buf, a_buf, dva_buf, du_buf, p_buf, q_buf, c_buf, lcarry):
        i = pl.program_id(0)
        first_tile = i == nt - 1

        @pl.when(i == 0)
        def _():
            a_buf[...] = jnp.zeros(a_buf.shape, F32)
            dva_buf[...] = jnp.zeros(dva_buf.shape, F32)
            du_buf[...] = jnp.zeros(du_buf.shape, F32)
            lcarry[...] = jnp.zeros((8, D), F32)
            sm_ref[...] = sm0_ref[...]
            dwa_ref[...] = jnp.zeros((HEADS, HB, HB), F32)
            dwx_ref[...] = jnp.zeros((HEADS, HB, HB), F32)
            dwo_ref[...] = jnp.zeros((D, D), F32)

        def seg(j):
            return proj_ref[:, j * D:(j + 1) * D].astype(F32)

        def vrow(j):
            return v_ref[j:j + 1, :]

        def acc(j, val):
            sm_ref[j:j + 1, :] += _rowsum(val)

        cb, cc, cx, rx = (seg(j) for j in range(4))
        ua = cc * cx
        va, r, ig, sga, sgb, gel, dgel, mult, u, h = (kept_ref[:, j * D:(j + 1) * D].astype(F32) for j in range(N_KEPT))
        a = decay_ref[...]
        rows = lax.broadcasted_iota(jnp.int32, (ts, D), 0)
        row0 = jnp.logical_and(rows == 0, first_tile)

        dx1 = dx1_ref[...]
        acc(G_GT1, dx1 * z1_ref[...].astype(F32))
        dz1 = (vrow(V_GT1) * dx1).astype(BF16)
        dwo_ref[...] += _dot_tn(mg_ref[...], dz1)
        dmg = _dot_nt(dz1, wo_ref[...])
        dya = dmg * sga
        dyb = dmg * sgb
        dproj_ref[:, 5 * D:6 * D] = (dya * (cb * va) * (1.0 - sga)).astype(BF16)
        dproj_ref[:, 6 * D:7 * D] = (dyb * (h * gel) * (1.0 - sgb)).astype(BF16)

        dproj_ref[:, 0:D] = (dya * va).astype(BF16)
        dva = dya * cb
        dva_early = _early_blocks(dva, dva_buf, g)
        dva1 = _later(dva, 1, dva_early, g)
        dva2 = _later(dva, 2, dva_early, g)
        dua = vrow(V_WA2) * dva + vrow(V_WA1) * dva1 + vrow(V_WA0) * dva2
        acc(G_WA2, ua * dva)
        acc(G_WA1, ua * dva1)
        acc(G_WA0, ua * dva2)
        dproj_ref[:, D:2 * D] = (dua * cx).astype(BF16)
        dproj_ref[:, 2 * D:3 * D] = (dua * cc).astype(BF16)

        dproj_ref[:, 4 * D:5 * D] = (dyb * h * dgel).astype(BF16)
        a_next = _later(a, 1, _early_blocks(a, a_buf, g), g)
        dh = dyb * gel
        last = TIME_BLOCKS - 1
        prods, sums = {last: a_next[last * g:]}, {last: dh[last * g:]}
        for k in range(last - 1, -1, -1):
            ak = a_next[k * g:(k + 1) * g]
            sums[k] = dh[k * g:(k + 1) * g] + ak * sums[k + 1]
            prods[k] = ak * prods[k + 1]
        p_buf[...] = prods[0]
        q_buf[...] = sums[0]
        state = lcarry[0:1, :]
        for j in range(g - 1, -1, -1):
            c_buf[j:j + 1, :] = state
            state = q_buf[j:j + 1, :] + p_buf[j:j + 1, :] * state
        lcarry[0:1, :] = state
        entering = c_buf[...]
        lam = jnp.concatenate([sums[k] + prods[k] * entering for k in range(TIME_BLOCKS)], axis=0)

        last = lax.broadcasted_iota(jnp.int32, hh_ref.shape, 0) == hh_ref.shape[0] - 1
        h_halo = [jnp.where(first_tile, 0.0, jnp.sum(jnp.where(last, hh_ref[...].astype(F32), 0.0), axis=0, keepdims=True))]
        da = lam * _earlier(h, 1, _late_blocks(h, h_buf, g, h_halo), g)
        dmult = jnp.where(row0, 0.0, lam * (ig * u))
        di = lam * mult * u
        du = lam * mult * ig
        dlog_a = da * a - dmult * (a * a) / mult
        lam_p = vrow(V_LAM)
        dr = dlog_a * (LRU_C * _log_sigmoid(lam_p))
        sm_ref[G_LAM:G_LAM + 1, :] += _rowsum(dlog_a * r) * (LRU_C * jax.nn.sigmoid(-lam_p))
        dpa = dr * r * (1.0 - r)
        dpx = di * ig * (1.0 - ig)
        acc(G_BA, dpa)
        acc(G_BX, dpx)
        dpab = dpa.astype(BF16)
        dpxb = dpx.astype(BF16)
        ub = u.astype(BF16)
        back = []
        for hd in range(HEADS):
            cols = slice(hd * HB, (hd + 1) * HB)
            back.append(_dot_nt(dpab[:, cols], wa_ref[hd]) + _dot_nt(dpxb[:, cols], wx_ref[hd]))
            dwa_ref[hd] += _dot_tn(ub[:, cols], dpab[:, cols])
            dwx_ref[hd] += _dot_tn(ub[:, cols], dpxb[:, cols])
        du = du + jnp.concatenate(back, axis=1)

        acc(G_CBB, du)
        du_early = _early_blocks(du, du_buf, g)
        du1 = _later(du, 1, du_early, g)
        du2 = _later(du, 2, du_early, g)
        du3 = _later(du, 3, du_early, g)
        dproj_ref[:, 3 * D:4 * D] = (vrow(V_WB3) * du + vrow(V_WB2) * du1 + vrow(V_WB1) * du2 + vrow(V_WB0) * du3).astype(BF16)
        acc(G_WB3, rx * du)
        acc(G_WB2, rx * du1)
        acc(G_WB1, rx * du2)
        acc(G_WB0, rx * du3)

    rev = lambda i: (nt - 1 - i, 0)
    h_halo16 = lambda i: (jnp.maximum((nt - 1 - i) * (ts // 16) - 1, 0), N_KEPT - 1)
    const2 = lambda i: (0, 0)
    const3 = lambda i: (0, 0, 0)
    return pl.pallas_call(
        _after(deps, body), grid=(nt,),
        out_shape=(jax.ShapeDtypeStruct((s, D_IN), BF16), jax.ShapeDtypeStruct((N_SMALL, D), F32),
                   jax.ShapeDtypeStruct((HEADS, HB, HB), F32), jax.ShapeDtypeStruct((HEADS, HB, HB), F32),
                   jax.ShapeDtypeStruct((D, D), F32)),
        in_specs=[_ANY] * len(deps) + [pl.BlockSpec((ts, D), rev), pl.BlockSpec((ts, D), rev), pl.BlockSpec((ts, D), rev),
                  pl.BlockSpec((ts, 4 * D), rev), pl.BlockSpec((ts, N_KEPT * D), rev), pl.BlockSpec((ts, D), rev),
                  pl.BlockSpec((16, D), h_halo16), _VMEM, _VMEM, _VMEM, _VMEM, _VMEM],
        out_specs=[pl.BlockSpec((ts, D_IN), rev), pl.BlockSpec((N_SMALL, D), const2),
                   pl.BlockSpec((HEADS, HB, HB), const3), pl.BlockSpec((HEADS, HB, HB), const3), pl.BlockSpec((D, D), const2)],
        scratch_shapes=[pltpu.VMEM((1, g + 8, D), F32), pltpu.VMEM((1, g + 8, D), F32),
                        pltpu.VMEM((2, g + 8, D), F32), pltpu.VMEM((3, g + 8, D), F32), pltpu.VMEM((g, D), F32),
                        pltpu.VMEM((g, D), F32), pltpu.VMEM((g, D), F32), pltpu.VMEM((8, D), F32)],
        compiler_params=_cparams(), name="bwd_mix")(*deps, dx1, z1, merged, proj, kept, decay, kept, vecs, w_rga,
                                                    w_rgx, w_out, small)


def _bwd_in(dproj, x, dx1, vecs, w_in_g, small, ts, deps=()):
    s = x.shape[0]

    def body(dp_ref, x_ref, dx1_ref, v_ref, w_ref, sm0_ref, gx_ref, sm_ref):
        @pl.when(pl.program_id(0) == 0)
        def _():
            sm_ref[...] = sm0_ref[...]

        def vrow(j):
            return v_ref[j:j + 1, :]

        dh1 = _dot_nt(dp_ref[:, 0:C_IN], w_ref[0])
        for k in range(1, N_CHIPS):
            dh1 += _dot_nt(dp_ref[:, k * C_IN:(k + 1) * C_IN], w_ref[k])
        xh, rstd = _rms(x_ref[...])
        sm_ref[G_SH1:G_SH1 + 1, :] += _rowsum(dh1)
        sm_ref[G_SC1:G_SC1 + 1, :] += _rowsum(dh1 * (xh * vrow(V_GMIX)))
        dn1 = dh1 * (1.0 + vrow(V_SC1))
        sm_ref[G_GMIX:G_GMIX + 1, :] += _rowsum(dn1 * xh)
        gx_ref[...] = dx1_ref[...] + _rms_bwd(dn1 * vrow(V_GMIX), xh, rstd)

    row = lambda i: (i, 0)
    return pl.pallas_call(
        _after(deps, body), grid=(s // ts,),
        out_shape=(jax.ShapeDtypeStruct((s, D), F32), jax.ShapeDtypeStruct((N_SMALL, D), F32)),
        in_specs=[_ANY] * len(deps) + [pl.BlockSpec((ts, D_IN), row), pl.BlockSpec((ts, D), row), pl.BlockSpec((ts, D), row),
                                       _VMEM, _VMEM, _VMEM],
        out_specs=[pl.BlockSpec((ts, D), row), pl.BlockSpec((N_SMALL, D), lambda i: (0, 0))],
        compiler_params=_cparams(), name="bwd_in")(*deps, dproj, x, dx1, vecs, w_in_g, small)


def _grad_w(a, b, n_col_blocks, ts, name, deps=()):
    s, m = a.shape
    tn = b.shape[1] // n_col_blocks
    n_steps = s // ts

    def body(a_ref, b_ref, o_ref, acc_ref):
        k = pl.program_id(1)

        @pl.when(k == 0)
        def _():
            acc_ref[...] = jnp.zeros((m, tn), F32)

        acc_ref[...] += _dot_tn(a_ref[...], b_ref[...])

        @pl.when(k == n_steps - 1)
        def _():
            o_ref[...] = acc_ref[...].astype(BF16)

    return pl.pallas_call(
        _after(deps, body), grid=(n_col_blocks, n_steps),
        out_shape=jax.ShapeDtypeStruct((n_col_blocks, m, tn), BF16),
        in_specs=[_ANY] * len(deps) + [pl.BlockSpec((ts, m), lambda n, k: (k, 0)), pl.BlockSpec((ts, tn), lambda n, k: (k, n))],
        out_specs=pl.BlockSpec((None, m, tn), lambda n, k: (n, 0, 0)),
        scratch_shapes=[pltpu.VMEM((m, tn), F32)],
        compiler_params=_cparams(2), name=name)(*deps, a, b)


def _ada_fwd(c_all, w_ada, b_ada):
    n = w_ada.shape[1]

    def body(c_ref, w_ref, b_ref, o_ref, ca_ref):
        c = c_ref[...]
        ca = c * jax.nn.sigmoid(c)
        ca_ref[...] = ca
        o_ref[...] = jnp.dot(ca, w_ref[...], preferred_element_type=F32, precision=lax.Precision.HIGHEST) + b_ref[...]

    return pl.pallas_call(
        body, out_shape=(jax.ShapeDtypeStruct((N_DEV, n), F32), jax.ShapeDtypeStruct((N_DEV, D), F32)),
        in_specs=[_VMEM] * 3, out_specs=[_VMEM] * 2, compiler_params=_cparams(0), name="ada_fwd")(c_all, w_ada, b_ada)


def _sum_small(parts):
    def body(p_ref, o_ref, d_ref):
        tot = p_ref[0]
        for dev in range(1, N_DEV):
            tot = tot + p_ref[dev]
        o_ref[...] = tot
        d_ref[...] = p_ref[:, 0:8, :]

    return pl.pallas_call(
        body, out_shape=(jax.ShapeDtypeStruct((N_SMALL, D), F32), jax.ShapeDtypeStruct((N_DEV, 8, D), F32)),
        in_specs=[_VMEM], out_specs=[_VMEM] * 2, compiler_params=_cparams(0), name="sum_small")(parts)


def _adamw_small(items, name):
    n = len(items)

    def body(*refs):
        ins, outs = refs[:4 * n], refs[4 * n:]
        for k in range(n):
            w_ref, g_ref, m_ref, v_ref = ins[4 * k:4 * k + 4]
            go_ref, d_ref, nm_ref, nv_ref = outs[4 * k:4 * k + 4]
            pieces = g_ref.shape[0] if g_ref.shape != w_ref.shape else 1
            for j in range(pieces):
                at = (slice(None), slice(j * D, (j + 1) * D)) if pieces > 1 else (slice(None), slice(None))
                g_ = g_ref[j:j + 1, :] if pieces > 1 else g_ref[...]
                go_ref[at] = g_
                m_ = ADAM_B1 * m_ref[at] + (1.0 - ADAM_B1) * g_
                v_ = ADAM_B2 * v_ref[at] + (1.0 - ADAM_B2) * (g_ * g_)
                nm_ref[at] = m_
                nv_ref[at] = v_
                m_hat = m_ / (1.0 - ADAM_B1 ** ADAM_STEP)
                v_hat = v_ / (1.0 - ADAM_B2 ** ADAM_STEP)
                d_ref[at] = -ADAM_LR * (m_hat / (jnp.sqrt(v_hat) + ADAM_EPS) + ADAM_WD * w_ref[at])

    out = pl.pallas_call(
        body, out_shape=tuple(jax.ShapeDtypeStruct(it[0].shape, F32) for it in items for _ in range(4)),
        in_specs=[_VMEM] * (4 * n), out_specs=[_VMEM] * (4 * n), name=name)(*[a for it in items for a in it])
    return [tuple(out[4 * k:4 * k + 4]) for k in range(n)]


HALF_STEPS = 4


def _adamw_halves(sets, c_idx, name, deps=()):
    nh = HALF_STEPS
    n = len(sets)

    def body(c_ref, *refs):
        refs = refs[len(deps):]
        ins, outs = refs[:5 * n], refs[5 * n:]
        for k in range(n):
            w_ref, mine_ref, other_ref, m_ref, v_ref = ins[5 * k:5 * k + 5]
            g_ref, d_ref, nm_ref, nv_ref = outs[4 * k:4 * k + 4]
            g_ = jnp.where(pl.program_id(0) // nh == c_ref[0], mine_ref[...], other_ref[...])
            g_ref[...] = g_
            m_ = ADAM_B1 * m_ref[...] + (1.0 - ADAM_B1) * g_
            v_ = ADAM_B2 * v_ref[...] + (1.0 - ADAM_B2) * (g_ * g_)
            nm_ref[...] = m_
            nv_ref[...] = v_
            m_hat = m_ / (1.0 - ADAM_B1 ** ADAM_STEP)
            v_hat = v_ / (1.0 - ADAM_B2 ** ADAM_STEP)
            d_ref[...] = -ADAM_LR * (m_hat / (jnp.sqrt(v_hat) + ADAM_EPS) + ADAM_WD * w_ref[...])

    in_specs, out_specs, out_shape = [], [], []
    for w, mine, _, _, _ in sets:
        r2, cols = mine.shape
        block = (r2 // nh, cols)
        full = pl.BlockSpec(block, lambda i, c: (i, 0))
        in_specs += [full, pl.BlockSpec(block, lambda i, c: (jnp.clip(i - c[0] * nh, 0, nh - 1), 0)),
                     pl.BlockSpec(block, lambda i, c: (jnp.clip(i - (1 - c[0]) * nh, 0, nh - 1), 0)), full, full]
        out_specs += [full] * 4
        out_shape += [jax.ShapeDtypeStruct((2 * r2, cols), F32)] * 4
    out = pl.pallas_call(
        body,
        grid_spec=pltpu.PrefetchScalarGridSpec(num_scalar_prefetch=1, grid=(2 * nh,),
                                               in_specs=[_ANY] * len(deps) + in_specs, out_specs=out_specs),
        out_shape=tuple(out_shape), compiler_params=_cparams(), name=name,
    )(c_idx, *deps, *[a for s in sets for a in s])
    return [tuple(out[4 * k:4 * k + 4]) for k in range(n)]


def _adamw_ada(w, c_act, dmod, m, v):
    rows, n = w.shape
    tr = 128

    def body(c_ref, d_ref, w_ref, m_ref, v_ref, g_ref, dl_ref, nm_ref, nv_ref):
        g_ = lax.dot_general(c_ref[...], d_ref[...], (((0,), (0,)), ((), ())), preferred_element_type=F32,
                             precision=lax.Precision.HIGHEST)
        g_ref[...] = g_
        m_ = ADAM_B1 * m_ref[...] + (1.0 - ADAM_B1) * g_
        v_ = ADAM_B2 * v_ref[...] + (1.0 - ADAM_B2) * (g_ * g_)
        nm_ref[...] = m_
        nv_ref[...] = v_
        m_hat = m_ / (1.0 - ADAM_B1 ** ADAM_STEP)
        v_hat = v_ / (1.0 - ADAM_B2 ** ADAM_STEP)
        dl_ref[...] = -ADAM_LR * (m_hat / (jnp.sqrt(v_hat) + ADAM_EPS) + ADAM_WD * w_ref[...])

    spec = pl.BlockSpec((tr, n), lambda i: (i, 0))
    return pl.pallas_call(
        body, grid=(rows // tr,), out_shape=(jax.ShapeDtypeStruct((rows, n), F32),) * 4,
        in_specs=[pl.BlockSpec((N_DEV, tr), lambda i: (0, i)), _VMEM, spec, spec, spec], out_specs=[spec] * 4,
        compiler_params=_cparams(), name="adamw_w_ada")(c_act, dmod, w, m, v)


def _add_halves(grads, recvs, c_idx, name):
    nw = len(grads)

    def body(c_ref, *refs):
        for g_ref, r_ref, o_ref in zip(refs[:nw], refs[nw:2 * nw], refs[2 * nw:]):
            o_ref[...] = (g_ref[...].astype(F32) + r_ref[...].astype(F32)).astype(BF16)

    mine = [pl.BlockSpec((None, None) + g.shape[2:], lambda k, c: (k, c[0], 0, 0)) for g in grads]
    whole = [pl.BlockSpec((None,) + g.shape[2:], lambda k, c: (k, 0, 0)) for g in grads]
    return pl.pallas_call(
        body,
        grid_spec=pltpu.PrefetchScalarGridSpec(num_scalar_prefetch=1, grid=(N_CHIPS,), in_specs=mine + whole, out_specs=whole),
        out_shape=tuple(jax.ShapeDtypeStruct((N_CHIPS,) + g.shape[2:], BF16) for g in grads),
        compiler_params=_cparams(), name=name)(c_idx, *grads, *recvs)


def _sum_chips(owns, others, chip_idx, name):
    nw = len(owns)
    steps = 4 if all(a.shape[1] % 64 == 0 for a in owns) else 2

    def body(p_ref, *refs):
        for own_ref, got_ref, o_ref in zip(refs[:nw], refs[nw:2 * nw], refs[2 * nw:]):
            o_ref[...] = (((own_ref[...].astype(F32) + got_ref[0].astype(F32)) + got_ref[1].astype(F32))
                          + got_ref[2].astype(F32))

    blocks = [(a.shape[1] // steps, a.shape[2]) for a in owns]
    return pl.pallas_call(
        body,
        grid_spec=pltpu.PrefetchScalarGridSpec(
            num_scalar_prefetch=1, grid=(steps,),
            in_specs=([pl.BlockSpec((None,) + b, lambda i, p: (p[0], i, 0)) for b in blocks]
                      + [pl.BlockSpec((N_CHIPS - 1,) + b, lambda i, p: (0, i, 0)) for b in blocks]),
            out_specs=[pl.BlockSpec(b, lambda i, p: (i, 0)) for b in blocks]),
        out_shape=tuple(jax.ShapeDtypeStruct(a.shape[1:], F32) for a in owns), compiler_params=_cparams(),
        name=name)(chip_idx, *owns, *others)


def _place():
    x, y, c = lax.axis_index("x"), lax.axis_index("y"), lax.axis_index("c")
    return x, y, c, 2 * x + y


def _flip(v, bit):
    return 1 - v if bit else v


def _allgather8(v, name, deps=()):
    r, n = v.shape

    def body(*refs):
        v_ref, out_ref, send_sems, recv_sems, local_sem = refs[len(deps):]
        x, y, c, _ = _place()
        me = 4 * x + 2 * y + c
        mine = pltpu.make_async_copy(v_ref, out_ref.at[me], local_sem)
        mine.start()
        sends = []
        for rel in range(1, N_DEV):
            peer = (_flip(x, rel & 4), _flip(y, rel & 2), _flip(c, rel & 1))
            cp = pltpu.make_async_remote_copy(v_ref, out_ref.at[me], send_sems.at[rel - 1], recv_sems.at[rel - 1],
                                              device_id=peer, device_id_type=MESH)
            cp.start()
            sends.append(cp)
        for rel in range(1, N_DEV):
            peer = (_flip(x, rel & 4), _flip(y, rel & 2), _flip(c, rel & 1))
            peer_idx = 4 * peer[0] + 2 * peer[1] + peer[2]
            pltpu.make_async_remote_copy(v_ref, out_ref.at[peer_idx], send_sems.at[rel - 1], recv_sems.at[rel - 1],
                                         device_id=peer, device_id_type=MESH).wait_recv()
        for cp in sends:
            cp.wait_send()
        mine.wait()

    return pl.pallas_call(
        body, out_shape=jax.ShapeDtypeStruct((N_DEV, r, n), F32), in_specs=[_ANY] * len(deps) + [_VMEM], out_specs=_VMEM,
        scratch_shapes=[pltpu.SemaphoreType.DMA((N_DEV - 1,)), pltpu.SemaphoreType.DMA((N_DEV - 1,)), pltpu.SemaphoreType.DMA(())],
        name=name)(*deps, v)


_HBM = pl.BlockSpec(memory_space=pltpu.HBM)
_SEM = pl.BlockSpec(memory_space=pltpu.SEMAPHORE)
_EFFECT = pltpu.SideEffectType.DATAFLOW_SIDE_EFFECTING


def _xchg_start(name, plan, n_copies, srcs, lands, after=(), sibling_id=None):
    bufs = list(srcs) + list(lands)
    ns, nb = len(srcs), len(srcs) + len(lands)

    def body(*refs):
        send_sems, recv_sems, token = refs[nb + len(after)], refs[nb + len(after) + 1], refs[-1]
        if sibling_id is not None:
            x, y, c, _ = _place()
            barrier = pltpu.get_barrier_semaphore()
            pl.semaphore_signal(barrier, inc=1, device_id=(x, y, 1 - c), device_id_type=MESH)
            pl.semaphore_wait(barrier, 1)
        for i, (src, dst, peer, _) in enumerate(plan(_place(), refs[:ns], refs[ns:nb])):
            pltpu.make_async_remote_copy(src, dst, send_sems.at[i], recv_sems.at[i], device_id=peer, device_id_type=MESH).start()
        token[...] = jnp.zeros_like(token)

    out = pl.pallas_call(
        body, name=name,
        out_shape=(pltpu.SemaphoreType.DMA((n_copies,)), pltpu.SemaphoreType.DMA((n_copies,)),
                   *[pltpu.HBM(a.shape, a.dtype) for a in bufs], jax.ShapeDtypeStruct((8, 128), F32)),
        in_specs=[_HBM] * nb + [_ANY] * len(after), out_specs=(_SEM, _SEM, *[_HBM] * nb, _VMEM),
        input_output_aliases={i: 2 + i for i in range(nb)},
        compiler_params=pltpu.CompilerParams(has_side_effects=_EFFECT, collective_id=sibling_id),
    )(*[pltpu.with_memory_space_constraint(a, pltpu.HBM) for a in bufs], *after)
    return (out[0], out[1]), out[2:2 + ns], out[2 + ns:2 + nb], out[-1]


def _xchg_wait(name, plan, sems, srcs, lands, after, sem_ids=None):
    bufs = list(srcs) + list(lands)
    ns, nb = len(srcs), len(srcs) + len(lands)

    def body(*refs):
        send_sems, recv_sems = refs[nb], refs[nb + 1]
        copies = plan(_place(), refs[:ns], refs[ns:nb])
        ids = range(len(copies)) if sem_ids is None else sem_ids
        for i, (src, _, peer, mine) in zip(ids, copies, strict=True):
            if i is not None:
                cp = pltpu.make_async_remote_copy(src, mine, send_sems.at[i], recv_sems.at[i], device_id=peer,
                                                  device_id_type=MESH)
                cp.wait_send()
                cp.wait_recv()

    out = pl.pallas_call(
        body, name=name, out_shape=tuple(pltpu.HBM(a.shape, a.dtype) for a in bufs),
        in_specs=[_HBM] * nb + [_SEM, _SEM] + [_ANY] * len(after), out_specs=tuple([_HBM] * nb),
        input_output_aliases={i: i for i in range(nb)},
        compiler_params=pltpu.CompilerParams(has_side_effects=_EFFECT),
    )(*bufs, *sems, *after)
    return out[:ns], out[ns:]


def _other_chips(place, which=(1, 2, 3)):
    x, y, c, _ = place
    return [((_flip(x, j & 2), _flip(y, j & 1), c), 2 * _flip(x, j & 2) + _flip(y, j & 1)) for j in which]


def _plan_gather_ici(chips):
    def plan(place, src_refs, land_refs):
        _, _, c, p = place
        return [(s.at[c], l.at[p, c], peer, l.at[q, c]) for s, l, which in zip(src_refs, land_refs, chips, strict=True)
                for peer, q in _other_chips(place, which)]
    return plan


def _plan_relay(which):
    def plan(place, src_refs, land_refs):
        x, y, c, _ = place
        return [(l.at[q, c], l.at[q, c], (x, y, 1 - c), l.at[q, 1 - c]) for l in land_refs for _, q in _other_chips(place, which)]
    return plan


def _plan_swap(place, src_refs, land_refs):
    x, y, c, _ = place
    return [(s.at[k, 1 - c], l.at[k], (x, y, 1 - c), l.at[k]) for s, l in zip(src_refs, land_refs) for k in range(N_CHIPS)]


def _plan_scatter(place, src_refs, land_refs):
    return [(s.at[q], l.at[j], peer, l.at[j]) for s, l in zip(src_refs, land_refs)
            for j, (peer, q) in enumerate(_other_chips(place))]


def _plan_share(place, src_refs, land_refs):
    x, y, c, _ = place
    return [(s, l, (x, y, 1 - c), l) for s, l in zip(src_refs, land_refs)]


def _plan_gather8(place, src_refs, land_refs):
    x, y, c, _ = place
    me = 4 * x + 2 * y + c
    copies = []
    for s, l in zip(src_refs, land_refs):
        for rel in range(1, N_DEV):
            peer = (_flip(x, rel & 4), _flip(y, rel & 2), _flip(c, rel & 1))
            copies.append((s, l.at[me], peer, l.at[4 * peer[0] + 2 * peer[1] + peer[2]]))
    return copies


def _pack_rows(parts, n_rows, name, deps=()):
    def body(*refs):
        refs = refs[len(deps):]
        out_ref = refs[-1]
        out_ref[...] = jnp.zeros((n_rows, D), F32)
        at = 0
        for ref in refs[:-1]:
            k = ref.shape[0]
            out_ref[at:at + k, :] = ref[...]
            at += k

    return pl.pallas_call(
        body, out_shape=jax.ShapeDtypeStruct((n_rows, D), F32), in_specs=[_ANY] * len(deps) + [_VMEM] * len(parts),
        out_specs=_VMEM, name=name)(*deps, *parts)


TS_MM = 512
TS_IN = 1024
TS_GW = 2048
TS_GW_WIDE = 1024
TS_MIX = 256


def _halved(a):
    n, r, cols = a.shape
    return a.reshape(n, 2, r // 2, cols)


SIBLING_IDS = (1, 2)


def _rs_swap(name, grads, after=()):
    lands = [lax.empty((N_CHIPS,) + g.shape[2:], g.dtype) for g in grads]
    sems, grads, lands, token = _xchg_start(name + "_swap", _plan_swap, N_CHIPS * len(grads), grads, lands, after,
                                            sibling_id=SIBLING_IDS[0])
    return name, sems, grads, lands, token


def _rs_scatter(handle, after, chip, ci):
    name, sems, grads, lands, _ = handle
    grads, from_sibling = _xchg_wait(name + "_swap_wait", _plan_swap, sems, grads, lands, after)
    c_arr = jnp.reshape(ci, (1,)).astype(jnp.int32)
    pair_sums = _add_halves(list(grads), list(from_sibling), c_arr, name + "_add_halves")
    lands = [lax.empty((N_CHIPS - 1,) + p.shape[1:], p.dtype) for p in pair_sums]
    sems, pair_sums, lands, token = _xchg_start(name + "_scatter", _plan_scatter, 3 * len(pair_sums), pair_sums, lands)
    return name, sems, pair_sums, lands, jnp.reshape(chip, (1,)).astype(jnp.int32), token


def _rs_share(handle, after):
    name, sems, pair_sums, lands, chip_idx, _ = handle
    pair_sums, received = _xchg_wait(name + "_scatter_wait", _plan_scatter, sems, pair_sums, lands, after)
    halves = _sum_chips(list(pair_sums), list(received), chip_idx, name + "_sum_chips")
    lands = [lax.empty(h.shape, h.dtype) for h in halves]
    sems, halves, lands, token = _xchg_start(name + "_share", _plan_share, len(halves), halves, lands,
                                             sibling_id=SIBLING_IDS[1])
    return name, sems, halves, lands, token


def _rs_end(handle, after):
    name, sems, halves, lands, _ = handle
    halves, others = _xchg_wait(name + "_share_wait", _plan_share, sems, halves, lands, after)
    return list(zip(halves, others))


def kernel(x, c, w_ada, b_ada, g_norm_mix, w_in, conv_a_w, conv_b_w, conv_b_bias, w_rg_a, b_rg_a, w_rg_x, b_rg_x, lru_lambda, w_out, g_norm_ffn, w_gate_up, w_down, g_norm_final, loss_target, m_w_ada, m_b_ada, m_g_norm_mix, m_w_in, m_conv_a_w, m_conv_b_w, m_conv_b_bias, m_w_rg_a, m_b_rg_a, m_w_rg_x, m_b_rg_x, m_lru_lambda, m_w_out, m_g_norm_ffn, m_w_gate_up, m_w_down, m_g_norm_final, v_w_ada, v_b_ada, v_g_norm_mix, v_w_in, v_conv_a_w, v_conv_b_w, v_conv_b_bias, v_w_rg_a, v_b_rg_a, v_w_rg_x, v_b_rg_x, v_lru_lambda, v_w_out, v_g_norm_ffn, v_w_gate_up, v_w_down, v_g_norm_final):
    xi, yi, ci = lax.axis_index("x"), lax.axis_index("y"), lax.axis_index("c")
    chip = 2 * xi + yi
    me = 2 * chip + ci
    n_ada = w_ada.shape[2]

    def widen(w):
        return jnp.pad(w, ((0, 0), (0, D - w.shape[1])))

    got = _allgather8(_pack_rows([c, widen(conv_a_w[0]), widen(conv_b_w[0])], 8, "pack_c_conv"), "gather_c_conv")
    c_all = got[:, 0, :]
    conv_full = got[::2, 1:8, :D // N_CHIPS].transpose(1, 0, 2).reshape(7, D)

    mod_part, c_act = _ada_fwd(c_all, w_ada[0], lax.dynamic_slice_in_dim(b_ada, chip * n_ada, n_ada, axis=1))
    mod_all = _allgather8(mod_part, "gather_mod")
    mod_mine = lax.dynamic_index_in_dim(mod_all, me, axis=1, keepdims=False)[::2].reshape(6, D)
    vecs = _pack_rows([mod_mine, g_norm_mix, g_norm_ffn, g_norm_final.reshape(1, D), conv_b_bias, b_rg_a, b_rg_x, lru_lambda,
                       conv_full], N_VEC, "pack_vecs")

    def rg_shard(w):
        return w[0].astype(BF16).reshape(2, HEADS * HB // N_CHIPS // 2, HB)

    shards = [w_in[0].astype(BF16).reshape(2, D // 2, C_IN), rg_shard(w_rg_a), rg_shard(w_rg_x),
              w_out[0].astype(BF16).reshape(2, D // N_CHIPS // 2, D), w_gate_up[0].astype(BF16).reshape(2, D // 2, C_GU),
              w_down[0].astype(BF16).reshape(2, D_FF // N_CHIPS // 2, D)]
    lands = [lax.dynamic_update_index_in_dim(lax.empty((N_CHIPS,) + s.shape, s.dtype), s, chip, 0) for s in shards]

    def send(name, first, last, after, chips):
        copies = [(k, j) for k, which in zip(range(first, last), chips, strict=True) for j in which]
        sems, srcs, zone, token = _xchg_start(name + "_ici", _plan_gather_ici(chips), len(copies), shards[first:last],
                                              lands[first:last], after)
        shards[first:last], lands[first:last] = srcs, zone
        return sems, copies, token

    def arrive(name, sent, first, last, after):
        sems, copies, _ = sent
        chips = [tuple(j for k, j in copies if k == want) for want in range(first, last)]
        ids = [copies.index((k, j)) for k, which in zip(range(first, last), chips) for j in which]
        srcs, zone = _xchg_wait(name + "_ici_wait", _plan_gather_ici(chips), sems, shards[first:last], lands[first:last], after,
                                ids)
        shards[first:last], lands[first:last] = srcs, zone

    def relay(name, first, last, which, sibling_id):
        plan = _plan_relay(which)
        sems, _, zone, token = _xchg_start(name + "_d2d", plan, len(which) * (last - first), [], lands[first:last],
                                           sibling_id=sibling_id)
        lands[first:last] = zone
        return name, plan, sems, first, last, token

    def relayed(handle, after):
        name, plan, sems, first, last, _ = handle
        lands[first:last] = _xchg_wait(name + "_d2d_wait", plan, sems, [], lands[first:last], after)[1]

    def to_blocks(v):
        return v.reshape(-1, TS_MIX // TIME_BLOCKS, TIME_BLOCKS, D).transpose(0, 2, 1, 3).reshape(v.shape)

    def from_blocks(v):
        return v.reshape(-1, TIME_BLOCKS, TS_MIX // TIME_BLOCKS, D).transpose(0, 2, 1, 3).reshape(v.shape)

    def chip_index(j):
        return jnp.reshape(chip ^ j, (1,)).astype(jnp.int32)

    def wg_in():
        return lands[0].reshape(N_CHIPS, D, C_IN)

    xs, target = to_blocks(x[0]), to_blocks(loss_target[0])
    sent_near = send("gather_in_near", 0, 1, [vecs], [(1, 2)])
    ts_in = min(TS_IN, xs.shape[0])
    h1, proj = _fwd_in_first(xs, vecs, wg_in(), chip_index(0), ts_in, deps=[sent_near[-1]])
    arrive("gather_in_near", sent_near, 0, 1, [proj])
    near = relay("gather_in_near", 0, 1, (1, 2), SIBLING_IDS[0])
    sent_rest = send("gather_rest", 0, 6, [near[-1]], [(3,)] + [(1, 2, 3)] * 5)
    relayed(near, [sent_rest[-1]])
    proj = _fwd_in_more(h1, wg_in(), proj, chip_index(1), ts_in, "fwd_in_y")
    proj = _fwd_in_more(h1, wg_in(), proj, chip_index(2), ts_in, "fwd_in_x")
    arrive("gather_in_far", sent_rest, 0, 1, [proj])
    far = relay("gather_in_far", 0, 1, (3,), SIBLING_IDS[1])
    arrive("gather_mix", sent_rest, 1, 4, [far[-1]])
    relayed(far, [far[-1]])
    mix = relay("gather_mix", 1, 4, (1, 2, 3), SIBLING_IDS[0])
    proj = _fwd_in_more(h1, wg_in(), proj, chip_index(3), ts_in, "fwd_in_xy", deps=[mix[-1]])
    relayed(mix, [proj])
    wg_rga, wg_rgx, wg_out = lands[1:4]
    wg_out = wg_out.reshape(D, D)

    def rg_full(wg):
        return wg.reshape(N_CHIPS, HEADS, HB // N_CHIPS, HB).transpose(1, 0, 2, 3).reshape(HEADS, HB, HB)

    wg_rga, wg_rgx = rg_full(wg_rga), rg_full(wg_rgx)

    x1, merged, z1, kept, decay = _fwd_mix(proj, xs, vecs, wg_rga, wg_rgx, wg_out, TS_MIX)
    arrive("gather_ffn", sent_rest, 4, 6, [x1])
    ffn = relay("gather_ffn", 4, 6, (1, 2, 3), SIBLING_IDS[1])
    relayed(ffn, [ffn[-1]])
    wg_gu, wg_dn = lands[4:6]
    wg_gu, wg_dn = wg_gu.reshape(N_CHIPS, D, C_GU), wg_dn.reshape(D_FF, D)
    dx1, h2, act, dz2, dgu, sm_ffn = _ffn_loss(x1, target, vecs, wg_gu, wg_dn, TS_MIX)

    def rg_chunks(dw):
        return _halved(dw.reshape(HEADS, N_CHIPS, HB // N_CHIPS, HB).transpose(1, 0, 2, 3).reshape(N_CHIPS, HB, HB).astype(BF16))

    ts_gw = min(TS_GW, xs.shape[0])
    g_dn = _grad_w(act, dz2, 1, min(TS_GW_WIDE, xs.shape[0]), "grad_w_down")
    g_gu = _grad_w(h2, dgu, N_CHIPS, ts_gw, "grad_w_gate_up")
    rs_b = _rs_swap("rs_b", [_halved(g_gu), _halved(g_dn.reshape(N_CHIPS, D_FF // N_CHIPS, D))])
    dproj, sm_mix, dw_rga, dw_rgx, dw_out = _bwd_mix(dx1, z1, merged, proj, kept, decay, vecs, wg_rga, wg_rgx, wg_out, sm_ffn, TS_MIX,
                                                     deps=[rs_b[-1]])
    rs_b = _rs_scatter(rs_b, [dproj], chip, ci)
    g_in = _grad_w(h1, dproj, N_CHIPS, ts_gw, "grad_w_in", deps=[rs_b[-1]])
    rs_b = _rs_share(rs_b, [g_in])
    rs_a = _rs_swap("rs_a", [_halved(g_in), rg_chunks(dw_rga), rg_chunks(dw_rgx),
                             _halved(dw_out.astype(BF16).reshape(N_CHIPS, D // N_CHIPS, D))], after=[rs_b[-1]])

    c_arr = jnp.reshape(ci, (1,)).astype(jnp.int32)

    def step_halves(name, items, deps=()):
        two_d = lambda a: a.reshape(-1, a.shape[-1])
        sets = [(two_d(w), halves[0], halves[1], two_d(m), two_d(v)) for w, halves, m, v in items.values()]
        for (n, (w, _, _, _)), out in zip(items.items(), _adamw_halves(sets, c_arr, name, deps)):
            res[n] = tuple(a.reshape(w.shape) for a in out)

    def shard_cols(row_block):
        return lax.dynamic_slice_in_dim(row_block, chip * (D // N_CHIPS), D // N_CHIPS, axis=1)

    gw_gu, gw_dn = _rs_end(rs_b, [rs_a[-1]])
    res = {}
    step_halves("adamw_ffn", {"w_gate_up": (w_gate_up, gw_gu, m_w_gate_up, v_w_gate_up),
                              "w_down": (w_down, gw_dn, m_w_down, v_w_down)}, [rs_a[-1]])
    rs_a = _rs_scatter(rs_a, [res["w_gate_up"][1], res["w_down"][1]], chip, ci)
    grad_x, sm_in = _bwd_in(dproj, xs, dx1, vecs, wg_in(), sm_mix, TS_MM, deps=[rs_a[-1]])
    small_zone = lax.dynamic_update_index_in_dim(lax.empty((N_DEV,) + sm_in.shape, F32), sm_in, me, 0)
    small_sems, small_src, small_zone, small_token = _xchg_start("gather_small", _plan_gather8, N_DEV - 1, [sm_in], [small_zone])
    rs_a = _rs_share(rs_a, [grad_x, small_token])

    small, per_dev = _sum_small(_xchg_wait("gather_small_wait", _plan_gather8, small_sems, small_src, small_zone, [rs_a[-1]])[1][0])
    dmod_all = per_dev[:, 0:6, :].reshape(N_DEV, 6 * D)
    res["w_ada"] = tuple(a[None] for a in _adamw_ada(w_ada[0], c_act, lax.dynamic_slice_in_dim(dmod_all, chip * n_ada, n_ada, axis=1),
                                                     m_w_ada[0], v_w_ada[0]))
    small_sets = {
        "b_ada": (b_ada, small[G_SH1:G_SH1 + 6], m_b_ada, v_b_ada),
        "g_norm_mix": (g_norm_mix, small[G_GMIX:G_GMIX + 1], m_g_norm_mix, v_g_norm_mix),
        "conv_a_w": (conv_a_w[0], shard_cols(small[G_WA0:G_WA0 + 3]), m_conv_a_w[0], v_conv_a_w[0]),
        "conv_b_w": (conv_b_w[0], shard_cols(small[G_WB0:G_WB0 + 4]), m_conv_b_w[0], v_conv_b_w[0]),
        "conv_b_bias": (conv_b_bias, small[G_CBB:G_CBB + 1], m_conv_b_bias, v_conv_b_bias),
        "b_rg_a": (b_rg_a, small[G_BA:G_BA + 1], m_b_rg_a, v_b_rg_a),
        "b_rg_x": (b_rg_x, small[G_BX:G_BX + 1], m_b_rg_x, v_b_rg_x),
        "lru_lambda": (lru_lambda, small[G_LAM:G_LAM + 1], m_lru_lambda, v_lru_lambda),
        "g_norm_ffn": (g_norm_ffn, small[G_GFFN:G_GFFN + 1], m_g_norm_ffn, v_g_norm_ffn),
        "g_norm_final": (g_norm_final.reshape(1, D), small[G_GFIN:G_GFIN + 1], m_g_norm_final.reshape(1, D),
                         v_g_norm_final.reshape(1, D)),
    }
    stepped = _adamw_small(list(small_sets.values()), "adamw_small")
    for (n, (w_, _, _, _)), out in zip(small_sets.items(), stepped):
        shape = (1,) + w_.shape if n.startswith("conv_") and n != "conv_b_bias" else w_.shape
        res[n] = tuple(a.reshape(shape) for a in out)
    gw_in, gw_rga, gw_rgx, gw_out = _rs_end(rs_a, [res[n][1] for n in res])
    step_halves("adamw_mix", {"w_in": (w_in, gw_in, m_w_in, v_w_in), "w_rg_a": (w_rg_a, gw_rga, m_w_rg_a, v_w_rg_a),
                              "w_rg_x": (w_rg_x, gw_rgx, m_w_rg_x, v_w_rg_x), "w_out": (w_out, gw_out, m_w_out, v_w_out)})
    res["g_norm_final"] = tuple(a.reshape(D) for a in res["g_norm_final"])
    names = ["w_ada", "b_ada", "g_norm_mix", "w_in", "conv_a_w", "conv_b_w", "conv_b_bias", "w_rg_a", "b_rg_a", "w_rg_x",
             "b_rg_x", "lru_lambda", "w_out", "g_norm_ffn", "w_gate_up", "w_down", "g_norm_final"]
    loss = jnp.sum(small[G_LOSS])
    return (loss, from_blocks(grad_x)[None], *[res[n][0] for n in names], *[res[n][1] for n in names],
            *[res[n][2] for n in names], *[res[n][3] for n in names])
```

```python
import jax
import jax.numpy as jnp
from jax import lax
from jax.experimental import pallas as pl
from jax.experimental.pallas import tpu as pltpu

F32 = jnp.float32
BF16 = jnp.bfloat16
MESH = pl.DeviceIdType.MESH

D = 1024
N_CHIPS = 4
N_DEV = 8
D_IN = 7 * D
C_IN = D_IN // N_CHIPS
D_FF = 2816
C_GU = 2 * D_FF // N_CHIPS
HEADS = 4
HB = D // HEADS
EPS = 1e-6
LRU_C = 8.0
ADAM_LR, ADAM_B1, ADAM_B2, ADAM_EPS, ADAM_WD, ADAM_STEP = 0.001, 0.9, 0.999, 1e-08, 0.01, 10
VMEM_LIMIT = 56 << 20

(V_SH1, V_SC1, V_GT1, V_SH2, V_SC2, V_GT2, V_GMIX, V_GFFN, V_GFIN, V_CBB, V_BA, V_BX, V_LAM,
 V_WA0, V_WA1, V_WA2, V_WB0, V_WB1, V_WB2, V_WB3) = range(20)
N_VEC = 24
(G_SH1, G_SC1, G_GT1, G_SH2, G_SC2, G_GT2, G_GMIX, G_CBB, G_BA, G_BX, G_LAM, G_GFFN, G_GFIN,
 G_WA0, G_WA1, G_WA2, G_WB0, G_WB1, G_WB2, G_WB3, G_LOSS) = range(21)
N_SMALL = 24

_VMEM = pl.BlockSpec(memory_space=pltpu.VMEM)
_ANY = pl.BlockSpec(memory_space=pl.ANY)


def _cparams(n_grid=1):
    return pltpu.CompilerParams(dimension_semantics=("arbitrary",) * n_grid, vmem_limit_bytes=VMEM_LIMIT)


def _after(deps, body):
    n = len(deps)
    return lambda *refs: body(*refs[n:])


def _rms(x):
    rstd = lax.rsqrt(jnp.mean(x * x, axis=-1, keepdims=True) + EPS)
    return x * rstd, rstd


def _rms_bwd(dxhat, xhat, rstd):
    return rstd * (dxhat - xhat * jnp.mean(dxhat * xhat, axis=-1, keepdims=True))


def _rowsum(v):
    return jnp.sum(v, axis=0, keepdims=True)


def _dot(a, b):
    return jnp.dot(a, b, preferred_element_type=F32)


def _dot_nt(a, b):
    return lax.dot_general(a, b, (((1,), (1,)), ((), ())), preferred_element_type=F32)


def _dot_tn(a, b):
    return lax.dot_general(a, b, (((0,), (0,)), ((), ())), preferred_element_type=F32)


def _gelu(x):
    k, c = 0.7978845608028654, 0.044715
    t = jnp.tanh(k * (x + c * x * x * x))
    return 0.5 * x * (1.0 + t), 0.5 * (1.0 + t) + 0.5 * x * (1.0 - t * t) * k * (1.0 + 3.0 * c * x * x)


def _log_sigmoid(lam):
    return jnp.minimum(lam, 0.0) - jnp.log1p(jnp.exp(-jnp.abs(lam)))


def _lru_gates(u, wa_ref, wx_ref, v_ref, row0):
    ub = u.astype(BF16)
    pre_a = jnp.concatenate([_dot(ub[:, h * HB:(h + 1) * HB], wa_ref[h]) for h in range(HEADS)], axis=1)
    pre_x = jnp.concatenate([_dot(ub[:, h * HB:(h + 1) * HB], wx_ref[h]) for h in range(HEADS)], axis=1)
    r = jax.nn.sigmoid(pre_a + v_ref[V_BA:V_BA + 1, :])
    ig = jax.nn.sigmoid(pre_x + v_ref[V_BX:V_BX + 1, :])
    log_a = LRU_C * r * _log_sigmoid(v_ref[V_LAM:V_LAM + 1, :])
    a = jnp.exp(log_a)
    x2 = 2.0 * log_a
    m2 = jnp.where(x2 > -0.03, -x2 * (1.0 + x2 * (0.5 + x2 * (1.0 / 6.0 + x2 * (1.0 / 24.0)))), 1.0 - a * a)
    mult = jnp.where(row0, 1.0, jnp.sqrt(jnp.maximum(m2, 0.0)))
    return r, ig, a, mult


TIME_BLOCKS = 8
N_KEPT = 10


def _late_blocks(v, buf, g, halo=None):
    n = buf.shape[0]
    out = []
    for idx in range(n):
        k = TIME_BLOCKS - n + idx
        buf[idx, 8:g + 8, :] = v[k * g:(k + 1) * g]
        if halo is not None:
            buf[idx, 7:8, :] = halo[idx]
        out.append(buf[idx, pl.ds(7, g), :])
        if halo is None:
            buf[idx, 7:8, :] = buf[idx, g + 7:g + 8, :]
    return out


def _earlier(v, s, late, g):
    return jnp.concatenate(late[len(late) - s:] + [v[0:(TIME_BLOCKS - s) * g]], axis=0)


def _early_blocks(v, buf, g):
    out = []
    for k in range(buf.shape[0]):
        buf[k, 0:g, :] = v[k * g:(k + 1) * g]
        out.append(buf[k, pl.ds(1, g), :])
        buf[k, g:g + 1, :] = buf[k, 0:1, :]
    return out


def _later(v, s, early, g):
    return jnp.concatenate([v[s * g:]] + early[0:s], axis=0)


def _fwd_in_first(x, vecs, w_in_g, q_idx, ts, deps=()):
    s = x.shape[0]

    def body(q_ref, x_ref, v_ref, w_ref, h1_ref, proj_ref):
        xhat, _ = _rms(x_ref[...])
        h = xhat * v_ref[V_GMIX:V_GMIX + 1, :] * (1.0 + v_ref[V_SC1:V_SC1 + 1, :]) + v_ref[V_SH1:V_SH1 + 1, :]
        hb = h.astype(BF16)
        h1_ref[...] = hb
        proj_ref[...] = _dot(hb, w_ref[...]).astype(BF16)

    return pl.pallas_call(
        lambda q_ref, *refs: body(q_ref, *refs[len(deps):]),
        grid_spec=pltpu.PrefetchScalarGridSpec(
            num_scalar_prefetch=1, grid=(s // ts,),
            in_specs=[_ANY] * len(deps) + [pl.BlockSpec((ts, D), lambda i, q: (i, 0)), _VMEM,
                                           pl.BlockSpec((None, D, C_IN), lambda i, q: (q[0], 0, 0))],
            out_specs=[pl.BlockSpec((ts, D), lambda i, q: (i, 0)), pl.BlockSpec((ts, C_IN), lambda i, q: (i, q[0]))]),
        out_shape=(jax.ShapeDtypeStruct((s, D), BF16), jax.ShapeDtypeStruct((s, D_IN), BF16)),
        compiler_params=_cparams(), name="fwd_in_own")(q_idx, *deps, x, vecs, w_in_g)


def _fwd_in_more(h1, w_in_g, proj, q_idx, ts, name, deps=()):
    s = h1.shape[0]

    def body(q_ref, h1_ref, w_ref, proj_in_ref, proj_ref):
        proj_ref[...] = _dot(h1_ref[...], w_ref[...]).astype(BF16)

    return pl.pallas_call(
        lambda q_ref, *refs: body(q_ref, *refs[len(deps):]),
        grid_spec=pltpu.PrefetchScalarGridSpec(
            num_scalar_prefetch=1, grid=(s // ts,),
            in_specs=[_ANY] * len(deps) + [pl.BlockSpec((ts, D), lambda i, q: (i, 0)),
                                           pl.BlockSpec((None, D, C_IN), lambda i, q: (q[0], 0, 0)), _ANY],
            out_specs=pl.BlockSpec((ts, C_IN), lambda i, q: (i, q[0]))),
        out_shape=jax.ShapeDtypeStruct((s, D_IN), BF16), input_output_aliases={len(deps) + 3: 0},
        compiler_params=_cparams(), name=name)(q_idx, *deps, h1, w_in_g, proj)


def _fwd_mix(proj, x, vecs, w_rga, w_rgx, w_out, ts, deps=()):
    s = x.shape[0]
    g = ts // TIME_BLOCKS

    def body(proj_ref, x_ref, v_ref, wa_ref, wx_ref, wo_ref, x1_ref, mg_ref, z1_ref, kept_ref, decay_ref,
             ua_buf, rx_buf, p_buf, q_buf, c_buf, hcarry):
        i = pl.program_id(0)

        @pl.when(i == 0)
        def _():
            ua_buf[...] = jnp.zeros(ua_buf.shape, F32)
            rx_buf[...] = jnp.zeros(rx_buf.shape, F32)
            hcarry[...] = jnp.zeros((8, D), F32)

        def seg(j):
            return proj_ref[:, j * D:(j + 1) * D].astype(F32)

        def vrow(j):
            return v_ref[j:j + 1, :]

        cb, cc, cx, rx, rg, ga, gb = (seg(j) for j in range(7))
        ua = cc * cx
        ua_late = _late_blocks(ua, ua_buf, g)
        rx_late = _late_blocks(rx, rx_buf, g)
        va = vrow(V_WA2) * ua + vrow(V_WA1) * _earlier(ua, 1, ua_late, g) + vrow(V_WA0) * _earlier(ua, 2, ua_late, g)
        u = (vrow(V_WB3) * rx + vrow(V_WB2) * _earlier(rx, 1, rx_late, g) + vrow(V_WB1) * _earlier(rx, 2, rx_late, g)
             + vrow(V_WB0) * _earlier(rx, 3, rx_late, g) + vrow(V_CBB))

        rows = lax.broadcasted_iota(jnp.int32, (ts, D), 0)
        row0 = jnp.logical_and(rows == 0, i == 0)
        r, ig, a, mult = _lru_gates(u, wa_ref, wx_ref, v_ref, row0)
        decay_ref[...] = a
        bx = mult * (ig * u)

        prods, sums = [a[0:g]], [bx[0:g]]
        for k in range(1, TIME_BLOCKS):
            ak = a[k * g:(k + 1) * g]
            sums.append(ak * sums[-1] + bx[k * g:(k + 1) * g])
            prods.append(ak * prods[-1])
        p_buf[...] = prods[-1]
        q_buf[...] = sums[-1]
        state = hcarry[0:1, :]
        for j in range(g):
            c_buf[j:j + 1, :] = state
            state = p_buf[j:j + 1, :] * state + q_buf[j:j + 1, :]
        hcarry[0:1, :] = state
        entering = c_buf[...]
        h = jnp.concatenate([sums[k] + prods[k] * entering for k in range(TIME_BLOCKS)], axis=0)

        gel, dgel = _gelu(rg)
        sga = jax.nn.sigmoid(ga)
        sgb = jax.nn.sigmoid(gb)
        for j, keep in enumerate((va, r, ig, sga, sgb, gel, dgel, mult, u, h)):
            kept_ref[:, j * D:(j + 1) * D] = keep.astype(BF16)
        merged = (sga * (cb * va) + sgb * (h * gel)).astype(BF16)
        mg_ref[...] = merged
        z1 = _dot(merged, wo_ref[...])
        z1_ref[...] = z1.astype(BF16)
        x1_ref[...] = x_ref[...] + vrow(V_GT1) * z1

    row = lambda i: (i, 0)
    return pl.pallas_call(
        _after(deps, body), grid=(s // ts,),
        out_shape=(jax.ShapeDtypeStruct((s, D), F32), jax.ShapeDtypeStruct((s, D), BF16), jax.ShapeDtypeStruct((s, D), BF16),
                   jax.ShapeDtypeStruct((s, N_KEPT * D), BF16), jax.ShapeDtypeStruct((s, D), F32)),
        in_specs=[_ANY] * len(deps) + [pl.BlockSpec((ts, D_IN), row), pl.BlockSpec((ts, D), row), _VMEM, _VMEM, _VMEM, _VMEM],
        out_specs=[pl.BlockSpec((ts, D), row)] * 3 + [pl.BlockSpec((ts, N_KEPT * D), row), pl.BlockSpec((ts, D), row)],
        scratch_shapes=[pltpu.VMEM((2, g + 8, D), F32), pltpu.VMEM((3, g + 8, D), F32), pltpu.VMEM((g, D), F32),
                        pltpu.VMEM((g, D), F32), pltpu.VMEM((g, D), F32), pltpu.VMEM((8, D), F32)],
        compiler_params=_cparams(), name="fwd_mix")(*deps, proj, x, vecs, w_rga, w_rgx, w_out)


def _ffn_loss(x1, target, vecs, w_gu_g, w_dn, ts):
    s = x1.shape[0]

    def body(x1_ref, t_ref, v_ref, wgu_ref, wdn_ref, dx1_ref, h2_ref, act_ref, dz2_ref, dgu_ref, sm_ref):
        @pl.when(pl.program_id(0) == 0)
        def _():
            sm_ref[...] = jnp.zeros((N_SMALL, D), F32)

        def vrow(j):
            return v_ref[j:j + 1, :]

        n_sub = 1
        rows = [slice(k * (ts // n_sub), (k + 1) * (ts // n_sub)) for k in range(n_sub)]
        subs = [dict(r=r, sums={}) for r in rows]

        def stage_norm(t):
            t["x1"] = x1_ref[t["r"], :]
            t["xh1"], t["rstd1"] = _rms(t["x1"])
            t["n2"] = t["xh1"] * vrow(V_GFFN)
            t["h2"] = (t["n2"] * (1.0 + vrow(V_SC2)) + vrow(V_SH2)).astype(BF16)
            h2_ref[t["r"], :] = t["h2"]

        def stage_up(t):
            h2 = t["h2"]
            g = jnp.concatenate([_dot(h2, wgu_ref[0]), _dot(h2, wgu_ref[1])], axis=1)
            t["up"] = jnp.concatenate([_dot(h2, wgu_ref[2]), _dot(h2, wgu_ref[3])], axis=1)
            t["g"] = g
            t["sg"] = jax.nn.sigmoid(g)
            t["silu"] = g * t["sg"]
            t["act"] = (t["silu"] * t["up"]).astype(BF16)
            act_ref[t["r"], :] = t["act"]

        def stage_down_loss(t):
            z2 = _dot(t["act"], wdn_ref[...])
            x2 = t["x1"] + vrow(V_GT2) * z2
            xh2, rstd2 = _rms(x2)
            err = xh2 * vrow(V_GFIN) - t_ref[t["r"], :]
            t["sums"][G_LOSS] = _rowsum((0.5 / D) * err * err)
            dy = err * (1.0 / D)
            t["sums"][G_GFIN] = _rowsum(dy * xh2)
            t["dx2"] = _rms_bwd(dy * vrow(V_GFIN), xh2, rstd2)
            t["sums"][G_GT2] = _rowsum(t["dx2"] * z2)
            t["dz2"] = (vrow(V_GT2) * t["dx2"]).astype(BF16)
            dz2_ref[t["r"], :] = t["dz2"]

        def stage_back_act(t):
            dact = _dot_nt(t["dz2"], wdn_ref[...])
            g, sg = t["g"], t["sg"]
            t["dgate"] = (dact * t["up"] * (sg * (1.0 + g * (1.0 - sg)))).astype(BF16)
            t["dup"] = (dact * t["silu"]).astype(BF16)
            dgu_ref[t["r"], 0:D_FF] = t["dgate"]
            dgu_ref[t["r"], D_FF:2 * D_FF] = t["dup"]

        def stage_back_norm(t):
            dgate, dup = t["dgate"], t["dup"]
            dh2 = (_dot_nt(dgate[:, 0:C_GU], wgu_ref[0]) + _dot_nt(dgate[:, C_GU:2 * C_GU], wgu_ref[1])
                   + _dot_nt(dup[:, 0:C_GU], wgu_ref[2]) + _dot_nt(dup[:, C_GU:2 * C_GU], wgu_ref[3]))
            t["sums"][G_SH2] = _rowsum(dh2)
            t["sums"][G_SC2] = _rowsum(dh2 * t["n2"])
            dn2 = dh2 * (1.0 + vrow(V_SC2))
            t["sums"][G_GFFN] = _rowsum(dn2 * t["xh1"])
            dx1_ref[t["r"], :] = t["dx2"] + _rms_bwd(dn2 * vrow(V_GFFN), t["xh1"], t["rstd1"])

        for stage in (stage_norm, stage_up, stage_down_loss, stage_back_act, stage_back_norm):
            for t in subs:
                stage(t)
        for j in subs[0]["sums"]:
            total = subs[0]["sums"][j]
            for t in subs[1:]:
                total = total + t["sums"][j]
            sm_ref[j:j + 1, :] += total

    row = lambda i: (i, 0)
    return pl.pallas_call(
        body, grid=(s // ts,),
        out_shape=(jax.ShapeDtypeStruct((s, D), F32), jax.ShapeDtypeStruct((s, D), BF16), jax.ShapeDtypeStruct((s, D_FF), BF16),
                   jax.ShapeDtypeStruct((s, D), BF16), jax.ShapeDtypeStruct((s, 2 * D_FF), BF16),
                   jax.ShapeDtypeStruct((N_SMALL, D), F32)),
        in_specs=[pl.BlockSpec((ts, D), row), pl.BlockSpec((ts, D), row), _VMEM, _VMEM, _VMEM],
        out_specs=[pl.BlockSpec((ts, D), row), pl.BlockSpec((ts, D), row), pl.BlockSpec((ts, D_FF), row),
                   pl.BlockSpec((ts, D), row), pl.BlockSpec((ts, 2 * D_FF), row), pl.BlockSpec((N_SMALL, D), lambda i: (0, 0))],
        compiler_params=_cparams(), name="ffn_loss")(x1, target, vecs, w_gu_g, w_dn)


def _bwd_mix(dx1, z1, merged, proj, kept, decay, vecs, w_rga, w_rgx, w_out, small, ts, deps=()):
    s = dx1.shape[0]
    nt = s // ts
    g = ts // TIME_BLOCKS
    assert g % 16 == 0

    def body(dx1_ref, z1_ref, mg_ref, proj_ref, kept_ref, decay_ref, hh_ref, v_ref, wa_ref, wx_ref,
             wo_ref, sm0_ref, dproj_ref, sm_ref, dwa_ref, dwx_ref, dwo_ref,
             h_buf, a_buf, dva_buf, du_buf, p_buf, q_buf, c_buf, lcarry):
        i = pl.program_id(0)
        first_tile = i == nt - 1

        @pl.when(i == 0)
        def _():
            a_buf[...] = jnp.zeros(a_buf.shape, F32)
            dva_buf[...] = jnp.zeros(dva_buf.shape, F32)
            du_buf[...] = jnp.zeros(du_buf.shape, F32)
            lcarry[...] = jnp.zeros((8, D), F32)
            sm_ref[...] = sm0_ref[...]
            dwa_ref[...] = jnp.zeros((HEADS, HB, HB), F32)
            dwx_ref[...] = jnp.zeros((HEADS, HB, HB), F32)
            dwo_ref[...] = jnp.zeros((D, D), F32)

        def seg(j):
            return proj_ref[:, j * D:(j + 1) * D].astype(F32)

        def vrow(j):
            return v_ref[j:j + 1, :]

        def acc(j, val):
            sm_ref[j:j + 1, :] += _rowsum(val)

        cb, cc, cx, rx = (seg(j) for j in range(4))
        ua = cc * cx
        va, r, ig, sga, sgb, gel, dgel, mult, u, h = (kept_ref[:, j * D:(j + 1) * D].astype(F32) for j in range(N_KEPT))
        a = decay_ref[...]
        rows = lax.broadcasted_iota(jnp.int32, (ts, D), 0)
        row0 = jnp.logical_and(rows == 0, first_tile)

        dx1 = dx1_ref[...]
        acc(G_GT1, dx1 * z1_ref[...].astype(F32))
        dz1 = (vrow(V_GT1) * dx1).astype(BF16)
        dwo_ref[...] += _dot_tn(mg_ref[...], dz1)
        dmg = _dot_nt(dz1, wo_ref[...])
        dya = dmg * sga
        dyb = dmg * sgb
        dproj_ref[:, 5 * D:6 * D] = (dya * (cb * va) * (1.0 - sga)).astype(BF16)
        dproj_ref[:, 6 * D:7 * D] = (dyb * (h * gel) * (1.0 - sgb)).astype(BF16)

        dproj_ref[:, 0:D] = (dya * va).astype(BF16)
        dva = dya * cb
        dva_early = _early_blocks(dva, dva_buf, g)
        dva1 = _later(dva, 1, dva_early, g)
        dva2 = _later(dva, 2, dva_early, g)
        dua = vrow(V_WA2) * dva + vrow(V_WA1) * dva1 + vrow(V_WA0) * dva2
        acc(G_WA2, ua * dva)
        acc(G_WA1, ua * dva1)
        acc(G_WA0, ua * dva2)
        dproj_ref[:, D:2 * D] = (dua * cx).astype(BF16)
        dproj_ref[:, 2 * D:3 * D] = (dua * cc).astype(BF16)

        dproj_ref[:, 4 * D:5 * D] = (dyb * h * dgel).astype(BF16)
        a_next = _later(a, 1, _early_blocks(a, a_buf, g), g)
        dh = dyb * gel
        last = TIME_BLOCKS - 1
        prods, sums = {last: a_next[last * g:]}, {last: dh[last * g:]}
        for k in range(last - 1, -1, -1):
            ak = a_next[k * g:(k + 1) * g]
            sums[k] = dh[k * g:(k + 1) * g] + ak * sums[k + 1]
            prods[k] = ak * prods[k + 1]
        p_buf[...] = prods[0]
        q_buf[...] = sums[0]
        state = lcarry[0:1, :]
        for j in range(g - 1, -1, -1):
            c_buf[j:j + 1, :] = state
            state = q_buf[j:j + 1, :] + p_buf[j:j + 1, :] * state
        lcarry[0:1, :] = state
        entering = c_buf[...]
        lam = jnp.concatenate([sums[k] + prods[k] * entering for k in range(TIME_BLOCKS)], axis=0)

        last = lax.broadcasted_iota(jnp.int32, hh_ref.shape, 0) == hh_ref.shape[0] - 1
        h_halo = [jnp.where(first_tile, 0.0, jnp.sum(jnp.where(last, hh_ref[...].astype(F32), 0.0), axis=0, keepdims=True))]
        da = lam * _earlier(h, 1, _late_blocks(h, h_buf, g, h_halo), g)
        dmult = jnp.where(row0, 0.0, lam * (ig * u))
        di = lam * mult * u
        du = lam * mult * ig
        dlog_a = da * a - dmult * (a * a) / mult
        lam_p = vrow(V_LAM)
        dr = dlog_a * (LRU_C * _log_sigmoid(lam_p))
        sm_ref[G_LAM:G_LAM + 1, :] += _rowsum(dlog_a * r) * (LRU_C * jax.nn.sigmoid(-lam_p))
        dpa = dr * r * (1.0 - r)
        dpx = di * ig * (1.0 - ig)
        acc(G_BA, dpa)
        acc(G_BX, dpx)
        dpab = dpa.astype(BF16)
        dpxb = dpx.astype(BF16)
        ub = u.astype(BF16)
        back = []
        for hd in range(HEADS):
            cols = slice(hd * HB, (hd + 1) * HB)
            back.append(_dot_nt(dpab[:, cols], wa_ref[hd]) + _dot_nt(dpxb[:, cols], wx_ref[hd]))
            dwa_ref[hd] += _dot_tn(ub[:, cols], dpab[:, cols])
            dwx_ref[hd] += _dot_tn(ub[:, cols], dpxb[:, cols])
        du = du + jnp.concatenate(back, axis=1)

        acc(G_CBB, du)
        du_early = _early_blocks(du, du_buf, g)
        du1 = _later(du, 1, du_early, g)
        du2 = _later(du, 2, du_early, g)
        du3 = _later(du, 3, du_early, g)
        dproj_ref[:, 3 * D:4 * D] = (vrow(V_WB3) * du + vrow(V_WB2) * du1 + vrow(V_WB1) * du2 + vrow(V_WB0) * du3).astype(BF16)
        acc(G_WB3, rx * du)
        acc(G_WB2, rx * du1)
        acc(G_WB1, rx * du2)
        acc(G_WB0, rx * du3)

    rev = lambda i: (nt - 1 - i, 0)
    h_halo16 = lambda i: (jnp.maximum((nt - 1 - i) * (ts // 16) - 1, 0), N_KEPT - 1)
    const2 = lambda i: (0, 0)
    const3 = lambda i: (0, 0, 0)
    return pl.pallas_call(
        _after(deps, body), grid=(nt,),
        out_shape=(jax.ShapeDtypeStruct((s, D_IN), BF16), jax.ShapeDtypeStruct((N_SMALL, D), F32),
                   jax.ShapeDtypeStruct((HEADS, HB, HB), F32), jax.ShapeDtypeStruct((HEADS, HB, HB), F32),
                   jax.ShapeDtypeStruct((D, D), F32)),
        in_specs=[_ANY] * len(deps) + [pl.BlockSpec((ts, D), rev), pl.BlockSpec((ts, D), rev), pl.BlockSpec((ts, D), rev),
                  pl.BlockSpec((ts, 4 * D), rev), pl.BlockSpec((ts, N_KEPT * D), rev), pl.BlockSpec((ts, D), rev),
                  pl.BlockSpec((16, D), h_halo16), _VMEM, _VMEM, _VMEM, _VMEM, _VMEM],
        out_specs=[pl.BlockSpec((ts, D_IN), rev), pl.BlockSpec((N_SMALL, D), const2),
                   pl.BlockSpec((HEADS, HB, HB), const3), pl.BlockSpec((HEADS, HB, HB), const3), pl.BlockSpec((D, D), const2)],
        scratch_shapes=[pltpu.VMEM((1, g + 8, D), F32), pltpu.VMEM((1, g + 8, D), F32),
                        pltpu.VMEM((2, g + 8, D), F32), pltpu.VMEM((3, g + 8, D), F32), pltpu.VMEM((g, D), F32),
                        pltpu.VMEM((g, D), F32), pltpu.VMEM((g, D), F32), pltpu.VMEM((8, D), F32)],
        compiler_params=_cparams(), name="bwd_mix")(*deps, dx1, z1, merged, proj, kept, decay, kept, vecs, w_rga,
                                                    w_rgx, w_out, small)


def _bwd_in(dproj, x, dx1, vecs, w_in_g, small, ts, deps=()):
    s = x.shape[0]

    def body(dp_ref, x_ref, dx1_ref, v_ref, w_ref, sm0_ref, gx_ref, sm_ref):
        @pl.when(pl.program_id(0) == 0)
        def _():
            sm_ref[...] = sm0_ref[...]

        def vrow(j):
            return v_ref[j:j + 1, :]

        dh1 = _dot_nt(dp_ref[:, 0:C_IN], w_ref[0])
        for k in range(1, N_CHIPS):
            dh1 += _dot_nt(dp_ref[:, k * C_IN:(k + 1) * C_IN], w_ref[k])
        xh, rstd = _rms(x_ref[...])
        sm_ref[G_SH1:G_SH1 + 1, :] += _rowsum(dh1)
        sm_ref[G_SC1:G_SC1 + 1, :] += _rowsum(dh1 * (xh * vrow(V_GMIX)))
        dn1 = dh1 * (1.0 + vrow(V_SC1))
        sm_ref[G_GMIX:G_GMIX + 1, :] += _rowsum(dn1 * xh)
        gx_ref[...] = dx1_ref[...] + _rms_bwd(dn1 * vrow(V_GMIX), xh, rstd)

    row = lambda i: (i, 0)
    return pl.pallas_call(
        _after(deps, body), grid=(s // ts,),
        out_shape=(jax.ShapeDtypeStruct((s, D), F32), jax.ShapeDtypeStruct((N_SMALL, D), F32)),
        in_specs=[_ANY] * len(deps) + [pl.BlockSpec((ts, D_IN), row), pl.BlockSpec((ts, D), row), pl.BlockSpec((ts, D), row),
                                       _VMEM, _VMEM, _VMEM],
        out_specs=[pl.BlockSpec((ts, D), row), pl.BlockSpec((N_SMALL, D), lambda i: (0, 0))],
        compiler_params=_cparams(), name="bwd_in")(*deps, dproj, x, dx1, vecs, w_in_g, small)


def _grad_w(a, b, n_col_blocks, ts, name, deps=()):
    s, m = a.shape
    tn = b.shape[1] // n_col_blocks
    n_steps = s // ts

    def body(a_ref, b_ref, o_ref, acc_ref):
        k = pl.program_id(1)

        @pl.when(k == 0)
        def _():
            acc_ref[...] = jnp.zeros((m, tn), F32)

        acc_ref[...] += _dot_tn(a_ref[...], b_ref[...])

        @pl.when(k == n_steps - 1)
        def _():
            o_ref[...] = acc_ref[...].astype(BF16)

    return pl.pallas_call(
        _after(deps, body), grid=(n_col_blocks, n_steps),
        out_shape=jax.ShapeDtypeStruct((n_col_blocks, m, tn), BF16),
        in_specs=[_ANY] * len(deps) + [pl.BlockSpec((ts, m), lambda n, k: (k, 0)), pl.BlockSpec((ts, tn), lambda n, k: (k, n))],
        out_specs=pl.BlockSpec((None, m, tn), lambda n, k: (n, 0, 0)),
        scratch_shapes=[pltpu.VMEM((m, tn), F32)],
        compiler_params=_cparams(2), name=name)(*deps, a, b)


def _ada_fwd(c_all, w_ada, b_ada):
    n = w_ada.shape[1]

    def body(c_ref, w_ref, b_ref, o_ref, ca_ref):
        c = c_ref[...]
        ca = c * jax.nn.sigmoid(c)
        ca_ref[...] = ca
        o_ref[...] = jnp.dot(ca, w_ref[...], preferred_element_type=F32, precision=lax.Precision.HIGHEST) + b_ref[...]

    return pl.pallas_call(
        body, out_shape=(jax.ShapeDtypeStruct((N_DEV, n), F32), jax.ShapeDtypeStruct((N_DEV, D), F32)),
        in_specs=[_VMEM] * 3, out_specs=[_VMEM] * 2, compiler_params=_cparams(0), name="ada_fwd")(c_all, w_ada, b_ada)


def _sum_small(parts):
    def body(p_ref, o_ref, d_ref):
        tot = p_ref[0]
        for dev in range(1, N_DEV):
            tot = tot + p_ref[dev]
        o_ref[...] = tot
        d_ref[...] = p_ref[:, 0:8, :]

    return pl.pallas_call(
        body, out_shape=(jax.ShapeDtypeStruct((N_SMALL, D), F32), jax.ShapeDtypeStruct((N_DEV, 8, D), F32)),
        in_specs=[_VMEM], out_specs=[_VMEM] * 2, compiler_params=_cparams(0), name="sum_small")(parts)


def _adamw_small(items, name):
    n = len(items)

    def body(*refs):
        ins, outs = refs[:4 * n], refs[4 * n:]
        for k in range(n):
            w_ref, g_ref, m_ref, v_ref = ins[4 * k:4 * k + 4]
            go_ref, d_ref, nm_ref, nv_ref = outs[4 * k:4 * k + 4]
            pieces = g_ref.shape[0] if g_ref.shape != w_ref.shape else 1
            for j in range(pieces):
                at = (slice(None), slice(j * D, (j + 1) * D)) if pieces > 1 else (slice(None), slice(None))
                g_ = g_ref[j:j + 1, :] if pieces > 1 else g_ref[...]
                go_ref[at] = g_
                m_ = ADAM_B1 * m_ref[at] + (1.0 - ADAM_B1) * g_
                v_ = ADAM_B2 * v_ref[at] + (1.0 - ADAM_B2) * (g_ * g_)
                nm_ref[at] = m_
                nv_ref[at] = v_
                m_hat = m_ / (1.0 - ADAM_B1 ** ADAM_STEP)
                v_hat = v_ / (1.0 - ADAM_B2 ** ADAM_STEP)
                d_ref[at] = -ADAM_LR * (m_hat / (jnp.sqrt(v_hat) + ADAM_EPS) + ADAM_WD * w_ref[at])

    out = pl.pallas_call(
        body, out_shape=tuple(jax.ShapeDtypeStruct(it[0].shape, F32) for it in items for _ in range(4)),
        in_specs=[_VMEM] * (4 * n), out_specs=[_VMEM] * (4 * n), name=name)(*[a for it in items for a in it])
    return [tuple(out[4 * k:4 * k + 4]) for k in range(n)]


HALF_STEPS = 4


def _adamw_halves(sets, c_idx, name, deps=()):
    nh = HALF_STEPS
    n = len(sets)

    def body(c_ref, *refs):
        refs = refs[len(deps):]
        ins, outs = refs[:5 * n], refs[5 * n:]
        for k in range(n):
            w_ref, mine_ref, other_ref, m_ref, v_ref = ins[5 * k:5 * k + 5]
            g_ref, d_ref, nm_ref, nv_ref = outs[4 * k:4 * k + 4]
            g_ = jnp.where(pl.program_id(0) // nh == c_ref[0], mine_ref[...], other_ref[...])
            g_ref[...] = g_
            m_ = ADAM_B1 * m_ref[...] + (1.0 - ADAM_B1) * g_
            v_ = ADAM_B2 * v_ref[...] + (1.0 - ADAM_B2) * (g_ * g_)
            nm_ref[...] = m_
            nv_ref[...] = v_
            m_hat = m_ / (1.0 - ADAM_B1 ** ADAM_STEP)
            v_hat = v_ / (1.0 - ADAM_B2 ** ADAM_STEP)
            d_ref[...] = -ADAM_LR * (m_hat / (jnp.sqrt(v_hat) + ADAM_EPS) + ADAM_WD * w_ref[...])

    in_specs, out_specs, out_shape = [], [], []
    for w, mine, _, _, _ in sets:
        r2, cols = mine.shape
        block = (r2 // nh, cols)
        full = pl.BlockSpec(block, lambda i, c: (i, 0))
        in_specs += [full, pl.BlockSpec(block, lambda i, c: (jnp.clip(i - c[0] * nh, 0, nh - 1), 0)),
                     pl.BlockSpec(block, lambda i, c: (jnp.clip(i - (1 - c[0]) * nh, 0, nh - 1), 0)), full, full]
        out_specs += [full] * 4
        out_shape += [jax.ShapeDtypeStruct((2 * r2, cols), F32)] * 4
    out = pl.pallas_call(
        body,
        grid_spec=pltpu.PrefetchScalarGridSpec(num_scalar_prefetch=1, grid=(2 * nh,),
                                               in_specs=[_ANY] * len(deps) + in_specs, out_specs=out_specs),
        out_shape=tuple(out_shape), compiler_params=_cparams(), name=name,
    )(c_idx, *deps, *[a for s in sets for a in s])
    return [tuple(out[4 * k:4 * k + 4]) for k in range(n)]


def _adamw_ada(w, c_act, dmod, m, v):
    rows, n = w.shape
    tr = 128

    def body(c_ref, d_ref, w_ref, m_ref, v_ref, g_ref, dl_ref, nm_ref, nv_ref):
        g_ = lax.dot_general(c_ref[...], d_ref[...], (((0,), (0,)), ((), ())), preferred_element_type=F32,
                             precision=lax.Precision.HIGHEST)
        g_ref[...] = g_
        m_ = ADAM_B1 * m_ref[...] + (1.0 - ADAM_B1) * g_
        v_ = ADAM_B2 * v_ref[...] + (1.0 - ADAM_B2) * (g_ * g_)
        nm_ref[...] = m_
        nv_ref[...] = v_
        m_hat = m_ / (1.0 - ADAM_B1 ** ADAM_STEP)
        v_hat = v_ / (1.0 - ADAM_B2 ** ADAM_STEP)
        dl_ref[...] = -ADAM_LR * (m_hat / (jnp.sqrt(v_hat) + ADAM_EPS) + ADAM_WD * w_ref[...])

    spec = pl.BlockSpec((tr, n), lambda i: (i, 0))
    return pl.pallas_call(
        body, grid=(rows // tr,), out_shape=(jax.ShapeDtypeStruct((rows, n), F32),) * 4,
        in_specs=[pl.BlockSpec((N_DEV, tr), lambda i: (0, i)), _VMEM, spec, spec, spec], out_specs=[spec] * 4,
        compiler_params=_cparams(), name="adamw_w_ada")(c_act, dmod, w, m, v)


def _add_halves(grads, recvs, c_idx, name):
    nw = len(grads)

    def body(c_ref, *refs):
        for g_ref, r_ref, o_ref in zip(refs[:nw], refs[nw:2 * nw], refs[2 * nw:]):
            o_ref[...] = (g_ref[...].astype(F32) + r_ref[...].astype(F32)).astype(BF16)

    mine = [pl.BlockSpec((None, None) + g.shape[2:], lambda k, c: (k, c[0], 0, 0)) for g in grads]
    whole = [pl.BlockSpec((None,) + g.shape[2:], lambda k, c: (k, 0, 0)) for g in grads]
    return pl.pallas_call(
        body,
        grid_spec=pltpu.PrefetchScalarGridSpec(num_scalar_prefetch=1, grid=(N_CHIPS,), in_specs=mine + whole, out_specs=whole),
        out_shape=tuple(jax.ShapeDtypeStruct((N_CHIPS,) + g.shape[2:], BF16) for g in grads),
        compiler_params=_cparams(), name=name)(c_idx, *grads, *recvs)


def _sum_chips(owns, others, chip_idx, name):
    nw = len(owns)
    steps = 4 if all(a.shape[1] % 64 == 0 for a in owns) else 2

    def body(p_ref, *refs):
        for own_ref, got_ref, o_ref in zip(refs[:nw], refs[nw:2 * nw], refs[2 * nw:]):
            o_ref[...] = (((own_ref[...].astype(F32) + got_ref[0].astype(F32)) + got_ref[1].astype(F32))
                          + got_ref[2].astype(F32))

    blocks = [(a.shape[1] // steps, a.shape[2]) for a in owns]
    return pl.pallas_call(
        body,
        grid_spec=pltpu.PrefetchScalarGridSpec(
            num_scalar_prefetch=1, grid=(steps,),
            in_specs=([pl.BlockSpec((None,) + b, lambda i, p: (p[0], i, 0)) for b in blocks]
                      + [pl.BlockSpec((N_CHIPS - 1,) + b, lambda i, p: (0, i, 0)) for b in blocks]),
            out_specs=[pl.BlockSpec(b, lambda i, p: (i, 0)) for b in blocks]),
        out_shape=tuple(jax.ShapeDtypeStruct(a.shape[1:], F32) for a in owns), compiler_params=_cparams(),
        name=name)(chip_idx, *owns, *others)


def _place():
    x, y, c = lax.axis_index("x"), lax.axis_index("y"), lax.axis_index("c")
    return x, y, c, 2 * x + y


def _flip(v, bit):
    return 1 - v if bit else v


def _allgather8(v, name, deps=()):
    r, n = v.shape

    def body(*refs):
        v_ref, out_ref, send_sems, recv_sems, local_sem = refs[len(deps):]
        x, y, c, _ = _place()
        me = 4 * x + 2 * y + c
        mine = pltpu.make_async_copy(v_ref, out_ref.at[me], local_sem)
        mine.start()
        sends = []
        for rel in range(1, N_DEV):
            peer = (_flip(x, rel & 4), _flip(y, rel & 2), _flip(c, rel & 1))
            cp = pltpu.make_async_remote_copy(v_ref, out_ref.at[me], send_sems.at[rel - 1], recv_sems.at[rel - 1],
                                              device_id=peer, device_id_type=MESH)
            cp.start()
            sends.append(cp)
        for rel in range(1, N_DEV):
            peer = (_flip(x, rel & 4), _flip(y, rel & 2), _flip(c, rel & 1))
            peer_idx = 4 * peer[0] + 2 * peer[1] + peer[2]
            pltpu.make_async_remote_copy(v_ref, out_ref.at[peer_idx], send_sems.at[rel - 1], recv_sems.at[rel - 1],
                                         device_id=peer, device_id_type=MESH).wait_recv()
        for cp in sends:
            cp.wait_send()
        mine.wait()

    return pl.pallas_call(
        body, out_shape=jax.ShapeDtypeStruct((N_DEV, r, n), F32), in_specs=[_ANY] * len(deps) + [_VMEM], out_specs=_VMEM,
        scratch_shapes=[pltpu.SemaphoreType.DMA((N_DEV - 1,)), pltpu.SemaphoreType.DMA((N_DEV - 1,)), pltpu.SemaphoreType.DMA(())],
        name=name)(*deps, v)


_HBM = pl.BlockSpec(memory_space=pltpu.HBM)
_SEM = pl.BlockSpec(memory_space=pltpu.SEMAPHORE)
_EFFECT = pltpu.SideEffectType.DATAFLOW_SIDE_EFFECTING


def _xchg_start(name, plan, n_copies, srcs, lands, after=(), sibling_id=None):
    bufs = list(srcs) + list(lands)
    ns, nb = len(srcs), len(srcs) + len(lands)

    def body(*refs):
        send_sems, recv_sems, token = refs[nb + len(after)], refs[nb + len(after) + 1], refs[-1]
        if sibling_id is not None:
            x, y, c, _ = _place()
            barrier = pltpu.get_barrier_semaphore()
            pl.semaphore_signal(barrier, inc=1, device_id=(x, y, 1 - c), device_id_type=MESH)
            pl.semaphore_wait(barrier, 1)
        for i, (src, dst, peer, _) in enumerate(plan(_place(), refs[:ns], refs[ns:nb])):
            pltpu.make_async_remote_copy(src, dst, send_sems.at[i], recv_sems.at[i], device_id=peer, device_id_type=MESH).start()
        token[...] = jnp.zeros_like(token)

    out = pl.pallas_call(
        body, name=name,
        out_shape=(pltpu.SemaphoreType.DMA((n_copies,)), pltpu.SemaphoreType.DMA((n_copies,)),
                   *[pltpu.HBM(a.shape, a.dtype) for a in bufs], jax.ShapeDtypeStruct((8, 128), F32)),
        in_specs=[_HBM] * nb + [_ANY] * len(after), out_specs=(_SEM, _SEM, *[_HBM] * nb, _VMEM),
        input_output_aliases={i: 2 + i for i in range(nb)},
        compiler_params=pltpu.CompilerParams(has_side_effects=_EFFECT, collective_id=sibling_id),
    )(*[pltpu.with_memory_space_constraint(a, pltpu.HBM) for a in bufs], *after)
    return (out[0], out[1]), out[2:2 + ns], out[2 + ns:2 + nb], out[-1]


def _xchg_wait(name, plan, sems, srcs, lands, after, sem_ids=None):
    bufs = list(srcs) + list(lands)
    ns, nb = len(srcs), len(srcs) + len(lands)

    def body(*refs):
        send_sems, recv_sems = refs[nb], refs[nb + 1]
        copies = plan(_place(), refs[:ns], refs[ns:nb])
        ids = range(len(copies)) if sem_ids is None else sem_ids
        for i, (src, _, peer, mine) in zip(ids, copies, strict=True):
            if i is not None:
                cp = pltpu.make_async_remote_copy(src, mine, send_sems.at[i], recv_sems.at[i], device_id=peer,
                                                  device_id_type=MESH)
                cp.wait_send()
                cp.wait_recv()

    out = pl.pallas_call(
        body, name=name, out_shape=tuple(pltpu.HBM(a.shape, a.dtype) for a in bufs),
        in_specs=[_HBM] * nb + [_SEM, _SEM] + [_ANY] * len(after), out_specs=tuple([_HBM] * nb),
        input_output_aliases={i: i for i in range(nb)},
        compiler_params=pltpu.CompilerParams(has_side_effects=_EFFECT),
    )(*bufs, *sems, *after)
    return out[:ns], out[ns:]


def _other_chips(place, which=(1, 2, 3)):
    x, y, c, _ = place
    return [((_flip(x, j & 2), _flip(y, j & 1), c), 2 * _flip(x, j & 2) + _flip(y, j & 1)) for j in which]


def _plan_gather_ici(chips):
    def plan(place, src_refs, land_refs):
        _, _, c, p = place
        return [(s.at[c], l.at[p, c], peer, l.at[q, c]) for s, l, which in zip(src_refs, land_refs, chips, strict=True)
                for peer, q in _other_chips(place, which)]
    return plan


def _plan_relay(which):
    def plan(place, src_refs, land_refs):
        x, y, c, _ = place
        return [(l.at[q, c], l.at[q, c], (x, y, 1 - c), l.at[q, 1 - c]) for l in land_refs for _, q in _other_chips(place, which)]
    return plan


def _plan_swap(place, src_refs, land_refs):
    x, y, c, _ = place
    return [(s.at[k, 1 - c], l.at[k], (x, y, 1 - c), l.at[k]) for s, l in zip(src_refs, land_refs) for k in range(N_CHIPS)]


def _plan_scatter(place, src_refs, land_refs):
    return [(s.at[q], l.at[j], peer, l.at[j]) for s, l in zip(src_refs, land_refs)
            for j, (peer, q) in enumerate(_other_chips(place))]


def _plan_share(place, src_refs, land_refs):
    x, y, c, _ = place
    return [(s, l, (x, y, 1 - c), l) for s, l in zip(src_refs, land_refs)]


def _plan_gather8(place, src_refs, land_refs):
    x, y, c, _ = place
    me = 4 * x + 2 * y + c
    copies = []
    for s, l in zip(src_refs, land_refs):
        for rel in range(1, N_DEV):
            peer = (_flip(x, rel & 4), _flip(y, rel & 2), _flip(c, rel & 1))
            copies.append((s, l.at[me], peer, l.at[4 * peer[0] + 2 * peer[1] + peer[2]]))
    return copies


def _pack_rows(parts, n_rows, name, deps=()):
    def body(*refs):
        refs = refs[len(deps):]
        out_ref = refs[-1]
        out_ref[...] = jnp.zeros((n_rows, D), F32)
        at = 0
        for ref in refs[:-1]:
            k = ref.shape[0]
            out_ref[at:at + k, :] = ref[...]
            at += k

    return pl.pallas_call(
        body, out_shape=jax.ShapeDtypeStruct((n_rows, D), F32), in_specs=[_ANY] * len(deps) + [_VMEM] * len(parts),
        out_specs=_VMEM, name=name)(*deps, *parts)


TS_MM = 512
TS_IN = 1024
TS_GW = 2048
TS_GW_WIDE = 1024
TS_MIX = 256


def _halved(a):
    n, r, cols = a.shape
    return a.reshape(n, 2, r // 2, cols)


SIBLING_IDS = (1, 2)


def _rs_swap(name, grads, after=()):
    lands = [lax.empty((N_CHIPS,) + g.shape[2:], g.dtype) for g in grads]
    sems, grads, lands, token = _xchg_start(name + "_swap", _plan_swap, N_CHIPS * len(grads), grads, lands, after,
                                            sibling_id=SIBLING_IDS[0])
    return name, sems, grads, lands, token


def _rs_scatter(handle, after, chip, ci):
    name, sems, grads, lands, _ = handle
    grads, from_sibling = _xchg_wait(name + "_swap_wait", _plan_swap, sems, grads, lands, after)
    c_arr = jnp.reshape(ci, (1,)).astype(jnp.int32)
    pair_sums = _add_halves(list(grads), list(from_sibling), c_arr, name + "_add_halves")
    lands = [lax.empty((N_CHIPS - 1,) + p.shape[1:], p.dtype) for p in pair_sums]
    sems, pair_sums, lands, token = _xchg_start(name + "_scatter", _plan_scatter, 3 * len(pair_sums), pair_sums, lands)
    return name, sems, pair_sums, lands, jnp.reshape(chip, (1,)).astype(jnp.int32), token


def _rs_share(handle, after):
    name, sems, pair_sums, lands, chip_idx, _ = handle
    pair_sums, received = _xchg_wait(name + "_scatter_wait", _plan_scatter, sems, pair_sums, lands, after)
    halves = _sum_chips(list(pair_sums), list(received), chip_idx, name + "_sum_chips")
    lands = [lax.empty(h.shape, h.dtype) for h in halves]
    sems, halves, lands, token = _xchg_start(name + "_share", _plan_share, len(halves), halves, lands,
                                             sibling_id=SIBLING_IDS[1])
    return name, sems, halves, lands, token


def _rs_end(handle, after):
    name, sems, halves, lands, _ = handle
    halves, others = _xchg_wait(name + "_share_wait", _plan_share, sems, halves, lands, after)
    return list(zip(halves, others))


def kernel(x, c, w_ada, b_ada, g_norm_mix, w_in, conv_a_w, conv_b_w, conv_b_bias, w_rg_a, b_rg_a, w_rg_x, b_rg_x, lru_lambda, w_out, g_norm_ffn, w_gate_up, w_down, g_norm_final, loss_target, m_w_ada, m_b_ada, m_g_norm_mix, m_w_in, m_conv_a_w, m_conv_b_w, m_conv_b_bias, m_w_rg_a, m_b_rg_a, m_w_rg_x, m_b_rg_x, m_lru_lambda, m_w_out, m_g_norm_ffn, m_w_gate_up, m_w_down, m_g_norm_final, v_w_ada, v_b_ada, v_g_norm_mix, v_w_in, v_conv_a_w, v_conv_b_w, v_conv_b_bias, v_w_rg_a, v_b_rg_a, v_w_rg_x, v_b_rg_x, v_lru_lambda, v_w_out, v_g_norm_ffn, v_w_gate_up, v_w_down, v_g_norm_final):
    xi, yi, ci = lax.axis_index("x"), lax.axis_index("y"), lax.axis_index("c")
    chip = 2 * xi + yi
    me = 2 * chip + ci
    n_ada = w_ada.shape[2]

    def widen(w):
        return jnp.pad(w, ((0, 0), (0, D - w.shape[1])))

    got = _allgather8(_pack_rows([c, widen(conv_a_w[0]), widen(conv_b_w[0])], 8, "pack_c_conv"), "gather_c_conv")
    c_all = got[:, 0, :]
    conv_full = got[::2, 1:8, :D // N_CHIPS].transpose(1, 0, 2).reshape(7, D)

    mod_part, c_act = _ada_fwd(c_all, w_ada[0], lax.dynamic_slice_in_dim(b_ada, chip * n_ada, n_ada, axis=1))
    mod_zone = lax.dynamic_update_index_in_dim(lax.empty((N_DEV,) + mod_part.shape, F32), mod_part, me, 0)
    mod_sems, mod_src, mod_zone, _ = _xchg_start("gather_mod", _plan_gather8, N_DEV - 1, [mod_part], [mod_zone])

    def rg_shard(w):
        return w[0].astype(BF16).reshape(2, HEADS * HB // N_CHIPS // 2, HB)

    shards = [w_in[0].astype(BF16).reshape(2, D // 2, C_IN), rg_shard(w_rg_a), rg_shard(w_rg_x),
              w_out[0].astype(BF16).reshape(2, D // N_CHIPS // 2, D), w_gate_up[0].astype(BF16).reshape(2, D // 2, C_GU),
              w_down[0].astype(BF16).reshape(2, D_FF // N_CHIPS // 2, D)]
    lands = [lax.dynamic_update_index_in_dim(lax.empty((N_CHIPS,) + s.shape, s.dtype), s, chip, 0) for s in shards]

    mod_all = _xchg_wait("gather_mod_wait", _plan_gather8, mod_sems, mod_src, mod_zone, lands)[1][0]
    mod_mine = lax.dynamic_index_in_dim(mod_all, me, axis=1, keepdims=False)[::2].reshape(6, D)
    vecs = _pack_rows([mod_mine, g_norm_mix, g_norm_ffn, g_norm_final.reshape(1, D), conv_b_bias, b_rg_a, b_rg_x, lru_lambda,
                       conv_full], N_VEC, "pack_vecs")

    def send(name, first, last, after, chips):
        copies = [(k, j) for k, which in zip(range(first, last), chips, strict=True) for j in which]
        sems, srcs, zone, token = _xchg_start(name + "_ici", _plan_gather_ici(chips), len(copies), shards[first:last],
                                              lands[first:last], after)
        shards[first:last], lands[first:last] = srcs, zone
        return sems, copies, token

    def arrive(name, sent, first, last, after):
        sems, copies, _ = sent
        chips = [tuple(j for k, j in copies if k == want) for want in range(first, last)]
        ids = [copies.index((k, j)) for k, which in zip(range(first, last), chips) for j in which]
        srcs, zone = _xchg_wait(name + "_ici_wait", _plan_gather_ici(chips), sems, shards[first:last], lands[first:last], after,
                                ids)
        shards[first:last], lands[first:last] = srcs, zone

    def relay(name, first, last, which, sibling_id):
        plan = _plan_relay(which)
        sems, _, zone, token = _xchg_start(name + "_d2d", plan, len(which) * (last - first), [], lands[first:last],
                                           sibling_id=sibling_id)
        lands[first:last] = zone
        return name, plan, sems, first, last, token

    def relayed(handle, after):
        name, plan, sems, first, last, _ = handle
        lands[first:last] = _xchg_wait(name + "_d2d_wait", plan, sems, [], lands[first:last], after)[1]

    def to_blocks(v):
        return v.reshape(-1, TS_MIX // TIME_BLOCKS, TIME_BLOCKS, D).transpose(0, 2, 1, 3).reshape(v.shape)

    def from_blocks(v):
        return v.reshape(-1, TIME_BLOCKS, TS_MIX // TIME_BLOCKS, D).transpose(0, 2, 1, 3).reshape(v.shape)

    def chip_index(j):
        return jnp.reshape(chip ^ j, (1,)).astype(jnp.int32)

    def wg_in():
        return lands[0].reshape(N_CHIPS, D, C_IN)

    xs, target = to_blocks(x[0]), to_blocks(loss_target[0])
    sent_near = send("gather_in_near", 0, 1, [vecs], [(1, 2)])
    ts_in = min(TS_IN, xs.shape[0])
    h1, proj = _fwd_in_first(xs, vecs, wg_in(), chip_index(0), ts_in, deps=[sent_near[-1]])
    arrive("gather_in_near", sent_near, 0, 1, [proj])
    near = relay("gather_in_near", 0, 1, (1, 2), SIBLING_IDS[0])
    sent_rest = send("gather_rest", 0, 6, [near[-1]], [(3,)] + [(1, 2, 3)] * 5)
    relayed(near, [sent_rest[-1]])
    proj = _fwd_in_more(h1, wg_in(), proj, chip_index(1), ts_in, "fwd_in_y")
    proj = _fwd_in_more(h1, wg_in(), proj, chip_index(2), ts_in, "fwd_in_x")
    arrive("gather_in_far", sent_rest, 0, 1, [proj])
    far = relay("gather_in_far", 0, 1, (3,), SIBLING_IDS[1])
    arrive("gather_mix", sent_rest, 1, 4, [far[-1]])
    relayed(far, [far[-1]])
    mix = relay("gather_mix", 1, 4, (1, 2, 3), SIBLING_IDS[0])
    proj = _fwd_in_more(h1, wg_in(), proj, chip_index(3), ts_in, "fwd_in_xy", deps=[mix[-1]])
    relayed(mix, [proj])
    wg_rga, wg_rgx, wg_out = lands[1:4]
    wg_out = wg_out.reshape(D, D)

    def rg_full(wg):
        return wg.reshape(N_CHIPS, HEADS, HB // N_CHIPS, HB).transpose(1, 0, 2, 3).reshape(HEADS, HB, HB)

    wg_rga, wg_rgx = rg_full(wg_rga), rg_full(wg_rgx)

    x1, merged, z1, kept, decay = _fwd_mix(proj, xs, vecs, wg_rga, wg_rgx, wg_out, TS_MIX)
    arrive("gather_ffn", sent_rest, 4, 6, [x1])
    ffn = relay("gather_ffn", 4, 6, (1, 2, 3), SIBLING_IDS[1])
    relayed(ffn, [ffn[-1]])
    wg_gu, wg_dn = lands[4:6]
    wg_gu, wg_dn = wg_gu.reshape(N_CHIPS, D, C_GU), wg_dn.reshape(D_FF, D)
    dx1, h2, act, dz2, dgu, sm_ffn = _ffn_loss(x1, target, vecs, wg_gu, wg_dn, TS_MIX)

    def rg_chunks(dw):
        return _halved(dw.reshape(HEADS, N_CHIPS, HB // N_CHIPS, HB).transpose(1, 0, 2, 3).reshape(N_CHIPS, HB, HB).astype(BF16))

    ts_gw = min(TS_GW, xs.shape[0])
    g_dn = _grad_w(act, dz2, 1, min(TS_GW_WIDE, xs.shape[0]), "grad_w_down")
    g_gu = _grad_w(h2, dgu, N_CHIPS, ts_gw, "grad_w_gate_up")
    rs_b = _rs_swap("rs_b", [_halved(g_gu), _halved(g_dn.reshape(N_CHIPS, D_FF // N_CHIPS, D))])
    dproj, sm_mix, dw_rga, dw_rgx, dw_out = _bwd_mix(dx1, z1, merged, proj, kept, decay, vecs, wg_rga, wg_rgx, wg_out, sm_ffn, TS_MIX,
                                                     deps=[rs_b[-1]])
    rs_b = _rs_scatter(rs_b, [dproj], chip, ci)
    g_in = _grad_w(h1, dproj, N_CHIPS, ts_gw, "grad_w_in", deps=[rs_b[-1]])
    rs_b = _rs_share(rs_b, [g_in])
    rs_a = _rs_swap("rs_a", [_halved(g_in), rg_chunks(dw_rga), rg_chunks(dw_rgx),
                             _halved(dw_out.astype(BF16).reshape(N_CHIPS, D // N_CHIPS, D))], after=[rs_b[-1]])

    c_arr = jnp.reshape(ci, (1,)).astype(jnp.int32)

    def step_halves(name, items, deps=()):
        two_d = lambda a: a.reshape(-1, a.shape[-1])
        sets = [(two_d(w), halves[0], halves[1], two_d(m), two_d(v)) for w, halves, m, v in items.values()]
        for (n, (w, _, _, _)), out in zip(items.items(), _adamw_halves(sets, c_arr, name, deps)):
            res[n] = tuple(a.reshape(w.shape) for a in out)

    def shard_cols(row_block):
        return lax.dynamic_slice_in_dim(row_block, chip * (D // N_CHIPS), D // N_CHIPS, axis=1)

    gw_gu, gw_dn = _rs_end(rs_b, [rs_a[-1]])
    res = {}
    step_halves("adamw_ffn", {"w_gate_up": (w_gate_up, gw_gu, m_w_gate_up, v_w_gate_up),
                              "w_down": (w_down, gw_dn, m_w_down, v_w_down)}, [rs_a[-1]])
    rs_a = _rs_scatter(rs_a, [res["w_gate_up"][1], res["w_down"][1]], chip, ci)
    grad_x, sm_in = _bwd_in(dproj, xs, dx1, vecs, wg_in(), sm_mix, TS_MM, deps=[rs_a[-1]])
    small_zone = lax.dynamic_update_index_in_dim(lax.empty((N_DEV,) + sm_in.shape, F32), sm_in, me, 0)
    small_sems, small_src, small_zone, small_token = _xchg_start("gather_small", _plan_gather8, N_DEV - 1, [sm_in], [small_zone])
    rs_a = _rs_share(rs_a, [grad_x, small_token])

    small, per_dev = _sum_small(_xchg_wait("gather_small_wait", _plan_gather8, small_sems, small_src, small_zone, [rs_a[-1]])[1][0])
    dmod_all = per_dev[:, 0:6, :].reshape(N_DEV, 6 * D)
    res["w_ada"] = tuple(a[None] for a in _adamw_ada(w_ada[0], c_act, lax.dynamic_slice_in_dim(dmod_all, chip * n_ada, n_ada, axis=1),
                                                     m_w_ada[0], v_w_ada[0]))
    small_sets = {
        "b_ada": (b_ada, small[G_SH1:G_SH1 + 6], m_b_ada, v_b_ada),
        "g_norm_mix": (g_norm_mix, small[G_GMIX:G_GMIX + 1], m_g_norm_mix, v_g_norm_mix),
        "conv_a_w": (conv_a_w[0], shard_cols(small[G_WA0:G_WA0 + 3]), m_conv_a_w[0], v_conv_a_w[0]),
        "conv_b_w": (conv_b_w[0], shard_cols(small[G_WB0:G_WB0 + 4]), m_conv_b_w[0], v_conv_b_w[0]),
        "conv_b_bias": (conv_b_bias, small[G_CBB:G_CBB + 1], m_conv_b_bias, v_conv_b_bias),
        "b_rg_a": (b_rg_a, small[G_BA:G_BA + 1], m_b_rg_a, v_b_rg_a),
        "b_rg_x": (b_rg_x, small[G_BX:G_BX + 1], m_b_rg_x, v_b_rg_x),
        "lru_lambda": (lru_lambda, small[G_LAM:G_LAM + 1], m_lru_lambda, v_lru_lambda),
        "g_norm_ffn": (g_norm_ffn, small[G_GFFN:G_GFFN + 1], m_g_norm_ffn, v_g_norm_ffn),
        "g_norm_final": (g_norm_final.reshape(1, D), small[G_GFIN:G_GFIN + 1], m_g_norm_final.reshape(1, D),
                         v_g_norm_final.reshape(1, D)),
    }
    stepped = _adamw_small(list(small_sets.values()), "adamw_small")
    for (n, (w_, _, _, _)), out in zip(small_sets.items(), stepped):
        shape = (1,) + w_.shape if n.startswith("conv_") and n != "conv_b_bias" else w_.shape
        res[n] = tuple(a.reshape(shape) for a in out)
    gw_in, gw_rga, gw_rgx, gw_out = _rs_end(rs_a, [res[n][1] for n in res])
    step_halves("adamw_mix", {"w_in": (w_in, gw_in, m_w_in, v_w_in), "w_rg_a": (w_rg_a, gw_rga, m_w_rg_a, v_w_rg_a),
                              "w_rg_x": (w_rg_x, gw_rgx, m_w_rg_x, v_w_rg_x), "w_out": (w_out, gw_out, m_w_out, v_w_out)})
    res["g_norm_final"] = tuple(a.reshape(D) for a in res["g_norm_final"])
    names = ["w_ada", "b_ada", "g_norm_mix", "w_in", "conv_a_w", "conv_b_w", "conv_b_bias", "w_rg_a", "b_rg_a", "w_rg_x",
             "b_rg_x", "lru_lambda", "w_out", "g_norm_ffn", "w_gate_up", "w_down", "g_norm_final"]
    loss = jnp.sum(small[G_LOSS])
    return (loss, from_blocks(grad_x)[None], *[res[n][0] for n in names], *[res[n][1] for n in names],
            *[res[n][2] for n in names], *[res[n][3] for n in names])
```

```python
import jax
import jax.numpy as jnp
from jax import lax
from jax.experimental import pallas as pl
from jax.experimental.pallas import tpu as pltpu

F32 = jnp.float32
BF16 = jnp.bfloat16
MESH = pl.DeviceIdType.MESH

D = 1024
N_CHIPS = 4
N_DEV = 8
D_IN = 7 * D
C_IN = D_IN // N_CHIPS
D_FF = 2816
C_GU = 2 * D_FF // N_CHIPS
HEADS = 4
HB = D // HEADS
EPS = 1e-6
LRU_C = 8.0
ADAM_LR, ADAM_B1, ADAM_B2, ADAM_EPS, ADAM_WD, ADAM_STEP = 0.001, 0.9, 0.999, 1e-08, 0.01, 10
VMEM_LIMIT = 56 << 20

(V_SH1, V_SC1, V_GT1, V_SH2, V_SC2, V_GT2, V_GMIX, V_GFFN, V_GFIN, V_CBB, V_BA, V_BX, V_LAM,
 V_WA0, V_WA1, V_WA2, V_WB0, V_WB1, V_WB2, V_WB3) = range(20)
N_VEC = 24
(G_SH1, G_SC1, G_GT1, G_SH2, G_SC2, G_GT2, G_GMIX, G_CBB, G_BA, G_BX, G_LAM, G_GFFN, G_GFIN,
 G_WA0, G_WA1, G_WA2, G_WB0, G_WB1, G_WB2, G_WB3, G_LOSS) = range(21)
N_SMALL = 24

_VMEM = pl.BlockSpec(memory_space=pltpu.VMEM)
_ANY = pl.BlockSpec(memory_space=pl.ANY)


def _cparams(n_grid=1):
    return pltpu.CompilerParams(dimension_semantics=("arbitrary",) * n_grid, vmem_limit_bytes=VMEM_LIMIT)


def _after(deps, body):
    n = len(deps)
    return lambda *refs: body(*refs[n:])


def _rms(x):
    rstd = lax.rsqrt(jnp.mean(x * x, axis=-1, keepdims=True) + EPS)
    return x * rstd, rstd


def _rms_bwd(dxhat, xhat, rstd):
    return rstd * (dxhat - xhat * jnp.mean(dxhat * xhat, axis=-1, keepdims=True))


def _rowsum(v):
    return jnp.sum(v, axis=0, keepdims=True)


def _dot(a, b):
    return jnp.dot(a, b, preferred_element_type=F32)


def _dot_nt(a, b):
    return lax.dot_general(a, b, (((1,), (1,)), ((), ())), preferred_element_type=F32)


def _dot_tn(a, b):
    return lax.dot_general(a, b, (((0,), (0,)), ((), ())), preferred_element_type=F32)


def _gelu(x):
    k, c = 0.7978845608028654, 0.044715
    t = jnp.tanh(k * (x + c * x * x * x))
    return 0.5 * x * (1.0 + t), 0.5 * (1.0 + t) + 0.5 * x * (1.0 - t * t) * k * (1.0 + 3.0 * c * x * x)


def _log_sigmoid(lam):
    return jnp.minimum(lam, 0.0) - jnp.log1p(jnp.exp(-jnp.abs(lam)))


def _lru_gates(u, wa_ref, wx_ref, v_ref, row0):
    ub = u.astype(BF16)
    pre_a = jnp.concatenate([_dot(ub[:, h * HB:(h + 1) * HB], wa_ref[h]) for h in range(HEADS)], axis=1)
    pre_x = jnp.concatenate([_dot(ub[:, h * HB:(h + 1) * HB], wx_ref[h]) for h in range(HEADS)], axis=1)
    r = jax.nn.sigmoid(pre_a + v_ref[V_BA:V_BA + 1, :])
    ig = jax.nn.sigmoid(pre_x + v_ref[V_BX:V_BX + 1, :])
    log_a = LRU_C * r * _log_sigmoid(v_ref[V_LAM:V_LAM + 1, :])
    a = jnp.exp(log_a)
    x2 = 2.0 * log_a
    m2 = jnp.where(x2 > -0.03, -x2 * (1.0 + x2 * (0.5 + x2 * (1.0 / 6.0 + x2 * (1.0 / 24.0)))), 1.0 - a * a)
    mult = jnp.where(row0, 1.0, jnp.sqrt(jnp.maximum(m2, 0.0)))
    return r, ig, a, mult


TIME_BLOCKS = 8
N_KEPT = 10


def _late_blocks(v, buf, g, halo=None):
    n = buf.shape[0]
    out = []
    for idx in range(n):
        k = TIME_BLOCKS - n + idx
        buf[idx, 8:g + 8, :] = v[k * g:(k + 1) * g]
        if halo is not None:
            buf[idx, 7:8, :] = halo[idx]
        out.append(buf[idx, pl.ds(7, g), :])
        if halo is None:
            buf[idx, 7:8, :] = buf[idx, g + 7:g + 8, :]
    return out


def _earlier(v, s, late, g):
    return jnp.concatenate(late[len(late) - s:] + [v[0:(TIME_BLOCKS - s) * g]], axis=0)


def _early_blocks(v, buf, g):
    out = []
    for k in range(buf.shape[0]):
        buf[k, 0:g, :] = v[k * g:(k + 1) * g]
        out.append(buf[k, pl.ds(1, g), :])
        buf[k, g:g + 1, :] = buf[k, 0:1, :]
    return out


def _later(v, s, early, g):
    return jnp.concatenate([v[s * g:]] + early[0:s], axis=0)


def _fwd_in_first(x, vecs, w_in_g, q_idx, ts, deps=()):
    s = x.shape[0]

    def body(q_ref, x_ref, v_ref, w_ref, h1_ref, proj_ref):
        xhat, _ = _rms(x_ref[...])
        h = xhat * v_ref[V_GMIX:V_GMIX + 1, :] * (1.0 + v_ref[V_SC1:V_SC1 + 1, :]) + v_ref[V_SH1:V_SH1 + 1, :]
        hb = h.astype(BF16)
        h1_ref[...] = hb
        proj_ref[...] = _dot(hb, w_ref[...]).astype(BF16)

    return pl.pallas_call(
        lambda q_ref, *refs: body(q_ref, *refs[len(deps):]),
        grid_spec=pltpu.PrefetchScalarGridSpec(
            num_scalar_prefetch=1, grid=(s // ts,),
            in_specs=[_ANY] * len(deps) + [pl.BlockSpec((ts, D), lambda i, q: (i, 0)), _VMEM,
                                           pl.BlockSpec((None, D, C_IN), lambda i, q: (q[0], 0, 0))],
            out_specs=[pl.BlockSpec((ts, D), lambda i, q: (i, 0)), pl.BlockSpec((ts, C_IN), lambda i, q: (i, q[0]))]),
        out_shape=(jax.ShapeDtypeStruct((s, D), BF16), jax.ShapeDtypeStruct((s, D_IN), BF16)),
        compiler_params=_cparams(), name="fwd_in_own")(q_idx, *deps, x, vecs, w_in_g)


def _fwd_in_more(h1, w_in_g, proj, q_idx, ts, name, deps=()):
    s = h1.shape[0]

    def body(q_ref, h1_ref, w_ref, proj_in_ref, proj_ref):
        proj_ref[...] = _dot(h1_ref[...], w_ref[...]).astype(BF16)

    return pl.pallas_call(
        lambda q_ref, *refs: body(q_ref, *refs[len(deps):]),
        grid_spec=pltpu.PrefetchScalarGridSpec(
            num_scalar_prefetch=1, grid=(s // ts,),
            in_specs=[_ANY] * len(deps) + [pl.BlockSpec((ts, D), lambda i, q: (i, 0)),
                                           pl.BlockSpec((None, D, C_IN), lambda i, q: (q[0], 0, 0)), _ANY],
            out_specs=pl.BlockSpec((ts, C_IN), lambda i, q: (i, q[0]))),
        out_shape=jax.ShapeDtypeStruct((s, D_IN), BF16), input_output_aliases={len(deps) + 3: 0},
        compiler_params=_cparams(), name=name)(q_idx, *deps, h1, w_in_g, proj)


def _fwd_mix(proj, x, vecs, w_rga, w_rgx, w_out, ts, deps=()):
    s = x.shape[0]
    g = ts // TIME_BLOCKS

    def body(proj_ref, x_ref, v_ref, wa_ref, wx_ref, wo_ref, x1_ref, mg_ref, z1_ref, kept_ref, decay_ref,
             ua_buf, rx_buf, p_buf, q_buf, c_buf, hcarry):
        i = pl.program_id(0)

        @pl.when(i == 0)
        def _():
            ua_buf[...] = jnp.zeros(ua_buf.shape, F32)
            rx_buf[...] = jnp.zeros(rx_buf.shape, F32)
            hcarry[...] = jnp.zeros((8, D), F32)

        def seg(j):
            return proj_ref[:, j * D:(j + 1) * D].astype(F32)

        def vrow(j):
            return v_ref[j:j + 1, :]

        cb, cc, cx, rx, rg, ga, gb = (seg(j) for j in range(7))
        ua = cc * cx
        ua_late = _late_blocks(ua, ua_buf, g)
        rx_late = _late_blocks(rx, rx_buf, g)
        va = vrow(V_WA2) * ua + vrow(V_WA1) * _earlier(ua, 1, ua_late, g) + vrow(V_WA0) * _earlier(ua, 2, ua_late, g)
        u = (vrow(V_WB3) * rx + vrow(V_WB2) * _earlier(rx, 1, rx_late, g) + vrow(V_WB1) * _earlier(rx, 2, rx_late, g)
             + vrow(V_WB0) * _earlier(rx, 3, rx_late, g) + vrow(V_CBB))

        rows = lax.broadcasted_iota(jnp.int32, (ts, D), 0)
        row0 = jnp.logical_and(rows == 0, i == 0)
        r, ig, a, mult = _lru_gates(u, wa_ref, wx_ref, v_ref, row0)
        decay_ref[...] = a
        bx = mult * (ig * u)

        prods, sums = [a[0:g]], [bx[0:g]]
        for k in range(1, TIME_BLOCKS):
            ak = a[k * g:(k + 1) * g]
            sums.append(ak * sums[-1] + bx[k * g:(k + 1) * g])
            prods.append(ak * prods[-1])
        p_buf[...] = prods[-1]
        q_buf[...] = sums[-1]
        state = hcarry[0:1, :]
        for j in range(g):
            c_buf[j:j + 1, :] = state
            state = p_buf[j:j + 1, :] * state + q_buf[j:j + 1, :]
        hcarry[0:1, :] = state
        entering = c_buf[...]
        h = jnp.concatenate([sums[k] + prods[k] * entering for k in range(TIME_BLOCKS)], axis=0)

        gel, dgel = _gelu(rg)
        sga = jax.nn.sigmoid(ga)
        sgb = jax.nn.sigmoid(gb)
        for j, keep in enumerate((va, r, ig, sga, sgb, gel, dgel, mult, u, h)):
            kept_ref[:, j * D:(j + 1) * D] = keep.astype(BF16)
        merged = (sga * (cb * va) + sgb * (h * gel)).astype(BF16)
        mg_ref[...] = merged
        z1 = _dot(merged, wo_ref[...])
        z1_ref[...] = z1.astype(BF16)
        x1_ref[...] = x_ref[...] + vrow(V_GT1) * z1

    row = lambda i: (i, 0)
    return pl.pallas_call(
        _after(deps, body), grid=(s // ts,),
        out_shape=(jax.ShapeDtypeStruct((s, D), F32), jax.ShapeDtypeStruct((s, D), BF16), jax.ShapeDtypeStruct((s, D), BF16),
                   jax.ShapeDtypeStruct((s, N_KEPT * D), BF16), jax.ShapeDtypeStruct((s, D), F32)),
        in_specs=[_ANY] * len(deps) + [pl.BlockSpec((ts, D_IN), row), pl.BlockSpec((ts, D), row), _VMEM, _VMEM, _VMEM, _VMEM],
        out_specs=[pl.BlockSpec((ts, D), row)] * 3 + [pl.BlockSpec((ts, N_KEPT * D), row), pl.BlockSpec((ts, D), row)],
        scratch_shapes=[pltpu.VMEM((2, g + 8, D), F32), pltpu.VMEM((3, g + 8, D), F32), pltpu.VMEM((g, D), F32),
                        pltpu.VMEM((g, D), F32), pltpu.VMEM((g, D), F32), pltpu.VMEM((8, D), F32)],
        compiler_params=_cparams(), name="fwd_mix")(*deps, proj, x, vecs, w_rga, w_rgx, w_out)


def _ffn_loss(x1, target, vecs, w_gu_g, w_dn, ts):
    s = x1.shape[0]

    def body(x1_ref, t_ref, v_ref, wgu_ref, wdn_ref, dx1_ref, h2_ref, act_ref, dz2_ref, dgu_ref, sm_ref):
        @pl.when(pl.program_id(0) == 0)
        def _():
            sm_ref[...] = jnp.zeros((N_SMALL, D), F32)

        def vrow(j):
            return v_ref[j:j + 1, :]

        n_sub = 1
        rows = [slice(k * (ts // n_sub), (k + 1) * (ts // n_sub)) for k in range(n_sub)]
        subs = [dict(r=r, sums={}) for r in rows]

        def stage_norm(t):
            t["x1"] = x1_ref[t["r"], :]
            t["xh1"], t["rstd1"] = _rms(t["x1"])
            t["n2"] = t["xh1"] * vrow(V_GFFN)
            t["h2"] = (t["n2"] * (1.0 + vrow(V_SC2)) + vrow(V_SH2)).astype(BF16)
            h2_ref[t["r"], :] = t["h2"]

        def stage_up(t):
            h2 = t["h2"]
            g = jnp.concatenate([_dot(h2, wgu_ref[0]), _dot(h2, wgu_ref[1])], axis=1)
            t["up"] = jnp.concatenate([_dot(h2, wgu_ref[2]), _dot(h2, wgu_ref[3])], axis=1)
            t["g"] = g
            t["sg"] = jax.nn.sigmoid(g)
            t["silu"] = g * t["sg"]
            t["act"] = (t["silu"] * t["up"]).astype(BF16)
            act_ref[t["r"], :] = t["act"]

        def stage_down_loss(t):
            z2 = _dot(t["act"], wdn_ref[...])
            x2 = t["x1"] + vrow(V_GT2) * z2
            xh2, rstd2 = _rms(x2)
            err = xh2 * vrow(V_GFIN) - t_ref[t["r"], :]
            t["sums"][G_LOSS] = _rowsum((0.5 / D) * err * err)
            dy = err * (1.0 / D)
            t["sums"][G_GFIN] = _rowsum(dy * xh2)
            t["dx2"] = _rms_bwd(dy * vrow(V_GFIN), xh2, rstd2)
            t["sums"][G_GT2] = _rowsum(t["dx2"] * z2)
            t["dz2"] = (vrow(V_GT2) * t["dx2"]).astype(BF16)
            dz2_ref[t["r"], :] = t["dz2"]

        def stage_back_act(t):
            dact = _dot_nt(t["dz2"], wdn_ref[...])
            g, sg = t["g"], t["sg"]
            t["dgate"] = (dact * t["up"] * (sg * (1.0 + g * (1.0 - sg)))).astype(BF16)
            t["dup"] = (dact * t["silu"]).astype(BF16)
            dgu_ref[t["r"], 0:D_FF] = t["dgate"]
            dgu_ref[t["r"], D_FF:2 * D_FF] = t["dup"]

        def stage_back_norm(t):
            dgate, dup = t["dgate"], t["dup"]
            dh2 = (_dot_nt(dgate[:, 0:C_GU], wgu_ref[0]) + _dot_nt(dgate[:, C_GU:2 * C_GU], wgu_ref[1])
                   + _dot_nt(dup[:, 0:C_GU], wgu_ref[2]) + _dot_nt(dup[:, C_GU:2 * C_GU], wgu_ref[3]))
            t["sums"][G_SH2] = _rowsum(dh2)
            t["sums"][G_SC2] = _rowsum(dh2 * t["n2"])
            dn2 = dh2 * (1.0 + vrow(V_SC2))
            t["sums"][G_GFFN] = _rowsum(dn2 * t["xh1"])
            dx1_ref[t["r"], :] = t["dx2"] + _rms_bwd(dn2 * vrow(V_GFFN), t["xh1"], t["rstd1"])

        for stage in (stage_norm, stage_up, stage_down_loss, stage_back_act, stage_back_norm):
            for t in subs:
                stage(t)
        for j in subs[0]["sums"]:
            total = subs[0]["sums"][j]
            for t in subs[1:]:
                total = total + t["sums"][j]
            sm_ref[j:j + 1, :] += total

    row = lambda i: (i, 0)
    return pl.pallas_call(
        body, grid=(s // ts,),
        out_shape=(jax.ShapeDtypeStruct((s, D), F32), jax.ShapeDtypeStruct((s, D), BF16), jax.ShapeDtypeStruct((s, D_FF), BF16),
                   jax.ShapeDtypeStruct((s, D), BF16), jax.ShapeDtypeStruct((s, 2 * D_FF), BF16),
                   jax.ShapeDtypeStruct((N_SMALL, D), F32)),
        in_specs=[pl.BlockSpec((ts, D), row), pl.BlockSpec((ts, D), row), _VMEM, _VMEM, _VMEM],
        out_specs=[pl.BlockSpec((ts, D), row), pl.BlockSpec((ts, D), row), pl.BlockSpec((ts, D_FF), row),
                   pl.BlockSpec((ts, D), row), pl.BlockSpec((ts, 2 * D_FF), row), pl.BlockSpec((N_SMALL, D), lambda i: (0, 0))],
        compiler_params=_cparams(), name="ffn_loss")(x1, target, vecs, w_gu_g, w_dn)


def _bwd_mix(dx1, z1, merged, proj, kept, decay, vecs, w_rga, w_rgx, w_out, small, ts, deps=()):
    s = dx1.shape[0]
    nt = s // ts
    g = ts // TIME_BLOCKS
    assert g % 16 == 0

    def body(dx1_ref, z1_ref, mg_ref, proj_ref, kept_ref, decay_ref, hh_ref, v_ref, wa_ref, wx_ref,
             wo_ref, sm0_ref, dproj_ref, sm_ref, dwa_ref, dwx_ref, dwo_ref,
             h_buf, a_buf, dva_buf, du_buf, p_buf, q_buf, c_buf, lcarry):
        i = pl.program_id(0)
        first_tile = i == nt - 1

        @pl.when(i == 0)
        def _():
            a_buf[...] = jnp.zeros(a_buf.shape, F32)
            dva_buf[...] = jnp.zeros(dva_buf.shape, F32)
            du_buf[...] = jnp.zeros(du_buf.shape, F32)
            lcarry[...] = jnp.zeros((8, D), F32)
            sm_ref[...] = sm0_ref[...]
            dwa_ref[...] = jnp.zeros((HEADS, HB, HB), F32)
            dwx_ref[...] = jnp.zeros((HEADS, HB, HB), F32)
            dwo_ref[...] = jnp.zeros((D, D), F32)

        def seg(j):
            return proj_ref[:, j * D:(j + 1) * D].astype(F32)

        def vrow(j):
            return v_ref[j:j + 1, :]

        ones = jnp.ones((8, ts), BF16)

        def acc(j, val):
            sm_ref[j:j + 1, :] += 0.125 * _rowsum(_dot(ones, val.astype(BF16)))

        cb, cc, cx, rx = (seg(j) for j in range(4))
        ua = cc * cx
        va, r, ig, sga, sgb, gel, dgel, mult, u, h = (kept_ref[:, j * D:(j + 1) * D].astype(F32) for j in range(N_KEPT))
        a = decay_ref[...]
        rows = lax.broadcasted_iota(jnp.int32, (ts, D), 0)
        row0 = jnp.logical_and(rows == 0, first_tile)

        dx1 = dx1_ref[...]
        acc(G_GT1, dx1 * z1_ref[...].astype(F32))
        dz1 = (vrow(V_GT1) * dx1).astype(BF16)
        dwo_ref[...] += _dot_tn(mg_ref[...], dz1)
        dmg = _dot_nt(dz1, wo_ref[...])
        dya = dmg * sga
        dyb = dmg * sgb
        dproj_ref[:, 5 * D:6 * D] = (dya * (cb * va) * (1.0 - sga)).astype(BF16)
        dproj_ref[:, 6 * D:7 * D] = (dyb * (h * gel) * (1.0 - sgb)).astype(BF16)

        dproj_ref[:, 0:D] = (dya * va).astype(BF16)
        dva = dya * cb
        dva_early = _early_blocks(dva, dva_buf, g)
        dva1 = _later(dva, 1, dva_early, g)
        dva2 = _later(dva, 2, dva_early, g)
        dua = vrow(V_WA2) * dva + vrow(V_WA1) * dva1 + vrow(V_WA0) * dva2
        acc(G_WA2, ua * dva)
        acc(G_WA1, ua * dva1)
        acc(G_WA0, ua * dva2)
        dproj_ref[:, D:2 * D] = (dua * cx).astype(BF16)
        dproj_ref[:, 2 * D:3 * D] = (dua * cc).astype(BF16)

        dproj_ref[:, 4 * D:5 * D] = (dyb * h * dgel).astype(BF16)
        a_next = _later(a, 1, _early_blocks(a, a_buf, g), g)
        dh = dyb * gel
        last = TIME_BLOCKS - 1
        prods, sums = {last: a_next[last * g:]}, {last: dh[last * g:]}
        for k in range(last - 1, -1, -1):
            ak = a_next[k * g:(k + 1) * g]
            sums[k] = dh[k * g:(k + 1) * g] + ak * sums[k + 1]
            prods[k] = ak * prods[k + 1]
        p_buf[...] = prods[0]
        q_buf[...] = sums[0]
        state = lcarry[0:1, :]
        for j in range(g - 1, -1, -1):
            c_buf[j:j + 1, :] = state
            state = q_buf[j:j + 1, :] + p_buf[j:j + 1, :] * state
        lcarry[0:1, :] = state
        entering = c_buf[...]
        lam = jnp.concatenate([sums[k] + prods[k] * entering for k in range(TIME_BLOCKS)], axis=0)

        last = lax.broadcasted_iota(jnp.int32, hh_ref.shape, 0) == hh_ref.shape[0] - 1
        h_halo = [jnp.where(first_tile, 0.0, jnp.sum(jnp.where(last, hh_ref[...].astype(F32), 0.0), axis=0, keepdims=True))]
        da = lam * _earlier(h, 1, _late_blocks(h, h_buf, g, h_halo), g)
        dmult = jnp.where(row0, 0.0, lam * (ig * u))
        di = lam * mult * u
        du = lam * mult * ig
        dlog_a = da * a - dmult * (a * a) / mult
        lam_p = vrow(V_LAM)
        dr = dlog_a * (LRU_C * _log_sigmoid(lam_p))
        sm_ref[G_LAM:G_LAM + 1, :] += _rowsum(dlog_a * r) * (LRU_C * jax.nn.sigmoid(-lam_p))
        dpa = dr * r * (1.0 - r)
        dpx = di * ig * (1.0 - ig)
        acc(G_BA, dpa)
        acc(G_BX, dpx)
        dpab = dpa.astype(BF16)
        dpxb = dpx.astype(BF16)
        ub = u.astype(BF16)
        back = []
        for hd in range(HEADS):
            cols = slice(hd * HB, (hd + 1) * HB)
            back.append(_dot_nt(dpab[:, cols], wa_ref[hd]) + _dot_nt(dpxb[:, cols], wx_ref[hd]))
            dwa_ref[hd] += _dot_tn(ub[:, cols], dpab[:, cols])
            dwx_ref[hd] += _dot_tn(ub[:, cols], dpxb[:, cols])
        du = du + jnp.concatenate(back, axis=1)

        acc(G_CBB, du)
        du_early = _early_blocks(du, du_buf, g)
        du1 = _later(du, 1, du_early, g)
        du2 = _later(du, 2, du_early, g)
        du3 = _later(du, 3, du_early, g)
        dproj_ref[:, 3 * D:4 * D] = (vrow(V_WB3) * du + vrow(V_WB2) * du1 + vrow(V_WB1) * du2 + vrow(V_WB0) * du3).astype(BF16)
        acc(G_WB3, rx * du)
        acc(G_WB2, rx * du1)
        acc(G_WB1, rx * du2)
        acc(G_WB0, rx * du3)

    rev = lambda i: (nt - 1 - i, 0)
    h_halo16 = lambda i: (jnp.maximum((nt - 1 - i) * (ts // 16) - 1, 0), N_KEPT - 1)
    const2 = lambda i: (0, 0)
    const3 = lambda i: (0, 0, 0)
    return pl.pallas_call(
        _after(deps, body), grid=(nt,),
        out_shape=(jax.ShapeDtypeStruct((s, D_IN), BF16), jax.ShapeDtypeStruct((N_SMALL, D), F32),
                   jax.ShapeDtypeStruct((HEADS, HB, HB), F32), jax.ShapeDtypeStruct((HEADS, HB, HB), F32),
                   jax.ShapeDtypeStruct((D, D), F32)),
        in_specs=[_ANY] * len(deps) + [pl.BlockSpec((ts, D), rev), pl.BlockSpec((ts, D), rev), pl.BlockSpec((ts, D), rev),
                  pl.BlockSpec((ts, 4 * D), rev), pl.BlockSpec((ts, N_KEPT * D), rev), pl.BlockSpec((ts, D), rev),
                  pl.BlockSpec((16, D), h_halo16), _VMEM, _VMEM, _VMEM, _VMEM, _VMEM],
        out_specs=[pl.BlockSpec((ts, D_IN), rev), pl.BlockSpec((N_SMALL, D), const2),
                   pl.BlockSpec((HEADS, HB, HB), const3), pl.BlockSpec((HEADS, HB, HB), const3), pl.BlockSpec((D, D), const2)],
        scratch_shapes=[pltpu.VMEM((1, g + 8, D), F32), pltpu.VMEM((1, g + 8, D), F32),
                        pltpu.VMEM((2, g + 8, D), F32), pltpu.VMEM((3, g + 8, D), F32), pltpu.VMEM((g, D), F32),
                        pltpu.VMEM((g, D), F32), pltpu.VMEM((g, D), F32), pltpu.VMEM((8, D), F32)],
        compiler_params=_cparams(), name="bwd_mix")(*deps, dx1, z1, merged, proj, kept, decay, kept, vecs, w_rga,
                                                    w_rgx, w_out, small)


def _bwd_in(dproj, x, dx1, vecs, w_in_g, small, ts, deps=()):
    s = x.shape[0]

    def body(dp_ref, x_ref, dx1_ref, v_ref, w_ref, sm0_ref, gx_ref, sm_ref):
        @pl.when(pl.program_id(0) == 0)
        def _():
            sm_ref[...] = sm0_ref[...]

        def vrow(j):
            return v_ref[j:j + 1, :]

        dh1 = _dot_nt(dp_ref[:, 0:C_IN], w_ref[0])
        for k in range(1, N_CHIPS):
            dh1 += _dot_nt(dp_ref[:, k * C_IN:(k + 1) * C_IN], w_ref[k])
        xh, rstd = _rms(x_ref[...])
        sm_ref[G_SH1:G_SH1 + 1, :] += _rowsum(dh1)
        sm_ref[G_SC1:G_SC1 + 1, :] += _rowsum(dh1 * (xh * vrow(V_GMIX)))
        dn1 = dh1 * (1.0 + vrow(V_SC1))
        sm_ref[G_GMIX:G_GMIX + 1, :] += _rowsum(dn1 * xh)
        gx_ref[...] = dx1_ref[...] + _rms_bwd(dn1 * vrow(V_GMIX), xh, rstd)

    row = lambda i: (i, 0)
    return pl.pallas_call(
        _after(deps, body), grid=(s // ts,),
        out_shape=(jax.ShapeDtypeStruct((s, D), F32), jax.ShapeDtypeStruct((N_SMALL, D), F32)),
        in_specs=[_ANY] * len(deps) + [pl.BlockSpec((ts, D_IN), row), pl.BlockSpec((ts, D), row), pl.BlockSpec((ts, D), row),
                                       _VMEM, _VMEM, _VMEM],
        out_specs=[pl.BlockSpec((ts, D), row), pl.BlockSpec((N_SMALL, D), lambda i: (0, 0))],
        compiler_params=_cparams(), name="bwd_in")(*deps, dproj, x, dx1, vecs, w_in_g, small)


def _grad_w(a, b, n_col_blocks, ts, name, deps=()):
    s, m = a.shape
    tn = b.shape[1] // n_col_blocks
    n_steps = s // ts

    def body(a_ref, b_ref, o_ref, acc_ref):
        k = pl.program_id(1)

        @pl.when(k == 0)
        def _():
            acc_ref[...] = jnp.zeros((m, tn), F32)

        acc_ref[...] += _dot_tn(a_ref[...], b_ref[...])

        @pl.when(k == n_steps - 1)
        def _():
            o_ref[...] = acc_ref[...].astype(BF16)

    return pl.pallas_call(
        _after(deps, body), grid=(n_col_blocks, n_steps),
        out_shape=jax.ShapeDtypeStruct((n_col_blocks, m, tn), BF16),
        in_specs=[_ANY] * len(deps) + [pl.BlockSpec((ts, m), lambda n, k: (k, 0)), pl.BlockSpec((ts, tn), lambda n, k: (k, n))],
        out_specs=pl.BlockSpec((None, m, tn), lambda n, k: (n, 0, 0)),
        scratch_shapes=[pltpu.VMEM((m, tn), F32)],
        compiler_params=_cparams(2), name=name)(*deps, a, b)


def _ada_fwd(c_all, w_ada, b_ada):
    n = w_ada.shape[1]

    def body(c_ref, w_ref, b_ref, o_ref, ca_ref):
        c = c_ref[...]
        ca = c * jax.nn.sigmoid(c)
        ca_ref[...] = ca
        o_ref[...] = jnp.dot(ca, w_ref[...], preferred_element_type=F32, precision=lax.Precision.HIGHEST) + b_ref[...]

    return pl.pallas_call(
        body, out_shape=(jax.ShapeDtypeStruct((N_DEV, n), F32), jax.ShapeDtypeStruct((N_DEV, D), F32)),
        in_specs=[_VMEM] * 3, out_specs=[_VMEM] * 2, compiler_params=_cparams(0), name="ada_fwd")(c_all, w_ada, b_ada)


def _sum_small(parts):
    def body(p_ref, o_ref, d_ref):
        tot = p_ref[0]
        for dev in range(1, N_DEV):
            tot = tot + p_ref[dev]
        o_ref[...] = tot
        d_ref[...] = p_ref[:, 0:8, :]

    return pl.pallas_call(
        body, out_shape=(jax.ShapeDtypeStruct((N_SMALL, D), F32), jax.ShapeDtypeStruct((N_DEV, 8, D), F32)),
        in_specs=[_VMEM], out_specs=[_VMEM] * 2, compiler_params=_cparams(0), name="sum_small")(parts)


def _adamw_small(items, name):
    n = len(items)

    def body(*refs):
        ins, outs = refs[:4 * n], refs[4 * n:]
        for k in range(n):
            w_ref, g_ref, m_ref, v_ref = ins[4 * k:4 * k + 4]
            go_ref, d_ref, nm_ref, nv_ref = outs[4 * k:4 * k + 4]
            pieces = g_ref.shape[0] if g_ref.shape != w_ref.shape else 1
            for j in range(pieces):
                at = (slice(None), slice(j * D, (j + 1) * D)) if pieces > 1 else (slice(None), slice(None))
                g_ = g_ref[j:j + 1, :] if pieces > 1 else g_ref[...]
                go_ref[at] = g_
                m_ = ADAM_B1 * m_ref[at] + (1.0 - ADAM_B1) * g_
                v_ = ADAM_B2 * v_ref[at] + (1.0 - ADAM_B2) * (g_ * g_)
                nm_ref[at] = m_
                nv_ref[at] = v_
                m_hat = m_ / (1.0 - ADAM_B1 ** ADAM_STEP)
                v_hat = v_ / (1.0 - ADAM_B2 ** ADAM_STEP)
                d_ref[at] = -ADAM_LR * (m_hat / (jnp.sqrt(v_hat) + ADAM_EPS) + ADAM_WD * w_ref[at])

    out = pl.pallas_call(
        body, out_shape=tuple(jax.ShapeDtypeStruct(it[0].shape, F32) for it in items for _ in range(4)),
        in_specs=[_VMEM] * (4 * n), out_specs=[_VMEM] * (4 * n), name=name)(*[a for it in items for a in it])
    return [tuple(out[4 * k:4 * k + 4]) for k in range(n)]


HALF_STEPS = 4


def _adamw_halves(sets, c_idx, name, deps=()):
    nh = HALF_STEPS
    n = len(sets)

    def body(c_ref, *refs):
        refs = refs[len(deps):]
        ins, outs = refs[:5 * n], refs[5 * n:]
        for k in range(n):
            w_ref, mine_ref, other_ref, m_ref, v_ref = ins[5 * k:5 * k + 5]
            g_ref, d_ref, nm_ref, nv_ref = outs[4 * k:4 * k + 4]
            g_ = jnp.where(pl.program_id(0) // nh == c_ref[0], mine_ref[...], other_ref[...])
            g_ref[...] = g_
            m_ = ADAM_B1 * m_ref[...] + (1.0 - ADAM_B1) * g_
            v_ = ADAM_B2 * v_ref[...] + (1.0 - ADAM_B2) * (g_ * g_)
            nm_ref[...] = m_
            nv_ref[...] = v_
            m_hat = m_ / (1.0 - ADAM_B1 ** ADAM_STEP)
            v_hat = v_ / (1.0 - ADAM_B2 ** ADAM_STEP)
            d_ref[...] = -ADAM_LR * (m_hat / (jnp.sqrt(v_hat) + ADAM_EPS) + ADAM_WD * w_ref[...])

    in_specs, out_specs, out_shape = [], [], []
    for w, mine, _, _, _ in sets:
        r2, cols = mine.shape
        block = (r2 // nh, cols)
        full = pl.BlockSpec(block, lambda i, c: (i, 0))
        in_specs += [full, pl.BlockSpec(block, lambda i, c: (jnp.clip(i - c[0] * nh, 0, nh - 1), 0)),
                     pl.BlockSpec(block, lambda i, c: (jnp.clip(i - (1 - c[0]) * nh, 0, nh - 1), 0)), full, full]
        out_specs += [full] * 4
        out_shape += [jax.ShapeDtypeStruct((2 * r2, cols), F32)] * 4
    out = pl.pallas_call(
        body,
        grid_spec=pltpu.PrefetchScalarGridSpec(num_scalar_prefetch=1, grid=(2 * nh,),
                                               in_specs=[_ANY] * len(deps) + in_specs, out_specs=out_specs),
        out_shape=tuple(out_shape), compiler_params=_cparams(), name=name,
    )(c_idx, *deps, *[a for s in sets for a in s])
    return [tuple(out[4 * k:4 * k + 4]) for k in range(n)]


def _adamw_ada(w, c_act, dmod, m, v):
    rows, n = w.shape
    tr = 128

    def body(c_ref, d_ref, w_ref, m_ref, v_ref, g_ref, dl_ref, nm_ref, nv_ref):
        g_ = lax.dot_general(c_ref[...], d_ref[...], (((0,), (0,)), ((), ())), preferred_element_type=F32,
                             precision=lax.Precision.HIGHEST)
        g_ref[...] = g_
        m_ = ADAM_B1 * m_ref[...] + (1.0 - ADAM_B1) * g_
        v_ = ADAM_B2 * v_ref[...] + (1.0 - ADAM_B2) * (g_ * g_)
        nm_ref[...] = m_
        nv_ref[...] = v_
        m_hat = m_ / (1.0 - ADAM_B1 ** ADAM_STEP)
        v_hat = v_ / (1.0 - ADAM_B2 ** ADAM_STEP)
        dl_ref[...] = -ADAM_LR * (m_hat / (jnp.sqrt(v_hat) + ADAM_EPS) + ADAM_WD * w_ref[...])

    spec = pl.BlockSpec((tr, n), lambda i: (i, 0))
    return pl.pallas_call(
        body, grid=(rows // tr,), out_shape=(jax.ShapeDtypeStruct((rows, n), F32),) * 4,
        in_specs=[pl.BlockSpec((N_DEV, tr), lambda i: (0, i)), _VMEM, spec, spec, spec], out_specs=[spec] * 4,
        compiler_params=_cparams(), name="adamw_w_ada")(c_act, dmod, w, m, v)


def _add_halves(grads, recvs, c_idx, name):
    nw = len(grads)

    def body(c_ref, *refs):
        for g_ref, r_ref, o_ref in zip(refs[:nw], refs[nw:2 * nw], refs[2 * nw:]):
            o_ref[...] = (g_ref[...].astype(F32) + r_ref[...].astype(F32)).astype(BF16)

    mine = [pl.BlockSpec((None, None) + g.shape[2:], lambda k, c: (k, c[0], 0, 0)) for g in grads]
    whole = [pl.BlockSpec((None,) + g.shape[2:], lambda k, c: (k, 0, 0)) for g in grads]
    return pl.pallas_call(
        body,
        grid_spec=pltpu.PrefetchScalarGridSpec(num_scalar_prefetch=1, grid=(N_CHIPS,), in_specs=mine + whole, out_specs=whole),
        out_shape=tuple(jax.ShapeDtypeStruct((N_CHIPS,) + g.shape[2:], BF16) for g in grads),
        compiler_params=_cparams(), name=name)(c_idx, *grads, *recvs)


def _sum_chips(owns, others, chip_idx, name):
    nw = len(owns)
    steps = 4 if all(a.shape[1] % 64 == 0 for a in owns) else 2

    def body(p_ref, *refs):
        for own_ref, got_ref, o_ref in zip(refs[:nw], refs[nw:2 * nw], refs[2 * nw:]):
            o_ref[...] = (((own_ref[...].astype(F32) + got_ref[0].astype(F32)) + got_ref[1].astype(F32))
                          + got_ref[2].astype(F32))

    blocks = [(a.shape[1] // steps, a.shape[2]) for a in owns]
    return pl.pallas_call(
        body,
        grid_spec=pltpu.PrefetchScalarGridSpec(
            num_scalar_prefetch=1, grid=(steps,),
            in_specs=([pl.BlockSpec((None,) + b, lambda i, p: (p[0], i, 0)) for b in blocks]
                      + [pl.BlockSpec((N_CHIPS - 1,) + b, lambda i, p: (0, i, 0)) for b in blocks]),
            out_specs=[pl.BlockSpec(b, lambda i, p: (i, 0)) for b in blocks]),
        out_shape=tuple(jax.ShapeDtypeStruct(a.shape[1:], F32) for a in owns), compiler_params=_cparams(),
        name=name)(chip_idx, *owns, *others)


def _place():
    x, y, c = lax.axis_index("x"), lax.axis_index("y"), lax.axis_index("c")
    return x, y, c, 2 * x + y


def _flip(v, bit):
    return 1 - v if bit else v


def _allgather8(v, name, deps=()):
    r, n = v.shape

    def body(*refs):
        v_ref, out_ref, send_sems, recv_sems, local_sem = refs[len(deps):]
        x, y, c, _ = _place()
        me = 4 * x + 2 * y + c
        mine = pltpu.make_async_copy(v_ref, out_ref.at[me], local_sem)
        mine.start()
        sends = []
        for rel in range(1, N_DEV):
            peer = (_flip(x, rel & 4), _flip(y, rel & 2), _flip(c, rel & 1))
            cp = pltpu.make_async_remote_copy(v_ref, out_ref.at[me], send_sems.at[rel - 1], recv_sems.at[rel - 1],
                                              device_id=peer, device_id_type=MESH)
            cp.start()
            sends.append(cp)
        for rel in range(1, N_DEV):
            peer = (_flip(x, rel & 4), _flip(y, rel & 2), _flip(c, rel & 1))
            peer_idx = 4 * peer[0] + 2 * peer[1] + peer[2]
            pltpu.make_async_remote_copy(v_ref, out_ref.at[peer_idx], send_sems.at[rel - 1], recv_sems.at[rel - 1],
                                         device_id=peer, device_id_type=MESH).wait_recv()
        for cp in sends:
            cp.wait_send()
        mine.wait()

    return pl.pallas_call(
        body, out_shape=jax.ShapeDtypeStruct((N_DEV, r, n), F32), in_specs=[_ANY] * len(deps) + [_VMEM], out_specs=_VMEM,
        scratch_shapes=[pltpu.SemaphoreType.DMA((N_DEV - 1,)), pltpu.SemaphoreType.DMA((N_DEV - 1,)), pltpu.SemaphoreType.DMA(())],
        name=name)(*deps, v)


_HBM = pl.BlockSpec(memory_space=pltpu.HBM)
_SEM = pl.BlockSpec(memory_space=pltpu.SEMAPHORE)
_EFFECT = pltpu.SideEffectType.DATAFLOW_SIDE_EFFECTING


def _xchg_start(name, plan, n_copies, srcs, lands, after=(), sibling_id=None):
    bufs = list(srcs) + list(lands)
    ns, nb = len(srcs), len(srcs) + len(lands)

    def body(*refs):
        send_sems, recv_sems, token = refs[nb + len(after)], refs[nb + len(after) + 1], refs[-1]
        if sibling_id is not None:
            x, y, c, _ = _place()
            barrier = pltpu.get_barrier_semaphore()
            pl.semaphore_signal(barrier, inc=1, device_id=(x, y, 1 - c), device_id_type=MESH)
            pl.semaphore_wait(barrier, 1)
        for i, (src, dst, peer, _) in enumerate(plan(_place(), refs[:ns], refs[ns:nb])):
            pltpu.make_async_remote_copy(src, dst, send_sems.at[i], recv_sems.at[i], device_id=peer, device_id_type=MESH).start()
        token[...] = jnp.zeros_like(token)

    out = pl.pallas_call(
        body, name=name,
        out_shape=(pltpu.SemaphoreType.DMA((n_copies,)), pltpu.SemaphoreType.DMA((n_copies,)),
                   *[pltpu.HBM(a.shape, a.dtype) for a in bufs], jax.ShapeDtypeStruct((8, 128), F32)),
        in_specs=[_HBM] * nb + [_ANY] * len(after), out_specs=(_SEM, _SEM, *[_HBM] * nb, _VMEM),
        input_output_aliases={i: 2 + i for i in range(nb)},
        compiler_params=pltpu.CompilerParams(has_side_effects=_EFFECT, collective_id=sibling_id),
    )(*[pltpu.with_memory_space_constraint(a, pltpu.HBM) for a in bufs], *after)
    return (out[0], out[1]), out[2:2 + ns], out[2 + ns:2 + nb], out[-1]


def _xchg_wait(name, plan, sems, srcs, lands, after, sem_ids=None):
    bufs = list(srcs) + list(lands)
    ns, nb = len(srcs), len(srcs) + len(lands)

    def body(*refs):
        send_sems, recv_sems = refs[nb], refs[nb + 1]
        copies = plan(_place(), refs[:ns], refs[ns:nb])
        ids = range(len(copies)) if sem_ids is None else sem_ids
        for i, (src, _, peer, mine) in zip(ids, copies, strict=True):
            if i is not None:
                cp = pltpu.make_async_remote_copy(src, mine, send_sems.at[i], recv_sems.at[i], device_id=peer,
                                                  device_id_type=MESH)
                cp.wait_send()
                cp.wait_recv()

    out = pl.pallas_call(
        body, name=name, out_shape=tuple(pltpu.HBM(a.shape, a.dtype) for a in bufs),
        in_specs=[_HBM] * nb + [_SEM, _SEM] + [_ANY] * len(after), out_specs=tuple([_HBM] * nb),
        input_output_aliases={i: i for i in range(nb)},
        compiler_params=pltpu.CompilerParams(has_side_effects=_EFFECT),
    )(*bufs, *sems, *after)
    return out[:ns], out[ns:]


def _other_chips(place, which=(1, 2, 3)):
    x, y, c, _ = place
    return [((_flip(x, j & 2), _flip(y, j & 1), c), 2 * _flip(x, j & 2) + _flip(y, j & 1)) for j in which]


def _plan_gather_ici(chips):
    def plan(place, src_refs, land_refs):
        _, _, c, p = place
        return [(s.at[c], l.at[p, c], peer, l.at[q, c]) for s, l, which in zip(src_refs, land_refs, chips, strict=True)
                for peer, q in _other_chips(place, which)]
    return plan


def _plan_relay(which):
    def plan(place, src_refs, land_refs):
        x, y, c, _ = place
        return [(l.at[q, c], l.at[q, c], (x, y, 1 - c), l.at[q, 1 - c]) for l in land_refs for _, q in _other_chips(place, which)]
    return plan


def _plan_swap(place, src_refs, land_refs):
    x, y, c, _ = place
    return [(s.at[k, 1 - c], l.at[k], (x, y, 1 - c), l.at[k]) for s, l in zip(src_refs, land_refs) for k in range(N_CHIPS)]


def _plan_scatter(place, src_refs, land_refs):
    return [(s.at[q], l.at[j], peer, l.at[j]) for s, l in zip(src_refs, land_refs)
            for j, (peer, q) in enumerate(_other_chips(place))]


def _plan_share(place, src_refs, land_refs):
    x, y, c, _ = place
    return [(s, l, (x, y, 1 - c), l) for s, l in zip(src_refs, land_refs)]


def _plan_gather8(place, src_refs, land_refs):
    x, y, c, _ = place
    me = 4 * x + 2 * y + c
    copies = []
    for s, l in zip(src_refs, land_refs):
        for rel in range(1, N_DEV):
            peer = (_flip(x, rel & 4), _flip(y, rel & 2), _flip(c, rel & 1))
            copies.append((s, l.at[me], peer, l.at[4 * peer[0] + 2 * peer[1] + peer[2]]))
    return copies


def _pack_rows(parts, n_rows, name, deps=()):
    def body(*refs):
        refs = refs[len(deps):]
        out_ref = refs[-1]
        out_ref[...] = jnp.zeros((n_rows, D), F32)
        at = 0
        for ref in refs[:-1]:
            k = ref.shape[0]
            out_ref[at:at + k, :] = ref[...]
            at += k

    return pl.pallas_call(
        body, out_shape=jax.ShapeDtypeStruct((n_rows, D), F32), in_specs=[_ANY] * len(deps) + [_VMEM] * len(parts),
        out_specs=_VMEM, name=name)(*deps, *parts)


TS_MM = 512
TS_IN = 1024
TS_GW = 2048
TS_GW_WIDE = 1024
TS_MIX = 256


def _halved(a):
    n, r, cols = a.shape
    return a.reshape(n, 2, r // 2, cols)


SIBLING_IDS = (1, 2)


def _rs_swap(name, grads, after=()):
    lands = [lax.empty((N_CHIPS,) + g.shape[2:], g.dtype) for g in grads]
    sems, grads, lands, token = _xchg_start(name + "_swap", _plan_swap, N_CHIPS * len(grads), grads, lands, after,
                                            sibling_id=SIBLING_IDS[0])
    return name, sems, grads, lands, token


def _rs_scatter(handle, after, chip, ci):
    name, sems, grads, lands, _ = handle
    grads, from_sibling = _xchg_wait(name + "_swap_wait", _plan_swap, sems, grads, lands, after)
    c_arr = jnp.reshape(ci, (1,)).astype(jnp.int32)
    pair_sums = _add_halves(list(grads), list(from_sibling), c_arr, name + "_add_halves")
    lands = [lax.empty((N_CHIPS - 1,) + p.shape[1:], p.dtype) for p in pair_sums]
    sems, pair_sums, lands, token = _xchg_start(name + "_scatter", _plan_scatter, 3 * len(pair_sums), pair_sums, lands)
    return name, sems, pair_sums, lands, jnp.reshape(chip, (1,)).astype(jnp.int32), token


def _rs_share(handle, after):
    name, sems, pair_sums, lands, chip_idx, _ = handle
    pair_sums, received = _xchg_wait(name + "_scatter_wait", _plan_scatter, sems, pair_sums, lands, after)
    halves = _sum_chips(list(pair_sums), list(received), chip_idx, name + "_sum_chips")
    lands = [lax.empty(h.shape, h.dtype) for h in halves]
    sems, halves, lands, token = _xchg_start(name + "_share", _plan_share, len(halves), halves, lands,
                                             sibling_id=SIBLING_IDS[1])
    return name, sems, halves, lands, token


def _rs_end(handle, after):
    name, sems, halves, lands, _ = handle
    halves, others = _xchg_wait(name + "_share_wait", _plan_share, sems, halves, lands, after)
    return list(zip(halves, others))


def kernel(x, c, w_ada, b_ada, g_norm_mix, w_in, conv_a_w, conv_b_w, conv_b_bias, w_rg_a, b_rg_a, w_rg_x, b_rg_x, lru_lambda, w_out, g_norm_ffn, w_gate_up, w_down, g_norm_final, loss_target, m_w_ada, m_b_ada, m_g_norm_mix, m_w_in, m_conv_a_w, m_conv_b_w, m_conv_b_bias, m_w_rg_a, m_b_rg_a, m_w_rg_x, m_b_rg_x, m_lru_lambda, m_w_out, m_g_norm_ffn, m_w_gate_up, m_w_down, m_g_norm_final, v_w_ada, v_b_ada, v_g_norm_mix, v_w_in, v_conv_a_w, v_conv_b_w, v_conv_b_bias, v_w_rg_a, v_b_rg_a, v_w_rg_x, v_b_rg_x, v_lru_lambda, v_w_out, v_g_norm_ffn, v_w_gate_up, v_w_down, v_g_norm_final):
    xi, yi, ci = lax.axis_index("x"), lax.axis_index("y"), lax.axis_index("c")
    chip = 2 * xi + yi
    me = 2 * chip + ci
    n_ada = w_ada.shape[2]

    def widen(w):
        return jnp.pad(w, ((0, 0), (0, D - w.shape[1])))

    got = _allgather8(_pack_rows([c, widen(conv_a_w[0]), widen(conv_b_w[0])], 8, "pack_c_conv"), "gather_c_conv")
    c_all = got[:, 0, :]
    conv_full = got[::2, 1:8, :D // N_CHIPS].transpose(1, 0, 2).reshape(7, D)

    mod_part, c_act = _ada_fwd(c_all, w_ada[0], lax.dynamic_slice_in_dim(b_ada, chip * n_ada, n_ada, axis=1))
    mod_all = _allgather8(mod_part, "gather_mod")
    mod_mine = lax.dynamic_index_in_dim(mod_all, me, axis=1, keepdims=False)[::2].reshape(6, D)
    vecs = _pack_rows([mod_mine, g_norm_mix, g_norm_ffn, g_norm_final.reshape(1, D), conv_b_bias, b_rg_a, b_rg_x, lru_lambda,
                       conv_full], N_VEC, "pack_vecs")

    def rg_shard(w):
        return w[0].astype(BF16).reshape(2, HEADS * HB // N_CHIPS // 2, HB)

    shards = [w_in[0].astype(BF16).reshape(2, D // 2, C_IN), rg_shard(w_rg_a), rg_shard(w_rg_x),
              w_out[0].astype(BF16).reshape(2, D // N_CHIPS // 2, D), w_gate_up[0].astype(BF16).reshape(2, D // 2, C_GU),
              w_down[0].astype(BF16).reshape(2, D_FF // N_CHIPS // 2, D)]
    lands = [lax.dynamic_update_index_in_dim(lax.empty((N_CHIPS,) + s.shape, s.dtype), s, chip, 0) for s in shards]

    def send(name, first, last, after, chips):
        copies = [(k, j) for k, which in zip(range(first, last), chips, strict=True) for j in which]
        sems, srcs, zone, token = _xchg_start(name + "_ici", _plan_gather_ici(chips), len(copies), shards[first:last],
                                              lands[first:last], after)
        shards[first:last], lands[first:last] = srcs, zone
        return sems, copies, token

    def arrive(name, sent, first, last, after):
        sems, copies, _ = sent
        chips = [tuple(j for k, j in copies if k == want) for want in range(first, last)]
        ids = [copies.index((k, j)) for k, which in zip(range(first, last), chips) for j in which]
        srcs, zone = _xchg_wait(name + "_ici_wait", _plan_gather_ici(chips), sems, shards[first:last], lands[first:last], after,
                                ids)
        shards[first:last], lands[first:last] = srcs, zone

    def relay(name, first, last, which, sibling_id):
        plan = _plan_relay(which)
        sems, _, zone, token = _xchg_start(name + "_d2d", plan, len(which) * (last - first), [], lands[first:last],
                                           sibling_id=sibling_id)
        lands[first:last] = zone
        return name, plan, sems, first, last, token

    def relayed(handle, after):
        name, plan, sems, first, last, _ = handle
        lands[first:last] = _xchg_wait(name + "_d2d_wait", plan, sems, [], lands[first:last], after)[1]

    def to_blocks(v):
        return v.reshape(-1, TS_MIX // TIME_BLOCKS, TIME_BLOCKS, D).transpose(0, 2, 1, 3).reshape(v.shape)

    def from_blocks(v):
        return v.reshape(-1, TIME_BLOCKS, TS_MIX // TIME_BLOCKS, D).transpose(0, 2, 1, 3).reshape(v.shape)

    def chip_index(j):
        return jnp.reshape(chip ^ j, (1,)).astype(jnp.int32)

    def wg_in():
        return lands[0].reshape(N_CHIPS, D, C_IN)

    xs, target = to_blocks(x[0]), to_blocks(loss_target[0])
    sent_near = send("gather_in_near", 0, 1, [vecs], [(1, 2)])
    ts_in = min(TS_IN, xs.shape[0])
    h1, proj = _fwd_in_first(xs, vecs, wg_in(), chip_index(0), ts_in, deps=[sent_near[-1]])
    arrive("gather_in_near", sent_near, 0, 1, [proj])
    near = relay("gather_in_near", 0, 1, (1, 2), SIBLING_IDS[0])
    sent_rest = send("gather_rest", 0, 6, [near[-1]], [(3,)] + [(1, 2, 3)] * 5)
    relayed(near, [sent_rest[-1]])
    proj = _fwd_in_more(h1, wg_in(), proj, chip_index(1), ts_in, "fwd_in_y")
    proj = _fwd_in_more(h1, wg_in(), proj, chip_index(2), ts_in, "fwd_in_x")
    arrive("gather_in_far", sent_rest, 0, 1, [proj])
    far = relay("gather_in_far", 0, 1, (3,), SIBLING_IDS[1])
    arrive("gather_mix", sent_rest, 1, 4, [far[-1]])
    relayed(far, [far[-1]])
    mix = relay("gather_mix", 1, 4, (1, 2, 3), SIBLING_IDS[0])
    proj = _fwd_in_more(h1, wg_in(), proj, chip_index(3), ts_in, "fwd_in_xy", deps=[mix[-1]])
    relayed(mix, [proj])
    wg_rga, wg_rgx, wg_out = lands[1:4]
    wg_out = wg_out.reshape(D, D)

    def rg_full(wg):
        return wg.reshape(N_CHIPS, HEADS, HB // N_CHIPS, HB).transpose(1, 0, 2, 3).reshape(HEADS, HB, HB)

    wg_rga, wg_rgx = rg_full(wg_rga), rg_full(wg_rgx)

    x1, merged, z1, kept, decay = _fwd_mix(proj, xs, vecs, wg_rga, wg_rgx, wg_out, TS_MIX)
    arrive("gather_ffn", sent_rest, 4, 6, [x1])
    ffn = relay("gather_ffn", 4, 6, (1, 2, 3), SIBLING_IDS[1])
    relayed(ffn, [ffn[-1]])
    wg_gu, wg_dn = lands[4:6]
    wg_gu, wg_dn = wg_gu.reshape(N_CHIPS, D, C_GU), wg_dn.reshape(D_FF, D)
    dx1, h2, act, dz2, dgu, sm_ffn = _ffn_loss(x1, target, vecs, wg_gu, wg_dn, TS_MIX)

    def rg_chunks(dw):
        return _halved(dw.reshape(HEADS, N_CHIPS, HB // N_CHIPS, HB).transpose(1, 0, 2, 3).reshape(N_CHIPS, HB, HB).astype(BF16))

    ts_gw = min(TS_GW, xs.shape[0])
    g_dn = _grad_w(act, dz2, 1, min(TS_GW_WIDE, xs.shape[0]), "grad_w_down")
    g_gu = _grad_w(h2, dgu, N_CHIPS, ts_gw, "grad_w_gate_up")
    rs_b = _rs_swap("rs_b", [_halved(g_gu), _halved(g_dn.reshape(N_CHIPS, D_FF // N_CHIPS, D))])
    dproj, sm_mix, dw_rga, dw_rgx, dw_out = _bwd_mix(dx1, z1, merged, proj, kept, decay, vecs, wg_rga, wg_rgx, wg_out, sm_ffn, TS_MIX,
                                                     deps=[rs_b[-1]])
    rs_b = _rs_scatter(rs_b, [dproj], chip, ci)
    g_in = _grad_w(h1, dproj, N_CHIPS, ts_gw, "grad_w_in", deps=[rs_b[-1]])
    rs_b = _rs_share(rs_b, [g_in])
    rs_a = _rs_swap("rs_a", [_halved(g_in), rg_chunks(dw_rga), rg_chunks(dw_rgx),
                             _halved(dw_out.astype(BF16).reshape(N_CHIPS, D // N_CHIPS, D))], after=[rs_b[-1]])

    c_arr = jnp.reshape(ci, (1,)).astype(jnp.int32)

    def step_halves(name, items, deps=()):
        two_d = lambda a: a.reshape(-1, a.shape[-1])
        sets = [(two_d(w), halves[0], halves[1], two_d(m), two_d(v)) for w, halves, m, v in items.values()]
        for (n, (w, _, _, _)), out in zip(items.items(), _adamw_halves(sets, c_arr, name, deps)):
            res[n] = tuple(a.reshape(w.shape) for a in out)

    def shard_cols(row_block):
        return lax.dynamic_slice_in_dim(row_block, chip * (D // N_CHIPS), D // N_CHIPS, axis=1)

    gw_gu, gw_dn = _rs_end(rs_b, [rs_a[-1]])
    res = {}
    step_halves("adamw_ffn", {"w_gate_up": (w_gate_up, gw_gu, m_w_gate_up, v_w_gate_up),
                              "w_down": (w_down, gw_dn, m_w_down, v_w_down)}, [rs_a[-1]])
    rs_a = _rs_scatter(rs_a, [res["w_gate_up"][1], res["w_down"][1]], chip, ci)
    grad_x, sm_in = _bwd_in(dproj, xs, dx1, vecs, wg_in(), sm_mix, TS_MM, deps=[rs_a[-1]])
    small_zone = lax.dynamic_update_index_in_dim(lax.empty((N_DEV,) + sm_in.shape, F32), sm_in, me, 0)
    small_sems, small_src, small_zone, small_token = _xchg_start("gather_small", _plan_gather8, N_DEV - 1, [sm_in], [small_zone])
    rs_a = _rs_share(rs_a, [grad_x, small_token])

    small, per_dev = _sum_small(_xchg_wait("gather_small_wait", _plan_gather8, small_sems, small_src, small_zone, [rs_a[-1]])[1][0])
    dmod_all = per_dev[:, 0:6, :].reshape(N_DEV, 6 * D)
    res["w_ada"] = tuple(a[None] for a in _adamw_ada(w_ada[0], c_act, lax.dynamic_slice_in_dim(dmod_all, chip * n_ada, n_ada, axis=1),
                                                     m_w_ada[0], v_w_ada[0]))
    small_sets = {
        "b_ada": (b_ada, small[G_SH1:G_SH1 + 6], m_b_ada, v_b_ada),
        "g_norm_mix": (g_norm_mix, small[G_GMIX:G_GMIX + 1], m_g_norm_mix, v_g_norm_mix),
        "conv_a_w": (conv_a_w[0], shard_cols(small[G_WA0:G_WA0 + 3]), m_conv_a_w[0], v_conv_a_w[0]),
        "conv_b_w": (conv_b_w[0], shard_cols(small[G_WB0:G_WB0 + 4]), m_conv_b_w[0], v_conv_b_w[0]),
        "conv_b_bias": (conv_b_bias, small[G_CBB:G_CBB + 1], m_conv_b_bias, v_conv_b_bias),
        "b_rg_a": (b_rg_a, small[G_BA:G_BA + 1], m_b_rg_a, v_b_rg_a),
        "b_rg_x": (b_rg_x, small[G_BX:G_BX + 1], m_b_rg_x, v_b_rg_x),
        "lru_lambda": (lru_lambda, small[G_LAM:G_LAM + 1], m_lru_lambda, v_lru_lambda),
        "g_norm_ffn": (g_norm_ffn, small[G_GFFN:G_GFFN + 1], m_g_norm_ffn, v_g_norm_ffn),
        "g_norm_final": (g_norm_final.reshape(1, D), small[G_GFIN:G_GFIN + 1], m_g_norm_final.reshape(1, D),
                         v_g_norm_final.reshape(1, D)),
    }
    stepped = _adamw_small(list(small_sets.values()), "adamw_small")
    for (n, (w_, _, _, _)), out in zip(small_sets.items(), stepped):
        shape = (1,) + w_.shape if n.startswith("conv_") and n != "conv_b_bias" else w_.shape
        res[n] = tuple(a.reshape(shape) for a in out)
    gw_in, gw_rga, gw_rgx, gw_out = _rs_end(rs_a, [res[n][1] for n in res])
    step_halves("adamw_mix", {"w_in": (w_in, gw_in, m_w_in, v_w_in), "w_rg_a": (w_rg_a, gw_rga, m_w_rg_a, v_w_rg_a),
                              "w_rg_x": (w_rg_x, gw_rgx, m_w_rg_x, v_w_rg_x), "w_out": (w_out, gw_out, m_w_out, v_w_out)})
    res["g_norm_final"] = tuple(a.reshape(D) for a in res["g_norm_final"])
    names = ["w_ada", "b_ada", "g_norm_mix", "w_in", "conv_a_w", "conv_b_w", "conv_b_bias", "w_rg_a", "b_rg_a", "w_rg_x",
             "b_rg_x", "lru_lambda", "w_out", "g_norm_ffn", "w_gate_up", "w_down", "g_norm_final"]
    loss = jnp.sum(small[G_LOSS])
    return (loss, from_blocks(grad_x)[None], *[res[n][0] for n in names], *[res[n][1] for n in names],
            *[res[n][2] for n in names], *[res[n][3] for n in names])
```
